```python
import math
import jax, jax.numpy as jnp
from jax import lax
import numpy as np

D_MODEL = 1024
BATCH = 16
SEQ = 2048
DEPTH = 1

F32 = jnp.float32
MEM_LEN = 256
NSA_HEADS = 8
NSA_KV_GROUPS = 2
NSA_HPG = NSA_HEADS // NSA_KV_GROUPS
NSA_HEAD_DIM = D_MODEL // 16
NSA_WIDTH = NSA_HEADS * NSA_HEAD_DIM
NSA_KV_WIDTH = NSA_KV_GROUPS * NSA_HEAD_DIM
NSA_CMP_LEN = 32
NSA_CMP_STRIDE = 16
NSA_SEL_BLOCK = 64
NSA_SEL_TOPN = 8
NSA_WINDOW = 512
NSA_QBLOCK = 64
MLSTM_HEADS = 4
MLSTM_HEAD_DIM = D_MODEL // 8
MLSTM_WIDTH = MLSTM_HEADS * MLSTM_HEAD_DIM
MLSTM_CHUNK = 64
MLSTM_CONV = 4
MIX_WIDTH = NSA_WIDTH + MLSTM_WIDTH
IN_SPLIT_SIZES = (NSA_WIDTH,) + (NSA_KV_WIDTH,) * 6 + (3 * NSA_HEADS,) + (MLSTM_WIDTH,) * 4 + (MLSTM_HEADS,) * 2
IN_WIDTH = sum(IN_SPLIT_SIZES)
XATTN_HEADS = 4
XATTN_HEAD_DIM = D_MODEL // XATTN_HEADS
MOE_EXPERTS = 256
MOE_TOPK = 8
MOE_GROUPS = 8
MOE_TOPK_GROUPS = 4
MOE_FF = D_MODEL // 4
MOE_SHARED_FF = MOE_FF
MOE_ROUTE_SCALE = 2.5
MOE_ROW_BLOCK = 128
DEEPNORM_ALPHA = (2.0 * DEPTH) ** 0.25
DEEPNORM_BETA = (8.0 * DEPTH) ** -0.25
LN_EPS = 1e-5
NEG_INF = -1e30
FORCE_BONUS = 1e4

kernel_name = 'hybrid_nsa_mlstm_moe_deepnorm'


def layer_norm(x, g, b):
    xf = x.astype(F32)
    mu = jnp.mean(xf, -1, keepdims=True)
    var = jnp.mean(jnp.square(xf - mu), -1, keepdims=True)
    return ((xf - mu) * lax.rsqrt(var + LN_EPS) * g.astype(F32) + b.astype(F32)).astype(x.dtype)


def masked_softmax(s, mask):
    return jax.nn.softmax(jnp.where(mask, s, NEG_INF), axis=-1)


def nsa_mixer(q, k_cmp, v_cmp, k_sel, v_sel, k_win, v_win, gates,
              pos_k, ck_w1, ck_w2, pos_v, cv_w1, cv_w2):
    B, S = q.shape[:2]
    G, Hg, dk = NSA_KV_GROUPS, NSA_HPG, NSA_HEAD_DIM
    L, STR, BS, W, QB = NSA_CMP_LEN, NSA_CMP_STRIDE, NSA_SEL_BLOCK, NSA_WINDOW, NSA_QBLOCK
    scale = dk ** -0.5
    out_dtype = q.dtype
    q = q.reshape(B, S, G, Hg, dk)
    kv = lambda a: a.reshape(B, S, G, dk)
    k_cmp, v_cmp, k_sel, v_sel, k_win, v_win = map(kv, (k_cmp, v_cmp, k_sel, v_sel, k_win, v_win))
    t = jnp.arange(S)

    n_cmp = (S - L) // STR + 1
    cmp_start = jnp.arange(n_cmp) * STR
    win_idx = cmp_start[:, None] + jnp.arange(L)[None, :]

    def compress(a, pos, w1, w2):
        blocks = a[:, win_idx] + pos[None, None, :, None, :]
        blocks = blocks.transpose(0, 1, 3, 2, 4).reshape(B, n_cmp, G, L * dk)
        return jax.nn.silu(blocks @ w1) @ w2

    kc = compress(k_cmp, pos_k, ck_w1, ck_w2)
    vc = compress(v_cmp, pos_v, cv_w1, cv_w2)
    cmask = (cmp_start + L - 1)[None, :] <= t[:, None]
    s_c = jnp.einsum('bsghd,bcgd->bghsc', q, kc).astype(F32) * scale
    p_c = masked_softmax(s_c, cmask) * cmask
    o_cmp = jnp.einsum('bghsc,bcgd->bsghd', p_c.astype(vc.dtype), vc)

    n_sel = S // BS
    sel_start = jnp.arange(n_sel) * BS
    overlap = ((cmp_start[:, None] < sel_start[None, :] + BS) &
               (cmp_start[:, None] + L > sel_start[None, :])).astype(F32)
    p_slc = jnp.einsum('bghsc,cn->bgsn', p_c, overlap)
    cur = t // BS
    sblk = jnp.arange(n_sel)[None, :]
    valid = sblk <= cur[:, None]
    forced = (sblk == 0) | (sblk == cur[:, None]) | (sblk == cur[:, None] - 1)
    imp = jnp.where(valid, p_slc + jnp.where(forced, FORCE_BONUS, 0.0), NEG_INF)
    top_n = min(NSA_SEL_TOPN, n_sel)
    top_val, top_idx = lax.top_k(imp, top_n)
    top_ok = top_val > 0.5 * NEG_INF

    ksel_blocks = k_sel.reshape(B, n_sel, BS, G, dk).transpose(0, 3, 1, 2, 4)
    vsel_blocks = v_sel.reshape(B, n_sel, BS, G, dk).transpose(0, 3, 1, 2, 4)
    pad = jnp.zeros((B, W, G, dk), k_win.dtype)
    kw_pad = jnp.concatenate([pad, k_win], axis=1)
    vw_pad = jnp.concatenate([pad, v_win], axis=1)
    bi = jnp.arange(B)[:, None, None, None]
    gi = jnp.arange(G)[None, :, None, None]

    def query_block(i):
        qs = i * QB
        qb = lax.dynamic_slice_in_dim(q, qs, QB, axis=1)
        tq = qs + jnp.arange(QB)
        idx = lax.dynamic_slice_in_dim(top_idx, qs, QB, axis=2)
        ok = lax.dynamic_slice_in_dim(top_ok, qs, QB, axis=2)
        ks = ksel_blocks[bi, gi, idx].reshape(B, G, QB, top_n * BS, dk)
        vs = vsel_blocks[bi, gi, idx].reshape(B, G, QB, top_n * BS, dk)
        kpos = (idx[..., None] * BS + jnp.arange(BS)).reshape(B, G, QB, top_n * BS)
        smask = (kpos <= tq[None, None, :, None]) & jnp.repeat(ok, BS, axis=-1)
        s_s = jnp.einsum('bqghd,bgqkd->bghqk', qb, ks).astype(F32) * scale
        p_s = masked_softmax(s_s, smask[:, :, None])
        o_s = jnp.einsum('bghqk,bgqkd->bqghd', p_s.astype(vs.dtype), vs)
        kw = lax.dynamic_slice_in_dim(kw_pad, qs, W + QB, axis=1)
        vw = lax.dynamic_slice_in_dim(vw_pad, qs, W + QB, axis=1)
        wpos = qs - W + jnp.arange(W + QB)
        wmask = (wpos[None, :] <= tq[:, None]) & (wpos[None, :] > tq[:, None] - W) & (wpos[None, :] >= 0)
        s_w = jnp.einsum('bqghd,bkgd->bghqk', qb, kw).astype(F32) * scale
        p_w = masked_softmax(s_w, wmask)
        o_w = jnp.einsum('bghqk,bkgd->bqghd', p_w.astype(vw.dtype), vw)
        return o_s, o_w

    o_sel, o_win = lax.map(query_block, jnp.arange(S // QB))
    unblock = lambda o: jnp.moveaxis(o, 0, 1).reshape(B, S, G, Hg, dk)
    g = jax.nn.sigmoid(gates.astype(F32)).reshape(B, S, G, Hg, 3)
    o = g[..., 0:1] * o_cmp + g[..., 1:2] * unblock(o_sel) + g[..., 2:3] * unblock(o_win)
    return o.reshape(B, S, NSA_WIDTH).astype(out_dtype)


def causal_depthwise_conv(u, w):
    K = w.shape[0]
    return lax.conv_general_dilated(u, w.astype(u.dtype), window_strides=(1,), padding=((K - 1, 0),),
                                    dimension_numbers=('NWC', 'WIO', 'NWC'),
                                    feature_group_count=u.shape[-1])


def mlstm_chunkwise(q, k, v, log_i, log_f):
    B, H, S, dh = q.shape
    L = MLSTM_CHUNK
    nc = S // L
    chunks = lambda a: jnp.moveaxis(a.reshape(B, H, nc, L, *a.shape[3:]), 2, 0)
    causal = jnp.tril(jnp.ones((L, L), bool))

    def step(carry, inp):
        C, n, m = carry
        qc, kc, vc, ic, fc = inp
        b = jnp.cumsum(fc, axis=-1)
        g = b[..., -1]
        d_log = jnp.where(causal, b[..., :, None] - b[..., None, :] + ic[..., None, :], -jnp.inf)
        inter = b + m[..., None]
        m_q = jnp.maximum(inter, jnp.max(d_log, axis=-1))
        w_intra = jnp.exp(d_log - m_q[..., None])
        w_inter = jnp.exp(inter - m_q)
        s = jnp.einsum('bhld,bhmd->bhlm', qc, kc) * w_intra
        num = w_inter[..., None] * jnp.einsum('bhld,bhde->bhle', qc, C) + jnp.einsum('bhlm,bhme->bhle', s, vc)
        den = w_inter * jnp.einsum('bhld,bhd->bhl', qc, n) + jnp.sum(s, axis=-1)
        h = num / jnp.maximum(jnp.abs(den), jnp.exp(-m_q))[..., None]
        log_k = g[..., None] - b + ic
        m_new = jnp.maximum(g + m, jnp.max(log_k, axis=-1))
        wk = jnp.exp(log_k - m_new[..., None])
        decay = jnp.exp(g + m - m_new)
        C_new = decay[..., None, None] * C + jnp.einsum('bhl,bhld,bhle->bhde', wk, kc, vc)
        n_new = decay[..., None] * n + jnp.einsum('bhl,bhld->bhd', wk, kc)
        return (C_new, n_new, m_new), h

    init = (jnp.zeros((B, H, dh, dh), F32), jnp.zeros((B, H, dh), F32), jnp.zeros((B, H), F32))
    _, hs = lax.scan(step, init, (chunks(q), chunks(k), chunks(v), chunks(log_i), chunks(log_f)))
    return jnp.moveaxis(hs, 0, 2).reshape(B, H, S, dh)


def mlstm_mixer(q, k, v, o_pre, i_pre, f_pre, conv_w, i_bias, f_bias, norm_g):
    B, S, _ = q.shape
    H, dh = MLSTM_HEADS, MLSTM_HEAD_DIM
    out_dtype = v.dtype
    qk = jax.nn.silu(causal_depthwise_conv(jnp.concatenate([q, k], axis=-1), conv_w))
    q, k = jnp.split(qk, 2, axis=-1)
    heads = lambda a: a.astype(F32).reshape(B, S, H, dh).transpose(0, 2, 1, 3)
    qh = heads(q) * (dh ** -0.5)
    kh, vh = heads(k), heads(v)
    log_i = (i_pre.astype(F32) + i_bias.astype(F32)).transpose(0, 2, 1)
    log_f = jax.nn.log_sigmoid(f_pre.astype(F32) + f_bias.astype(F32)).transpose(0, 2, 1)
    h = mlstm_chunkwise(qh, kh, vh, log_i, log_f)
    mu = jnp.mean(h, -1, keepdims=True)
    var = jnp.mean(jnp.square(h - mu), -1, keepdims=True)
    hn = ((h - mu) * lax.rsqrt(var + LN_EPS)).transpose(0, 2, 1, 3).reshape(B, S, MLSTM_WIDTH)
    hn = hn * norm_g.astype(F32)
    return (jax.nn.sigmoid(o_pre.astype(F32)) * hn).astype(out_dtype)


def memory_cross_attention(x, mem, wq, wk, wv, wo):
    B, S, _ = x.shape
    M = mem.shape[1]
    q = (x @ wq).reshape(B, S, XATTN_HEADS, XATTN_HEAD_DIM)
    k = (mem @ wk).reshape(B, M, XATTN_HEADS, XATTN_HEAD_DIM)
    v = (mem @ wv).reshape(B, M, XATTN_HEADS, XATTN_HEAD_DIM)
    s = jnp.einsum('bshd,bmhd->bhsm', q, k).astype(F32) * (XATTN_HEAD_DIM ** -0.5)
    p = jax.nn.softmax(s, axis=-1)
    o = jnp.einsum('bhsm,bmhd->bshd', p.astype(v.dtype), v).reshape(B, S, XATTN_HEADS * XATTN_HEAD_DIM)
    return o @ wo


def moe_ffn(x, router_w, router_bias, w1, w3, w2, sw1, sw3, sw2):
    B, S, D = x.shape
    T = B * S
    E, K, RB = MOE_EXPERTS, MOE_TOPK, MOE_ROW_BLOCK
    xt = x.reshape(T, D)
    scores = jax.nn.sigmoid((xt @ router_w).astype(F32))
    biased = scores + router_bias.astype(F32)
    grp_score = jnp.sum(lax.top_k(biased.reshape(T, MOE_GROUPS, E // MOE_GROUPS), 2)[0], axis=-1)
    _, grp_idx = lax.top_k(grp_score, MOE_TOPK_GROUPS)
    grp_mask = jnp.sum(jax.nn.one_hot(grp_idx, MOE_GROUPS, dtype=F32), axis=-2) > 0
    exp_mask = jnp.repeat(grp_mask, E // MOE_GROUPS, axis=-1)
    _, top_idx = lax.top_k(jnp.where(exp_mask, biased, NEG_INF), K)
    top_w = jnp.take_along_axis(scores, top_idx, axis=-1)
    top_w = top_w / jnp.sum(top_w, -1, keepdims=True) * MOE_ROUTE_SCALE
    TK = T * K
    e_flat = top_idx.reshape(TK)
    tok_flat = jnp.repeat(jnp.arange(T, dtype=jnp.int32), K)
    w_flat = top_w.reshape(TK)
    order = jnp.argsort(e_flat)
    e_s, tok_s, w_s = e_flat[order], tok_flat[order], w_flat[order]
    counts = jax.ops.segment_sum(jnp.ones((TK,), jnp.int32), e_flat, num_segments=E)
    offs = jnp.cumsum(counts) - counts
    pcounts = (counts + RB - 1) // RB * RB
    pends = jnp.cumsum(pcounts)
    poffs = pends - pcounts
    dest = poffs[e_s] + jnp.arange(TK, dtype=jnp.int32) - offs[e_s]
    n_blocks = (TK + E * (RB - 1) + RB - 1) // RB
    n_rows = n_blocks * RB
    row_tok = jnp.full((n_rows,), T, jnp.int32).at[dest].set(tok_s)
    row_w = jnp.zeros((n_rows,), F32).at[dest].set(w_s)
    blk_exp = jnp.minimum(jnp.searchsorted(pends, jnp.arange(n_blocks) * RB, side='right'), E - 1)
    xpad = jnp.concatenate([xt, jnp.zeros((1, D), xt.dtype)], axis=0)

    def expert_block(y, inp):
        tok, wt, e = inp
        xb = xpad[tok]
        h = jax.nn.silu(xb @ w1[e]) * (xb @ w3[e])
        return y.at[tok].add(((h @ w2[e]) * wt[:, None]).astype(y.dtype)), None

    y, _ = lax.scan(expert_block, jnp.zeros_like(xpad),
                    (row_tok.reshape(n_blocks, RB), row_w.reshape(n_blocks, RB), blk_exp))
    shared = (jax.nn.silu(xt @ sw1) * (xt @ sw3)) @ sw2
    return (y[:T] + shared).reshape(B, S, D)


def setup_inputs(seed: int = 0) -> dict:
    key = jax.random.key(seed)
    ks = jax.random.split(key, 32)

    def nrm(k, shape, scale):
        return jax.random.normal(k, shape, F32) * scale

    L, dk, Dp = NSA_CMP_LEN, NSA_HEAD_DIM, DEPTH
    return {
        'x': nrm(ks[0], (BATCH, SEQ, D_MODEL), 1.0),
        'mem': nrm(ks[1], (BATCH, MEM_LEN, D_MODEL), 1.0),
        'w_in': nrm(ks[2], (Dp, D_MODEL, IN_WIDTH), D_MODEL ** -0.5),
        'nsa_pos_k': nrm(ks[3], (Dp, L, dk), 0.1),
        'nsa_cmp_k_w1': nrm(ks[4], (Dp, L * dk, dk), (L * dk) ** -0.5),
        'nsa_cmp_k_w2': nrm(ks[5], (Dp, dk, dk), dk ** -0.5),
        'nsa_pos_v': nrm(ks[6], (Dp, L, dk), 0.1),
        'nsa_cmp_v_w1': nrm(ks[7], (Dp, L * dk, dk), (L * dk) ** -0.5),
        'nsa_cmp_v_w2': nrm(ks[8], (Dp, dk, dk), dk ** -0.5),
        'mlstm_conv_w': nrm(ks[9], (Dp, MLSTM_CONV, 1, 2 * MLSTM_WIDTH), MLSTM_CONV ** -0.5),
        'mlstm_i_bias': nrm(ks[10], (Dp, MLSTM_HEADS), 0.1),
        'mlstm_f_bias': jnp.linspace(3.0, 6.0, MLSTM_HEADS, dtype=F32)[None, :] + nrm(ks[11], (Dp, MLSTM_HEADS), 0.1),
        'mlstm_norm_g': 1.0 + nrm(ks[12], (Dp, MLSTM_WIDTH), 0.02),
        'w_out': nrm(ks[13], (Dp, MIX_WIDTH, D_MODEL), MIX_WIDTH ** -0.5 * DEEPNORM_BETA),
        'ln1_g': 1.0 + nrm(ks[14], (Dp, D_MODEL), 0.02),
        'ln1_b': nrm(ks[15], (Dp, D_MODEL), 0.02),
        'xa_wq': nrm(ks[16], (Dp, D_MODEL, XATTN_HEADS * XATTN_HEAD_DIM), D_MODEL ** -0.5),
        'xa_wk': nrm(ks[17], (Dp, D_MODEL, XATTN_HEADS * XATTN_HEAD_DIM), D_MODEL ** -0.5),
        'xa_wv': nrm(ks[18], (Dp, D_MODEL, XATTN_HEADS * XATTN_HEAD_DIM), D_MODEL ** -0.5),
        'xa_wo': nrm(ks[19], (Dp, XATTN_HEADS * XATTN_HEAD_DIM, D_MODEL), (XATTN_HEADS * XATTN_HEAD_DIM) ** -0.5 * DEEPNORM_BETA),
        'ln2_g': 1.0 + nrm(ks[20], (Dp, D_MODEL), 0.02),
        'ln2_b': nrm(ks[21], (Dp, D_MODEL), 0.02),
        'router_w': nrm(ks[22], (Dp, D_MODEL, MOE_EXPERTS), D_MODEL ** -0.5),
        'router_bias': nrm(ks[23], (Dp, MOE_EXPERTS), 0.01),
        'moe_w1': nrm(ks[24], (Dp, MOE_EXPERTS, D_MODEL, MOE_FF), D_MODEL ** -0.5),
        'moe_w3': nrm(ks[25], (Dp, MOE_EXPERTS, D_MODEL, MOE_FF), D_MODEL ** -0.5),
        'moe_w2': nrm(ks[26], (Dp, MOE_EXPERTS, MOE_FF, D_MODEL), MOE_FF ** -0.5 * DEEPNORM_BETA),
        'shared_w1': nrm(ks[27], (Dp, D_MODEL, MOE_SHARED_FF), D_MODEL ** -0.5),
        'shared_w3': nrm(ks[28], (Dp, D_MODEL, MOE_SHARED_FF), D_MODEL ** -0.5),
        'shared_w2': nrm(ks[29], (Dp, MOE_SHARED_FF, D_MODEL), MOE_SHARED_FF ** -0.5 * DEEPNORM_BETA),
        'ln3_g': 1.0 + nrm(ks[30], (Dp, D_MODEL), 0.02),
        'ln3_b': nrm(ks[31], (Dp, D_MODEL), 0.02),
    }


def reference(x, mem, w_in, nsa_pos_k, nsa_cmp_k_w1, nsa_cmp_k_w2, nsa_pos_v, nsa_cmp_v_w1, nsa_cmp_v_w2,
              mlstm_conv_w, mlstm_i_bias, mlstm_f_bias, mlstm_norm_g, w_out, ln1_g, ln1_b,
              xa_wq, xa_wk, xa_wv, xa_wo, ln2_g, ln2_b, router_w, router_bias,
              moe_w1, moe_w3, moe_w2, shared_w1, shared_w3, shared_w2, ln3_g, ln3_b):
    cuts = np.cumsum(IN_SPLIT_SIZES)[:-1].tolist()
    for l in range(DEPTH):
        (nsa_q, k_c, v_c, k_s, v_s, k_w, v_w, nsa_g,
         m_q, m_k, m_v, m_o, m_i, m_f) = jnp.split(x @ w_in[l], cuts, axis=-1)
        y_nsa = nsa_mixer(nsa_q, k_c, v_c, k_s, v_s, k_w, v_w, nsa_g,
                          nsa_pos_k[l], nsa_cmp_k_w1[l], nsa_cmp_k_w2[l],
                          nsa_pos_v[l], nsa_cmp_v_w1[l], nsa_cmp_v_w2[l])
        y_mlstm = mlstm_mixer(m_q, m_k, m_v, m_o, m_i, m_f, mlstm_conv_w[l],
                              mlstm_i_bias[l], mlstm_f_bias[l], mlstm_norm_g[l])
        mix = jnp.concatenate([y_nsa, y_mlstm], axis=-1) @ w_out[l]
        x = layer_norm(DEEPNORM_ALPHA * x + mix, ln1_g[l], ln1_b[l])
        xa = memory_cross_attention(x, mem, xa_wq[l], xa_wk[l], xa_wv[l], xa_wo[l])
        x = layer_norm(DEEPNORM_ALPHA * x + xa, ln2_g[l], ln2_b[l])
        ff = moe_ffn(x, router_w[l], router_bias[l], moe_w1[l], moe_w3[l], moe_w2[l],
                     shared_w1[l], shared_w3[l], shared_w2[l])
        x = layer_norm(DEEPNORM_ALPHA * x + ff, ln3_g[l], ln3_b[l])
    return x
```

```python
import functools
import numpy as np
import jax
import jax.numpy as jnp
from jax import lax
from jax.experimental import pallas as pl
from jax.experimental.pallas import tpu as pltpu

F32 = jnp.float32
BF16 = jnp.bfloat16
I32 = jnp.int32

D_MODEL = 1024
MEM_LEN = 256
NSA_HEADS = 8
NSA_GROUPS = 2
NSA_HPG = 4
NSA_DK = 64
NSA_CMP_LEN = 32
NSA_CMP_STRIDE = 16
NSA_SEL_BLOCK = 64
NSA_SEL_TOPN = 8
NSA_WINDOW = 512
ML_HEADS = 4
ML_DH = 128
ML_CHUNK = 64
ML_CONV = 4
XA_HEADS = 4
XA_DH = 256
MOE_E = 256
MOE_K = 8
MOE_GROUPS = 8
MOE_TOPK_GROUPS = 4
MOE_FF = 256
MOE_ROUTE_SCALE = 2.5
DEPTH = 1
ALPHA = (2.0 * DEPTH) ** 0.25
LN_EPS = 1e-5
NEG = -1e30
FORCE_BONUS = 1e4

LANES = 128
ROW_BLOCK = 256
VMEM_LIMIT = 56 * 1024 * 1024

_DN_T = (((1,), (1,)), ((), ()))
_DN_TA = (((0,), (0,)), ((), ()))


def _params(sem):
    return pltpu.CompilerParams(dimension_semantics=sem, vmem_limit_bytes=VMEM_LIMIT)


def _ln_rows(v, g, b):
    mu = jnp.mean(v, axis=-1, keepdims=True)
    d = v - mu
    var = jnp.mean(d * d, axis=-1, keepdims=True)
    return d * lax.rsqrt(var + LN_EPS) * g + b


_SEGS = (('q', 512, BF16), ('cmp', 256, F32), ('ksel', 128, BF16), ('vsel', 128, BF16),
         ('kwin', 128, BF16), ('vwin', 128, BF16), ('gates', 128, F32), ('mq', 512, F32),
         ('mk', 512, F32), ('mv', 512, BF16), ('mo', 512, F32), ('mif', 128, F32))


def _proj_kernel(x_ref, w_ref, *out_refs):
    xb = x_ref[...].astype(BF16)
    off = 0
    for o_ref, (_, wd, _) in zip(out_refs, _SEGS):
        o_ref[...] = jnp.dot(xb, w_ref[:, off:off + wd], preferred_element_type=F32).astype(o_ref.dtype)
        off += wd


def _prep_w_in(w):
    sizes = (512,) + (128,) * 6 + (24,) + (512,) * 4 + (4, 4)
    cuts = np.cumsum(sizes)[:-1].tolist()
    (wq, kc, vc, ks, vs, kw, vw, wg, mq, mk, mv, mo, mi, mf) = jnp.split(w, cuts, axis=1)
    wq = wq.reshape(D_MODEL, NSA_GROUPS, NSA_HPG, NSA_DK).transpose(0, 2, 1, 3).reshape(D_MODEL, 512)
    pad = lambda a: jnp.pad(a, ((0, 0), (0, LANES - a.shape[1])))
    segs = [wq, kc, vc, ks, vs, kw, vw, pad(wg), mq, mk, mv, mo, pad(jnp.concatenate([mi, mf], axis=1))]
    return jnp.concatenate(segs, axis=1).astype(BF16)


def _project(x2d, w_all, tm):
    T = x2d.shape[0]
    n = w_all.shape[1]
    out_shape = tuple(jax.ShapeDtypeStruct((T, wd), dt) for _, wd, dt in _SEGS)
    out_specs = tuple(pl.BlockSpec((tm, wd), lambda i: (i, 0)) for _, wd, _ in _SEGS)
    return pl.pallas_call(
        _proj_kernel, grid=(T // tm,),
        in_specs=[pl.BlockSpec((tm, D_MODEL), lambda i: (i, 0)),
                  pl.BlockSpec((D_MODEL, n), lambda i: (0, 0))],
        out_specs=out_specs, out_shape=out_shape,
        compiler_params=_params(("parallel",)), name="proj")(x2d, w_all)


def _cmp_kernel(r_ref, pa_ref, pb_ref, wa_ref, wb_ref, w2_ref, o_ref):
    r = r_ref[...]
    a = jnp.dot((r + pa_ref[...]).astype(BF16), wa_ref[...], preferred_element_type=F32)
    b = jnp.dot((r + pb_ref[...]).astype(BF16), wb_ref[...], preferred_element_type=F32)
    nr = r.shape[0]
    hid = a + pltpu.roll(b, nr - 1, 0)
    hid = hid * jax.nn.sigmoid(hid)
    out = jnp.dot(hid.astype(BF16), w2_ref[...], preferred_element_type=F32)
    row = lax.broadcasted_iota(I32, out.shape, 0)
    o_ref[...] = jnp.where(row < nr - 1, out, 0.0).astype(o_ref.dtype)


def _prep_cmp(pos_k, w1_k, w2_k, pos_v, w1_v, w2_v):
    eye = jnp.eye(NSA_GROUPS, dtype=F32)

    def expand_w1(w1, half):
        w = w1.reshape(NSA_CMP_LEN, NSA_DK, NSA_DK)[half * 16:(half + 1) * 16]
        return jnp.einsum('jde,gk->jgdke', w, eye).reshape(16, 128, 128)

    def both(fk, fv):
        z = jnp.zeros_like(fk)
        top = jnp.concatenate([fk, z], axis=-1)
        bot = jnp.concatenate([z, fv], axis=-1)
        return jnp.concatenate([top, bot], axis=-2)

    wa = both(expand_w1(w1_k, 0), expand_w1(w1_v, 0)).reshape(16 * 256, 256).astype(BF16)
    wb = both(expand_w1(w1_k, 1), expand_w1(w1_v, 1)).reshape(16 * 256, 256).astype(BF16)
    w2 = both(jnp.kron(eye, w2_k), jnp.kron(eye, w2_v)).astype(BF16)

    def pos_row(half):
        pk = jnp.tile(pos_k[half * 16:(half + 1) * 16], (1, NSA_GROUPS))
        pv = jnp.tile(pos_v[half * 16:(half + 1) * 16], (1, NSA_GROUPS))
        return jnp.concatenate([pk, pv], axis=1).reshape(1, 16 * 256)

    return pos_row(0), pos_row(1), wa, wb, w2


def _compress(cmp2d, B, S, prep):
    pa, pb, wa, wb, w2 = prep
    nr = S // NSA_CMP_STRIDE
    r = cmp2d.reshape(B, nr, NSA_CMP_STRIDE * 256)
    full = lambda a: pl.BlockSpec(a.shape, lambda b: (0,) * a.ndim)
    return pl.pallas_call(
        _cmp_kernel, grid=(B,),
        in_specs=[pl.BlockSpec((None, nr, NSA_CMP_STRIDE * 256), lambda b: (b, 0, 0)),
                  full(pa), full(pb), full(wa), full(wb), full(w2)],
        out_specs=pl.BlockSpec((None, nr, 256), lambda b: (b, 0, 0)),
        out_shape=jax.ShapeDtypeStruct((B, nr, 256), BF16),
        compiler_params=_params(("parallel",)), name="cmp")(r, pa, pb, wa, wb, w2)


def _nsa_consts(S):
    n_cmp = (S - NSA_CMP_LEN) // NSA_CMP_STRIDE + 1
    n_sel = S // NSA_SEL_BLOCK
    cs = np.arange(n_cmp) * NSA_CMP_STRIDE
    ss = np.arange(n_sel) * NSA_SEL_BLOCK
    ov = ((cs[:, None] < ss[None, :] + NSA_SEL_BLOCK) & (cs[:, None] + NSA_CMP_LEN > ss[None, :]))
    ovt = np.zeros((LANES, S // NSA_CMP_STRIDE), np.float32)
    ovt[:n_sel, :n_cmp] = ov.T
    e = np.zeros((LANES, S), np.float32)
    e[np.arange(S) // NSA_SEL_BLOCK, np.arange(S)] = 1.0
    return jnp.asarray(ovt, BF16), jnp.asarray(e, BF16)


def _nsa_kernel(q_ref, kcvc_ref, ksel_ref, vsel_ref, kwin_ref, vwin_ref, gates_ref, ovt_ref, e_ref,
                y_ref, bias_ref, *, tq, ck, n_sel):
    H = NSA_HPG
    M = H * tq
    S = ksel_ref.shape[0]
    W = NSA_WINDOW
    ws = min(W + tq, S)
    scale = NSA_DK ** -0.5
    t0 = pl.program_id(1) * tq
    gates = jax.nn.sigmoid(gates_ref[...])
    lane = lax.broadcasted_iota(I32, (tq, LANES), 1)
    t_col = t0 + lax.broadcasted_iota(I32, (tq, 1), 0)
    kc = kcvc_ref[:, 0:LANES]
    vc = kcvc_ref[:, LANES:2 * LANES]
    nc = kc.shape[0]
    outs = [None] * H
    for g in range(NSA_GROUPS):
        gmask = (lane // NSA_DK) == g
        qs = []
        for h in range(H):
            qh = q_ref[:, h * LANES:(h + 1) * LANES]
            qs.append(jnp.where(gmask, qh, jnp.zeros_like(qh)))
        Q = jnp.concatenate(qs, axis=0)

        s = lax.dot_general(Q, kc, _DN_T, preferred_element_type=F32) * scale
        c_idx = lax.broadcasted_iota(I32, (tq, nc), 1)
        cmask = (c_idx * NSA_CMP_STRIDE + NSA_CMP_LEN - 1) <= t_col
        s3 = jnp.where(cmask[None], s.reshape(H, tq, nc), NEG)
        p = jnp.exp(s3 - jnp.max(s3, axis=-1, keepdims=True))
        p = p / jnp.sum(p, axis=-1, keepdims=True)
        p = jnp.where(cmask[None], p, 0.0)
        o_cmp = jnp.dot(p.reshape(M, nc).astype(BF16), vc, preferred_element_type=F32).reshape(H, tq, LANES)

        psum = jnp.sum(p, axis=0)
        hi = psum.astype(BF16)
        lo = (psum - hi.astype(F32)).astype(BF16)
        ovt = ovt_ref[...]
        pslt = (lax.dot_general(ovt, hi, _DN_T, preferred_element_type=F32) +
                lax.dot_general(ovt, lo, _DN_T, preferred_element_type=F32))
        imp_p = pslt[0:n_sel, :]
        n_i = lax.broadcasted_iota(I32, (n_sel, tq), 0)
        cur = (t0 + lax.broadcasted_iota(I32, (n_sel, tq), 1)) // NSA_SEL_BLOCK
        forced = (n_i == 0) | (n_i == cur) | (n_i == cur - 1)
        imp = jnp.where(n_i <= cur, imp_p + jnp.where(forced, FORCE_BONUS, 0.0), NEG)
        cnt = jnp.zeros((n_sel, tq), F32)
        for m in range(n_sel):
            row = imp[m:m + 1, :]
            beats = (row > imp) | ((row == imp) & (n_i > m))
            cnt = cnt + jnp.where(beats, 1.0, 0.0)
        selt = jnp.where(cnt < float(min(NSA_SEL_TOPN, n_sel)), 1.0, 0.0)
        selt = jnp.concatenate([selt, jnp.zeros((LANES - n_sel, tq), F32)], axis=0)
        sel = selt.T.astype(BF16)
        maskf = jnp.dot(sel, e_ref[...], preferred_element_type=F32)
        kpos = lax.broadcasted_iota(I32, (tq, S), 1)
        bias_ref[...] = jnp.where((maskf > 0.5) & (kpos <= t_col), 0.0, NEG)

        def sel_body(j, carry):
            m_i, l_i, acc = carry
            ks = pl.multiple_of(j * ck, ck)
            k = ksel_ref[pl.ds(ks, ck), :]
            v = vsel_ref[pl.ds(ks, ck), :]
            sj = lax.dot_general(Q, k, _DN_T, preferred_element_type=F32) * scale
            sj = sj.reshape(H, tq, ck) + bias_ref[:, pl.ds(ks, ck)][None]
            m_new = jnp.maximum(m_i, jnp.max(sj, axis=-1, keepdims=True))
            a = jnp.exp(m_i - m_new)
            pj = jnp.exp(sj - m_new)
            l_new = a * l_i + jnp.sum(pj, axis=-1, keepdims=True)
            pv = jnp.dot(pj.reshape(M, ck).astype(BF16), v, preferred_element_type=F32)
            return m_new, l_new, a * acc + pv.reshape(H, tq, LANES)

        init = (jnp.full((H, tq, 1), NEG, F32), jnp.zeros((H, tq, 1), F32), jnp.zeros((H, tq, LANES), F32))
        _, l_f, acc = lax.fori_loop(0, (t0 + tq + ck - 1) // ck, sel_body, init)
        o_sel = acc / l_f

        kst = pl.multiple_of(jnp.clip(t0 - W, 0, S - ws), LANES)
        kwn = kwin_ref[pl.ds(kst, ws), :]
        vwn = vwin_ref[pl.ds(kst, ws), :]
        sw = lax.dot_general(Q, kwn, _DN_T, preferred_element_type=F32) * scale
        wpos = kst + lax.broadcasted_iota(I32, (tq, ws), 1)
        wmask = (wpos <= t_col) & (wpos > t_col - W)
        sw3 = jnp.where(wmask[None], sw.reshape(H, tq, ws), NEG)
        pw = jnp.exp(sw3 - jnp.max(sw3, axis=-1, keepdims=True))
        lw = jnp.sum(pw, axis=-1, keepdims=True)
        o_win = jnp.dot(pw.reshape(M, ws).astype(BF16), vwn, preferred_element_type=F32).reshape(H, tq, LANES) / lw

        for h in range(H):
            c0 = (g * H + h) * 3
            o = (gates[:, c0:c0 + 1] * o_cmp[h] + gates[:, c0 + 1:c0 + 2] * o_sel[h] +
                 gates[:, c0 + 2:c0 + 3] * o_win[h])
            outs[h] = o if g == 0 else jnp.where(gmask, o, outs[h])
    for h in range(H):
        y_ref[:, h * LANES:(h + 1) * LANES] = outs[h].astype(y_ref.dtype)


def _nsa(q, kcvc, ksel, vsel, kwin, vwin, gates, B, S, tq, ck):
    T = B * S
    nq = S // tq
    ovt, e = _nsa_consts(S)
    seq = lambda a: a.reshape(B, S, LANES)
    kv_spec = pl.BlockSpec((None, S, LANES), lambda b, i: (b, 0, 0))
    kern = functools.partial(_nsa_kernel, tq=tq, ck=ck, n_sel=S // NSA_SEL_BLOCK)
    return pl.pallas_call(
        kern, grid=(B, nq),
        in_specs=[pl.BlockSpec((tq, 512), lambda b, i: (b * nq + i, 0)),
                  pl.BlockSpec((None,) + kcvc.shape[1:], lambda b, i: (b, 0, 0)),
                  kv_spec, kv_spec, kv_spec, kv_spec,
                  pl.BlockSpec((tq, LANES), lambda b, i: (b * nq + i, 0)),
                  pl.BlockSpec(ovt.shape, lambda b, i: (0, 0)),
                  pl.BlockSpec(e.shape, lambda b, i: (0, 0))],
        out_specs=pl.BlockSpec((tq, 512), lambda b, i: (b * nq + i, 0)),
        out_shape=jax.ShapeDtypeStruct((T, 512), BF16),
        scratch_shapes=[pltpu.VMEM((tq, S), F32)],
        compiler_params=_params(("parallel", "parallel")), name="nsa",
    )(q, kcvc, seq(ksel), seq(vsel), seq(kwin), seq(vwin), gates, ovt, e)


def _mlstm_kernel(q_ref, k_ref, v_ref, o_ref, gn_ref, gt_ref, cw_ref, bn_ref, bt_ref, ng_ref, tri_ref,
                  y_ref, c_scr, n_scr):
    H, dh, L = ML_HEADS, ML_DH, ML_CHUNK
    S = q_ref.shape[0]
    nchunk = S // L
    c_scr[...] = jnp.zeros_like(c_scr)
    n_scr[...] = jnp.zeros_like(n_scr)
    row = lax.broadcasted_iota(I32, (L, H * dh), 0)
    li = lax.broadcasted_iota(I32, (L, L), 0)
    mi = lax.broadcasted_iota(I32, (L, L), 1)
    causal = mi <= li
    tril = tri_ref[0]
    triu = tri_ref[1]
    hp = lax.Precision.HIGHEST

    def conv_silu(ref, c, wofs):
        r0 = pl.multiple_of(c * L, L)
        rp = pl.multiple_of(jnp.maximum(c - 1, 0) * L, L)
        cur = ref[pl.ds(r0, L), :]
        prev = ref[pl.ds(rp, L), :] * jnp.where(c > 0, 1.0, 0.0)
        acc = cur * cw_ref[ML_CONV - 1:ML_CONV, wofs:wofs + H * dh]
        for j in range(1, ML_CONV):
            sh = jnp.where(row < j, pltpu.roll(prev, j, 0), pltpu.roll(cur, j, 0))
            acc = acc + sh * cw_ref[ML_CONV - 1 - j:ML_CONV - j, wofs:wofs + H * dh]
        return acc * jax.nn.sigmoid(acc)

    def body(c, m_state):
        r0 = pl.multiple_of(c * L, L)
        qa = conv_silu(q_ref, c, 0) * (dh ** -0.5)
        ka = conv_silu(k_ref, c, H * dh)
        va = v_ref[pl.ds(r0, L), :]
        oa = o_ref[pl.ds(r0, L), :]
        gn = gn_ref[pl.ds(r0, L), :] + bn_ref[...]
        gt = gt_ref[:, c, :] + bt_ref[...]
        lf_n = jax.nn.log_sigmoid(gn)
        lf_t = jax.nn.log_sigmoid(gt)
        b_n = jnp.dot(tril, lf_n, precision=hp, preferred_element_type=F32)
        b_t = jnp.dot(lf_t, triu, precision=hp, preferred_element_type=F32)
        new_m = []
        for h in range(H):
            q = qa[:, h * dh:(h + 1) * dh]
            k = ka[:, h * dh:(h + 1) * dh]
            v = va[:, h * dh:(h + 1) * dh]
            m_old = m_state[h]
            b_col = b_n[:, H + h:H + h + 1]
            i_col = gn[:, h:h + 1]
            b_row = b_t[H + h:H + h + 1, :]
            i_row = gt[h:h + 1, :]
            g_tot = b_t[H + h:H + h + 1, L - 1:L]
            d_log = jnp.where(causal, b_col - b_row + i_row, NEG)
            inter = b_col + m_old
            m_q = jnp.maximum(inter, jnp.max(d_log, axis=-1, keepdims=True))
            w_intra = jnp.exp(d_log - m_q)
            w_inter = jnp.exp(inter - m_q)
            qb = q.astype(BF16)
            s = lax.dot_general(qb, k.astype(BF16), _DN_T, preferred_element_type=F32) * w_intra
            cst = c_scr[h]
            nst = n_scr[h]
            num = (w_inter * jnp.dot(qb, cst.astype(BF16), preferred_element_type=F32) +
                   jnp.dot(s.astype(BF16), v, preferred_element_type=F32))
            den = w_inter * jnp.sum(q * nst, axis=-1, keepdims=True) + jnp.sum(s, axis=-1, keepdims=True)
            hv = num / jnp.maximum(jnp.abs(den), jnp.exp(-m_q))
            log_k = g_tot - b_col + i_col
            m_new = jnp.maximum(g_tot + m_old, jnp.max(log_k, axis=0, keepdims=True))
            wk = jnp.exp(log_k - m_new)
            decay = jnp.exp(g_tot + m_old - m_new)
            kw = k * wk
            c_scr[h] = decay * cst + lax.dot_general(kw.astype(BF16), v, _DN_TA, preferred_element_type=F32)
            n_scr[h] = decay * nst + jnp.sum(kw, axis=0, keepdims=True)
            new_m.append(m_new)
            mu = jnp.mean(hv, axis=-1, keepdims=True)
            dv = hv - mu
            var = jnp.mean(dv * dv, axis=-1, keepdims=True)
            hn = dv * lax.rsqrt(var + LN_EPS) * ng_ref[:, h * dh:(h + 1) * dh]
            og = jax.nn.sigmoid(oa[:, h * dh:(h + 1) * dh])
            y_ref[pl.ds(r0, L), h * dh:(h + 1) * dh] = (og * hn).astype(y_ref.dtype)
        return tuple(new_m)

    lax.fori_loop(0, nchunk, body, tuple(jnp.zeros((1, 1), F32) for _ in range(H)))


def _mlstm(mq, mk, mv, mo, mif, conv_w, i_bias, f_bias, norm_g, B, S):
    T = B * S
    H, dh, L = ML_HEADS, ML_DH, ML_CHUNK
    W = H * dh
    gt = mif[:, :2 * H].reshape(B, S, 2 * H).transpose(0, 2, 1).reshape(B, 2 * H, S // L, L)
    cw = conv_w.reshape(ML_CONV, 2 * W)
    bias = jnp.concatenate([i_bias, f_bias])
    bn = jnp.pad(bias, (0, LANES - 2 * H)).reshape(1, LANES)
    bt = bias.reshape(2 * H, 1)
    ng = norm_g.reshape(1, W)
    tri = jnp.stack([jnp.tril(jnp.ones((L, L), F32)), jnp.triu(jnp.ones((L, L), F32))])
    rows = lambda w: pl.BlockSpec((S, w), lambda b: (b, 0))
    full = lambda a: pl.BlockSpec(a.shape, lambda b: (0,) * a.ndim)
    return pl.pallas_call(
        _mlstm_kernel, grid=(B,),
        in_specs=[rows(W), rows(W), rows(W), rows(W), rows(LANES),
                  pl.BlockSpec((None, 2 * H, S // L, L), lambda b: (b, 0, 0, 0)),
                  full(cw), full(bn), full(bt), full(ng), full(tri)],
        out_specs=rows(W),
        out_shape=jax.ShapeDtypeStruct((T, W), BF16),
        scratch_shapes=[pltpu.VMEM((H, dh, dh), F32), pltpu.VMEM((H, 1, dh), F32)],
        compiler_params=_params(("parallel",)), name="mlstm",
    )(mq, mk, mv, mo, mif, gt, cw, bn, bt, ng, tri)


def _outproj_kernel(yn_ref, ym_ref, x_ref, w_ref, g_ref, b_ref, o_ref):
    mix = (jnp.dot(yn_ref[...], w_ref[0:512, :], preferred_element_type=F32) +
           jnp.dot(ym_ref[...], w_ref[512:1024, :], preferred_element_type=F32))
    o_ref[...] = _ln_rows(ALPHA * x_ref[...] + mix, g_ref[...], b_ref[...])


def _outproj(y_nsa, y_ml, x2d, w_out, g, b, tm):
    T = x2d.shape[0]
    wn = w_out[:512].reshape(NSA_GROUPS, NSA_HPG, NSA_DK, D_MODEL).transpose(1, 0, 2, 3).reshape(512, D_MODEL)
    w = jnp.concatenate([wn, w_out[512:]], axis=0).astype(BF16)
    row = lambda wd: pl.BlockSpec((tm, wd), lambda i: (i, 0))
    full = lambda a: pl.BlockSpec(a.shape, lambda i: (0,) * a.ndim)
    g2, b2 = g.reshape(1, -1), b.reshape(1, -1)
    return pl.pallas_call(
        _outproj_kernel, grid=(T // tm,),
        in_specs=[row(512), row(512), row(D_MODEL), full(w), full(g2), full(b2)],
        out_specs=row(D_MODEL), out_shape=jax.ShapeDtypeStruct((T, D_MODEL), F32),
        compiler_params=_params(("parallel",)), name="outproj")(y_nsa, y_ml, x2d, w, g2, b2)


def _memkv_kernel(m_ref, w_ref, o_ref):
    o_ref[...] = jnp.dot(m_ref[...].astype(BF16), w_ref[...], preferred_element_type=F32).astype(o_ref.dtype)


def _memkv(mem2d, wk, wv):
    w = jnp.concatenate([wk, wv], axis=1).astype(BF16)
    R = mem2d.shape[0]
    return pl.pallas_call(
        _memkv_kernel, grid=(R // MEM_LEN,),
        in_specs=[pl.BlockSpec((MEM_LEN, D_MODEL), lambda i: (i, 0)),
                  pl.BlockSpec(w.shape, lambda i: (0, 0))],
        out_specs=pl.BlockSpec((MEM_LEN, 2 * D_MODEL), lambda i: (i, 0)),
        out_shape=jax.ShapeDtypeStruct((R, 2 * D_MODEL), BF16),
        compiler_params=_params(("parallel",)), name="memkv")(mem2d, w)


def _xattn_kernel(x_ref, kv_ref, wq_ref, wo_ref, g_ref, b_ref, rw_ref, x2_ref, sc_ref):
    x1 = x_ref[...]
    q = jnp.dot(x1.astype(BF16), wq_ref[...], preferred_element_type=F32).astype(BF16)
    outs = []
    for h in range(XA_HEADS):
        qh = q[:, h * XA_DH:(h + 1) * XA_DH]
        kh = kv_ref[:, h * XA_DH:(h + 1) * XA_DH]
        vh = kv_ref[:, D_MODEL + h * XA_DH:D_MODEL + (h + 1) * XA_DH]
        s = lax.dot_general(qh, kh, _DN_T, preferred_element_type=F32) * (XA_DH ** -0.5)
        p = jnp.exp(s - jnp.max(s, axis=-1, keepdims=True))
        p = p / jnp.sum(p, axis=-1, keepdims=True)
        outs.append(jnp.dot(p.astype(BF16), vh, preferred_element_type=F32).astype(BF16))
    o = jnp.concatenate(outs, axis=1)
    xa = jnp.dot(o, wo_ref[...], preferred_element_type=F32)
    x2 = _ln_rows(ALPHA * x1 + xa, g_ref[...], b_ref[...])
    x2_ref[...] = x2
    xh = x2.astype(BF16)
    xl = (x2 - xh.astype(F32)).astype(BF16)
    wh = rw_ref[0]
    wl = rw_ref[1]
    logit = (lax.dot_general(wh, xh, _DN_T, preferred_element_type=F32) +
             lax.dot_general(wh, xl, _DN_T, preferred_element_type=F32) +
             lax.dot_general(wl, xh, _DN_T, preferred_element_type=F32))
    sc_ref[...] = jax.nn.sigmoid(logit)


def _xattn(x1, kv, wq, wo, g, b, router_w, S, tq):
    T = x1.shape[0]
    wqb, wob = wq.astype(BF16), wo.astype(BF16)
    rwt = router_w.T
    rh = rwt.astype(BF16)
    rw = jnp.stack([rh, (rwt - rh.astype(F32)).astype(BF16)])
    g2, b2 = g.reshape(1, -1), b.reshape(1, -1)
    full = lambda a: pl.BlockSpec(a.shape, lambda i: (0,) * a.ndim)
    per = S // tq
    return pl.pallas_call(
        _xattn_kernel, grid=(T // tq,),
        in_specs=[pl.BlockSpec((tq, D_MODEL), lambda i: (i, 0)),
                  pl.BlockSpec((MEM_LEN, 2 * D_MODEL), lambda i: (i // per, 0)),
                  full(wqb), full(wob), full(g2), full(b2), full(rw)],
        out_specs=(pl.BlockSpec((tq, D_MODEL), lambda i: (i, 0)),
                   pl.BlockSpec((MOE_E, tq), lambda i: (0, i))),
        out_shape=(jax.ShapeDtypeStruct((T, D_MODEL), F32), jax.ShapeDtypeStruct((MOE_E, T), F32)),
        compiler_params=_params(("parallel",)), name="xattn")(x1, kv, wqb, wob, g2, b2, rw)


def _route_kernel(sc_ref, rb_ref, idx_ref, w_ref):
    E, G = MOE_E, MOE_GROUPS
    per = E // G
    scores = sc_ref[...]
    tr = scores.shape[1]
    biased = scores + rb_ref[...]
    g3 = biased.reshape(G, per, tr)
    j3 = lax.broadcasted_iota(I32, (G, per, tr), 1)
    m1 = jnp.max(g3, axis=1, keepdims=True)
    first = jnp.min(jnp.where(g3 == m1, j3, per), axis=1, keepdims=True)
    m2 = jnp.max(jnp.where(j3 == first, -jnp.inf, g3), axis=1, keepdims=True)
    gs = (m1 + m2).reshape(G, tr)
    gi = lax.broadcasted_iota(I32, (G, tr), 0)
    cnt = jnp.zeros((G, tr), F32)
    for m in range(G):
        row = gs[m:m + 1, :]
        cnt = cnt + jnp.where((row > gs) | ((row == gs) & (gi > m)), 1.0, 0.0)
    gmask = cnt < float(MOE_TOPK_GROUPS)
    masked = jnp.where(gmask[:, None, :], g3, NEG).reshape(E, tr)
    ei = lax.broadcasted_iota(I32, (E, tr), 0)
    idxs, ws = [], []
    for _ in range(MOE_K):
        mx = jnp.max(masked, axis=0, keepdims=True)
        ix = jnp.min(jnp.where(masked == mx, ei, E), axis=0, keepdims=True)
        hit = ei == ix
        ws.append(jnp.sum(jnp.where(hit, scores, 0.0), axis=0, keepdims=True))
        idxs.append(ix)
        masked = jnp.where(hit, -jnp.inf, masked)
    w = jnp.concatenate(ws, axis=0)
    idx_ref[...] = jnp.concatenate(idxs, axis=0)
    w_ref[...] = w / jnp.sum(w, axis=0, keepdims=True) * MOE_ROUTE_SCALE


def _route(scores_t, router_bias, tr):
    E, T = scores_t.shape
    rb = router_bias.reshape(E, 1)
    return pl.pallas_call(
        _route_kernel, grid=(T // tr,),
        in_specs=[pl.BlockSpec((E, tr), lambda i: (0, i)), pl.BlockSpec((E, 1), lambda i: (0, 0))],
        out_specs=(pl.BlockSpec((MOE_K, tr), lambda i: (0, i)), pl.BlockSpec((MOE_K, tr), lambda i: (0, i))),
        out_shape=(jax.ShapeDtypeStruct((MOE_K, T), I32), jax.ShapeDtypeStruct((MOE_K, T), F32)),
        compiler_params=_params(("parallel",)), name="route")(scores_t, rb)


def _rank_kernel(idx_ref, u_ref, rank_ref, cnt_ref, carry):
    E = MOE_E

    @pl.when(pl.program_id(0) == 0)
    def _():
        carry[...] = jnp.zeros_like(carry)

    idx = idx_ref[...]
    tp = idx.shape[1]
    ei = lax.broadcasted_iota(I32, (E, tp), 0)
    hits = [ei == idx[k:k + 1, :] for k in range(MOE_K)]
    onehot = jnp.zeros((E, tp), F32)
    for hit in hits:
        onehot = onehot + jnp.where(hit, 1.0, 0.0)
    pos = jnp.dot(onehot.astype(BF16), u_ref[...], preferred_element_type=F32) + carry[...]
    ranks = [jnp.sum(jnp.where(hit, pos, 0.0), axis=0, keepdims=True) for hit in hits]
    rank_ref[...] = jnp.concatenate(ranks, axis=0).astype(I32)
    total = carry[...] + jnp.sum(onehot, axis=1, keepdims=True)
    carry[...] = total
    cnt_ref[...] = jnp.broadcast_to(total, cnt_ref.shape).astype(I32)


def _rank(idx_t, tp):
    K, T = idx_t.shape
    u = jnp.triu(jnp.ones((tp, tp), F32), k=1).astype(BF16)
    rank, cnt = pl.pallas_call(
        _rank_kernel, grid=(T // tp,),
        in_specs=[pl.BlockSpec((K, tp), lambda i: (0, i)), pl.BlockSpec((tp, tp), lambda i: (0, 0))],
        out_specs=(pl.BlockSpec((K, tp), lambda i: (0, i)), pl.BlockSpec((MOE_E, LANES), lambda i: (0, 0))),
        out_shape=(jax.ShapeDtypeStruct((K, T), I32), jax.ShapeDtypeStruct((MOE_E, LANES), I32)),
        scratch_shapes=[pltpu.VMEM((MOE_E, 1), F32)],
        compiler_params=_params(("arbitrary",)), name="rank")(idx_t, u)
    return rank, cnt[:, 0]


def _scatter_kernel(poffs_ref, idx_ref, rank_ref, x_ref, xs_ref, sem):
    ts = x_ref.shape[0]

    def copy(t, k):
        d = poffs_ref[idx_ref[k, t]] + rank_ref[k, t]
        return pltpu.make_async_copy(x_ref.at[pl.ds(t, 1)], xs_ref.at[pl.ds(d, 1)], sem)

    def start(t, c):
        for k in range(MOE_K):
            copy(t, k).start()
        return c

    def wait(t, c):
        for k in range(MOE_K):
            copy(t, k).wait()
        return c

    lax.fori_loop(0, ts, start, 0)
    lax.fori_loop(0, ts, wait, 0)


def _scatter(poffs, idx_t, rank_t, x2, n_rows, ts):
    T = x2.shape[0]
    smem = lambda: pl.BlockSpec((MOE_K, ts), lambda i, p: (0, i), memory_space=pltpu.SMEM)
    gs = pltpu.PrefetchScalarGridSpec(
        num_scalar_prefetch=1, grid=(T // ts,),
        in_specs=[smem(), smem(), pl.BlockSpec((ts, D_MODEL), lambda i, p: (i, 0))],
        out_specs=pl.BlockSpec(memory_space=pl.ANY),
        scratch_shapes=[pltpu.SemaphoreType.DMA(())])
    return pl.pallas_call(
        _scatter_kernel, grid_spec=gs, out_shape=jax.ShapeDtypeStruct((n_rows, D_MODEL), F32),
        compiler_params=_params(("arbitrary",)), name="scatter")(poffs, idx_t, rank_t, x2)


def _expert_kernel(be_ref, nv_ref, xs_ref, w1_ref, w3_ref, w2_ref, ys_ref):
    i = pl.program_id(0)
    nv = nv_ref[i]

    @pl.when(nv > 0)
    def _():
        row = lax.broadcasted_iota(I32, xs_ref.shape, 0)
        xb = jnp.where(row < nv, xs_ref[...], 0.0).astype(BF16)
        a = jnp.dot(xb, w1_ref[...].astype(BF16), preferred_element_type=F32)
        c = jnp.dot(xb, w3_ref[...].astype(BF16), preferred_element_type=F32)
        h = (a * jax.nn.sigmoid(a) * c).astype(BF16)
        ys_ref[...] = jnp.dot(h, w2_ref[...].astype(BF16), preferred_element_type=F32)

    @pl.when(nv <= 0)
    def _():
        ys_ref[...] = jnp.zeros_like(ys_ref)


def _experts(blk_exp, blk_valid, xs, w1, w3, w2):
    n_rows = xs.shape[0]
    rb = ROW_BLOCK
    gs = pltpu.PrefetchScalarGridSpec(
        num_scalar_prefetch=2, grid=(n_rows // rb,),
        in_specs=[pl.BlockSpec((rb, D_MODEL), lambda i, be, nv: (i, 0)),
                  pl.BlockSpec((None, D_MODEL, MOE_FF), lambda i, be, nv: (be[i], 0, 0)),
                  pl.BlockSpec((None, D_MODEL, MOE_FF), lambda i, be, nv: (be[i], 0, 0)),
                  pl.BlockSpec((None, MOE_FF, D_MODEL), lambda i, be, nv: (be[i], 0, 0))],
        out_specs=pl.BlockSpec((rb, D_MODEL), lambda i, be, nv: (i, 0)))
    return pl.pallas_call(
        _expert_kernel, grid_spec=gs, out_shape=jax.ShapeDtypeStruct((n_rows, D_MODEL), F32),
        compiler_params=_params(("arbitrary",)), name="experts")(blk_exp, blk_valid, xs, w1, w3, w2)


def _combine_kernel(poffs_ref, idx_ref, rank_ref, w_ref, x_ref, ys_ref, s1_ref, s3_ref, s2_ref, g_ref, b_ref,
                    o_ref, buf, sem):
    tc = x_ref.shape[0]

    def copy(t, k):
        d = poffs_ref[idx_ref[k, t]] + rank_ref[k, t]
        return pltpu.make_async_copy(ys_ref.at[pl.ds(d, 1)], buf.at[k, pl.ds(t, 1)], sem)

    def start(t, c):
        for k in range(MOE_K):
            copy(t, k).start()
        return c

    def wait(t, c):
        for k in range(MOE_K):
            copy(t, k).wait()
        return c

    lax.fori_loop(0, tc, start, 0)
    x2 = x_ref[...]
    xb = x2.astype(BF16)
    a = jnp.dot(xb, s1_ref[...], preferred_element_type=F32)
    c = jnp.dot(xb, s3_ref[...], preferred_element_type=F32)
    shared = jnp.dot((a * jax.nn.sigmoid(a) * c).astype(BF16), s2_ref[...], preferred_element_type=F32)
    lax.fori_loop(0, tc, wait, 0)
    y = shared
    w = w_ref[...]
    for k in range(MOE_K):
        y = y + w[:, k:k + 1] * buf[k]
    o_ref[...] = _ln_rows(ALPHA * x2 + y, g_ref[...], b_ref[...])


def _combine(poffs, idx_t, rank_t, w_nat, x2, ys, sw1, sw3, sw2, g, b, tc):
    T = x2.shape[0]
    smem = lambda: pl.BlockSpec((MOE_K, tc), lambda i, p: (0, i), memory_space=pltpu.SMEM)
    full = lambda a: pl.BlockSpec(a.shape, lambda i, p: (0,) * a.ndim)
    s1, s3, s2 = sw1.astype(BF16), sw3.astype(BF16), sw2.astype(BF16)
    g2, b2 = g.reshape(1, -1), b.reshape(1, -1)
    gs = pltpu.PrefetchScalarGridSpec(
        num_scalar_prefetch=1, grid=(T // tc,),
        in_specs=[smem(), smem(), pl.BlockSpec((tc, MOE_K), lambda i, p: (i, 0)),
                  pl.BlockSpec((tc, D_MODEL), lambda i, p: (i, 0)),
                  pl.BlockSpec(memory_space=pl.ANY), full(s1), full(s3), full(s2), full(g2), full(b2)],
        out_specs=pl.BlockSpec((tc, D_MODEL), lambda i, p: (i, 0)),
        scratch_shapes=[pltpu.VMEM((MOE_K, tc, D_MODEL), F32), pltpu.SemaphoreType.DMA(())])
    return pl.pallas_call(
        _combine_kernel, grid_spec=gs, out_shape=jax.ShapeDtypeStruct((T, D_MODEL), F32),
        compiler_params=_params(("arbitrary",)), name="combine",
    )(poffs, idx_t, rank_t, w_nat, x2, ys, s1, s3, s2, g2, b2)


def _moe(x2, scores_t, router_bias, w1, w3, w2, sw1, sw3, sw2, g, b, tiles):
    T = x2.shape[0]
    rb = ROW_BLOCK
    idx_t, w_t = _route(scores_t, router_bias, tiles['route'])
    rank_t, counts = _rank(idx_t, tiles['rank'])
    pcounts = (counts + rb - 1) // rb * rb
    pends = jnp.cumsum(pcounts)
    poffs = (pends - pcounts).astype(I32)
    n_blocks = (T * MOE_K + MOE_E * (rb - 1) + rb - 1) // rb
    blk_start = jnp.arange(n_blocks, dtype=I32) * rb
    blk_exp = jnp.minimum(jnp.searchsorted(pends, blk_start, side='right'), MOE_E - 1).astype(I32)
    blk_valid = jnp.clip(counts[blk_exp] - (blk_start - poffs[blk_exp]), 0, rb).astype(I32)
    xs = _scatter(poffs, idx_t, rank_t, x2, n_blocks * rb, tiles['scatter'])
    ys = _experts(blk_exp, blk_valid, xs, w1, w3, w2)
    return _combine(poffs, idx_t, rank_t, w_t.T, x2, ys, sw1, sw3, sw2, g, b, tiles['combine'])


def _tiles(B, S):
    T = B * S
    pick = lambda want, n: want if n % want == 0 else n
    return dict(proj=pick(512, T), nsa_q=pick(128, S), nsa_ck=pick(256, S), outproj=pick(512, T),
                xattn=pick(256, S), route=pick(512, T), rank=pick(512, T), scatter=pick(256, T),
                combine=pick(128, T))


def kernel(x, mem, w_in, nsa_pos_k, nsa_cmp_k_w1, nsa_cmp_k_w2, nsa_pos_v, nsa_cmp_v_w1, nsa_cmp_v_w2,
           mlstm_conv_w, mlstm_i_bias, mlstm_f_bias, mlstm_norm_g, w_out, ln1_g, ln1_b,
           xa_wq, xa_wk, xa_wv, xa_wo, ln2_g, ln2_b, router_w, router_bias,
           moe_w1, moe_w3, moe_w2, shared_w1, shared_w3, shared_w2, ln3_g, ln3_b):
    B, S, D = x.shape
    T = B * S
    tl = _tiles(B, S)
    xc = x.reshape(T, D)
    memc = mem.reshape(B * MEM_LEN, D)
    for l in range(w_in.shape[0]):
        (q, cmp, ksel, vsel, kwin, vwin, gates, mq, mk, mv, mo, mif) = _project(xc, _prep_w_in(w_in[l]), tl['proj'])
        kcvc = _compress(cmp, B, S, _prep_cmp(nsa_pos_k[l], nsa_cmp_k_w1[l], nsa_cmp_k_w2[l],
                                              nsa_pos_v[l], nsa_cmp_v_w1[l], nsa_cmp_v_w2[l]))
        y_nsa = _nsa(q, kcvc, ksel, vsel, kwin, vwin, gates, B, S, tl['nsa_q'], tl['nsa_ck'])
        y_ml = _mlstm(mq, mk, mv, mo, mif, mlstm_conv_w[l], mlstm_i_bias[l], mlstm_f_bias[l],
                      mlstm_norm_g[l], B, S)
        x1 = _outproj(y_nsa, y_ml, xc, w_out[l], ln1_g[l], ln1_b[l], tl['outproj'])
        kv = _memkv(memc, xa_wk[l], xa_wv[l])
        x2, scores_t = _xattn(x1, kv, xa_wq[l], xa_wo[l], ln2_g[l], ln2_b[l], router_w[l], S, tl['xattn'])
        xc = _moe(x2, scores_t, router_bias[l], moe_w1[l], moe_w3[l], moe_w2[l],
                  shared_w1[l], shared_w3[l], shared_w2[l], ln3_g[l], ln3_b[l], tl)
    return xc.reshape(B, S, D)
```

```python
import functools
import numpy as np
import jax
import jax.numpy as jnp
from jax import lax
from jax.experimental import pallas as pl
from jax.experimental.pallas import tpu as pltpu

F32 = jnp.float32
BF16 = jnp.bfloat16
I32 = jnp.int32

D_MODEL = 1024
MEM_LEN = 256
NSA_HEADS = 8
NSA_GROUPS = 2
NSA_HPG = 4
NSA_DK = 64
NSA_CMP_LEN = 32
NSA_CMP_STRIDE = 16
NSA_SEL_BLOCK = 64
NSA_SEL_TOPN = 8
NSA_WINDOW = 512
ML_HEADS = 4
ML_DH = 128
ML_CHUNK = 64
ML_CONV = 4
XA_HEADS = 4
XA_DH = 256
MOE_E = 256
MOE_K = 8
MOE_GROUPS = 8
MOE_TOPK_GROUPS = 4
MOE_FF = 256
MOE_ROUTE_SCALE = 2.5
DEPTH = 1
ALPHA = (2.0 * DEPTH) ** 0.25
LN_EPS = 1e-5
NEG = -1e30
FORCE_BONUS = 1e4

LANES = 128
ROW_BLOCK = 256
VMEM_LIMIT = 56 * 1024 * 1024

_DN_T = (((1,), (1,)), ((), ()))
_DN_TA = (((0,), (0,)), ((), ()))


def _params(sem):
    return pltpu.CompilerParams(dimension_semantics=sem, vmem_limit_bytes=VMEM_LIMIT)


def _ln_rows(v, g, b):
    mu = jnp.mean(v, axis=-1, keepdims=True)
    d = v - mu
    var = jnp.mean(d * d, axis=-1, keepdims=True)
    return d * lax.rsqrt(var + LN_EPS) * g + b


_SEGS = (('q', 512, BF16), ('cmp', 256, F32), ('ksel', 128, BF16), ('vsel', 128, BF16),
         ('kwin', 128, BF16), ('vwin', 128, BF16), ('gates', 128, F32), ('mq', 512, F32),
         ('mk', 512, F32), ('mv', 512, BF16), ('mo', 512, F32), ('mif', 128, F32))


def _proj_kernel(x_ref, w_ref, *out_refs):
    xb = x_ref[...].astype(BF16)
    off = 0
    for o_ref, (_, wd, _) in zip(out_refs, _SEGS):
        o_ref[...] = jnp.dot(xb, w_ref[:, off:off + wd], preferred_element_type=F32).astype(o_ref.dtype)
        off += wd


def _prep_w_in(w):
    sizes = (512,) + (128,) * 6 + (24,) + (512,) * 4 + (4, 4)
    cuts = np.cumsum(sizes)[:-1].tolist()
    (wq, kc, vc, ks, vs, kw, vw, wg, mq, mk, mv, mo, mi, mf) = jnp.split(w, cuts, axis=1)
    wq = wq.reshape(D_MODEL, NSA_GROUPS, NSA_HPG, NSA_DK).transpose(0, 2, 1, 3).reshape(D_MODEL, 512)
    pad = lambda a: jnp.pad(a, ((0, 0), (0, LANES - a.shape[1])))
    segs = [wq, kc, vc, ks, vs, kw, vw, pad(wg), mq, mk, mv, mo, pad(jnp.concatenate([mi, mf], axis=1))]
    return jnp.concatenate(segs, axis=1).astype(BF16)


def _project(x2d, w_all, tm):
    T = x2d.shape[0]
    n = w_all.shape[1]
    out_shape = tuple(jax.ShapeDtypeStruct((T, wd), dt) for _, wd, dt in _SEGS)
    out_specs = tuple(pl.BlockSpec((tm, wd), lambda i: (i, 0)) for _, wd, _ in _SEGS)
    return pl.pallas_call(
        _proj_kernel, grid=(T // tm,),
        in_specs=[pl.BlockSpec((tm, D_MODEL), lambda i: (i, 0)),
                  pl.BlockSpec((D_MODEL, n), lambda i: (0, 0))],
        out_specs=out_specs, out_shape=out_shape,
        compiler_params=_params(("parallel",)), name="proj")(x2d, w_all)


def _cmp_kernel(r_ref, pa_ref, pb_ref, wa_ref, wb_ref, w2_ref, o_ref):
    r = r_ref[...]
    a = jnp.dot((r + pa_ref[...]).astype(BF16), wa_ref[...], preferred_element_type=F32)
    b = jnp.dot((r + pb_ref[...]).astype(BF16), wb_ref[...], preferred_element_type=F32)
    nr = r.shape[0]
    hid = a + pltpu.roll(b, nr - 1, 0)
    hid = hid * jax.nn.sigmoid(hid)
    out = jnp.dot(hid.astype(BF16), w2_ref[...], preferred_element_type=F32)
    row = lax.broadcasted_iota(I32, out.shape, 0)
    o_ref[...] = jnp.where(row < nr - 1, out, 0.0).astype(o_ref.dtype)


def _prep_cmp(pos_k, w1_k, w2_k, pos_v, w1_v, w2_v):
    eye = jnp.eye(NSA_GROUPS, dtype=F32)

    def expand_w1(w1, half):
        w = w1.reshape(NSA_CMP_LEN, NSA_DK, NSA_DK)[half * 16:(half + 1) * 16]
        return jnp.einsum('jde,gk->jgdke', w, eye).reshape(16, 128, 128)

    def both(fk, fv):
        z = jnp.zeros_like(fk)
        top = jnp.concatenate([fk, z], axis=-1)
        bot = jnp.concatenate([z, fv], axis=-1)
        return jnp.concatenate([top, bot], axis=-2)

    wa = both(expand_w1(w1_k, 0), expand_w1(w1_v, 0)).reshape(16 * 256, 256).astype(BF16)
    wb = both(expand_w1(w1_k, 1), expand_w1(w1_v, 1)).reshape(16 * 256, 256).astype(BF16)
    w2 = both(jnp.kron(eye, w2_k), jnp.kron(eye, w2_v)).astype(BF16)

    def pos_row(half):
        pk = jnp.tile(pos_k[half * 16:(half + 1) * 16], (1, NSA_GROUPS))
        pv = jnp.tile(pos_v[half * 16:(half + 1) * 16], (1, NSA_GROUPS))
        return jnp.concatenate([pk, pv], axis=1).reshape(1, 16 * 256)

    return pos_row(0), pos_row(1), wa, wb, w2


def _compress(cmp2d, B, S, prep):
    pa, pb, wa, wb, w2 = prep
    nr = S // NSA_CMP_STRIDE
    r = cmp2d.reshape(B, nr, NSA_CMP_STRIDE * 256)
    full = lambda a: pl.BlockSpec(a.shape, lambda b: (0,) * a.ndim)
    return pl.pallas_call(
        _cmp_kernel, grid=(B,),
        in_specs=[pl.BlockSpec((None, nr, NSA_CMP_STRIDE * 256), lambda b: (b, 0, 0)),
                  full(pa), full(pb), full(wa), full(wb), full(w2)],
        out_specs=pl.BlockSpec((None, nr, 256), lambda b: (b, 0, 0)),
        out_shape=jax.ShapeDtypeStruct((B, nr, 256), BF16),
        compiler_params=_params(("parallel",)), name="cmp")(r, pa, pb, wa, wb, w2)


def _nsa_consts(S):
    n_cmp = (S - NSA_CMP_LEN) // NSA_CMP_STRIDE + 1
    n_sel = S // NSA_SEL_BLOCK
    cs = np.arange(n_cmp) * NSA_CMP_STRIDE
    ss = np.arange(n_sel) * NSA_SEL_BLOCK
    ov = ((cs[:, None] < ss[None, :] + NSA_SEL_BLOCK) & (cs[:, None] + NSA_CMP_LEN > ss[None, :]))
    ovt = np.zeros((LANES, S // NSA_CMP_STRIDE), np.float32)
    ovt[:n_sel, :n_cmp] = ov.T
    e = np.zeros((LANES, S), np.float32)
    e[np.arange(S) // NSA_SEL_BLOCK, np.arange(S)] = 1.0
    return jnp.asarray(ovt, BF16), jnp.asarray(e, BF16)


def _nsa_kernel(q_ref, kcvc_ref, ksel_ref, vsel_ref, kwin_ref, vwin_ref, gates_ref, ovt_ref, e_ref,
                y_ref, bias_ref, *, tq, ck, n_sel):
    G, H = NSA_GROUPS, NSA_HPG
    GH = G * H
    M = GH * tq
    S = ksel_ref.shape[0]
    W = NSA_WINDOW
    ws = min(W + tq, S)
    t0 = pl.program_id(1) * tq
    gates = jax.nn.sigmoid(gates_ref[...])
    lane = lax.broadcasted_iota(I32, (tq, LANES), 1)
    t_col = t0 + lax.broadcasted_iota(I32, (tq, 1), 0)
    kc = kcvc_ref[:, 0:LANES]
    vc = kcvc_ref[:, LANES:2 * LANES]
    nc = kc.shape[0]
    qs = []
    for g in range(G):
        gmask = (lane // NSA_DK) == g
        for h in range(H):
            qh = q_ref[:, h * LANES:(h + 1) * LANES] * (NSA_DK ** -0.5)
            qs.append(jnp.where(gmask, qh, jnp.zeros_like(qh)))
    Q = jnp.concatenate(qs, axis=0).astype(BF16)

    s = lax.dot_general(Q, kc, _DN_T, preferred_element_type=F32)
    c_idx = lax.broadcasted_iota(I32, (tq, nc), 1)
    cmask = (c_idx * NSA_CMP_STRIDE + NSA_CMP_LEN - 1) <= t_col
    s3 = jnp.where(cmask[None], s.reshape(GH, tq, nc), NEG)
    p = jnp.exp(s3 - jnp.max(s3, axis=-1, keepdims=True))
    p = p / jnp.sum(p, axis=-1, keepdims=True)
    p = jnp.where(cmask[None], p, 0.0)
    o_cmp = jnp.dot(p.reshape(M, nc).astype(BF16), vc, preferred_element_type=F32).reshape(GH, tq, LANES)

    for g in range(G):
        psum = jnp.sum(p[g * H:(g + 1) * H], axis=0)
        hi = psum.astype(BF16)
        lo = (psum - hi.astype(F32)).astype(BF16)
        ovt = ovt_ref[...]
        pslt = (lax.dot_general(ovt, hi, _DN_T, preferred_element_type=F32) +
                lax.dot_general(ovt, lo, _DN_T, preferred_element_type=F32))
        imp_p = pslt[0:n_sel, :]
        n_i = lax.broadcasted_iota(I32, (n_sel, tq), 0)
        cur = (t0 + lax.broadcasted_iota(I32, (n_sel, tq), 1)) // NSA_SEL_BLOCK
        forced = (n_i == 0) | (n_i == cur) | (n_i == cur - 1)
        imp = jnp.where(n_i <= cur, imp_p + jnp.where(forced, FORCE_BONUS, 0.0), NEG)
        cnt = jnp.zeros((n_sel, tq), F32)
        for m in range(n_sel):
            row = imp[m:m + 1, :]
            beats = (row > imp) | ((row == imp) & (n_i > m))
            cnt = cnt + jnp.where(beats, 1.0, 0.0)
        selt = jnp.where(cnt < float(min(NSA_SEL_TOPN, n_sel)), 1.0, 0.0)
        selt = jnp.concatenate([selt, jnp.zeros((LANES - n_sel, tq), F32)], axis=0)
        sel = selt.T.astype(BF16)
        maskf = jnp.dot(sel, e_ref[...], preferred_element_type=F32)
        kpos = lax.broadcasted_iota(I32, (tq, S), 1)
        bias_ref[g] = jnp.where((maskf > 0.5) & (kpos <= t_col), 0.0, NEG)

    def sel_body(j, carry):
        m_i, l_i, acc = carry
        ks = pl.multiple_of(j * ck, ck)
        k = ksel_ref[pl.ds(ks, ck), :]
        v = vsel_ref[pl.ds(ks, ck), :]
        sj = lax.dot_general(Q, k, _DN_T, preferred_element_type=F32)
        sj = (sj.reshape(G, H, tq, ck) + bias_ref[:, :, pl.ds(ks, ck)][:, None]).reshape(GH, tq, ck)
        m_new = jnp.maximum(m_i, jnp.max(sj, axis=-1, keepdims=True))
        a = jnp.exp(m_i - m_new)
        pj = jnp.exp(sj - m_new)
        l_new = a * l_i + jnp.sum(pj, axis=-1, keepdims=True)
        pv = jnp.dot(pj.reshape(M, ck).astype(BF16), v, preferred_element_type=F32)
        return m_new, l_new, a * acc + pv.reshape(GH, tq, LANES)

    init = (jnp.full((GH, tq, 1), NEG, F32), jnp.zeros((GH, tq, 1), F32), jnp.zeros((GH, tq, LANES), F32))
    _, l_f, acc = lax.fori_loop(0, (t0 + tq + ck - 1) // ck, sel_body, init)
    o_sel = acc / l_f

    kst = pl.multiple_of(jnp.clip(t0 - W, 0, S - ws), LANES)
    kwn = kwin_ref[pl.ds(kst, ws), :]
    vwn = vwin_ref[pl.ds(kst, ws), :]
    sw = lax.dot_general(Q, kwn, _DN_T, preferred_element_type=F32)
    wpos = kst + lax.broadcasted_iota(I32, (tq, ws), 1)
    wmask = (wpos <= t_col) & (wpos > t_col - W)
    sw3 = jnp.where(wmask[None], sw.reshape(GH, tq, ws), NEG)
    pw = jnp.exp(sw3 - jnp.max(sw3, axis=-1, keepdims=True))
    lw = jnp.sum(pw, axis=-1, keepdims=True)
    o_win = jnp.dot(pw.reshape(M, ws).astype(BF16), vwn, preferred_element_type=F32).reshape(GH, tq, LANES) / lw

    g0mask = lane < NSA_DK
    for h in range(H):
        o_g = []
        for g in range(G):
            r = g * H + h
            c0 = r * 3
            o_g.append(gates[:, c0:c0 + 1] * o_cmp[r] + gates[:, c0 + 1:c0 + 2] * o_sel[r] +
                       gates[:, c0 + 2:c0 + 3] * o_win[r])
        y_ref[:, h * LANES:(h + 1) * LANES] = jnp.where(g0mask, o_g[0], o_g[1]).astype(y_ref.dtype)


def _nsa(q, kcvc, ksel, vsel, kwin, vwin, gates, B, S, tq, ck):
    T = B * S
    nq = S // tq
    ovt, e = _nsa_consts(S)
    seq = lambda a: a.reshape(B, S, LANES)
    kv_spec = pl.BlockSpec((None, S, LANES), lambda b, i: (b, 0, 0))
    kern = functools.partial(_nsa_kernel, tq=tq, ck=ck, n_sel=S // NSA_SEL_BLOCK)
    return pl.pallas_call(
        kern, grid=(B, nq),
        in_specs=[pl.BlockSpec((tq, 512), lambda b, i: (b * nq + i, 0)),
                  pl.BlockSpec((None,) + kcvc.shape[1:], lambda b, i: (b, 0, 0)),
                  kv_spec, kv_spec, kv_spec, kv_spec,
                  pl.BlockSpec((tq, LANES), lambda b, i: (b * nq + i, 0)),
                  pl.BlockSpec(ovt.shape, lambda b, i: (0, 0)),
                  pl.BlockSpec(e.shape, lambda b, i: (0, 0))],
        out_specs=pl.BlockSpec((tq, 512), lambda b, i: (b * nq + i, 0)),
        out_shape=jax.ShapeDtypeStruct((T, 512), BF16),
        scratch_shapes=[pltpu.VMEM((NSA_GROUPS, tq, S), F32)],
        compiler_params=_params(("parallel", "parallel")), name="nsa",
    )(q, kcvc, seq(ksel), seq(vsel), seq(kwin), seq(vwin), gates, ovt, e)


def _mlstm_kernel(q_ref, k_ref, v_ref, o_ref, gn_ref, gt_ref, cw_ref, bn_ref, bt_ref, ng_ref, tri_ref,
                  y_ref, c_scr, n_scr):
    H, dh, L = ML_HEADS, ML_DH, ML_CHUNK
    S = q_ref.shape[0]
    nchunk = S // L
    c_scr[...] = jnp.zeros_like(c_scr)
    n_scr[...] = jnp.zeros_like(n_scr)
    row = lax.broadcasted_iota(I32, (L, H * dh), 0)
    li = lax.broadcasted_iota(I32, (L, L), 0)
    mi = lax.broadcasted_iota(I32, (L, L), 1)
    causal = mi <= li
    tril = tri_ref[0]
    triu = tri_ref[1]
    hp = lax.Precision.HIGHEST

    def conv_silu(ref, c, wofs):
        r0 = pl.multiple_of(c * L, L)
        rp = pl.multiple_of(jnp.maximum(c - 1, 0) * L, L)
        cur = ref[pl.ds(r0, L), :]
        prev = ref[pl.ds(rp, L), :] * jnp.where(c > 0, 1.0, 0.0)
        acc = cur * cw_ref[ML_CONV - 1:ML_CONV, wofs:wofs + H * dh]
        for j in range(1, ML_CONV):
            sh = jnp.where(row < j, pltpu.roll(prev, j, 0), pltpu.roll(cur, j, 0))
            acc = acc + sh * cw_ref[ML_CONV - 1 - j:ML_CONV - j, wofs:wofs + H * dh]
        return acc * jax.nn.sigmoid(acc)

    def body(c, m_state):
        r0 = pl.multiple_of(c * L, L)
        qa = conv_silu(q_ref, c, 0) * (dh ** -0.5)
        ka = conv_silu(k_ref, c, H * dh)
        va = v_ref[pl.ds(r0, L), :]
        oa = o_ref[pl.ds(r0, L), :]
        gn = gn_ref[pl.ds(r0, L), :] + bn_ref[...]
        gt = gt_ref[:, c, :] + bt_ref[...]
        lf_n = jax.nn.log_sigmoid(gn)
        lf_t = jax.nn.log_sigmoid(gt)
        b_n = jnp.dot(tril, lf_n, precision=hp, preferred_element_type=F32)
        b_t = jnp.dot(lf_t, triu, precision=hp, preferred_element_type=F32)
        new_m = []
        for h in range(H):
            q = qa[:, h * dh:(h + 1) * dh]
            k = ka[:, h * dh:(h + 1) * dh]
            v = va[:, h * dh:(h + 1) * dh]
            m_old = m_state[h]
            b_col = b_n[:, H + h:H + h + 1]
            i_col = gn[:, h:h + 1]
            b_row = b_t[H + h:H + h + 1, :]
            i_row = gt[h:h + 1, :]
            g_tot = b_t[H + h:H + h + 1, L - 1:L]
            d_log = jnp.where(causal, b_col - b_row + i_row, NEG)
            inter = b_col + m_old
            m_q = jnp.maximum(inter, jnp.max(d_log, axis=-1, keepdims=True))
            w_intra = jnp.exp(d_log - m_q)
            w_inter = jnp.exp(inter - m_q)
            qb = q.astype(BF16)
            s = lax.dot_general(qb, k.astype(BF16), _DN_T, preferred_element_type=F32) * w_intra
            cst = c_scr[h]
            nst = n_scr[h]
            num = (w_inter * jnp.dot(qb, cst.astype(BF16), preferred_element_type=F32) +
                   jnp.dot(s.astype(BF16), v, preferred_element_type=F32))
            den = w_inter * jnp.sum(q * nst, axis=-1, keepdims=True) + jnp.sum(s, axis=-1, keepdims=True)
            hv = num / jnp.maximum(jnp.abs(den), jnp.exp(-m_q))
            log_k = g_tot - b_col + i_col
            m_new = jnp.maximum(g_tot + m_old, jnp.max(log_k, axis=0, keepdims=True))
            wk = jnp.exp(log_k - m_new)
            decay = jnp.exp(g_tot + m_old - m_new)
            kw = k * wk
            c_scr[h] = decay * cst + lax.dot_general(kw.astype(BF16), v, _DN_TA, preferred_element_type=F32)
            n_scr[h] = decay * nst + jnp.sum(kw, axis=0, keepdims=True)
            new_m.append(m_new)
            mu = jnp.mean(hv, axis=-1, keepdims=True)
            dv = hv - mu
            var = jnp.mean(dv * dv, axis=-1, keepdims=True)
            hn = dv * lax.rsqrt(var + LN_EPS) * ng_ref[:, h * dh:(h + 1) * dh]
            og = jax.nn.sigmoid(oa[:, h * dh:(h + 1) * dh])
            y_ref[pl.ds(r0, L), h * dh:(h + 1) * dh] = (og * hn).astype(y_ref.dtype)
        return tuple(new_m)

    lax.fori_loop(0, nchunk, body, tuple(jnp.zeros((1, 1), F32) for _ in range(H)))


def _mlstm(mq, mk, mv, mo, mif, conv_w, i_bias, f_bias, norm_g, B, S):
    T = B * S
    H, dh, L = ML_HEADS, ML_DH, ML_CHUNK
    W = H * dh
    gt = mif[:, :2 * H].reshape(B, S, 2 * H).transpose(0, 2, 1).reshape(B, 2 * H, S // L, L)
    cw = conv_w.reshape(ML_CONV, 2 * W)
    bias = jnp.concatenate([i_bias, f_bias])
    bn = jnp.pad(bias, (0, LANES - 2 * H)).reshape(1, LANES)
    bt = bias.reshape(2 * H, 1)
    ng = norm_g.reshape(1, W)
    tri = jnp.stack([jnp.tril(jnp.ones((L, L), F32)), jnp.triu(jnp.ones((L, L), F32))])
    rows = lambda w: pl.BlockSpec((S, w), lambda b: (b, 0))
    full = lambda a: pl.BlockSpec(a.shape, lambda b: (0,) * a.ndim)
    return pl.pallas_call(
        _mlstm_kernel, grid=(B,),
        in_specs=[rows(W), rows(W), rows(W), rows(W), rows(LANES),
                  pl.BlockSpec((None, 2 * H, S // L, L), lambda b: (b, 0, 0, 0)),
                  full(cw), full(bn), full(bt), full(ng), full(tri)],
        out_specs=rows(W),
        out_shape=jax.ShapeDtypeStruct((T, W), BF16),
        scratch_shapes=[pltpu.VMEM((H, dh, dh), F32), pltpu.VMEM((H, 1, dh), F32)],
        compiler_params=_params(("parallel",)), name="mlstm",
    )(mq, mk, mv, mo, mif, gt, cw, bn, bt, ng, tri)


def _outproj_kernel(yn_ref, ym_ref, x_ref, w_ref, g_ref, b_ref, o_ref):
    mix = (jnp.dot(yn_ref[...], w_ref[0:512, :], preferred_element_type=F32) +
           jnp.dot(ym_ref[...], w_ref[512:1024, :], preferred_element_type=F32))
    o_ref[...] = _ln_rows(ALPHA * x_ref[...] + mix, g_ref[...], b_ref[...])


def _outproj(y_nsa, y_ml, x2d, w_out, g, b, tm):
    T = x2d.shape[0]
    wn = w_out[:512].reshape(NSA_GROUPS, NSA_HPG, NSA_DK, D_MODEL).transpose(1, 0, 2, 3).reshape(512, D_MODEL)
    w = jnp.concatenate([wn, w_out[512:]], axis=0).astype(BF16)
    row = lambda wd: pl.BlockSpec((tm, wd), lambda i: (i, 0))
    full = lambda a: pl.BlockSpec(a.shape, lambda i: (0,) * a.ndim)
    g2, b2 = g.reshape(1, -1), b.reshape(1, -1)
    return pl.pallas_call(
        _outproj_kernel, grid=(T // tm,),
        in_specs=[row(512), row(512), row(D_MODEL), full(w), full(g2), full(b2)],
        out_specs=row(D_MODEL), out_shape=jax.ShapeDtypeStruct((T, D_MODEL), F32),
        compiler_params=_params(("parallel",)), name="outproj")(y_nsa, y_ml, x2d, w, g2, b2)


def _memkv_kernel(m_ref, w_ref, o_ref):
    o_ref[...] = jnp.dot(m_ref[...].astype(BF16), w_ref[...], preferred_element_type=F32).astype(o_ref.dtype)


def _memkv(mem2d, wk, wv):
    w = jnp.concatenate([wk, wv], axis=1).astype(BF16)
    R = mem2d.shape[0]
    return pl.pallas_call(
        _memkv_kernel, grid=(R // MEM_LEN,),
        in_specs=[pl.BlockSpec((MEM_LEN, D_MODEL), lambda i: (i, 0)),
                  pl.BlockSpec(w.shape, lambda i: (0, 0))],
        out_specs=pl.BlockSpec((MEM_LEN, 2 * D_MODEL), lambda i: (i, 0)),
        out_shape=jax.ShapeDtypeStruct((R, 2 * D_MODEL), BF16),
        compiler_params=_params(("parallel",)), name="memkv")(mem2d, w)


def _xattn_kernel(x_ref, kv_ref, wq_ref, wo_ref, g_ref, b_ref, rw_ref, x2_ref, x2p_ref, sc_ref):
    x1 = x_ref[...]
    q = jnp.dot(x1.astype(BF16), wq_ref[...], preferred_element_type=F32).astype(BF16)
    outs = []
    for h in range(XA_HEADS):
        qh = q[:, h * XA_DH:(h + 1) * XA_DH]
        kh = kv_ref[:, h * XA_DH:(h + 1) * XA_DH]
        vh = kv_ref[:, D_MODEL + h * XA_DH:D_MODEL + (h + 1) * XA_DH]
        s = lax.dot_general(qh, kh, _DN_T, preferred_element_type=F32) * (XA_DH ** -0.5)
        p = jnp.exp(s - jnp.max(s, axis=-1, keepdims=True))
        p = p / jnp.sum(p, axis=-1, keepdims=True)
        outs.append(jnp.dot(p.astype(BF16), vh, preferred_element_type=F32).astype(BF16))
    o = jnp.concatenate(outs, axis=1)
    xa = jnp.dot(o, wo_ref[...], preferred_element_type=F32)
    x2 = _ln_rows(ALPHA * x1 + xa, g_ref[...], b_ref[...])
    x2_ref[...] = x2
    x2p_ref[...] = _pack_bf16_pairs(x2)
    xh = x2.astype(BF16)
    xl = (x2 - xh.astype(F32)).astype(BF16)
    wh = rw_ref[0]
    wl = rw_ref[1]
    logit = (lax.dot_general(wh, xh, _DN_T, preferred_element_type=F32) +
             lax.dot_general(wh, xl, _DN_T, preferred_element_type=F32) +
             lax.dot_general(wl, xh, _DN_T, preferred_element_type=F32))
    sc_ref[...] = jax.nn.sigmoid(logit)


def _xattn(x1, kv, wq, wo, g, b, router_w, S, tq):
    T = x1.shape[0]
    wqb, wob = wq.astype(BF16), wo.astype(BF16)
    rwt = router_w.T
    rh = rwt.astype(BF16)
    rw = jnp.stack([rh, (rwt - rh.astype(F32)).astype(BF16)])
    g2, b2 = g.reshape(1, -1), b.reshape(1, -1)
    full = lambda a: pl.BlockSpec(a.shape, lambda i: (0,) * a.ndim)
    per = S // tq
    return pl.pallas_call(
        _xattn_kernel, grid=(T // tq,),
        in_specs=[pl.BlockSpec((tq, D_MODEL), lambda i: (i, 0)),
                  pl.BlockSpec((MEM_LEN, 2 * D_MODEL), lambda i: (i // per, 0)),
                  full(wqb), full(wob), full(g2), full(b2), full(rw)],
        out_specs=(pl.BlockSpec((tq, D_MODEL), lambda i: (i, 0)),
                   pl.BlockSpec((tq, D_MODEL // 2), lambda i: (i, 0)),
                   pl.BlockSpec((MOE_E, tq), lambda i: (0, i))),
        out_shape=(jax.ShapeDtypeStruct((T, D_MODEL), F32), jax.ShapeDtypeStruct((T, D_MODEL // 2), jnp.uint32),
                   jax.ShapeDtypeStruct((MOE_E, T), F32)),
        compiler_params=_params(("parallel",)), name="xattn")(x1, kv, wqb, wob, g2, b2, rw)


def _route_kernel(sc_ref, rb_ref, idx_ref, w_ref):
    E, G = MOE_E, MOE_GROUPS
    per = E // G
    scores = sc_ref[...]
    tr = scores.shape[1]
    biased = scores + rb_ref[...]
    g3 = biased.reshape(G, per, tr)
    j3 = lax.broadcasted_iota(I32, (G, per, tr), 1)
    m1 = jnp.max(g3, axis=1, keepdims=True)
    first = jnp.min(jnp.where(g3 == m1, j3, per), axis=1, keepdims=True)
    m2 = jnp.max(jnp.where(j3 == first, -jnp.inf, g3), axis=1, keepdims=True)
    gs = (m1 + m2).reshape(G, tr)
    gi = lax.broadcasted_iota(I32, (G, tr), 0)
    cnt = jnp.zeros((G, tr), F32)
    for m in range(G):
        row = gs[m:m + 1, :]
        cnt = cnt + jnp.where((row > gs) | ((row == gs) & (gi > m)), 1.0, 0.0)
    gmask = cnt < float(MOE_TOPK_GROUPS)
    masked = jnp.where(gmask[:, None, :], g3, NEG).reshape(E, tr)
    ei = lax.broadcasted_iota(I32, (E, tr), 0)
    idxs, ws = [], []
    for _ in range(MOE_K):
        mx = jnp.max(masked, axis=0, keepdims=True)
        ix = jnp.min(jnp.where(masked == mx, ei, E), axis=0, keepdims=True)
        hit = ei == ix
        ws.append(jnp.sum(jnp.where(hit, scores, 0.0), axis=0, keepdims=True))
        idxs.append(ix)
        masked = jnp.where(hit, -jnp.inf, masked)
    w = jnp.concatenate(ws, axis=0)
    idx_ref[...] = jnp.concatenate(idxs, axis=0)
    w_ref[...] = w / jnp.sum(w, axis=0, keepdims=True) * MOE_ROUTE_SCALE


def _route(scores_t, router_bias, tr):
    E, T = scores_t.shape
    rb = router_bias.reshape(E, 1)
    return pl.pallas_call(
        _route_kernel, grid=(T // tr,),
        in_specs=[pl.BlockSpec((E, tr), lambda i: (0, i)), pl.BlockSpec((E, 1), lambda i: (0, 0))],
        out_specs=(pl.BlockSpec((MOE_K, tr), lambda i: (0, i)), pl.BlockSpec((MOE_K, tr), lambda i: (0, i))),
        out_shape=(jax.ShapeDtypeStruct((MOE_K, T), I32), jax.ShapeDtypeStruct((MOE_K, T), F32)),
        compiler_params=_params(("parallel",)), name="route")(scores_t, rb)


def _rank_kernel(idx_ref, u_ref, rank_ref, cnt_ref, carry):
    E = MOE_E

    @pl.when(pl.program_id(0) == 0)
    def _():
        carry[...] = jnp.zeros_like(carry)

    idx = idx_ref[...]
    tp = idx.shape[1]
    ei = lax.broadcasted_iota(I32, (E, tp), 0)
    hits = [ei == idx[k:k + 1, :] for k in range(MOE_K)]
    onehot = jnp.zeros((E, tp), F32)
    for hit in hits:
        onehot = onehot + jnp.where(hit, 1.0, 0.0)
    pos = jnp.dot(onehot.astype(BF16), u_ref[...], preferred_element_type=F32) + carry[...]
    ranks = [jnp.sum(jnp.where(hit, pos, 0.0), axis=0, keepdims=True) for hit in hits]
    rank_ref[...] = jnp.concatenate(ranks, axis=0).astype(I32)
    total = carry[...] + jnp.sum(onehot, axis=1, keepdims=True)
    carry[...] = total
    cnt_ref[...] = jnp.broadcast_to(total, cnt_ref.shape).astype(I32)


def _rank(idx_t, tp):
    K, T = idx_t.shape
    u = jnp.triu(jnp.ones((tp, tp), F32), k=1).astype(BF16)
    rank, cnt = pl.pallas_call(
        _rank_kernel, grid=(T // tp,),
        in_specs=[pl.BlockSpec((K, tp), lambda i: (0, i)), pl.BlockSpec((tp, tp), lambda i: (0, 0))],
        out_specs=(pl.BlockSpec((K, tp), lambda i: (0, i)), pl.BlockSpec((MOE_E, LANES), lambda i: (0, 0))),
        out_shape=(jax.ShapeDtypeStruct((K, T), I32), jax.ShapeDtypeStruct((MOE_E, LANES), I32)),
        scratch_shapes=[pltpu.VMEM((MOE_E, 1), F32)],
        compiler_params=_params(("arbitrary",)), name="rank")(idx_t, u)
    return rank, cnt[:, 0]


def _pack_bf16_pairs(v):
    m = v.shape[1] // 2
    bits = lax.bitcast_convert_type(v.astype(BF16).astype(F32), jnp.uint32)
    return (bits[:, :m] >> 16) | (bits[:, m:] & jnp.uint32(0xFFFF0000))


def _unpack_bf16_pairs(w):
    lo = lax.bitcast_convert_type(w << 16, F32)
    hi = lax.bitcast_convert_type(w & jnp.uint32(0xFFFF0000), F32)
    return lo, hi


def _scatter_kernel(dest_ref, x_ref, xs_ref, sem):
    ts = x_ref.shape[0]

    def copy(t, k):
        return pltpu.make_async_copy(x_ref.at[pl.ds(t, 1)], xs_ref.at[pl.ds(dest_ref[k, t], 1)], sem)

    def start(t, c):
        for k in range(MOE_K):
            copy(t, k).start()
        return c

    def wait(t, c):
        for k in range(MOE_K):
            copy(t, k).wait()
        return c

    lax.fori_loop(0, ts, start, 0)
    lax.fori_loop(0, ts, wait, 0)


def _scatter(dest_t, x2p, n_rows, ts):
    T, wp = x2p.shape
    return pl.pallas_call(
        _scatter_kernel, grid=(T // ts,),
        in_specs=[pl.BlockSpec((MOE_K, ts), lambda i: (0, i), memory_space=pltpu.SMEM),
                  pl.BlockSpec((ts, wp), lambda i: (i, 0))],
        out_specs=pl.BlockSpec(memory_space=pl.ANY),
        out_shape=jax.ShapeDtypeStruct((n_rows, wp), jnp.uint32),
        scratch_shapes=[pltpu.SemaphoreType.DMA(())],
        compiler_params=_params(("arbitrary",)), name="scatter")(dest_t, x2p)


def _expert_kernel(be_ref, nv_ref, xs_ref, w1_ref, w3_ref, w2_ref, ys_ref):
    i = pl.program_id(0)
    nv = nv_ref[i]
    hw = D_MODEL // 2

    @pl.when(nv > 0)
    def _():
        row = lax.broadcasted_iota(I32, xs_ref.shape, 0)
        lo, hi = _unpack_bf16_pairs(jnp.where(row < nv, xs_ref[...], jnp.uint32(0)))
        lo, hi = lo.astype(BF16), hi.astype(BF16)
        w1 = w1_ref[...].astype(BF16)
        w3 = w3_ref[...].astype(BF16)
        a = (jnp.dot(lo, w1[:hw], preferred_element_type=F32) + jnp.dot(hi, w1[hw:], preferred_element_type=F32))
        c = (jnp.dot(lo, w3[:hw], preferred_element_type=F32) + jnp.dot(hi, w3[hw:], preferred_element_type=F32))
        h = (a * jax.nn.sigmoid(a) * c).astype(BF16)
        ys_ref[...] = _pack_bf16_pairs(jnp.dot(h, w2_ref[...].astype(BF16), preferred_element_type=F32))

    @pl.when(nv <= 0)
    def _():
        ys_ref[...] = jnp.zeros_like(ys_ref)


def _experts(blk_exp, blk_valid, xs, w1, w3, w2):
    n_rows, wp = xs.shape
    rb = ROW_BLOCK
    gs = pltpu.PrefetchScalarGridSpec(
        num_scalar_prefetch=2, grid=(n_rows // rb,),
        in_specs=[pl.BlockSpec((rb, wp), lambda i, be, nv: (i, 0)),
                  pl.BlockSpec((None, D_MODEL, MOE_FF), lambda i, be, nv: (be[i], 0, 0)),
                  pl.BlockSpec((None, D_MODEL, MOE_FF), lambda i, be, nv: (be[i], 0, 0)),
                  pl.BlockSpec((None, MOE_FF, D_MODEL), lambda i, be, nv: (be[i], 0, 0))],
        out_specs=pl.BlockSpec((rb, wp), lambda i, be, nv: (i, 0)))
    return pl.pallas_call(
        _expert_kernel, grid_spec=gs, out_shape=jax.ShapeDtypeStruct((n_rows, wp), jnp.uint32),
        compiler_params=_params(("arbitrary",)), name="experts")(blk_exp, blk_valid, xs, w1, w3, w2)


def _combine_kernel(dest_ref, w_ref, x_ref, ys_ref, s1_ref, s3_ref, s2_ref, g_ref, b_ref, o_ref, buf, sem):
    tc = x_ref.shape[0]

    def copy(t, k):
        return pltpu.make_async_copy(ys_ref.at[pl.ds(dest_ref[k, t], 1)], buf.at[k, pl.ds(t, 1)], sem)

    def start(t, c):
        for k in range(MOE_K):
            copy(t, k).start()
        return c

    def wait(t, c):
        for k in range(MOE_K):
            copy(t, k).wait()
        return c

    lax.fori_loop(0, tc, start, 0)
    x2 = x_ref[...]
    xb = x2.astype(BF16)
    a = jnp.dot(xb, s1_ref[...], preferred_element_type=F32)
    c = jnp.dot(xb, s3_ref[...], preferred_element_type=F32)
    shared = jnp.dot((a * jax.nn.sigmoid(a) * c).astype(BF16), s2_ref[...], preferred_element_type=F32)
    lax.fori_loop(0, tc, wait, 0)
    w = w_ref[...]
    y_lo = jnp.zeros((tc, D_MODEL // 2), F32)
    y_hi = jnp.zeros((tc, D_MODEL // 2), F32)
    for k in range(MOE_K):
        lo, hi = _unpack_bf16_pairs(buf[k])
        y_lo = y_lo + w[:, k:k + 1] * lo
        y_hi = y_hi + w[:, k:k + 1] * hi
    y = shared + jnp.concatenate([y_lo, y_hi], axis=1)
    o_ref[...] = _ln_rows(ALPHA * x2 + y, g_ref[...], b_ref[...])


def _combine(dest_t, w_nat, x2, ys, sw1, sw3, sw2, g, b, tc):
    T = x2.shape[0]
    wp = ys.shape[1]
    full = lambda a: pl.BlockSpec(a.shape, lambda i: (0,) * a.ndim)
    s1, s3, s2 = sw1.astype(BF16), sw3.astype(BF16), sw2.astype(BF16)
    g2, b2 = g.reshape(1, -1), b.reshape(1, -1)
    return pl.pallas_call(
        _combine_kernel, grid=(T // tc,),
        in_specs=[pl.BlockSpec((MOE_K, tc), lambda i: (0, i), memory_space=pltpu.SMEM),
                  pl.BlockSpec((tc, MOE_K), lambda i: (i, 0)),
                  pl.BlockSpec((tc, D_MODEL), lambda i: (i, 0)),
                  pl.BlockSpec(memory_space=pl.ANY), full(s1), full(s3), full(s2), full(g2), full(b2)],
        out_specs=pl.BlockSpec((tc, D_MODEL), lambda i: (i, 0)),
        out_shape=jax.ShapeDtypeStruct((T, D_MODEL), F32),
        scratch_shapes=[pltpu.VMEM((MOE_K, tc, wp), jnp.uint32), pltpu.SemaphoreType.DMA(())],
        compiler_params=_params(("arbitrary",)), name="combine",
    )(dest_t, w_nat, x2, ys, s1, s3, s2, g2, b2)


def _moe(x2, x2p, scores_t, router_bias, w1, w3, w2, sw1, sw3, sw2, g, b, tiles):
    T = x2.shape[0]
    rb = ROW_BLOCK
    idx_t, w_t = _route(scores_t, router_bias, tiles['route'])
    rank_t, counts = _rank(idx_t, tiles['rank'])
    pcounts = (counts + rb - 1) // rb * rb
    pends = jnp.cumsum(pcounts)
    poffs = (pends - pcounts).astype(I32)
    n_blocks = (T * MOE_K + MOE_E * (rb - 1) + rb - 1) // rb
    blk_start = jnp.arange(n_blocks, dtype=I32) * rb
    blk_exp = jnp.minimum(jnp.searchsorted(pends, blk_start, side='right'), MOE_E - 1).astype(I32)
    blk_valid = jnp.clip(counts[blk_exp] - (blk_start - poffs[blk_exp]), 0, rb).astype(I32)
    dest_t = poffs[idx_t] + rank_t
    xs = _scatter(dest_t, x2p, n_blocks * rb, tiles['scatter'])
    ys = _experts(blk_exp, blk_valid, xs, w1, w3, w2)
    return _combine(dest_t, w_t.T, x2, ys, sw1, sw3, sw2, g, b, tiles['combine'])


def _tiles(B, S):
    T = B * S
    pick = lambda want, n: want if n % want == 0 else n
    return dict(proj=pick(512, T), nsa_q=pick(128, S), nsa_ck=pick(256, S), outproj=pick(512, T),
                xattn=pick(256, S), route=pick(512, T), rank=pick(512, T), scatter=pick(256, T),
                combine=pick(128, T))


def kernel(x, mem, w_in, nsa_pos_k, nsa_cmp_k_w1, nsa_cmp_k_w2, nsa_pos_v, nsa_cmp_v_w1, nsa_cmp_v_w2,
           mlstm_conv_w, mlstm_i_bias, mlstm_f_bias, mlstm_norm_g, w_out, ln1_g, ln1_b,
           xa_wq, xa_wk, xa_wv, xa_wo, ln2_g, ln2_b, router_w, router_bias,
           moe_w1, moe_w3, moe_w2, shared_w1, shared_w3, shared_w2, ln3_g, ln3_b):
    B, S, D = x.shape
    T = B * S
    tl = _tiles(B, S)
    xc = x.reshape(T, D)
    memc = mem.reshape(B * MEM_LEN, D)
    for l in range(w_in.shape[0]):
        (q, cmp, ksel, vsel, kwin, vwin, gates, mq, mk, mv, mo, mif) = _project(xc, _prep_w_in(w_in[l]), tl['proj'])
        kcvc = _compress(cmp, B, S, _prep_cmp(nsa_pos_k[l], nsa_cmp_k_w1[l], nsa_cmp_k_w2[l],
                                              nsa_pos_v[l], nsa_cmp_v_w1[l], nsa_cmp_v_w2[l]))
        y_nsa = _nsa(q, kcvc, ksel, vsel, kwin, vwin, gates, B, S, tl['nsa_q'], tl['nsa_ck'])
        y_ml = _mlstm(mq, mk, mv, mo, mif, mlstm_conv_w[l], mlstm_i_bias[l], mlstm_f_bias[l],
                      mlstm_norm_g[l], B, S)
        x1 = _outproj(y_nsa, y_ml, xc, w_out[l], ln1_g[l], ln1_b[l], tl['outproj'])
        kv = _memkv(memc, xa_wk[l], xa_wv[l])
        x2, x2p, scores_t = _xattn(x1, kv, xa_wq[l], xa_wo[l], ln2_g[l], ln2_b[l], router_w[l], S, tl['xattn'])
        xc = _moe(x2, x2p, scores_t, router_bias[l], moe_w1[l], moe_w3[l], moe_w2[l],
                  shared_w1[l], shared_w3[l], shared_w2[l], ln3_g[l], ln3_b[l], tl)
    return xc.reshape(B, S, D)
```

```python
import functools
import numpy as np
import jax
import jax.numpy as jnp
from jax import lax
from jax.experimental import pallas as pl
from jax.experimental.pallas import tpu as pltpu

F32 = jnp.float32
BF16 = jnp.bfloat16
I32 = jnp.int32

D_MODEL = 1024
MEM_LEN = 256
NSA_HEADS = 8
NSA_GROUPS = 2
NSA_HPG = 4
NSA_DK = 64
NSA_CMP_LEN = 32
NSA_CMP_STRIDE = 16
NSA_SEL_BLOCK = 64
NSA_SEL_TOPN = 8
NSA_WINDOW = 512
ML_HEADS = 4
ML_DH = 128
ML_CHUNK = 64
ML_CONV = 4
XA_HEADS = 4
XA_DH = 256
MOE_E = 256
MOE_K = 8
MOE_GROUPS = 8
MOE_TOPK_GROUPS = 4
MOE_FF = 256
MOE_ROUTE_SCALE = 2.5
DEPTH = 1
ALPHA = (2.0 * DEPTH) ** 0.25
LN_EPS = 1e-5
NEG = -1e30
FORCE_BONUS = 1e4

LANES = 128
ROW_BLOCK = 256
VMEM_LIMIT = 56 * 1024 * 1024

_DN_T = (((1,), (1,)), ((), ()))
_DN_TA = (((0,), (0,)), ((), ()))


def _params(sem):
    return pltpu.CompilerParams(dimension_semantics=sem, vmem_limit_bytes=VMEM_LIMIT)


def _ln_rows(v, g, b):
    mu = jnp.mean(v, axis=-1, keepdims=True)
    d = v - mu
    var = jnp.mean(d * d, axis=-1, keepdims=True)
    return d * lax.rsqrt(var + LN_EPS) * g + b


_SEGS = (('q', 512, BF16), ('cmp', 256, F32), ('ksel', 128, BF16), ('vsel', 128, BF16),
         ('kwin', 128, BF16), ('vwin', 128, BF16), ('gates', 128, F32), ('mq', 512, F32),
         ('mk', 512, F32), ('mv', 512, BF16), ('mo', 512, F32), ('mif', 128, F32))


def _proj_kernel(x_ref, w_ref, *out_refs):
    xb = x_ref[...].astype(BF16)
    off = 0
    for o_ref, (_, wd, _) in zip(out_refs, _SEGS):
        o_ref[...] = jnp.dot(xb, w_ref[:, off:off + wd], preferred_element_type=F32).astype(o_ref.dtype)
        off += wd


def _prep_w_in(w):
    sizes = (512,) + (128,) * 6 + (24,) + (512,) * 4 + (4, 4)
    cuts = np.cumsum(sizes)[:-1].tolist()
    (wq, kc, vc, ks, vs, kw, vw, wg, mq, mk, mv, mo, mi, mf) = jnp.split(w, cuts, axis=1)
    wq = wq.reshape(D_MODEL, NSA_GROUPS, NSA_HPG, NSA_DK).transpose(0, 2, 1, 3).reshape(D_MODEL, 512)
    pad = lambda a: jnp.pad(a, ((0, 0), (0, LANES - a.shape[1])))
    segs = [wq, kc, vc, ks, vs, kw, vw, pad(wg), mq, mk, mv, mo, pad(jnp.concatenate([mi, mf], axis=1))]
    return jnp.concatenate(segs, axis=1).astype(BF16)


def _project(x2d, w_all, tm):
    T = x2d.shape[0]
    n = w_all.shape[1]
    out_shape = tuple(jax.ShapeDtypeStruct((T, wd), dt) for _, wd, dt in _SEGS)
    out_specs = tuple(pl.BlockSpec((tm, wd), lambda i: (i, 0)) for _, wd, _ in _SEGS)
    return pl.pallas_call(
        _proj_kernel, grid=(T // tm,),
        in_specs=[pl.BlockSpec((tm, D_MODEL), lambda i: (i, 0)),
                  pl.BlockSpec((D_MODEL, n), lambda i: (0, 0))],
        out_specs=out_specs, out_shape=out_shape,
        compiler_params=_params(("parallel",)), name="proj")(x2d, w_all)


def _cmp_kernel(r_ref, pa_ref, pb_ref, wa_ref, wb_ref, w2_ref, o_ref):
    r = r_ref[...]
    a = jnp.dot((r + pa_ref[...]).astype(BF16), wa_ref[...], preferred_element_type=F32)
    b = jnp.dot((r + pb_ref[...]).astype(BF16), wb_ref[...], preferred_element_type=F32)
    nr = r.shape[0]
    hid = a + pltpu.roll(b, nr - 1, 0)
    hid = hid * jax.nn.sigmoid(hid)
    out = jnp.dot(hid.astype(BF16), w2_ref[...], preferred_element_type=F32)
    row = lax.broadcasted_iota(I32, out.shape, 0)
    o_ref[...] = jnp.where(row < nr - 1, out, 0.0).astype(o_ref.dtype)


def _prep_cmp(pos_k, w1_k, w2_k, pos_v, w1_v, w2_v):
    eye = jnp.eye(NSA_GROUPS, dtype=F32)

    def expand_w1(w1, half):
        w = w1.reshape(NSA_CMP_LEN, NSA_DK, NSA_DK)[half * 16:(half + 1) * 16]
        return jnp.einsum('jde,gk->jgdke', w, eye).reshape(16, 128, 128)

    def both(fk, fv):
        z = jnp.zeros_like(fk)
        top = jnp.concatenate([fk, z], axis=-1)
        bot = jnp.concatenate([z, fv], axis=-1)
        return jnp.concatenate([top, bot], axis=-2)

    wa = both(expand_w1(w1_k, 0), expand_w1(w1_v, 0)).reshape(16 * 256, 256).astype(BF16)
    wb = both(expand_w1(w1_k, 1), expand_w1(w1_v, 1)).reshape(16 * 256, 256).astype(BF16)
    w2 = both(jnp.kron(eye, w2_k), jnp.kron(eye, w2_v)).astype(BF16)

    def pos_row(half):
        pk = jnp.tile(pos_k[half * 16:(half + 1) * 16], (1, NSA_GROUPS))
        pv = jnp.tile(pos_v[half * 16:(half + 1) * 16], (1, NSA_GROUPS))
        return jnp.concatenate([pk, pv], axis=1).reshape(1, 16 * 256)

    return pos_row(0), pos_row(1), wa, wb, w2


def _compress(cmp2d, B, S, prep):
    pa, pb, wa, wb, w2 = prep
    nr = S // NSA_CMP_STRIDE
    r = cmp2d.reshape(B, nr, NSA_CMP_STRIDE * 256)
    full = lambda a: pl.BlockSpec(a.shape, lambda b: (0,) * a.ndim)
    return pl.pallas_call(
        _cmp_kernel, grid=(B,),
        in_specs=[pl.BlockSpec((None, nr, NSA_CMP_STRIDE * 256), lambda b: (b, 0, 0)),
                  full(pa), full(pb), full(wa), full(wb), full(w2)],
        out_specs=pl.BlockSpec((None, nr, 256), lambda b: (b, 0, 0)),
        out_shape=jax.ShapeDtypeStruct((B, nr, 256), BF16),
        compiler_params=_params(("parallel",)), name="cmp")(r, pa, pb, wa, wb, w2)


def _nsa_consts(S):
    n_cmp = (S - NSA_CMP_LEN) // NSA_CMP_STRIDE + 1
    n_sel = S // NSA_SEL_BLOCK
    cs = np.arange(n_cmp) * NSA_CMP_STRIDE
    ss = np.arange(n_sel) * NSA_SEL_BLOCK
    ov = ((cs[:, None] < ss[None, :] + NSA_SEL_BLOCK) & (cs[:, None] + NSA_CMP_LEN > ss[None, :]))
    ovt = np.zeros((LANES, S // NSA_CMP_STRIDE), np.float32)
    ovt[:n_sel, :n_cmp] = ov.T
    e = np.zeros((LANES, S), np.float32)
    e[np.arange(S) // NSA_SEL_BLOCK, np.arange(S)] = 1.0
    return jnp.asarray(ovt, BF16), jnp.asarray(e, BF16)


def _nsa_kernel(q_ref, kcvc_ref, ksel_ref, vsel_ref, kwin_ref, vwin_ref, gates_ref, ovt_ref, e_ref,
                y_ref, bias_ref, *, tq, ck, n_sel):
    G, H = NSA_GROUPS, NSA_HPG
    GH = G * H
    M = GH * tq
    S = ksel_ref.shape[0]
    W = NSA_WINDOW
    ws = min(W + tq, S)
    t0 = pl.program_id(1) * tq
    gates = jax.nn.sigmoid(gates_ref[...])
    lane = lax.broadcasted_iota(I32, (tq, LANES), 1)
    t_col = t0 + lax.broadcasted_iota(I32, (tq, 1), 0)
    kc = kcvc_ref[:, 0:LANES]
    vc = kcvc_ref[:, LANES:2 * LANES]
    nc = kc.shape[0]
    qs = []
    for g in range(G):
        gmask = (lane // NSA_DK) == g
        for h in range(H):
            qh = q_ref[:, h * LANES:(h + 1) * LANES] * (NSA_DK ** -0.5)
            qs.append(jnp.where(gmask, qh, jnp.zeros_like(qh)))
    Q = jnp.concatenate(qs, axis=0).astype(BF16)

    s = lax.dot_general(Q, kc, _DN_T, preferred_element_type=F32)
    c_idx = lax.broadcasted_iota(I32, (tq, nc), 1)
    cmask = (c_idx * NSA_CMP_STRIDE + NSA_CMP_LEN - 1) <= t_col
    s3 = jnp.where(cmask[None], s.reshape(GH, tq, nc), NEG)
    p = jnp.exp(s3 - jnp.max(s3, axis=-1, keepdims=True))
    p = p / jnp.sum(p, axis=-1, keepdims=True)
    p = jnp.where(cmask[None], p, 0.0)
    o_cmp = jnp.dot(p.reshape(M, nc).astype(BF16), vc, preferred_element_type=F32).reshape(GH, tq, LANES)

    for g in range(G):
        psum = jnp.sum(p[g * H:(g + 1) * H], axis=0)
        hi = psum.astype(BF16)
        lo = (psum - hi.astype(F32)).astype(BF16)
        ovt = ovt_ref[...]
        pslt = (lax.dot_general(ovt, hi, _DN_T, preferred_element_type=F32) +
                lax.dot_general(ovt, lo, _DN_T, preferred_element_type=F32))
        imp_p = pslt[0:n_sel, :]
        n_i = lax.broadcasted_iota(I32, (n_sel, tq), 0)
        cur = (t0 + lax.broadcasted_iota(I32, (n_sel, tq), 1)) // NSA_SEL_BLOCK
        forced = (n_i == 0) | (n_i == cur) | (n_i == cur - 1)
        imp = jnp.where(n_i <= cur, imp_p + jnp.where(forced, FORCE_BONUS, 0.0), NEG)
        cnt = jnp.zeros((n_sel, tq), F32)
        for m in range(n_sel):
            row = imp[m:m + 1, :]
            beats = (row > imp) | ((row == imp) & (n_i > m))
            cnt = cnt + jnp.where(beats, 1.0, 0.0)
        selt = jnp.where(cnt < float(min(NSA_SEL_TOPN, n_sel)), 1.0, 0.0)
        selt = jnp.concatenate([selt, jnp.zeros((LANES - n_sel, tq), F32)], axis=0)
        sel = selt.T.astype(BF16)
        maskf = jnp.dot(sel, e_ref[...], preferred_element_type=F32)
        kpos = lax.broadcasted_iota(I32, (tq, S), 1)
        bias_ref[g] = jnp.where((maskf > 0.5) & (kpos <= t_col), 0.0, NEG)

    def sel_body(j, carry):
        m_i, l_i, acc = carry
        ks = pl.multiple_of(j * ck, ck)
        k = ksel_ref[pl.ds(ks, ck), :]
        v = vsel_ref[pl.ds(ks, ck), :]
        sj = lax.dot_general(Q, k, _DN_T, preferred_element_type=F32)
        sj = (sj.reshape(G, H, tq, ck) + bias_ref[:, :, pl.ds(ks, ck)][:, None]).reshape(GH, tq, ck)
        m_new = jnp.maximum(m_i, jnp.max(sj, axis=-1, keepdims=True))
        a = jnp.exp(m_i - m_new)
        pj = jnp.exp(sj - m_new)
        l_new = a * l_i + jnp.sum(pj, axis=-1, keepdims=True)
        pv = jnp.dot(pj.reshape(M, ck).astype(BF16), v, preferred_element_type=F32)
        return m_new, l_new, a * acc + pv.reshape(GH, tq, LANES)

    init = (jnp.full((GH, tq, 1), NEG, F32), jnp.zeros((GH, tq, 1), F32), jnp.zeros((GH, tq, LANES), F32))
    _, l_f, acc = lax.fori_loop(0, (t0 + tq + ck - 1) // ck, sel_body, init)
    o_sel = acc / l_f

    kst = pl.multiple_of(jnp.clip(t0 - W, 0, S - ws), LANES)
    kwn = kwin_ref[pl.ds(kst, ws), :]
    vwn = vwin_ref[pl.ds(kst, ws), :]
    sw = lax.dot_general(Q, kwn, _DN_T, preferred_element_type=F32)
    wpos = kst + lax.broadcasted_iota(I32, (tq, ws), 1)
    wmask = (wpos <= t_col) & (wpos > t_col - W)
    sw3 = jnp.where(wmask[None], sw.reshape(GH, tq, ws), NEG)
    pw = jnp.exp(sw3 - jnp.max(sw3, axis=-1, keepdims=True))
    lw = jnp.sum(pw, axis=-1, keepdims=True)
    o_win = jnp.dot(pw.reshape(M, ws).astype(BF16), vwn, preferred_element_type=F32).reshape(GH, tq, LANES) / lw

    g0mask = lane < NSA_DK
    for h in range(H):
        o_g = []
        for g in range(G):
            r = g * H + h
            c0 = r * 3
            o_g.append(gates[:, c0:c0 + 1] * o_cmp[r] + gates[:, c0 + 1:c0 + 2] * o_sel[r] +
                       gates[:, c0 + 2:c0 + 3] * o_win[r])
        y_ref[:, h * LANES:(h + 1) * LANES] = jnp.where(g0mask, o_g[0], o_g[1]).astype(y_ref.dtype)


def _nsa(q, kcvc, ksel, vsel, kwin, vwin, gates, B, S, tq, ck):
    T = B * S
    nq = S // tq
    ovt, e = _nsa_consts(S)
    seq = lambda a: a.reshape(B, S, LANES)
    kv_spec = pl.BlockSpec((None, S, LANES), lambda b, i: (b, 0, 0))
    kern = functools.partial(_nsa_kernel, tq=tq, ck=ck, n_sel=S // NSA_SEL_BLOCK)
    return pl.pallas_call(
        kern, grid=(B, nq),
        in_specs=[pl.BlockSpec((tq, 512), lambda b, i: (b * nq + i, 0)),
                  pl.BlockSpec((None,) + kcvc.shape[1:], lambda b, i: (b, 0, 0)),
                  kv_spec, kv_spec, kv_spec, kv_spec,
                  pl.BlockSpec((tq, LANES), lambda b, i: (b * nq + i, 0)),
                  pl.BlockSpec(ovt.shape, lambda b, i: (0, 0)),
                  pl.BlockSpec(e.shape, lambda b, i: (0, 0))],
        out_specs=pl.BlockSpec((tq, 512), lambda b, i: (b * nq + i, 0)),
        out_shape=jax.ShapeDtypeStruct((T, 512), BF16),
        scratch_shapes=[pltpu.VMEM((NSA_GROUPS, tq, S), F32)],
        compiler_params=_params(("parallel", "parallel")), name="nsa",
    )(q, kcvc, seq(ksel), seq(vsel), seq(kwin), seq(vwin), gates, ovt, e)


def _mlstm_kernel(q_ref, k_ref, v_ref, o_ref, gn_ref, gt_ref, cw_ref, bn_ref, bt_ref, ng_ref, tri_ref,
                  y_ref, c_scr, n_scr):
    H, dh, L = ML_HEADS, ML_DH, ML_CHUNK
    S = q_ref.shape[0]
    nchunk = S // L
    c_scr[...] = jnp.zeros_like(c_scr)
    n_scr[...] = jnp.zeros_like(n_scr)
    row = lax.broadcasted_iota(I32, (L, H * dh), 0)
    li = lax.broadcasted_iota(I32, (L, L), 0)
    mi = lax.broadcasted_iota(I32, (L, L), 1)
    causal = mi <= li
    tril = tri_ref[0]
    triu = tri_ref[1]
    hp = lax.Precision.HIGHEST

    def conv_silu(ref, c, wofs):
        r0 = pl.multiple_of(c * L, L)
        rp = pl.multiple_of(jnp.maximum(c - 1, 0) * L, L)
        cur = ref[pl.ds(r0, L), :]
        prev = ref[pl.ds(rp, L), :] * jnp.where(c > 0, 1.0, 0.0)
        acc = cur * cw_ref[ML_CONV - 1:ML_CONV, wofs:wofs + H * dh]
        for j in range(1, ML_CONV):
            sh = jnp.where(row < j, pltpu.roll(prev, j, 0), pltpu.roll(cur, j, 0))
            acc = acc + sh * cw_ref[ML_CONV - 1 - j:ML_CONV - j, wofs:wofs + H * dh]
        return acc * jax.nn.sigmoid(acc)

    def body(c, m_state):
        r0 = pl.multiple_of(c * L, L)
        qa = conv_silu(q_ref, c, 0) * (dh ** -0.5)
        ka = conv_silu(k_ref, c, H * dh)
        va = v_ref[pl.ds(r0, L), :]
        oa = o_ref[pl.ds(r0, L), :]
        gn = gn_ref[pl.ds(r0, L), :] + bn_ref[...]
        gt = gt_ref[:, c, :] + bt_ref[...]
        lf_n = jax.nn.log_sigmoid(gn)
        lf_t = jax.nn.log_sigmoid(gt)
        b_n = jnp.dot(tril, lf_n, precision=hp, preferred_element_type=F32)
        b_t = jnp.dot(lf_t, triu, precision=hp, preferred_element_type=F32)
        new_m = []
        for h in range(H):
            q = qa[:, h * dh:(h + 1) * dh]
            k = ka[:, h * dh:(h + 1) * dh]
            v = va[:, h * dh:(h + 1) * dh]
            m_old = m_state[h]
            b_col = b_n[:, H + h:H + h + 1]
            i_col = gn[:, h:h + 1]
            b_row = b_t[H + h:H + h + 1, :]
            i_row = gt[h:h + 1, :]
            g_tot = b_t[H + h:H + h + 1, L - 1:L]
            d_log = jnp.where(causal, b_col - b_row + i_row, NEG)
            inter = b_col + m_old
            m_q = jnp.maximum(inter, jnp.max(d_log, axis=-1, keepdims=True))
            w_intra = jnp.exp(d_log - m_q)
            w_inter = jnp.exp(inter - m_q)
            qb = q.astype(BF16)
            s = lax.dot_general(qb, k.astype(BF16), _DN_T, preferred_element_type=F32) * w_intra
            cst = c_scr[h]
            nst = n_scr[h]
            num = (w_inter * jnp.dot(qb, cst.astype(BF16), preferred_element_type=F32) +
                   jnp.dot(s.astype(BF16), v, preferred_element_type=F32))
            den = w_inter * jnp.sum(q * nst, axis=-1, keepdims=True) + jnp.sum(s, axis=-1, keepdims=True)
            hv = num / jnp.maximum(jnp.abs(den), jnp.exp(-m_q))
            log_k = g_tot - b_col + i_col
            m_new = jnp.maximum(g_tot + m_old, jnp.max(log_k, axis=0, keepdims=True))
            wk = jnp.exp(log_k - m_new)
            decay = jnp.exp(g_tot + m_old - m_new)
            kw = k * wk
            c_scr[h] = decay * cst + lax.dot_general(kw.astype(BF16), v, _DN_TA, preferred_element_type=F32)
            n_scr[h] = decay * nst + jnp.sum(kw, axis=0, keepdims=True)
            new_m.append(m_new)
            mu = jnp.mean(hv, axis=-1, keepdims=True)
            dv = hv - mu
            var = jnp.mean(dv * dv, axis=-1, keepdims=True)
            hn = dv * lax.rsqrt(var + LN_EPS) * ng_ref[:, h * dh:(h + 1) * dh]
            og = jax.nn.sigmoid(oa[:, h * dh:(h + 1) * dh])
            y_ref[pl.ds(r0, L), h * dh:(h + 1) * dh] = (og * hn).astype(y_ref.dtype)
        return tuple(new_m)

    lax.fori_loop(0, nchunk, body, tuple(jnp.zeros((1, 1), F32) for _ in range(H)))


def _mlstm(mq, mk, mv, mo, mif, conv_w, i_bias, f_bias, norm_g, B, S):
    T = B * S
    H, dh, L = ML_HEADS, ML_DH, ML_CHUNK
    W = H * dh
    gt = mif[:, :2 * H].reshape(B, S, 2 * H).transpose(0, 2, 1).reshape(B, 2 * H, S // L, L)
    cw = conv_w.reshape(ML_CONV, 2 * W)
    bias = jnp.concatenate([i_bias, f_bias])
    bn = jnp.pad(bias, (0, LANES - 2 * H)).reshape(1, LANES)
    bt = bias.reshape(2 * H, 1)
    ng = norm_g.reshape(1, W)
    tri = jnp.stack([jnp.tril(jnp.ones((L, L), F32)), jnp.triu(jnp.ones((L, L), F32))])
    rows = lambda w: pl.BlockSpec((S, w), lambda b: (b, 0))
    full = lambda a: pl.BlockSpec(a.shape, lambda b: (0,) * a.ndim)
    return pl.pallas_call(
        _mlstm_kernel, grid=(B,),
        in_specs=[rows(W), rows(W), rows(W), rows(W), rows(LANES),
                  pl.BlockSpec((None, 2 * H, S // L, L), lambda b: (b, 0, 0, 0)),
                  full(cw), full(bn), full(bt), full(ng), full(tri)],
        out_specs=rows(W),
        out_shape=jax.ShapeDtypeStruct((T, W), BF16),
        scratch_shapes=[pltpu.VMEM((H, dh, dh), F32), pltpu.VMEM((H, 1, dh), F32)],
        compiler_params=_params(("parallel",)), name="mlstm",
    )(mq, mk, mv, mo, mif, gt, cw, bn, bt, ng, tri)


def _outproj_kernel(yn_ref, ym_ref, x_ref, w_ref, g_ref, b_ref, o_ref):
    mix = (jnp.dot(yn_ref[...], w_ref[0:512, :], preferred_element_type=F32) +
           jnp.dot(ym_ref[...], w_ref[512:1024, :], preferred_element_type=F32))
    o_ref[...] = _ln_rows(ALPHA * x_ref[...] + mix, g_ref[...], b_ref[...])


def _outproj(y_nsa, y_ml, x2d, w_out, g, b, tm):
    T = x2d.shape[0]
    wn = w_out[:512].reshape(NSA_GROUPS, NSA_HPG, NSA_DK, D_MODEL).transpose(1, 0, 2, 3).reshape(512, D_MODEL)
    w = jnp.concatenate([wn, w_out[512:]], axis=0).astype(BF16)
    row = lambda wd: pl.BlockSpec((tm, wd), lambda i: (i, 0))
    full = lambda a: pl.BlockSpec(a.shape, lambda i: (0,) * a.ndim)
    g2, b2 = g.reshape(1, -1), b.reshape(1, -1)
    return pl.pallas_call(
        _outproj_kernel, grid=(T // tm,),
        in_specs=[row(512), row(512), row(D_MODEL), full(w), full(g2), full(b2)],
        out_specs=row(D_MODEL), out_shape=jax.ShapeDtypeStruct((T, D_MODEL), F32),
        compiler_params=_params(("parallel",)), name="outproj")(y_nsa, y_ml, x2d, w, g2, b2)


def _memkv_kernel(m_ref, w_ref, o_ref):
    o_ref[...] = jnp.dot(m_ref[...].astype(BF16), w_ref[...], preferred_element_type=F32).astype(o_ref.dtype)


def _memkv(mem2d, wk, wv):
    w = jnp.concatenate([wk, wv], axis=1).astype(BF16)
    R = mem2d.shape[0]
    return pl.pallas_call(
        _memkv_kernel, grid=(R // MEM_LEN,),
        in_specs=[pl.BlockSpec((MEM_LEN, D_MODEL), lambda i: (i, 0)),
                  pl.BlockSpec(w.shape, lambda i: (0, 0))],
        out_specs=pl.BlockSpec((MEM_LEN, 2 * D_MODEL), lambda i: (i, 0)),
        out_shape=jax.ShapeDtypeStruct((R, 2 * D_MODEL), BF16),
        compiler_params=_params(("parallel",)), name="memkv")(mem2d, w)


def _xattn_kernel(x_ref, kv_ref, wq_ref, wo_ref, g_ref, b_ref, rw_ref, x2_ref, x2p_ref, sc_ref):
    x1 = x_ref[...]
    q = jnp.dot(x1.astype(BF16), wq_ref[...], preferred_element_type=F32).astype(BF16)
    outs = []
    for h in range(XA_HEADS):
        qh = q[:, h * XA_DH:(h + 1) * XA_DH]
        kh = kv_ref[:, h * XA_DH:(h + 1) * XA_DH]
        vh = kv_ref[:, D_MODEL + h * XA_DH:D_MODEL + (h + 1) * XA_DH]
        s = lax.dot_general(qh, kh, _DN_T, preferred_element_type=F32) * (XA_DH ** -0.5)
        p = jnp.exp(s - jnp.max(s, axis=-1, keepdims=True))
        p = p / jnp.sum(p, axis=-1, keepdims=True)
        outs.append(jnp.dot(p.astype(BF16), vh, preferred_element_type=F32).astype(BF16))
    o = jnp.concatenate(outs, axis=1)
    xa = jnp.dot(o, wo_ref[...], preferred_element_type=F32)
    x2 = _ln_rows(ALPHA * x1 + xa, g_ref[...], b_ref[...])
    x2_ref[...] = x2
    x2p_ref[...] = _pack_bf16_pairs(x2)
    xh = x2.astype(BF16)
    xl = (x2 - xh.astype(F32)).astype(BF16)
    wh = rw_ref[0]
    wl = rw_ref[1]
    logit = (lax.dot_general(wh, xh, _DN_T, preferred_element_type=F32) +
             lax.dot_general(wh, xl, _DN_T, preferred_element_type=F32) +
             lax.dot_general(wl, xh, _DN_T, preferred_element_type=F32))
    sc_ref[...] = jax.nn.sigmoid(logit)


def _xattn(x1, kv, wq, wo, g, b, router_w, S, tq):
    T = x1.shape[0]
    wqb, wob = wq.astype(BF16), wo.astype(BF16)
    rwt = router_w.T
    rh = rwt.astype(BF16)
    rw = jnp.stack([rh, (rwt - rh.astype(F32)).astype(BF16)])
    g2, b2 = g.reshape(1, -1), b.reshape(1, -1)
    full = lambda a: pl.BlockSpec(a.shape, lambda i: (0,) * a.ndim)
    per = S // tq
    return pl.pallas_call(
        _xattn_kernel, grid=(T // tq,),
        in_specs=[pl.BlockSpec((tq, D_MODEL), lambda i: (i, 0)),
                  pl.BlockSpec((MEM_LEN, 2 * D_MODEL), lambda i: (i // per, 0)),
                  full(wqb), full(wob), full(g2), full(b2), full(rw)],
        out_specs=(pl.BlockSpec((tq, D_MODEL), lambda i: (i, 0)),
                   pl.BlockSpec((tq, D_MODEL // 2), lambda i: (i, 0)),
                   pl.BlockSpec((MOE_E, tq), lambda i: (0, i))),
        out_shape=(jax.ShapeDtypeStruct((T, D_MODEL), F32), jax.ShapeDtypeStruct((T, D_MODEL // 2), jnp.uint32),
                   jax.ShapeDtypeStruct((MOE_E, T), F32)),
        compiler_params=_params(("parallel",)), name="xattn")(x1, kv, wqb, wob, g2, b2, rw)


def _route_kernel(sc_ref, rb_ref, idx_ref, w_ref):
    E, G = MOE_E, MOE_GROUPS
    per = E // G
    scores = sc_ref[...]
    tr = scores.shape[1]
    biased = scores + rb_ref[...]
    g3 = biased.reshape(G, per, tr)
    j3 = lax.broadcasted_iota(I32, (G, per, tr), 1)
    m1 = jnp.max(g3, axis=1, keepdims=True)
    first = jnp.min(jnp.where(g3 == m1, j3, per), axis=1, keepdims=True)
    m2 = jnp.max(jnp.where(j3 == first, -jnp.inf, g3), axis=1, keepdims=True)
    gs = (m1 + m2).reshape(G, tr)
    gi = lax.broadcasted_iota(I32, (G, tr), 0)
    cnt = jnp.zeros((G, tr), F32)
    for m in range(G):
        row = gs[m:m + 1, :]
        cnt = cnt + jnp.where((row > gs) | ((row == gs) & (gi > m)), 1.0, 0.0)
    gmask = cnt < float(MOE_TOPK_GROUPS)
    masked = jnp.where(gmask[:, None, :], g3, NEG).reshape(E, tr)
    ei = lax.broadcasted_iota(I32, (E, tr), 0)
    idxs, ws = [], []
    for _ in range(MOE_K):
        mx = jnp.max(masked, axis=0, keepdims=True)
        ix = jnp.min(jnp.where(masked == mx, ei, E), axis=0, keepdims=True)
        hit = ei == ix
        ws.append(jnp.sum(jnp.where(hit, scores, 0.0), axis=0, keepdims=True))
        idxs.append(ix)
        masked = jnp.where(hit, -jnp.inf, masked)
    w = jnp.concatenate(ws, axis=0)
    idx_ref[...] = jnp.concatenate(idxs, axis=0)
    w_ref[...] = w / jnp.sum(w, axis=0, keepdims=True) * MOE_ROUTE_SCALE


def _route(scores_t, router_bias, tr):
    E, T = scores_t.shape
    rb = router_bias.reshape(E, 1)
    return pl.pallas_call(
        _route_kernel, grid=(T // tr,),
        in_specs=[pl.BlockSpec((E, tr), lambda i: (0, i)), pl.BlockSpec((E, 1), lambda i: (0, 0))],
        out_specs=(pl.BlockSpec((MOE_K, tr), lambda i: (0, i)), pl.BlockSpec((MOE_K, tr), lambda i: (0, i))),
        out_shape=(jax.ShapeDtypeStruct((MOE_K, T), I32), jax.ShapeDtypeStruct((MOE_K, T), F32)),
        compiler_params=_params(("parallel",)), name="route")(scores_t, rb)


def _rank_kernel(idx_ref, u_ref, rank_ref, cnt_ref, carry):
    E = MOE_E

    @pl.when(pl.program_id(0) == 0)
    def _():
        carry[...] = jnp.zeros_like(carry)

    idx = idx_ref[...]
    tp = idx.shape[1]
    ei = lax.broadcasted_iota(I32, (E, tp), 0)
    hits = [ei == idx[k:k + 1, :] for k in range(MOE_K)]
    onehot = jnp.zeros((E, tp), F32)
    for hit in hits:
        onehot = onehot + jnp.where(hit, 1.0, 0.0)
    pos = jnp.dot(onehot.astype(BF16), u_ref[...], preferred_element_type=F32) + carry[...]
    ranks = [jnp.sum(jnp.where(hit, pos, 0.0), axis=0, keepdims=True) for hit in hits]
    rank_ref[...] = jnp.concatenate(ranks, axis=0).astype(I32)
    total = carry[...] + jnp.sum(onehot, axis=1, keepdims=True)
    carry[...] = total
    cnt_ref[...] = jnp.broadcast_to(total, cnt_ref.shape).astype(I32)


def _rank(idx_t, tp):
    K, T = idx_t.shape
    u = jnp.triu(jnp.ones((tp, tp), F32), k=1).astype(BF16)
    rank, cnt = pl.pallas_call(
        _rank_kernel, grid=(T // tp,),
        in_specs=[pl.BlockSpec((K, tp), lambda i: (0, i)), pl.BlockSpec((tp, tp), lambda i: (0, 0))],
        out_specs=(pl.BlockSpec((K, tp), lambda i: (0, i)), pl.BlockSpec((MOE_E, LANES), lambda i: (0, 0))),
        out_shape=(jax.ShapeDtypeStruct((K, T), I32), jax.ShapeDtypeStruct((MOE_E, LANES), I32)),
        scratch_shapes=[pltpu.VMEM((MOE_E, 1), F32)],
        compiler_params=_params(("arbitrary",)), name="rank")(idx_t, u)
    return rank, cnt[:, 0]


def _dest_kernel(idx_ref, rank_ref, po_ref, dest_ref):
    idx = idx_ref[...]
    tp = idx.shape[1]
    ei = lax.broadcasted_iota(I32, (MOE_E, tp), 0)
    po = po_ref[...]
    base = [jnp.sum(jnp.where(ei == idx[k:k + 1, :], po, 0.0), axis=0, keepdims=True) for k in range(MOE_K)]
    dest_ref[...] = jnp.concatenate(base, axis=0).astype(I32) + rank_ref[...]


def _dest(idx_t, rank_t, poffs, tp):
    K, T = idx_t.shape
    po = poffs.astype(F32).reshape(MOE_E, 1)
    spec = pl.BlockSpec((K, tp), lambda i: (0, i))
    return pl.pallas_call(
        _dest_kernel, grid=(T // tp,),
        in_specs=[spec, spec, pl.BlockSpec((MOE_E, 1), lambda i: (0, 0))],
        out_specs=spec, out_shape=jax.ShapeDtypeStruct((K, T), I32),
        compiler_params=_params(("parallel",)), name="dest")(idx_t, rank_t, po)


def _pack_bf16_pairs(v):
    m = v.shape[1] // 2
    bits = lax.bitcast_convert_type(v.astype(BF16).astype(F32), jnp.uint32)
    return (bits[:, :m] >> 16) | (bits[:, m:] & jnp.uint32(0xFFFF0000))


def _unpack_bf16_pairs(w):
    lo = lax.bitcast_convert_type(w << 16, F32)
    hi = lax.bitcast_convert_type(w & jnp.uint32(0xFFFF0000), F32)
    return lo, hi


def _scatter_kernel(dest_ref, x_ref, xs_ref, sem):
    ts = x_ref.shape[0]

    def copy(t, k):
        return pltpu.make_async_copy(x_ref.at[pl.ds(t, 1)], xs_ref.at[pl.ds(dest_ref[k, t], 1)], sem)

    def start(t, c):
        for k in range(MOE_K):
            copy(t, k).start()
        return c

    def wait(t, c):
        for k in range(MOE_K):
            copy(t, k).wait()
        return c

    lax.fori_loop(0, ts, start, 0)
    lax.fori_loop(0, ts, wait, 0)


def _scatter(dest_t, x2p, n_rows, ts):
    T, wp = x2p.shape
    return pl.pallas_call(
        _scatter_kernel, grid=(T // ts,),
        in_specs=[pl.BlockSpec((MOE_K, ts), lambda i: (0, i), memory_space=pltpu.SMEM),
                  pl.BlockSpec((ts, wp), lambda i: (i, 0))],
        out_specs=pl.BlockSpec(memory_space=pl.ANY),
        out_shape=jax.ShapeDtypeStruct((n_rows, wp), jnp.uint32),
        scratch_shapes=[pltpu.SemaphoreType.DMA(())],
        compiler_params=_params(("arbitrary",)), name="scatter")(dest_t, x2p)


def _expert_kernel(be_ref, nv_ref, xs_ref, w1_ref, w3_ref, w2_ref, ys_ref):
    i = pl.program_id(0)
    nv = nv_ref[i]
    hw = D_MODEL // 2

    @pl.when(nv > 0)
    def _():
        row = lax.broadcasted_iota(I32, xs_ref.shape, 0)
        lo, hi = _unpack_bf16_pairs(jnp.where(row < nv, xs_ref[...], jnp.uint32(0)))
        lo, hi = lo.astype(BF16), hi.astype(BF16)
        w1 = w1_ref[...].astype(BF16)
        w3 = w3_ref[...].astype(BF16)
        a = (jnp.dot(lo, w1[:hw], preferred_element_type=F32) + jnp.dot(hi, w1[hw:], preferred_element_type=F32))
        c = (jnp.dot(lo, w3[:hw], preferred_element_type=F32) + jnp.dot(hi, w3[hw:], preferred_element_type=F32))
        h = (a * jax.nn.sigmoid(a) * c).astype(BF16)
        ys_ref[...] = _pack_bf16_pairs(jnp.dot(h, w2_ref[...].astype(BF16), preferred_element_type=F32))

    @pl.when(nv <= 0)
    def _():
        ys_ref[...] = jnp.zeros_like(ys_ref)


def _experts(blk_exp, blk_valid, xs, w1, w3, w2):
    n_rows, wp = xs.shape
    rb = ROW_BLOCK
    gs = pltpu.PrefetchScalarGridSpec(
        num_scalar_prefetch=2, grid=(n_rows // rb,),
        in_specs=[pl.BlockSpec((rb, wp), lambda i, be, nv: (i, 0)),
                  pl.BlockSpec((None, D_MODEL, MOE_FF), lambda i, be, nv: (be[i], 0, 0)),
                  pl.BlockSpec((None, D_MODEL, MOE_FF), lambda i, be, nv: (be[i], 0, 0)),
                  pl.BlockSpec((None, MOE_FF, D_MODEL), lambda i, be, nv: (be[i], 0, 0))],
        out_specs=pl.BlockSpec((rb, wp), lambda i, be, nv: (i, 0)))
    return pl.pallas_call(
        _expert_kernel, grid_spec=gs, out_shape=jax.ShapeDtypeStruct((n_rows, wp), jnp.uint32),
        compiler_params=_params(("arbitrary",)), name="experts")(blk_exp, blk_valid, xs, w1, w3, w2)


def _combine_kernel(dest_ref, w_ref, x_ref, ys_ref, s1_ref, s3_ref, s2_ref, g_ref, b_ref, o_ref, buf, sem):
    tc = x_ref.shape[0]

    def copy(t, k):
        return pltpu.make_async_copy(ys_ref.at[pl.ds(dest_ref[k, t], 1)], buf.at[k, pl.ds(t, 1)], sem)

    def start(t, c):
        for k in range(MOE_K):
            copy(t, k).start()
        return c

    def wait(t, c):
        for k in range(MOE_K):
            copy(t, k).wait()
        return c

    lax.fori_loop(0, tc, start, 0)
    x2 = x_ref[...]
    xb = x2.astype(BF16)
    a = jnp.dot(xb, s1_ref[...], preferred_element_type=F32)
    c = jnp.dot(xb, s3_ref[...], preferred_element_type=F32)
    shared = jnp.dot((a * jax.nn.sigmoid(a) * c).astype(BF16), s2_ref[...], preferred_element_type=F32)
    lax.fori_loop(0, tc, wait, 0)
    w = w_ref[...]
    y_lo = jnp.zeros((tc, D_MODEL // 2), F32)
    y_hi = jnp.zeros((tc, D_MODEL // 2), F32)
    for k in range(MOE_K):
        lo, hi = _unpack_bf16_pairs(buf[k])
        y_lo = y_lo + w[:, k:k + 1] * lo
        y_hi = y_hi + w[:, k:k + 1] * hi
    y = shared + jnp.concatenate([y_lo, y_hi], axis=1)
    o_ref[...] = _ln_rows(ALPHA * x2 + y, g_ref[...], b_ref[...])


def _combine(dest_t, w_nat, x2, ys, sw1, sw3, sw2, g, b, tc):
    T = x2.shape[0]
    wp = ys.shape[1]
    full = lambda a: pl.BlockSpec(a.shape, lambda i: (0,) * a.ndim)
    s1, s3, s2 = sw1.astype(BF16), sw3.astype(BF16), sw2.astype(BF16)
    g2, b2 = g.reshape(1, -1), b.reshape(1, -1)
    return pl.pallas_call(
        _combine_kernel, grid=(T // tc,),
        in_specs=[pl.BlockSpec((MOE_K, tc), lambda i: (0, i), memory_space=pltpu.SMEM),
                  pl.BlockSpec((tc, MOE_K), lambda i: (i, 0)),
                  pl.BlockSpec((tc, D_MODEL), lambda i: (i, 0)),
                  pl.BlockSpec(memory_space=pl.ANY), full(s1), full(s3), full(s2), full(g2), full(b2)],
        out_specs=pl.BlockSpec((tc, D_MODEL), lambda i: (i, 0)),
        out_shape=jax.ShapeDtypeStruct((T, D_MODEL), F32),
        scratch_shapes=[pltpu.VMEM((MOE_K, tc, wp), jnp.uint32), pltpu.SemaphoreType.DMA(())],
        compiler_params=_params(("arbitrary",)), name="combine",
    )(dest_t, w_nat, x2, ys, s1, s3, s2, g2, b2)


def _moe(x2, x2p, scores_t, router_bias, w1, w3, w2, sw1, sw3, sw2, g, b, tiles):
    T = x2.shape[0]
    rb = ROW_BLOCK
    idx_t, w_t = _route(scores_t, router_bias, tiles['route'])
    rank_t, counts = _rank(idx_t, tiles['rank'])
    pcounts = (counts + rb - 1) // rb * rb
    pends = jnp.cumsum(pcounts)
    poffs = (pends - pcounts).astype(I32)
    n_blocks = (T * MOE_K + MOE_E * (rb - 1) + rb - 1) // rb
    blk_start = jnp.arange(n_blocks, dtype=I32) * rb
    blk_exp = jnp.minimum(jnp.searchsorted(pends, blk_start, side='right'), MOE_E - 1).astype(I32)
    blk_valid = jnp.clip(counts[blk_exp] - (blk_start - poffs[blk_exp]), 0, rb).astype(I32)
    dest_t = _dest(idx_t, rank_t, poffs, tiles['rank'])
    xs = _scatter(dest_t, x2p, n_blocks * rb, tiles['scatter'])
    ys = _experts(blk_exp, blk_valid, xs, w1, w3, w2)
    return _combine(dest_t, w_t.T, x2, ys, sw1, sw3, sw2, g, b, tiles['combine'])


def _tiles(B, S):
    T = B * S
    pick = lambda want, n: want if n % want == 0 else n
    return dict(proj=pick(512, T), nsa_q=pick(128, S), nsa_ck=pick(256, S), outproj=pick(512, T),
                xattn=pick(256, S), route=pick(512, T), rank=pick(512, T), scatter=pick(256, T),
                combine=pick(128, T))


def kernel(x, mem, w_in, nsa_pos_k, nsa_cmp_k_w1, nsa_cmp_k_w2, nsa_pos_v, nsa_cmp_v_w1, nsa_cmp_v_w2,
           mlstm_conv_w, mlstm_i_bias, mlstm_f_bias, mlstm_norm_g, w_out, ln1_g, ln1_b,
           xa_wq, xa_wk, xa_wv, xa_wo, ln2_g, ln2_b, router_w, router_bias,
           moe_w1, moe_w3, moe_w2, shared_w1, shared_w3, shared_w2, ln3_g, ln3_b):
    B, S, D = x.shape
    T = B * S
    tl = _tiles(B, S)
    xc = x.reshape(T, D)
    memc = mem.reshape(B * MEM_LEN, D)
    for l in range(w_in.shape[0]):
        (q, cmp, ksel, vsel, kwin, vwin, gates, mq, mk, mv, mo, mif) = _project(xc, _prep_w_in(w_in[l]), tl['proj'])
        kcvc = _compress(cmp, B, S, _prep_cmp(nsa_pos_k[l], nsa_cmp_k_w1[l], nsa_cmp_k_w2[l],
                                              nsa_pos_v[l], nsa_cmp_v_w1[l], nsa_cmp_v_w2[l]))
        y_nsa = _nsa(q, kcvc, ksel, vsel, kwin, vwin, gates, B, S, tl['nsa_q'], tl['nsa_ck'])
        y_ml = _mlstm(mq, mk, mv, mo, mif, mlstm_conv_w[l], mlstm_i_bias[l], mlstm_f_bias[l],
                      mlstm_norm_g[l], B, S)
        x1 = _outproj(y_nsa, y_ml, xc, w_out[l], ln1_g[l], ln1_b[l], tl['outproj'])
        kv = _memkv(memc, xa_wk[l], xa_wv[l])
        x2, x2p, scores_t = _xattn(x1, kv, xa_wq[l], xa_wo[l], ln2_g[l], ln2_b[l], router_w[l], S, tl['xattn'])
        xc = _moe(x2, x2p, scores_t, router_bias[l], moe_w1[l], moe_w3[l], moe_w2[l],
                  shared_w1[l], shared_w3[l], shared_w2[l], ln3_g[l], ln3_b[l], tl)
    return xc.reshape(B, S, D)
```

```python
import functools
import numpy as np
import jax
import jax.numpy as jnp
from jax import lax
from jax.experimental import pallas as pl
from jax.experimental.pallas import tpu as pltpu

F32 = jnp.float32
BF16 = jnp.bfloat16
I32 = jnp.int32

D_MODEL = 1024
MEM_LEN = 256
NSA_HEADS = 8
NSA_GROUPS = 2
NSA_HPG = 4
NSA_DK = 64
NSA_CMP_LEN = 32
NSA_CMP_STRIDE = 16
NSA_SEL_BLOCK = 64
NSA_SEL_TOPN = 8
NSA_WINDOW = 512
ML_HEADS = 4
ML_DH = 128
ML_CHUNK = 64
ML_CONV = 4
XA_HEADS = 4
XA_DH = 256
MOE_E = 256
MOE_K = 8
MOE_GROUPS = 8
MOE_TOPK_GROUPS = 4
MOE_FF = 256
MOE_ROUTE_SCALE = 2.5
DEPTH = 1
ALPHA = (2.0 * DEPTH) ** 0.25
LN_EPS = 1e-5
NEG = -1e30
FORCE_BONUS = 1e4

LANES = 128
ROW_BLOCK = 512
VMEM_LIMIT = 56 * 1024 * 1024

_DN_T = (((1,), (1,)), ((), ()))
_DN_TA = (((0,), (0,)), ((), ()))


def _params(sem):
    return pltpu.CompilerParams(dimension_semantics=sem, vmem_limit_bytes=VMEM_LIMIT)


def _ln_rows(v, g, b):
    mu = jnp.mean(v, axis=-1, keepdims=True)
    d = v - mu
    var = jnp.mean(d * d, axis=-1, keepdims=True)
    return d * lax.rsqrt(var + LN_EPS) * g + b


_SEGS = (('q', 512, BF16), ('cmp', 256, F32), ('ksel', 128, BF16), ('vsel', 128, BF16),
         ('kwin', 128, BF16), ('vwin', 128, BF16), ('gates', 128, F32), ('mq', 512, BF16),
         ('mk', 512, BF16), ('mv', 512, BF16), ('mo', 512, BF16), ('mif', 128, F32))


def _proj_kernel(x_ref, w_ref, *out_refs):
    xb = x_ref[...].astype(BF16)
    off = 0
    for o_ref, (_, wd, _) in zip(out_refs, _SEGS):
        o_ref[...] = jnp.dot(xb, w_ref[:, off:off + wd], preferred_element_type=F32).astype(o_ref.dtype)
        off += wd


def _prep_w_in(w):
    sizes = (512,) + (128,) * 6 + (24,) + (512,) * 4 + (4, 4)
    cuts = np.cumsum(sizes)[:-1].tolist()
    (wq, kc, vc, ks, vs, kw, vw, wg, mq, mk, mv, mo, mi, mf) = jnp.split(w, cuts, axis=1)
    wq = wq.reshape(D_MODEL, NSA_GROUPS, NSA_HPG, NSA_DK).transpose(0, 2, 1, 3).reshape(D_MODEL, 512)
    pad = lambda a: jnp.pad(a, ((0, 0), (0, LANES - a.shape[1])))
    segs = [wq, kc, vc, ks, vs, kw, vw, pad(wg), mq, mk, mv, mo, pad(jnp.concatenate([mi, mf], axis=1))]
    return jnp.concatenate(segs, axis=1).astype(BF16)


def _project(x2d, w_all, tm):
    T = x2d.shape[0]
    n = w_all.shape[1]
    out_shape = tuple(jax.ShapeDtypeStruct((T, wd), dt) for _, wd, dt in _SEGS)
    out_specs = tuple(pl.BlockSpec((tm, wd), lambda i: (i, 0)) for _, wd, _ in _SEGS)
    return pl.pallas_call(
        _proj_kernel, grid=(T // tm,),
        in_specs=[pl.BlockSpec((tm, D_MODEL), lambda i: (i, 0)),
                  pl.BlockSpec((D_MODEL, n), lambda i: (0, 0))],
        out_specs=out_specs, out_shape=out_shape,
        compiler_params=_params(("parallel",)), name="proj")(x2d, w_all)


def _cmp_kernel(r_ref, pa_ref, pb_ref, wa_ref, wb_ref, w2_ref, o_ref):
    r = r_ref[...]
    a = jnp.dot((r + pa_ref[...]).astype(BF16), wa_ref[...], preferred_element_type=F32)
    b = jnp.dot((r + pb_ref[...]).astype(BF16), wb_ref[...], preferred_element_type=F32)
    nr = r.shape[0]
    hid = a + pltpu.roll(b, nr - 1, 0)
    hid = hid * jax.nn.sigmoid(hid)
    out = jnp.dot(hid.astype(BF16), w2_ref[...], preferred_element_type=F32)
    row = lax.broadcasted_iota(I32, out.shape, 0)
    o_ref[...] = jnp.where(row < nr - 1, out, 0.0).astype(o_ref.dtype)


def _prep_cmp(pos_k, w1_k, w2_k, pos_v, w1_v, w2_v):
    eye = jnp.eye(NSA_GROUPS, dtype=F32)

    def expand_w1(w1, half):
        w = w1.reshape(NSA_CMP_LEN, NSA_DK, NSA_DK)[half * 16:(half + 1) * 16]
        return jnp.einsum('jde,gk->jgdke', w, eye).reshape(16, 128, 128)

    def both(fk, fv):
        z = jnp.zeros_like(fk)
        top = jnp.concatenate([fk, z], axis=-1)
        bot = jnp.concatenate([z, fv], axis=-1)
        return jnp.concatenate([top, bot], axis=-2)

    wa = both(expand_w1(w1_k, 0), expand_w1(w1_v, 0)).reshape(16 * 256, 256).astype(BF16)
    wb = both(expand_w1(w1_k, 1), expand_w1(w1_v, 1)).reshape(16 * 256, 256).astype(BF16)
    w2 = both(jnp.kron(eye, w2_k), jnp.kron(eye, w2_v)).astype(BF16)

    def pos_row(half):
        pk = jnp.tile(pos_k[half * 16:(half + 1) * 16], (1, NSA_GROUPS))
        pv = jnp.tile(pos_v[half * 16:(half + 1) * 16], (1, NSA_GROUPS))
        return jnp.concatenate([pk, pv], axis=1).reshape(1, 16 * 256)

    return pos_row(0), pos_row(1), wa, wb, w2


def _compress(cmp2d, B, S, prep):
    pa, pb, wa, wb, w2 = prep
    nr = S // NSA_CMP_STRIDE
    r = cmp2d.reshape(B, nr, NSA_CMP_STRIDE * 256)
    full = lambda a: pl.BlockSpec(a.shape, lambda b: (0,) * a.ndim)
    return pl.pallas_call(
        _cmp_kernel, grid=(B,),
        in_specs=[pl.BlockSpec((None, nr, NSA_CMP_STRIDE * 256), lambda b: (b, 0, 0)),
                  full(pa), full(pb), full(wa), full(wb), full(w2)],
        out_specs=pl.BlockSpec((None, nr, 256), lambda b: (b, 0, 0)),
        out_shape=jax.ShapeDtypeStruct((B, nr, 256), BF16),
        compiler_params=_params(("parallel",)), name="cmp")(r, pa, pb, wa, wb, w2)


def _nsa_consts(S):
    n_cmp = (S - NSA_CMP_LEN) // NSA_CMP_STRIDE + 1
    n_sel = S // NSA_SEL_BLOCK
    cs = np.arange(n_cmp) * NSA_CMP_STRIDE
    ss = np.arange(n_sel) * NSA_SEL_BLOCK
    ov = ((cs[:, None] < ss[None, :] + NSA_SEL_BLOCK) & (cs[:, None] + NSA_CMP_LEN > ss[None, :]))
    ovt = np.zeros((LANES, S // NSA_CMP_STRIDE), np.float32)
    ovt[:n_sel, :n_cmp] = ov.T
    e = np.zeros((LANES, S), np.float32)
    e[np.arange(S) // NSA_SEL_BLOCK, np.arange(S)] = 1.0
    return jnp.asarray(ovt, BF16), jnp.asarray(e, BF16)


def _nsa_kernel(q_ref, kcvc_ref, ksel_ref, vsel_ref, kwin_ref, vwin_ref, gates_ref, ovt_ref, e_ref,
                y_ref, bias_ref, *, tq, ck, n_sel):
    G, H = NSA_GROUPS, NSA_HPG
    GH = G * H
    M = GH * tq
    S = ksel_ref.shape[0]
    W = NSA_WINDOW
    ws = min(W + tq, S)
    t0 = pl.program_id(1) * tq
    gates = jax.nn.sigmoid(gates_ref[...])
    lane = lax.broadcasted_iota(I32, (tq, LANES), 1)
    t_col = t0 + lax.broadcasted_iota(I32, (tq, 1), 0)
    kc = kcvc_ref[:, 0:LANES]
    vc = kcvc_ref[:, LANES:2 * LANES]
    nc = kc.shape[0]
    qs = []
    for g in range(G):
        gmask = (lane // NSA_DK) == g
        for h in range(H):
            qh = q_ref[:, h * LANES:(h + 1) * LANES] * (NSA_DK ** -0.5)
            qs.append(jnp.where(gmask, qh, jnp.zeros_like(qh)))
    Q = jnp.concatenate(qs, axis=0).astype(BF16)

    s = lax.dot_general(Q, kc, _DN_T, preferred_element_type=F32)
    c_idx = lax.broadcasted_iota(I32, (tq, nc), 1)
    cmask = (c_idx * NSA_CMP_STRIDE + NSA_CMP_LEN - 1) <= t_col
    s3 = jnp.where(cmask[None], s.reshape(GH, tq, nc), NEG)
    p = jnp.exp(s3 - jnp.max(s3, axis=-1, keepdims=True))
    p = p / jnp.sum(p, axis=-1, keepdims=True)
    p = jnp.where(cmask[None], p, 0.0)
    o_cmp = jnp.dot(p.reshape(M, nc).astype(BF16), vc, preferred_element_type=F32).reshape(GH, tq, LANES)

    for g in range(G):
        psum = jnp.sum(p[g * H:(g + 1) * H], axis=0)
        hi = psum.astype(BF16)
        lo = (psum - hi.astype(F32)).astype(BF16)
        ovt = ovt_ref[...]
        pslt = (lax.dot_general(ovt, hi, _DN_T, preferred_element_type=F32) +
                lax.dot_general(ovt, lo, _DN_T, preferred_element_type=F32))
        imp_p = pslt[0:n_sel, :]
        n_i = lax.broadcasted_iota(I32, (n_sel, tq), 0)
        cur = (t0 + lax.broadcasted_iota(I32, (n_sel, tq), 1)) // NSA_SEL_BLOCK
        forced = (n_i == 0) | (n_i == cur) | (n_i == cur - 1)
        imp = jnp.where(n_i <= cur, imp_p + jnp.where(forced, FORCE_BONUS, 0.0), NEG)
        cnt = jnp.zeros((n_sel, tq), F32)
        for m in range(n_sel):
            row = imp[m:m + 1, :]
            beats = (row > imp) | ((row == imp) & (n_i > m))
            cnt = cnt + jnp.where(beats, 1.0, 0.0)
        selt = jnp.where(cnt < float(min(NSA_SEL_TOPN, n_sel)), 1.0, 0.0)
        selt = jnp.concatenate([selt, jnp.zeros((LANES - n_sel, tq), F32)], axis=0)
        sel = selt.T.astype(BF16)
        maskf = jnp.dot(sel, e_ref[...], preferred_element_type=F32)
        kpos = lax.broadcasted_iota(I32, (tq, S), 1)
        bias_ref[g] = jnp.where((maskf > 0.5) & (kpos <= t_col), 0.0, NEG)

    def sel_body(j, carry):
        m_i, l_i, acc = carry
        ks = pl.multiple_of(j * ck, ck)
        k = ksel_ref[pl.ds(ks, ck), :]
        v = vsel_ref[pl.ds(ks, ck), :]
        sj = lax.dot_general(Q, k, _DN_T, preferred_element_type=F32)
        sj = (sj.reshape(G, H, tq, ck) + bias_ref[:, :, pl.ds(ks, ck)][:, None]).reshape(GH, tq, ck)
        m_new = jnp.maximum(m_i, jnp.max(sj, axis=-1, keepdims=True))
        a = jnp.exp(m_i - m_new)
        pj = jnp.exp(sj - m_new)
        l_new = a * l_i + jnp.sum(pj, axis=-1, keepdims=True)
        pv = jnp.dot(pj.reshape(M, ck).astype(BF16), v, preferred_element_type=F32)
        return m_new, l_new, a * acc + pv.reshape(GH, tq, LANES)

    init = (jnp.full((GH, tq, 1), NEG, F32), jnp.zeros((GH, tq, 1), F32), jnp.zeros((GH, tq, LANES), F32))
    _, l_f, acc = lax.fori_loop(0, (t0 + tq + ck - 1) // ck, sel_body, init)
    o_sel = acc / l_f

    kst = pl.multiple_of(jnp.clip(t0 - W, 0, S - ws), LANES)
    kwn = kwin_ref[pl.ds(kst, ws), :]
    vwn = vwin_ref[pl.ds(kst, ws), :]
    sw = lax.dot_general(Q, kwn, _DN_T, preferred_element_type=F32)
    wpos = kst + lax.broadcasted_iota(I32, (tq, ws), 1)
    wmask = (wpos <= t_col) & (wpos > t_col - W)
    sw3 = jnp.where(wmask[None], sw.reshape(GH, tq, ws), NEG)
    pw = jnp.exp(sw3 - jnp.max(sw3, axis=-1, keepdims=True))
    lw = jnp.sum(pw, axis=-1, keepdims=True)
    o_win = jnp.dot(pw.reshape(M, ws).astype(BF16), vwn, preferred_element_type=F32).reshape(GH, tq, LANES) / lw

    g0mask = lane < NSA_DK
    for h in range(H):
        o_g = []
        for g in range(G):
            r = g * H + h
            c0 = r * 3
            o_g.append(gates[:, c0:c0 + 1] * o_cmp[r] + gates[:, c0 + 1:c0 + 2] * o_sel[r] +
                       gates[:, c0 + 2:c0 + 3] * o_win[r])
        y_ref[:, h * LANES:(h + 1) * LANES] = jnp.where(g0mask, o_g[0], o_g[1]).astype(y_ref.dtype)


def _nsa(q, kcvc, ksel, vsel, kwin, vwin, gates, B, S, tq, ck):
    T = B * S
    nq = S // tq
    ovt, e = _nsa_consts(S)
    seq = lambda a: a.reshape(B, S, LANES)
    kv_spec = pl.BlockSpec((None, S, LANES), lambda b, i: (b, 0, 0))
    kern = functools.partial(_nsa_kernel, tq=tq, ck=ck, n_sel=S // NSA_SEL_BLOCK)
    return pl.pallas_call(
        kern, grid=(B, nq),
        in_specs=[pl.BlockSpec((tq, 512), lambda b, i: (b * nq + i, 0)),
                  pl.BlockSpec((None,) + kcvc.shape[1:], lambda b, i: (b, 0, 0)),
                  kv_spec, kv_spec, kv_spec, kv_spec,
                  pl.BlockSpec((tq, LANES), lambda b, i: (b * nq + i, 0)),
                  pl.BlockSpec(ovt.shape, lambda b, i: (0, 0)),
                  pl.BlockSpec(e.shape, lambda b, i: (0, 0))],
        out_specs=pl.BlockSpec((tq, 512), lambda b, i: (b * nq + i, 0)),
        out_shape=jax.ShapeDtypeStruct((T, 512), BF16),
        scratch_shapes=[pltpu.VMEM((NSA_GROUPS, tq, S), F32)],
        compiler_params=_params(("parallel", "parallel")), name="nsa",
    )(q, kcvc, seq(ksel), seq(vsel), seq(kwin), seq(vwin), gates, ovt, e)


def _mlstm_kernel(q_ref, k_ref, v_ref, o_ref, gn_ref, gt_ref, cw_ref, bn_ref, bt_ref, ng_ref, tri_ref,
                  y_ref, c_scr, n_scr):
    H, dh, L = ML_HEADS, ML_DH, ML_CHUNK
    nb, S = q_ref.shape[0], q_ref.shape[1]
    nchunk = S // L
    c_scr[...] = jnp.zeros_like(c_scr)
    n_scr[...] = jnp.zeros_like(n_scr)
    row = lax.broadcasted_iota(I32, (L, H * dh), 0)
    li = lax.broadcasted_iota(I32, (L, L), 0)
    mi = lax.broadcasted_iota(I32, (L, L), 1)
    causal = mi <= li
    tril = tri_ref[0]
    triu = tri_ref[1]
    hp = lax.Precision.HIGHEST

    def conv_silu(ref, bi, c, wofs):
        r0 = pl.multiple_of(c * L, L)
        rp = pl.multiple_of(jnp.maximum(c - 1, 0) * L, L)
        cur = ref[bi, pl.ds(r0, L), :].astype(F32)
        prev = ref[bi, pl.ds(rp, L), :].astype(F32) * jnp.where(c > 0, 1.0, 0.0)
        acc = cur * cw_ref[ML_CONV - 1:ML_CONV, wofs:wofs + H * dh]
        for j in range(1, ML_CONV):
            sh = jnp.where(row < j, pltpu.roll(prev, j, 0), pltpu.roll(cur, j, 0))
            acc = acc + sh * cw_ref[ML_CONV - 1 - j:ML_CONV - j, wofs:wofs + H * dh]
        return acc * jax.nn.sigmoid(acc)

    def body(c, m_state):
        r0 = pl.multiple_of(c * L, L)
        new_m = []
        for bi in range(nb):
            qa = conv_silu(q_ref, bi, c, 0) * (dh ** -0.5)
            ka = conv_silu(k_ref, bi, c, H * dh)
            va = v_ref[bi, pl.ds(r0, L), :]
            oa = o_ref[bi, pl.ds(r0, L), :].astype(F32)
            gn = gn_ref[bi, pl.ds(r0, L), :] + bn_ref[...]
            gt = gt_ref[bi, :, c, :] + bt_ref[...]
            lf_n = jax.nn.log_sigmoid(gn)
            lf_t = jax.nn.log_sigmoid(gt)
            b_n = jnp.dot(tril, lf_n, precision=hp, preferred_element_type=F32)
            b_t = jnp.dot(lf_t, triu, precision=hp, preferred_element_type=F32)
            for h in range(H):
                st = bi * H + h
                q = qa[:, h * dh:(h + 1) * dh]
                k = ka[:, h * dh:(h + 1) * dh]
                v = va[:, h * dh:(h + 1) * dh]
                m_old = m_state[st]
                b_col = b_n[:, H + h:H + h + 1]
                i_col = gn[:, h:h + 1]
                b_row = b_t[H + h:H + h + 1, :]
                i_row = gt[h:h + 1, :]
                g_tot = b_t[H + h:H + h + 1, L - 1:L]
                d_log = jnp.where(causal, b_col - b_row + i_row, NEG)
                inter = b_col + m_old
                m_q = jnp.maximum(inter, jnp.max(d_log, axis=-1, keepdims=True))
                w_intra = jnp.exp(d_log - m_q)
                w_inter = jnp.exp(inter - m_q)
                qb = q.astype(BF16)
                s = lax.dot_general(qb, k.astype(BF16), _DN_T, preferred_element_type=F32) * w_intra
                cst = c_scr[st]
                nst = n_scr[st]
                num = (w_inter * jnp.dot(qb, cst.astype(BF16), preferred_element_type=F32) +
                       jnp.dot(s.astype(BF16), v, preferred_element_type=F32))
                den = w_inter * jnp.sum(q * nst, axis=-1, keepdims=True) + jnp.sum(s, axis=-1, keepdims=True)
                hv = num / jnp.maximum(jnp.abs(den), jnp.exp(-m_q))
                log_k = g_tot - b_col + i_col
                m_new = jnp.maximum(g_tot + m_old, jnp.max(log_k, axis=0, keepdims=True))
                wk = jnp.exp(log_k - m_new)
                decay = jnp.exp(g_tot + m_old - m_new)
                kw = k * wk
                c_scr[st] = decay * cst + lax.dot_general(kw.astype(BF16), v, _DN_TA, preferred_element_type=F32)
                n_scr[st] = decay * nst + jnp.sum(kw, axis=0, keepdims=True)
                new_m.append(m_new)
                mu = jnp.mean(hv, axis=-1, keepdims=True)
                dv = hv - mu
                var = jnp.mean(dv * dv, axis=-1, keepdims=True)
                hn = dv * lax.rsqrt(var + LN_EPS) * ng_ref[:, h * dh:(h + 1) * dh]
                og = jax.nn.sigmoid(oa[:, h * dh:(h + 1) * dh])
                y_ref[bi, pl.ds(r0, L), h * dh:(h + 1) * dh] = (og * hn).astype(y_ref.dtype)
        return tuple(new_m)

    lax.fori_loop(0, nchunk, body, tuple(jnp.zeros((1, 1), F32) for _ in range(nb * H)))


def _mlstm(mq, mk, mv, mo, mif, conv_w, i_bias, f_bias, norm_g, B, S, nb):
    T = B * S
    H, dh, L = ML_HEADS, ML_DH, ML_CHUNK
    W = H * dh
    gt = mif[:, :2 * H].reshape(B, S, 2 * H).transpose(0, 2, 1).reshape(B, 2 * H, S // L, L)
    cw = conv_w.reshape(ML_CONV, 2 * W)
    bias = jnp.concatenate([i_bias, f_bias])
    bn = jnp.pad(bias, (0, LANES - 2 * H)).reshape(1, LANES)
    bt = bias.reshape(2 * H, 1)
    ng = norm_g.reshape(1, W)
    tri = jnp.stack([jnp.tril(jnp.ones((L, L), F32)), jnp.triu(jnp.ones((L, L), F32))])
    seq = lambda a: a.reshape(B, S, a.shape[1])
    rows = lambda w: pl.BlockSpec((nb, S, w), lambda b: (b, 0, 0))
    full = lambda a: pl.BlockSpec(a.shape, lambda b: (0,) * a.ndim)
    y = pl.pallas_call(
        _mlstm_kernel, grid=(B // nb,),
        in_specs=[rows(W), rows(W), rows(W), rows(W), rows(LANES),
                  pl.BlockSpec((nb, 2 * H, S // L, L), lambda b: (b, 0, 0, 0)),
                  full(cw), full(bn), full(bt), full(ng), full(tri)],
        out_specs=rows(W),
        out_shape=jax.ShapeDtypeStruct((B, S, W), BF16),
        scratch_shapes=[pltpu.VMEM((nb * H, dh, dh), F32), pltpu.VMEM((nb * H, 1, dh), F32)],
        compiler_params=_params(("parallel",)), name="mlstm",
    )(seq(mq), seq(mk), seq(mv), seq(mo), seq(mif), gt, cw, bn, bt, ng, tri)
    return y.reshape(T, W)


def _outproj_kernel(yn_ref, ym_ref, x_ref, w_ref, g_ref, b_ref, o_ref):
    mix = (jnp.dot(yn_ref[...], w_ref[0:512, :], preferred_element_type=F32) +
           jnp.dot(ym_ref[...], w_ref[512:1024, :], preferred_element_type=F32))
    o_ref[...] = _ln_rows(ALPHA * x_ref[...] + mix, g_ref[...], b_ref[...])


def _outproj(y_nsa, y_ml, x2d, w_out, g, b, tm):
    T = x2d.shape[0]
    wn = w_out[:512].reshape(NSA_GROUPS, NSA_HPG, NSA_DK, D_MODEL).transpose(1, 0, 2, 3).reshape(512, D_MODEL)
    w = jnp.concatenate([wn, w_out[512:]], axis=0).astype(BF16)
    row = lambda wd: pl.BlockSpec((tm, wd), lambda i: (i, 0))
    full = lambda a: pl.BlockSpec(a.shape, lambda i: (0,) * a.ndim)
    g2, b2 = g.reshape(1, -1), b.reshape(1, -1)
    return pl.pallas_call(
        _outproj_kernel, grid=(T // tm,),
        in_specs=[row(512), row(512), row(D_MODEL), full(w), full(g2), full(b2)],
        out_specs=row(D_MODEL), out_shape=jax.ShapeDtypeStruct((T, D_MODEL), F32),
        compiler_params=_params(("parallel",)), name="outproj")(y_nsa, y_ml, x2d, w, g2, b2)


def _memkv_kernel(m_ref, w_ref, o_ref):
    o_ref[...] = jnp.dot(m_ref[...].astype(BF16), w_ref[...], preferred_element_type=F32).astype(o_ref.dtype)


def _memkv(mem2d, wk, wv):
    w = jnp.concatenate([wk, wv], axis=1).astype(BF16)
    R = mem2d.shape[0]
    return pl.pallas_call(
        _memkv_kernel, grid=(R // MEM_LEN,),
        in_specs=[pl.BlockSpec((MEM_LEN, D_MODEL), lambda i: (i, 0)),
                  pl.BlockSpec(w.shape, lambda i: (0, 0))],
        out_specs=pl.BlockSpec((MEM_LEN, 2 * D_MODEL), lambda i: (i, 0)),
        out_shape=jax.ShapeDtypeStruct((R, 2 * D_MODEL), BF16),
        compiler_params=_params(("parallel",)), name="memkv")(mem2d, w)


def _xattn_kernel(x_ref, kv_ref, wq_ref, wo_ref, g_ref, b_ref, rw_ref, x2_ref, x2p_ref, sc_ref):
    x1 = x_ref[...]
    q = jnp.dot(x1.astype(BF16), wq_ref[...], preferred_element_type=F32).astype(BF16)
    outs = []
    for h in range(XA_HEADS):
        qh = q[:, h * XA_DH:(h + 1) * XA_DH]
        kh = kv_ref[:, h * XA_DH:(h + 1) * XA_DH]
        vh = kv_ref[:, D_MODEL + h * XA_DH:D_MODEL + (h + 1) * XA_DH]
        s = lax.dot_general(qh, kh, _DN_T, preferred_element_type=F32) * (XA_DH ** -0.5)
        p = jnp.exp(s - jnp.max(s, axis=-1, keepdims=True))
        p = p / jnp.sum(p, axis=-1, keepdims=True)
        outs.append(jnp.dot(p.astype(BF16), vh, preferred_element_type=F32).astype(BF16))
    o = jnp.concatenate(outs, axis=1)
    xa = jnp.dot(o, wo_ref[...], preferred_element_type=F32)
    x2 = _ln_rows(ALPHA * x1 + xa, g_ref[...], b_ref[...])
    x2_ref[...] = x2
    x2p_ref[...] = _pack_bf16_pairs(x2)
    xh = x2.astype(BF16)
    xl = (x2 - xh.astype(F32)).astype(BF16)
    wh = rw_ref[0]
    wl = rw_ref[1]
    logit = (lax.dot_general(wh, xh, _DN_T, preferred_element_type=F32) +
             lax.dot_general(wh, xl, _DN_T, preferred_element_type=F32) +
             lax.dot_general(wl, xh, _DN_T, preferred_element_type=F32))
    sc_ref[...] = jax.nn.sigmoid(logit)


def _xattn(x1, kv, wq, wo, g, b, router_w, S, tq):
    T = x1.shape[0]
    wqb, wob = wq.astype(BF16), wo.astype(BF16)
    rwt = router_w.T
    rh = rwt.astype(BF16)
    rw = jnp.stack([rh, (rwt - rh.astype(F32)).astype(BF16)])
    g2, b2 = g.reshape(1, -1), b.reshape(1, -1)
    full = lambda a: pl.BlockSpec(a.shape, lambda i: (0,) * a.ndim)
    per = S // tq
    return pl.pallas_call(
        _xattn_kernel, grid=(T // tq,),
        in_specs=[pl.BlockSpec((tq, D_MODEL), lambda i: (i, 0)),
                  pl.BlockSpec((MEM_LEN, 2 * D_MODEL), lambda i: (i // per, 0)),
                  full(wqb), full(wob), full(g2), full(b2), full(rw)],
        out_specs=(pl.BlockSpec((tq, D_MODEL), lambda i: (i, 0)),
                   pl.BlockSpec((tq, D_MODEL // 2), lambda i: (i, 0)),
                   pl.BlockSpec((MOE_E, tq), lambda i: (0, i))),
        out_shape=(jax.ShapeDtypeStruct((T, D_MODEL), F32), jax.ShapeDtypeStruct((T, D_MODEL // 2), jnp.uint32),
                   jax.ShapeDtypeStruct((MOE_E, T), F32)),
        compiler_params=_params(("parallel",)), name="xattn")(x1, kv, wqb, wob, g2, b2, rw)


def _route_kernel(sc_ref, rb_ref, idx_ref, w_ref):
    E, G = MOE_E, MOE_GROUPS
    per = E // G
    scores = sc_ref[...]
    tr = scores.shape[1]
    biased = scores + rb_ref[...]
    g3 = biased.reshape(G, per, tr)
    j3 = lax.broadcasted_iota(I32, (G, per, tr), 1)
    m1 = jnp.max(g3, axis=1, keepdims=True)
    first = jnp.min(jnp.where(g3 == m1, j3, per), axis=1, keepdims=True)
    m2 = jnp.max(jnp.where(j3 == first, -jnp.inf, g3), axis=1, keepdims=True)
    gs = (m1 + m2).reshape(G, tr)
    gi = lax.broadcasted_iota(I32, (G, tr), 0)
    cnt = jnp.zeros((G, tr), F32)
    for m in range(G):
        row = gs[m:m + 1, :]
        cnt = cnt + jnp.where((row > gs) | ((row == gs) & (gi > m)), 1.0, 0.0)
    gmask = cnt < float(MOE_TOPK_GROUPS)
    masked = jnp.where(gmask[:, None, :], g3, NEG).reshape(E, tr)
    ei = lax.broadcasted_iota(I32, (E, tr), 0)
    idxs, ws = [], []
    for _ in range(MOE_K):
        mx = jnp.max(masked, axis=0, keepdims=True)
        ix = jnp.min(jnp.where(masked == mx, ei, E), axis=0, keepdims=True)
        hit = ei == ix
        ws.append(jnp.sum(jnp.where(hit, scores, 0.0), axis=0, keepdims=True))
        idxs.append(ix)
        masked = jnp.where(hit, -jnp.inf, masked)
    w = jnp.concatenate(ws, axis=0)
    idx_ref[...] = jnp.concatenate(idxs, axis=0)
    w_ref[...] = w / jnp.sum(w, axis=0, keepdims=True) * MOE_ROUTE_SCALE


def _route(scores_t, router_bias, tr):
    E, T = scores_t.shape
    rb = router_bias.reshape(E, 1)
    return pl.pallas_call(
        _route_kernel, grid=(T // tr,),
        in_specs=[pl.BlockSpec((E, tr), lambda i: (0, i)), pl.BlockSpec((E, 1), lambda i: (0, 0))],
        out_specs=(pl.BlockSpec((MOE_K, tr), lambda i: (0, i)), pl.BlockSpec((MOE_K, tr), lambda i: (0, i))),
        out_shape=(jax.ShapeDtypeStruct((MOE_K, T), I32), jax.ShapeDtypeStruct((MOE_K, T), F32)),
        compiler_params=_params(("parallel",)), name="route")(scores_t, rb)


def _rank_kernel(idx_ref, u_ref, rank_ref, cnt_ref, carry):
    E = MOE_E

    @pl.when(pl.program_id(0) == 0)
    def _():
        carry[...] = jnp.zeros_like(carry)

    idx = idx_ref[...]
    tp = idx.shape[1]
    ei = lax.broadcasted_iota(I32, (E, tp), 0)
    hits = [ei == idx[k:k + 1, :] for k in range(MOE_K)]
    onehot = jnp.zeros((E, tp), F32)
    for hit in hits:
        onehot = onehot + jnp.where(hit, 1.0, 0.0)
    pos = jnp.dot(onehot.astype(BF16), u_ref[...], preferred_element_type=F32) + carry[...]
    ranks = [jnp.sum(jnp.where(hit, pos, 0.0), axis=0, keepdims=True) for hit in hits]
    rank_ref[...] = jnp.concatenate(ranks, axis=0).astype(I32)
    total = carry[...] + jnp.sum(onehot, axis=1, keepdims=True)
    carry[...] = total
    cnt_ref[...] = jnp.broadcast_to(total, cnt_ref.shape).astype(I32)


def _rank(idx_t, tp):
    K, T = idx_t.shape
    u = jnp.triu(jnp.ones((tp, tp), F32), k=1).astype(BF16)
    rank, cnt = pl.pallas_call(
        _rank_kernel, grid=(T // tp,),
        in_specs=[pl.BlockSpec((K, tp), lambda i: (0, i)), pl.BlockSpec((tp, tp), lambda i: (0, 0))],
        out_specs=(pl.BlockSpec((K, tp), lambda i: (0, i)), pl.BlockSpec((MOE_E, LANES), lambda i: (0, 0))),
        out_shape=(jax.ShapeDtypeStruct((K, T), I32), jax.ShapeDtypeStruct((MOE_E, LANES), I32)),
        scratch_shapes=[pltpu.VMEM((MOE_E, 1), F32)],
        compiler_params=_params(("arbitrary",)), name="rank")(idx_t, u)
    return rank, cnt[:, 0]


def _dest_kernel(idx_ref, rank_ref, po_ref, dest_ref):
    idx = idx_ref[...]
    tp = idx.shape[1]
    ei = lax.broadcasted_iota(I32, (MOE_E, tp), 0)
    po = po_ref[...]
    base = [jnp.sum(jnp.where(ei == idx[k:k + 1, :], po, 0.0), axis=0, keepdims=True) for k in range(MOE_K)]
    dest_ref[...] = jnp.concatenate(base, axis=0).astype(I32) + rank_ref[...]


def _dest(idx_t, rank_t, poffs, tp):
    K, T = idx_t.shape
    po = poffs.astype(F32).reshape(MOE_E, 1)
    spec = pl.BlockSpec((K, tp), lambda i: (0, i))
    return pl.pallas_call(
        _dest_kernel, grid=(T // tp,),
        in_specs=[spec, spec, pl.BlockSpec((MOE_E, 1), lambda i: (0, 0))],
        out_specs=spec, out_shape=jax.ShapeDtypeStruct((K, T), I32),
        compiler_params=_params(("parallel",)), name="dest")(idx_t, rank_t, po)


def _pack_bf16_pairs(v):
    m = v.shape[1] // 2
    bits = lax.bitcast_convert_type(v.astype(BF16).astype(F32), jnp.uint32)
    return (bits[:, :m] >> 16) | (bits[:, m:] & jnp.uint32(0xFFFF0000))


def _unpack_bf16_pairs(w):
    lo = lax.bitcast_convert_type(w << 16, F32)
    hi = lax.bitcast_convert_type(w & jnp.uint32(0xFFFF0000), F32)
    return lo, hi


def _scatter_kernel(dest_ref, x_ref, xs_ref, sem):
    ts = x_ref.shape[0]

    def copy(t, k):
        return pltpu.make_async_copy(x_ref.at[pl.ds(t, 1)], xs_ref.at[pl.ds(dest_ref[k, t], 1)], sem)

    def start(t, c):
        for k in range(MOE_K):
            copy(t, k).start(priority=k % 2)
        return c

    def wait(t, c):
        for k in range(MOE_K):
            copy(t, k).wait()
        return c

    lax.fori_loop(0, ts, start, 0)
    lax.fori_loop(0, ts, wait, 0)


def _scatter(dest_t, x2p, n_rows, ts):
    T, wp = x2p.shape
    return pl.pallas_call(
        _scatter_kernel, grid=(T // ts,),
        in_specs=[pl.BlockSpec((MOE_K, ts), lambda i: (0, i), memory_space=pltpu.SMEM),
                  pl.BlockSpec((ts, wp), lambda i: (i, 0))],
        out_specs=pl.BlockSpec(memory_space=pl.ANY),
        out_shape=jax.ShapeDtypeStruct((n_rows, wp), jnp.uint32),
        scratch_shapes=[pltpu.SemaphoreType.DMA(())],
        compiler_params=_params(("arbitrary",)), name="scatter")(dest_t, x2p)


def _expert_kernel(be_ref, nv_ref, xs_ref, w1_ref, w3_ref, w2_ref, ys_ref):
    i = pl.program_id(0)
    nv = nv_ref[i]
    hw = D_MODEL // 2

    @pl.when(nv > 0)
    def _():
        row = lax.broadcasted_iota(I32, xs_ref.shape, 0)
        lo, hi = _unpack_bf16_pairs(jnp.where(row < nv, xs_ref[...], jnp.uint32(0)))
        lo, hi = lo.astype(BF16), hi.astype(BF16)
        w1 = w1_ref[...].astype(BF16)
        w3 = w3_ref[...].astype(BF16)
        a = (jnp.dot(lo, w1[:hw], preferred_element_type=F32) + jnp.dot(hi, w1[hw:], preferred_element_type=F32))
        c = (jnp.dot(lo, w3[:hw], preferred_element_type=F32) + jnp.dot(hi, w3[hw:], preferred_element_type=F32))
        h = (a * jax.nn.sigmoid(a) * c).astype(BF16)
        ys_ref[...] = _pack_bf16_pairs(jnp.dot(h, w2_ref[...].astype(BF16), preferred_element_type=F32))

    @pl.when(nv <= 0)
    def _():
        ys_ref[...] = jnp.zeros_like(ys_ref)


def _experts(blk_exp, blk_valid, xs, w1, w3, w2):
    n_rows, wp = xs.shape
    rb = ROW_BLOCK
    gs = pltpu.PrefetchScalarGridSpec(
        num_scalar_prefetch=2, grid=(n_rows // rb,),
        in_specs=[pl.BlockSpec((rb, wp), lambda i, be, nv: (i, 0)),
                  pl.BlockSpec((None, D_MODEL, MOE_FF), lambda i, be, nv: (be[i], 0, 0)),
                  pl.BlockSpec((None, D_MODEL, MOE_FF), lambda i, be, nv: (be[i], 0, 0)),
                  pl.BlockSpec((None, MOE_FF, D_MODEL), lambda i, be, nv: (be[i], 0, 0))],
        out_specs=pl.BlockSpec((rb, wp), lambda i, be, nv: (i, 0)))
    return pl.pallas_call(
        _expert_kernel, grid_spec=gs, out_shape=jax.ShapeDtypeStruct((n_rows, wp), jnp.uint32),
        compiler_params=_params(("arbitrary",)), name="experts")(blk_exp, blk_valid, xs, w1, w3, w2)


def _combine_kernel(dest_ref, w_ref, x_ref, ys_ref, s1_ref, s3_ref, s2_ref, g_ref, b_ref, o_ref, buf, sem):
    tc = x_ref.shape[0]

    def copy(t, k):
        return pltpu.make_async_copy(ys_ref.at[pl.ds(dest_ref[k, t], 1)], buf.at[k, pl.ds(t, 1)], sem)

    def start(t, c):
        for k in range(MOE_K):
            copy(t, k).start(priority=k % 2)
        return c

    def wait(t, c):
        for k in range(MOE_K):
            copy(t, k).wait()
        return c

    lax.fori_loop(0, tc, start, 0)
    x2 = x_ref[...]
    xb = x2.astype(BF16)
    a = jnp.dot(xb, s1_ref[...], preferred_element_type=F32)
    c = jnp.dot(xb, s3_ref[...], preferred_element_type=F32)
    shared = jnp.dot((a * jax.nn.sigmoid(a) * c).astype(BF16), s2_ref[...], preferred_element_type=F32)
    lax.fori_loop(0, tc, wait, 0)
    w = w_ref[...]
    y_lo = jnp.zeros((tc, D_MODEL // 2), F32)
    y_hi = jnp.zeros((tc, D_MODEL // 2), F32)
    for k in range(MOE_K):
        lo, hi = _unpack_bf16_pairs(buf[k])
        y_lo = y_lo + w[:, k:k + 1] * lo
        y_hi = y_hi + w[:, k:k + 1] * hi
    y = shared + jnp.concatenate([y_lo, y_hi], axis=1)
    o_ref[...] = _ln_rows(ALPHA * x2 + y, g_ref[...], b_ref[...])


def _combine(dest_t, w_nat, x2, ys, sw1, sw3, sw2, g, b, tc):
    T = x2.shape[0]
    wp = ys.shape[1]
    full = lambda a: pl.BlockSpec(a.shape, lambda i: (0,) * a.ndim)
    s1, s3, s2 = sw1.astype(BF16), sw3.astype(BF16), sw2.astype(BF16)
    g2, b2 = g.reshape(1, -1), b.reshape(1, -1)
    return pl.pallas_call(
        _combine_kernel, grid=(T // tc,),
        in_specs=[pl.BlockSpec((MOE_K, tc), lambda i: (0, i), memory_space=pltpu.SMEM),
                  pl.BlockSpec((tc, MOE_K), lambda i: (i, 0)),
                  pl.BlockSpec((tc, D_MODEL), lambda i: (i, 0)),
                  pl.BlockSpec(memory_space=pl.ANY), full(s1), full(s3), full(s2), full(g2), full(b2)],
        out_specs=pl.BlockSpec((tc, D_MODEL), lambda i: (i, 0)),
        out_shape=jax.ShapeDtypeStruct((T, D_MODEL), F32),
        scratch_shapes=[pltpu.VMEM((MOE_K, tc, wp), jnp.uint32), pltpu.SemaphoreType.DMA(())],
        compiler_params=_params(("arbitrary",)), name="combine",
    )(dest_t, w_nat, x2, ys, s1, s3, s2, g2, b2)


def _moe(x2, x2p, scores_t, router_bias, w1, w3, w2, sw1, sw3, sw2, g, b, tiles):
    T = x2.shape[0]
    rb = ROW_BLOCK
    idx_t, w_t = _route(scores_t, router_bias, tiles['route'])
    rank_t, counts = _rank(idx_t, tiles['rank'])
    pcounts = (counts + rb - 1) // rb * rb
    pends = jnp.cumsum(pcounts)
    poffs = (pends - pcounts).astype(I32)
    n_blocks = (T * MOE_K + MOE_E * (rb - 1) + rb - 1) // rb
    blk_start = jnp.arange(n_blocks, dtype=I32) * rb
    blk_exp = jnp.minimum(jnp.searchsorted(pends, blk_start, side='right'), MOE_E - 1).astype(I32)
    blk_valid = jnp.clip(counts[blk_exp] - (blk_start - poffs[blk_exp]), 0, rb).astype(I32)
    dest_t = _dest(idx_t, rank_t, poffs, tiles['rank'])
    xs = _scatter(dest_t, x2p, n_blocks * rb, tiles['scatter'])
    ys = _experts(blk_exp, blk_valid, xs, w1, w3, w2)
    return _combine(dest_t, w_t.T, x2, ys, sw1, sw3, sw2, g, b, tiles['combine'])


def _tiles(B, S):
    T = B * S
    pick = lambda want, n: want if n % want == 0 else n
    return dict(proj=pick(512, T), nsa_q=pick(128, S), nsa_ck=pick(256, S), outproj=pick(512, T),
                mlstm_nb=2 if B % 2 == 0 else 1, xattn=pick(256, S), route=pick(512, T), rank=pick(512, T), scatter=pick(256, T),
                combine=pick(128, T))


def kernel(x, mem, w_in, nsa_pos_k, nsa_cmp_k_w1, nsa_cmp_k_w2, nsa_pos_v, nsa_cmp_v_w1, nsa_cmp_v_w2,
           mlstm_conv_w, mlstm_i_bias, mlstm_f_bias, mlstm_norm_g, w_out, ln1_g, ln1_b,
           xa_wq, xa_wk, xa_wv, xa_wo, ln2_g, ln2_b, router_w, router_bias,
           moe_w1, moe_w3, moe_w2, shared_w1, shared_w3, shared_w2, ln3_g, ln3_b):
    B, S, D = x.shape
    T = B * S
    tl = _tiles(B, S)
    xc = x.reshape(T, D)
    memc = mem.reshape(B * MEM_LEN, D)
    for l in range(w_in.shape[0]):
        (q, cmp, ksel, vsel, kwin, vwin, gates, mq, mk, mv, mo, mif) = _project(xc, _prep_w_in(w_in[l]), tl['proj'])
        kcvc = _compress(cmp, B, S, _prep_cmp(nsa_pos_k[l], nsa_cmp_k_w1[l], nsa_cmp_k_w2[l],
                                              nsa_pos_v[l], nsa_cmp_v_w1[l], nsa_cmp_v_w2[l]))
        y_nsa = _nsa(q, kcvc, ksel, vsel, kwin, vwin, gates, B, S, tl['nsa_q'], tl['nsa_ck'])
        y_ml = _mlstm(mq, mk, mv, mo, mif, mlstm_conv_w[l], mlstm_i_bias[l], mlstm_f_bias[l],
                      mlstm_norm_g[l], B, S, tl['mlstm_nb'])
        x1 = _outproj(y_nsa, y_ml, xc, w_out[l], ln1_g[l], ln1_b[l], tl['outproj'])
        kv = _memkv(memc, xa_wk[l], xa_wv[l])
        x2, x2p, scores_t = _xattn(x1, kv, xa_wq[l], xa_wo[l], ln2_g[l], ln2_b[l], router_w[l], S, tl['xattn'])
        xc = _moe(x2, x2p, scores_t, router_bias[l], moe_w1[l], moe_w3[l], moe_w2[l],
                  shared_w1[l], shared_w3[l], shared_w2[l], ln3_g[l], ln3_b[l], tl)
    return xc.reshape(B, S, D)
```

```python
import functools
import numpy as np
import jax
import jax.numpy as jnp
from jax import lax
from jax.experimental import pallas as pl
from jax.experimental.pallas import tpu as pltpu

F32 = jnp.float32
BF16 = jnp.bfloat16
I32 = jnp.int32

D_MODEL = 1024
MEM_LEN = 256
NSA_HEADS = 8
NSA_GROUPS = 2
NSA_HPG = 4
NSA_DK = 64
NSA_CMP_LEN = 32
NSA_CMP_STRIDE = 16
NSA_SEL_BLOCK = 64
NSA_SEL_TOPN = 8
NSA_WINDOW = 512
ML_HEADS = 4
ML_DH = 128
ML_CHUNK = 64
ML_CONV = 4
XA_HEADS = 4
XA_DH = 256
MOE_E = 256
MOE_K = 8
MOE_GROUPS = 8
MOE_TOPK_GROUPS = 4
MOE_FF = 256
MOE_ROUTE_SCALE = 2.5
DEPTH = 1
ALPHA = (2.0 * DEPTH) ** 0.25
LN_EPS = 1e-5
NEG = -1e30
FORCE_BONUS = 1e4

LANES = 128
ROW_BLOCK = 512
VMEM_LIMIT = 56 * 1024 * 1024

_DN_T = (((1,), (1,)), ((), ()))
_DN_TA = (((0,), (0,)), ((), ()))


def _params(sem):
    return pltpu.CompilerParams(dimension_semantics=sem, vmem_limit_bytes=VMEM_LIMIT)


def _ln_rows(v, g, b):
    mu = jnp.mean(v, axis=-1, keepdims=True)
    d = v - mu
    var = jnp.mean(d * d, axis=-1, keepdims=True)
    return d * lax.rsqrt(var + LN_EPS) * g + b


_SEGS = (('q', 512, BF16), ('cmp', 256, F32), ('ksel', 128, BF16), ('vsel', 128, BF16),
         ('kwin', 128, BF16), ('vwin', 128, BF16), ('gates', 128, F32), ('mq', 512, BF16),
         ('mk', 512, BF16), ('mv', 512, BF16), ('mo', 512, BF16), ('mif', 128, F32))


def _proj_kernel(x_ref, w_ref, *out_refs):
    xb = x_ref[...].astype(BF16)
    off = 0
    for o_ref, (_, wd, _) in zip(out_refs, _SEGS):
        o_ref[...] = jnp.dot(xb, w_ref[:, off:off + wd], preferred_element_type=F32).astype(o_ref.dtype)
        off += wd


def _prep_w_in(w):
    sizes = (512,) + (128,) * 6 + (24,) + (512,) * 4 + (4, 4)
    cuts = np.cumsum(sizes)[:-1].tolist()
    (wq, kc, vc, ks, vs, kw, vw, wg, mq, mk, mv, mo, mi, mf) = jnp.split(w, cuts, axis=1)
    wq = wq.reshape(D_MODEL, NSA_GROUPS, NSA_HPG, NSA_DK).transpose(0, 2, 1, 3).reshape(D_MODEL, 512)
    pad = lambda a: jnp.pad(a, ((0, 0), (0, LANES - a.shape[1])))
    segs = [wq, kc, vc, ks, vs, kw, vw, pad(wg), mq, mk, mv, mo, pad(jnp.concatenate([mi, mf], axis=1))]
    return jnp.concatenate(segs, axis=1).astype(BF16)


def _project(x2d, w_all, tm):
    T = x2d.shape[0]
    n = w_all.shape[1]
    out_shape = tuple(jax.ShapeDtypeStruct((T, wd), dt) for _, wd, dt in _SEGS)
    out_specs = tuple(pl.BlockSpec((tm, wd), lambda i: (i, 0)) for _, wd, _ in _SEGS)
    return pl.pallas_call(
        _proj_kernel, grid=(T // tm,),
        in_specs=[pl.BlockSpec((tm, D_MODEL), lambda i: (i, 0)),
                  pl.BlockSpec((D_MODEL, n), lambda i: (0, 0))],
        out_specs=out_specs, out_shape=out_shape,
        compiler_params=_params(("parallel",)), name="proj")(x2d, w_all)


def _cmp_kernel(r_ref, pa_ref, pb_ref, wa_ref, wb_ref, w2_ref, o_ref):
    r = r_ref[...]
    a = jnp.dot((r + pa_ref[...]).astype(BF16), wa_ref[...], preferred_element_type=F32)
    b = jnp.dot((r + pb_ref[...]).astype(BF16), wb_ref[...], preferred_element_type=F32)
    nr = r.shape[0]
    hid = a + pltpu.roll(b, nr - 1, 0)
    hid = hid * jax.nn.sigmoid(hid)
    out = jnp.dot(hid.astype(BF16), w2_ref[...], preferred_element_type=F32)
    row = lax.broadcasted_iota(I32, out.shape, 0)
    o_ref[...] = jnp.where(row < nr - 1, out, 0.0).astype(o_ref.dtype)


def _prep_cmp(pos_k, w1_k, w2_k, pos_v, w1_v, w2_v):
    eye = jnp.eye(NSA_GROUPS, dtype=F32)

    def expand_w1(w1, half):
        w = w1.reshape(NSA_CMP_LEN, NSA_DK, NSA_DK)[half * 16:(half + 1) * 16]
        return jnp.einsum('jde,gk->jgdke', w, eye).reshape(16, 128, 128)

    def both(fk, fv):
        z = jnp.zeros_like(fk)
        top = jnp.concatenate([fk, z], axis=-1)
        bot = jnp.concatenate([z, fv], axis=-1)
        return jnp.concatenate([top, bot], axis=-2)

    wa = both(expand_w1(w1_k, 0), expand_w1(w1_v, 0)).reshape(16 * 256, 256).astype(BF16)
    wb = both(expand_w1(w1_k, 1), expand_w1(w1_v, 1)).reshape(16 * 256, 256).astype(BF16)
    w2 = both(jnp.kron(eye, w2_k), jnp.kron(eye, w2_v)).astype(BF16)

    def pos_row(half):
        pk = jnp.tile(pos_k[half * 16:(half + 1) * 16], (1, NSA_GROUPS))
        pv = jnp.tile(pos_v[half * 16:(half + 1) * 16], (1, NSA_GROUPS))
        return jnp.concatenate([pk, pv], axis=1).reshape(1, 16 * 256)

    return pos_row(0), pos_row(1), wa, wb, w2


def _compress(cmp2d, B, S, prep):
    pa, pb, wa, wb, w2 = prep
    nr = S // NSA_CMP_STRIDE
    r = cmp2d.reshape(B, nr, NSA_CMP_STRIDE * 256)
    full = lambda a: pl.BlockSpec(a.shape, lambda b: (0,) * a.ndim)
    return pl.pallas_call(
        _cmp_kernel, grid=(B,),
        in_specs=[pl.BlockSpec((None, nr, NSA_CMP_STRIDE * 256), lambda b: (b, 0, 0)),
                  full(pa), full(pb), full(wa), full(wb), full(w2)],
        out_specs=pl.BlockSpec((None, nr, 256), lambda b: (b, 0, 0)),
        out_shape=jax.ShapeDtypeStruct((B, nr, 256), BF16),
        compiler_params=_params(("parallel",)), name="cmp")(r, pa, pb, wa, wb, w2)


def _nsa_consts(S):
    n_cmp = (S - NSA_CMP_LEN) // NSA_CMP_STRIDE + 1
    n_sel = S // NSA_SEL_BLOCK
    cs = np.arange(n_cmp) * NSA_CMP_STRIDE
    ss = np.arange(n_sel) * NSA_SEL_BLOCK
    ov = ((cs[:, None] < ss[None, :] + NSA_SEL_BLOCK) & (cs[:, None] + NSA_CMP_LEN > ss[None, :]))
    ovt = np.zeros((LANES, S // NSA_CMP_STRIDE), np.float32)
    ovt[:n_sel, :n_cmp] = ov.T
    e = np.zeros((LANES, S), np.float32)
    e[np.arange(S) // NSA_SEL_BLOCK, np.arange(S)] = 1.0
    return jnp.asarray(ovt, BF16), jnp.asarray(e, BF16)


def _nsa_kernel(q_ref, kcvc_ref, ksel_ref, vsel_ref, kwin_ref, vwin_ref, gates_ref, ovt_ref, e_ref,
                y_ref, bias_ref, *, tq, ck, n_sel):
    G, H = NSA_GROUPS, NSA_HPG
    GH = G * H
    M = GH * tq
    S = ksel_ref.shape[0]
    W = NSA_WINDOW
    ws = min(W + tq, S)
    t0 = pl.program_id(1) * tq
    gates = jax.nn.sigmoid(gates_ref[...])
    lane = lax.broadcasted_iota(I32, (tq, LANES), 1)
    t_col = t0 + lax.broadcasted_iota(I32, (tq, 1), 0)
    kc = kcvc_ref[:, 0:LANES]
    vc = kcvc_ref[:, LANES:2 * LANES]
    nc = kc.shape[0]
    qs = []
    for g in range(G):
        gmask = (lane // NSA_DK) == g
        for h in range(H):
            qh = q_ref[:, h * LANES:(h + 1) * LANES] * (NSA_DK ** -0.5)
            qs.append(jnp.where(gmask, qh, jnp.zeros_like(qh)))
    Q = jnp.concatenate(qs, axis=0).astype(BF16)

    s = lax.dot_general(Q, kc, _DN_T, preferred_element_type=F32)
    c_idx = lax.broadcasted_iota(I32, (tq, nc), 1)
    cmask = (c_idx * NSA_CMP_STRIDE + NSA_CMP_LEN - 1) <= t_col
    s3 = jnp.where(cmask[None], s.reshape(GH, tq, nc), NEG)
    p = jnp.exp(s3 - jnp.max(s3, axis=-1, keepdims=True))
    p = p / jnp.sum(p, axis=-1, keepdims=True)
    p = jnp.where(cmask[None], p, 0.0)
    o_cmp = jnp.dot(p.reshape(M, nc).astype(BF16), vc, preferred_element_type=F32).reshape(GH, tq, LANES)

    for g in range(G):
        psum = jnp.sum(p[g * H:(g + 1) * H], axis=0)
        hi = psum.astype(BF16)
        lo = (psum - hi.astype(F32)).astype(BF16)
        ovt = ovt_ref[...]
        pslt = (lax.dot_general(ovt, hi, _DN_T, preferred_element_type=F32) +
                lax.dot_general(ovt, lo, _DN_T, preferred_element_type=F32))
        imp_p = pslt[0:n_sel, :]
        n_i = lax.broadcasted_iota(I32, (n_sel, tq), 0)
        cur = (t0 + lax.broadcasted_iota(I32, (n_sel, tq), 1)) // NSA_SEL_BLOCK
        forced = (n_i == 0) | (n_i == cur) | (n_i == cur - 1)
        imp = jnp.where(n_i <= cur, imp_p + jnp.where(forced, FORCE_BONUS, 0.0), NEG)
        cnt = jnp.zeros((n_sel, tq), F32)
        for m in range(n_sel):
            row = imp[m:m + 1, :]
            beats = (row > imp) | ((row == imp) & (n_i > m))
            cnt = cnt + jnp.where(beats, 1.0, 0.0)
        selt = jnp.where(cnt < float(min(NSA_SEL_TOPN, n_sel)), 1.0, 0.0)
        selt = jnp.concatenate([selt, jnp.zeros((LANES - n_sel, tq), F32)], axis=0)
        sel = selt.T.astype(BF16)
        maskf = jnp.dot(sel, e_ref[...], preferred_element_type=F32)
        kpos = lax.broadcasted_iota(I32, (tq, S), 1)
        bias_ref[g] = jnp.where((maskf > 0.5) & (kpos <= t_col), 0.0, NEG)

    def sel_body(j, carry):
        m_i, l_i, acc = carry
        ks = pl.multiple_of(j * ck, ck)
        k = ksel_ref[pl.ds(ks, ck), :]
        v = vsel_ref[pl.ds(ks, ck), :]
        sj = lax.dot_general(Q, k, _DN_T, preferred_element_type=F32)
        sj = (sj.reshape(G, H, tq, ck) + bias_ref[:, :, pl.ds(ks, ck)][:, None]).reshape(GH, tq, ck)
        m_new = jnp.maximum(m_i, jnp.max(sj, axis=-1, keepdims=True))
        a = jnp.exp(m_i - m_new)
        pj = jnp.exp(sj - m_new)
        l_new = a * l_i + jnp.sum(pj, axis=-1, keepdims=True)
        pv = jnp.dot(pj.reshape(M, ck).astype(BF16), v, preferred_element_type=F32)
        return m_new, l_new, a * acc + pv.reshape(GH, tq, LANES)

    init = (jnp.full((GH, tq, 1), NEG, F32), jnp.zeros((GH, tq, 1), F32), jnp.zeros((GH, tq, LANES), F32))
    _, l_f, acc = lax.fori_loop(0, (t0 + tq + ck - 1) // ck, sel_body, init)
    o_sel = acc / l_f

    kst = pl.multiple_of(jnp.clip(t0 - W, 0, S - ws), LANES)
    kwn = kwin_ref[pl.ds(kst, ws), :]
    vwn = vwin_ref[pl.ds(kst, ws), :]
    sw = lax.dot_general(Q, kwn, _DN_T, preferred_element_type=F32)
    wpos = kst + lax.broadcasted_iota(I32, (tq, ws), 1)
    wmask = (wpos <= t_col) & (wpos > t_col - W)
    sw3 = jnp.where(wmask[None], sw.reshape(GH, tq, ws), NEG)
    pw = jnp.exp(sw3 - jnp.max(sw3, axis=-1, keepdims=True))
    lw = jnp.sum(pw, axis=-1, keepdims=True)
    o_win = jnp.dot(pw.reshape(M, ws).astype(BF16), vwn, preferred_element_type=F32).reshape(GH, tq, LANES) / lw

    g0mask = lane < NSA_DK
    for h in range(H):
        o_g = []
        for g in range(G):
            r = g * H + h
            c0 = r * 3
            o_g.append(gates[:, c0:c0 + 1] * o_cmp[r] + gates[:, c0 + 1:c0 + 2] * o_sel[r] +
                       gates[:, c0 + 2:c0 + 3] * o_win[r])
        y_ref[:, h * LANES:(h + 1) * LANES] = jnp.where(g0mask, o_g[0], o_g[1]).astype(y_ref.dtype)


def _nsa(q, kcvc, ksel, vsel, kwin, vwin, gates, B, S, tq, ck):
    T = B * S
    nq = S // tq
    ovt, e = _nsa_consts(S)
    seq = lambda a: a.reshape(B, S, LANES)
    kv_spec = pl.BlockSpec((None, S, LANES), lambda b, i: (b, 0, 0))
    kern = functools.partial(_nsa_kernel, tq=tq, ck=ck, n_sel=S // NSA_SEL_BLOCK)
    return pl.pallas_call(
        kern, grid=(B, nq),
        in_specs=[pl.BlockSpec((tq, 512), lambda b, i: (b * nq + i, 0)),
                  pl.BlockSpec((None,) + kcvc.shape[1:], lambda b, i: (b, 0, 0)),
                  kv_spec, kv_spec, kv_spec, kv_spec,
                  pl.BlockSpec((tq, LANES), lambda b, i: (b * nq + i, 0)),
                  pl.BlockSpec(ovt.shape, lambda b, i: (0, 0)),
                  pl.BlockSpec(e.shape, lambda b, i: (0, 0))],
        out_specs=pl.BlockSpec((tq, 512), lambda b, i: (b * nq + i, 0)),
        out_shape=jax.ShapeDtypeStruct((T, 512), BF16),
        scratch_shapes=[pltpu.VMEM((NSA_GROUPS, tq, S), F32)],
        compiler_params=_params(("parallel", "parallel")), name="nsa",
    )(q, kcvc, seq(ksel), seq(vsel), seq(kwin), seq(vwin), gates, ovt, e)


def _mlstm_kernel(q_ref, k_ref, v_ref, o_ref, gn_ref, gt_ref, cw_ref, bn_ref, bt_ref, ng_ref, tri_ref,
                  y_ref, c_scr, n_scr):
    H, dh, L = ML_HEADS, ML_DH, ML_CHUNK
    nb, S = q_ref.shape[0], q_ref.shape[1]
    nchunk = S // L
    c_scr[...] = jnp.zeros_like(c_scr)
    n_scr[...] = jnp.zeros_like(n_scr)
    row = lax.broadcasted_iota(I32, (L, H * dh), 0)
    li = lax.broadcasted_iota(I32, (L, L), 0)
    mi = lax.broadcasted_iota(I32, (L, L), 1)
    causal = mi <= li
    tril = tri_ref[0]
    triu = tri_ref[1]
    hp = lax.Precision.HIGHEST

    def conv_silu(ref, bi, c, wofs):
        r0 = pl.multiple_of(c * L, L)
        rp = pl.multiple_of(jnp.maximum(c - 1, 0) * L, L)
        cur = ref[bi, pl.ds(r0, L), :].astype(F32)
        prev = ref[bi, pl.ds(rp, L), :].astype(F32) * jnp.where(c > 0, 1.0, 0.0)
        acc = cur * cw_ref[ML_CONV - 1:ML_CONV, wofs:wofs + H * dh]
        for j in range(1, ML_CONV):
            sh = jnp.where(row < j, pltpu.roll(prev, j, 0), pltpu.roll(cur, j, 0))
            acc = acc + sh * cw_ref[ML_CONV - 1 - j:ML_CONV - j, wofs:wofs + H * dh]
        return acc * jax.nn.sigmoid(acc)

    def body(c, m_state):
        r0 = pl.multiple_of(c * L, L)
        new_m = []
        for bi in range(nb):
            qa = conv_silu(q_ref, bi, c, 0) * (dh ** -0.5)
            ka = conv_silu(k_ref, bi, c, H * dh)
            va = v_ref[bi, pl.ds(r0, L), :]
            oa = o_ref[bi, pl.ds(r0, L), :].astype(F32)
            gn = gn_ref[bi, pl.ds(r0, L), :] + bn_ref[...]
            gt = gt_ref[bi, :, c, :] + bt_ref[...]
            lf_n = jax.nn.log_sigmoid(gn)
            lf_t = jax.nn.log_sigmoid(gt)
            b_n = jnp.dot(tril, lf_n, precision=hp, preferred_element_type=F32)
            b_t = jnp.dot(lf_t, triu, precision=hp, preferred_element_type=F32)
            for h in range(H):
                st = bi * H + h
                q = qa[:, h * dh:(h + 1) * dh]
                k = ka[:, h * dh:(h + 1) * dh]
                v = va[:, h * dh:(h + 1) * dh]
                m_old = m_state[st]
                b_col = b_n[:, H + h:H + h + 1]
                i_col = gn[:, h:h + 1]
                b_row = b_t[H + h:H + h + 1, :]
                i_row = gt[h:h + 1, :]
                g_tot = b_t[H + h:H + h + 1, L - 1:L]
                d_log = jnp.where(causal, b_col - b_row + i_row, NEG)
                inter = b_col + m_old
                m_q = jnp.maximum(inter, jnp.max(d_log, axis=-1, keepdims=True))
                w_intra = jnp.exp(d_log - m_q)
                w_inter = jnp.exp(inter - m_q)
                qb = q.astype(BF16)
                s = lax.dot_general(qb, k.astype(BF16), _DN_T, preferred_element_type=F32) * w_intra
                cst = c_scr[st]
                nst = n_scr[st]
                num = (w_inter * jnp.dot(qb, cst.astype(BF16), preferred_element_type=F32) +
                       jnp.dot(s.astype(BF16), v, preferred_element_type=F32))
                den = w_inter * jnp.sum(q * nst, axis=-1, keepdims=True) + jnp.sum(s, axis=-1, keepdims=True)
                hv = num / jnp.maximum(jnp.abs(den), jnp.exp(-m_q))
                log_k = g_tot - b_col + i_col
                m_new = jnp.maximum(g_tot + m_old, jnp.max(log_k, axis=0, keepdims=True))
                wk = jnp.exp(log_k - m_new)
                decay = jnp.exp(g_tot + m_old - m_new)
                kw = k * wk
                c_scr[st] = decay * cst + lax.dot_general(kw.astype(BF16), v, _DN_TA, preferred_element_type=F32)
                n_scr[st] = decay * nst + jnp.sum(kw, axis=0, keepdims=True)
                new_m.append(m_new)
                mu = jnp.mean(hv, axis=-1, keepdims=True)
                dv = hv - mu
                var = jnp.mean(dv * dv, axis=-1, keepdims=True)
                hn = dv * lax.rsqrt(var + LN_EPS) * ng_ref[:, h * dh:(h + 1) * dh]
                og = jax.nn.sigmoid(oa[:, h * dh:(h + 1) * dh])
                y_ref[bi, pl.ds(r0, L), h * dh:(h + 1) * dh] = (og * hn).astype(y_ref.dtype)
        return tuple(new_m)

    lax.fori_loop(0, nchunk, body, tuple(jnp.zeros((1, 1), F32) for _ in range(nb * H)))


def _mlstm(mq, mk, mv, mo, mif, conv_w, i_bias, f_bias, norm_g, B, S, nb):
    T = B * S
    H, dh, L = ML_HEADS, ML_DH, ML_CHUNK
    W = H * dh
    gt = mif[:, :2 * H].reshape(B, S, 2 * H).transpose(0, 2, 1).reshape(B, 2 * H, S // L, L)
    cw = conv_w.reshape(ML_CONV, 2 * W)
    bias = jnp.concatenate([i_bias, f_bias])
    bn = jnp.pad(bias, (0, LANES - 2 * H)).reshape(1, LANES)
    bt = bias.reshape(2 * H, 1)
    ng = norm_g.reshape(1, W)
    tri = jnp.stack([jnp.tril(jnp.ones((L, L), F32)), jnp.triu(jnp.ones((L, L), F32))])
    seq = lambda a: a.reshape(B, S, a.shape[1])
    rows = lambda w: pl.BlockSpec((nb, S, w), lambda b: (b, 0, 0))
    full = lambda a: pl.BlockSpec(a.shape, lambda b: (0,) * a.ndim)
    y = pl.pallas_call(
        _mlstm_kernel, grid=(B // nb,),
        in_specs=[rows(W), rows(W), rows(W), rows(W), rows(LANES),
                  pl.BlockSpec((nb, 2 * H, S // L, L), lambda b: (b, 0, 0, 0)),
                  full(cw), full(bn), full(bt), full(ng), full(tri)],
        out_specs=rows(W),
        out_shape=jax.ShapeDtypeStruct((B, S, W), BF16),
        scratch_shapes=[pltpu.VMEM((nb * H, dh, dh), F32), pltpu.VMEM((nb * H, 1, dh), F32)],
        compiler_params=_params(("parallel",)), name="mlstm",
    )(seq(mq), seq(mk), seq(mv), seq(mo), seq(mif), gt, cw, bn, bt, ng, tri)
    return y.reshape(T, W)


def _outproj_kernel(yn_ref, ym_ref, x_ref, w_ref, g_ref, b_ref, o_ref):
    mix = (jnp.dot(yn_ref[...], w_ref[0:512, :], preferred_element_type=F32) +
           jnp.dot(ym_ref[...], w_ref[512:1024, :], preferred_element_type=F32))
    o_ref[...] = _ln_rows(ALPHA * x_ref[...] + mix, g_ref[...], b_ref[...])


def _outproj(y_nsa, y_ml, x2d, w_out, g, b, tm):
    T = x2d.shape[0]
    wn = w_out[:512].reshape(NSA_GROUPS, NSA_HPG, NSA_DK, D_MODEL).transpose(1, 0, 2, 3).reshape(512, D_MODEL)
    w = jnp.concatenate([wn, w_out[512:]], axis=0).astype(BF16)
    row = lambda wd: pl.BlockSpec((tm, wd), lambda i: (i, 0))
    full = lambda a: pl.BlockSpec(a.shape, lambda i: (0,) * a.ndim)
    g2, b2 = g.reshape(1, -1), b.reshape(1, -1)
    return pl.pallas_call(
        _outproj_kernel, grid=(T // tm,),
        in_specs=[row(512), row(512), row(D_MODEL), full(w), full(g2), full(b2)],
        out_specs=row(D_MODEL), out_shape=jax.ShapeDtypeStruct((T, D_MODEL), F32),
        compiler_params=_params(("parallel",)), name="outproj")(y_nsa, y_ml, x2d, w, g2, b2)


def _memkv_kernel(m_ref, w_ref, o_ref):
    o_ref[...] = jnp.dot(m_ref[...].astype(BF16), w_ref[...], preferred_element_type=F32).astype(o_ref.dtype)


def _memkv(mem2d, wk, wv):
    w = jnp.concatenate([wk, wv], axis=1).astype(BF16)
    R = mem2d.shape[0]
    return pl.pallas_call(
        _memkv_kernel, grid=(R // MEM_LEN,),
        in_specs=[pl.BlockSpec((MEM_LEN, D_MODEL), lambda i: (i, 0)),
                  pl.BlockSpec(w.shape, lambda i: (0, 0))],
        out_specs=pl.BlockSpec((MEM_LEN, 2 * D_MODEL), lambda i: (i, 0)),
        out_shape=jax.ShapeDtypeStruct((R, 2 * D_MODEL), BF16),
        compiler_params=_params(("parallel",)), name="memkv")(mem2d, w)


def _xattn_kernel(x_ref, kv_ref, wq_ref, wo_ref, g_ref, b_ref, rw_ref, x2_ref, x2p_ref, sc_ref):
    x1 = x_ref[...]
    q = jnp.dot(x1.astype(BF16), wq_ref[...], preferred_element_type=F32).astype(BF16)
    outs = []
    for h in range(XA_HEADS):
        qh = q[:, h * XA_DH:(h + 1) * XA_DH]
        kh = kv_ref[:, h * XA_DH:(h + 1) * XA_DH]
        vh = kv_ref[:, D_MODEL + h * XA_DH:D_MODEL + (h + 1) * XA_DH]
        s = lax.dot_general(qh, kh, _DN_T, preferred_element_type=F32) * (XA_DH ** -0.5)
        p = jnp.exp(s - jnp.max(s, axis=-1, keepdims=True))
        p = p / jnp.sum(p, axis=-1, keepdims=True)
        outs.append(jnp.dot(p.astype(BF16), vh, preferred_element_type=F32).astype(BF16))
    o = jnp.concatenate(outs, axis=1)
    xa = jnp.dot(o, wo_ref[...], preferred_element_type=F32)
    x2 = _ln_rows(ALPHA * x1 + xa, g_ref[...], b_ref[...])
    x2_ref[...] = x2
    x2p_ref[...] = _pack_bf16_pairs(x2)
    xh = x2.astype(BF16)
    xl = (x2 - xh.astype(F32)).astype(BF16)
    wh = rw_ref[0]
    wl = rw_ref[1]
    logit = (lax.dot_general(wh, xh, _DN_T, preferred_element_type=F32) +
             lax.dot_general(wh, xl, _DN_T, preferred_element_type=F32) +
             lax.dot_general(wl, xh, _DN_T, preferred_element_type=F32))
    sc_ref[...] = jax.nn.sigmoid(logit)


def _xattn(x1, kv, wq, wo, g, b, router_w, S, tq):
    T = x1.shape[0]
    wqb, wob = wq.astype(BF16), wo.astype(BF16)
    rwt = router_w.T
    rh = rwt.astype(BF16)
    rw = jnp.stack([rh, (rwt - rh.astype(F32)).astype(BF16)])
    g2, b2 = g.reshape(1, -1), b.reshape(1, -1)
    full = lambda a: pl.BlockSpec(a.shape, lambda i: (0,) * a.ndim)
    per = S // tq
    return pl.pallas_call(
        _xattn_kernel, grid=(T // tq,),
        in_specs=[pl.BlockSpec((tq, D_MODEL), lambda i: (i, 0)),
                  pl.BlockSpec((MEM_LEN, 2 * D_MODEL), lambda i: (i // per, 0)),
                  full(wqb), full(wob), full(g2), full(b2), full(rw)],
        out_specs=(pl.BlockSpec((tq, D_MODEL), lambda i: (i, 0)),
                   pl.BlockSpec((tq, D_MODEL // 2), lambda i: (i, 0)),
                   pl.BlockSpec((MOE_E, tq), lambda i: (0, i))),
        out_shape=(jax.ShapeDtypeStruct((T, D_MODEL), F32), jax.ShapeDtypeStruct((T, D_MODEL // 2), jnp.uint32),
                   jax.ShapeDtypeStruct((MOE_E, T), F32)),
        compiler_params=_params(("parallel",)), name="xattn")(x1, kv, wqb, wob, g2, b2, rw)


def _route_kernel(sc_ref, rb_ref, idx_ref, w_ref):
    E, G = MOE_E, MOE_GROUPS
    per = E // G
    scores = sc_ref[...]
    tr = scores.shape[1]
    biased = scores + rb_ref[...]
    g3 = biased.reshape(G, per, tr)
    j3 = lax.broadcasted_iota(I32, (G, per, tr), 1)
    m1 = jnp.max(g3, axis=1, keepdims=True)
    first = jnp.min(jnp.where(g3 == m1, j3, per), axis=1, keepdims=True)
    m2 = jnp.max(jnp.where(j3 == first, -jnp.inf, g3), axis=1, keepdims=True)
    gs = (m1 + m2).reshape(G, tr)
    gi = lax.broadcasted_iota(I32, (G, tr), 0)
    cnt = jnp.zeros((G, tr), F32)
    for m in range(G):
        row = gs[m:m + 1, :]
        cnt = cnt + jnp.where((row > gs) | ((row == gs) & (gi > m)), 1.0, 0.0)
    gmask = cnt < float(MOE_TOPK_GROUPS)
    masked = jnp.where(gmask[:, None, :], g3, NEG).reshape(E, tr)
    ei = lax.broadcasted_iota(I32, (E, tr), 0)
    idxs, ws = [], []
    for _ in range(MOE_K):
        mx = jnp.max(masked, axis=0, keepdims=True)
        ix = jnp.min(jnp.where(masked == mx, ei, E), axis=0, keepdims=True)
        hit = ei == ix
        ws.append(jnp.sum(jnp.where(hit, scores, 0.0), axis=0, keepdims=True))
        idxs.append(ix)
        masked = jnp.where(hit, -jnp.inf, masked)
    w = jnp.concatenate(ws, axis=0)
    idx_ref[...] = jnp.concatenate(idxs, axis=0)
    w_ref[...] = w / jnp.sum(w, axis=0, keepdims=True) * MOE_ROUTE_SCALE


def _route(scores_t, router_bias, tr):
    E, T = scores_t.shape
    rb = router_bias.reshape(E, 1)
    return pl.pallas_call(
        _route_kernel, grid=(T // tr,),
        in_specs=[pl.BlockSpec((E, tr), lambda i: (0, i)), pl.BlockSpec((E, 1), lambda i: (0, 0))],
        out_specs=(pl.BlockSpec((MOE_K, tr), lambda i: (0, i)), pl.BlockSpec((MOE_K, tr), lambda i: (0, i))),
        out_shape=(jax.ShapeDtypeStruct((MOE_K, T), I32), jax.ShapeDtypeStruct((MOE_K, T), F32)),
        compiler_params=_params(("parallel",)), name="route")(scores_t, rb)


def _rank_kernel(idx_ref, u_ref, rank_ref, cnt_ref, carry):
    E = MOE_E

    @pl.when(pl.program_id(0) == 0)
    def _():
        carry[...] = jnp.zeros_like(carry)

    idx = idx_ref[...]
    tp = idx.shape[1]
    ei = lax.broadcasted_iota(I32, (E, tp), 0)
    hits = [ei == idx[k:k + 1, :] for k in range(MOE_K)]
    onehot = jnp.zeros((E, tp), F32)
    for hit in hits:
        onehot = onehot + jnp.where(hit, 1.0, 0.0)
    pos = jnp.dot(onehot.astype(BF16), u_ref[...], preferred_element_type=F32) + carry[...]
    ranks = [jnp.sum(jnp.where(hit, pos, 0.0), axis=0, keepdims=True) for hit in hits]
    rank_ref[...] = jnp.concatenate(ranks, axis=0).astype(I32)
    total = carry[...] + jnp.sum(onehot, axis=1, keepdims=True)
    carry[...] = total
    cnt_ref[...] = jnp.broadcast_to(total, cnt_ref.shape).astype(I32)


def _rank(idx_t, tp):
    K, T = idx_t.shape
    u = jnp.triu(jnp.ones((tp, tp), F32), k=1).astype(BF16)
    rank, cnt = pl.pallas_call(
        _rank_kernel, grid=(T // tp,),
        in_specs=[pl.BlockSpec((K, tp), lambda i: (0, i)), pl.BlockSpec((tp, tp), lambda i: (0, 0))],
        out_specs=(pl.BlockSpec((K, tp), lambda i: (0, i)), pl.BlockSpec((MOE_E, LANES), lambda i: (0, 0))),
        out_shape=(jax.ShapeDtypeStruct((K, T), I32), jax.ShapeDtypeStruct((MOE_E, LANES), I32)),
        scratch_shapes=[pltpu.VMEM((MOE_E, 1), F32)],
        compiler_params=_params(("arbitrary",)), name="rank")(idx_t, u)
    return rank, cnt[:, 0]


def _dest_kernel(idx_ref, rank_ref, po_ref, dest_ref):
    idx = idx_ref[...]
    tp = idx.shape[1]
    ei = lax.broadcasted_iota(I32, (MOE_E, tp), 0)
    po = po_ref[...]
    base = [jnp.sum(jnp.where(ei == idx[k:k + 1, :], po, 0.0), axis=0, keepdims=True) for k in range(MOE_K)]
    dest_ref[...] = jnp.concatenate(base, axis=0).astype(I32) + rank_ref[...]


def _dest(idx_t, rank_t, poffs, tp):
    K, T = idx_t.shape
    po = poffs.astype(F32).reshape(MOE_E, 1)
    spec = pl.BlockSpec((K, tp), lambda i: (0, i))
    return pl.pallas_call(
        _dest_kernel, grid=(T // tp,),
        in_specs=[spec, spec, pl.BlockSpec((MOE_E, 1), lambda i: (0, 0))],
        out_specs=spec, out_shape=jax.ShapeDtypeStruct((K, T), I32),
        compiler_params=_params(("parallel",)), name="dest")(idx_t, rank_t, po)


def _pack_bf16_pairs(v):
    m = v.shape[1] // 2
    bits = lax.bitcast_convert_type(v.astype(BF16).astype(F32), jnp.uint32)
    return (bits[:, :m] >> 16) | (bits[:, m:] & jnp.uint32(0xFFFF0000))


def _unpack_bf16_pairs(w):
    lo = lax.bitcast_convert_type(w << 16, F32)
    hi = lax.bitcast_convert_type(w & jnp.uint32(0xFFFF0000), F32)
    return lo, hi


def _scatter_kernel(dest_ref, x_ref, xs_ref, sem):
    ts = x_ref.shape[0]

    def copy(t, k):
        return pltpu.make_async_copy(x_ref.at[pl.ds(t, 1)], xs_ref.at[pl.ds(dest_ref[t * MOE_K + k], 1)], sem)

    def start(t, c):
        for k in range(MOE_K):
            copy(t, k).start(priority=k % 2)
        return c

    def wait(t, c):
        for k in range(MOE_K):
            copy(t, k).wait()
        return c

    lax.fori_loop(0, ts, start, 0)
    lax.fori_loop(0, ts, wait, 0)


def _scatter(dest, x2p, n_rows, ts):
    T, wp = x2p.shape
    return pl.pallas_call(
        _scatter_kernel, grid=(T // ts,),
        in_specs=[pl.BlockSpec((MOE_K * ts,), lambda i: (i,), memory_space=pltpu.SMEM),
                  pl.BlockSpec((ts, wp), lambda i: (i, 0))],
        out_specs=pl.BlockSpec(memory_space=pl.ANY),
        out_shape=jax.ShapeDtypeStruct((n_rows, wp), jnp.uint32),
        scratch_shapes=[pltpu.SemaphoreType.DMA(())],
        compiler_params=_params(("arbitrary",)), name="scatter")(dest, x2p)


def _expert_kernel(be_ref, nv_ref, xs_ref, w1_ref, w3_ref, w2_ref, ys_ref):
    i = pl.program_id(0)
    nv = nv_ref[i]
    hw = D_MODEL // 2

    @pl.when(nv > 0)
    def _():
        row = lax.broadcasted_iota(I32, xs_ref.shape, 0)
        lo, hi = _unpack_bf16_pairs(jnp.where(row < nv, xs_ref[...], jnp.uint32(0)))
        lo, hi = lo.astype(BF16), hi.astype(BF16)
        w1 = w1_ref[...].astype(BF16)
        w3 = w3_ref[...].astype(BF16)
        a = (jnp.dot(lo, w1[:hw], preferred_element_type=F32) + jnp.dot(hi, w1[hw:], preferred_element_type=F32))
        c = (jnp.dot(lo, w3[:hw], preferred_element_type=F32) + jnp.dot(hi, w3[hw:], preferred_element_type=F32))
        h = (a * jax.nn.sigmoid(a) * c).astype(BF16)
        ys_ref[...] = _pack_bf16_pairs(jnp.dot(h, w2_ref[...].astype(BF16), preferred_element_type=F32))

    @pl.when(nv <= 0)
    def _():
        ys_ref[...] = jnp.zeros_like(ys_ref)


def _experts(blk_exp, blk_valid, xs, w1, w3, w2):
    n_rows, wp = xs.shape
    rb = ROW_BLOCK
    gs = pltpu.PrefetchScalarGridSpec(
        num_scalar_prefetch=2, grid=(n_rows // rb,),
        in_specs=[pl.BlockSpec((rb, wp), lambda i, be, nv: (i, 0)),
                  pl.BlockSpec((None, D_MODEL, MOE_FF), lambda i, be, nv: (be[i], 0, 0)),
                  pl.BlockSpec((None, D_MODEL, MOE_FF), lambda i, be, nv: (be[i], 0, 0)),
                  pl.BlockSpec((None, MOE_FF, D_MODEL), lambda i, be, nv: (be[i], 0, 0))],
        out_specs=pl.BlockSpec((rb, wp), lambda i, be, nv: (i, 0)))
    return pl.pallas_call(
        _expert_kernel, grid_spec=gs, out_shape=jax.ShapeDtypeStruct((n_rows, wp), jnp.uint32),
        compiler_params=_params(("arbitrary",)), name="experts")(blk_exp, blk_valid, xs, w1, w3, w2)


def _combine_kernel(dest_ref, w_ref, x_ref, ys_ref, s1_ref, s3_ref, s2_ref, g_ref, b_ref, o_ref, buf, sem):
    tc = x_ref.shape[0]

    def copy(t, k):
        return pltpu.make_async_copy(ys_ref.at[pl.ds(dest_ref[t * MOE_K + k], 1)], buf.at[k, pl.ds(t, 1)], sem)

    def start(t, c):
        for k in range(MOE_K):
            copy(t, k).start(priority=k % 2)
        return c

    def wait(t, c):
        for k in range(MOE_K):
            copy(t, k).wait()
        return c

    lax.fori_loop(0, tc, start, 0)
    x2 = x_ref[...]
    xb = x2.astype(BF16)
    a = jnp.dot(xb, s1_ref[...], preferred_element_type=F32)
    c = jnp.dot(xb, s3_ref[...], preferred_element_type=F32)
    shared = jnp.dot((a * jax.nn.sigmoid(a) * c).astype(BF16), s2_ref[...], preferred_element_type=F32)
    lax.fori_loop(0, tc, wait, 0)
    w = w_ref[...]
    y_lo = jnp.zeros((tc, D_MODEL // 2), F32)
    y_hi = jnp.zeros((tc, D_MODEL // 2), F32)
    for k in range(MOE_K):
        lo, hi = _unpack_bf16_pairs(buf[k])
        y_lo = y_lo + w[:, k:k + 1] * lo
        y_hi = y_hi + w[:, k:k + 1] * hi
    y = shared + jnp.concatenate([y_lo, y_hi], axis=1)
    o_ref[...] = _ln_rows(ALPHA * x2 + y, g_ref[...], b_ref[...])


def _combine(dest, w_nat, x2, ys, sw1, sw3, sw2, g, b, tc):
    T = x2.shape[0]
    wp = ys.shape[1]
    full = lambda a: pl.BlockSpec(a.shape, lambda i: (0,) * a.ndim)
    s1, s3, s2 = sw1.astype(BF16), sw3.astype(BF16), sw2.astype(BF16)
    g2, b2 = g.reshape(1, -1), b.reshape(1, -1)
    return pl.pallas_call(
        _combine_kernel, grid=(T // tc,),
        in_specs=[pl.BlockSpec((MOE_K * tc,), lambda i: (i,), memory_space=pltpu.SMEM),
                  pl.BlockSpec((tc, MOE_K), lambda i: (i, 0)),
                  pl.BlockSpec((tc, D_MODEL), lambda i: (i, 0)),
                  pl.BlockSpec(memory_space=pl.ANY), full(s1), full(s3), full(s2), full(g2), full(b2)],
        out_specs=pl.BlockSpec((tc, D_MODEL), lambda i: (i, 0)),
        out_shape=jax.ShapeDtypeStruct((T, D_MODEL), F32),
        scratch_shapes=[pltpu.VMEM((MOE_K, tc, wp), jnp.uint32), pltpu.SemaphoreType.DMA(())],
        compiler_params=_params(("arbitrary",)), name="combine",
    )(dest, w_nat, x2, ys, s1, s3, s2, g2, b2)


def _moe(x2, x2p, scores_t, router_bias, w1, w3, w2, sw1, sw3, sw2, g, b, tiles):
    T = x2.shape[0]
    rb = ROW_BLOCK
    idx_t, w_t = _route(scores_t, router_bias, tiles['route'])
    rank_t, counts = _rank(idx_t, tiles['rank'])
    pcounts = (counts + rb - 1) // rb * rb
    pends = jnp.cumsum(pcounts)
    poffs = (pends - pcounts).astype(I32)
    n_blocks = (T * MOE_K + MOE_E * (rb - 1) + rb - 1) // rb
    blk_start = jnp.arange(n_blocks, dtype=I32) * rb
    blk_exp = jnp.minimum(jnp.searchsorted(pends, blk_start, side='right'), MOE_E - 1).astype(I32)
    blk_valid = jnp.clip(counts[blk_exp] - (blk_start - poffs[blk_exp]), 0, rb).astype(I32)
    dest = _dest(idx_t, rank_t, poffs, tiles['rank']).T.reshape(-1)
    xs = _scatter(dest, x2p, n_blocks * rb, tiles['scatter'])
    ys = _experts(blk_exp, blk_valid, xs, w1, w3, w2)
    return _combine(dest, w_t.T, x2, ys, sw1, sw3, sw2, g, b, tiles['combine'])


def _tiles(B, S):
    T = B * S
    pick = lambda want, n: want if n % want == 0 else n
    return dict(proj=pick(512, T), nsa_q=pick(128, S), nsa_ck=pick(512, S), outproj=pick(512, T),
                mlstm_nb=2 if B % 2 == 0 else 1, xattn=pick(256, S), route=pick(512, T), rank=pick(512, T), scatter=pick(256, T),
                combine=pick(128, T))


def kernel(x, mem, w_in, nsa_pos_k, nsa_cmp_k_w1, nsa_cmp_k_w2, nsa_pos_v, nsa_cmp_v_w1, nsa_cmp_v_w2,
           mlstm_conv_w, mlstm_i_bias, mlstm_f_bias, mlstm_norm_g, w_out, ln1_g, ln1_b,
           xa_wq, xa_wk, xa_wv, xa_wo, ln2_g, ln2_b, router_w, router_bias,
           moe_w1, moe_w3, moe_w2, shared_w1, shared_w3, shared_w2, ln3_g, ln3_b):
    B, S, D = x.shape
    T = B * S
    tl = _tiles(B, S)
    xc = x.reshape(T, D)
    memc = mem.reshape(B * MEM_LEN, D)
    for l in range(w_in.shape[0]):
        (q, cmp, ksel, vsel, kwin, vwin, gates, mq, mk, mv, mo, mif) = _project(xc, _prep_w_in(w_in[l]), tl['proj'])
        kcvc = _compress(cmp, B, S, _prep_cmp(nsa_pos_k[l], nsa_cmp_k_w1[l], nsa_cmp_k_w2[l],
                                              nsa_pos_v[l], nsa_cmp_v_w1[l], nsa_cmp_v_w2[l]))
        y_nsa = _nsa(q, kcvc, ksel, vsel, kwin, vwin, gates, B, S, tl['nsa_q'], tl['nsa_ck'])
        y_ml = _mlstm(mq, mk, mv, mo, mif, mlstm_conv_w[l], mlstm_i_bias[l], mlstm_f_bias[l],
                      mlstm_norm_g[l], B, S, tl['mlstm_nb'])
        x1 = _outproj(y_nsa, y_ml, xc, w_out[l], ln1_g[l], ln1_b[l], tl['outproj'])
        kv = _memkv(memc, xa_wk[l], xa_wv[l])
        x2, x2p, scores_t = _xattn(x1, kv, xa_wq[l], xa_wo[l], ln2_g[l], ln2_b[l], router_w[l], S, tl['xattn'])
        xc = _moe(x2, x2p, scores_t, router_bias[l], moe_w1[l], moe_w3[l], moe_w2[l],
                  shared_w1[l], shared_w3[l], shared_w2[l], ln3_g[l], ln3_b[l], tl)
    return xc.reshape(B, S, D)
```

```python
import functools
import numpy as np
import jax
import jax.numpy as jnp
from jax import lax
from jax.experimental import pallas as pl
from jax.experimental.pallas import tpu as pltpu
from jax.experimental.pallas import tpu_sc as plsc

F32 = jnp.float32
BF16 = jnp.bfloat16
I32 = jnp.int32

D_MODEL = 1024
MEM_LEN = 256
NSA_HEADS = 8
NSA_GROUPS = 2
NSA_HPG = 4
NSA_DK = 64
NSA_CMP_LEN = 32
NSA_CMP_STRIDE = 16
NSA_SEL_BLOCK = 64
NSA_SEL_TOPN = 8
NSA_WINDOW = 512
ML_HEADS = 4
ML_DH = 128
ML_CHUNK = 64
ML_CONV = 4
XA_HEADS = 4
XA_DH = 256
MOE_E = 256
MOE_K = 8
MOE_GROUPS = 8
MOE_TOPK_GROUPS = 4
MOE_FF = 256
MOE_ROUTE_SCALE = 2.5
DEPTH = 1
ALPHA = (2.0 * DEPTH) ** 0.25
LN_EPS = 1e-5
NEG = -1e30
FORCE_BONUS = 1e4

LANES = 128
ROW_BLOCK = 512
VMEM_LIMIT = 56 * 1024 * 1024
SC_CORES = 2
SC_SUBCORES = 16
SC_GATHER_ROWS = 128

_DN_T = (((1,), (1,)), ((), ()))
_DN_TA = (((0,), (0,)), ((), ()))


def _params(sem):
    return pltpu.CompilerParams(dimension_semantics=sem, vmem_limit_bytes=VMEM_LIMIT)


def _ln_rows(v, g, b):
    mu = jnp.mean(v, axis=-1, keepdims=True)
    d = v - mu
    var = jnp.mean(d * d, axis=-1, keepdims=True)
    return d * lax.rsqrt(var + LN_EPS) * g + b


_SEGS = (('q', 512, BF16), ('cmp', 256, F32), ('ksel', 128, BF16), ('vsel', 128, BF16),
         ('kwin', 128, BF16), ('vwin', 128, BF16), ('gates', 128, F32), ('mq', 512, BF16),
         ('mk', 512, BF16), ('mv', 512, BF16), ('mo', 512, BF16), ('mif', 128, F32))


def _proj_kernel(x_ref, w_ref, *out_refs):
    xb = x_ref[...].astype(BF16)
    off = 0
    for o_ref, (_, wd, _) in zip(out_refs, _SEGS):
        o_ref[...] = jnp.dot(xb, w_ref[:, off:off + wd], preferred_element_type=F32).astype(o_ref.dtype)
        off += wd


def _prep_w_in(w):
    sizes = (512,) + (128,) * 6 + (24,) + (512,) * 4 + (4, 4)
    cuts = np.cumsum(sizes)[:-1].tolist()
    (wq, kc, vc, ks, vs, kw, vw, wg, mq, mk, mv, mo, mi, mf) = jnp.split(w, cuts, axis=1)
    wq = wq.reshape(D_MODEL, NSA_GROUPS, NSA_HPG, NSA_DK).transpose(0, 2, 1, 3).reshape(D_MODEL, 512)
    pad = lambda a: jnp.pad(a, ((0, 0), (0, LANES - a.shape[1])))
    segs = [wq, kc, vc, ks, vs, kw, vw, pad(wg), mq, mk, mv, mo, pad(jnp.concatenate([mi, mf], axis=1))]
    return jnp.concatenate(segs, axis=1).astype(BF16)


def _project(x2d, w_all, tm):
    T = x2d.shape[0]
    n = w_all.shape[1]
    out_shape = tuple(jax.ShapeDtypeStruct((T, wd), dt) for _, wd, dt in _SEGS)
    out_specs = tuple(pl.BlockSpec((tm, wd), lambda i: (i, 0)) for _, wd, _ in _SEGS)
    return pl.pallas_call(
        _proj_kernel, grid=(T // tm,),
        in_specs=[pl.BlockSpec((tm, D_MODEL), lambda i: (i, 0)),
                  pl.BlockSpec((D_MODEL, n), lambda i: (0, 0))],
        out_specs=out_specs, out_shape=out_shape,
        compiler_params=_params(("parallel",)), name="proj")(x2d, w_all)


def _cmp_kernel(r_ref, pa_ref, pb_ref, wa_ref, wb_ref, w2_ref, o_ref):
    r = r_ref[...]
    a = jnp.dot((r + pa_ref[...]).astype(BF16), wa_ref[...], preferred_element_type=F32)
    b = jnp.dot((r + pb_ref[...]).astype(BF16), wb_ref[...], preferred_element_type=F32)
    nr = r.shape[0]
    hid = a + pltpu.roll(b, nr - 1, 0)
    hid = hid * jax.nn.sigmoid(hid)
    out = jnp.dot(hid.astype(BF16), w2_ref[...], preferred_element_type=F32)
    row = lax.broadcasted_iota(I32, out.shape, 0)
    o_ref[...] = jnp.where(row < nr - 1, out, 0.0).astype(o_ref.dtype)


def _prep_cmp(pos_k, w1_k, w2_k, pos_v, w1_v, w2_v):
    eye = jnp.eye(NSA_GROUPS, dtype=F32)

    def expand_w1(w1, half):
        w = w1.reshape(NSA_CMP_LEN, NSA_DK, NSA_DK)[half * 16:(half + 1) * 16]
        return jnp.einsum('jde,gk->jgdke', w, eye).reshape(16, 128, 128)

    def both(fk, fv):
        z = jnp.zeros_like(fk)
        top = jnp.concatenate([fk, z], axis=-1)
        bot = jnp.concatenate([z, fv], axis=-1)
        return jnp.concatenate([top, bot], axis=-2)

    wa = both(expand_w1(w1_k, 0), expand_w1(w1_v, 0)).reshape(16 * 256, 256).astype(BF16)
    wb = both(expand_w1(w1_k, 1), expand_w1(w1_v, 1)).reshape(16 * 256, 256).astype(BF16)
    w2 = both(jnp.kron(eye, w2_k), jnp.kron(eye, w2_v)).astype(BF16)

    def pos_row(half):
        pk = jnp.tile(pos_k[half * 16:(half + 1) * 16], (1, NSA_GROUPS))
        pv = jnp.tile(pos_v[half * 16:(half + 1) * 16], (1, NSA_GROUPS))
        return jnp.concatenate([pk, pv], axis=1).reshape(1, 16 * 256)

    return pos_row(0), pos_row(1), wa, wb, w2


def _compress(cmp2d, B, S, prep):
    pa, pb, wa, wb, w2 = prep
    nr = S // NSA_CMP_STRIDE
    r = cmp2d.reshape(B, nr, NSA_CMP_STRIDE * 256)
    full = lambda a: pl.BlockSpec(a.shape, lambda b: (0,) * a.ndim)
    return pl.pallas_call(
        _cmp_kernel, grid=(B,),
        in_specs=[pl.BlockSpec((None, nr, NSA_CMP_STRIDE * 256), lambda b: (b, 0, 0)),
                  full(pa), full(pb), full(wa), full(wb), full(w2)],
        out_specs=pl.BlockSpec((None, nr, 256), lambda b: (b, 0, 0)),
        out_shape=jax.ShapeDtypeStruct((B, nr, 256), BF16),
        compiler_params=_params(("parallel",)), name="cmp")(r, pa, pb, wa, wb, w2)


def _nsa_consts(S):
    n_cmp = (S - NSA_CMP_LEN) // NSA_CMP_STRIDE + 1
    n_sel = S // NSA_SEL_BLOCK
    cs = np.arange(n_cmp) * NSA_CMP_STRIDE
    ss = np.arange(n_sel) * NSA_SEL_BLOCK
    ov = ((cs[:, None] < ss[None, :] + NSA_SEL_BLOCK) & (cs[:, None] + NSA_CMP_LEN > ss[None, :]))
    ovt = np.zeros((LANES, S // NSA_CMP_STRIDE), np.float32)
    ovt[:n_sel, :n_cmp] = ov.T
    e = np.zeros((LANES, S), np.float32)
    e[np.arange(S) // NSA_SEL_BLOCK, np.arange(S)] = 1.0
    return jnp.asarray(ovt, BF16), jnp.asarray(e, BF16)


def _nsa_kernel(q_ref, kcvc_ref, ksel_ref, vsel_ref, kwin_ref, vwin_ref, gates_ref, ovt_ref, e_ref,
                y_ref, bias_ref, *, tq, ck, n_sel):
    G, H = NSA_GROUPS, NSA_HPG
    GH = G * H
    M = GH * tq
    S = ksel_ref.shape[0]
    W = NSA_WINDOW
    ws = min(W + tq, S)
    t0 = pl.program_id(1) * tq
    gates = jax.nn.sigmoid(gates_ref[...])
    lane = lax.broadcasted_iota(I32, (tq, LANES), 1)
    t_col = t0 + lax.broadcasted_iota(I32, (tq, 1), 0)
    kc = kcvc_ref[:, 0:LANES]
    vc = kcvc_ref[:, LANES:2 * LANES]
    nc = kc.shape[0]
    qs = []
    for g in range(G):
        gmask = (lane // NSA_DK) == g
        for h in range(H):
            qh = q_ref[:, h * LANES:(h + 1) * LANES] * (NSA_DK ** -0.5)
            qs.append(jnp.where(gmask, qh, jnp.zeros_like(qh)))
    Q = jnp.concatenate(qs, axis=0).astype(BF16)

    s = lax.dot_general(Q, kc, _DN_T, preferred_element_type=F32)
    c_idx = lax.broadcasted_iota(I32, (tq, nc), 1)
    cmask = (c_idx * NSA_CMP_STRIDE + NSA_CMP_LEN - 1) <= t_col
    s3 = jnp.where(cmask[None], s.reshape(GH, tq, nc), NEG)
    p = jnp.exp(s3 - jnp.max(s3, axis=-1, keepdims=True))
    p = p / jnp.sum(p, axis=-1, keepdims=True)
    p = jnp.where(cmask[None], p, 0.0)
    o_cmp = jnp.dot(p.reshape(M, nc).astype(BF16), vc, preferred_element_type=F32).reshape(GH, tq, LANES)

    for g in range(G):
        psum = jnp.sum(p[g * H:(g + 1) * H], axis=0)
        hi = psum.astype(BF16)
        lo = (psum - hi.astype(F32)).astype(BF16)
        ovt = ovt_ref[...]
        pslt = (lax.dot_general(ovt, hi, _DN_T, preferred_element_type=F32) +
                lax.dot_general(ovt, lo, _DN_T, preferred_element_type=F32))
        imp_p = pslt[0:n_sel, :]
        n_i = lax.broadcasted_iota(I32, (n_sel, tq), 0)
        cur = (t0 + lax.broadcasted_iota(I32, (n_sel, tq), 1)) // NSA_SEL_BLOCK
        forced = (n_i == 0) | (n_i == cur) | (n_i == cur - 1)
        imp = jnp.where(n_i <= cur, imp_p + jnp.where(forced, FORCE_BONUS, 0.0), NEG)
        cnt = jnp.zeros((n_sel, tq), F32)
        for m in range(n_sel):
            row = imp[m:m + 1, :]
            beats = (row > imp) | ((row == imp) & (n_i > m))
            cnt = cnt + jnp.where(beats, 1.0, 0.0)
        selt = jnp.where(cnt < float(min(NSA_SEL_TOPN, n_sel)), 1.0, 0.0)
        selt = jnp.concatenate([selt, jnp.zeros((LANES - n_sel, tq), F32)], axis=0)
        sel = selt.T.astype(BF16)
        maskf = jnp.dot(sel, e_ref[...], preferred_element_type=F32)
        kpos = lax.broadcasted_iota(I32, (tq, S), 1)
        bias_ref[g] = jnp.where((maskf > 0.5) & (kpos <= t_col), 0.0, NEG)

    def sel_body(j, carry):
        m_i, l_i, acc = carry
        ks = pl.multiple_of(j * ck, ck)
        k = ksel_ref[pl.ds(ks, ck), :]
        v = vsel_ref[pl.ds(ks, ck), :]
        sj = lax.dot_general(Q, k, _DN_T, preferred_element_type=F32)
        sj = (sj.reshape(G, H, tq, ck) + bias_ref[:, :, pl.ds(ks, ck)][:, None]).reshape(GH, tq, ck)
        m_new = jnp.maximum(m_i, jnp.max(sj, axis=-1, keepdims=True))
        a = jnp.exp(m_i - m_new)
        pj = jnp.exp(sj - m_new)
        l_new = a * l_i + jnp.sum(pj, axis=-1, keepdims=True)
        pv = jnp.dot(pj.reshape(M, ck).astype(BF16), v, preferred_element_type=F32)
        return m_new, l_new, a * acc + pv.reshape(GH, tq, LANES)

    init = (jnp.full((GH, tq, 1), NEG, F32), jnp.zeros((GH, tq, 1), F32), jnp.zeros((GH, tq, LANES), F32))
    _, l_f, acc = lax.fori_loop(0, (t0 + tq + ck - 1) // ck, sel_body, init)
    o_sel = acc / l_f

    kst = pl.multiple_of(jnp.clip(t0 - W, 0, S - ws), LANES)
    kwn = kwin_ref[pl.ds(kst, ws), :]
    vwn = vwin_ref[pl.ds(kst, ws), :]
    sw = lax.dot_general(Q, kwn, _DN_T, preferred_element_type=F32)
    wpos = kst + lax.broadcasted_iota(I32, (tq, ws), 1)
    wmask = (wpos <= t_col) & (wpos > t_col - W)
    sw3 = jnp.where(wmask[None], sw.reshape(GH, tq, ws), NEG)
    pw = jnp.exp(sw3 - jnp.max(sw3, axis=-1, keepdims=True))
    lw = jnp.sum(pw, axis=-1, keepdims=True)
    o_win = jnp.dot(pw.reshape(M, ws).astype(BF16), vwn, preferred_element_type=F32).reshape(GH, tq, LANES) / lw

    g0mask = lane < NSA_DK
    for h in range(H):
        o_g = []
        for g in range(G):
            r = g * H + h
            c0 = r * 3
            o_g.append(gates[:, c0:c0 + 1] * o_cmp[r] + gates[:, c0 + 1:c0 + 2] * o_sel[r] +
                       gates[:, c0 + 2:c0 + 3] * o_win[r])
        y_ref[:, h * LANES:(h + 1) * LANES] = jnp.where(g0mask, o_g[0], o_g[1]).astype(y_ref.dtype)


def _nsa(q, kcvc, ksel, vsel, kwin, vwin, gates, B, S, tq, ck):
    T = B * S
    nq = S // tq
    ovt, e = _nsa_consts(S)
    seq = lambda a: a.reshape(B, S, LANES)
    kv_spec = pl.BlockSpec((None, S, LANES), lambda b, i: (b, 0, 0))
    kern = functools.partial(_nsa_kernel, tq=tq, ck=ck, n_sel=S // NSA_SEL_BLOCK)
    return pl.pallas_call(
        kern, grid=(B, nq),
        in_specs=[pl.BlockSpec((tq, 512), lambda b, i: (b * nq + i, 0)),
                  pl.BlockSpec((None,) + kcvc.shape[1:], lambda b, i: (b, 0, 0)),
                  kv_spec, kv_spec, kv_spec, kv_spec,
                  pl.BlockSpec((tq, LANES), lambda b, i: (b * nq + i, 0)),
                  pl.BlockSpec(ovt.shape, lambda b, i: (0, 0)),
                  pl.BlockSpec(e.shape, lambda b, i: (0, 0))],
        out_specs=pl.BlockSpec((tq, 512), lambda b, i: (b * nq + i, 0)),
        out_shape=jax.ShapeDtypeStruct((T, 512), BF16),
        scratch_shapes=[pltpu.VMEM((NSA_GROUPS, tq, S), F32)],
        compiler_params=_params(("parallel", "parallel")), name="nsa",
    )(q, kcvc, seq(ksel), seq(vsel), seq(kwin), seq(vwin), gates, ovt, e)


def _mlstm_kernel(q_ref, k_ref, v_ref, o_ref, gn_ref, gt_ref, cw_ref, bn_ref, bt_ref, ng_ref, tri_ref,
                  y_ref, c_scr, n_scr):
    H, dh, L = ML_HEADS, ML_DH, ML_CHUNK
    nb, S = q_ref.shape[0], q_ref.shape[1]
    nchunk = S // L
    c_scr[...] = jnp.zeros_like(c_scr)
    n_scr[...] = jnp.zeros_like(n_scr)
    row = lax.broadcasted_iota(I32, (L, H * dh), 0)
    li = lax.broadcasted_iota(I32, (L, L), 0)
    mi = lax.broadcasted_iota(I32, (L, L), 1)
    causal = mi <= li
    tril = tri_ref[0]
    triu = tri_ref[1]
    hp = lax.Precision.HIGHEST

    def conv_silu(ref, bi, c, wofs):
        r0 = pl.multiple_of(c * L, L)
        rp = pl.multiple_of(jnp.maximum(c - 1, 0) * L, L)
        cur = ref[bi, pl.ds(r0, L), :].astype(F32)
        prev = ref[bi, pl.ds(rp, L), :].astype(F32) * jnp.where(c > 0, 1.0, 0.0)
        acc = cur * cw_ref[ML_CONV - 1:ML_CONV, wofs:wofs + H * dh]
        for j in range(1, ML_CONV):
            sh = jnp.where(row < j, pltpu.roll(prev, j, 0), pltpu.roll(cur, j, 0))
            acc = acc + sh * cw_ref[ML_CONV - 1 - j:ML_CONV - j, wofs:wofs + H * dh]
        return acc * jax.nn.sigmoid(acc)

    def body(c, m_state):
        r0 = pl.multiple_of(c * L, L)
        new_m = []
        for bi in range(nb):
            qa = conv_silu(q_ref, bi, c, 0) * (dh ** -0.5)
            ka = conv_silu(k_ref, bi, c, H * dh)
            va = v_ref[bi, pl.ds(r0, L), :]
            oa = o_ref[bi, pl.ds(r0, L), :].astype(F32)
            gn = gn_ref[bi, pl.ds(r0, L), :] + bn_ref[...]
            gt = gt_ref[bi, :, c, :] + bt_ref[...]
            lf_n = jax.nn.log_sigmoid(gn)
            lf_t = jax.nn.log_sigmoid(gt)
            b_n = jnp.dot(tril, lf_n, precision=hp, preferred_element_type=F32)
            b_t = jnp.dot(lf_t, triu, precision=hp, preferred_element_type=F32)
            for h in range(H):
                st = bi * H + h
                q = qa[:, h * dh:(h + 1) * dh]
                k = ka[:, h * dh:(h + 1) * dh]
                v = va[:, h * dh:(h + 1) * dh]
                m_old = m_state[st]
                b_col = b_n[:, H + h:H + h + 1]
                i_col = gn[:, h:h + 1]
                b_row = b_t[H + h:H + h + 1, :]
                i_row = gt[h:h + 1, :]
                g_tot = b_t[H + h:H + h + 1, L - 1:L]
                d_log = jnp.where(causal, b_col - b_row + i_row, NEG)
                inter = b_col + m_old
                m_q = jnp.maximum(inter, jnp.max(d_log, axis=-1, keepdims=True))
                w_intra = jnp.exp(d_log - m_q)
                w_inter = jnp.exp(inter - m_q)
                qb = q.astype(BF16)
                s = lax.dot_general(qb, k.astype(BF16), _DN_T, preferred_element_type=F32) * w_intra
                cst = c_scr[st]
                nst = n_scr[st]
                num = (w_inter * jnp.dot(qb, cst.astype(BF16), preferred_element_type=F32) +
                       jnp.dot(s.astype(BF16), v, preferred_element_type=F32))
                den = w_inter * jnp.sum(q * nst, axis=-1, keepdims=True) + jnp.sum(s, axis=-1, keepdims=True)
                hv = num / jnp.maximum(jnp.abs(den), jnp.exp(-m_q))
                log_k = g_tot - b_col + i_col
                m_new = jnp.maximum(g_tot + m_old, jnp.max(log_k, axis=0, keepdims=True))
                wk = jnp.exp(log_k - m_new)
                decay = jnp.exp(g_tot + m_old - m_new)
                kw = k * wk
                c_scr[st] = decay * cst + lax.dot_general(kw.astype(BF16), v, _DN_TA, preferred_element_type=F32)
                n_scr[st] = decay * nst + jnp.sum(kw, axis=0, keepdims=True)
                new_m.append(m_new)
                mu = jnp.mean(hv, axis=-1, keepdims=True)
                dv = hv - mu
                var = jnp.mean(dv * dv, axis=-1, keepdims=True)
                hn = dv * lax.rsqrt(var + LN_EPS) * ng_ref[:, h * dh:(h + 1) * dh]
                og = jax.nn.sigmoid(oa[:, h * dh:(h + 1) * dh])
                y_ref[bi, pl.ds(r0, L), h * dh:(h + 1) * dh] = (og * hn).astype(y_ref.dtype)
        return tuple(new_m)

    lax.fori_loop(0, nchunk, body, tuple(jnp.zeros((1, 1), F32) for _ in range(nb * H)))


def _mlstm(mq, mk, mv, mo, mif, conv_w, i_bias, f_bias, norm_g, B, S, nb):
    T = B * S
    H, dh, L = ML_HEADS, ML_DH, ML_CHUNK
    W = H * dh
    gt = mif[:, :2 * H].reshape(B, S, 2 * H).transpose(0, 2, 1).reshape(B, 2 * H, S // L, L)
    cw = conv_w.reshape(ML_CONV, 2 * W)
    bias = jnp.concatenate([i_bias, f_bias])
    bn = jnp.pad(bias, (0, LANES - 2 * H)).reshape(1, LANES)
    bt = bias.reshape(2 * H, 1)
    ng = norm_g.reshape(1, W)
    tri = jnp.stack([jnp.tril(jnp.ones((L, L), F32)), jnp.triu(jnp.ones((L, L), F32))])
    seq = lambda a: a.reshape(B, S, a.shape[1])
    rows = lambda w: pl.BlockSpec((nb, S, w), lambda b: (b, 0, 0))
    full = lambda a: pl.BlockSpec(a.shape, lambda b: (0,) * a.ndim)
    y = pl.pallas_call(
        _mlstm_kernel, grid=(B // nb,),
        in_specs=[rows(W), rows(W), rows(W), rows(W), rows(LANES),
                  pl.BlockSpec((nb, 2 * H, S // L, L), lambda b: (b, 0, 0, 0)),
                  full(cw), full(bn), full(bt), full(ng), full(tri)],
        out_specs=rows(W),
        out_shape=jax.ShapeDtypeStruct((B, S, W), BF16),
        scratch_shapes=[pltpu.VMEM((nb * H, dh, dh), F32), pltpu.VMEM((nb * H, 1, dh), F32)],
        compiler_params=_params(("parallel",)), name="mlstm",
    )(seq(mq), seq(mk), seq(mv), seq(mo), seq(mif), gt, cw, bn, bt, ng, tri)
    return y.reshape(T, W)


def _outproj_kernel(yn_ref, ym_ref, x_ref, w_ref, g_ref, b_ref, o_ref):
    mix = (jnp.dot(yn_ref[...], w_ref[0:512, :], preferred_element_type=F32) +
           jnp.dot(ym_ref[...], w_ref[512:1024, :], preferred_element_type=F32))
    o_ref[...] = _ln_rows(ALPHA * x_ref[...] + mix, g_ref[...], b_ref[...])


def _outproj(y_nsa, y_ml, x2d, w_out, g, b, tm):
    T = x2d.shape[0]
    wn = w_out[:512].reshape(NSA_GROUPS, NSA_HPG, NSA_DK, D_MODEL).transpose(1, 0, 2, 3).reshape(512, D_MODEL)
    w = jnp.concatenate([wn, w_out[512:]], axis=0).astype(BF16)
    row = lambda wd: pl.BlockSpec((tm, wd), lambda i: (i, 0))
    full = lambda a: pl.BlockSpec(a.shape, lambda i: (0,) * a.ndim)
    g2, b2 = g.reshape(1, -1), b.reshape(1, -1)
    return pl.pallas_call(
        _outproj_kernel, grid=(T // tm,),
        in_specs=[row(512), row(512), row(D_MODEL), full(w), full(g2), full(b2)],
        out_specs=row(D_MODEL), out_shape=jax.ShapeDtypeStruct((T, D_MODEL), F32),
        compiler_params=_params(("parallel",)), name="outproj")(y_nsa, y_ml, x2d, w, g2, b2)


def _memkv_kernel(m_ref, w_ref, o_ref):
    o_ref[...] = jnp.dot(m_ref[...].astype(BF16), w_ref[...], preferred_element_type=F32).astype(o_ref.dtype)


def _memkv(mem2d, wk, wv):
    w = jnp.concatenate([wk, wv], axis=1).astype(BF16)
    R = mem2d.shape[0]
    return pl.pallas_call(
        _memkv_kernel, grid=(R // MEM_LEN,),
        in_specs=[pl.BlockSpec((MEM_LEN, D_MODEL), lambda i: (i, 0)),
                  pl.BlockSpec(w.shape, lambda i: (0, 0))],
        out_specs=pl.BlockSpec((MEM_LEN, 2 * D_MODEL), lambda i: (i, 0)),
        out_shape=jax.ShapeDtypeStruct((R, 2 * D_MODEL), BF16),
        compiler_params=_params(("parallel",)), name="memkv")(mem2d, w)


def _xattn_kernel(x_ref, kv_ref, wq_ref, wo_ref, g_ref, b_ref, rw_ref, x2_ref, x2p_ref, sc_ref):
    x1 = x_ref[...]
    q = jnp.dot(x1.astype(BF16), wq_ref[...], preferred_element_type=F32).astype(BF16)
    outs = []
    for h in range(XA_HEADS):
        qh = q[:, h * XA_DH:(h + 1) * XA_DH]
        kh = kv_ref[:, h * XA_DH:(h + 1) * XA_DH]
        vh = kv_ref[:, D_MODEL + h * XA_DH:D_MODEL + (h + 1) * XA_DH]
        s = lax.dot_general(qh, kh, _DN_T, preferred_element_type=F32) * (XA_DH ** -0.5)
        p = jnp.exp(s - jnp.max(s, axis=-1, keepdims=True))
        p = p / jnp.sum(p, axis=-1, keepdims=True)
        outs.append(jnp.dot(p.astype(BF16), vh, preferred_element_type=F32).astype(BF16))
    o = jnp.concatenate(outs, axis=1)
    xa = jnp.dot(o, wo_ref[...], preferred_element_type=F32)
    x2 = _ln_rows(ALPHA * x1 + xa, g_ref[...], b_ref[...])
    x2_ref[...] = x2
    x2p_ref[...] = _pack_bf16_pairs(x2)
    xh = x2.astype(BF16)
    xl = (x2 - xh.astype(F32)).astype(BF16)
    wh = rw_ref[0]
    wl = rw_ref[1]
    logit = (lax.dot_general(wh, xh, _DN_T, preferred_element_type=F32) +
             lax.dot_general(wh, xl, _DN_T, preferred_element_type=F32) +
             lax.dot_general(wl, xh, _DN_T, preferred_element_type=F32))
    sc_ref[...] = jax.nn.sigmoid(logit)


def _xattn(x1, kv, wq, wo, g, b, router_w, S, tq):
    T = x1.shape[0]
    wqb, wob = wq.astype(BF16), wo.astype(BF16)
    rwt = router_w.T
    rh = rwt.astype(BF16)
    rw = jnp.stack([rh, (rwt - rh.astype(F32)).astype(BF16)])
    g2, b2 = g.reshape(1, -1), b.reshape(1, -1)
    full = lambda a: pl.BlockSpec(a.shape, lambda i: (0,) * a.ndim)
    per = S // tq
    return pl.pallas_call(
        _xattn_kernel, grid=(T // tq,),
        in_specs=[pl.BlockSpec((tq, D_MODEL), lambda i: (i, 0)),
                  pl.BlockSpec((MEM_LEN, 2 * D_MODEL), lambda i: (i // per, 0)),
                  full(wqb), full(wob), full(g2), full(b2), full(rw)],
        out_specs=(pl.BlockSpec((tq, D_MODEL), lambda i: (i, 0)),
                   pl.BlockSpec((tq, D_MODEL // 2), lambda i: (i, 0)),
                   pl.BlockSpec((MOE_E, tq), lambda i: (0, i))),
        out_shape=(jax.ShapeDtypeStruct((T, D_MODEL), F32), jax.ShapeDtypeStruct((T, D_MODEL // 2), jnp.uint32),
                   jax.ShapeDtypeStruct((MOE_E, T), F32)),
        compiler_params=_params(("parallel",)), name="xattn")(x1, kv, wqb, wob, g2, b2, rw)


def _route_kernel(sc_ref, rb_ref, idx_ref, w_ref):
    E, G = MOE_E, MOE_GROUPS
    per = E // G
    scores = sc_ref[...]
    tr = scores.shape[1]
    biased = scores + rb_ref[...]
    g3 = biased.reshape(G, per, tr)
    j3 = lax.broadcasted_iota(I32, (G, per, tr), 1)
    m1 = jnp.max(g3, axis=1, keepdims=True)
    first = jnp.min(jnp.where(g3 == m1, j3, per), axis=1, keepdims=True)
    m2 = jnp.max(jnp.where(j3 == first, -jnp.inf, g3), axis=1, keepdims=True)
    gs = (m1 + m2).reshape(G, tr)
    gi = lax.broadcasted_iota(I32, (G, tr), 0)
    cnt = jnp.zeros((G, tr), F32)
    for m in range(G):
        row = gs[m:m + 1, :]
        cnt = cnt + jnp.where((row > gs) | ((row == gs) & (gi > m)), 1.0, 0.0)
    gmask = cnt < float(MOE_TOPK_GROUPS)
    masked = jnp.where(gmask[:, None, :], g3, NEG).reshape(E, tr)
    ei = lax.broadcasted_iota(I32, (E, tr), 0)
    idxs, ws = [], []
    for _ in range(MOE_K):
        mx = jnp.max(masked, axis=0, keepdims=True)
        ix = jnp.min(jnp.where(masked == mx, ei, E), axis=0, keepdims=True)
        hit = ei == ix
        ws.append(jnp.sum(jnp.where(hit, scores, 0.0), axis=0, keepdims=True))
        idxs.append(ix)
        masked = jnp.where(hit, -jnp.inf, masked)
    w = jnp.concatenate(ws, axis=0)
    idx_ref[...] = jnp.concatenate(idxs, axis=0)
    w_ref[...] = w / jnp.sum(w, axis=0, keepdims=True) * MOE_ROUTE_SCALE


def _route(scores_t, router_bias, tr):
    E, T = scores_t.shape
    rb = router_bias.reshape(E, 1)
    return pl.pallas_call(
        _route_kernel, grid=(T // tr,),
        in_specs=[pl.BlockSpec((E, tr), lambda i: (0, i)), pl.BlockSpec((E, 1), lambda i: (0, 0))],
        out_specs=(pl.BlockSpec((MOE_K, tr), lambda i: (0, i)), pl.BlockSpec((MOE_K, tr), lambda i: (0, i))),
        out_shape=(jax.ShapeDtypeStruct((MOE_K, T), I32), jax.ShapeDtypeStruct((MOE_K, T), F32)),
        compiler_params=_params(("parallel",)), name="route")(scores_t, rb)


def _rank_kernel(idx_ref, u_ref, rank_ref, cnt_ref, carry):
    E = MOE_E

    @pl.when(pl.program_id(0) == 0)
    def _():
        carry[...] = jnp.zeros_like(carry)

    idx = idx_ref[...]
    tp = idx.shape[1]
    ei = lax.broadcasted_iota(I32, (E, tp), 0)
    hits = [ei == idx[k:k + 1, :] for k in range(MOE_K)]
    onehot = jnp.zeros((E, tp), F32)
    for hit in hits:
        onehot = onehot + jnp.where(hit, 1.0, 0.0)
    pos = jnp.dot(onehot.astype(BF16), u_ref[...], preferred_element_type=F32) + carry[...]
    ranks = [jnp.sum(jnp.where(hit, pos, 0.0), axis=0, keepdims=True) for hit in hits]
    rank_ref[...] = jnp.concatenate(ranks, axis=0).astype(I32)
    total = carry[...] + jnp.sum(onehot, axis=1, keepdims=True)
    carry[...] = total
    cnt_ref[...] = jnp.broadcast_to(total, cnt_ref.shape).astype(I32)


def _rank(idx_t, tp):
    K, T = idx_t.shape
    u = jnp.triu(jnp.ones((tp, tp), F32), k=1).astype(BF16)
    rank, cnt = pl.pallas_call(
        _rank_kernel, grid=(T // tp,),
        in_specs=[pl.BlockSpec((K, tp), lambda i: (0, i)), pl.BlockSpec((tp, tp), lambda i: (0, 0))],
        out_specs=(pl.BlockSpec((K, tp), lambda i: (0, i)), pl.BlockSpec((MOE_E, LANES), lambda i: (0, 0))),
        out_shape=(jax.ShapeDtypeStruct((K, T), I32), jax.ShapeDtypeStruct((MOE_E, LANES), I32)),
        scratch_shapes=[pltpu.VMEM((MOE_E, 1), F32)],
        compiler_params=_params(("arbitrary",)), name="rank")(idx_t, u)
    return rank, cnt[:, 0]


def _dest_kernel(idx_ref, rank_ref, po_ref, dest_ref):
    idx = idx_ref[...]
    tp = idx.shape[1]
    ei = lax.broadcasted_iota(I32, (MOE_E, tp), 0)
    po = po_ref[...]
    base = [jnp.sum(jnp.where(ei == idx[k:k + 1, :], po, 0.0), axis=0, keepdims=True) for k in range(MOE_K)]
    dest_ref[...] = jnp.concatenate(base, axis=0).astype(I32) + rank_ref[...]


def _dest(idx_t, rank_t, poffs, tp):
    K, T = idx_t.shape
    po = poffs.astype(F32).reshape(MOE_E, 1)
    spec = pl.BlockSpec((K, tp), lambda i: (0, i))
    return pl.pallas_call(
        _dest_kernel, grid=(T // tp,),
        in_specs=[spec, spec, pl.BlockSpec((MOE_E, 1), lambda i: (0, 0))],
        out_specs=spec, out_shape=jax.ShapeDtypeStruct((K, T), I32),
        compiler_params=_params(("parallel",)), name="dest")(idx_t, rank_t, po)


def _pack_bf16_pairs(v):
    m = v.shape[1] // 2
    bits = lax.bitcast_convert_type(v.astype(BF16).astype(F32), jnp.uint32)
    return (bits[:, :m] >> 16) | (bits[:, m:] & jnp.uint32(0xFFFF0000))


def _unpack_bf16_pairs(w):
    lo = lax.bitcast_convert_type(w << 16, F32)
    hi = lax.bitcast_convert_type(w & jnp.uint32(0xFFFF0000), F32)
    return lo, hi


def _scatter_kernel(dest_ref, x_ref, xs_ref, sem):
    ts = x_ref.shape[0]

    def copy(t, k):
        return pltpu.make_async_copy(x_ref.at[pl.ds(t, 1)], xs_ref.at[pl.ds(dest_ref[t * MOE_K + k], 1)], sem)

    def start(t, c):
        for k in range(MOE_K):
            copy(t, k).start(priority=k % 2)
        return c

    def wait(t, c):
        for k in range(MOE_K):
            copy(t, k).wait()
        return c

    lax.fori_loop(0, ts, start, 0)
    lax.fori_loop(0, ts, wait, 0)


def _scatter(dest, x2p, n_rows, ts):
    T, wp = x2p.shape
    return pl.pallas_call(
        _scatter_kernel, grid=(T // ts,),
        in_specs=[pl.BlockSpec((MOE_K * ts,), lambda i: (i,), memory_space=pltpu.SMEM),
                  pl.BlockSpec((ts, wp), lambda i: (i, 0))],
        out_specs=pl.BlockSpec(memory_space=pl.ANY),
        out_shape=jax.ShapeDtypeStruct((n_rows, wp), jnp.uint32),
        scratch_shapes=[pltpu.SemaphoreType.DMA(())],
        compiler_params=_params(("arbitrary",)), name="scatter")(dest, x2p)


def _expert_kernel(be_ref, nv_ref, xs_ref, w1_ref, w3_ref, w2_ref, ys_ref):
    i = pl.program_id(0)
    nv = nv_ref[i]
    hw = D_MODEL // 2

    @pl.when(nv > 0)
    def _():
        row = lax.broadcasted_iota(I32, xs_ref.shape, 0)
        lo, hi = _unpack_bf16_pairs(jnp.where(row < nv, xs_ref[...], jnp.uint32(0)))
        lo, hi = lo.astype(BF16), hi.astype(BF16)
        w1 = w1_ref[...].astype(BF16)
        w3 = w3_ref[...].astype(BF16)
        a = (jnp.dot(lo, w1[:hw], preferred_element_type=F32) + jnp.dot(hi, w1[hw:], preferred_element_type=F32))
        c = (jnp.dot(lo, w3[:hw], preferred_element_type=F32) + jnp.dot(hi, w3[hw:], preferred_element_type=F32))
        h = (a * jax.nn.sigmoid(a) * c).astype(BF16)
        ys_ref[...] = _pack_bf16_pairs(jnp.dot(h, w2_ref[...].astype(BF16), preferred_element_type=F32))

    @pl.when(nv <= 0)
    def _():
        ys_ref[...] = jnp.zeros_like(ys_ref)


def _experts(blk_exp, blk_valid, xs, w1, w3, w2):
    n_rows, wp = xs.shape
    rb = ROW_BLOCK
    gs = pltpu.PrefetchScalarGridSpec(
        num_scalar_prefetch=2, grid=(n_rows // rb,),
        in_specs=[pl.BlockSpec((rb, wp), lambda i, be, nv: (i, 0)),
                  pl.BlockSpec((None, D_MODEL, MOE_FF), lambda i, be, nv: (be[i], 0, 0)),
                  pl.BlockSpec((None, D_MODEL, MOE_FF), lambda i, be, nv: (be[i], 0, 0)),
                  pl.BlockSpec((None, MOE_FF, D_MODEL), lambda i, be, nv: (be[i], 0, 0))],
        out_specs=pl.BlockSpec((rb, wp), lambda i, be, nv: (i, 0)))
    return pl.pallas_call(
        _expert_kernel, grid_spec=gs, out_shape=jax.ShapeDtypeStruct((n_rows, wp), jnp.uint32),
        compiler_params=_params(("arbitrary",)), name="experts")(blk_exp, blk_valid, xs, w1, w3, w2)


def _row_gather(table, idx):
    n, d = idx.shape[0], table.shape[1]
    nw = SC_CORES * SC_SUBCORES
    per_w = n // nw
    ch = SC_GATHER_ROWS
    mesh = plsc.VectorSubcoreMesh(core_axis_name="c", subcore_axis_name="s")

    @functools.partial(
        pl.kernel, mesh=mesh, out_type=jax.ShapeDtypeStruct((n, d), I32),
        scratch_types=[pltpu.VMEM((ch,), I32), pltpu.VMEM((ch, d), I32), pltpu.SemaphoreType.DMA],
        name="row_gather")
    def gather(table_hbm, idx_hbm, out_hbm, idx_v, rows_v, sem):
        wid = lax.axis_index("s") * SC_CORES + lax.axis_index("c")
        base = wid * per_w

        @pl.loop(0, per_w // ch)
        def _(i):
            off = pl.multiple_of(base + i * ch, ch)
            pltpu.sync_copy(idx_hbm.at[pl.ds(off, ch)], idx_v)
            pltpu.async_copy(table_hbm.at[idx_v], rows_v, sem).wait()
            pltpu.sync_copy(rows_v, out_hbm.at[pl.ds(off, ch)])

    return gather(table, idx)


def _combine_kernel(w_ref, x_ref, ysg_ref, s1_ref, s3_ref, s2_ref, g_ref, b_ref, o_ref):
    tc = x_ref.shape[0]
    x2 = x_ref[...]
    xb = x2.astype(BF16)
    a = jnp.dot(xb, s1_ref[...], preferred_element_type=F32)
    c = jnp.dot(xb, s3_ref[...], preferred_element_type=F32)
    shared = jnp.dot((a * jax.nn.sigmoid(a) * c).astype(BF16), s2_ref[...], preferred_element_type=F32)
    lo, hi = _unpack_bf16_pairs(lax.bitcast_convert_type(ysg_ref[...], jnp.uint32))
    w = w_ref[...]
    hw = lo.shape[1]
    y_lo = jnp.sum((w * lo).reshape(tc, MOE_K, hw), axis=1)
    y_hi = jnp.sum((w * hi).reshape(tc, MOE_K, hw), axis=1)
    y = shared + jnp.concatenate([y_lo, y_hi], axis=1)
    o_ref[...] = _ln_rows(ALPHA * x2 + y, g_ref[...], b_ref[...])


def _combine(w_col, x2, ysg, sw1, sw3, sw2, g, b, tc):
    T = x2.shape[0]
    wp = ysg.shape[1]
    full = lambda a: pl.BlockSpec(a.shape, lambda i: (0,) * a.ndim)
    s1, s3, s2 = sw1.astype(BF16), sw3.astype(BF16), sw2.astype(BF16)
    g2, b2 = g.reshape(1, -1), b.reshape(1, -1)
    return pl.pallas_call(
        _combine_kernel, grid=(T // tc,),
        in_specs=[pl.BlockSpec((tc * MOE_K, 1), lambda i: (i, 0)),
                  pl.BlockSpec((tc, D_MODEL), lambda i: (i, 0)),
                  pl.BlockSpec((tc * MOE_K, wp), lambda i: (i, 0)),
                  full(s1), full(s3), full(s2), full(g2), full(b2)],
        out_specs=pl.BlockSpec((tc, D_MODEL), lambda i: (i, 0)),
        out_shape=jax.ShapeDtypeStruct((T, D_MODEL), F32),
        compiler_params=_params(("parallel",)), name="combine",
    )(w_col, x2, ysg, s1, s3, s2, g2, b2)


def _moe(x2, x2p, scores_t, router_bias, w1, w3, w2, sw1, sw3, sw2, g, b, tiles):
    T = x2.shape[0]
    rb = ROW_BLOCK
    idx_t, w_t = _route(scores_t, router_bias, tiles['route'])
    rank_t, counts = _rank(idx_t, tiles['rank'])
    pcounts = (counts + rb - 1) // rb * rb
    pends = jnp.cumsum(pcounts)
    poffs = (pends - pcounts).astype(I32)
    n_blocks = (T * MOE_K + MOE_E * (rb - 1) + rb - 1) // rb
    blk_start = jnp.arange(n_blocks, dtype=I32) * rb
    blk_exp = jnp.minimum(jnp.searchsorted(pends, blk_start, side='right'), MOE_E - 1).astype(I32)
    blk_valid = jnp.clip(counts[blk_exp] - (blk_start - poffs[blk_exp]), 0, rb).astype(I32)
    dest = _dest(idx_t, rank_t, poffs, tiles['rank']).T.reshape(-1)
    xs = _scatter(dest, x2p, n_blocks * rb, tiles['scatter'])
    ys = _experts(blk_exp, blk_valid, xs, w1, w3, w2)
    ysg = _row_gather(lax.bitcast_convert_type(ys, I32), dest)
    return _combine(w_t.T.reshape(-1, 1), x2, ysg, sw1, sw3, sw2, g, b, tiles['combine'])


def _tiles(B, S):
    T = B * S
    pick = lambda want, n: want if n % want == 0 else n
    return dict(proj=pick(512, T), nsa_q=pick(128, S), nsa_ck=pick(512, S), outproj=pick(512, T),
                mlstm_nb=2 if B % 2 == 0 else 1, xattn=pick(256, S), route=pick(512, T), rank=pick(512, T), scatter=pick(256, T),
                combine=pick(128, T))


def kernel(x, mem, w_in, nsa_pos_k, nsa_cmp_k_w1, nsa_cmp_k_w2, nsa_pos_v, nsa_cmp_v_w1, nsa_cmp_v_w2,
           mlstm_conv_w, mlstm_i_bias, mlstm_f_bias, mlstm_norm_g, w_out, ln1_g, ln1_b,
           xa_wq, xa_wk, xa_wv, xa_wo, ln2_g, ln2_b, router_w, router_bias,
           moe_w1, moe_w3, moe_w2, shared_w1, shared_w3, shared_w2, ln3_g, ln3_b):
    B, S, D = x.shape
    T = B * S
    tl = _tiles(B, S)
    xc = x.reshape(T, D)
    memc = mem.reshape(B * MEM_LEN, D)
    for l in range(w_in.shape[0]):
        (q, cmp, ksel, vsel, kwin, vwin, gates, mq, mk, mv, mo, mif) = _project(xc, _prep_w_in(w_in[l]), tl['proj'])
        kcvc = _compress(cmp, B, S, _prep_cmp(nsa_pos_k[l], nsa_cmp_k_w1[l], nsa_cmp_k_w2[l],
                                              nsa_pos_v[l], nsa_cmp_v_w1[l], nsa_cmp_v_w2[l]))
        y_nsa = _nsa(q, kcvc, ksel, vsel, kwin, vwin, gates, B, S, tl['nsa_q'], tl['nsa_ck'])
        y_ml = _mlstm(mq, mk, mv, mo, mif, mlstm_conv_w[l], mlstm_i_bias[l], mlstm_f_bias[l],
                      mlstm_norm_g[l], B, S, tl['mlstm_nb'])
        x1 = _outproj(y_nsa, y_ml, xc, w_out[l], ln1_g[l], ln1_b[l], tl['outproj'])
        kv = _memkv(memc, xa_wk[l], xa_wv[l])
        x2, x2p, scores_t = _xattn(x1, kv, xa_wq[l], xa_wo[l], ln2_g[l], ln2_b[l], router_w[l], S, tl['xattn'])
        xc = _moe(x2, x2p, scores_t, router_bias[l], moe_w1[l], moe_w3[l], moe_w2[l],
                  shared_w1[l], shared_w3[l], shared_w2[l], ln3_g[l], ln3_b[l], tl)
    return xc.reshape(B, S, D)
```

```python
import functools
import numpy as np
import jax
import jax.numpy as jnp
from jax import lax
from jax.experimental import pallas as pl
from jax.experimental.pallas import tpu as pltpu
from jax.experimental.pallas import tpu_sc as plsc

F32 = jnp.float32
BF16 = jnp.bfloat16
I32 = jnp.int32

D_MODEL = 1024
MEM_LEN = 256
NSA_HEADS = 8
NSA_GROUPS = 2
NSA_HPG = 4
NSA_DK = 64
NSA_CMP_LEN = 32
NSA_CMP_STRIDE = 16
NSA_SEL_BLOCK = 64
NSA_SEL_TOPN = 8
NSA_WINDOW = 512
ML_HEADS = 4
ML_DH = 128
ML_CHUNK = 64
ML_CONV = 4
XA_HEADS = 4
XA_DH = 256
MOE_E = 256
MOE_K = 8
MOE_GROUPS = 8
MOE_TOPK_GROUPS = 4
MOE_FF = 256
MOE_ROUTE_SCALE = 2.5
DEPTH = 1
ALPHA = (2.0 * DEPTH) ** 0.25
LN_EPS = 1e-5
NEG = -1e30
FORCE_BONUS = 1e4

LANES = 128
ROW_BLOCK = 512
VMEM_LIMIT = 56 * 1024 * 1024
SC_CORES = 2
SC_SUBCORES = 16
SC_GATHER_ROWS = 128

_DN_T = (((1,), (1,)), ((), ()))
_DN_TA = (((0,), (0,)), ((), ()))


def _params(sem):
    return pltpu.CompilerParams(dimension_semantics=sem, vmem_limit_bytes=VMEM_LIMIT)


def _ln_rows(v, g, b):
    mu = jnp.mean(v, axis=-1, keepdims=True)
    d = v - mu
    var = jnp.mean(d * d, axis=-1, keepdims=True)
    return d * lax.rsqrt(var + LN_EPS) * g + b


_SEGS = (('q', 512, BF16), ('cmp', 256, F32), ('ksel', 128, BF16), ('vsel', 128, BF16),
         ('kwin', 128, BF16), ('vwin', 128, BF16), ('gates', 128, F32), ('mq', 512, BF16),
         ('mk', 512, BF16), ('mv', 512, BF16), ('mo', 512, BF16), ('mif', 128, F32))


def _proj_kernel(x_ref, w_ref, *out_refs):
    xb = x_ref[...].astype(BF16)
    off = 0
    for o_ref, (_, wd, _) in zip(out_refs, _SEGS):
        o_ref[...] = jnp.dot(xb, w_ref[:, off:off + wd], preferred_element_type=F32).astype(o_ref.dtype)
        off += wd


def _prep_w_in(w):
    sizes = (512,) + (128,) * 6 + (24,) + (512,) * 4 + (4, 4)
    cuts = np.cumsum(sizes)[:-1].tolist()
    (wq, kc, vc, ks, vs, kw, vw, wg, mq, mk, mv, mo, mi, mf) = jnp.split(w, cuts, axis=1)
    wq = wq.reshape(D_MODEL, NSA_GROUPS, NSA_HPG, NSA_DK).transpose(0, 2, 1, 3).reshape(D_MODEL, 512)
    pad = lambda a: jnp.pad(a, ((0, 0), (0, LANES - a.shape[1])))
    segs = [wq, kc, vc, ks, vs, kw, vw, pad(wg), mq, mk, mv, mo, pad(jnp.concatenate([mi, mf], axis=1))]
    return jnp.concatenate(segs, axis=1).astype(BF16)


def _project(x2d, w_all, tm):
    T = x2d.shape[0]
    n = w_all.shape[1]
    out_shape = tuple(jax.ShapeDtypeStruct((T, wd), dt) for _, wd, dt in _SEGS)
    out_specs = tuple(pl.BlockSpec((tm, wd), lambda i: (i, 0)) for _, wd, _ in _SEGS)
    return pl.pallas_call(
        _proj_kernel, grid=(T // tm,),
        in_specs=[pl.BlockSpec((tm, D_MODEL), lambda i: (i, 0)),
                  pl.BlockSpec((D_MODEL, n), lambda i: (0, 0))],
        out_specs=out_specs, out_shape=out_shape,
        compiler_params=_params(("parallel",)), name="proj")(x2d, w_all)


def _cmp_kernel(r_ref, pa_ref, pb_ref, wa_ref, wb_ref, w2_ref, o_ref):
    r = r_ref[...]
    a = jnp.dot((r + pa_ref[...]).astype(BF16), wa_ref[...], preferred_element_type=F32)
    b = jnp.dot((r + pb_ref[...]).astype(BF16), wb_ref[...], preferred_element_type=F32)
    nr = r.shape[0]
    hid = a + pltpu.roll(b, nr - 1, 0)
    hid = hid * jax.nn.sigmoid(hid)
    out = jnp.dot(hid.astype(BF16), w2_ref[...], preferred_element_type=F32)
    row = lax.broadcasted_iota(I32, out.shape, 0)
    o_ref[...] = jnp.where(row < nr - 1, out, 0.0).astype(o_ref.dtype)


def _prep_cmp(pos_k, w1_k, w2_k, pos_v, w1_v, w2_v):
    eye = jnp.eye(NSA_GROUPS, dtype=F32)

    def expand_w1(w1, half):
        w = w1.reshape(NSA_CMP_LEN, NSA_DK, NSA_DK)[half * 16:(half + 1) * 16]
        return jnp.einsum('jde,gk->jgdke', w, eye).reshape(16, 128, 128)

    def both(fk, fv):
        z = jnp.zeros_like(fk)
        top = jnp.concatenate([fk, z], axis=-1)
        bot = jnp.concatenate([z, fv], axis=-1)
        return jnp.concatenate([top, bot], axis=-2)

    wa = both(expand_w1(w1_k, 0), expand_w1(w1_v, 0)).reshape(16 * 256, 256).astype(BF16)
    wb = both(expand_w1(w1_k, 1), expand_w1(w1_v, 1)).reshape(16 * 256, 256).astype(BF16)
    w2 = both(jnp.kron(eye, w2_k), jnp.kron(eye, w2_v)).astype(BF16)

    def pos_row(half):
        pk = jnp.tile(pos_k[half * 16:(half + 1) * 16], (1, NSA_GROUPS))
        pv = jnp.tile(pos_v[half * 16:(half + 1) * 16], (1, NSA_GROUPS))
        return jnp.concatenate([pk, pv], axis=1).reshape(1, 16 * 256)

    return pos_row(0), pos_row(1), wa, wb, w2


def _compress(cmp2d, B, S, prep):
    pa, pb, wa, wb, w2 = prep
    nr = S // NSA_CMP_STRIDE
    r = cmp2d.reshape(B, nr, NSA_CMP_STRIDE * 256)
    full = lambda a: pl.BlockSpec(a.shape, lambda b: (0,) * a.ndim)
    return pl.pallas_call(
        _cmp_kernel, grid=(B,),
        in_specs=[pl.BlockSpec((None, nr, NSA_CMP_STRIDE * 256), lambda b: (b, 0, 0)),
                  full(pa), full(pb), full(wa), full(wb), full(w2)],
        out_specs=pl.BlockSpec((None, nr, 256), lambda b: (b, 0, 0)),
        out_shape=jax.ShapeDtypeStruct((B, nr, 256), BF16),
        compiler_params=_params(("parallel",)), name="cmp")(r, pa, pb, wa, wb, w2)


def _nsa_consts(S):
    n_cmp = (S - NSA_CMP_LEN) // NSA_CMP_STRIDE + 1
    n_sel = S // NSA_SEL_BLOCK
    cs = np.arange(n_cmp) * NSA_CMP_STRIDE
    ss = np.arange(n_sel) * NSA_SEL_BLOCK
    ov = ((cs[:, None] < ss[None, :] + NSA_SEL_BLOCK) & (cs[:, None] + NSA_CMP_LEN > ss[None, :]))
    ovt = np.zeros((LANES, S // NSA_CMP_STRIDE), np.float32)
    ovt[:n_sel, :n_cmp] = ov.T
    e = np.zeros((LANES, S), np.float32)
    e[np.arange(S) // NSA_SEL_BLOCK, np.arange(S)] = 1.0
    return jnp.asarray(ovt, BF16), jnp.asarray(e, BF16)


def _nsa_kernel(q_ref, kcvc_ref, ksel_ref, vsel_ref, kwin_ref, vwin_ref, gates_ref, ovt_ref, e_ref,
                y_ref, bias_ref, *, tq, ck, n_sel):
    G, H = NSA_GROUPS, NSA_HPG
    GH = G * H
    M = GH * tq
    S = ksel_ref.shape[0]
    W = NSA_WINDOW
    ws = min(W + tq, S)
    t0 = pl.program_id(1) * tq
    gates = jax.nn.sigmoid(gates_ref[...])
    lane = lax.broadcasted_iota(I32, (tq, LANES), 1)
    t_col = t0 + lax.broadcasted_iota(I32, (tq, 1), 0)
    kc = kcvc_ref[:, 0:LANES]
    vc = kcvc_ref[:, LANES:2 * LANES]
    nc = kc.shape[0]
    qs = []
    for g in range(G):
        gmask = (lane // NSA_DK) == g
        for h in range(H):
            qh = q_ref[:, h * LANES:(h + 1) * LANES] * (NSA_DK ** -0.5)
            qs.append(jnp.where(gmask, qh, jnp.zeros_like(qh)))
    Q = jnp.concatenate(qs, axis=0).astype(BF16)

    s = lax.dot_general(Q, kc, _DN_T, preferred_element_type=F32)
    c_idx = lax.broadcasted_iota(I32, (tq, nc), 1)
    cmask = (c_idx * NSA_CMP_STRIDE + NSA_CMP_LEN - 1) <= t_col
    s3 = jnp.where(cmask[None], s.reshape(GH, tq, nc), NEG)
    p = jnp.exp(s3 - jnp.max(s3, axis=-1, keepdims=True))
    p = p / jnp.sum(p, axis=-1, keepdims=True)
    p = jnp.where(cmask[None], p, 0.0)
    o_cmp = jnp.dot(p.reshape(M, nc).astype(BF16), vc, preferred_element_type=F32).reshape(GH, tq, LANES)

    for g in range(G):
        psum = jnp.sum(p[g * H:(g + 1) * H], axis=0)
        hi = psum.astype(BF16)
        lo = (psum - hi.astype(F32)).astype(BF16)
        ovt = ovt_ref[...]
        pslt = (lax.dot_general(ovt, hi, _DN_T, preferred_element_type=F32) +
                lax.dot_general(ovt, lo, _DN_T, preferred_element_type=F32))
        imp_p = pslt[0:n_sel, :]
        n_i = lax.broadcasted_iota(I32, (n_sel, tq), 0)
        cur = (t0 + lax.broadcasted_iota(I32, (n_sel, tq), 1)) // NSA_SEL_BLOCK
        forced = (n_i == 0) | (n_i == cur) | (n_i == cur - 1)
        imp = jnp.where(n_i <= cur, imp_p + jnp.where(forced, FORCE_BONUS, 0.0), NEG)
        cnt = jnp.zeros((n_sel, tq), F32)
        for m in range(n_sel):
            row = imp[m:m + 1, :]
            beats = (row > imp) | ((row == imp) & (n_i > m))
            cnt = cnt + jnp.where(beats, 1.0, 0.0)
        selt = jnp.where(cnt < float(min(NSA_SEL_TOPN, n_sel)), 1.0, 0.0)
        selt = jnp.concatenate([selt, jnp.zeros((LANES - n_sel, tq), F32)], axis=0)
        sel = selt.T.astype(BF16)
        maskf = jnp.dot(sel, e_ref[...], preferred_element_type=F32)
        kpos = lax.broadcasted_iota(I32, (tq, S), 1)
        bias_ref[g] = jnp.where((maskf > 0.5) & (kpos <= t_col), 0.0, NEG)

    def sel_body(j, carry):
        m_i, l_i, acc = carry
        ks = pl.multiple_of(j * ck, ck)
        k = ksel_ref[pl.ds(ks, ck), :]
        v = vsel_ref[pl.ds(ks, ck), :]
        sj = lax.dot_general(Q, k, _DN_T, preferred_element_type=F32)
        sj = (sj.reshape(G, H, tq, ck) + bias_ref[:, :, pl.ds(ks, ck)][:, None]).reshape(GH, tq, ck)
        m_new = jnp.maximum(m_i, jnp.max(sj, axis=-1, keepdims=True))
        a = jnp.exp(m_i - m_new)
        pj = jnp.exp(sj - m_new)
        l_new = a * l_i + jnp.sum(pj, axis=-1, keepdims=True)
        pv = jnp.dot(pj.reshape(M, ck).astype(BF16), v, preferred_element_type=F32)
        return m_new, l_new, a * acc + pv.reshape(GH, tq, LANES)

    init = (jnp.full((GH, tq, 1), NEG, F32), jnp.zeros((GH, tq, 1), F32), jnp.zeros((GH, tq, LANES), F32))
    _, l_f, acc = lax.fori_loop(0, (t0 + tq + ck - 1) // ck, sel_body, init)
    o_sel = acc / l_f

    kst = pl.multiple_of(jnp.clip(t0 - W, 0, S - ws), LANES)
    kwn = kwin_ref[pl.ds(kst, ws), :]
    vwn = vwin_ref[pl.ds(kst, ws), :]
    sw = lax.dot_general(Q, kwn, _DN_T, preferred_element_type=F32)
    wpos = kst + lax.broadcasted_iota(I32, (tq, ws), 1)
    wmask = (wpos <= t_col) & (wpos > t_col - W)
    sw3 = jnp.where(wmask[None], sw.reshape(GH, tq, ws), NEG)
    pw = jnp.exp(sw3 - jnp.max(sw3, axis=-1, keepdims=True))
    lw = jnp.sum(pw, axis=-1, keepdims=True)
    o_win = jnp.dot(pw.reshape(M, ws).astype(BF16), vwn, preferred_element_type=F32).reshape(GH, tq, LANES) / lw

    g0mask = lane < NSA_DK
    for h in range(H):
        o_g = []
        for g in range(G):
            r = g * H + h
            c0 = r * 3
            o_g.append(gates[:, c0:c0 + 1] * o_cmp[r] + gates[:, c0 + 1:c0 + 2] * o_sel[r] +
                       gates[:, c0 + 2:c0 + 3] * o_win[r])
        y_ref[:, h * LANES:(h + 1) * LANES] = jnp.where(g0mask, o_g[0], o_g[1]).astype(y_ref.dtype)


def _nsa(q, kcvc, ksel, vsel, kwin, vwin, gates, B, S, tq, ck):
    T = B * S
    nq = S // tq
    ovt, e = _nsa_consts(S)
    seq = lambda a: a.reshape(B, S, LANES)
    kv_spec = pl.BlockSpec((None, S, LANES), lambda b, i: (b, 0, 0))
    kern = functools.partial(_nsa_kernel, tq=tq, ck=ck, n_sel=S // NSA_SEL_BLOCK)
    return pl.pallas_call(
        kern, grid=(B, nq),
        in_specs=[pl.BlockSpec((tq, 512), lambda b, i: (b * nq + i, 0)),
                  pl.BlockSpec((None,) + kcvc.shape[1:], lambda b, i: (b, 0, 0)),
                  kv_spec, kv_spec, kv_spec, kv_spec,
                  pl.BlockSpec((tq, LANES), lambda b, i: (b * nq + i, 0)),
                  pl.BlockSpec(ovt.shape, lambda b, i: (0, 0)),
                  pl.BlockSpec(e.shape, lambda b, i: (0, 0))],
        out_specs=pl.BlockSpec((tq, 512), lambda b, i: (b * nq + i, 0)),
        out_shape=jax.ShapeDtypeStruct((T, 512), BF16),
        scratch_shapes=[pltpu.VMEM((NSA_GROUPS, tq, S), F32)],
        compiler_params=_params(("parallel", "parallel")), name="nsa",
    )(q, kcvc, seq(ksel), seq(vsel), seq(kwin), seq(vwin), gates, ovt, e)


def _mlstm_kernel(q_ref, k_ref, v_ref, o_ref, gn_ref, gt_ref, cw_ref, bn_ref, bt_ref, ng_ref, tri_ref,
                  y_ref, c_scr, n_scr):
    H, dh, L = ML_HEADS, ML_DH, ML_CHUNK
    nb, S = q_ref.shape[0], q_ref.shape[1]
    nchunk = S // L
    c_scr[...] = jnp.zeros_like(c_scr)
    n_scr[...] = jnp.zeros_like(n_scr)
    row = lax.broadcasted_iota(I32, (L, H * dh), 0)
    li = lax.broadcasted_iota(I32, (L, L), 0)
    mi = lax.broadcasted_iota(I32, (L, L), 1)
    causal = mi <= li
    tril = tri_ref[0]
    triu = tri_ref[1]
    hp = lax.Precision.HIGHEST

    def conv_silu(ref, bi, c, wofs):
        r0 = pl.multiple_of(c * L, L)
        rp = pl.multiple_of(jnp.maximum(c - 1, 0) * L, L)
        cur = ref[bi, pl.ds(r0, L), :].astype(F32)
        prev = ref[bi, pl.ds(rp, L), :].astype(F32) * jnp.where(c > 0, 1.0, 0.0)
        acc = cur * cw_ref[ML_CONV - 1:ML_CONV, wofs:wofs + H * dh]
        for j in range(1, ML_CONV):
            sh = jnp.where(row < j, pltpu.roll(prev, j, 0), pltpu.roll(cur, j, 0))
            acc = acc + sh * cw_ref[ML_CONV - 1 - j:ML_CONV - j, wofs:wofs + H * dh]
        return acc * jax.nn.sigmoid(acc)

    def body(c, m_state):
        r0 = pl.multiple_of(c * L, L)
        new_m = []
        for bi in range(nb):
            qa = conv_silu(q_ref, bi, c, 0) * (dh ** -0.5)
            ka = conv_silu(k_ref, bi, c, H * dh)
            va = v_ref[bi, pl.ds(r0, L), :]
            oa = o_ref[bi, pl.ds(r0, L), :].astype(F32)
            gn = gn_ref[bi, pl.ds(r0, L), :] + bn_ref[...]
            gt = gt_ref[bi, :, c, :] + bt_ref[...]
            lf_n = jax.nn.log_sigmoid(gn)
            lf_t = jax.nn.log_sigmoid(gt)
            b_n = jnp.dot(tril, lf_n, precision=hp, preferred_element_type=F32)
            b_t = jnp.dot(lf_t, triu, precision=hp, preferred_element_type=F32)
            for h in range(H):
                st = bi * H + h
                q = qa[:, h * dh:(h + 1) * dh]
                k = ka[:, h * dh:(h + 1) * dh]
                v = va[:, h * dh:(h + 1) * dh]
                m_old = m_state[st]
                b_col = b_n[:, H + h:H + h + 1]
                i_col = gn[:, h:h + 1]
                b_row = b_t[H + h:H + h + 1, :]
                i_row = gt[h:h + 1, :]
                g_tot = b_t[H + h:H + h + 1, L - 1:L]
                d_log = jnp.where(causal, b_col - b_row + i_row, NEG)
                inter = b_col + m_old
                m_q = jnp.maximum(inter, jnp.max(d_log, axis=-1, keepdims=True))
                w_intra = jnp.exp(d_log - m_q)
                w_inter = jnp.exp(inter - m_q)
                qb = q.astype(BF16)
                s = lax.dot_general(qb, k.astype(BF16), _DN_T, preferred_element_type=F32) * w_intra
                cst = c_scr[st]
                nst = n_scr[st]
                num = (w_inter * jnp.dot(qb, cst.astype(BF16), preferred_element_type=F32) +
                       jnp.dot(s.astype(BF16), v, preferred_element_type=F32))
                den = w_inter * jnp.sum(q * nst, axis=-1, keepdims=True) + jnp.sum(s, axis=-1, keepdims=True)
                hv = num / jnp.maximum(jnp.abs(den), jnp.exp(-m_q))
                log_k = g_tot - b_col + i_col
                m_new = jnp.maximum(g_tot + m_old, jnp.max(log_k, axis=0, keepdims=True))
                wk = jnp.exp(log_k - m_new)
                decay = jnp.exp(g_tot + m_old - m_new)
                kw = k * wk
                c_scr[st] = decay * cst + lax.dot_general(kw.astype(BF16), v, _DN_TA, preferred_element_type=F32)
                n_scr[st] = decay * nst + jnp.sum(kw, axis=0, keepdims=True)
                new_m.append(m_new)
                mu = jnp.mean(hv, axis=-1, keepdims=True)
                dv = hv - mu
                var = jnp.mean(dv * dv, axis=-1, keepdims=True)
                hn = dv * lax.rsqrt(var + LN_EPS) * ng_ref[:, h * dh:(h + 1) * dh]
                og = jax.nn.sigmoid(oa[:, h * dh:(h + 1) * dh])
                y_ref[bi, pl.ds(r0, L), h * dh:(h + 1) * dh] = (og * hn).astype(y_ref.dtype)
        return tuple(new_m)

    lax.fori_loop(0, nchunk, body, tuple(jnp.zeros((1, 1), F32) for _ in range(nb * H)))


def _mlstm(mq, mk, mv, mo, mif, conv_w, i_bias, f_bias, norm_g, B, S, nb):
    T = B * S
    H, dh, L = ML_HEADS, ML_DH, ML_CHUNK
    W = H * dh
    gt = mif[:, :2 * H].reshape(B, S, 2 * H).transpose(0, 2, 1).reshape(B, 2 * H, S // L, L)
    cw = conv_w.reshape(ML_CONV, 2 * W)
    bias = jnp.concatenate([i_bias, f_bias])
    bn = jnp.pad(bias, (0, LANES - 2 * H)).reshape(1, LANES)
    bt = bias.reshape(2 * H, 1)
    ng = norm_g.reshape(1, W)
    tri = jnp.stack([jnp.tril(jnp.ones((L, L), F32)), jnp.triu(jnp.ones((L, L), F32))])
    seq = lambda a: a.reshape(B, S, a.shape[1])
    rows = lambda w: pl.BlockSpec((nb, S, w), lambda b: (b, 0, 0))
    full = lambda a: pl.BlockSpec(a.shape, lambda b: (0,) * a.ndim)
    y = pl.pallas_call(
        _mlstm_kernel, grid=(B // nb,),
        in_specs=[rows(W), rows(W), rows(W), rows(W), rows(LANES),
                  pl.BlockSpec((nb, 2 * H, S // L, L), lambda b: (b, 0, 0, 0)),
                  full(cw), full(bn), full(bt), full(ng), full(tri)],
        out_specs=rows(W),
        out_shape=jax.ShapeDtypeStruct((B, S, W), BF16),
        scratch_shapes=[pltpu.VMEM((nb * H, dh, dh), F32), pltpu.VMEM((nb * H, 1, dh), F32)],
        compiler_params=_params(("parallel",)), name="mlstm",
    )(seq(mq), seq(mk), seq(mv), seq(mo), seq(mif), gt, cw, bn, bt, ng, tri)
    return y.reshape(T, W)


def _outproj_kernel(yn_ref, ym_ref, x_ref, w_ref, g_ref, b_ref, o_ref):
    mix = (jnp.dot(yn_ref[...], w_ref[0:512, :], preferred_element_type=F32) +
           jnp.dot(ym_ref[...], w_ref[512:1024, :], preferred_element_type=F32))
    o_ref[...] = _ln_rows(ALPHA * x_ref[...] + mix, g_ref[...], b_ref[...])


def _outproj(y_nsa, y_ml, x2d, w_out, g, b, tm):
    T = x2d.shape[0]
    wn = w_out[:512].reshape(NSA_GROUPS, NSA_HPG, NSA_DK, D_MODEL).transpose(1, 0, 2, 3).reshape(512, D_MODEL)
    w = jnp.concatenate([wn, w_out[512:]], axis=0).astype(BF16)
    row = lambda wd: pl.BlockSpec((tm, wd), lambda i: (i, 0))
    full = lambda a: pl.BlockSpec(a.shape, lambda i: (0,) * a.ndim)
    g2, b2 = g.reshape(1, -1), b.reshape(1, -1)
    return pl.pallas_call(
        _outproj_kernel, grid=(T // tm,),
        in_specs=[row(512), row(512), row(D_MODEL), full(w), full(g2), full(b2)],
        out_specs=row(D_MODEL), out_shape=jax.ShapeDtypeStruct((T, D_MODEL), F32),
        compiler_params=_params(("parallel",)), name="outproj")(y_nsa, y_ml, x2d, w, g2, b2)


def _memkv_kernel(m_ref, w_ref, o_ref):
    o_ref[...] = jnp.dot(m_ref[...].astype(BF16), w_ref[...], preferred_element_type=F32).astype(o_ref.dtype)


def _memkv(mem2d, wk, wv):
    w = jnp.concatenate([wk, wv], axis=1).astype(BF16)
    R = mem2d.shape[0]
    return pl.pallas_call(
        _memkv_kernel, grid=(R // MEM_LEN,),
        in_specs=[pl.BlockSpec((MEM_LEN, D_MODEL), lambda i: (i, 0)),
                  pl.BlockSpec(w.shape, lambda i: (0, 0))],
        out_specs=pl.BlockSpec((MEM_LEN, 2 * D_MODEL), lambda i: (i, 0)),
        out_shape=jax.ShapeDtypeStruct((R, 2 * D_MODEL), BF16),
        compiler_params=_params(("parallel",)), name="memkv")(mem2d, w)


def _xattn_kernel(x_ref, kv_ref, wq_ref, wo_ref, g_ref, b_ref, rw_ref, x2_ref, x2p_ref, sc_ref):
    x1 = x_ref[...]
    q = jnp.dot(x1.astype(BF16), wq_ref[...], preferred_element_type=F32).astype(BF16)
    outs = []
    for h in range(XA_HEADS):
        qh = q[:, h * XA_DH:(h + 1) * XA_DH]
        kh = kv_ref[:, h * XA_DH:(h + 1) * XA_DH]
        vh = kv_ref[:, D_MODEL + h * XA_DH:D_MODEL + (h + 1) * XA_DH]
        s = lax.dot_general(qh, kh, _DN_T, preferred_element_type=F32) * (XA_DH ** -0.5)
        p = jnp.exp(s - jnp.max(s, axis=-1, keepdims=True))
        p = p / jnp.sum(p, axis=-1, keepdims=True)
        outs.append(jnp.dot(p.astype(BF16), vh, preferred_element_type=F32).astype(BF16))
    o = jnp.concatenate(outs, axis=1)
    xa = jnp.dot(o, wo_ref[...], preferred_element_type=F32)
    x2 = _ln_rows(ALPHA * x1 + xa, g_ref[...], b_ref[...])
    x2_ref[...] = x2
    x2p_ref[...] = _pack_bf16_pairs(x2)
    xh = x2.astype(BF16)
    xl = (x2 - xh.astype(F32)).astype(BF16)
    wh = rw_ref[0]
    wl = rw_ref[1]
    logit = (lax.dot_general(wh, xh, _DN_T, preferred_element_type=F32) +
             lax.dot_general(wh, xl, _DN_T, preferred_element_type=F32) +
             lax.dot_general(wl, xh, _DN_T, preferred_element_type=F32))
    sc_ref[...] = jax.nn.sigmoid(logit)


def _xattn(x1, kv, wq, wo, g, b, router_w, S, tq):
    T = x1.shape[0]
    wqb, wob = wq.astype(BF16), wo.astype(BF16)
    rwt = router_w.T
    rh = rwt.astype(BF16)
    rw = jnp.stack([rh, (rwt - rh.astype(F32)).astype(BF16)])
    g2, b2 = g.reshape(1, -1), b.reshape(1, -1)
    full = lambda a: pl.BlockSpec(a.shape, lambda i: (0,) * a.ndim)
    per = S // tq
    return pl.pallas_call(
        _xattn_kernel, grid=(T // tq,),
        in_specs=[pl.BlockSpec((tq, D_MODEL), lambda i: (i, 0)),
                  pl.BlockSpec((MEM_LEN, 2 * D_MODEL), lambda i: (i // per, 0)),
                  full(wqb), full(wob), full(g2), full(b2), full(rw)],
        out_specs=(pl.BlockSpec((tq, D_MODEL), lambda i: (i, 0)),
                   pl.BlockSpec((tq, D_MODEL // 2), lambda i: (i, 0)),
                   pl.BlockSpec((MOE_E, tq), lambda i: (0, i))),
        out_shape=(jax.ShapeDtypeStruct((T, D_MODEL), F32), jax.ShapeDtypeStruct((T, D_MODEL // 2), I32),
                   jax.ShapeDtypeStruct((MOE_E, T), F32)),
        compiler_params=_params(("parallel",)), name="xattn")(x1, kv, wqb, wob, g2, b2, rw)


def _route_kernel(sc_ref, rb_ref, idx_ref, w_ref):
    E, G = MOE_E, MOE_GROUPS
    per = E // G
    scores = sc_ref[...]
    tr = scores.shape[1]
    biased = scores + rb_ref[...]
    g3 = biased.reshape(G, per, tr)
    j3 = lax.broadcasted_iota(I32, (G, per, tr), 1)
    m1 = jnp.max(g3, axis=1, keepdims=True)
    first = jnp.min(jnp.where(g3 == m1, j3, per), axis=1, keepdims=True)
    m2 = jnp.max(jnp.where(j3 == first, -jnp.inf, g3), axis=1, keepdims=True)
    gs = (m1 + m2).reshape(G, tr)
    gi = lax.broadcasted_iota(I32, (G, tr), 0)
    cnt = jnp.zeros((G, tr), F32)
    for m in range(G):
        row = gs[m:m + 1, :]
        cnt = cnt + jnp.where((row > gs) | ((row == gs) & (gi > m)), 1.0, 0.0)
    gmask = cnt < float(MOE_TOPK_GROUPS)
    masked = jnp.where(gmask[:, None, :], g3, NEG).reshape(E, tr)
    ei = lax.broadcasted_iota(I32, (E, tr), 0)
    idxs, ws = [], []
    for _ in range(MOE_K):
        mx = jnp.max(masked, axis=0, keepdims=True)
        ix = jnp.min(jnp.where(masked == mx, ei, E), axis=0, keepdims=True)
        hit = ei == ix
        ws.append(jnp.sum(jnp.where(hit, scores, 0.0), axis=0, keepdims=True))
        idxs.append(ix)
        masked = jnp.where(hit, -jnp.inf, masked)
    w = jnp.concatenate(ws, axis=0)
    idx_ref[...] = jnp.concatenate(idxs, axis=0)
    w_ref[...] = w / jnp.sum(w, axis=0, keepdims=True) * MOE_ROUTE_SCALE


def _route(scores_t, router_bias, tr):
    E, T = scores_t.shape
    rb = router_bias.reshape(E, 1)
    return pl.pallas_call(
        _route_kernel, grid=(T // tr,),
        in_specs=[pl.BlockSpec((E, tr), lambda i: (0, i)), pl.BlockSpec((E, 1), lambda i: (0, 0))],
        out_specs=(pl.BlockSpec((MOE_K, tr), lambda i: (0, i)), pl.BlockSpec((MOE_K, tr), lambda i: (0, i))),
        out_shape=(jax.ShapeDtypeStruct((MOE_K, T), I32), jax.ShapeDtypeStruct((MOE_K, T), F32)),
        compiler_params=_params(("parallel",)), name="route")(scores_t, rb)


def _rank_kernel(idx_ref, u_ref, rank_ref, cnt_ref, carry):
    E = MOE_E

    @pl.when(pl.program_id(0) == 0)
    def _():
        carry[...] = jnp.zeros_like(carry)

    idx = idx_ref[...]
    tp = idx.shape[1]
    ei = lax.broadcasted_iota(I32, (E, tp), 0)
    hits = [ei == idx[k:k + 1, :] for k in range(MOE_K)]
    onehot = jnp.zeros((E, tp), F32)
    for hit in hits:
        onehot = onehot + jnp.where(hit, 1.0, 0.0)
    pos = jnp.dot(onehot.astype(BF16), u_ref[...], preferred_element_type=F32) + carry[...]
    ranks = [jnp.sum(jnp.where(hit, pos, 0.0), axis=0, keepdims=True) for hit in hits]
    rank_ref[...] = jnp.concatenate(ranks, axis=0).astype(I32)
    total = carry[...] + jnp.sum(onehot, axis=1, keepdims=True)
    carry[...] = total
    cnt_ref[...] = jnp.broadcast_to(total, cnt_ref.shape).astype(I32)


def _rank(idx_t, tp):
    K, T = idx_t.shape
    u = jnp.triu(jnp.ones((tp, tp), F32), k=1).astype(BF16)
    rank, cnt = pl.pallas_call(
        _rank_kernel, grid=(T // tp,),
        in_specs=[pl.BlockSpec((K, tp), lambda i: (0, i)), pl.BlockSpec((tp, tp), lambda i: (0, 0))],
        out_specs=(pl.BlockSpec((K, tp), lambda i: (0, i)), pl.BlockSpec((MOE_E, LANES), lambda i: (0, 0))),
        out_shape=(jax.ShapeDtypeStruct((K, T), I32), jax.ShapeDtypeStruct((MOE_E, LANES), I32)),
        scratch_shapes=[pltpu.VMEM((MOE_E, 1), F32)],
        compiler_params=_params(("arbitrary",)), name="rank")(idx_t, u)
    return rank, cnt[:, 0]


def _dest_kernel(idx_ref, rank_ref, po_ref, dest_ref):
    idx = idx_ref[...]
    tp = idx.shape[1]
    ei = lax.broadcasted_iota(I32, (MOE_E, tp), 0)
    po = po_ref[...]
    base = [jnp.sum(jnp.where(ei == idx[k:k + 1, :], po, 0.0), axis=0, keepdims=True) for k in range(MOE_K)]
    dest_ref[...] = jnp.concatenate(base, axis=0).astype(I32) + rank_ref[...]


def _dest(idx_t, rank_t, poffs, tp):
    K, T = idx_t.shape
    po = poffs.astype(F32).reshape(MOE_E, 1)
    spec = pl.BlockSpec((K, tp), lambda i: (0, i))
    return pl.pallas_call(
        _dest_kernel, grid=(T // tp,),
        in_specs=[spec, spec, pl.BlockSpec((MOE_E, 1), lambda i: (0, 0))],
        out_specs=spec, out_shape=jax.ShapeDtypeStruct((K, T), I32),
        compiler_params=_params(("parallel",)), name="dest")(idx_t, rank_t, po)


def _pack_bf16_pairs(v):
    m = v.shape[1] // 2
    bits = lax.bitcast_convert_type(v.astype(BF16).astype(F32), jnp.uint32)
    return lax.bitcast_convert_type((bits[:, :m] >> 16) | (bits[:, m:] & jnp.uint32(0xFFFF0000)), I32)


def _unpack_bf16_pairs(w):
    w = lax.bitcast_convert_type(w, jnp.uint32)
    lo = lax.bitcast_convert_type(w << 16, F32)
    hi = lax.bitcast_convert_type(w & jnp.uint32(0xFFFF0000), F32)
    return lo, hi


def _row_scatter(rows, dest_t, n_rows):
    T, d = rows.shape
    K = dest_t.shape[0]
    nw = SC_CORES * SC_SUBCORES
    per_w = T // nw
    ch = SC_GATHER_ROWS
    mesh = plsc.VectorSubcoreMesh(core_axis_name="c", subcore_axis_name="s")

    @functools.partial(
        pl.kernel, mesh=mesh, out_type=jax.ShapeDtypeStruct((n_rows, d), I32),
        scratch_types=[pltpu.VMEM((ch,), I32), pltpu.VMEM((ch, d), I32), pltpu.SemaphoreType.DMA],
        name="row_scatter")
    def scatter(rows_hbm, dest_hbm, out_hbm, idx_v, rows_v, sem):
        wid = lax.axis_index("s") * SC_CORES + lax.axis_index("c")
        base = wid * per_w

        @pl.loop(0, per_w // ch)
        def _(i):
            off = pl.multiple_of(base + i * ch, ch)
            pltpu.sync_copy(rows_hbm.at[pl.ds(off, ch)], rows_v)
            for k in range(K):
                pltpu.sync_copy(dest_hbm.at[k, pl.ds(off, ch)], idx_v)
                pltpu.async_copy(rows_v, out_hbm.at[idx_v], sem).wait()

    return scatter(rows, dest_t)


def _expert_kernel(be_ref, nv_ref, xs_ref, w1_ref, w3_ref, w2_ref, ys_ref):
    i = pl.program_id(0)
    nv = nv_ref[i]
    hw = D_MODEL // 2

    @pl.when(nv > 0)
    def _():
        row = lax.broadcasted_iota(I32, xs_ref.shape, 0)
        lo, hi = _unpack_bf16_pairs(jnp.where(row < nv, xs_ref[...], 0))
        lo, hi = lo.astype(BF16), hi.astype(BF16)
        w1 = w1_ref[...].astype(BF16)
        w3 = w3_ref[...].astype(BF16)
        a = (jnp.dot(lo, w1[:hw], preferred_element_type=F32) + jnp.dot(hi, w1[hw:], preferred_element_type=F32))
        c = (jnp.dot(lo, w3[:hw], preferred_element_type=F32) + jnp.dot(hi, w3[hw:], preferred_element_type=F32))
        h = (a * jax.nn.sigmoid(a) * c).astype(BF16)
        ys_ref[...] = _pack_bf16_pairs(jnp.dot(h, w2_ref[...].astype(BF16), preferred_element_type=F32))

    @pl.when(nv <= 0)
    def _():
        ys_ref[...] = jnp.zeros_like(ys_ref)


def _experts(blk_exp, blk_valid, xs, w1, w3, w2):
    n_rows, wp = xs.shape
    rb = ROW_BLOCK
    gs = pltpu.PrefetchScalarGridSpec(
        num_scalar_prefetch=2, grid=(n_rows // rb,),
        in_specs=[pl.BlockSpec((rb, wp), lambda i, be, nv: (i, 0)),
                  pl.BlockSpec((None, D_MODEL, MOE_FF), lambda i, be, nv: (be[i], 0, 0)),
                  pl.BlockSpec((None, D_MODEL, MOE_FF), lambda i, be, nv: (be[i], 0, 0)),
                  pl.BlockSpec((None, MOE_FF, D_MODEL), lambda i, be, nv: (be[i], 0, 0))],
        out_specs=pl.BlockSpec((rb, wp), lambda i, be, nv: (i, 0)))
    return pl.pallas_call(
        _expert_kernel, grid_spec=gs, out_shape=jax.ShapeDtypeStruct((n_rows, wp), I32),
        compiler_params=_params(("arbitrary",)), name="experts")(blk_exp, blk_valid, xs, w1, w3, w2)


def _row_gather(table, idx):
    n, d = idx.shape[0], table.shape[1]
    nw = SC_CORES * SC_SUBCORES
    per_w = n // nw
    ch = SC_GATHER_ROWS
    mesh = plsc.VectorSubcoreMesh(core_axis_name="c", subcore_axis_name="s")

    @functools.partial(
        pl.kernel, mesh=mesh, out_type=jax.ShapeDtypeStruct((n, d), I32),
        scratch_types=[pltpu.VMEM((ch,), I32), pltpu.VMEM((ch, d), I32), pltpu.SemaphoreType.DMA],
        name="row_gather")
    def gather(table_hbm, idx_hbm, out_hbm, idx_v, rows_v, sem):
        wid = lax.axis_index("s") * SC_CORES + lax.axis_index("c")
        base = wid * per_w

        @pl.loop(0, per_w // ch)
        def _(i):
            off = pl.multiple_of(base + i * ch, ch)
            pltpu.sync_copy(idx_hbm.at[pl.ds(off, ch)], idx_v)
            pltpu.async_copy(table_hbm.at[idx_v], rows_v, sem).wait()
            pltpu.sync_copy(rows_v, out_hbm.at[pl.ds(off, ch)])

    return gather(table, idx)


def _combine_kernel(w_ref, tile_ref, x_ref, ysg_ref, s1_ref, s3_ref, s2_ref, g_ref, b_ref, o_ref):
    x2 = x_ref[...]
    xb = x2.astype(BF16)
    a = jnp.dot(xb, s1_ref[...], preferred_element_type=F32)
    c = jnp.dot(xb, s3_ref[...], preferred_element_type=F32)
    shared = jnp.dot((a * jax.nn.sigmoid(a) * c).astype(BF16), s2_ref[...], preferred_element_type=F32)
    lo, hi = _unpack_bf16_pairs(ysg_ref[...])
    lo, hi = lo.astype(BF16), hi.astype(BF16)
    wrep = jnp.dot(w_ref[...], tile_ref[...], precision=lax.Precision.HIGHEST, preferred_element_type=F32)
    col = lax.broadcasted_iota(I32, wrep.shape, 1)
    row = lax.broadcasted_iota(I32, wrep.shape, 0)
    wsel = jnp.where(col // MOE_K == row, wrep, 0.0)
    wh = wsel.astype(BF16)
    wl = (wsel - wh.astype(F32)).astype(BF16)
    y_lo = jnp.dot(wh, lo, preferred_element_type=F32) + jnp.dot(wl, lo, preferred_element_type=F32)
    y_hi = jnp.dot(wh, hi, preferred_element_type=F32) + jnp.dot(wl, hi, preferred_element_type=F32)
    y = shared + jnp.concatenate([y_lo, y_hi], axis=1)
    o_ref[...] = _ln_rows(ALPHA * x2 + y, g_ref[...], b_ref[...])


def _combine(w_nat, x2, ysg, sw1, sw3, sw2, g, b, tc):
    T = x2.shape[0]
    wp = ysg.shape[1]
    full = lambda a: pl.BlockSpec(a.shape, lambda i: (0,) * a.ndim)
    s1, s3, s2 = sw1.astype(BF16), sw3.astype(BF16), sw2.astype(BF16)
    g2, b2 = g.reshape(1, -1), b.reshape(1, -1)
    tile = jnp.asarray(np.arange(tc * MOE_K)[None, :] % MOE_K == np.arange(MOE_K)[:, None], F32)
    return pl.pallas_call(
        _combine_kernel, grid=(T // tc,),
        in_specs=[pl.BlockSpec((tc, MOE_K), lambda i: (i, 0)), full(tile),
                  pl.BlockSpec((tc, D_MODEL), lambda i: (i, 0)),
                  pl.BlockSpec((tc * MOE_K, wp), lambda i: (i, 0)),
                  full(s1), full(s3), full(s2), full(g2), full(b2)],
        out_specs=pl.BlockSpec((tc, D_MODEL), lambda i: (i, 0)),
        out_shape=jax.ShapeDtypeStruct((T, D_MODEL), F32),
        compiler_params=_params(("parallel",)), name="combine",
    )(w_nat, tile, x2, ysg, s1, s3, s2, g2, b2)


def _moe(x2, x2p, scores_t, router_bias, w1, w3, w2, sw1, sw3, sw2, g, b, tiles):
    T = x2.shape[0]
    rb = ROW_BLOCK
    idx_t, w_t = _route(scores_t, router_bias, tiles['route'])
    rank_t, counts = _rank(idx_t, tiles['rank'])
    pcounts = (counts + rb - 1) // rb * rb
    pends = jnp.cumsum(pcounts)
    poffs = (pends - pcounts).astype(I32)
    n_blocks = (T * MOE_K + MOE_E * (rb - 1) + rb - 1) // rb
    blk_start = jnp.arange(n_blocks, dtype=I32) * rb
    blk_exp = jnp.minimum(jnp.searchsorted(pends, blk_start, side='right'), MOE_E - 1).astype(I32)
    blk_valid = jnp.clip(counts[blk_exp] - (blk_start - poffs[blk_exp]), 0, rb).astype(I32)
    dest_t = _dest(idx_t, rank_t, poffs, tiles['rank'])
    xs = _row_scatter(x2p, dest_t, n_blocks * rb)
    ys = _experts(blk_exp, blk_valid, xs, w1, w3, w2)
    ysg = _row_gather(ys, dest_t.T.reshape(-1))
    return _combine(w_t.T, x2, ysg, sw1, sw3, sw2, g, b, tiles['combine'])


def _tiles(B, S):
    T = B * S
    pick = lambda want, n: want if n % want == 0 else n
    return dict(proj=pick(512, T), nsa_q=pick(128, S), nsa_ck=pick(512, S), outproj=pick(512, T),
                mlstm_nb=2 if B % 2 == 0 else 1, xattn=pick(256, S), route=pick(512, T), rank=pick(512, T), scatter=pick(256, T),
                combine=pick(128, T))


def kernel(x, mem, w_in, nsa_pos_k, nsa_cmp_k_w1, nsa_cmp_k_w2, nsa_pos_v, nsa_cmp_v_w1, nsa_cmp_v_w2,
           mlstm_conv_w, mlstm_i_bias, mlstm_f_bias, mlstm_norm_g, w_out, ln1_g, ln1_b,
           xa_wq, xa_wk, xa_wv, xa_wo, ln2_g, ln2_b, router_w, router_bias,
           moe_w1, moe_w3, moe_w2, shared_w1, shared_w3, shared_w2, ln3_g, ln3_b):
    B, S, D = x.shape
    T = B * S
    tl = _tiles(B, S)
    xc = x.reshape(T, D)
    memc = mem.reshape(B * MEM_LEN, D)
    for l in range(w_in.shape[0]):
        (q, cmp, ksel, vsel, kwin, vwin, gates, mq, mk, mv, mo, mif) = _project(xc, _prep_w_in(w_in[l]), tl['proj'])
        kcvc = _compress(cmp, B, S, _prep_cmp(nsa_pos_k[l], nsa_cmp_k_w1[l], nsa_cmp_k_w2[l],
                                              nsa_pos_v[l], nsa_cmp_v_w1[l], nsa_cmp_v_w2[l]))
        y_nsa = _nsa(q, kcvc, ksel, vsel, kwin, vwin, gates, B, S, tl['nsa_q'], tl['nsa_ck'])
        y_ml = _mlstm(mq, mk, mv, mo, mif, mlstm_conv_w[l], mlstm_i_bias[l], mlstm_f_bias[l],
                      mlstm_norm_g[l], B, S, tl['mlstm_nb'])
        x1 = _outproj(y_nsa, y_ml, xc, w_out[l], ln1_g[l], ln1_b[l], tl['outproj'])
        kv = _memkv(memc, xa_wk[l], xa_wv[l])
        x2, x2p, scores_t = _xattn(x1, kv, xa_wq[l], xa_wo[l], ln2_g[l], ln2_b[l], router_w[l], S, tl['xattn'])
        xc = _moe(x2, x2p, scores_t, router_bias[l], moe_w1[l], moe_w3[l], moe_w2[l],
                  shared_w1[l], shared_w3[l], shared_w2[l], ln3_g[l], ln3_b[l], tl)
    return xc.reshape(B, S, D)
```

```python
import functools
import numpy as np
import jax
import jax.numpy as jnp
from jax import lax
from jax.experimental import pallas as pl
from jax.experimental.pallas import tpu as pltpu
from jax.experimental.pallas import tpu_sc as plsc

F32 = jnp.float32
BF16 = jnp.bfloat16
I32 = jnp.int32

D_MODEL = 1024
MEM_LEN = 256
NSA_HEADS = 8
NSA_GROUPS = 2
NSA_HPG = 4
NSA_DK = 64
NSA_CMP_LEN = 32
NSA_CMP_STRIDE = 16
NSA_SEL_BLOCK = 64
NSA_SEL_TOPN = 8
NSA_WINDOW = 512
ML_HEADS = 4
ML_DH = 128
ML_CHUNK = 64
ML_CONV = 4
XA_HEADS = 4
XA_DH = 256
MOE_E = 256
MOE_K = 8
MOE_GROUPS = 8
MOE_TOPK_GROUPS = 4
MOE_FF = 256
MOE_ROUTE_SCALE = 2.5
DEPTH = 1
ALPHA = (2.0 * DEPTH) ** 0.25
LN_EPS = 1e-5
NEG = -1e30
FORCE_BONUS = 1e4

LANES = 128
ROW_BLOCK = 512
VMEM_LIMIT = 56 * 1024 * 1024
SC_CORES = 2
SC_SUBCORES = 16
SC_GATHER_ROWS = 128

_DN_T = (((1,), (1,)), ((), ()))
_DN_TA = (((0,), (0,)), ((), ()))


def _params(sem):
    return pltpu.CompilerParams(dimension_semantics=sem, vmem_limit_bytes=VMEM_LIMIT)


def _ln_rows(v, g, b):
    mu = jnp.mean(v, axis=-1, keepdims=True)
    d = v - mu
    var = jnp.mean(d * d, axis=-1, keepdims=True)
    return d * lax.rsqrt(var + LN_EPS) * g + b


_SEGS = (('q', 512, BF16), ('cmp', 256, F32), ('ksel', 128, BF16), ('vsel', 128, BF16),
         ('kwin', 128, BF16), ('vwin', 128, BF16), ('gates', 128, F32), ('mq', 512, BF16),
         ('mk', 512, BF16), ('mv', 512, BF16), ('mo', 512, BF16), ('mif', 128, F32))


def _proj_kernel(x_ref, w_ref, *out_refs):
    xb = x_ref[...].astype(BF16)
    off = 0
    for o_ref, (_, wd, _) in zip(out_refs, _SEGS):
        o_ref[...] = jnp.dot(xb, w_ref[:, off:off + wd], preferred_element_type=F32).astype(o_ref.dtype)
        off += wd


def _prep_w_in(w):
    sizes = (512,) + (128,) * 6 + (24,) + (512,) * 4 + (4, 4)
    cuts = np.cumsum(sizes)[:-1].tolist()
    (wq, kc, vc, ks, vs, kw, vw, wg, mq, mk, mv, mo, mi, mf) = jnp.split(w, cuts, axis=1)
    wq = wq.reshape(D_MODEL, NSA_GROUPS, NSA_HPG, NSA_DK).transpose(0, 2, 1, 3).reshape(D_MODEL, 512)
    pad = lambda a: jnp.pad(a, ((0, 0), (0, LANES - a.shape[1])))
    segs = [wq, kc, vc, ks, vs, kw, vw, pad(wg), mq, mk, mv, mo, pad(jnp.concatenate([mi, mf], axis=1))]
    return jnp.concatenate(segs, axis=1).astype(BF16)


def _project(x2d, w_all, tm):
    T = x2d.shape[0]
    n = w_all.shape[1]
    out_shape = tuple(jax.ShapeDtypeStruct((T, wd), dt) for _, wd, dt in _SEGS)
    out_specs = tuple(pl.BlockSpec((tm, wd), lambda i: (i, 0)) for _, wd, _ in _SEGS)
    return pl.pallas_call(
        _proj_kernel, grid=(T // tm,),
        in_specs=[pl.BlockSpec((tm, D_MODEL), lambda i: (i, 0)),
                  pl.BlockSpec((D_MODEL, n), lambda i: (0, 0))],
        out_specs=out_specs, out_shape=out_shape,
        compiler_params=_params(("parallel",)), name="proj")(x2d, w_all)


def _cmp_kernel(r_ref, pa_ref, pb_ref, wa_ref, wb_ref, w2_ref, o_ref):
    r = r_ref[...]
    a = jnp.dot((r + pa_ref[...]).astype(BF16), wa_ref[...], preferred_element_type=F32)
    b = jnp.dot((r + pb_ref[...]).astype(BF16), wb_ref[...], preferred_element_type=F32)
    nr = r.shape[0]
    hid = a + pltpu.roll(b, nr - 1, 0)
    hid = hid * jax.nn.sigmoid(hid)
    out = jnp.dot(hid.astype(BF16), w2_ref[...], preferred_element_type=F32)
    row = lax.broadcasted_iota(I32, out.shape, 0)
    o_ref[...] = jnp.where(row < nr - 1, out, 0.0).astype(o_ref.dtype)


def _prep_cmp(pos_k, w1_k, w2_k, pos_v, w1_v, w2_v):
    eye = jnp.eye(NSA_GROUPS, dtype=F32)

    def expand_w1(w1, half):
        w = w1.reshape(NSA_CMP_LEN, NSA_DK, NSA_DK)[half * 16:(half + 1) * 16]
        return jnp.einsum('jde,gk->jgdke', w, eye).reshape(16, 128, 128)

    def both(fk, fv):
        z = jnp.zeros_like(fk)
        top = jnp.concatenate([fk, z], axis=-1)
        bot = jnp.concatenate([z, fv], axis=-1)
        return jnp.concatenate([top, bot], axis=-2)

    wa = both(expand_w1(w1_k, 0), expand_w1(w1_v, 0)).reshape(16 * 256, 256).astype(BF16)
    wb = both(expand_w1(w1_k, 1), expand_w1(w1_v, 1)).reshape(16 * 256, 256).astype(BF16)
    w2 = both(jnp.kron(eye, w2_k), jnp.kron(eye, w2_v)).astype(BF16)

    def pos_row(half):
        pk = jnp.tile(pos_k[half * 16:(half + 1) * 16], (1, NSA_GROUPS))
        pv = jnp.tile(pos_v[half * 16:(half + 1) * 16], (1, NSA_GROUPS))
        return jnp.concatenate([pk, pv], axis=1).reshape(1, 16 * 256)

    return pos_row(0), pos_row(1), wa, wb, w2


def _compress(cmp2d, B, S, prep):
    pa, pb, wa, wb, w2 = prep
    nr = S // NSA_CMP_STRIDE
    r = cmp2d.reshape(B, nr, NSA_CMP_STRIDE * 256)
    full = lambda a: pl.BlockSpec(a.shape, lambda b: (0,) * a.ndim)
    return pl.pallas_call(
        _cmp_kernel, grid=(B,),
        in_specs=[pl.BlockSpec((None, nr, NSA_CMP_STRIDE * 256), lambda b: (b, 0, 0)),
                  full(pa), full(pb), full(wa), full(wb), full(w2)],
        out_specs=pl.BlockSpec((None, nr, 256), lambda b: (b, 0, 0)),
        out_shape=jax.ShapeDtypeStruct((B, nr, 256), BF16),
        compiler_params=_params(("parallel",)), name="cmp")(r, pa, pb, wa, wb, w2)


def _nsa_consts(S):
    n_cmp = (S - NSA_CMP_LEN) // NSA_CMP_STRIDE + 1
    n_sel = S // NSA_SEL_BLOCK
    cs = np.arange(n_cmp) * NSA_CMP_STRIDE
    ss = np.arange(n_sel) * NSA_SEL_BLOCK
    ov = ((cs[:, None] < ss[None, :] + NSA_SEL_BLOCK) & (cs[:, None] + NSA_CMP_LEN > ss[None, :]))
    ovt = np.zeros((LANES, S // NSA_CMP_STRIDE), np.float32)
    ovt[:n_sel, :n_cmp] = ov.T
    e = np.zeros((LANES, S), np.float32)
    e[np.arange(S) // NSA_SEL_BLOCK, np.arange(S)] = 1.0
    return jnp.asarray(ovt, BF16), jnp.asarray(e, BF16)


def _nsa_kernel(q_ref, kcvc_ref, ksel_ref, vsel_ref, kwin_ref, vwin_ref, gates_ref, ovt_ref, e_ref,
                y_ref, bias_ref, *, tq, ck, n_sel):
    G, H = NSA_GROUPS, NSA_HPG
    GH = G * H
    M = GH * tq
    S = ksel_ref.shape[0]
    W = NSA_WINDOW
    ws = min(W + tq, S)
    t0 = pl.program_id(1) * tq
    gates = jax.nn.sigmoid(gates_ref[...])
    lane = lax.broadcasted_iota(I32, (tq, LANES), 1)
    t_col = t0 + lax.broadcasted_iota(I32, (tq, 1), 0)
    kc = kcvc_ref[:, 0:LANES]
    vc = kcvc_ref[:, LANES:2 * LANES]
    nc = kc.shape[0]
    qs = []
    for g in range(G):
        gmask = (lane // NSA_DK) == g
        for h in range(H):
            qh = q_ref[:, h * LANES:(h + 1) * LANES] * (NSA_DK ** -0.5)
            qs.append(jnp.where(gmask, qh, jnp.zeros_like(qh)))
    Q = jnp.concatenate(qs, axis=0).astype(BF16)

    s = lax.dot_general(Q, kc, _DN_T, preferred_element_type=F32)
    c_idx = lax.broadcasted_iota(I32, (tq, nc), 1)
    cmask = (c_idx * NSA_CMP_STRIDE + NSA_CMP_LEN - 1) <= t_col
    s3 = jnp.where(cmask[None], s.reshape(GH, tq, nc), NEG)
    p = jnp.exp(s3 - jnp.max(s3, axis=-1, keepdims=True))
    p = p / jnp.sum(p, axis=-1, keepdims=True)
    p = jnp.where(cmask[None], p, 0.0)
    o_cmp = jnp.dot(p.reshape(M, nc).astype(BF16), vc, preferred_element_type=F32).reshape(GH, tq, LANES)

    for g in range(G):
        psum = jnp.sum(p[g * H:(g + 1) * H], axis=0)
        hi = psum.astype(BF16)
        lo = (psum - hi.astype(F32)).astype(BF16)
        ovt = ovt_ref[...]
        pslt = (lax.dot_general(ovt, hi, _DN_T, preferred_element_type=F32) +
                lax.dot_general(ovt, lo, _DN_T, preferred_element_type=F32))
        imp_p = pslt[0:n_sel, :]
        n_i = lax.broadcasted_iota(I32, (n_sel, tq), 0)
        cur = (t0 + lax.broadcasted_iota(I32, (n_sel, tq), 1)) // NSA_SEL_BLOCK
        forced = (n_i == 0) | (n_i == cur) | (n_i == cur - 1)
        imp = jnp.where(n_i <= cur, imp_p + jnp.where(forced, FORCE_BONUS, 0.0), NEG)
        cnt = jnp.zeros((n_sel, tq), F32)
        for m in range(n_sel):
            row = imp[m:m + 1, :]
            beats = (row > imp) | ((row == imp) & (n_i > m))
            cnt = cnt + jnp.where(beats, 1.0, 0.0)
        selt = jnp.where(cnt < float(min(NSA_SEL_TOPN, n_sel)), 1.0, 0.0)
        selt = jnp.concatenate([selt, jnp.zeros((LANES - n_sel, tq), F32)], axis=0)
        sel = selt.T.astype(BF16)
        maskf = jnp.dot(sel, e_ref[...], preferred_element_type=F32)
        kpos = lax.broadcasted_iota(I32, (tq, S), 1)
        bias_ref[g] = jnp.where((maskf > 0.5) & (kpos <= t_col), 0.0, NEG)

    vlane = lax.broadcasted_iota(I32, (1, LANES), 1) // NSA_DK

    def pv_with_sums(pb, v):
        outs = []
        for g in range(G):
            vg = jnp.where(vlane == g, v, jnp.ones_like(v))
            outs.append(jnp.dot(pb[g * H * tq:(g + 1) * H * tq], vg, preferred_element_type=F32))
        return jnp.concatenate(outs, axis=0)

    def normalise(acc):
        outs = []
        for r in range(GH):
            c = NSA_DK * (1 - r // H)
            outs.append(acc[r] / acc[r][:, c:c + 1])
        return outs

    def sel_body(j, carry):
        m_i, acc = carry
        ks = pl.multiple_of(j * ck, ck)
        k = ksel_ref[pl.ds(ks, ck), :]
        v = vsel_ref[pl.ds(ks, ck), :]
        sj = lax.dot_general(Q, k, _DN_T, preferred_element_type=F32)
        sj = (sj.reshape(G, H, tq, ck) + bias_ref[:, :, pl.ds(ks, ck)][:, None]).reshape(GH, tq, ck)
        m_new = jnp.maximum(m_i, jnp.max(sj, axis=-1, keepdims=True))
        a = jnp.exp(m_i - m_new)
        pj = jnp.exp(sj - m_new).reshape(M, ck).astype(BF16)
        return m_new, a * acc + pv_with_sums(pj, v).reshape(GH, tq, LANES)

    init = (jnp.full((GH, tq, 1), NEG, F32), jnp.zeros((GH, tq, LANES), F32))
    _, acc = lax.fori_loop(0, (t0 + tq + ck - 1) // ck, sel_body, init)
    o_sel = normalise(acc)

    kst = pl.multiple_of(jnp.clip(t0 - W, 0, S - ws), LANES)
    kwn = kwin_ref[pl.ds(kst, ws), :]
    vwn = vwin_ref[pl.ds(kst, ws), :]
    sw = lax.dot_general(Q, kwn, _DN_T, preferred_element_type=F32)
    wpos = kst + lax.broadcasted_iota(I32, (tq, ws), 1)
    wmask = (wpos <= t_col) & (wpos > t_col - W)
    sw3 = jnp.where(wmask[None], sw.reshape(GH, tq, ws), NEG)
    pw = jnp.exp(sw3 - jnp.max(sw3, axis=-1, keepdims=True)).reshape(M, ws).astype(BF16)
    o_win = normalise(pv_with_sums(pw, vwn).reshape(GH, tq, LANES))

    g0mask = lane < NSA_DK
    for h in range(H):
        o_g = []
        for g in range(G):
            r = g * H + h
            c0 = r * 3
            o_g.append(gates[:, c0:c0 + 1] * o_cmp[r] + gates[:, c0 + 1:c0 + 2] * o_sel[r] +
                       gates[:, c0 + 2:c0 + 3] * o_win[r])
        y_ref[:, h * LANES:(h + 1) * LANES] = jnp.where(g0mask, o_g[0], o_g[1]).astype(y_ref.dtype)


def _nsa(q, kcvc, ksel, vsel, kwin, vwin, gates, B, S, tq, ck):
    T = B * S
    nq = S // tq
    ovt, e = _nsa_consts(S)
    seq = lambda a: a.reshape(B, S, LANES)
    kv_spec = pl.BlockSpec((None, S, LANES), lambda b, i: (b, 0, 0))
    kern = functools.partial(_nsa_kernel, tq=tq, ck=ck, n_sel=S // NSA_SEL_BLOCK)
    return pl.pallas_call(
        kern, grid=(B, nq),
        in_specs=[pl.BlockSpec((tq, 512), lambda b, i: (b * nq + i, 0)),
                  pl.BlockSpec((None,) + kcvc.shape[1:], lambda b, i: (b, 0, 0)),
                  kv_spec, kv_spec, kv_spec, kv_spec,
                  pl.BlockSpec((tq, LANES), lambda b, i: (b * nq + i, 0)),
                  pl.BlockSpec(ovt.shape, lambda b, i: (0, 0)),
                  pl.BlockSpec(e.shape, lambda b, i: (0, 0))],
        out_specs=pl.BlockSpec((tq, 512), lambda b, i: (b * nq + i, 0)),
        out_shape=jax.ShapeDtypeStruct((T, 512), BF16),
        scratch_shapes=[pltpu.VMEM((NSA_GROUPS, tq, S), F32)],
        compiler_params=_params(("parallel", "parallel")), name="nsa",
    )(q, kcvc, seq(ksel), seq(vsel), seq(kwin), seq(vwin), gates, ovt, e)


def _mlstm_kernel(q_ref, k_ref, v_ref, o_ref, gn_ref, gt_ref, cw_ref, bn_ref, bt_ref, ng_ref, tri_ref,
                  y_ref, c_scr, n_scr):
    H, dh, L = ML_HEADS, ML_DH, ML_CHUNK
    nb, S = q_ref.shape[0], q_ref.shape[1]
    nchunk = S // L
    c_scr[...] = jnp.zeros_like(c_scr)
    n_scr[...] = jnp.zeros_like(n_scr)
    row = lax.broadcasted_iota(I32, (L, H * dh), 0)
    li = lax.broadcasted_iota(I32, (L, L), 0)
    mi = lax.broadcasted_iota(I32, (L, L), 1)
    causal = mi <= li
    tril = tri_ref[0]
    triu = tri_ref[1]
    hp = lax.Precision.HIGHEST

    def conv_silu(ref, bi, c, wofs):
        r0 = pl.multiple_of(c * L, L)
        rp = pl.multiple_of(jnp.maximum(c - 1, 0) * L, L)
        cur = ref[bi, pl.ds(r0, L), :].astype(F32)
        prev = ref[bi, pl.ds(rp, L), :].astype(F32) * jnp.where(c > 0, 1.0, 0.0)
        acc = cur * cw_ref[ML_CONV - 1:ML_CONV, wofs:wofs + H * dh]
        for j in range(1, ML_CONV):
            sh = jnp.where(row < j, pltpu.roll(prev, j, 0), pltpu.roll(cur, j, 0))
            acc = acc + sh * cw_ref[ML_CONV - 1 - j:ML_CONV - j, wofs:wofs + H * dh]
        return acc * jax.nn.sigmoid(acc)

    def body(c, m_state):
        r0 = pl.multiple_of(c * L, L)
        new_m = []
        for bi in range(nb):
            qa = conv_silu(q_ref, bi, c, 0) * (dh ** -0.5)
            ka = conv_silu(k_ref, bi, c, H * dh)
            va = v_ref[bi, pl.ds(r0, L), :]
            oa = o_ref[bi, pl.ds(r0, L), :].astype(F32)
            gn = gn_ref[bi, pl.ds(r0, L), :] + bn_ref[...]
            gt = gt_ref[bi, :, c, :] + bt_ref[...]
            lf_n = jax.nn.log_sigmoid(gn)
            lf_t = jax.nn.log_sigmoid(gt)
            b_n = jnp.dot(tril, lf_n, precision=hp, preferred_element_type=F32)
            b_t = jnp.dot(lf_t, triu, precision=hp, preferred_element_type=F32)
            for h in range(H):
                st = bi * H + h
                q = qa[:, h * dh:(h + 1) * dh]
                k = ka[:, h * dh:(h + 1) * dh]
                v = va[:, h * dh:(h + 1) * dh]
                m_old = m_state[st]
                b_col = b_n[:, H + h:H + h + 1]
                i_col = gn[:, h:h + 1]
                b_row = b_t[H + h:H + h + 1, :]
                i_row = gt[h:h + 1, :]
                g_tot = b_t[H + h:H + h + 1, L - 1:L]
                d_log = jnp.where(causal, b_col - b_row + i_row, NEG)
                inter = b_col + m_old
                m_q = jnp.maximum(inter, jnp.max(d_log, axis=-1, keepdims=True))
                w_intra = jnp.exp(d_log - m_q)
                w_inter = jnp.exp(inter - m_q)
                qb = q.astype(BF16)
                s = lax.dot_general(qb, k.astype(BF16), _DN_T, preferred_element_type=F32) * w_intra
                cst = c_scr[st]
                nst = n_scr[st]
                num = (w_inter * jnp.dot(qb, cst.astype(BF16), preferred_element_type=F32) +
                       jnp.dot(s.astype(BF16), v, preferred_element_type=F32))
                den = w_inter * jnp.sum(q * nst, axis=-1, keepdims=True) + jnp.sum(s, axis=-1, keepdims=True)
                hv = num / jnp.maximum(jnp.abs(den), jnp.exp(-m_q))
                log_k = g_tot - b_col + i_col
                m_new = jnp.maximum(g_tot + m_old, jnp.max(log_k, axis=0, keepdims=True))
                wk = jnp.exp(log_k - m_new)
                decay = jnp.exp(g_tot + m_old - m_new)
                kw = k * wk
                c_scr[st] = decay * cst + lax.dot_general(kw.astype(BF16), v, _DN_TA, preferred_element_type=F32)
                n_scr[st] = decay * nst + jnp.sum(kw, axis=0, keepdims=True)
                new_m.append(m_new)
                mu = jnp.mean(hv, axis=-1, keepdims=True)
                dv = hv - mu
                var = jnp.mean(dv * dv, axis=-1, keepdims=True)
                hn = dv * lax.rsqrt(var + LN_EPS) * ng_ref[:, h * dh:(h + 1) * dh]
                og = jax.nn.sigmoid(oa[:, h * dh:(h + 1) * dh])
                y_ref[bi, pl.ds(r0, L), h * dh:(h + 1) * dh] = (og * hn).astype(y_ref.dtype)
        return tuple(new_m)

    lax.fori_loop(0, nchunk, body, tuple(jnp.zeros((1, 1), F32) for _ in range(nb * H)))


def _mlstm(mq, mk, mv, mo, mif, conv_w, i_bias, f_bias, norm_g, B, S, nb):
    T = B * S
    H, dh, L = ML_HEADS, ML_DH, ML_CHUNK
    W = H * dh
    gt = mif[:, :2 * H].reshape(B, S, 2 * H).transpose(0, 2, 1).reshape(B, 2 * H, S // L, L)
    cw = conv_w.reshape(ML_CONV, 2 * W)
    bias = jnp.concatenate([i_bias, f_bias])
    bn = jnp.pad(bias, (0, LANES - 2 * H)).reshape(1, LANES)
    bt = bias.reshape(2 * H, 1)
    ng = norm_g.reshape(1, W)
    tri = jnp.stack([jnp.tril(jnp.ones((L, L), F32)), jnp.triu(jnp.ones((L, L), F32))])
    seq = lambda a: a.reshape(B, S, a.shape[1])
    rows = lambda w: pl.BlockSpec((nb, S, w), lambda b: (b, 0, 0))
    full = lambda a: pl.BlockSpec(a.shape, lambda b: (0,) * a.ndim)
    y = pl.pallas_call(
        _mlstm_kernel, grid=(B // nb,),
        in_specs=[rows(W), rows(W), rows(W), rows(W), rows(LANES),
                  pl.BlockSpec((nb, 2 * H, S // L, L), lambda b: (b, 0, 0, 0)),
                  full(cw), full(bn), full(bt), full(ng), full(tri)],
        out_specs=rows(W),
        out_shape=jax.ShapeDtypeStruct((B, S, W), BF16),
        scratch_shapes=[pltpu.VMEM((nb * H, dh, dh), F32), pltpu.VMEM((nb * H, 1, dh), F32)],
        compiler_params=_params(("parallel",)), name="mlstm",
    )(seq(mq), seq(mk), seq(mv), seq(mo), seq(mif), gt, cw, bn, bt, ng, tri)
    return y.reshape(T, W)


def _outproj_kernel(yn_ref, ym_ref, x_ref, w_ref, g_ref, b_ref, o_ref):
    mix = (jnp.dot(yn_ref[...], w_ref[0:512, :], preferred_element_type=F32) +
           jnp.dot(ym_ref[...], w_ref[512:1024, :], preferred_element_type=F32))
    o_ref[...] = _ln_rows(ALPHA * x_ref[...] + mix, g_ref[...], b_ref[...])


def _outproj(y_nsa, y_ml, x2d, w_out, g, b, tm):
    T = x2d.shape[0]
    wn = w_out[:512].reshape(NSA_GROUPS, NSA_HPG, NSA_DK, D_MODEL).transpose(1, 0, 2, 3).reshape(512, D_MODEL)
    w = jnp.concatenate([wn, w_out[512:]], axis=0).astype(BF16)
    row = lambda wd: pl.BlockSpec((tm, wd), lambda i: (i, 0))
    full = lambda a: pl.BlockSpec(a.shape, lambda i: (0,) * a.ndim)
    g2, b2 = g.reshape(1, -1), b.reshape(1, -1)
    return pl.pallas_call(
        _outproj_kernel, grid=(T // tm,),
        in_specs=[row(512), row(512), row(D_MODEL), full(w), full(g2), full(b2)],
        out_specs=row(D_MODEL), out_shape=jax.ShapeDtypeStruct((T, D_MODEL), F32),
        compiler_params=_params(("parallel",)), name="outproj")(y_nsa, y_ml, x2d, w, g2, b2)


def _memkv_kernel(m_ref, w_ref, o_ref):
    o_ref[...] = jnp.dot(m_ref[...].astype(BF16), w_ref[...], preferred_element_type=F32).astype(o_ref.dtype)


def _memkv(mem2d, wk, wv):
    w = jnp.concatenate([wk, wv], axis=1).astype(BF16)
    R = mem2d.shape[0]
    return pl.pallas_call(
        _memkv_kernel, grid=(R // MEM_LEN,),
        in_specs=[pl.BlockSpec((MEM_LEN, D_MODEL), lambda i: (i, 0)),
                  pl.BlockSpec(w.shape, lambda i: (0, 0))],
        out_specs=pl.BlockSpec((MEM_LEN, 2 * D_MODEL), lambda i: (i, 0)),
        out_shape=jax.ShapeDtypeStruct((R, 2 * D_MODEL), BF16),
        compiler_params=_params(("parallel",)), name="memkv")(mem2d, w)


def _xattn_kernel(x_ref, kv_ref, wq_ref, wo_ref, g_ref, b_ref, rw_ref, x2_ref, x2p_ref, sc_ref):
    x1 = x_ref[...]
    q = jnp.dot(x1.astype(BF16), wq_ref[...], preferred_element_type=F32).astype(BF16)
    outs = []
    for h in range(XA_HEADS):
        qh = q[:, h * XA_DH:(h + 1) * XA_DH]
        kh = kv_ref[:, h * XA_DH:(h + 1) * XA_DH]
        vh = kv_ref[:, D_MODEL + h * XA_DH:D_MODEL + (h + 1) * XA_DH]
        s = lax.dot_general(qh, kh, _DN_T, preferred_element_type=F32) * (XA_DH ** -0.5)
        p = jnp.exp(s - jnp.max(s, axis=-1, keepdims=True))
        p = p / jnp.sum(p, axis=-1, keepdims=True)
        outs.append(jnp.dot(p.astype(BF16), vh, preferred_element_type=F32).astype(BF16))
    o = jnp.concatenate(outs, axis=1)
    xa = jnp.dot(o, wo_ref[...], preferred_element_type=F32)
    x2 = _ln_rows(ALPHA * x1 + xa, g_ref[...], b_ref[...])
    x2_ref[...] = x2
    x2p_ref[...] = _pack_bf16_pairs(x2)
    xh = x2.astype(BF16)
    xl = (x2 - xh.astype(F32)).astype(BF16)
    wh = rw_ref[0]
    wl = rw_ref[1]
    logit = (lax.dot_general(wh, xh, _DN_T, preferred_element_type=F32) +
             lax.dot_general(wh, xl, _DN_T, preferred_element_type=F32) +
             lax.dot_general(wl, xh, _DN_T, preferred_element_type=F32))
    sc_ref[...] = jax.nn.sigmoid(logit)


def _xattn(x1, kv, wq, wo, g, b, router_w, S, tq):
    T = x1.shape[0]
    wqb, wob = wq.astype(BF16), wo.astype(BF16)
    rwt = router_w.T
    rh = rwt.astype(BF16)
    rw = jnp.stack([rh, (rwt - rh.astype(F32)).astype(BF16)])
    g2, b2 = g.reshape(1, -1), b.reshape(1, -1)
    full = lambda a: pl.BlockSpec(a.shape, lambda i: (0,) * a.ndim)
    per = S // tq
    return pl.pallas_call(
        _xattn_kernel, grid=(T // tq,),
        in_specs=[pl.BlockSpec((tq, D_MODEL), lambda i: (i, 0)),
                  pl.BlockSpec((MEM_LEN, 2 * D_MODEL), lambda i: (i // per, 0)),
                  full(wqb), full(wob), full(g2), full(b2), full(rw)],
        out_specs=(pl.BlockSpec((tq, D_MODEL), lambda i: (i, 0)),
                   pl.BlockSpec((tq, D_MODEL // 2), lambda i: (i, 0)),
                   pl.BlockSpec((MOE_E, tq), lambda i: (0, i))),
        out_shape=(jax.ShapeDtypeStruct((T, D_MODEL), F32), jax.ShapeDtypeStruct((T, D_MODEL // 2), I32),
                   jax.ShapeDtypeStruct((MOE_E, T), F32)),
        compiler_params=_params(("parallel",)), name="xattn")(x1, kv, wqb, wob, g2, b2, rw)


def _route_kernel(sc_ref, rb_ref, idx_ref, w_ref):
    E, G = MOE_E, MOE_GROUPS
    per = E // G
    scores = sc_ref[...]
    tr = scores.shape[1]
    biased = scores + rb_ref[...]
    g3 = biased.reshape(G, per, tr)
    j3 = lax.broadcasted_iota(I32, (G, per, tr), 1)
    m1 = jnp.max(g3, axis=1, keepdims=True)
    first = jnp.min(jnp.where(g3 == m1, j3, per), axis=1, keepdims=True)
    m2 = jnp.max(jnp.where(j3 == first, -jnp.inf, g3), axis=1, keepdims=True)
    gs = (m1 + m2).reshape(G, tr)
    gi = lax.broadcasted_iota(I32, (G, tr), 0)
    cnt = jnp.zeros((G, tr), F32)
    for m in range(G):
        row = gs[m:m + 1, :]
        cnt = cnt + jnp.where((row > gs) | ((row == gs) & (gi > m)), 1.0, 0.0)
    gmask = cnt < float(MOE_TOPK_GROUPS)
    masked = jnp.where(gmask[:, None, :], g3, NEG).reshape(E, tr)
    ei = lax.broadcasted_iota(I32, (E, tr), 0)
    idxs, ws = [], []
    for _ in range(MOE_K):
        mx = jnp.max(masked, axis=0, keepdims=True)
        ix = jnp.min(jnp.where(masked == mx, ei, E), axis=0, keepdims=True)
        hit = ei == ix
        ws.append(jnp.sum(jnp.where(hit, scores, 0.0), axis=0, keepdims=True))
        idxs.append(ix)
        masked = jnp.where(hit, -jnp.inf, masked)
    w = jnp.concatenate(ws, axis=0)
    idx_ref[...] = jnp.concatenate(idxs, axis=0)
    w_ref[...] = w / jnp.sum(w, axis=0, keepdims=True) * MOE_ROUTE_SCALE


def _route(scores_t, router_bias, tr):
    E, T = scores_t.shape
    rb = router_bias.reshape(E, 1)
    return pl.pallas_call(
        _route_kernel, grid=(T // tr,),
        in_specs=[pl.BlockSpec((E, tr), lambda i: (0, i)), pl.BlockSpec((E, 1), lambda i: (0, 0))],
        out_specs=(pl.BlockSpec((MOE_K, tr), lambda i: (0, i)), pl.BlockSpec((MOE_K, tr), lambda i: (0, i))),
        out_shape=(jax.ShapeDtypeStruct((MOE_K, T), I32), jax.ShapeDtypeStruct((MOE_K, T), F32)),
        compiler_params=_params(("parallel",)), name="route")(scores_t, rb)


def _rank_kernel(idx_ref, u_ref, rank_ref, cnt_ref, carry):
    E = MOE_E

    @pl.when(pl.program_id(0) == 0)
    def _():
        carry[...] = jnp.zeros_like(carry)

    idx = idx_ref[...]
    tp = idx.shape[1]
    ei = lax.broadcasted_iota(I32, (E, tp), 0)
    hits = [ei == idx[k:k + 1, :] for k in range(MOE_K)]
    onehot = jnp.zeros((E, tp), F32)
    for hit in hits:
        onehot = onehot + jnp.where(hit, 1.0, 0.0)
    pos = jnp.dot(onehot.astype(BF16), u_ref[...], preferred_element_type=F32) + carry[...]
    ranks = [jnp.sum(jnp.where(hit, pos, 0.0), axis=0, keepdims=True) for hit in hits]
    rank_ref[...] = jnp.concatenate(ranks, axis=0).astype(I32)
    total = carry[...] + jnp.sum(onehot, axis=1, keepdims=True)
    carry[...] = total
    cnt_ref[...] = jnp.broadcast_to(total, cnt_ref.shape).astype(I32)


def _rank(idx_t, tp):
    K, T = idx_t.shape
    u = jnp.triu(jnp.ones((tp, tp), F32), k=1).astype(BF16)
    rank, cnt = pl.pallas_call(
        _rank_kernel, grid=(T // tp,),
        in_specs=[pl.BlockSpec((K, tp), lambda i: (0, i)), pl.BlockSpec((tp, tp), lambda i: (0, 0))],
        out_specs=(pl.BlockSpec((K, tp), lambda i: (0, i)), pl.BlockSpec((MOE_E, LANES), lambda i: (0, 0))),
        out_shape=(jax.ShapeDtypeStruct((K, T), I32), jax.ShapeDtypeStruct((MOE_E, LANES), I32)),
        scratch_shapes=[pltpu.VMEM((MOE_E, 1), F32)],
        compiler_params=_params(("arbitrary",)), name="rank")(idx_t, u)
    return rank, cnt[:, 0]


def _dest_kernel(idx_ref, rank_ref, po_ref, dest_ref):
    idx = idx_ref[...]
    tp = idx.shape[1]
    ei = lax.broadcasted_iota(I32, (MOE_E, tp), 0)
    po = po_ref[...]
    base = [jnp.sum(jnp.where(ei == idx[k:k + 1, :], po, 0.0), axis=0, keepdims=True) for k in range(MOE_K)]
    dest_ref[...] = jnp.concatenate(base, axis=0).astype(I32) + rank_ref[...]


def _dest(idx_t, rank_t, poffs, tp):
    K, T = idx_t.shape
    po = poffs.astype(F32).reshape(MOE_E, 1)
    spec = pl.BlockSpec((K, tp), lambda i: (0, i))
    return pl.pallas_call(
        _dest_kernel, grid=(T // tp,),
        in_specs=[spec, spec, pl.BlockSpec((MOE_E, 1), lambda i: (0, 0))],
        out_specs=spec, out_shape=jax.ShapeDtypeStruct((K, T), I32),
        compiler_params=_params(("parallel",)), name="dest")(idx_t, rank_t, po)


def _pack_bf16_pairs(v):
    m = v.shape[1] // 2
    bits = lax.bitcast_convert_type(v.astype(BF16).astype(F32), jnp.uint32)
    return lax.bitcast_convert_type((bits[:, :m] >> 16) | (bits[:, m:] & jnp.uint32(0xFFFF0000)), I32)


def _unpack_bf16_pairs(w):
    w = lax.bitcast_convert_type(w, jnp.uint32)
    lo = lax.bitcast_convert_type(w << 16, F32)
    hi = lax.bitcast_convert_type(w & jnp.uint32(0xFFFF0000), F32)
    return lo, hi


def _row_scatter(rows, dest_t, n_rows):
    T, d = rows.shape
    K = dest_t.shape[0]
    nw = SC_CORES * SC_SUBCORES
    per_w = T // nw
    ch = SC_GATHER_ROWS
    mesh = plsc.VectorSubcoreMesh(core_axis_name="c", subcore_axis_name="s")

    @functools.partial(
        pl.kernel, mesh=mesh, out_type=jax.ShapeDtypeStruct((n_rows, d), I32),
        scratch_types=[pltpu.VMEM((ch,), I32), pltpu.VMEM((ch, d), I32), pltpu.SemaphoreType.DMA],
        name="row_scatter")
    def scatter(rows_hbm, dest_hbm, out_hbm, idx_v, rows_v, sem):
        wid = lax.axis_index("s") * SC_CORES + lax.axis_index("c")
        base = wid * per_w

        @pl.loop(0, per_w // ch)
        def _(i):
            off = pl.multiple_of(base + i * ch, ch)
            pltpu.sync_copy(rows_hbm.at[pl.ds(off, ch)], rows_v)
            for k in range(K):
                pltpu.sync_copy(dest_hbm.at[k, pl.ds(off, ch)], idx_v)
                pltpu.async_copy(rows_v, out_hbm.at[idx_v], sem).wait()

    return scatter(rows, dest_t)


def _expert_kernel(be_ref, nv_ref, xs_ref, w1_ref, w3_ref, w2_ref, ys_ref):
    i = pl.program_id(0)
    nv = nv_ref[i]
    hw = D_MODEL // 2
    half = xs_ref.shape[0] // 2

    def ffn(r0, n):
        words = xs_ref[r0:r0 + n, :]
        row = r0 + lax.broadcasted_iota(I32, words.shape, 0)
        lo, hi = _unpack_bf16_pairs(jnp.where(row < nv, words, 0))
        lo, hi = lo.astype(BF16), hi.astype(BF16)
        w1 = w1_ref[...].astype(BF16)
        w3 = w3_ref[...].astype(BF16)
        a = (jnp.dot(lo, w1[:hw], preferred_element_type=F32) + jnp.dot(hi, w1[hw:], preferred_element_type=F32))
        c = (jnp.dot(lo, w3[:hw], preferred_element_type=F32) + jnp.dot(hi, w3[hw:], preferred_element_type=F32))
        h = (a * jax.nn.sigmoid(a) * c).astype(BF16)
        ys_ref[r0:r0 + n, :] = _pack_bf16_pairs(jnp.dot(h, w2_ref[...].astype(BF16), preferred_element_type=F32))

    @pl.when(nv > half)
    def _():
        ffn(0, 2 * half)

    @pl.when((nv > 0) & (nv <= half))
    def _():
        ffn(0, half)
        ys_ref[half:, :] = jnp.zeros((half, ys_ref.shape[1]), ys_ref.dtype)

    @pl.when(nv <= 0)
    def _():
        ys_ref[...] = jnp.zeros_like(ys_ref)


def _experts(blk_exp, blk_valid, xs, w1, w3, w2):
    n_rows, wp = xs.shape
    rb = ROW_BLOCK
    gs = pltpu.PrefetchScalarGridSpec(
        num_scalar_prefetch=2, grid=(n_rows // rb,),
        in_specs=[pl.BlockSpec((rb, wp), lambda i, be, nv: (i, 0)),
                  pl.BlockSpec((None, D_MODEL, MOE_FF), lambda i, be, nv: (be[i], 0, 0)),
                  pl.BlockSpec((None, D_MODEL, MOE_FF), lambda i, be, nv: (be[i], 0, 0)),
                  pl.BlockSpec((None, MOE_FF, D_MODEL), lambda i, be, nv: (be[i], 0, 0))],
        out_specs=pl.BlockSpec((rb, wp), lambda i, be, nv: (i, 0)))
    return pl.pallas_call(
        _expert_kernel, grid_spec=gs, out_shape=jax.ShapeDtypeStruct((n_rows, wp), I32),
        compiler_params=_params(("arbitrary",)), name="experts")(blk_exp, blk_valid, xs, w1, w3, w2)


def _row_gather(table, idx):
    n, d = idx.shape[0], table.shape[1]
    nw = SC_CORES * SC_SUBCORES
    per_w = n // nw
    ch = SC_GATHER_ROWS
    mesh = plsc.VectorSubcoreMesh(core_axis_name="c", subcore_axis_name="s")

    @functools.partial(
        pl.kernel, mesh=mesh, out_type=jax.ShapeDtypeStruct((n, d), I32),
        scratch_types=[pltpu.VMEM((ch,), I32), pltpu.VMEM((ch, d), I32), pltpu.SemaphoreType.DMA],
        name="row_gather")
    def gather(table_hbm, idx_hbm, out_hbm, idx_v, rows_v, sem):
        wid = lax.axis_index("s") * SC_CORES + lax.axis_index("c")
        base = wid * per_w

        @pl.loop(0, per_w // ch)
        def _(i):
            off = pl.multiple_of(base + i * ch, ch)
            pltpu.sync_copy(idx_hbm.at[pl.ds(off, ch)], idx_v)
            pltpu.async_copy(table_hbm.at[idx_v], rows_v, sem).wait()
            pltpu.sync_copy(rows_v, out_hbm.at[pl.ds(off, ch)])

    return gather(table, idx)


def _combine_kernel(w_ref, tile_ref, x_ref, ysg_ref, s1_ref, s3_ref, s2_ref, g_ref, b_ref, o_ref):
    x2 = x_ref[...]
    xb = x2.astype(BF16)
    a = jnp.dot(xb, s1_ref[...], preferred_element_type=F32)
    c = jnp.dot(xb, s3_ref[...], preferred_element_type=F32)
    shared = jnp.dot((a * jax.nn.sigmoid(a) * c).astype(BF16), s2_ref[...], preferred_element_type=F32)
    lo, hi = _unpack_bf16_pairs(ysg_ref[...])
    lo, hi = lo.astype(BF16), hi.astype(BF16)
    wrep = jnp.dot(w_ref[...], tile_ref[...], precision=lax.Precision.HIGHEST, preferred_element_type=F32)
    col = lax.broadcasted_iota(I32, wrep.shape, 1)
    row = lax.broadcasted_iota(I32, wrep.shape, 0)
    wsel = jnp.where(col // MOE_K == row, wrep, 0.0)
    wh = wsel.astype(BF16)
    wl = (wsel - wh.astype(F32)).astype(BF16)
    y_lo = jnp.dot(wh, lo, preferred_element_type=F32) + jnp.dot(wl, lo, preferred_element_type=F32)
    y_hi = jnp.dot(wh, hi, preferred_element_type=F32) + jnp.dot(wl, hi, preferred_element_type=F32)
    y = shared + jnp.concatenate([y_lo, y_hi], axis=1)
    o_ref[...] = _ln_rows(ALPHA * x2 + y, g_ref[...], b_ref[...])


def _combine(w_nat, x2, ysg, sw1, sw3, sw2, g, b, tc):
    T = x2.shape[0]
    wp = ysg.shape[1]
    full = lambda a: pl.BlockSpec(a.shape, lambda i: (0,) * a.ndim)
    s1, s3, s2 = sw1.astype(BF16), sw3.astype(BF16), sw2.astype(BF16)
    g2, b2 = g.reshape(1, -1), b.reshape(1, -1)
    tile = jnp.asarray(np.arange(tc * MOE_K)[None, :] % MOE_K == np.arange(MOE_K)[:, None], F32)
    return pl.pallas_call(
        _combine_kernel, grid=(T // tc,),
        in_specs=[pl.BlockSpec((tc, MOE_K), lambda i: (i, 0)), full(tile),
                  pl.BlockSpec((tc, D_MODEL), lambda i: (i, 0)),
                  pl.BlockSpec((tc * MOE_K, wp), lambda i: (i, 0)),
                  full(s1), full(s3), full(s2), full(g2), full(b2)],
        out_specs=pl.BlockSpec((tc, D_MODEL), lambda i: (i, 0)),
        out_shape=jax.ShapeDtypeStruct((T, D_MODEL), F32),
        compiler_params=_params(("parallel",)), name="combine",
    )(w_nat, tile, x2, ysg, s1, s3, s2, g2, b2)


def _moe(x2, x2p, scores_t, router_bias, w1, w3, w2, sw1, sw3, sw2, g, b, tiles):
    T = x2.shape[0]
    rb = ROW_BLOCK
    idx_t, w_t = _route(scores_t, router_bias, tiles['route'])
    rank_t, counts = _rank(idx_t, tiles['rank'])
    pcounts = (counts + rb - 1) // rb * rb
    pends = jnp.cumsum(pcounts)
    poffs = (pends - pcounts).astype(I32)
    n_blocks = (T * MOE_K + MOE_E * (rb - 1) + rb - 1) // rb
    blk_start = jnp.arange(n_blocks, dtype=I32) * rb
    blk_exp = jnp.minimum(jnp.searchsorted(pends, blk_start, side='right'), MOE_E - 1).astype(I32)
    blk_valid = jnp.clip(counts[blk_exp] - (blk_start - poffs[blk_exp]), 0, rb).astype(I32)
    dest_t = _dest(idx_t, rank_t, poffs, tiles['rank'])
    xs = _row_scatter(x2p, dest_t, n_blocks * rb)
    ys = _experts(blk_exp, blk_valid, xs, w1, w3, w2)
    ysg = _row_gather(ys, dest_t.T.reshape(-1))
    return _combine(w_t.T, x2, ysg, sw1, sw3, sw2, g, b, tiles['combine'])


def _tiles(B, S):
    T = B * S
    pick = lambda want, n: want if n % want == 0 else n
    return dict(proj=pick(512, T), nsa_q=pick(128, S), nsa_ck=pick(512, S), outproj=pick(512, T),
                mlstm_nb=2 if B % 2 == 0 else 1, xattn=pick(256, S), route=pick(512, T), rank=pick(512, T), scatter=pick(256, T),
                combine=pick(128, T))


def kernel(x, mem, w_in, nsa_pos_k, nsa_cmp_k_w1, nsa_cmp_k_w2, nsa_pos_v, nsa_cmp_v_w1, nsa_cmp_v_w2,
           mlstm_conv_w, mlstm_i_bias, mlstm_f_bias, mlstm_norm_g, w_out, ln1_g, ln1_b,
           xa_wq, xa_wk, xa_wv, xa_wo, ln2_g, ln2_b, router_w, router_bias,
           moe_w1, moe_w3, moe_w2, shared_w1, shared_w3, shared_w2, ln3_g, ln3_b):
    B, S, D = x.shape
    T = B * S
    tl = _tiles(B, S)
    xc = x.reshape(T, D)
    memc = mem.reshape(B * MEM_LEN, D)
    for l in range(w_in.shape[0]):
        (q, cmp, ksel, vsel, kwin, vwin, gates, mq, mk, mv, mo, mif) = _project(xc, _prep_w_in(w_in[l]), tl['proj'])
        kcvc = _compress(cmp, B, S, _prep_cmp(nsa_pos_k[l], nsa_cmp_k_w1[l], nsa_cmp_k_w2[l],
                                              nsa_pos_v[l], nsa_cmp_v_w1[l], nsa_cmp_v_w2[l]))
        y_nsa = _nsa(q, kcvc, ksel, vsel, kwin, vwin, gates, B, S, tl['nsa_q'], tl['nsa_ck'])
        y_ml = _mlstm(mq, mk, mv, mo, mif, mlstm_conv_w[l], mlstm_i_bias[l], mlstm_f_bias[l],
                      mlstm_norm_g[l], B, S, tl['mlstm_nb'])
        x1 = _outproj(y_nsa, y_ml, xc, w_out[l], ln1_g[l], ln1_b[l], tl['outproj'])
        kv = _memkv(memc, xa_wk[l], xa_wv[l])
        x2, x2p, scores_t = _xattn(x1, kv, xa_wq[l], xa_wo[l], ln2_g[l], ln2_b[l], router_w[l], S, tl['xattn'])
        xc = _moe(x2, x2p, scores_t, router_bias[l], moe_w1[l], moe_w3[l], moe_w2[l],
                  shared_w1[l], shared_w3[l], shared_w2[l], ln3_g[l], ln3_b[l], tl)
    return xc.reshape(B, S, D)
```

```python
import functools
import numpy as np
import jax
import jax.numpy as jnp
from jax import lax
from jax.experimental import pallas as pl
from jax.experimental.pallas import tpu as pltpu
from jax.experimental.pallas import tpu_sc as plsc

F32 = jnp.float32
BF16 = jnp.bfloat16
I32 = jnp.int32

D_MODEL = 1024
MEM_LEN = 256
NSA_HEADS = 8
NSA_GROUPS = 2
NSA_HPG = 4
NSA_DK = 64
NSA_CMP_LEN = 32
NSA_CMP_STRIDE = 16
NSA_SEL_BLOCK = 64
NSA_SEL_TOPN = 8
NSA_WINDOW = 512
ML_HEADS = 4
ML_DH = 128
ML_CHUNK = 64
ML_CONV = 4
XA_HEADS = 4
XA_DH = 256
MOE_E = 256
MOE_K = 8
MOE_GROUPS = 8
MOE_TOPK_GROUPS = 4
MOE_FF = 256
MOE_ROUTE_SCALE = 2.5
DEPTH = 1
ALPHA = (2.0 * DEPTH) ** 0.25
LN_EPS = 1e-5
NEG = -1e30
FORCE_BONUS = 1e4

LANES = 128
ROW_BLOCK = 256
VMEM_LIMIT = 56 * 1024 * 1024
SC_CORES = 2
SC_SUBCORES = 16
SC_GATHER_ROWS = 128

_DN_T = (((1,), (1,)), ((), ()))
_DN_TA = (((0,), (0,)), ((), ()))


def _params(sem):
    return pltpu.CompilerParams(dimension_semantics=sem, vmem_limit_bytes=VMEM_LIMIT)


def _ln_rows(v, g, b):
    mu = jnp.mean(v, axis=-1, keepdims=True)
    d = v - mu
    var = jnp.mean(d * d, axis=-1, keepdims=True)
    return d * lax.rsqrt(var + LN_EPS) * g + b


_SEGS = (('q', 512, BF16), ('cmp', 256, F32), ('ksel', 128, BF16), ('vsel', 128, BF16),
         ('kwin', 128, BF16), ('vwin', 128, BF16), ('gates', 128, F32), ('mq', 512, BF16),
         ('mk', 512, BF16), ('mv', 512, BF16), ('mo', 512, BF16), ('mif', 128, F32))


def _proj_kernel(x_ref, w_ref, *out_refs):
    xb = x_ref[...].astype(BF16)
    off = 0
    for o_ref, (_, wd, _) in zip(out_refs, _SEGS):
        o_ref[...] = jnp.dot(xb, w_ref[:, off:off + wd], preferred_element_type=F32).astype(o_ref.dtype)
        off += wd


def _prep_w_in(w):
    sizes = (512,) + (128,) * 6 + (24,) + (512,) * 4 + (4, 4)
    cuts = np.cumsum(sizes)[:-1].tolist()
    (wq, kc, vc, ks, vs, kw, vw, wg, mq, mk, mv, mo, mi, mf) = jnp.split(w, cuts, axis=1)
    wq = wq.reshape(D_MODEL, NSA_GROUPS, NSA_HPG, NSA_DK).transpose(0, 2, 1, 3).reshape(D_MODEL, 512)
    pad = lambda a: jnp.pad(a, ((0, 0), (0, LANES - a.shape[1])))
    segs = [wq, kc, vc, ks, vs, kw, vw, pad(wg), mq, mk, mv, mo, pad(jnp.concatenate([mi, mf], axis=1))]
    return jnp.concatenate(segs, axis=1).astype(BF16)


def _project(x2d, w_all, tm):
    T = x2d.shape[0]
    n = w_all.shape[1]
    out_shape = tuple(jax.ShapeDtypeStruct((T, wd), dt) for _, wd, dt in _SEGS)
    out_specs = tuple(pl.BlockSpec((tm, wd), lambda i: (i, 0)) for _, wd, _ in _SEGS)
    return pl.pallas_call(
        _proj_kernel, grid=(T // tm,),
        in_specs=[pl.BlockSpec((tm, D_MODEL), lambda i: (i, 0)),
                  pl.BlockSpec((D_MODEL, n), lambda i: (0, 0))],
        out_specs=out_specs, out_shape=out_shape,
        compiler_params=_params(("parallel",)), name="proj")(x2d, w_all)


def _cmp_kernel(r_ref, pa_ref, pb_ref, wa_ref, wb_ref, w2_ref, o_ref):
    r = r_ref[...]
    a = jnp.dot((r + pa_ref[...]).astype(BF16), wa_ref[...], preferred_element_type=F32)
    b = jnp.dot((r + pb_ref[...]).astype(BF16), wb_ref[...], preferred_element_type=F32)
    nr = r.shape[0]
    hid = a + pltpu.roll(b, nr - 1, 0)
    hid = hid * jax.nn.sigmoid(hid)
    out = jnp.dot(hid.astype(BF16), w2_ref[...], preferred_element_type=F32)
    row = lax.broadcasted_iota(I32, out.shape, 0)
    o_ref[...] = jnp.where(row < nr - 1, out, 0.0).astype(o_ref.dtype)


def _prep_cmp(pos_k, w1_k, w2_k, pos_v, w1_v, w2_v):
    eye = jnp.eye(NSA_GROUPS, dtype=F32)

    def expand_w1(w1, half):
        w = w1.reshape(NSA_CMP_LEN, NSA_DK, NSA_DK)[half * 16:(half + 1) * 16]
        return jnp.einsum('jde,gk->jgdke', w, eye).reshape(16, 128, 128)

    def both(fk, fv):
        z = jnp.zeros_like(fk)
        top = jnp.concatenate([fk, z], axis=-1)
        bot = jnp.concatenate([z, fv], axis=-1)
        return jnp.concatenate([top, bot], axis=-2)

    wa = both(expand_w1(w1_k, 0), expand_w1(w1_v, 0)).reshape(16 * 256, 256).astype(BF16)
    wb = both(expand_w1(w1_k, 1), expand_w1(w1_v, 1)).reshape(16 * 256, 256).astype(BF16)
    w2 = both(jnp.kron(eye, w2_k), jnp.kron(eye, w2_v)).astype(BF16)

    def pos_row(half):
        pk = jnp.tile(pos_k[half * 16:(half + 1) * 16], (1, NSA_GROUPS))
        pv = jnp.tile(pos_v[half * 16:(half + 1) * 16], (1, NSA_GROUPS))
        return jnp.concatenate([pk, pv], axis=1).reshape(1, 16 * 256)

    return pos_row(0), pos_row(1), wa, wb, w2


def _compress(cmp2d, B, S, prep):
    pa, pb, wa, wb, w2 = prep
    nr = S // NSA_CMP_STRIDE
    r = cmp2d.reshape(B, nr, NSA_CMP_STRIDE * 256)
    full = lambda a: pl.BlockSpec(a.shape, lambda b: (0,) * a.ndim)
    return pl.pallas_call(
        _cmp_kernel, grid=(B,),
        in_specs=[pl.BlockSpec((None, nr, NSA_CMP_STRIDE * 256), lambda b: (b, 0, 0)),
                  full(pa), full(pb), full(wa), full(wb), full(w2)],
        out_specs=pl.BlockSpec((None, nr, 256), lambda b: (b, 0, 0)),
        out_shape=jax.ShapeDtypeStruct((B, nr, 256), BF16),
        compiler_params=_params(("parallel",)), name="cmp")(r, pa, pb, wa, wb, w2)


def _nsa_consts(S):
    n_cmp = (S - NSA_CMP_LEN) // NSA_CMP_STRIDE + 1
    n_sel = S // NSA_SEL_BLOCK
    cs = np.arange(n_cmp) * NSA_CMP_STRIDE
    ss = np.arange(n_sel) * NSA_SEL_BLOCK
    ov = ((cs[:, None] < ss[None, :] + NSA_SEL_BLOCK) & (cs[:, None] + NSA_CMP_LEN > ss[None, :]))
    ovt = np.zeros((LANES, S // NSA_CMP_STRIDE), np.float32)
    ovt[:n_sel, :n_cmp] = ov.T
    e = np.zeros((LANES, S), np.float32)
    e[np.arange(S) // NSA_SEL_BLOCK, np.arange(S)] = 1.0
    return jnp.asarray(ovt, BF16), jnp.asarray(e, BF16)


def _nsa_kernel(q_ref, kcvc_ref, ksel_ref, vsel_ref, kwin_ref, vwin_ref, gates_ref, ovt_ref, e_ref,
                y_ref, bias_ref, *, tq, ck, n_sel):
    G, H = NSA_GROUPS, NSA_HPG
    GH = G * H
    M = GH * tq
    S = ksel_ref.shape[0]
    W = NSA_WINDOW
    ws = min(W + tq, S)
    t0 = pl.program_id(1) * tq
    gates = jax.nn.sigmoid(gates_ref[...])
    lane = lax.broadcasted_iota(I32, (tq, LANES), 1)
    t_col = t0 + lax.broadcasted_iota(I32, (tq, 1), 0)
    kc = kcvc_ref[:, 0:LANES]
    vc = kcvc_ref[:, LANES:2 * LANES]
    nc = kc.shape[0]
    qs = []
    for g in range(G):
        gmask = (lane // NSA_DK) == g
        for h in range(H):
            qh = q_ref[:, h * LANES:(h + 1) * LANES] * (NSA_DK ** -0.5)
            qs.append(jnp.where(gmask, qh, jnp.zeros_like(qh)))
    Q = jnp.concatenate(qs, axis=0).astype(BF16)

    s = lax.dot_general(Q, kc, _DN_T, preferred_element_type=F32)
    c_idx = lax.broadcasted_iota(I32, (tq, nc), 1)
    cmask = (c_idx * NSA_CMP_STRIDE + NSA_CMP_LEN - 1) <= t_col
    s3 = jnp.where(cmask[None], s.reshape(GH, tq, nc), NEG)
    p = jnp.exp(s3 - jnp.max(s3, axis=-1, keepdims=True))
    p = p / jnp.sum(p, axis=-1, keepdims=True)
    p = jnp.where(cmask[None], p, 0.0)
    o_cmp = jnp.dot(p.reshape(M, nc).astype(BF16), vc, preferred_element_type=F32).reshape(GH, tq, LANES)

    for g in range(G):
        psum = jnp.sum(p[g * H:(g + 1) * H], axis=0)
        hi = psum.astype(BF16)
        lo = (psum - hi.astype(F32)).astype(BF16)
        ovt = ovt_ref[...]
        pslt = (lax.dot_general(ovt, hi, _DN_T, preferred_element_type=F32) +
                lax.dot_general(ovt, lo, _DN_T, preferred_element_type=F32))
        imp_p = pslt[0:n_sel, :]
        n_i = lax.broadcasted_iota(I32, (n_sel, tq), 0)
        cur = (t0 + lax.broadcasted_iota(I32, (n_sel, tq), 1)) // NSA_SEL_BLOCK
        forced = (n_i == 0) | (n_i == cur) | (n_i == cur - 1)
        imp = jnp.where(n_i <= cur, imp_p + jnp.where(forced, FORCE_BONUS, 0.0), NEG)
        cnt = jnp.zeros((n_sel, tq), F32)
        for m in range(n_sel):
            row = imp[m:m + 1, :]
            beats = (row > imp) | ((row == imp) & (n_i > m))
            cnt = cnt + jnp.where(beats, 1.0, 0.0)
        selt = jnp.where(cnt < float(min(NSA_SEL_TOPN, n_sel)), 1.0, 0.0)
        selt = jnp.concatenate([selt, jnp.zeros((LANES - n_sel, tq), F32)], axis=0)
        sel = selt.T.astype(BF16)
        maskf = jnp.dot(sel, e_ref[...], preferred_element_type=F32)
        kpos = lax.broadcasted_iota(I32, (tq, S), 1)
        bias_ref[g] = jnp.where((maskf > 0.5) & (kpos <= t_col), 0.0, NEG)

    vlane = lax.broadcasted_iota(I32, (1, LANES), 1) // NSA_DK

    def pv_with_sums(pb, v):
        outs = []
        for g in range(G):
            vg = jnp.where(vlane == g, v, jnp.ones_like(v))
            outs.append(jnp.dot(pb[g * H * tq:(g + 1) * H * tq], vg, preferred_element_type=F32))
        return jnp.concatenate(outs, axis=0)

    def normalise(acc):
        outs = []
        for r in range(GH):
            c = NSA_DK * (1 - r // H)
            outs.append(acc[r] / acc[r][:, c:c + 1])
        return outs

    def sel_body(j, carry):
        m_i, acc = carry
        ks = pl.multiple_of(j * ck, ck)
        k = ksel_ref[pl.ds(ks, ck), :]
        v = vsel_ref[pl.ds(ks, ck), :]
        sj = lax.dot_general(Q, k, _DN_T, preferred_element_type=F32)
        sj = (sj.reshape(G, H, tq, ck) + bias_ref[:, :, pl.ds(ks, ck)][:, None]).reshape(GH, tq, ck)
        m_new = jnp.maximum(m_i, jnp.max(sj, axis=-1, keepdims=True))
        a = jnp.exp(m_i - m_new)
        pj = jnp.exp(sj - m_new).reshape(M, ck).astype(BF16)
        return m_new, a * acc + pv_with_sums(pj, v).reshape(GH, tq, LANES)

    init = (jnp.full((GH, tq, 1), NEG, F32), jnp.zeros((GH, tq, LANES), F32))
    _, acc = lax.fori_loop(0, (t0 + tq + ck - 1) // ck, sel_body, init)
    o_sel = normalise(acc)

    kst = pl.multiple_of(jnp.clip(t0 - W, 0, S - ws), LANES)
    kwn = kwin_ref[pl.ds(kst, ws), :]
    vwn = vwin_ref[pl.ds(kst, ws), :]
    sw = lax.dot_general(Q, kwn, _DN_T, preferred_element_type=F32)
    wpos = kst + lax.broadcasted_iota(I32, (tq, ws), 1)
    wmask = (wpos <= t_col) & (wpos > t_col - W)
    sw3 = jnp.where(wmask[None], sw.reshape(GH, tq, ws), NEG)
    pw = jnp.exp(sw3 - jnp.max(sw3, axis=-1, keepdims=True)).reshape(M, ws).astype(BF16)
    o_win = normalise(pv_with_sums(pw, vwn).reshape(GH, tq, LANES))

    g0mask = lane < NSA_DK
    for h in range(H):
        o_g = []
        for g in range(G):
            r = g * H + h
            c0 = r * 3
            o_g.append(gates[:, c0:c0 + 1] * o_cmp[r] + gates[:, c0 + 1:c0 + 2] * o_sel[r] +
                       gates[:, c0 + 2:c0 + 3] * o_win[r])
        y_ref[:, h * LANES:(h + 1) * LANES] = jnp.where(g0mask, o_g[0], o_g[1]).astype(y_ref.dtype)


def _nsa(q, kcvc, ksel, vsel, kwin, vwin, gates, B, S, tq, ck):
    T = B * S
    nq = S // tq
    ovt, e = _nsa_consts(S)
    seq = lambda a: a.reshape(B, S, LANES)
    kv_spec = pl.BlockSpec((None, S, LANES), lambda b, i: (b, 0, 0))
    kern = functools.partial(_nsa_kernel, tq=tq, ck=ck, n_sel=S // NSA_SEL_BLOCK)
    return pl.pallas_call(
        kern, grid=(B, nq),
        in_specs=[pl.BlockSpec((tq, 512), lambda b, i: (b * nq + i, 0)),
                  pl.BlockSpec((None,) + kcvc.shape[1:], lambda b, i: (b, 0, 0)),
                  kv_spec, kv_spec, kv_spec, kv_spec,
                  pl.BlockSpec((tq, LANES), lambda b, i: (b * nq + i, 0)),
                  pl.BlockSpec(ovt.shape, lambda b, i: (0, 0)),
                  pl.BlockSpec(e.shape, lambda b, i: (0, 0))],
        out_specs=pl.BlockSpec((tq, 512), lambda b, i: (b * nq + i, 0)),
        out_shape=jax.ShapeDtypeStruct((T, 512), BF16),
        scratch_shapes=[pltpu.VMEM((NSA_GROUPS, tq, S), F32)],
        compiler_params=_params(("parallel", "parallel")), name="nsa",
    )(q, kcvc, seq(ksel), seq(vsel), seq(kwin), seq(vwin), gates, ovt, e)


def _mlstm_kernel(q_ref, k_ref, v_ref, o_ref, gn_ref, gt_ref, cw_ref, bn_ref, bt_ref, ng_ref, tri_ref,
                  y_ref, c_scr, n_scr):
    H, dh, L = ML_HEADS, ML_DH, ML_CHUNK
    nb, S = q_ref.shape[0], q_ref.shape[1]
    nchunk = S // L
    c_scr[...] = jnp.zeros_like(c_scr)
    n_scr[...] = jnp.zeros_like(n_scr)
    row = lax.broadcasted_iota(I32, (L, H * dh), 0)
    li = lax.broadcasted_iota(I32, (L, L), 0)
    mi = lax.broadcasted_iota(I32, (L, L), 1)
    causal = mi <= li
    tril = tri_ref[0]
    triu = tri_ref[1]
    hp = lax.Precision.HIGHEST

    def conv_silu(ref, bi, c, wofs):
        r0 = pl.multiple_of(c * L, L)
        rp = pl.multiple_of(jnp.maximum(c - 1, 0) * L, L)
        cur = ref[bi, pl.ds(r0, L), :].astype(F32)
        prev = ref[bi, pl.ds(rp, L), :].astype(F32) * jnp.where(c > 0, 1.0, 0.0)
        acc = cur * cw_ref[ML_CONV - 1:ML_CONV, wofs:wofs + H * dh]
        for j in range(1, ML_CONV):
            sh = jnp.where(row < j, pltpu.roll(prev, j, 0), pltpu.roll(cur, j, 0))
            acc = acc + sh * cw_ref[ML_CONV - 1 - j:ML_CONV - j, wofs:wofs + H * dh]
        return acc * jax.nn.sigmoid(acc)

    def body(c, m_state):
        r0 = pl.multiple_of(c * L, L)
        new_m = []
        for bi in range(nb):
            qa = conv_silu(q_ref, bi, c, 0) * (dh ** -0.5)
            ka = conv_silu(k_ref, bi, c, H * dh)
            va = v_ref[bi, pl.ds(r0, L), :]
            oa = o_ref[bi, pl.ds(r0, L), :].astype(F32)
            gn = gn_ref[bi, pl.ds(r0, L), :] + bn_ref[...]
            gt = gt_ref[bi, :, c, :] + bt_ref[...]
            lf_n = jax.nn.log_sigmoid(gn)
            lf_t = jax.nn.log_sigmoid(gt)
            b_n = jnp.dot(tril, lf_n, precision=hp, preferred_element_type=F32)
            b_t = jnp.dot(lf_t, triu, precision=hp, preferred_element_type=F32)
            for h in range(H):
                st = bi * H + h
                q = qa[:, h * dh:(h + 1) * dh]
                k = ka[:, h * dh:(h + 1) * dh]
                v = va[:, h * dh:(h + 1) * dh]
                m_old = m_state[st]
                b_col = b_n[:, H + h:H + h + 1]
                i_col = gn[:, h:h + 1]
                b_row = b_t[H + h:H + h + 1, :]
                i_row = gt[h:h + 1, :]
                g_tot = b_t[H + h:H + h + 1, L - 1:L]
                d_log = jnp.where(causal, b_col - b_row + i_row, NEG)
                inter = b_col + m_old
                m_q = jnp.maximum(inter, jnp.max(d_log, axis=-1, keepdims=True))
                w_intra = jnp.exp(d_log - m_q)
                w_inter = jnp.exp(inter - m_q)
                qb = q.astype(BF16)
                s = lax.dot_general(qb, k.astype(BF16), _DN_T, preferred_element_type=F32) * w_intra
                cst = c_scr[st]
                nst = n_scr[st]
                num = (w_inter * jnp.dot(qb, cst.astype(BF16), preferred_element_type=F32) +
                       jnp.dot(s.astype(BF16), v, preferred_element_type=F32))
                den = w_inter * jnp.sum(q * nst, axis=-1, keepdims=True) + jnp.sum(s, axis=-1, keepdims=True)
                hv = num / jnp.maximum(jnp.abs(den), jnp.exp(-m_q))
                log_k = g_tot - b_col + i_col
                m_new = jnp.maximum(g_tot + m_old, jnp.max(log_k, axis=0, keepdims=True))
                wk = jnp.exp(log_k - m_new)
                decay = jnp.exp(g_tot + m_old - m_new)
                kw = k * wk
                c_scr[st] = decay * cst + lax.dot_general(kw.astype(BF16), v, _DN_TA, preferred_element_type=F32)
                n_scr[st] = decay * nst + jnp.sum(kw, axis=0, keepdims=True)
                new_m.append(m_new)
                mu = jnp.mean(hv, axis=-1, keepdims=True)
                dv = hv - mu
                var = jnp.mean(dv * dv, axis=-1, keepdims=True)
                hn = dv * lax.rsqrt(var + LN_EPS) * ng_ref[:, h * dh:(h + 1) * dh]
                og = jax.nn.sigmoid(oa[:, h * dh:(h + 1) * dh])
                y_ref[bi, pl.ds(r0, L), h * dh:(h + 1) * dh] = (og * hn).astype(y_ref.dtype)
        return tuple(new_m)

    lax.fori_loop(0, nchunk, body, tuple(jnp.zeros((1, 1), F32) for _ in range(nb * H)))


def _mlstm(mq, mk, mv, mo, mif, conv_w, i_bias, f_bias, norm_g, B, S, nb):
    T = B * S
    H, dh, L = ML_HEADS, ML_DH, ML_CHUNK
    W = H * dh
    gt = mif[:, :2 * H].reshape(B, S, 2 * H).transpose(0, 2, 1).reshape(B, 2 * H, S // L, L)
    cw = conv_w.reshape(ML_CONV, 2 * W)
    bias = jnp.concatenate([i_bias, f_bias])
    bn = jnp.pad(bias, (0, LANES - 2 * H)).reshape(1, LANES)
    bt = bias.reshape(2 * H, 1)
    ng = norm_g.reshape(1, W)
    tri = jnp.stack([jnp.tril(jnp.ones((L, L), F32)), jnp.triu(jnp.ones((L, L), F32))])
    seq = lambda a: a.reshape(B, S, a.shape[1])
    rows = lambda w: pl.BlockSpec((nb, S, w), lambda b: (b, 0, 0))
    full = lambda a: pl.BlockSpec(a.shape, lambda b: (0,) * a.ndim)
    y = pl.pallas_call(
        _mlstm_kernel, grid=(B // nb,),
        in_specs=[rows(W), rows(W), rows(W), rows(W), rows(LANES),
                  pl.BlockSpec((nb, 2 * H, S // L, L), lambda b: (b, 0, 0, 0)),
                  full(cw), full(bn), full(bt), full(ng), full(tri)],
        out_specs=rows(W),
        out_shape=jax.ShapeDtypeStruct((B, S, W), BF16),
        scratch_shapes=[pltpu.VMEM((nb * H, dh, dh), F32), pltpu.VMEM((nb * H, 1, dh), F32)],
        compiler_params=_params(("parallel",)), name="mlstm",
    )(seq(mq), seq(mk), seq(mv), seq(mo), seq(mif), gt, cw, bn, bt, ng, tri)
    return y.reshape(T, W)


def _outproj_kernel(yn_ref, ym_ref, x_ref, w_ref, g_ref, b_ref, o_ref):
    mix = (jnp.dot(yn_ref[...], w_ref[0:512, :], preferred_element_type=F32) +
           jnp.dot(ym_ref[...], w_ref[512:1024, :], preferred_element_type=F32))
    o_ref[...] = _ln_rows(ALPHA * x_ref[...] + mix, g_ref[...], b_ref[...])


def _outproj(y_nsa, y_ml, x2d, w_out, g, b, tm):
    T = x2d.shape[0]
    wn = w_out[:512].reshape(NSA_GROUPS, NSA_HPG, NSA_DK, D_MODEL).transpose(1, 0, 2, 3).reshape(512, D_MODEL)
    w = jnp.concatenate([wn, w_out[512:]], axis=0).astype(BF16)
    row = lambda wd: pl.BlockSpec((tm, wd), lambda i: (i, 0))
    full = lambda a: pl.BlockSpec(a.shape, lambda i: (0,) * a.ndim)
    g2, b2 = g.reshape(1, -1), b.reshape(1, -1)
    return pl.pallas_call(
        _outproj_kernel, grid=(T // tm,),
        in_specs=[row(512), row(512), row(D_MODEL), full(w), full(g2), full(b2)],
        out_specs=row(D_MODEL), out_shape=jax.ShapeDtypeStruct((T, D_MODEL), F32),
        compiler_params=_params(("parallel",)), name="outproj")(y_nsa, y_ml, x2d, w, g2, b2)


def _memkv_kernel(m_ref, w_ref, o_ref):
    o_ref[...] = jnp.dot(m_ref[...].astype(BF16), w_ref[...], preferred_element_type=F32).astype(o_ref.dtype)


def _memkv(mem2d, wk, wv):
    w = jnp.concatenate([wk, wv], axis=1).astype(BF16)
    R = mem2d.shape[0]
    return pl.pallas_call(
        _memkv_kernel, grid=(R // MEM_LEN,),
        in_specs=[pl.BlockSpec((MEM_LEN, D_MODEL), lambda i: (i, 0)),
                  pl.BlockSpec(w.shape, lambda i: (0, 0))],
        out_specs=pl.BlockSpec((MEM_LEN, 2 * D_MODEL), lambda i: (i, 0)),
        out_shape=jax.ShapeDtypeStruct((R, 2 * D_MODEL), BF16),
        compiler_params=_params(("parallel",)), name="memkv")(mem2d, w)


def _xattn_kernel(x_ref, kv_ref, wq_ref, wo_ref, g_ref, b_ref, rw_ref, x2_ref, x2p_ref, sc_ref):
    x1 = x_ref[...]
    q = jnp.dot(x1.astype(BF16), wq_ref[...], preferred_element_type=F32).astype(BF16)
    outs = []
    for h in range(XA_HEADS):
        qh = q[:, h * XA_DH:(h + 1) * XA_DH]
        kh = kv_ref[:, h * XA_DH:(h + 1) * XA_DH]
        vh = kv_ref[:, D_MODEL + h * XA_DH:D_MODEL + (h + 1) * XA_DH]
        s = lax.dot_general(qh, kh, _DN_T, preferred_element_type=F32) * (XA_DH ** -0.5)
        p = jnp.exp(s - jnp.max(s, axis=-1, keepdims=True))
        p = p / jnp.sum(p, axis=-1, keepdims=True)
        outs.append(jnp.dot(p.astype(BF16), vh, preferred_element_type=F32).astype(BF16))
    o = jnp.concatenate(outs, axis=1)
    xa = jnp.dot(o, wo_ref[...], preferred_element_type=F32)
    x2 = _ln_rows(ALPHA * x1 + xa, g_ref[...], b_ref[...])
    x2_ref[...] = x2
    x2p_ref[...] = _pack_bf16_pairs(x2)
    xh = x2.astype(BF16)
    xl = (x2 - xh.astype(F32)).astype(BF16)
    wh = rw_ref[0]
    wl = rw_ref[1]
    logit = (lax.dot_general(wh, xh, _DN_T, preferred_element_type=F32) +
             lax.dot_general(wh, xl, _DN_T, preferred_element_type=F32) +
             lax.dot_general(wl, xh, _DN_T, preferred_element_type=F32))
    sc_ref[...] = jax.nn.sigmoid(logit)


def _xattn(x1, kv, wq, wo, g, b, router_w, S, tq):
    T = x1.shape[0]
    wqb, wob = wq.astype(BF16), wo.astype(BF16)
    rwt = router_w.T
    rh = rwt.astype(BF16)
    rw = jnp.stack([rh, (rwt - rh.astype(F32)).astype(BF16)])
    g2, b2 = g.reshape(1, -1), b.reshape(1, -1)
    full = lambda a: pl.BlockSpec(a.shape, lambda i: (0,) * a.ndim)
    per = S // tq
    return pl.pallas_call(
        _xattn_kernel, grid=(T // tq,),
        in_specs=[pl.BlockSpec((tq, D_MODEL), lambda i: (i, 0)),
                  pl.BlockSpec((MEM_LEN, 2 * D_MODEL), lambda i: (i // per, 0)),
                  full(wqb), full(wob), full(g2), full(b2), full(rw)],
        out_specs=(pl.BlockSpec((tq, D_MODEL), lambda i: (i, 0)),
                   pl.BlockSpec((tq, D_MODEL // 2), lambda i: (i, 0)),
                   pl.BlockSpec((MOE_E, tq), lambda i: (0, i))),
        out_shape=(jax.ShapeDtypeStruct((T, D_MODEL), F32), jax.ShapeDtypeStruct((T, D_MODEL // 2), I32),
                   jax.ShapeDtypeStruct((MOE_E, T), F32)),
        compiler_params=_params(("parallel",)), name="xattn")(x1, kv, wqb, wob, g2, b2, rw)


def _route_kernel(sc_ref, rb_ref, idx_ref, w_ref):
    E, G = MOE_E, MOE_GROUPS
    per = E // G
    scores = sc_ref[...]
    tr = scores.shape[1]
    biased = scores + rb_ref[...]
    g3 = biased.reshape(G, per, tr)
    j3 = lax.broadcasted_iota(I32, (G, per, tr), 1)
    m1 = jnp.max(g3, axis=1, keepdims=True)
    first = jnp.min(jnp.where(g3 == m1, j3, per), axis=1, keepdims=True)
    m2 = jnp.max(jnp.where(j3 == first, -jnp.inf, g3), axis=1, keepdims=True)
    gs = (m1 + m2).reshape(G, tr)
    gi = lax.broadcasted_iota(I32, (G, tr), 0)
    cnt = jnp.zeros((G, tr), F32)
    for m in range(G):
        row = gs[m:m + 1, :]
        cnt = cnt + jnp.where((row > gs) | ((row == gs) & (gi > m)), 1.0, 0.0)
    gmask = cnt < float(MOE_TOPK_GROUPS)
    masked = jnp.where(gmask[:, None, :], g3, NEG).reshape(E, tr)
    ei = lax.broadcasted_iota(I32, (E, tr), 0)
    idxs, ws = [], []
    for _ in range(MOE_K):
        mx = jnp.max(masked, axis=0, keepdims=True)
        ix = jnp.min(jnp.where(masked == mx, ei, E), axis=0, keepdims=True)
        hit = ei == ix
        ws.append(jnp.sum(jnp.where(hit, scores, 0.0), axis=0, keepdims=True))
        idxs.append(ix)
        masked = jnp.where(hit, -jnp.inf, masked)
    w = jnp.concatenate(ws, axis=0)
    idx_ref[...] = jnp.concatenate(idxs, axis=0)
    w_ref[...] = w / jnp.sum(w, axis=0, keepdims=True) * MOE_ROUTE_SCALE


def _route(scores_t, router_bias, tr):
    E, T = scores_t.shape
    rb = router_bias.reshape(E, 1)
    return pl.pallas_call(
        _route_kernel, grid=(T // tr,),
        in_specs=[pl.BlockSpec((E, tr), lambda i: (0, i)), pl.BlockSpec((E, 1), lambda i: (0, 0))],
        out_specs=(pl.BlockSpec((MOE_K, tr), lambda i: (0, i)), pl.BlockSpec((MOE_K, tr), lambda i: (0, i))),
        out_shape=(jax.ShapeDtypeStruct((MOE_K, T), I32), jax.ShapeDtypeStruct((MOE_K, T), F32)),
        compiler_params=_params(("parallel",)), name="route")(scores_t, rb)


def _rank_kernel(idx_ref, u_ref, rank_ref, cnt_ref, carry):
    E = MOE_E

    @pl.when(pl.program_id(0) == 0)
    def _():
        carry[...] = jnp.zeros_like(carry)

    idx = idx_ref[...]
    tp = idx.shape[1]
    ei = lax.broadcasted_iota(I32, (E, tp), 0)
    hits = [ei == idx[k:k + 1, :] for k in range(MOE_K)]
    onehot = jnp.zeros((E, tp), F32)
    for hit in hits:
        onehot = onehot + jnp.where(hit, 1.0, 0.0)
    pos = jnp.dot(onehot.astype(BF16), u_ref[...], preferred_element_type=F32) + carry[...]
    ranks = [jnp.sum(jnp.where(hit, pos, 0.0), axis=0, keepdims=True) for hit in hits]
    rank_ref[...] = jnp.concatenate(ranks, axis=0).astype(I32)
    total = carry[...] + jnp.sum(onehot, axis=1, keepdims=True)
    carry[...] = total
    cnt_ref[...] = jnp.broadcast_to(total, cnt_ref.shape).astype(I32)


def _rank(idx_t, tp):
    K, T = idx_t.shape
    u = jnp.triu(jnp.ones((tp, tp), F32), k=1).astype(BF16)
    rank, cnt = pl.pallas_call(
        _rank_kernel, grid=(T // tp,),
        in_specs=[pl.BlockSpec((K, tp), lambda i: (0, i)), pl.BlockSpec((tp, tp), lambda i: (0, 0))],
        out_specs=(pl.BlockSpec((K, tp), lambda i: (0, i)), pl.BlockSpec((MOE_E, LANES), lambda i: (0, 0))),
        out_shape=(jax.ShapeDtypeStruct((K, T), I32), jax.ShapeDtypeStruct((MOE_E, LANES), I32)),
        scratch_shapes=[pltpu.VMEM((MOE_E, 1), F32)],
        compiler_params=_params(("arbitrary",)), name="rank")(idx_t, u)
    return rank, cnt[:, 0]


def _dest_kernel(idx_ref, rank_ref, po_ref, dest_ref):
    idx = idx_ref[...]
    tp = idx.shape[1]
    ei = lax.broadcasted_iota(I32, (MOE_E, tp), 0)
    po = po_ref[...]
    base = [jnp.sum(jnp.where(ei == idx[k:k + 1, :], po, 0.0), axis=0, keepdims=True) for k in range(MOE_K)]
    dest_ref[...] = jnp.concatenate(base, axis=0).astype(I32) + rank_ref[...]


def _dest(idx_t, rank_t, poffs, tp):
    K, T = idx_t.shape
    po = poffs.astype(F32).reshape(MOE_E, 1)
    spec = pl.BlockSpec((K, tp), lambda i: (0, i))
    return pl.pallas_call(
        _dest_kernel, grid=(T // tp,),
        in_specs=[spec, spec, pl.BlockSpec((MOE_E, 1), lambda i: (0, 0))],
        out_specs=spec, out_shape=jax.ShapeDtypeStruct((K, T), I32),
        compiler_params=_params(("parallel",)), name="dest")(idx_t, rank_t, po)


def _pack_bf16_pairs(v):
    m = v.shape[1] // 2
    bits = lax.bitcast_convert_type(v.astype(BF16).astype(F32), jnp.uint32)
    return lax.bitcast_convert_type((bits[:, :m] >> 16) | (bits[:, m:] & jnp.uint32(0xFFFF0000)), I32)


def _unpack_bf16_pairs(w):
    w = lax.bitcast_convert_type(w, jnp.uint32)
    lo = lax.bitcast_convert_type(w << 16, F32)
    hi = lax.bitcast_convert_type(w & jnp.uint32(0xFFFF0000), F32)
    return lo, hi


def _row_scatter(rows, dest_t, n_rows):
    T, d = rows.shape
    K = dest_t.shape[0]
    nw = SC_CORES * SC_SUBCORES
    per_w = T // nw
    ch = SC_GATHER_ROWS
    mesh = plsc.VectorSubcoreMesh(core_axis_name="c", subcore_axis_name="s")

    @functools.partial(
        pl.kernel, mesh=mesh, out_type=jax.ShapeDtypeStruct((n_rows, d), I32),
        scratch_types=[pltpu.VMEM((ch,), I32), pltpu.VMEM((ch, d), I32), pltpu.SemaphoreType.DMA],
        name="row_scatter")
    def scatter(rows_hbm, dest_hbm, out_hbm, idx_v, rows_v, sem):
        wid = lax.axis_index("s") * SC_CORES + lax.axis_index("c")
        base = wid * per_w

        @pl.loop(0, per_w // ch)
        def _(i):
            off = pl.multiple_of(base + i * ch, ch)
            pltpu.sync_copy(rows_hbm.at[pl.ds(off, ch)], rows_v)
            for k in range(K):
                pltpu.sync_copy(dest_hbm.at[k, pl.ds(off, ch)], idx_v)
                pltpu.async_copy(rows_v, out_hbm.at[idx_v], sem).wait()

    return scatter(rows, dest_t)


def _expert_kernel(po_ref, cnt_ref, xs_hbm, w1_ref, w3_ref, w2_ref, ys_hbm, xbuf, ybuf, w1b, w3b, w2b, insem, outsem):
    e = pl.program_id(0)
    n = cnt_ref[e]
    base = po_ref[e]
    rb = xbuf.shape[1]
    hw = D_MODEL // 2
    nblk = (n + rb - 1) // rb
    w1b[...] = w1_ref[...].astype(BF16)
    w3b[...] = w3_ref[...].astype(BF16)
    w2b[...] = w2_ref[...].astype(BF16)

    def rows_of(j):
        return pl.ds(pl.multiple_of(base + j * rb, rb), rb)

    def in_copy(j, slot):
        return pltpu.make_async_copy(xs_hbm.at[rows_of(j)], xbuf.at[slot], insem.at[slot])

    def out_copy(j, slot):
        return pltpu.make_async_copy(ybuf.at[slot], ys_hbm.at[rows_of(j)], outsem.at[slot])

    @pl.when(nblk > 0)
    def _():
        in_copy(0, 0).start()

    def body(j, c):
        slot = j % 2

        @pl.when(j + 1 < nblk)
        def _():
            in_copy(j + 1, 1 - slot).start()

        in_copy(j, slot).wait()

        @pl.when(j >= 2)
        def _():
            out_copy(j - 2, slot).wait()

        words = xbuf[slot]
        row = j * rb + lax.broadcasted_iota(I32, words.shape, 0)
        lo, hi = _unpack_bf16_pairs(jnp.where(row < n, words, 0))
        lo, hi = lo.astype(BF16), hi.astype(BF16)
        a = (jnp.dot(lo, w1b[0:hw, :], preferred_element_type=F32) +
             jnp.dot(hi, w1b[hw:, :], preferred_element_type=F32))
        g = (jnp.dot(lo, w3b[0:hw, :], preferred_element_type=F32) +
             jnp.dot(hi, w3b[hw:, :], preferred_element_type=F32))
        h = (a * jax.nn.sigmoid(a) * g).astype(BF16)
        ybuf[slot] = _pack_bf16_pairs(jnp.dot(h, w2b[...], preferred_element_type=F32))
        out_copy(j, slot).start()
        return c

    lax.fori_loop(0, nblk, body, 0)

    @pl.when(nblk >= 2)
    def _():
        out_copy(nblk - 2, nblk % 2).wait()

    @pl.when(nblk >= 1)
    def _():
        out_copy(nblk - 1, (nblk - 1) % 2).wait()


def _experts(poffs, counts, xs, w1, w3, w2):
    n_rows, wp = xs.shape
    rb = ROW_BLOCK
    wspec = lambda shape: pl.BlockSpec((None,) + shape, lambda e, po, cn: (e, 0, 0))
    gs = pltpu.PrefetchScalarGridSpec(
        num_scalar_prefetch=2, grid=(MOE_E,),
        in_specs=[pl.BlockSpec(memory_space=pl.ANY), wspec((D_MODEL, MOE_FF)), wspec((D_MODEL, MOE_FF)),
                  wspec((MOE_FF, D_MODEL))],
        out_specs=pl.BlockSpec(memory_space=pl.ANY),
        scratch_shapes=[pltpu.VMEM((2, rb, wp), I32), pltpu.VMEM((2, rb, wp), I32),
                        pltpu.VMEM((D_MODEL, MOE_FF), BF16), pltpu.VMEM((D_MODEL, MOE_FF), BF16),
                        pltpu.VMEM((MOE_FF, D_MODEL), BF16),
                        pltpu.SemaphoreType.DMA((2,)), pltpu.SemaphoreType.DMA((2,))])
    return pl.pallas_call(
        _expert_kernel, grid_spec=gs, out_shape=jax.ShapeDtypeStruct((n_rows, wp), I32),
        compiler_params=_params(("arbitrary",)), name="experts")(poffs, counts, xs, w1, w3, w2)


def _row_gather(table, idx):
    n, d = idx.shape[0], table.shape[1]
    nw = SC_CORES * SC_SUBCORES
    per_w = n // nw
    ch = SC_GATHER_ROWS
    mesh = plsc.VectorSubcoreMesh(core_axis_name="c", subcore_axis_name="s")

    @functools.partial(
        pl.kernel, mesh=mesh, out_type=jax.ShapeDtypeStruct((n, d), I32),
        scratch_types=[pltpu.VMEM((ch,), I32), pltpu.VMEM((ch, d), I32), pltpu.SemaphoreType.DMA],
        name="row_gather")
    def gather(table_hbm, idx_hbm, out_hbm, idx_v, rows_v, sem):
        wid = lax.axis_index("s") * SC_CORES + lax.axis_index("c")
        base = wid * per_w

        @pl.loop(0, per_w // ch)
        def _(i):
            off = pl.multiple_of(base + i * ch, ch)
            pltpu.sync_copy(idx_hbm.at[pl.ds(off, ch)], idx_v)
            pltpu.async_copy(table_hbm.at[idx_v], rows_v, sem).wait()
            pltpu.sync_copy(rows_v, out_hbm.at[pl.ds(off, ch)])

    return gather(table, idx)


def _combine_kernel(w_ref, tile_ref, x_ref, ysg_ref, s1_ref, s3_ref, s2_ref, g_ref, b_ref, o_ref):
    x2 = x_ref[...]
    xb = x2.astype(BF16)
    a = jnp.dot(xb, s1_ref[...], preferred_element_type=F32)
    c = jnp.dot(xb, s3_ref[...], preferred_element_type=F32)
    shared = jnp.dot((a * jax.nn.sigmoid(a) * c).astype(BF16), s2_ref[...], preferred_element_type=F32)
    lo, hi = _unpack_bf16_pairs(ysg_ref[...])
    lo, hi = lo.astype(BF16), hi.astype(BF16)
    wrep = jnp.dot(w_ref[...], tile_ref[...], precision=lax.Precision.HIGHEST, preferred_element_type=F32)
    col = lax.broadcasted_iota(I32, wrep.shape, 1)
    row = lax.broadcasted_iota(I32, wrep.shape, 0)
    wsel = jnp.where(col // MOE_K == row, wrep, 0.0)
    wh = wsel.astype(BF16)
    wl = (wsel - wh.astype(F32)).astype(BF16)
    y_lo = jnp.dot(wh, lo, preferred_element_type=F32) + jnp.dot(wl, lo, preferred_element_type=F32)
    y_hi = jnp.dot(wh, hi, preferred_element_type=F32) + jnp.dot(wl, hi, preferred_element_type=F32)
    y = shared + jnp.concatenate([y_lo, y_hi], axis=1)
    o_ref[...] = _ln_rows(ALPHA * x2 + y, g_ref[...], b_ref[...])


def _combine(w_nat, x2, ysg, sw1, sw3, sw2, g, b, tc):
    T = x2.shape[0]
    wp = ysg.shape[1]
    full = lambda a: pl.BlockSpec(a.shape, lambda i: (0,) * a.ndim)
    s1, s3, s2 = sw1.astype(BF16), sw3.astype(BF16), sw2.astype(BF16)
    g2, b2 = g.reshape(1, -1), b.reshape(1, -1)
    tile = jnp.asarray(np.arange(tc * MOE_K)[None, :] % MOE_K == np.arange(MOE_K)[:, None], F32)
    return pl.pallas_call(
        _combine_kernel, grid=(T // tc,),
        in_specs=[pl.BlockSpec((tc, MOE_K), lambda i: (i, 0)), full(tile),
                  pl.BlockSpec((tc, D_MODEL), lambda i: (i, 0)),
                  pl.BlockSpec((tc * MOE_K, wp), lambda i: (i, 0)),
                  full(s1), full(s3), full(s2), full(g2), full(b2)],
        out_specs=pl.BlockSpec((tc, D_MODEL), lambda i: (i, 0)),
        out_shape=jax.ShapeDtypeStruct((T, D_MODEL), F32),
        compiler_params=_params(("parallel",)), name="combine",
    )(w_nat, tile, x2, ysg, s1, s3, s2, g2, b2)


def _moe(x2, x2p, scores_t, router_bias, w1, w3, w2, sw1, sw3, sw2, g, b, tiles):
    T = x2.shape[0]
    rb = ROW_BLOCK
    idx_t, w_t = _route(scores_t, router_bias, tiles['route'])
    rank_t, counts = _rank(idx_t, tiles['rank'])
    pcounts = (counts + rb - 1) // rb * rb
    pends = jnp.cumsum(pcounts)
    poffs = (pends - pcounts).astype(I32)
    n_blocks = (T * MOE_K + MOE_E * (rb - 1) + rb - 1) // rb
    dest_t = _dest(idx_t, rank_t, poffs, tiles['rank'])
    xs = _row_scatter(x2p, dest_t, n_blocks * rb)
    ys = _experts(poffs, counts.astype(I32), xs, w1, w3, w2)
    ysg = _row_gather(ys, dest_t.T.reshape(-1))
    return _combine(w_t.T, x2, ysg, sw1, sw3, sw2, g, b, tiles['combine'])


def _tiles(B, S):
    T = B * S
    pick = lambda want, n: want if n % want == 0 else n
    return dict(proj=pick(512, T), nsa_q=pick(128, S), nsa_ck=pick(512, S), outproj=pick(512, T),
                mlstm_nb=2 if B % 2 == 0 else 1, xattn=pick(256, S), route=pick(512, T), rank=pick(512, T), scatter=pick(256, T),
                combine=pick(128, T))


def kernel(x, mem, w_in, nsa_pos_k, nsa_cmp_k_w1, nsa_cmp_k_w2, nsa_pos_v, nsa_cmp_v_w1, nsa_cmp_v_w2,
           mlstm_conv_w, mlstm_i_bias, mlstm_f_bias, mlstm_norm_g, w_out, ln1_g, ln1_b,
           xa_wq, xa_wk, xa_wv, xa_wo, ln2_g, ln2_b, router_w, router_bias,
           moe_w1, moe_w3, moe_w2, shared_w1, shared_w3, shared_w2, ln3_g, ln3_b):
    B, S, D = x.shape
    T = B * S
    tl = _tiles(B, S)
    xc = x.reshape(T, D)
    memc = mem.reshape(B * MEM_LEN, D)
    for l in range(w_in.shape[0]):
        (q, cmp, ksel, vsel, kwin, vwin, gates, mq, mk, mv, mo, mif) = _project(xc, _prep_w_in(w_in[l]), tl['proj'])
        kcvc = _compress(cmp, B, S, _prep_cmp(nsa_pos_k[l], nsa_cmp_k_w1[l], nsa_cmp_k_w2[l],
                                              nsa_pos_v[l], nsa_cmp_v_w1[l], nsa_cmp_v_w2[l]))
        y_nsa = _nsa(q, kcvc, ksel, vsel, kwin, vwin, gates, B, S, tl['nsa_q'], tl['nsa_ck'])
        y_ml = _mlstm(mq, mk, mv, mo, mif, mlstm_conv_w[l], mlstm_i_bias[l], mlstm_f_bias[l],
                      mlstm_norm_g[l], B, S, tl['mlstm_nb'])
        x1 = _outproj(y_nsa, y_ml, xc, w_out[l], ln1_g[l], ln1_b[l], tl['outproj'])
        kv = _memkv(memc, xa_wk[l], xa_wv[l])
        x2, x2p, scores_t = _xattn(x1, kv, xa_wq[l], xa_wo[l], ln2_g[l], ln2_b[l], router_w[l], S, tl['xattn'])
        xc = _moe(x2, x2p, scores_t, router_bias[l], moe_w1[l], moe_w3[l], moe_w2[l],
                  shared_w1[l], shared_w3[l], shared_w2[l], ln3_g[l], ln3_b[l], tl)
    return xc.reshape(B, S, D)
```

```python
import functools
import numpy as np
import jax
import jax.numpy as jnp
from jax import lax
from jax.experimental import pallas as pl
from jax.experimental.pallas import tpu as pltpu
from jax.experimental.pallas import tpu_sc as plsc

F32 = jnp.float32
BF16 = jnp.bfloat16
I32 = jnp.int32

D_MODEL = 1024
MEM_LEN = 256
NSA_HEADS = 8
NSA_GROUPS = 2
NSA_HPG = 4
NSA_DK = 64
NSA_CMP_LEN = 32
NSA_CMP_STRIDE = 16
NSA_SEL_BLOCK = 64
NSA_SEL_TOPN = 8
NSA_WINDOW = 512
ML_HEADS = 4
ML_DH = 128
ML_CHUNK = 64
ML_CONV = 4
XA_HEADS = 4
XA_DH = 256
MOE_E = 256
MOE_K = 8
MOE_GROUPS = 8
MOE_TOPK_GROUPS = 4
MOE_FF = 256
MOE_ROUTE_SCALE = 2.5
DEPTH = 1
ALPHA = (2.0 * DEPTH) ** 0.25
LN_EPS = 1e-5
NEG = -1e30
FORCE_BONUS = 1e4

LANES = 128
ROW_BLOCK = 256
VMEM_LIMIT = 56 * 1024 * 1024
SC_CORES = 2
SC_SUBCORES = 16
SC_GATHER_ROWS = 128

_DN_T = (((1,), (1,)), ((), ()))
_DN_TA = (((0,), (0,)), ((), ()))


def _params(sem):
    return pltpu.CompilerParams(dimension_semantics=sem, vmem_limit_bytes=VMEM_LIMIT)


def _ln_rows(v, g, b):
    mu = jnp.mean(v, axis=-1, keepdims=True)
    d = v - mu
    var = jnp.mean(d * d, axis=-1, keepdims=True)
    return d * lax.rsqrt(var + LN_EPS) * g + b


_SEGS = (('q', 512, BF16), ('cmp', 256, F32), ('ksel', 128, BF16), ('vsel', 128, BF16),
         ('kwin', 128, BF16), ('vwin', 128, BF16), ('gates', 128, F32), ('mq', 512, BF16),
         ('mk', 512, BF16), ('mv', 512, BF16), ('mo', 512, BF16), ('mif', 128, F32))


def _proj_kernel(x_ref, w_ref, *out_refs):
    xb = x_ref[...].astype(BF16)
    off = 0
    for o_ref, (_, wd, _) in zip(out_refs, _SEGS):
        o_ref[...] = jnp.dot(xb, w_ref[:, off:off + wd], preferred_element_type=F32).astype(o_ref.dtype)
        off += wd


def _prep_w_in(w):
    sizes = (512,) + (128,) * 6 + (24,) + (512,) * 4 + (4, 4)
    cuts = np.cumsum(sizes)[:-1].tolist()
    (wq, kc, vc, ks, vs, kw, vw, wg, mq, mk, mv, mo, mi, mf) = jnp.split(w, cuts, axis=1)
    wq = wq.reshape(D_MODEL, NSA_GROUPS, NSA_HPG, NSA_DK).transpose(0, 2, 1, 3).reshape(D_MODEL, 512)
    pad = lambda a: jnp.pad(a, ((0, 0), (0, LANES - a.shape[1])))
    segs = [wq, kc, vc, ks, vs, kw, vw, pad(wg), mq, mk, mv, mo, pad(jnp.concatenate([mi, mf], axis=1))]
    return jnp.concatenate(segs, axis=1).astype(BF16)


def _project(x2d, w_all, tm):
    T = x2d.shape[0]
    n = w_all.shape[1]
    out_shape = tuple(jax.ShapeDtypeStruct((T, wd), dt) for _, wd, dt in _SEGS)
    out_specs = tuple(pl.BlockSpec((tm, wd), lambda i: (i, 0)) for _, wd, _ in _SEGS)
    return pl.pallas_call(
        _proj_kernel, grid=(T // tm,),
        in_specs=[pl.BlockSpec((tm, D_MODEL), lambda i: (i, 0)),
                  pl.BlockSpec((D_MODEL, n), lambda i: (0, 0))],
        out_specs=out_specs, out_shape=out_shape,
        compiler_params=_params(("parallel",)), name="proj")(x2d, w_all)


def _cmp_kernel(r_ref, pa_ref, pb_ref, wa_ref, wb_ref, w2_ref, o_ref):
    r = r_ref[...]
    a = jnp.dot((r + pa_ref[...]).astype(BF16), wa_ref[...], preferred_element_type=F32)
    b = jnp.dot((r + pb_ref[...]).astype(BF16), wb_ref[...], preferred_element_type=F32)
    nr = r.shape[0]
    hid = a + pltpu.roll(b, nr - 1, 0)
    hid = hid * jax.nn.sigmoid(hid)
    out = jnp.dot(hid.astype(BF16), w2_ref[...], preferred_element_type=F32)
    row = lax.broadcasted_iota(I32, out.shape, 0)
    o_ref[...] = jnp.where(row < nr - 1, out, 0.0).astype(o_ref.dtype)


def _prep_cmp(pos_k, w1_k, w2_k, pos_v, w1_v, w2_v):
    eye = jnp.eye(NSA_GROUPS, dtype=F32)

    def expand_w1(w1, half):
        w = w1.reshape(NSA_CMP_LEN, NSA_DK, NSA_DK)[half * 16:(half + 1) * 16]
        return jnp.einsum('jde,gk->jgdke', w, eye).reshape(16, 128, 128)

    def both(fk, fv):
        z = jnp.zeros_like(fk)
        top = jnp.concatenate([fk, z], axis=-1)
        bot = jnp.concatenate([z, fv], axis=-1)
        return jnp.concatenate([top, bot], axis=-2)

    wa = both(expand_w1(w1_k, 0), expand_w1(w1_v, 0)).reshape(16 * 256, 256).astype(BF16)
    wb = both(expand_w1(w1_k, 1), expand_w1(w1_v, 1)).reshape(16 * 256, 256).astype(BF16)
    w2 = both(jnp.kron(eye, w2_k), jnp.kron(eye, w2_v)).astype(BF16)

    def pos_row(half):
        pk = jnp.tile(pos_k[half * 16:(half + 1) * 16], (1, NSA_GROUPS))
        pv = jnp.tile(pos_v[half * 16:(half + 1) * 16], (1, NSA_GROUPS))
        return jnp.concatenate([pk, pv], axis=1).reshape(1, 16 * 256)

    return pos_row(0), pos_row(1), wa, wb, w2


def _compress(cmp2d, B, S, prep):
    pa, pb, wa, wb, w2 = prep
    nr = S // NSA_CMP_STRIDE
    r = cmp2d.reshape(B, nr, NSA_CMP_STRIDE * 256)
    full = lambda a: pl.BlockSpec(a.shape, lambda b: (0,) * a.ndim)
    return pl.pallas_call(
        _cmp_kernel, grid=(B,),
        in_specs=[pl.BlockSpec((None, nr, NSA_CMP_STRIDE * 256), lambda b: (b, 0, 0)),
                  full(pa), full(pb), full(wa), full(wb), full(w2)],
        out_specs=pl.BlockSpec((None, nr, 256), lambda b: (b, 0, 0)),
        out_shape=jax.ShapeDtypeStruct((B, nr, 256), BF16),
        compiler_params=_params(("parallel",)), name="cmp")(r, pa, pb, wa, wb, w2)


def _nsa_consts(S):
    n_cmp = (S - NSA_CMP_LEN) // NSA_CMP_STRIDE + 1
    n_sel = S // NSA_SEL_BLOCK
    cs = np.arange(n_cmp) * NSA_CMP_STRIDE
    ss = np.arange(n_sel) * NSA_SEL_BLOCK
    ov = ((cs[:, None] < ss[None, :] + NSA_SEL_BLOCK) & (cs[:, None] + NSA_CMP_LEN > ss[None, :]))
    ovt = np.zeros((LANES, S // NSA_CMP_STRIDE), np.float32)
    ovt[:n_sel, :n_cmp] = ov.T
    e = np.zeros((LANES, S), np.float32)
    e[np.arange(S) // NSA_SEL_BLOCK, np.arange(S)] = 1.0
    return jnp.asarray(ovt, BF16), jnp.asarray(e, BF16)


def _nsa_kernel(q_ref, kcvc_ref, ksel_ref, vsel_ref, kwin_ref, vwin_ref, gates_ref, ovt_ref, e_ref,
                y_ref, bias_ref, *, tq, ck, n_sel):
    G, H = NSA_GROUPS, NSA_HPG
    GH = G * H
    M = GH * tq
    S = ksel_ref.shape[0]
    W = NSA_WINDOW
    ws = min(W + tq, S)
    t0 = pl.program_id(1) * tq
    gates = jax.nn.sigmoid(gates_ref[...])
    lane = lax.broadcasted_iota(I32, (tq, LANES), 1)
    t_col = t0 + lax.broadcasted_iota(I32, (tq, 1), 0)
    kc = kcvc_ref[:, 0:LANES]
    vc = kcvc_ref[:, LANES:2 * LANES]
    nc = kc.shape[0]
    qs = []
    for g in range(G):
        gmask = (lane // NSA_DK) == g
        for h in range(H):
            qh = q_ref[:, h * LANES:(h + 1) * LANES] * (NSA_DK ** -0.5)
            qs.append(jnp.where(gmask, qh, jnp.zeros_like(qh)))
    Q = jnp.concatenate(qs, axis=0).astype(BF16)

    s = lax.dot_general(Q, kc, _DN_T, preferred_element_type=F32)
    c_idx = lax.broadcasted_iota(I32, (tq, nc), 1)
    cmask = (c_idx * NSA_CMP_STRIDE + NSA_CMP_LEN - 1) <= t_col
    s3 = jnp.where(cmask[None], s.reshape(GH, tq, nc), NEG)
    p = jnp.exp(s3 - jnp.max(s3, axis=-1, keepdims=True))
    p = p / jnp.sum(p, axis=-1, keepdims=True)
    p = jnp.where(cmask[None], p, 0.0)
    o_cmp = jnp.dot(p.reshape(M, nc).astype(BF16), vc, preferred_element_type=F32).reshape(GH, tq, LANES)

    for g in range(G):
        psum = jnp.sum(p[g * H:(g + 1) * H], axis=0)
        hi = psum.astype(BF16)
        lo = (psum - hi.astype(F32)).astype(BF16)
        ovt = ovt_ref[...]
        pslt = (lax.dot_general(ovt, hi, _DN_T, preferred_element_type=F32) +
                lax.dot_general(ovt, lo, _DN_T, preferred_element_type=F32))
        imp_p = pslt[0:n_sel, :]
        n_i = lax.broadcasted_iota(I32, (n_sel, tq), 0)
        cur = (t0 + lax.broadcasted_iota(I32, (n_sel, tq), 1)) // NSA_SEL_BLOCK
        forced = (n_i == 0) | (n_i == cur) | (n_i == cur - 1)
        imp = jnp.where(n_i <= cur, imp_p + jnp.where(forced, FORCE_BONUS, 0.0), NEG)
        cnt = jnp.zeros((n_sel, tq), F32)
        for m in range(n_sel):
            row = imp[m:m + 1, :]
            beats = (row > imp) | ((row == imp) & (n_i > m))
            cnt = cnt + jnp.where(beats, 1.0, 0.0)
        selt = jnp.where(cnt < float(min(NSA_SEL_TOPN, n_sel)), 1.0, 0.0)
        selt = jnp.concatenate([selt, jnp.zeros((LANES - n_sel, tq), F32)], axis=0)
        sel = selt.T.astype(BF16)
        maskf = jnp.dot(sel, e_ref[...], preferred_element_type=F32)
        kpos = lax.broadcasted_iota(I32, (tq, S), 1)
        bias_ref[g] = jnp.where((maskf > 0.5) & (kpos <= t_col), 0.0, NEG)

    vlane = lax.broadcasted_iota(I32, (1, LANES), 1) // NSA_DK

    def pv_with_sums(pb, v):
        outs = []
        for g in range(G):
            vg = jnp.where(vlane == g, v, jnp.ones_like(v))
            outs.append(jnp.dot(pb[g * H * tq:(g + 1) * H * tq], vg, preferred_element_type=F32))
        return jnp.concatenate(outs, axis=0)

    def normalise(acc):
        outs = []
        for r in range(GH):
            c = NSA_DK * (1 - r // H)
            outs.append(acc[r] / acc[r][:, c:c + 1])
        return outs

    def sel_body(j, carry):
        m_i, acc = carry
        ks = pl.multiple_of(j * ck, ck)
        k = ksel_ref[pl.ds(ks, ck), :]
        v = vsel_ref[pl.ds(ks, ck), :]
        sj = lax.dot_general(Q, k, _DN_T, preferred_element_type=F32)
        sj = (sj.reshape(G, H, tq, ck) + bias_ref[:, :, pl.ds(ks, ck)][:, None]).reshape(GH, tq, ck)
        m_new = jnp.maximum(m_i, jnp.max(sj, axis=-1, keepdims=True))
        a = jnp.exp(m_i - m_new)
        pj = jnp.exp(sj - m_new).reshape(M, ck).astype(BF16)
        return m_new, a * acc + pv_with_sums(pj, v).reshape(GH, tq, LANES)

    init = (jnp.full((GH, tq, 1), NEG, F32), jnp.zeros((GH, tq, LANES), F32))
    _, acc = lax.fori_loop(0, (t0 + tq + ck - 1) // ck, sel_body, init)
    o_sel = normalise(acc)

    kst = pl.multiple_of(jnp.clip(t0 - W, 0, S - ws), LANES)
    kwn = kwin_ref[pl.ds(kst, ws), :]
    vwn = vwin_ref[pl.ds(kst, ws), :]
    sw = lax.dot_general(Q, kwn, _DN_T, preferred_element_type=F32)
    wpos = kst + lax.broadcasted_iota(I32, (tq, ws), 1)
    wmask = (wpos <= t_col) & (wpos > t_col - W)
    sw3 = jnp.where(wmask[None], sw.reshape(GH, tq, ws), NEG)
    pw = jnp.exp(sw3 - jnp.max(sw3, axis=-1, keepdims=True)).reshape(M, ws).astype(BF16)
    o_win = normalise(pv_with_sums(pw, vwn).reshape(GH, tq, LANES))

    g0mask = lane < NSA_DK
    for h in range(H):
        o_g = []
        for g in range(G):
            r = g * H + h
            c0 = r * 3
            o_g.append(gates[:, c0:c0 + 1] * o_cmp[r] + gates[:, c0 + 1:c0 + 2] * o_sel[r] +
                       gates[:, c0 + 2:c0 + 3] * o_win[r])
        y_ref[:, h * LANES:(h + 1) * LANES] = jnp.where(g0mask, o_g[0], o_g[1]).astype(y_ref.dtype)


def _nsa(q, kcvc, ksel, vsel, kwin, vwin, gates, B, S, tq, ck):
    T = B * S
    nq = S // tq
    ovt, e = _nsa_consts(S)
    seq = lambda a: a.reshape(B, S, LANES)
    kv_spec = pl.BlockSpec((None, S, LANES), lambda b, i: (b, 0, 0))
    kern = functools.partial(_nsa_kernel, tq=tq, ck=ck, n_sel=S // NSA_SEL_BLOCK)
    return pl.pallas_call(
        kern, grid=(B, nq),
        in_specs=[pl.BlockSpec((tq, 512), lambda b, i: (b * nq + i, 0)),
                  pl.BlockSpec((None,) + kcvc.shape[1:], lambda b, i: (b, 0, 0)),
                  kv_spec, kv_spec, kv_spec, kv_spec,
                  pl.BlockSpec((tq, LANES), lambda b, i: (b * nq + i, 0)),
                  pl.BlockSpec(ovt.shape, lambda b, i: (0, 0)),
                  pl.BlockSpec(e.shape, lambda b, i: (0, 0))],
        out_specs=pl.BlockSpec((tq, 512), lambda b, i: (b * nq + i, 0)),
        out_shape=jax.ShapeDtypeStruct((T, 512), BF16),
        scratch_shapes=[pltpu.VMEM((NSA_GROUPS, tq, S), F32)],
        compiler_params=_params(("parallel", "parallel")), name="nsa",
    )(q, kcvc, seq(ksel), seq(vsel), seq(kwin), seq(vwin), gates, ovt, e)


def _mlstm_kernel(q_ref, k_ref, v_ref, o_ref, gn_ref, gt_ref, cw_ref, bn_ref, bt_ref, ng_ref, tri_ref,
                  y_ref, c_scr, n_scr):
    H, dh, L = ML_HEADS, ML_DH, ML_CHUNK
    nb, S = q_ref.shape[0], q_ref.shape[1]
    nchunk = S // L
    c_scr[...] = jnp.zeros_like(c_scr)
    n_scr[...] = jnp.zeros_like(n_scr)
    row = lax.broadcasted_iota(I32, (L, H * dh), 0)
    li = lax.broadcasted_iota(I32, (L, L), 0)
    mi = lax.broadcasted_iota(I32, (L, L), 1)
    causal = mi <= li
    tril = tri_ref[0]
    triu = tri_ref[1]
    hp = lax.Precision.HIGHEST

    def conv_silu(ref, bi, c, wofs):
        r0 = pl.multiple_of(c * L, L)
        rp = pl.multiple_of(jnp.maximum(c - 1, 0) * L, L)
        cur = ref[bi, pl.ds(r0, L), :].astype(F32)
        prev = ref[bi, pl.ds(rp, L), :].astype(F32) * jnp.where(c > 0, 1.0, 0.0)
        acc = cur * cw_ref[ML_CONV - 1:ML_CONV, wofs:wofs + H * dh]
        for j in range(1, ML_CONV):
            sh = jnp.where(row < j, pltpu.roll(prev, j, 0), pltpu.roll(cur, j, 0))
            acc = acc + sh * cw_ref[ML_CONV - 1 - j:ML_CONV - j, wofs:wofs + H * dh]
        return acc * jax.nn.sigmoid(acc)

    def body(c, m_state):
        r0 = pl.multiple_of(c * L, L)
        new_m = []
        for bi in range(nb):
            qa = conv_silu(q_ref, bi, c, 0) * (dh ** -0.5)
            ka = conv_silu(k_ref, bi, c, H * dh)
            va = v_ref[bi, pl.ds(r0, L), :]
            oa = o_ref[bi, pl.ds(r0, L), :].astype(F32)
            gn = gn_ref[bi, pl.ds(r0, L), :] + bn_ref[...]
            gt = gt_ref[bi, :, c, :] + bt_ref[...]
            lf_n = jax.nn.log_sigmoid(gn)
            lf_t = jax.nn.log_sigmoid(gt)
            b_n = jnp.dot(tril, lf_n, precision=hp, preferred_element_type=F32)
            b_t = jnp.dot(lf_t, triu, precision=hp, preferred_element_type=F32)
            for h in range(H):
                st = bi * H + h
                q = qa[:, h * dh:(h + 1) * dh]
                k = ka[:, h * dh:(h + 1) * dh]
                v = va[:, h * dh:(h + 1) * dh]
                m_old = m_state[st]
                b_col = b_n[:, H + h:H + h + 1]
                i_col = gn[:, h:h + 1]
                b_row = b_t[H + h:H + h + 1, :]
                i_row = gt[h:h + 1, :]
                g_tot = b_t[H + h:H + h + 1, L - 1:L]
                d_log = jnp.where(causal, b_col - b_row + i_row, NEG)
                inter = b_col + m_old
                m_q = jnp.maximum(inter, jnp.max(d_log, axis=-1, keepdims=True))
                w_intra = jnp.exp(d_log - m_q)
                w_inter = jnp.exp(inter - m_q)
                qb = q.astype(BF16)
                s = lax.dot_general(qb, k.astype(BF16), _DN_T, preferred_element_type=F32) * w_intra
                cst = c_scr[st]
                nst = n_scr[st]
                num = (w_inter * jnp.dot(qb, cst.astype(BF16), preferred_element_type=F32) +
                       jnp.dot(s.astype(BF16), v, preferred_element_type=F32))
                den = w_inter * jnp.sum(q * nst, axis=-1, keepdims=True) + jnp.sum(s, axis=-1, keepdims=True)
                hv = num / jnp.maximum(jnp.abs(den), jnp.exp(-m_q))
                log_k = g_tot - b_col + i_col
                m_new = jnp.maximum(g_tot + m_old, jnp.max(log_k, axis=0, keepdims=True))
                wk = jnp.exp(log_k - m_new)
                decay = jnp.exp(g_tot + m_old - m_new)
                kw = k * wk
                c_scr[st] = decay * cst + lax.dot_general(kw.astype(BF16), v, _DN_TA, preferred_element_type=F32)
                n_scr[st] = decay * nst + jnp.sum(kw, axis=0, keepdims=True)
                new_m.append(m_new)
                mu = jnp.mean(hv, axis=-1, keepdims=True)
                dv = hv - mu
                var = jnp.mean(dv * dv, axis=-1, keepdims=True)
                hn = dv * lax.rsqrt(var + LN_EPS) * ng_ref[:, h * dh:(h + 1) * dh]
                og = jax.nn.sigmoid(oa[:, h * dh:(h + 1) * dh])
                y_ref[bi, pl.ds(r0, L), h * dh:(h + 1) * dh] = (og * hn).astype(y_ref.dtype)
        return tuple(new_m)

    lax.fori_loop(0, nchunk, body, tuple(jnp.zeros((1, 1), F32) for _ in range(nb * H)))


def _mlstm(mq, mk, mv, mo, mif, conv_w, i_bias, f_bias, norm_g, B, S, nb):
    T = B * S
    H, dh, L = ML_HEADS, ML_DH, ML_CHUNK
    W = H * dh
    gt = mif[:, :2 * H].reshape(B, S, 2 * H).transpose(0, 2, 1).reshape(B, 2 * H, S // L, L)
    cw = conv_w.reshape(ML_CONV, 2 * W)
    bias = jnp.concatenate([i_bias, f_bias])
    bn = jnp.pad(bias, (0, LANES - 2 * H)).reshape(1, LANES)
    bt = bias.reshape(2 * H, 1)
    ng = norm_g.reshape(1, W)
    tri = jnp.stack([jnp.tril(jnp.ones((L, L), F32)), jnp.triu(jnp.ones((L, L), F32))])
    seq = lambda a: a.reshape(B, S, a.shape[1])
    rows = lambda w: pl.BlockSpec((nb, S, w), lambda b: (b, 0, 0))
    full = lambda a: pl.BlockSpec(a.shape, lambda b: (0,) * a.ndim)
    y = pl.pallas_call(
        _mlstm_kernel, grid=(B // nb,),
        in_specs=[rows(W), rows(W), rows(W), rows(W), rows(LANES),
                  pl.BlockSpec((nb, 2 * H, S // L, L), lambda b: (b, 0, 0, 0)),
                  full(cw), full(bn), full(bt), full(ng), full(tri)],
        out_specs=rows(W),
        out_shape=jax.ShapeDtypeStruct((B, S, W), BF16),
        scratch_shapes=[pltpu.VMEM((nb * H, dh, dh), F32), pltpu.VMEM((nb * H, 1, dh), F32)],
        compiler_params=_params(("parallel",)), name="mlstm",
    )(seq(mq), seq(mk), seq(mv), seq(mo), seq(mif), gt, cw, bn, bt, ng, tri)
    return y.reshape(T, W)


def _outproj_kernel(yn_ref, ym_ref, x_ref, w_ref, g_ref, b_ref, o_ref):
    mix = (jnp.dot(yn_ref[...], w_ref[0:512, :], preferred_element_type=F32) +
           jnp.dot(ym_ref[...], w_ref[512:1024, :], preferred_element_type=F32))
    o_ref[...] = _ln_rows(ALPHA * x_ref[...] + mix, g_ref[...], b_ref[...])


def _outproj(y_nsa, y_ml, x2d, w_out, g, b, tm):
    T = x2d.shape[0]
    wn = w_out[:512].reshape(NSA_GROUPS, NSA_HPG, NSA_DK, D_MODEL).transpose(1, 0, 2, 3).reshape(512, D_MODEL)
    w = jnp.concatenate([wn, w_out[512:]], axis=0).astype(BF16)
    row = lambda wd: pl.BlockSpec((tm, wd), lambda i: (i, 0))
    full = lambda a: pl.BlockSpec(a.shape, lambda i: (0,) * a.ndim)
    g2, b2 = g.reshape(1, -1), b.reshape(1, -1)
    return pl.pallas_call(
        _outproj_kernel, grid=(T // tm,),
        in_specs=[row(512), row(512), row(D_MODEL), full(w), full(g2), full(b2)],
        out_specs=row(D_MODEL), out_shape=jax.ShapeDtypeStruct((T, D_MODEL), F32),
        compiler_params=_params(("parallel",)), name="outproj")(y_nsa, y_ml, x2d, w, g2, b2)


def _memkv_kernel(m_ref, w_ref, o_ref):
    o_ref[...] = jnp.dot(m_ref[...].astype(BF16), w_ref[...], preferred_element_type=F32).astype(o_ref.dtype)


def _memkv(mem2d, wk, wv):
    w = jnp.concatenate([wk, wv], axis=1).astype(BF16)
    R = mem2d.shape[0]
    return pl.pallas_call(
        _memkv_kernel, grid=(R // MEM_LEN,),
        in_specs=[pl.BlockSpec((MEM_LEN, D_MODEL), lambda i: (i, 0)),
                  pl.BlockSpec(w.shape, lambda i: (0, 0))],
        out_specs=pl.BlockSpec((MEM_LEN, 2 * D_MODEL), lambda i: (i, 0)),
        out_shape=jax.ShapeDtypeStruct((R, 2 * D_MODEL), BF16),
        compiler_params=_params(("parallel",)), name="memkv")(mem2d, w)


def _xattn_kernel(x_ref, kv_ref, wq_ref, wo_ref, g_ref, b_ref, rw_ref, x2_ref, x2p_ref, sc_ref):
    x1 = x_ref[...]
    q = jnp.dot(x1.astype(BF16), wq_ref[...], preferred_element_type=F32).astype(BF16)
    outs = []
    for h in range(XA_HEADS):
        qh = q[:, h * XA_DH:(h + 1) * XA_DH]
        kh = kv_ref[:, h * XA_DH:(h + 1) * XA_DH]
        vh = kv_ref[:, D_MODEL + h * XA_DH:D_MODEL + (h + 1) * XA_DH]
        s = lax.dot_general(qh, kh, _DN_T, preferred_element_type=F32) * (XA_DH ** -0.5)
        p = jnp.exp(s - jnp.max(s, axis=-1, keepdims=True))
        p = p / jnp.sum(p, axis=-1, keepdims=True)
        outs.append(jnp.dot(p.astype(BF16), vh, preferred_element_type=F32).astype(BF16))
    o = jnp.concatenate(outs, axis=1)
    xa = jnp.dot(o, wo_ref[...], preferred_element_type=F32)
    x2 = _ln_rows(ALPHA * x1 + xa, g_ref[...], b_ref[...])
    x2_ref[...] = x2
    x2p_ref[...] = _pack_bf16_pairs(x2)
    xh = x2.astype(BF16)
    xl = (x2 - xh.astype(F32)).astype(BF16)
    wh = rw_ref[0]
    wl = rw_ref[1]
    logit = (lax.dot_general(wh, xh, _DN_T, preferred_element_type=F32) +
             lax.dot_general(wh, xl, _DN_T, preferred_element_type=F32) +
             lax.dot_general(wl, xh, _DN_T, preferred_element_type=F32))
    sc_ref[...] = jax.nn.sigmoid(logit)


def _xattn(x1, kv, wq, wo, g, b, router_w, S, tq):
    T = x1.shape[0]
    wqb, wob = wq.astype(BF16), wo.astype(BF16)
    rwt = router_w.T
    rh = rwt.astype(BF16)
    rw = jnp.stack([rh, (rwt - rh.astype(F32)).astype(BF16)])
    g2, b2 = g.reshape(1, -1), b.reshape(1, -1)
    full = lambda a: pl.BlockSpec(a.shape, lambda i: (0,) * a.ndim)
    per = S // tq
    return pl.pallas_call(
        _xattn_kernel, grid=(T // tq,),
        in_specs=[pl.BlockSpec((tq, D_MODEL), lambda i: (i, 0)),
                  pl.BlockSpec((MEM_LEN, 2 * D_MODEL), lambda i: (i // per, 0)),
                  full(wqb), full(wob), full(g2), full(b2), full(rw)],
        out_specs=(pl.BlockSpec((tq, D_MODEL), lambda i: (i, 0)),
                   pl.BlockSpec((tq, D_MODEL // 2), lambda i: (i, 0)),
                   pl.BlockSpec((MOE_E, tq), lambda i: (0, i))),
        out_shape=(jax.ShapeDtypeStruct((T, D_MODEL), F32), jax.ShapeDtypeStruct((T, D_MODEL // 2), I32),
                   jax.ShapeDtypeStruct((MOE_E, T), F32)),
        compiler_params=_params(("parallel",)), name="xattn")(x1, kv, wqb, wob, g2, b2, rw)


def _route_kernel(sc_ref, rb_ref, idx_ref, w_ref):
    E, G = MOE_E, MOE_GROUPS
    per = E // G
    scores = sc_ref[...]
    tr = scores.shape[1]
    biased = scores + rb_ref[...]
    g3 = biased.reshape(G, per, tr)
    j3 = lax.broadcasted_iota(I32, (G, per, tr), 1)
    m1 = jnp.max(g3, axis=1, keepdims=True)
    first = jnp.min(jnp.where(g3 == m1, j3, per), axis=1, keepdims=True)
    m2 = jnp.max(jnp.where(j3 == first, -jnp.inf, g3), axis=1, keepdims=True)
    gs = (m1 + m2).reshape(G, tr)
    gi = lax.broadcasted_iota(I32, (G, tr), 0)
    cnt = jnp.zeros((G, tr), F32)
    for m in range(G):
        row = gs[m:m + 1, :]
        cnt = cnt + jnp.where((row > gs) | ((row == gs) & (gi > m)), 1.0, 0.0)
    gmask = cnt < float(MOE_TOPK_GROUPS)
    masked = jnp.where(gmask[:, None, :], g3, NEG).reshape(E, tr)
    ei = lax.broadcasted_iota(I32, (E, tr), 0)
    idxs, ws = [], []
    for _ in range(MOE_K):
        mx = jnp.max(masked, axis=0, keepdims=True)
        ix = jnp.min(jnp.where(masked == mx, ei, E), axis=0, keepdims=True)
        hit = ei == ix
        ws.append(jnp.sum(jnp.where(hit, scores, 0.0), axis=0, keepdims=True))
        idxs.append(ix)
        masked = jnp.where(hit, -jnp.inf, masked)
    w = jnp.concatenate(ws, axis=0)
    idx_ref[...] = jnp.concatenate(idxs, axis=0)
    w_ref[...] = w / jnp.sum(w, axis=0, keepdims=True) * MOE_ROUTE_SCALE


def _route(scores_t, router_bias, tr):
    E, T = scores_t.shape
    rb = router_bias.reshape(E, 1)
    return pl.pallas_call(
        _route_kernel, grid=(T // tr,),
        in_specs=[pl.BlockSpec((E, tr), lambda i: (0, i)), pl.BlockSpec((E, 1), lambda i: (0, 0))],
        out_specs=(pl.BlockSpec((MOE_K, tr), lambda i: (0, i)), pl.BlockSpec((MOE_K, tr), lambda i: (0, i))),
        out_shape=(jax.ShapeDtypeStruct((MOE_K, T), I32), jax.ShapeDtypeStruct((MOE_K, T), F32)),
        compiler_params=_params(("parallel",)), name="route")(scores_t, rb)


def _rank_kernel(idx_ref, u_ref, rank_ref, cnt_ref, carry):
    E = MOE_E

    @pl.when(pl.program_id(0) == 0)
    def _():
        carry[...] = jnp.zeros_like(carry)

    idx = idx_ref[...]
    tp = idx.shape[1]
    ei = lax.broadcasted_iota(I32, (E, tp), 0)
    hits = [ei == idx[k:k + 1, :] for k in range(MOE_K)]
    onehot = jnp.zeros((E, tp), F32)
    for hit in hits:
        onehot = onehot + jnp.where(hit, 1.0, 0.0)
    pos = jnp.dot(onehot.astype(BF16), u_ref[...], preferred_element_type=F32) + carry[...]
    ranks = [jnp.sum(jnp.where(hit, pos, 0.0), axis=0, keepdims=True) for hit in hits]
    rank_ref[...] = jnp.concatenate(ranks, axis=0).astype(I32)
    total = carry[...] + jnp.sum(onehot, axis=1, keepdims=True)
    carry[...] = total
    cnt_ref[...] = jnp.broadcast_to(total, cnt_ref.shape).astype(I32)


def _rank(idx_t, tp):
    K, T = idx_t.shape
    u = jnp.triu(jnp.ones((tp, tp), F32), k=1).astype(BF16)
    rank, cnt = pl.pallas_call(
        _rank_kernel, grid=(T // tp,),
        in_specs=[pl.BlockSpec((K, tp), lambda i: (0, i)), pl.BlockSpec((tp, tp), lambda i: (0, 0))],
        out_specs=(pl.BlockSpec((K, tp), lambda i: (0, i)), pl.BlockSpec((MOE_E, LANES), lambda i: (0, 0))),
        out_shape=(jax.ShapeDtypeStruct((K, T), I32), jax.ShapeDtypeStruct((MOE_E, LANES), I32)),
        scratch_shapes=[pltpu.VMEM((MOE_E, 1), F32)],
        compiler_params=_params(("arbitrary",)), name="rank")(idx_t, u)
    return rank, cnt[:, 0]


def _dest_kernel(idx_ref, rank_ref, po_ref, dest_ref):
    idx = idx_ref[...]
    tp = idx.shape[1]
    ei = lax.broadcasted_iota(I32, (MOE_E, tp), 0)
    po = po_ref[...]
    base = [jnp.sum(jnp.where(ei == idx[k:k + 1, :], po, 0.0), axis=0, keepdims=True) for k in range(MOE_K)]
    dest_ref[...] = jnp.concatenate(base, axis=0).astype(I32) + rank_ref[...]


def _dest(idx_t, rank_t, poffs, tp):
    K, T = idx_t.shape
    po = poffs.astype(F32).reshape(MOE_E, 1)
    spec = pl.BlockSpec((K, tp), lambda i: (0, i))
    return pl.pallas_call(
        _dest_kernel, grid=(T // tp,),
        in_specs=[spec, spec, pl.BlockSpec((MOE_E, 1), lambda i: (0, 0))],
        out_specs=spec, out_shape=jax.ShapeDtypeStruct((K, T), I32),
        compiler_params=_params(("parallel",)), name="dest")(idx_t, rank_t, po)


def _pack_bf16_pairs(v):
    m = v.shape[1] // 2
    bits = lax.bitcast_convert_type(v.astype(BF16).astype(F32), jnp.uint32)
    return lax.bitcast_convert_type((bits[:, :m] >> 16) | (bits[:, m:] & jnp.uint32(0xFFFF0000)), I32)


def _unpack_bf16_pairs(w):
    w = lax.bitcast_convert_type(w, jnp.uint32)
    lo = lax.bitcast_convert_type(w << 16, F32)
    hi = lax.bitcast_convert_type(w & jnp.uint32(0xFFFF0000), F32)
    return lo, hi


def _row_scatter(rows, dest_t, n_rows):
    T, d = rows.shape
    K = dest_t.shape[0]
    nw = SC_CORES * SC_SUBCORES
    per_w = T // nw
    ch = SC_GATHER_ROWS
    mesh = plsc.VectorSubcoreMesh(core_axis_name="c", subcore_axis_name="s")

    @functools.partial(
        pl.kernel, mesh=mesh, out_type=jax.ShapeDtypeStruct((n_rows, d), I32),
        scratch_types=[pltpu.VMEM((ch,), I32), pltpu.VMEM((ch, d), I32), pltpu.SemaphoreType.DMA],
        name="row_scatter")
    def scatter(rows_hbm, dest_hbm, out_hbm, idx_v, rows_v, sem):
        wid = lax.axis_index("s") * SC_CORES + lax.axis_index("c")
        base = wid * per_w

        @pl.loop(0, per_w // ch)
        def _(i):
            off = pl.multiple_of(base + i * ch, ch)
            pltpu.sync_copy(rows_hbm.at[pl.ds(off, ch)], rows_v)
            for k in range(K):
                pltpu.sync_copy(dest_hbm.at[k, pl.ds(off, ch)], idx_v)
                pltpu.async_copy(rows_v, out_hbm.at[idx_v], sem).wait()

    return scatter(rows, dest_t)


def _expert_kernel(po_ref, cnt_ref, xs_hbm, w1_ref, w3_ref, w2_ref, ys_hbm,
                   xbuf, ybuf, w1b, w3b, w2b, gcnt, insem, outsem):
    e = pl.program_id(0)
    ne = pl.num_programs(0)
    n = cnt_ref[e]
    base = po_ref[e]
    rb = xbuf.shape[1]
    hw = D_MODEL // 2
    nblk = jnp.maximum((n + rb - 1) // rb, 1)
    w1b[...] = w1_ref[...].astype(BF16)
    w3b[...] = w3_ref[...].astype(BF16)
    w2b[...] = w2_ref[...].astype(BF16)

    def in_copy(r0, slot):
        return pltpu.make_async_copy(xs_hbm.at[pl.ds(pl.multiple_of(r0, rb), rb)], xbuf.at[slot], insem.at[slot])

    def out_copy(r0, slot):
        return pltpu.make_async_copy(ybuf.at[slot], ys_hbm.at[pl.ds(pl.multiple_of(r0, rb), rb)], outsem.at[slot])

    @pl.when(e == 0)
    def _():
        gcnt[0] = 0
        in_copy(base, 0).start()

    g0 = gcnt[0]

    def body(j, c):
        slot = (g0 + j) % 2

        @pl.when(j + 1 < nblk)
        def _():
            in_copy(base + (j + 1) * rb, 1 - slot).start()

        @pl.when((j + 1 == nblk) & (e + 1 < ne))
        def _():
            in_copy(po_ref[jnp.minimum(e + 1, ne - 1)], 1 - slot).start()

        in_copy(0, slot).wait()

        @pl.when(g0 + j >= 2)
        def _():
            out_copy(0, slot).wait()

        words = xbuf[slot]
        row = j * rb + lax.broadcasted_iota(I32, words.shape, 0)
        lo, hi = _unpack_bf16_pairs(jnp.where(row < n, words, 0))
        lo, hi = lo.astype(BF16), hi.astype(BF16)
        a = (jnp.dot(lo, w1b[0:hw, :], preferred_element_type=F32) +
             jnp.dot(hi, w1b[hw:, :], preferred_element_type=F32))
        g = (jnp.dot(lo, w3b[0:hw, :], preferred_element_type=F32) +
             jnp.dot(hi, w3b[hw:, :], preferred_element_type=F32))
        h = (a * jax.nn.sigmoid(a) * g).astype(BF16)
        ybuf[slot] = _pack_bf16_pairs(jnp.dot(h, w2b[...], preferred_element_type=F32))
        out_copy(base + j * rb, slot).start()
        return c

    lax.fori_loop(0, nblk, body, 0)
    total = g0 + nblk
    gcnt[0] = total

    @pl.when(e + 1 == ne)
    def _():
        @pl.when(total >= 2)
        def _():
            out_copy(0, total % 2).wait()

        out_copy(0, (total - 1) % 2).wait()


def _experts(poffs, counts, xs, w1, w3, w2):
    n_rows, wp = xs.shape
    rb = ROW_BLOCK
    wspec = lambda shape: pl.BlockSpec((None,) + shape, lambda e, po, cn: (e, 0, 0))
    gs = pltpu.PrefetchScalarGridSpec(
        num_scalar_prefetch=2, grid=(MOE_E,),
        in_specs=[pl.BlockSpec(memory_space=pl.ANY), wspec((D_MODEL, MOE_FF)), wspec((D_MODEL, MOE_FF)),
                  wspec((MOE_FF, D_MODEL))],
        out_specs=pl.BlockSpec(memory_space=pl.ANY),
        scratch_shapes=[pltpu.VMEM((2, rb, wp), I32), pltpu.VMEM((2, rb, wp), I32),
                        pltpu.VMEM((D_MODEL, MOE_FF), BF16), pltpu.VMEM((D_MODEL, MOE_FF), BF16),
                        pltpu.VMEM((MOE_FF, D_MODEL), BF16), pltpu.SMEM((1,), I32),
                        pltpu.SemaphoreType.DMA((2,)), pltpu.SemaphoreType.DMA((2,))])
    return pl.pallas_call(
        _expert_kernel, grid_spec=gs, out_shape=jax.ShapeDtypeStruct((n_rows, wp), I32),
        compiler_params=_params(("arbitrary",)), name="experts")(poffs, counts, xs, w1, w3, w2)


def _row_gather(table, idx):
    n, d = idx.shape[0], table.shape[1]
    nw = SC_CORES * SC_SUBCORES
    per_w = n // nw
    ch = SC_GATHER_ROWS
    mesh = plsc.VectorSubcoreMesh(core_axis_name="c", subcore_axis_name="s")

    @functools.partial(
        pl.kernel, mesh=mesh, out_type=jax.ShapeDtypeStruct((n, d), I32),
        scratch_types=[pltpu.VMEM((ch,), I32), pltpu.VMEM((ch, d), I32), pltpu.SemaphoreType.DMA],
        name="row_gather")
    def gather(table_hbm, idx_hbm, out_hbm, idx_v, rows_v, sem):
        wid = lax.axis_index("s") * SC_CORES + lax.axis_index("c")
        base = wid * per_w

        @pl.loop(0, per_w // ch)
        def _(i):
            off = pl.multiple_of(base + i * ch, ch)
            pltpu.sync_copy(idx_hbm.at[pl.ds(off, ch)], idx_v)
            pltpu.async_copy(table_hbm.at[idx_v], rows_v, sem).wait()
            pltpu.sync_copy(rows_v, out_hbm.at[pl.ds(off, ch)])

    return gather(table, idx)


def _combine_kernel(w_ref, tile_ref, x_ref, ysg_ref, s1_ref, s3_ref, s2_ref, g_ref, b_ref, o_ref):
    x2 = x_ref[...]
    xb = x2.astype(BF16)
    a = jnp.dot(xb, s1_ref[...], preferred_element_type=F32)
    c = jnp.dot(xb, s3_ref[...], preferred_element_type=F32)
    shared = jnp.dot((a * jax.nn.sigmoid(a) * c).astype(BF16), s2_ref[...], preferred_element_type=F32)
    lo, hi = _unpack_bf16_pairs(ysg_ref[...])
    lo, hi = lo.astype(BF16), hi.astype(BF16)
    wrep = jnp.dot(w_ref[...], tile_ref[...], precision=lax.Precision.HIGHEST, preferred_element_type=F32)
    col = lax.broadcasted_iota(I32, wrep.shape, 1)
    row = lax.broadcasted_iota(I32, wrep.shape, 0)
    wsel = jnp.where(col // MOE_K == row, wrep, 0.0)
    wh = wsel.astype(BF16)
    wl = (wsel - wh.astype(F32)).astype(BF16)
    y_lo = jnp.dot(wh, lo, preferred_element_type=F32) + jnp.dot(wl, lo, preferred_element_type=F32)
    y_hi = jnp.dot(wh, hi, preferred_element_type=F32) + jnp.dot(wl, hi, preferred_element_type=F32)
    y = shared + jnp.concatenate([y_lo, y_hi], axis=1)
    o_ref[...] = _ln_rows(ALPHA * x2 + y, g_ref[...], b_ref[...])


def _combine(w_nat, x2, ysg, sw1, sw3, sw2, g, b, tc):
    T = x2.shape[0]
    wp = ysg.shape[1]
    full = lambda a: pl.BlockSpec(a.shape, lambda i: (0,) * a.ndim)
    s1, s3, s2 = sw1.astype(BF16), sw3.astype(BF16), sw2.astype(BF16)
    g2, b2 = g.reshape(1, -1), b.reshape(1, -1)
    tile = jnp.asarray(np.arange(tc * MOE_K)[None, :] % MOE_K == np.arange(MOE_K)[:, None], F32)
    return pl.pallas_call(
        _combine_kernel, grid=(T // tc,),
        in_specs=[pl.BlockSpec((tc, MOE_K), lambda i: (i, 0)), full(tile),
                  pl.BlockSpec((tc, D_MODEL), lambda i: (i, 0)),
                  pl.BlockSpec((tc * MOE_K, wp), lambda i: (i, 0)),
                  full(s1), full(s3), full(s2), full(g2), full(b2)],
        out_specs=pl.BlockSpec((tc, D_MODEL), lambda i: (i, 0)),
        out_shape=jax.ShapeDtypeStruct((T, D_MODEL), F32),
        compiler_params=_params(("parallel",)), name="combine",
    )(w_nat, tile, x2, ysg, s1, s3, s2, g2, b2)


def _moe(x2, x2p, scores_t, router_bias, w1, w3, w2, sw1, sw3, sw2, g, b, tiles):
    T = x2.shape[0]
    rb = ROW_BLOCK
    idx_t, w_t = _route(scores_t, router_bias, tiles['route'])
    rank_t, counts = _rank(idx_t, tiles['rank'])
    pcounts = jnp.maximum((counts + rb - 1) // rb, 1) * rb
    pends = jnp.cumsum(pcounts)
    poffs = (pends - pcounts).astype(I32)
    n_blocks = -(-T * MOE_K // rb) + MOE_E
    dest_t = _dest(idx_t, rank_t, poffs, tiles['rank'])
    xs = _row_scatter(x2p, dest_t, n_blocks * rb)
    ys = _experts(poffs, counts.astype(I32), xs, w1, w3, w2)
    ysg = _row_gather(ys, dest_t.T.reshape(-1))
    return _combine(w_t.T, x2, ysg, sw1, sw3, sw2, g, b, tiles['combine'])


def _tiles(B, S):
    T = B * S
    pick = lambda want, n: want if n % want == 0 else n
    return dict(proj=pick(512, T), nsa_q=pick(128, S), nsa_ck=pick(512, S), outproj=pick(512, T),
                mlstm_nb=2 if B % 2 == 0 else 1, xattn=pick(256, S), route=pick(512, T), rank=pick(512, T), scatter=pick(256, T),
                combine=pick(128, T))


def kernel(x, mem, w_in, nsa_pos_k, nsa_cmp_k_w1, nsa_cmp_k_w2, nsa_pos_v, nsa_cmp_v_w1, nsa_cmp_v_w2,
           mlstm_conv_w, mlstm_i_bias, mlstm_f_bias, mlstm_norm_g, w_out, ln1_g, ln1_b,
           xa_wq, xa_wk, xa_wv, xa_wo, ln2_g, ln2_b, router_w, router_bias,
           moe_w1, moe_w3, moe_w2, shared_w1, shared_w3, shared_w2, ln3_g, ln3_b):
    B, S, D = x.shape
    T = B * S
    tl = _tiles(B, S)
    xc = x.reshape(T, D)
    memc = mem.reshape(B * MEM_LEN, D)
    for l in range(w_in.shape[0]):
        (q, cmp, ksel, vsel, kwin, vwin, gates, mq, mk, mv, mo, mif) = _project(xc, _prep_w_in(w_in[l]), tl['proj'])
        kcvc = _compress(cmp, B, S, _prep_cmp(nsa_pos_k[l], nsa_cmp_k_w1[l], nsa_cmp_k_w2[l],
                                              nsa_pos_v[l], nsa_cmp_v_w1[l], nsa_cmp_v_w2[l]))
        y_nsa = _nsa(q, kcvc, ksel, vsel, kwin, vwin, gates, B, S, tl['nsa_q'], tl['nsa_ck'])
        y_ml = _mlstm(mq, mk, mv, mo, mif, mlstm_conv_w[l], mlstm_i_bias[l], mlstm_f_bias[l],
                      mlstm_norm_g[l], B, S, tl['mlstm_nb'])
        x1 = _outproj(y_nsa, y_ml, xc, w_out[l], ln1_g[l], ln1_b[l], tl['outproj'])
        kv = _memkv(memc, xa_wk[l], xa_wv[l])
        x2, x2p, scores_t = _xattn(x1, kv, xa_wq[l], xa_wo[l], ln2_g[l], ln2_b[l], router_w[l], S, tl['xattn'])
        xc = _moe(x2, x2p, scores_t, router_bias[l], moe_w1[l], moe_w3[l], moe_w2[l],
                  shared_w1[l], shared_w3[l], shared_w2[l], ln3_g[l], ln3_b[l], tl)
    return xc.reshape(B, S, D)
```

```python
import functools
import numpy as np
import jax
import jax.numpy as jnp
from jax import lax
from jax.experimental import pallas as pl
from jax.experimental.pallas import tpu as pltpu
from jax.experimental.pallas import tpu_sc as plsc

F32 = jnp.float32
BF16 = jnp.bfloat16
I32 = jnp.int32

D_MODEL = 1024
MEM_LEN = 256
NSA_HEADS = 8
NSA_GROUPS = 2
NSA_HPG = 4
NSA_DK = 64
NSA_CMP_LEN = 32
NSA_CMP_STRIDE = 16
NSA_SEL_BLOCK = 64
NSA_SEL_TOPN = 8
NSA_WINDOW = 512
ML_HEADS = 4
ML_DH = 128
ML_CHUNK = 64
ML_CONV = 4
XA_HEADS = 4
XA_DH = 256
MOE_E = 256
MOE_K = 8
MOE_GROUPS = 8
MOE_TOPK_GROUPS = 4
MOE_FF = 256
MOE_ROUTE_SCALE = 2.5
DEPTH = 1
ALPHA = (2.0 * DEPTH) ** 0.25
LN_EPS = 1e-5
NEG = -1e30
FORCE_BONUS = 1e4

LANES = 128
ROW_BLOCK = 512
VMEM_LIMIT = 56 * 1024 * 1024
SC_CORES = 2
SC_SUBCORES = 16
SC_GATHER_ROWS = 128

_DN_T = (((1,), (1,)), ((), ()))
_DN_TA = (((0,), (0,)), ((), ()))


def _params(sem):
    return pltpu.CompilerParams(dimension_semantics=sem, vmem_limit_bytes=VMEM_LIMIT)


def _ln_rows(v, g, b):
    mu = jnp.mean(v, axis=-1, keepdims=True)
    d = v - mu
    var = jnp.mean(d * d, axis=-1, keepdims=True)
    return d * lax.rsqrt(var + LN_EPS) * g + b


_SEGS = (('q', 512, BF16), ('cmp', 256, F32), ('ksel', 128, BF16), ('vsel', 128, BF16),
         ('kwin', 128, BF16), ('vwin', 128, BF16), ('gates', 128, F32), ('mq', 512, BF16),
         ('mk', 512, BF16), ('mv', 512, BF16), ('mo', 512, BF16), ('mif', 128, F32))


def _proj_kernel(x_ref, w_ref, *out_refs):
    xb = x_ref[...].astype(BF16)
    off = 0
    for o_ref, (_, wd, _) in zip(out_refs, _SEGS):
        o_ref[...] = jnp.dot(xb, w_ref[:, off:off + wd], preferred_element_type=F32).astype(o_ref.dtype)
        off += wd


def _prep_w_in(w):
    sizes = (512,) + (128,) * 6 + (24,) + (512,) * 4 + (4, 4)
    cuts = np.cumsum(sizes)[:-1].tolist()
    (wq, kc, vc, ks, vs, kw, vw, wg, mq, mk, mv, mo, mi, mf) = jnp.split(w, cuts, axis=1)
    wq = wq.reshape(D_MODEL, NSA_GROUPS, NSA_HPG, NSA_DK).transpose(0, 2, 1, 3).reshape(D_MODEL, 512)
    pad = lambda a: jnp.pad(a, ((0, 0), (0, LANES - a.shape[1])))
    segs = [wq, kc, vc, ks, vs, kw, vw, pad(wg), mq, mk, mv, mo, pad(jnp.concatenate([mi, mf], axis=1))]
    return jnp.concatenate(segs, axis=1).astype(BF16)


def _project(x2d, w_all, tm):
    T = x2d.shape[0]
    n = w_all.shape[1]
    out_shape = tuple(jax.ShapeDtypeStruct((T, wd), dt) for _, wd, dt in _SEGS)
    out_specs = tuple(pl.BlockSpec((tm, wd), lambda i: (i, 0)) for _, wd, _ in _SEGS)
    return pl.pallas_call(
        _proj_kernel, grid=(T // tm,),
        in_specs=[pl.BlockSpec((tm, D_MODEL), lambda i: (i, 0)),
                  pl.BlockSpec((D_MODEL, n), lambda i: (0, 0))],
        out_specs=out_specs, out_shape=out_shape,
        compiler_params=_params(("parallel",)), name="proj")(x2d, w_all)


def _cmp_kernel(r_ref, pa_ref, pb_ref, wa_ref, wb_ref, w2_ref, o_ref):
    r = r_ref[...]
    a = jnp.dot((r + pa_ref[...]).astype(BF16), wa_ref[...], preferred_element_type=F32)
    b = jnp.dot((r + pb_ref[...]).astype(BF16), wb_ref[...], preferred_element_type=F32)
    nr = r.shape[0]
    hid = a + pltpu.roll(b, nr - 1, 0)
    hid = hid * jax.nn.sigmoid(hid)
    out = jnp.dot(hid.astype(BF16), w2_ref[...], preferred_element_type=F32)
    row = lax.broadcasted_iota(I32, out.shape, 0)
    o_ref[...] = jnp.where(row < nr - 1, out, 0.0).astype(o_ref.dtype)


def _prep_cmp(pos_k, w1_k, w2_k, pos_v, w1_v, w2_v):
    eye = jnp.eye(NSA_GROUPS, dtype=F32)

    def expand_w1(w1, half):
        w = w1.reshape(NSA_CMP_LEN, NSA_DK, NSA_DK)[half * 16:(half + 1) * 16]
        return jnp.einsum('jde,gk->jgdke', w, eye).reshape(16, 128, 128)

    def both(fk, fv):
        z = jnp.zeros_like(fk)
        top = jnp.concatenate([fk, z], axis=-1)
        bot = jnp.concatenate([z, fv], axis=-1)
        return jnp.concatenate([top, bot], axis=-2)

    wa = both(expand_w1(w1_k, 0), expand_w1(w1_v, 0)).reshape(16 * 256, 256).astype(BF16)
    wb = both(expand_w1(w1_k, 1), expand_w1(w1_v, 1)).reshape(16 * 256, 256).astype(BF16)
    w2 = both(jnp.kron(eye, w2_k), jnp.kron(eye, w2_v)).astype(BF16)

    def pos_row(half):
        pk = jnp.tile(pos_k[half * 16:(half + 1) * 16], (1, NSA_GROUPS))
        pv = jnp.tile(pos_v[half * 16:(half + 1) * 16], (1, NSA_GROUPS))
        return jnp.concatenate([pk, pv], axis=1).reshape(1, 16 * 256)

    return pos_row(0), pos_row(1), wa, wb, w2


def _compress(cmp2d, B, S, prep):
    pa, pb, wa, wb, w2 = prep
    nr = S // NSA_CMP_STRIDE
    r = cmp2d.reshape(B, nr, NSA_CMP_STRIDE * 256)
    full = lambda a: pl.BlockSpec(a.shape, lambda b: (0,) * a.ndim)
    return pl.pallas_call(
        _cmp_kernel, grid=(B,),
        in_specs=[pl.BlockSpec((None, nr, NSA_CMP_STRIDE * 256), lambda b: (b, 0, 0)),
                  full(pa), full(pb), full(wa), full(wb), full(w2)],
        out_specs=pl.BlockSpec((None, nr, 256), lambda b: (b, 0, 0)),
        out_shape=jax.ShapeDtypeStruct((B, nr, 256), BF16),
        compiler_params=_params(("parallel",)), name="cmp")(r, pa, pb, wa, wb, w2)


def _nsa_consts(S):
    n_cmp = (S - NSA_CMP_LEN) // NSA_CMP_STRIDE + 1
    n_sel = S // NSA_SEL_BLOCK
    cs = np.arange(n_cmp) * NSA_CMP_STRIDE
    ss = np.arange(n_sel) * NSA_SEL_BLOCK
    ov = ((cs[:, None] < ss[None, :] + NSA_SEL_BLOCK) & (cs[:, None] + NSA_CMP_LEN > ss[None, :]))
    ovt = np.zeros((LANES, S // NSA_CMP_STRIDE), np.float32)
    ovt[:n_sel, :n_cmp] = ov.T
    e = np.zeros((LANES, S), np.float32)
    e[np.arange(S) // NSA_SEL_BLOCK, np.arange(S)] = 1.0
    return jnp.asarray(ovt, BF16), jnp.asarray(e, BF16)


def _nsa_kernel(q_ref, kcvc_ref, ksel_ref, vsel_ref, kwin_ref, vwin_ref, gates_ref, ovt_ref, e_ref,
                y_ref, bias_ref, *, tq, ck, n_sel):
    G, H = NSA_GROUPS, NSA_HPG
    GH = G * H
    M = GH * tq
    S = ksel_ref.shape[0]
    W = NSA_WINDOW
    ws = min(W + tq, S)
    t0 = pl.program_id(1) * tq
    gates = jax.nn.sigmoid(gates_ref[...])
    lane = lax.broadcasted_iota(I32, (tq, LANES), 1)
    t_col = t0 + lax.broadcasted_iota(I32, (tq, 1), 0)
    kc = kcvc_ref[:, 0:LANES]
    vc = kcvc_ref[:, LANES:2 * LANES]
    nc = kc.shape[0]
    qs = []
    for g in range(G):
        gmask = (lane // NSA_DK) == g
        for h in range(H):
            qh = q_ref[:, h * LANES:(h + 1) * LANES] * (NSA_DK ** -0.5)
            qs.append(jnp.where(gmask, qh, jnp.zeros_like(qh)))
    Q = jnp.concatenate(qs, axis=0).astype(BF16)

    s = lax.dot_general(Q, kc, _DN_T, preferred_element_type=F32)
    c_idx = lax.broadcasted_iota(I32, (tq, nc), 1)
    cmask = (c_idx * NSA_CMP_STRIDE + NSA_CMP_LEN - 1) <= t_col
    s3 = jnp.where(cmask[None], s.reshape(GH, tq, nc), NEG)
    p = jnp.exp(s3 - jnp.max(s3, axis=-1, keepdims=True))
    p = p / jnp.sum(p, axis=-1, keepdims=True)
    p = jnp.where(cmask[None], p, 0.0)
    o_cmp = jnp.dot(p.reshape(M, nc).astype(BF16), vc, preferred_element_type=F32).reshape(GH, tq, LANES)

    for g in range(G):
        psum = jnp.sum(p[g * H:(g + 1) * H], axis=0)
        hi = psum.astype(BF16)
        lo = (psum - hi.astype(F32)).astype(BF16)
        ovt = ovt_ref[...]
        pslt = (lax.dot_general(ovt, hi, _DN_T, preferred_element_type=F32) +
                lax.dot_general(ovt, lo, _DN_T, preferred_element_type=F32))
        imp_p = pslt[0:n_sel, :]
        n_i = lax.broadcasted_iota(I32, (n_sel, tq), 0)
        cur = (t0 + lax.broadcasted_iota(I32, (n_sel, tq), 1)) // NSA_SEL_BLOCK
        forced = (n_i == 0) | (n_i == cur) | (n_i == cur - 1)
        imp = jnp.where(n_i <= cur, imp_p + jnp.where(forced, FORCE_BONUS, 0.0), NEG)
        cnt = jnp.zeros((n_sel, tq), F32)
        for m in range(n_sel):
            row = imp[m:m + 1, :]
            beats = (row > imp) | ((row == imp) & (n_i > m))
            cnt = cnt + jnp.where(beats, 1.0, 0.0)
        selt = jnp.where(cnt < float(min(NSA_SEL_TOPN, n_sel)), 1.0, 0.0)
        selt = jnp.concatenate([selt, jnp.zeros((LANES - n_sel, tq), F32)], axis=0)
        sel = selt.T.astype(BF16)
        maskf = jnp.dot(sel, e_ref[...], preferred_element_type=F32)
        kpos = lax.broadcasted_iota(I32, (tq, S), 1)
        bias_ref[g] = jnp.where((maskf > 0.5) & (kpos <= t_col), 0.0, NEG)

    vlane = lax.broadcasted_iota(I32, (1, LANES), 1) // NSA_DK

    def pv_with_sums(pb, v):
        outs = []
        for g in range(G):
            vg = jnp.where(vlane == g, v, jnp.ones_like(v))
            outs.append(jnp.dot(pb[g * H * tq:(g + 1) * H * tq], vg, preferred_element_type=F32))
        return jnp.concatenate(outs, axis=0)

    def normalise(acc):
        outs = []
        for r in range(GH):
            c = NSA_DK * (1 - r // H)
            outs.append(acc[r] / acc[r][:, c:c + 1])
        return outs

    def sel_body(j, carry):
        m_i, acc = carry
        ks = pl.multiple_of(j * ck, ck)
        k = ksel_ref[pl.ds(ks, ck), :]
        v = vsel_ref[pl.ds(ks, ck), :]
        sj = lax.dot_general(Q, k, _DN_T, preferred_element_type=F32)
        sj = (sj.reshape(G, H, tq, ck) + bias_ref[:, :, pl.ds(ks, ck)][:, None]).reshape(GH, tq, ck)
        m_new = jnp.maximum(m_i, jnp.max(sj, axis=-1, keepdims=True))
        a = jnp.exp(m_i - m_new)
        pj = jnp.exp(sj - m_new).reshape(M, ck).astype(BF16)
        return m_new, a * acc + pv_with_sums(pj, v).reshape(GH, tq, LANES)

    init = (jnp.full((GH, tq, 1), NEG, F32), jnp.zeros((GH, tq, LANES), F32))
    _, acc = lax.fori_loop(0, (t0 + tq + ck - 1) // ck, sel_body, init)
    o_sel = normalise(acc)

    kst = pl.multiple_of(jnp.clip(t0 - W, 0, S - ws), LANES)
    kwn = kwin_ref[pl.ds(kst, ws), :]
    vwn = vwin_ref[pl.ds(kst, ws), :]
    sw = lax.dot_general(Q, kwn, _DN_T, preferred_element_type=F32)
    wpos = kst + lax.broadcasted_iota(I32, (tq, ws), 1)
    wmask = (wpos <= t_col) & (wpos > t_col - W)
    sw3 = jnp.where(wmask[None], sw.reshape(GH, tq, ws), NEG)
    pw = jnp.exp(sw3 - jnp.max(sw3, axis=-1, keepdims=True)).reshape(M, ws).astype(BF16)
    o_win = normalise(pv_with_sums(pw, vwn).reshape(GH, tq, LANES))

    g0mask = lane < NSA_DK
    for h in range(H):
        o_g = []
        for g in range(G):
            r = g * H + h
            c0 = r * 3
            o_g.append(gates[:, c0:c0 + 1] * o_cmp[r] + gates[:, c0 + 1:c0 + 2] * o_sel[r] +
                       gates[:, c0 + 2:c0 + 3] * o_win[r])
        y_ref[:, h * LANES:(h + 1) * LANES] = jnp.where(g0mask, o_g[0], o_g[1]).astype(y_ref.dtype)


def _nsa(q, kcvc, ksel, vsel, kwin, vwin, gates, B, S, tq, ck):
    T = B * S
    nq = S // tq
    ovt, e = _nsa_consts(S)
    seq = lambda a: a.reshape(B, S, LANES)
    kv_spec = pl.BlockSpec((None, S, LANES), lambda b, i: (b, 0, 0))
    kern = functools.partial(_nsa_kernel, tq=tq, ck=ck, n_sel=S // NSA_SEL_BLOCK)
    return pl.pallas_call(
        kern, grid=(B, nq),
        in_specs=[pl.BlockSpec((tq, 512), lambda b, i: (b * nq + i, 0)),
                  pl.BlockSpec((None,) + kcvc.shape[1:], lambda b, i: (b, 0, 0)),
                  kv_spec, kv_spec, kv_spec, kv_spec,
                  pl.BlockSpec((tq, LANES), lambda b, i: (b * nq + i, 0)),
                  pl.BlockSpec(ovt.shape, lambda b, i: (0, 0)),
                  pl.BlockSpec(e.shape, lambda b, i: (0, 0))],
        out_specs=pl.BlockSpec((tq, 512), lambda b, i: (b * nq + i, 0)),
        out_shape=jax.ShapeDtypeStruct((T, 512), BF16),
        scratch_shapes=[pltpu.VMEM((NSA_GROUPS, tq, S), F32)],
        compiler_params=_params(("parallel", "parallel")), name="nsa",
    )(q, kcvc, seq(ksel), seq(vsel), seq(kwin), seq(vwin), gates, ovt, e)


def _mlstm_kernel(q_ref, k_ref, v_ref, o_ref, gn_ref, gt_ref, cw_ref, bn_ref, bt_ref, ng_ref, tri_ref,
                  y_ref, c_scr, n_scr):
    H, dh, L = ML_HEADS, ML_DH, ML_CHUNK
    nb, S = q_ref.shape[0], q_ref.shape[1]
    nchunk = S // L
    c_scr[...] = jnp.zeros_like(c_scr)
    n_scr[...] = jnp.zeros_like(n_scr)
    row = lax.broadcasted_iota(I32, (L, H * dh), 0)
    li = lax.broadcasted_iota(I32, (L, L), 0)
    mi = lax.broadcasted_iota(I32, (L, L), 1)
    causal = mi <= li
    tril = tri_ref[0]
    triu = tri_ref[1]
    hp = lax.Precision.HIGHEST

    def conv_silu(ref, bi, c, wofs):
        r0 = pl.multiple_of(c * L, L)
        rp = pl.multiple_of(jnp.maximum(c - 1, 0) * L, L)
        cur = ref[bi, pl.ds(r0, L), :].astype(F32)
        prev = ref[bi, pl.ds(rp, L), :].astype(F32) * jnp.where(c > 0, 1.0, 0.0)
        acc = cur * cw_ref[ML_CONV - 1:ML_CONV, wofs:wofs + H * dh]
        for j in range(1, ML_CONV):
            sh = jnp.where(row < j, pltpu.roll(prev, j, 0), pltpu.roll(cur, j, 0))
            acc = acc + sh * cw_ref[ML_CONV - 1 - j:ML_CONV - j, wofs:wofs + H * dh]
        return acc * jax.nn.sigmoid(acc)

    def body(c, m_state):
        r0 = pl.multiple_of(c * L, L)
        new_m = []
        for bi in range(nb):
            qa = conv_silu(q_ref, bi, c, 0) * (dh ** -0.5)
            ka = conv_silu(k_ref, bi, c, H * dh)
            va = v_ref[bi, pl.ds(r0, L), :]
            oa = o_ref[bi, pl.ds(r0, L), :].astype(F32)
            gn = gn_ref[bi, pl.ds(r0, L), :] + bn_ref[...]
            gt = gt_ref[bi, :, c, :] + bt_ref[...]
            lf_n = jax.nn.log_sigmoid(gn)
            lf_t = jax.nn.log_sigmoid(gt)
            b_n = jnp.dot(tril, lf_n, precision=hp, preferred_element_type=F32)
            b_t = jnp.dot(lf_t, triu, precision=hp, preferred_element_type=F32)
            for h in range(H):
                st = bi * H + h
                q = qa[:, h * dh:(h + 1) * dh]
                k = ka[:, h * dh:(h + 1) * dh]
                v = va[:, h * dh:(h + 1) * dh]
                m_old = m_state[st]
                b_col = b_n[:, H + h:H + h + 1]
                i_col = gn[:, h:h + 1]
                b_row = b_t[H + h:H + h + 1, :]
                i_row = gt[h:h + 1, :]
                g_tot = b_t[H + h:H + h + 1, L - 1:L]
                d_log = jnp.where(causal, b_col - b_row + i_row, NEG)
                inter = b_col + m_old
                m_q = jnp.maximum(inter, jnp.max(d_log, axis=-1, keepdims=True))
                w_intra = jnp.exp(d_log - m_q)
                w_inter = jnp.exp(inter - m_q)
                qb = q.astype(BF16)
                s = lax.dot_general(qb, k.astype(BF16), _DN_T, preferred_element_type=F32) * w_intra
                cst = c_scr[st]
                nst = n_scr[st]
                num = (w_inter * jnp.dot(qb, cst.astype(BF16), preferred_element_type=F32) +
                       jnp.dot(s.astype(BF16), v, preferred_element_type=F32))
                den = w_inter * jnp.sum(q * nst, axis=-1, keepdims=True) + jnp.sum(s, axis=-1, keepdims=True)
                hv = num / jnp.maximum(jnp.abs(den), jnp.exp(-m_q))
                log_k = g_tot - b_col + i_col
                m_new = jnp.maximum(g_tot + m_old, jnp.max(log_k, axis=0, keepdims=True))
                wk = jnp.exp(log_k - m_new)
                decay = jnp.exp(g_tot + m_old - m_new)
                kw = k * wk
                c_scr[st] = decay * cst + lax.dot_general(kw.astype(BF16), v, _DN_TA, preferred_element_type=F32)
                n_scr[st] = decay * nst + jnp.sum(kw, axis=0, keepdims=True)
                new_m.append(m_new)
                mu = jnp.mean(hv, axis=-1, keepdims=True)
                dv = hv - mu
                var = jnp.mean(dv * dv, axis=-1, keepdims=True)
                hn = dv * lax.rsqrt(var + LN_EPS) * ng_ref[:, h * dh:(h + 1) * dh]
                og = jax.nn.sigmoid(oa[:, h * dh:(h + 1) * dh])
                y_ref[bi, pl.ds(r0, L), h * dh:(h + 1) * dh] = (og * hn).astype(y_ref.dtype)
        return tuple(new_m)

    lax.fori_loop(0, nchunk, body, tuple(jnp.zeros((1, 1), F32) for _ in range(nb * H)))


def _mlstm(mq, mk, mv, mo, mif, conv_w, i_bias, f_bias, norm_g, B, S, nb):
    T = B * S
    H, dh, L = ML_HEADS, ML_DH, ML_CHUNK
    W = H * dh
    gt = mif[:, :2 * H].reshape(B, S, 2 * H).transpose(0, 2, 1).reshape(B, 2 * H, S // L, L)
    cw = conv_w.reshape(ML_CONV, 2 * W)
    bias = jnp.concatenate([i_bias, f_bias])
    bn = jnp.pad(bias, (0, LANES - 2 * H)).reshape(1, LANES)
    bt = bias.reshape(2 * H, 1)
    ng = norm_g.reshape(1, W)
    tri = jnp.stack([jnp.tril(jnp.ones((L, L), F32)), jnp.triu(jnp.ones((L, L), F32))])
    seq = lambda a: a.reshape(B, S, a.shape[1])
    rows = lambda w: pl.BlockSpec((nb, S, w), lambda b: (b, 0, 0))
    full = lambda a: pl.BlockSpec(a.shape, lambda b: (0,) * a.ndim)
    y = pl.pallas_call(
        _mlstm_kernel, grid=(B // nb,),
        in_specs=[rows(W), rows(W), rows(W), rows(W), rows(LANES),
                  pl.BlockSpec((nb, 2 * H, S // L, L), lambda b: (b, 0, 0, 0)),
                  full(cw), full(bn), full(bt), full(ng), full(tri)],
        out_specs=rows(W),
        out_shape=jax.ShapeDtypeStruct((B, S, W), BF16),
        scratch_shapes=[pltpu.VMEM((nb * H, dh, dh), F32), pltpu.VMEM((nb * H, 1, dh), F32)],
        compiler_params=_params(("parallel",)), name="mlstm",
    )(seq(mq), seq(mk), seq(mv), seq(mo), seq(mif), gt, cw, bn, bt, ng, tri)
    return y.reshape(T, W)


def _outproj_kernel(yn_ref, ym_ref, x_ref, w_ref, g_ref, b_ref, o_ref):
    mix = (jnp.dot(yn_ref[...], w_ref[0:512, :], preferred_element_type=F32) +
           jnp.dot(ym_ref[...], w_ref[512:1024, :], preferred_element_type=F32))
    o_ref[...] = _ln_rows(ALPHA * x_ref[...] + mix, g_ref[...], b_ref[...])


def _outproj(y_nsa, y_ml, x2d, w_out, g, b, tm):
    T = x2d.shape[0]
    wn = w_out[:512].reshape(NSA_GROUPS, NSA_HPG, NSA_DK, D_MODEL).transpose(1, 0, 2, 3).reshape(512, D_MODEL)
    w = jnp.concatenate([wn, w_out[512:]], axis=0).astype(BF16)
    row = lambda wd: pl.BlockSpec((tm, wd), lambda i: (i, 0))
    full = lambda a: pl.BlockSpec(a.shape, lambda i: (0,) * a.ndim)
    g2, b2 = g.reshape(1, -1), b.reshape(1, -1)
    return pl.pallas_call(
        _outproj_kernel, grid=(T // tm,),
        in_specs=[row(512), row(512), row(D_MODEL), full(w), full(g2), full(b2)],
        out_specs=row(D_MODEL), out_shape=jax.ShapeDtypeStruct((T, D_MODEL), F32),
        compiler_params=_params(("parallel",)), name="outproj")(y_nsa, y_ml, x2d, w, g2, b2)


def _memkv_kernel(m_ref, w_ref, o_ref):
    o_ref[...] = jnp.dot(m_ref[...].astype(BF16), w_ref[...], preferred_element_type=F32).astype(o_ref.dtype)


def _memkv(mem2d, wk, wv):
    w = jnp.concatenate([wk, wv], axis=1).astype(BF16)
    R = mem2d.shape[0]
    return pl.pallas_call(
        _memkv_kernel, grid=(R // MEM_LEN,),
        in_specs=[pl.BlockSpec((MEM_LEN, D_MODEL), lambda i: (i, 0)),
                  pl.BlockSpec(w.shape, lambda i: (0, 0))],
        out_specs=pl.BlockSpec((MEM_LEN, 2 * D_MODEL), lambda i: (i, 0)),
        out_shape=jax.ShapeDtypeStruct((R, 2 * D_MODEL), BF16),
        compiler_params=_params(("parallel",)), name="memkv")(mem2d, w)


def _xattn_kernel(x_ref, kv_ref, wq_ref, wo_ref, g_ref, b_ref, rw_ref, x2_ref, x2p_ref, sc_ref):
    x1 = x_ref[...]
    q = jnp.dot(x1.astype(BF16), wq_ref[...], preferred_element_type=F32).astype(BF16)
    outs = []
    for h in range(XA_HEADS):
        qh = q[:, h * XA_DH:(h + 1) * XA_DH]
        kh = kv_ref[:, h * XA_DH:(h + 1) * XA_DH]
        vh = kv_ref[:, D_MODEL + h * XA_DH:D_MODEL + (h + 1) * XA_DH]
        s = lax.dot_general(qh, kh, _DN_T, preferred_element_type=F32) * (XA_DH ** -0.5)
        p = jnp.exp(s - jnp.max(s, axis=-1, keepdims=True))
        p = p / jnp.sum(p, axis=-1, keepdims=True)
        outs.append(jnp.dot(p.astype(BF16), vh, preferred_element_type=F32).astype(BF16))
    o = jnp.concatenate(outs, axis=1)
    xa = jnp.dot(o, wo_ref[...], preferred_element_type=F32)
    x2 = _ln_rows(ALPHA * x1 + xa, g_ref[...], b_ref[...])
    x2_ref[...] = x2
    x2p_ref[...] = _pack_bf16_pairs(x2)
    xh = x2.astype(BF16)
    xl = (x2 - xh.astype(F32)).astype(BF16)
    wh = rw_ref[0]
    wl = rw_ref[1]
    logit = (lax.dot_general(wh, xh, _DN_T, preferred_element_type=F32) +
             lax.dot_general(wh, xl, _DN_T, preferred_element_type=F32) +
             lax.dot_general(wl, xh, _DN_T, preferred_element_type=F32))
    sc_ref[...] = jax.nn.sigmoid(logit)


def _xattn(x1, kv, wq, wo, g, b, router_w, S, tq):
    T = x1.shape[0]
    wqb, wob = wq.astype(BF16), wo.astype(BF16)
    rwt = router_w.T
    rh = rwt.astype(BF16)
    rw = jnp.stack([rh, (rwt - rh.astype(F32)).astype(BF16)])
    g2, b2 = g.reshape(1, -1), b.reshape(1, -1)
    full = lambda a: pl.BlockSpec(a.shape, lambda i: (0,) * a.ndim)
    per = S // tq
    return pl.pallas_call(
        _xattn_kernel, grid=(T // tq,),
        in_specs=[pl.BlockSpec((tq, D_MODEL), lambda i: (i, 0)),
                  pl.BlockSpec((MEM_LEN, 2 * D_MODEL), lambda i: (i // per, 0)),
                  full(wqb), full(wob), full(g2), full(b2), full(rw)],
        out_specs=(pl.BlockSpec((tq, D_MODEL), lambda i: (i, 0)),
                   pl.BlockSpec((tq, D_MODEL // 2), lambda i: (i, 0)),
                   pl.BlockSpec((MOE_E, tq), lambda i: (0, i))),
        out_shape=(jax.ShapeDtypeStruct((T, D_MODEL), F32), jax.ShapeDtypeStruct((T, D_MODEL // 2), I32),
                   jax.ShapeDtypeStruct((MOE_E, T), F32)),
        compiler_params=_params(("parallel",)), name="xattn")(x1, kv, wqb, wob, g2, b2, rw)


def _route_kernel(sc_ref, rb_ref, idx_ref, w_ref):
    E, G = MOE_E, MOE_GROUPS
    per = E // G
    scores = sc_ref[...]
    tr = scores.shape[1]
    biased = scores + rb_ref[...]
    g3 = biased.reshape(G, per, tr)
    j3 = lax.broadcasted_iota(I32, (G, per, tr), 1)
    m1 = jnp.max(g3, axis=1, keepdims=True)
    first = jnp.min(jnp.where(g3 == m1, j3, per), axis=1, keepdims=True)
    m2 = jnp.max(jnp.where(j3 == first, -jnp.inf, g3), axis=1, keepdims=True)
    gs = (m1 + m2).reshape(G, tr)
    gi = lax.broadcasted_iota(I32, (G, tr), 0)
    cnt = jnp.zeros((G, tr), F32)
    for m in range(G):
        row = gs[m:m + 1, :]
        cnt = cnt + jnp.where((row > gs) | ((row == gs) & (gi > m)), 1.0, 0.0)
    gmask = cnt < float(MOE_TOPK_GROUPS)
    masked = jnp.where(gmask[:, None, :], g3, NEG).reshape(E, tr)
    ei = lax.broadcasted_iota(I32, (E, tr), 0)
    idxs, ws = [], []
    for _ in range(MOE_K):
        mx = jnp.max(masked, axis=0, keepdims=True)
        ix = jnp.min(jnp.where(masked == mx, ei, E), axis=0, keepdims=True)
        hit = ei == ix
        ws.append(jnp.sum(jnp.where(hit, scores, 0.0), axis=0, keepdims=True))
        idxs.append(ix)
        masked = jnp.where(hit, -jnp.inf, masked)
    w = jnp.concatenate(ws, axis=0)
    idx_ref[...] = jnp.concatenate(idxs, axis=0)
    w_ref[...] = w / jnp.sum(w, axis=0, keepdims=True) * MOE_ROUTE_SCALE


def _route(scores_t, router_bias, tr):
    E, T = scores_t.shape
    rb = router_bias.reshape(E, 1)
    return pl.pallas_call(
        _route_kernel, grid=(T // tr,),
        in_specs=[pl.BlockSpec((E, tr), lambda i: (0, i)), pl.BlockSpec((E, 1), lambda i: (0, 0))],
        out_specs=(pl.BlockSpec((MOE_K, tr), lambda i: (0, i)), pl.BlockSpec((MOE_K, tr), lambda i: (0, i))),
        out_shape=(jax.ShapeDtypeStruct((MOE_K, T), I32), jax.ShapeDtypeStruct((MOE_K, T), F32)),
        compiler_params=_params(("parallel",)), name="route")(scores_t, rb)


def _rank_kernel(idx_ref, u_ref, rank_ref, cnt_ref, carry):
    E = MOE_E

    @pl.when(pl.program_id(0) == 0)
    def _():
        carry[...] = jnp.zeros_like(carry)

    idx = idx_ref[...]
    tp = idx.shape[1]
    ei = lax.broadcasted_iota(I32, (E, tp), 0)
    hits = [ei == idx[k:k + 1, :] for k in range(MOE_K)]
    onehot = jnp.zeros((E, tp), F32)
    for hit in hits:
        onehot = onehot + jnp.where(hit, 1.0, 0.0)
    pos = jnp.dot(onehot.astype(BF16), u_ref[...], preferred_element_type=F32) + carry[...]
    ranks = [jnp.sum(jnp.where(hit, pos, 0.0), axis=0, keepdims=True) for hit in hits]
    rank_ref[...] = jnp.concatenate(ranks, axis=0).astype(I32)
    total = carry[...] + jnp.sum(onehot, axis=1, keepdims=True)
    carry[...] = total
    cnt_ref[...] = jnp.broadcast_to(total, cnt_ref.shape).astype(I32)


def _rank(idx_t, tp):
    K, T = idx_t.shape
    u = jnp.triu(jnp.ones((tp, tp), F32), k=1).astype(BF16)
    rank, cnt = pl.pallas_call(
        _rank_kernel, grid=(T // tp,),
        in_specs=[pl.BlockSpec((K, tp), lambda i: (0, i)), pl.BlockSpec((tp, tp), lambda i: (0, 0))],
        out_specs=(pl.BlockSpec((K, tp), lambda i: (0, i)), pl.BlockSpec((MOE_E, LANES), lambda i: (0, 0))),
        out_shape=(jax.ShapeDtypeStruct((K, T), I32), jax.ShapeDtypeStruct((MOE_E, LANES), I32)),
        scratch_shapes=[pltpu.VMEM((MOE_E, 1), F32)],
        compiler_params=_params(("arbitrary",)), name="rank")(idx_t, u)
    return rank, cnt[:, 0]


def _dest_kernel(idx_ref, rank_ref, po_ref, dest_ref):
    idx = idx_ref[...]
    tp = idx.shape[1]
    ei = lax.broadcasted_iota(I32, (MOE_E, tp), 0)
    po = po_ref[...]
    base = [jnp.sum(jnp.where(ei == idx[k:k + 1, :], po, 0.0), axis=0, keepdims=True) for k in range(MOE_K)]
    dest_ref[...] = jnp.concatenate(base, axis=0).astype(I32) + rank_ref[...]


def _dest(idx_t, rank_t, poffs, tp):
    K, T = idx_t.shape
    po = poffs.astype(F32).reshape(MOE_E, 1)
    spec = pl.BlockSpec((K, tp), lambda i: (0, i))
    return pl.pallas_call(
        _dest_kernel, grid=(T // tp,),
        in_specs=[spec, spec, pl.BlockSpec((MOE_E, 1), lambda i: (0, 0))],
        out_specs=spec, out_shape=jax.ShapeDtypeStruct((K, T), I32),
        compiler_params=_params(("parallel",)), name="dest")(idx_t, rank_t, po)


def _pack_bf16_pairs(v):
    m = v.shape[1] // 2
    bits = lax.bitcast_convert_type(v.astype(BF16).astype(F32), jnp.uint32)
    return lax.bitcast_convert_type((bits[:, :m] >> 16) | (bits[:, m:] & jnp.uint32(0xFFFF0000)), I32)


def _unpack_bf16_pairs(w):
    w = lax.bitcast_convert_type(w, jnp.uint32)
    lo = lax.bitcast_convert_type(w << 16, F32)
    hi = lax.bitcast_convert_type(w & jnp.uint32(0xFFFF0000), F32)
    return lo, hi


def _row_scatter(rows, dest_t, n_rows):
    T, d = rows.shape
    K = dest_t.shape[0]
    nw = SC_CORES * SC_SUBCORES
    per_w = T // nw
    ch = SC_GATHER_ROWS
    mesh = plsc.VectorSubcoreMesh(core_axis_name="c", subcore_axis_name="s")

    @functools.partial(
        pl.kernel, mesh=mesh, out_type=jax.ShapeDtypeStruct((n_rows, d), I32),
        scratch_types=[pltpu.VMEM((ch,), I32), pltpu.VMEM((ch, d), I32), pltpu.SemaphoreType.DMA],
        name="row_scatter")
    def scatter(rows_hbm, dest_hbm, out_hbm, idx_v, rows_v, sem):
        wid = lax.axis_index("s") * SC_CORES + lax.axis_index("c")
        base = wid * per_w

        @pl.loop(0, per_w // ch)
        def _(i):
            off = pl.multiple_of(base + i * ch, ch)
            pltpu.sync_copy(rows_hbm.at[pl.ds(off, ch)], rows_v)
            for k in range(K):
                pltpu.sync_copy(dest_hbm.at[k, pl.ds(off, ch)], idx_v)
                pltpu.async_copy(rows_v, out_hbm.at[idx_v], sem).wait()

    return scatter(rows, dest_t)


def _expert_kernel(po_ref, cnt_ref, xs_hbm, w1_ref, w3_ref, w2_ref, ys_hbm,
                   xbuf, ybuf, w1b, w3b, w2b, gcnt, insem, outsem):
    e = pl.program_id(0)
    ne = pl.num_programs(0)
    n = cnt_ref[e]
    base = po_ref[e]
    rb = xbuf.shape[1]
    hw = D_MODEL // 2
    nblk = jnp.maximum((n + rb - 1) // rb, 1)
    w1b[...] = w1_ref[...].astype(BF16)
    w3b[...] = w3_ref[...].astype(BF16)
    w2b[...] = w2_ref[...].astype(BF16)

    def in_copy(r0, slot):
        return pltpu.make_async_copy(xs_hbm.at[pl.ds(pl.multiple_of(r0, rb), rb)], xbuf.at[slot], insem.at[slot])

    def out_copy(r0, slot):
        return pltpu.make_async_copy(ybuf.at[slot], ys_hbm.at[pl.ds(pl.multiple_of(r0, rb), rb)], outsem.at[slot])

    @pl.when(e == 0)
    def _():
        gcnt[0] = 0
        in_copy(base, 0).start()

    g0 = gcnt[0]

    def body(j, c):
        slot = (g0 + j) % 2

        @pl.when(j + 1 < nblk)
        def _():
            in_copy(base + (j + 1) * rb, 1 - slot).start()

        @pl.when((j + 1 == nblk) & (e + 1 < ne))
        def _():
            in_copy(po_ref[jnp.minimum(e + 1, ne - 1)], 1 - slot).start()

        in_copy(0, slot).wait()

        @pl.when(g0 + j >= 2)
        def _():
            out_copy(0, slot).wait()

        words = xbuf[slot]
        row = j * rb + lax.broadcasted_iota(I32, words.shape, 0)
        lo, hi = _unpack_bf16_pairs(jnp.where(row < n, words, 0))
        lo, hi = lo.astype(BF16), hi.astype(BF16)
        a = (jnp.dot(lo, w1b[0:hw, :], preferred_element_type=F32) +
             jnp.dot(hi, w1b[hw:, :], preferred_element_type=F32))
        g = (jnp.dot(lo, w3b[0:hw, :], preferred_element_type=F32) +
             jnp.dot(hi, w3b[hw:, :], preferred_element_type=F32))
        h = (a * jax.nn.sigmoid(a) * g).astype(BF16)
        ybuf[slot] = _pack_bf16_pairs(jnp.dot(h, w2b[...], preferred_element_type=F32))
        out_copy(base + j * rb, slot).start()
        return c

    lax.fori_loop(0, nblk, body, 0)
    total = g0 + nblk
    gcnt[0] = total

    @pl.when(e + 1 == ne)
    def _():
        @pl.when(total >= 2)
        def _():
            out_copy(0, total % 2).wait()

        out_copy(0, (total - 1) % 2).wait()


def _experts(poffs, counts, xs, w1, w3, w2):
    n_rows, wp = xs.shape
    rb = ROW_BLOCK
    wspec = lambda shape: pl.BlockSpec((None,) + shape, lambda e, po, cn: (e, 0, 0))
    gs = pltpu.PrefetchScalarGridSpec(
        num_scalar_prefetch=2, grid=(MOE_E,),
        in_specs=[pl.BlockSpec(memory_space=pl.ANY), wspec((D_MODEL, MOE_FF)), wspec((D_MODEL, MOE_FF)),
                  wspec((MOE_FF, D_MODEL))],
        out_specs=pl.BlockSpec(memory_space=pl.ANY),
        scratch_shapes=[pltpu.VMEM((2, rb, wp), I32), pltpu.VMEM((2, rb, wp), I32),
                        pltpu.VMEM((D_MODEL, MOE_FF), BF16), pltpu.VMEM((D_MODEL, MOE_FF), BF16),
                        pltpu.VMEM((MOE_FF, D_MODEL), BF16), pltpu.SMEM((1,), I32),
                        pltpu.SemaphoreType.DMA((2,)), pltpu.SemaphoreType.DMA((2,))])
    return pl.pallas_call(
        _expert_kernel, grid_spec=gs, out_shape=jax.ShapeDtypeStruct((n_rows, wp), I32),
        compiler_params=_params(("arbitrary",)), name="experts")(poffs, counts, xs, w1, w3, w2)


def _row_gather(table, idx):
    n, d = idx.shape[0], table.shape[1]
    nw = SC_CORES * SC_SUBCORES
    per_w = n // nw
    ch = SC_GATHER_ROWS
    mesh = plsc.VectorSubcoreMesh(core_axis_name="c", subcore_axis_name="s")

    @functools.partial(
        pl.kernel, mesh=mesh, out_type=jax.ShapeDtypeStruct((n, d), I32),
        scratch_types=[pltpu.VMEM((ch,), I32), pltpu.VMEM((ch, d), I32), pltpu.SemaphoreType.DMA],
        name="row_gather")
    def gather(table_hbm, idx_hbm, out_hbm, idx_v, rows_v, sem):
        wid = lax.axis_index("s") * SC_CORES + lax.axis_index("c")
        base = wid * per_w

        @pl.loop(0, per_w // ch)
        def _(i):
            off = pl.multiple_of(base + i * ch, ch)
            pltpu.sync_copy(idx_hbm.at[pl.ds(off, ch)], idx_v)
            pltpu.async_copy(table_hbm.at[idx_v], rows_v, sem).wait()
            pltpu.sync_copy(rows_v, out_hbm.at[pl.ds(off, ch)])

    return gather(table, idx)


def _combine_kernel(w_ref, tile_ref, x_ref, ysg_ref, s1_ref, s3_ref, s2_ref, g_ref, b_ref, o_ref):
    x2 = x_ref[...]
    xb = x2.astype(BF16)
    a = jnp.dot(xb, s1_ref[...], preferred_element_type=F32)
    c = jnp.dot(xb, s3_ref[...], preferred_element_type=F32)
    shared = jnp.dot((a * jax.nn.sigmoid(a) * c).astype(BF16), s2_ref[...], preferred_element_type=F32)
    lo, hi = _unpack_bf16_pairs(ysg_ref[...])
    lo, hi = lo.astype(BF16), hi.astype(BF16)
    wrep = jnp.dot(w_ref[...], tile_ref[...], precision=lax.Precision.HIGHEST, preferred_element_type=F32)
    col = lax.broadcasted_iota(I32, wrep.shape, 1)
    row = lax.broadcasted_iota(I32, wrep.shape, 0)
    wsel = jnp.where(col // MOE_K == row, wrep, 0.0)
    wh = wsel.astype(BF16)
    wl = (wsel - wh.astype(F32)).astype(BF16)
    y_lo = jnp.dot(wh, lo, preferred_element_type=F32) + jnp.dot(wl, lo, preferred_element_type=F32)
    y_hi = jnp.dot(wh, hi, preferred_element_type=F32) + jnp.dot(wl, hi, preferred_element_type=F32)
    y = shared + jnp.concatenate([y_lo, y_hi], axis=1)
    o_ref[...] = _ln_rows(ALPHA * x2 + y, g_ref[...], b_ref[...])


def _combine(w_nat, x2, ysg, sw1, sw3, sw2, g, b, tc):
    T = x2.shape[0]
    wp = ysg.shape[1]
    full = lambda a: pl.BlockSpec(a.shape, lambda i: (0,) * a.ndim)
    s1, s3, s2 = sw1.astype(BF16), sw3.astype(BF16), sw2.astype(BF16)
    g2, b2 = g.reshape(1, -1), b.reshape(1, -1)
    tile = jnp.asarray(np.arange(tc * MOE_K)[None, :] % MOE_K == np.arange(MOE_K)[:, None], F32)
    return pl.pallas_call(
        _combine_kernel, grid=(T // tc,),
        in_specs=[pl.BlockSpec((tc, MOE_K), lambda i: (i, 0)), full(tile),
                  pl.BlockSpec((tc, D_MODEL), lambda i: (i, 0)),
                  pl.BlockSpec((tc * MOE_K, wp), lambda i: (i, 0)),
                  full(s1), full(s3), full(s2), full(g2), full(b2)],
        out_specs=pl.BlockSpec((tc, D_MODEL), lambda i: (i, 0)),
        out_shape=jax.ShapeDtypeStruct((T, D_MODEL), F32),
        compiler_params=_params(("parallel",)), name="combine",
    )(w_nat, tile, x2, ysg, s1, s3, s2, g2, b2)


def _moe(x2, x2p, scores_t, router_bias, w1, w3, w2, sw1, sw3, sw2, g, b, tiles):
    T = x2.shape[0]
    rb = ROW_BLOCK
    idx_t, w_t = _route(scores_t, router_bias, tiles['route'])
    rank_t, counts = _rank(idx_t, tiles['rank'])
    pcounts = jnp.maximum((counts + rb - 1) // rb, 1) * rb
    pends = jnp.cumsum(pcounts)
    poffs = (pends - pcounts).astype(I32)
    n_blocks = -(-T * MOE_K // rb) + MOE_E
    dest_t = _dest(idx_t, rank_t, poffs, tiles['rank'])
    xs = _row_scatter(x2p, dest_t, n_blocks * rb)
    ys = _experts(poffs, counts.astype(I32), xs, w1, w3, w2)
    ysg = _row_gather(ys, dest_t.T.reshape(-1))
    return _combine(w_t.T, x2, ysg, sw1, sw3, sw2, g, b, tiles['combine'])


def _tiles(B, S):
    T = B * S
    pick = lambda want, n: want if n % want == 0 else n
    return dict(proj=pick(512, T), nsa_q=pick(128, S), nsa_ck=pick(512, S), outproj=pick(512, T),
                mlstm_nb=2 if B % 2 == 0 else 1, xattn=pick(256, S), route=pick(512, T), rank=pick(512, T), scatter=pick(256, T),
                combine=pick(128, T))


def kernel(x, mem, w_in, nsa_pos_k, nsa_cmp_k_w1, nsa_cmp_k_w2, nsa_pos_v, nsa_cmp_v_w1, nsa_cmp_v_w2,
           mlstm_conv_w, mlstm_i_bias, mlstm_f_bias, mlstm_norm_g, w_out, ln1_g, ln1_b,
           xa_wq, xa_wk, xa_wv, xa_wo, ln2_g, ln2_b, router_w, router_bias,
           moe_w1, moe_w3, moe_w2, shared_w1, shared_w3, shared_w2, ln3_g, ln3_b):
    B, S, D = x.shape
    T = B * S
    tl = _tiles(B, S)
    xc = x.reshape(T, D)
    memc = mem.reshape(B * MEM_LEN, D)
    for l in range(w_in.shape[0]):
        (q, cmp, ksel, vsel, kwin, vwin, gates, mq, mk, mv, mo, mif) = _project(xc, _prep_w_in(w_in[l]), tl['proj'])
        kcvc = _compress(cmp, B, S, _prep_cmp(nsa_pos_k[l], nsa_cmp_k_w1[l], nsa_cmp_k_w2[l],
                                              nsa_pos_v[l], nsa_cmp_v_w1[l], nsa_cmp_v_w2[l]))
        y_nsa = _nsa(q, kcvc, ksel, vsel, kwin, vwin, gates, B, S, tl['nsa_q'], tl['nsa_ck'])
        y_ml = _mlstm(mq, mk, mv, mo, mif, mlstm_conv_w[l], mlstm_i_bias[l], mlstm_f_bias[l],
                      mlstm_norm_g[l], B, S, tl['mlstm_nb'])
        x1 = _outproj(y_nsa, y_ml, xc, w_out[l], ln1_g[l], ln1_b[l], tl['outproj'])
        kv = _memkv(memc, xa_wk[l], xa_wv[l])
        x2, x2p, scores_t = _xattn(x1, kv, xa_wq[l], xa_wo[l], ln2_g[l], ln2_b[l], router_w[l], S, tl['xattn'])
        xc = _moe(x2, x2p, scores_t, router_bias[l], moe_w1[l], moe_w3[l], moe_w2[l],
                  shared_w1[l], shared_w3[l], shared_w2[l], ln3_g[l], ln3_b[l], tl)
    return xc.reshape(B, S, D)
```

```python
import functools
import numpy as np
import jax
import jax.numpy as jnp
from jax import lax
from jax.experimental import pallas as pl
from jax.experimental.pallas import tpu as pltpu
from jax.experimental.pallas import tpu_sc as plsc

F32 = jnp.float32
BF16 = jnp.bfloat16
I32 = jnp.int32

D_MODEL = 1024
MEM_LEN = 256
NSA_HEADS = 8
NSA_GROUPS = 2
NSA_HPG = 4
NSA_DK = 64
NSA_CMP_LEN = 32
NSA_CMP_STRIDE = 16
NSA_SEL_BLOCK = 64
NSA_SEL_TOPN = 8
NSA_WINDOW = 512
ML_HEADS = 4
ML_DH = 128
ML_CHUNK = 64
ML_CONV = 4
XA_HEADS = 4
XA_DH = 256
MOE_E = 256
MOE_K = 8
MOE_GROUPS = 8
MOE_TOPK_GROUPS = 4
MOE_FF = 256
MOE_ROUTE_SCALE = 2.5
DEPTH = 1
ALPHA = (2.0 * DEPTH) ** 0.25
LN_EPS = 1e-5
NEG = -1e30
FORCE_BONUS = 1e4

LANES = 128
ROW_BLOCK = 512
VMEM_LIMIT = 56 * 1024 * 1024
SC_CORES = 2
SC_SUBCORES = 16
SC_GATHER_ROWS = 128

_DN_T = (((1,), (1,)), ((), ()))
_DN_TA = (((0,), (0,)), ((), ()))


def _params(sem):
    return pltpu.CompilerParams(dimension_semantics=sem, vmem_limit_bytes=VMEM_LIMIT)


def _ln_rows(v, g, b):
    mu = jnp.mean(v, axis=-1, keepdims=True)
    d = v - mu
    var = jnp.mean(d * d, axis=-1, keepdims=True)
    return d * lax.rsqrt(var + LN_EPS) * g + b


_SEGS = (('q', 512, BF16), ('cmp', 256, F32), ('ksel', 128, BF16), ('vsel', 128, BF16),
         ('kwin', 128, BF16), ('vwin', 128, BF16), ('gates', 128, F32), ('mq', 512, BF16),
         ('mk', 512, BF16), ('mv', 512, BF16), ('mo', 512, BF16), ('mif', 128, F32))


def _proj_kernel(x_ref, w_ref, *out_refs):
    xb = x_ref[...].astype(BF16)
    off = 0
    for o_ref, (_, wd, _) in zip(out_refs, _SEGS):
        o_ref[...] = jnp.dot(xb, w_ref[:, off:off + wd], preferred_element_type=F32).astype(o_ref.dtype)
        off += wd


def _prep_w_in(w):
    sizes = (512,) + (128,) * 6 + (24,) + (512,) * 4 + (4, 4)
    cuts = np.cumsum(sizes)[:-1].tolist()
    (wq, kc, vc, ks, vs, kw, vw, wg, mq, mk, mv, mo, mi, mf) = jnp.split(w, cuts, axis=1)
    wq = wq.reshape(D_MODEL, NSA_GROUPS, NSA_HPG, NSA_DK).transpose(0, 2, 1, 3).reshape(D_MODEL, 512)
    pad = lambda a: jnp.pad(a, ((0, 0), (0, LANES - a.shape[1])))
    segs = [wq, kc, vc, ks, vs, kw, vw, pad(wg), mq, mk, mv, mo, pad(jnp.concatenate([mi, mf], axis=1))]
    return jnp.concatenate(segs, axis=1).astype(BF16)


def _project(x2d, w_all, tm):
    T = x2d.shape[0]
    n = w_all.shape[1]
    out_shape = tuple(jax.ShapeDtypeStruct((T, wd), dt) for _, wd, dt in _SEGS)
    out_specs = tuple(pl.BlockSpec((tm, wd), lambda i: (i, 0)) for _, wd, _ in _SEGS)
    return pl.pallas_call(
        _proj_kernel, grid=(T // tm,),
        in_specs=[pl.BlockSpec((tm, D_MODEL), lambda i: (i, 0)),
                  pl.BlockSpec((D_MODEL, n), lambda i: (0, 0))],
        out_specs=out_specs, out_shape=out_shape,
        compiler_params=_params(("parallel",)), name="proj")(x2d, w_all)


def _cmp_kernel(r_ref, pa_ref, pb_ref, wa_ref, wb_ref, w2_ref, o_ref):
    r = r_ref[...]
    a = jnp.dot((r + pa_ref[...]).astype(BF16), wa_ref[...], preferred_element_type=F32)
    b = jnp.dot((r + pb_ref[...]).astype(BF16), wb_ref[...], preferred_element_type=F32)
    nr = r.shape[0]
    hid = a + pltpu.roll(b, nr - 1, 0)
    hid = hid * jax.nn.sigmoid(hid)
    out = jnp.dot(hid.astype(BF16), w2_ref[...], preferred_element_type=F32)
    row = lax.broadcasted_iota(I32, out.shape, 0)
    o_ref[...] = jnp.where(row < nr - 1, out, 0.0).astype(o_ref.dtype)


def _prep_cmp(pos_k, w1_k, w2_k, pos_v, w1_v, w2_v):
    eye = jnp.eye(NSA_GROUPS, dtype=F32)

    def expand_w1(w1, half):
        w = w1.reshape(NSA_CMP_LEN, NSA_DK, NSA_DK)[half * 16:(half + 1) * 16]
        return jnp.einsum('jde,gk->jgdke', w, eye).reshape(16, 128, 128)

    def both(fk, fv):
        z = jnp.zeros_like(fk)
        top = jnp.concatenate([fk, z], axis=-1)
        bot = jnp.concatenate([z, fv], axis=-1)
        return jnp.concatenate([top, bot], axis=-2)

    wa = both(expand_w1(w1_k, 0), expand_w1(w1_v, 0)).reshape(16 * 256, 256).astype(BF16)
    wb = both(expand_w1(w1_k, 1), expand_w1(w1_v, 1)).reshape(16 * 256, 256).astype(BF16)
    w2 = both(jnp.kron(eye, w2_k), jnp.kron(eye, w2_v)).astype(BF16)

    def pos_row(half):
        pk = jnp.tile(pos_k[half * 16:(half + 1) * 16], (1, NSA_GROUPS))
        pv = jnp.tile(pos_v[half * 16:(half + 1) * 16], (1, NSA_GROUPS))
        return jnp.concatenate([pk, pv], axis=1).reshape(1, 16 * 256)

    return pos_row(0), pos_row(1), wa, wb, w2


def _compress(cmp2d, B, S, prep):
    pa, pb, wa, wb, w2 = prep
    nr = S // NSA_CMP_STRIDE
    r = cmp2d.reshape(B, nr, NSA_CMP_STRIDE * 256)
    full = lambda a: pl.BlockSpec(a.shape, lambda b: (0,) * a.ndim)
    return pl.pallas_call(
        _cmp_kernel, grid=(B,),
        in_specs=[pl.BlockSpec((None, nr, NSA_CMP_STRIDE * 256), lambda b: (b, 0, 0)),
                  full(pa), full(pb), full(wa), full(wb), full(w2)],
        out_specs=pl.BlockSpec((None, nr, 256), lambda b: (b, 0, 0)),
        out_shape=jax.ShapeDtypeStruct((B, nr, 256), BF16),
        compiler_params=_params(("parallel",)), name="cmp")(r, pa, pb, wa, wb, w2)


def _nsa_consts(S):
    n_cmp = (S - NSA_CMP_LEN) // NSA_CMP_STRIDE + 1
    n_sel = S // NSA_SEL_BLOCK
    cs = np.arange(n_cmp) * NSA_CMP_STRIDE
    ss = np.arange(n_sel) * NSA_SEL_BLOCK
    ov = ((cs[:, None] < ss[None, :] + NSA_SEL_BLOCK) & (cs[:, None] + NSA_CMP_LEN > ss[None, :]))
    ovt = np.zeros((LANES, S // NSA_CMP_STRIDE), np.float32)
    ovt[:n_sel, :n_cmp] = ov.T
    e = np.zeros((LANES, S), np.float32)
    e[np.arange(S) // NSA_SEL_BLOCK, np.arange(S)] = 1.0
    return jnp.asarray(ovt, BF16), jnp.asarray(e, BF16)


def _nsa_kernel(q_ref, kcvc_ref, ksel_ref, vsel_ref, kwin_ref, vwin_ref, gates_ref, ovt_ref, e_ref,
                y_ref, bias_ref, *, tq, ck, n_sel):
    G, H = NSA_GROUPS, NSA_HPG
    GH = G * H
    M = GH * tq
    S = ksel_ref.shape[0]
    W = NSA_WINDOW
    ws = min(W + tq, S)
    t0 = pl.program_id(1) * tq
    gates = jax.nn.sigmoid(gates_ref[...])
    lane = lax.broadcasted_iota(I32, (tq, LANES), 1)
    t_col = t0 + lax.broadcasted_iota(I32, (tq, 1), 0)
    kc = kcvc_ref[:, 0:LANES]
    vc = kcvc_ref[:, LANES:2 * LANES]
    nc = kc.shape[0]
    qs = []
    for g in range(G):
        gmask = (lane // NSA_DK) == g
        for h in range(H):
            qh = q_ref[:, h * LANES:(h + 1) * LANES] * (NSA_DK ** -0.5)
            qs.append(jnp.where(gmask, qh, jnp.zeros_like(qh)))
    Q = jnp.concatenate(qs, axis=0).astype(BF16)

    s = lax.dot_general(Q, kc, _DN_T, preferred_element_type=F32)
    c_idx = lax.broadcasted_iota(I32, (tq, nc), 1)
    cmask = (c_idx * NSA_CMP_STRIDE + NSA_CMP_LEN - 1) <= t_col
    s3 = jnp.where(cmask[None], s.reshape(GH, tq, nc), NEG)
    p = jnp.exp(s3 - jnp.max(s3, axis=-1, keepdims=True))
    p = p / jnp.sum(p, axis=-1, keepdims=True)
    p = jnp.where(cmask[None], p, 0.0)
    o_cmp = jnp.dot(p.reshape(M, nc).astype(BF16), vc, preferred_element_type=F32).reshape(GH, tq, LANES)

    for g in range(G):
        psum = jnp.sum(p[g * H:(g + 1) * H], axis=0)
        hi = psum.astype(BF16)
        lo = (psum - hi.astype(F32)).astype(BF16)
        ovt = ovt_ref[...]
        pslt = (lax.dot_general(ovt, hi, _DN_T, preferred_element_type=F32) +
                lax.dot_general(ovt, lo, _DN_T, preferred_element_type=F32))
        imp_p = pslt[0:n_sel, :]
        n_i = lax.broadcasted_iota(I32, (n_sel, tq), 0)
        cur = (t0 + lax.broadcasted_iota(I32, (n_sel, tq), 1)) // NSA_SEL_BLOCK
        forced = (n_i == 0) | (n_i == cur) | (n_i == cur - 1)
        imp = jnp.where(n_i <= cur, imp_p + jnp.where(forced, FORCE_BONUS, 0.0), NEG)
        cnt = jnp.zeros((n_sel, tq), F32)
        for m in range(n_sel):
            row = imp[m:m + 1, :]
            beats = (row > imp) | ((row == imp) & (n_i > m))
            cnt = cnt + jnp.where(beats, 1.0, 0.0)
        selt = jnp.where(cnt < float(min(NSA_SEL_TOPN, n_sel)), 1.0, 0.0)
        selt = jnp.concatenate([selt, jnp.zeros((LANES - n_sel, tq), F32)], axis=0)
        sel = selt.T.astype(BF16)
        maskf = jnp.dot(sel, e_ref[...], preferred_element_type=F32)
        kpos = lax.broadcasted_iota(I32, (tq, S), 1)
        bias_ref[g] = jnp.where((maskf > 0.5) & (kpos <= t_col), 0.0, NEG)

    vlane = lax.broadcasted_iota(I32, (1, LANES), 1) // NSA_DK

    def pv_with_sums(pb, v):
        outs = []
        for g in range(G):
            vg = jnp.where(vlane == g, v, jnp.ones_like(v))
            outs.append(jnp.dot(pb[g * H * tq:(g + 1) * H * tq], vg, preferred_element_type=F32))
        return jnp.concatenate(outs, axis=0)

    def normalise(acc):
        outs = []
        for r in range(GH):
            c = NSA_DK * (1 - r // H)
            outs.append(acc[r] / acc[r][:, c:c + 1])
        return outs

    def sel_body(j, carry):
        m_i, acc = carry
        ks = pl.multiple_of(j * ck, ck)
        k = ksel_ref[pl.ds(ks, ck), :]
        v = vsel_ref[pl.ds(ks, ck), :]
        sj = lax.dot_general(Q, k, _DN_T, preferred_element_type=F32)
        sj = (sj.reshape(G, H, tq, ck) + bias_ref[:, :, pl.ds(ks, ck)][:, None]).reshape(GH, tq, ck)
        m_new = jnp.maximum(m_i, jnp.max(sj, axis=-1, keepdims=True))
        a = jnp.exp(m_i - m_new)
        pj = jnp.exp(sj - m_new).reshape(M, ck).astype(BF16)
        return m_new, a * acc + pv_with_sums(pj, v).reshape(GH, tq, LANES)

    init = (jnp.full((GH, tq, 1), NEG, F32), jnp.zeros((GH, tq, LANES), F32))
    _, acc = lax.fori_loop(0, (t0 + tq + ck - 1) // ck, sel_body, init)
    o_sel = normalise(acc)

    kst = pl.multiple_of(jnp.clip(t0 - W, 0, S - ws), LANES)
    kwn = kwin_ref[pl.ds(kst, ws), :]
    vwn = vwin_ref[pl.ds(kst, ws), :]
    sw = lax.dot_general(Q, kwn, _DN_T, preferred_element_type=F32)
    wpos = kst + lax.broadcasted_iota(I32, (tq, ws), 1)
    wmask = (wpos <= t_col) & (wpos > t_col - W)
    sw3 = jnp.where(wmask[None], sw.reshape(GH, tq, ws), NEG)
    pw = jnp.exp(sw3 - jnp.max(sw3, axis=-1, keepdims=True)).reshape(M, ws).astype(BF16)
    o_win = normalise(pv_with_sums(pw, vwn).reshape(GH, tq, LANES))

    g0mask = lane < NSA_DK
    for h in range(H):
        o_g = []
        for g in range(G):
            r = g * H + h
            c0 = r * 3
            o_g.append(gates[:, c0:c0 + 1] * o_cmp[r] + gates[:, c0 + 1:c0 + 2] * o_sel[r] +
                       gates[:, c0 + 2:c0 + 3] * o_win[r])
        y_ref[:, h * LANES:(h + 1) * LANES] = jnp.where(g0mask, o_g[0], o_g[1]).astype(y_ref.dtype)


def _nsa(q, kcvc, ksel, vsel, kwin, vwin, gates, B, S, tq, ck):
    T = B * S
    nq = S // tq
    ovt, e = _nsa_consts(S)
    seq = lambda a: a.reshape(B, S, LANES)
    kv_spec = pl.BlockSpec((None, S, LANES), lambda b, i: (b, 0, 0))
    kern = functools.partial(_nsa_kernel, tq=tq, ck=ck, n_sel=S // NSA_SEL_BLOCK)
    return pl.pallas_call(
        kern, grid=(B, nq),
        in_specs=[pl.BlockSpec((tq, 512), lambda b, i: (b * nq + i, 0)),
                  pl.BlockSpec((None,) + kcvc.shape[1:], lambda b, i: (b, 0, 0)),
                  kv_spec, kv_spec, kv_spec, kv_spec,
                  pl.BlockSpec((tq, LANES), lambda b, i: (b * nq + i, 0)),
                  pl.BlockSpec(ovt.shape, lambda b, i: (0, 0)),
                  pl.BlockSpec(e.shape, lambda b, i: (0, 0))],
        out_specs=pl.BlockSpec((tq, 512), lambda b, i: (b * nq + i, 0)),
        out_shape=jax.ShapeDtypeStruct((T, 512), BF16),
        scratch_shapes=[pltpu.VMEM((NSA_GROUPS, tq, S), F32)],
        compiler_params=_params(("parallel", "parallel")), name="nsa",
    )(q, kcvc, seq(ksel), seq(vsel), seq(kwin), seq(vwin), gates, ovt, e)


def _mlstm_kernel(q_ref, k_ref, v_ref, o_ref, gn_ref, gt_ref, cw_ref, bn_ref, bt_ref, ng_ref, tri_ref,
                  y_ref, c_scr, n_scr):
    H, dh, L = ML_HEADS, ML_DH, ML_CHUNK
    nb, S = q_ref.shape[0], q_ref.shape[1]
    nchunk = S // L
    c_scr[...] = jnp.zeros_like(c_scr)
    n_scr[...] = jnp.zeros_like(n_scr)
    row = lax.broadcasted_iota(I32, (L, H * dh), 0)
    li = lax.broadcasted_iota(I32, (L, L), 0)
    mi = lax.broadcasted_iota(I32, (L, L), 1)
    causal = mi <= li
    tril = tri_ref[0]
    triu = tri_ref[1]
    hp = lax.Precision.HIGHEST

    def conv_silu(ref, bi, c, wofs):
        r0 = pl.multiple_of(c * L, L)
        rp = pl.multiple_of(jnp.maximum(c - 1, 0) * L, L)
        cur = ref[bi, pl.ds(r0, L), :].astype(F32)
        prev = ref[bi, pl.ds(rp, L), :].astype(F32) * jnp.where(c > 0, 1.0, 0.0)
        acc = cur * cw_ref[ML_CONV - 1:ML_CONV, wofs:wofs + H * dh]
        for j in range(1, ML_CONV):
            sh = jnp.where(row < j, pltpu.roll(prev, j, 0), pltpu.roll(cur, j, 0))
            acc = acc + sh * cw_ref[ML_CONV - 1 - j:ML_CONV - j, wofs:wofs + H * dh]
        return acc * jax.nn.sigmoid(acc)

    def body(c, m_state):
        r0 = pl.multiple_of(c * L, L)
        new_m = []
        for bi in range(nb):
            qa = conv_silu(q_ref, bi, c, 0) * (dh ** -0.5)
            ka = conv_silu(k_ref, bi, c, H * dh)
            va = v_ref[bi, pl.ds(r0, L), :]
            oa = o_ref[bi, pl.ds(r0, L), :].astype(F32)
            gn = gn_ref[bi, pl.ds(r0, L), :] + bn_ref[...]
            gt = gt_ref[bi, :, c, :] + bt_ref[...]
            lf_n = jax.nn.log_sigmoid(gn)
            lf_t = jax.nn.log_sigmoid(gt)
            b_n = jnp.dot(tril, lf_n, precision=hp, preferred_element_type=F32)
            b_t = jnp.dot(lf_t, triu, precision=hp, preferred_element_type=F32)
            for h in range(H):
                st = bi * H + h
                q = qa[:, h * dh:(h + 1) * dh]
                k = ka[:, h * dh:(h + 1) * dh]
                v = va[:, h * dh:(h + 1) * dh]
                m_old = m_state[st]
                b_col = b_n[:, H + h:H + h + 1]
                i_col = gn[:, h:h + 1]
                b_row = b_t[H + h:H + h + 1, :]
                i_row = gt[h:h + 1, :]
                g_tot = b_t[H + h:H + h + 1, L - 1:L]
                d_log = jnp.where(causal, b_col - b_row + i_row, NEG)
                inter = b_col + m_old
                m_q = jnp.maximum(inter, jnp.max(d_log, axis=-1, keepdims=True))
                w_intra = jnp.exp(d_log - m_q)
                w_inter = jnp.exp(inter - m_q)
                qb = q.astype(BF16)
                s = lax.dot_general(qb, k.astype(BF16), _DN_T, preferred_element_type=F32) * w_intra
                cst = c_scr[st]
                nst = n_scr[st]
                num = (w_inter * jnp.dot(qb, cst.astype(BF16), preferred_element_type=F32) +
                       jnp.dot(s.astype(BF16), v, preferred_element_type=F32))
                den = w_inter * jnp.sum(q * nst, axis=-1, keepdims=True) + jnp.sum(s, axis=-1, keepdims=True)
                hv = num / jnp.maximum(jnp.abs(den), jnp.exp(-m_q))
                log_k = g_tot - b_col + i_col
                m_new = jnp.maximum(g_tot + m_old, jnp.max(log_k, axis=0, keepdims=True))
                wk = jnp.exp(log_k - m_new)
                decay = jnp.exp(g_tot + m_old - m_new)
                kw = k * wk
                c_scr[st] = decay * cst + lax.dot_general(kw.astype(BF16), v, _DN_TA, preferred_element_type=F32)
                n_scr[st] = decay * nst + jnp.sum(kw, axis=0, keepdims=True)
                new_m.append(m_new)
                mu = jnp.mean(hv, axis=-1, keepdims=True)
                dv = hv - mu
                var = jnp.mean(dv * dv, axis=-1, keepdims=True)
                hn = dv * lax.rsqrt(var + LN_EPS) * ng_ref[:, h * dh:(h + 1) * dh]
                og = jax.nn.sigmoid(oa[:, h * dh:(h + 1) * dh])
                y_ref[bi, pl.ds(r0, L), h * dh:(h + 1) * dh] = (og * hn).astype(y_ref.dtype)
        return tuple(new_m)

    lax.fori_loop(0, nchunk, body, tuple(jnp.zeros((1, 1), F32) for _ in range(nb * H)))


def _mlstm(mq, mk, mv, mo, mif, conv_w, i_bias, f_bias, norm_g, B, S, nb):
    T = B * S
    H, dh, L = ML_HEADS, ML_DH, ML_CHUNK
    W = H * dh
    gt = mif[:, :2 * H].reshape(B, S, 2 * H).transpose(0, 2, 1).reshape(B, 2 * H, S // L, L)
    cw = conv_w.reshape(ML_CONV, 2 * W)
    bias = jnp.concatenate([i_bias, f_bias])
    bn = jnp.pad(bias, (0, LANES - 2 * H)).reshape(1, LANES)
    bt = bias.reshape(2 * H, 1)
    ng = norm_g.reshape(1, W)
    tri = jnp.stack([jnp.tril(jnp.ones((L, L), F32)), jnp.triu(jnp.ones((L, L), F32))])
    seq = lambda a: a.reshape(B, S, a.shape[1])
    rows = lambda w: pl.BlockSpec((nb, S, w), lambda b: (b, 0, 0))
    full = lambda a: pl.BlockSpec(a.shape, lambda b: (0,) * a.ndim)
    y = pl.pallas_call(
        _mlstm_kernel, grid=(B // nb,),
        in_specs=[rows(W), rows(W), rows(W), rows(W), rows(LANES),
                  pl.BlockSpec((nb, 2 * H, S // L, L), lambda b: (b, 0, 0, 0)),
                  full(cw), full(bn), full(bt), full(ng), full(tri)],
        out_specs=rows(W),
        out_shape=jax.ShapeDtypeStruct((B, S, W), BF16),
        scratch_shapes=[pltpu.VMEM((nb * H, dh, dh), F32), pltpu.VMEM((nb * H, 1, dh), F32)],
        compiler_params=_params(("parallel",)), name="mlstm",
    )(seq(mq), seq(mk), seq(mv), seq(mo), seq(mif), gt, cw, bn, bt, ng, tri)
    return y.reshape(T, W)


def _outproj_kernel(yn_ref, ym_ref, x_ref, w_ref, g_ref, b_ref, o_ref):
    mix = (jnp.dot(yn_ref[...], w_ref[0:512, :], preferred_element_type=F32) +
           jnp.dot(ym_ref[...], w_ref[512:1024, :], preferred_element_type=F32))
    o_ref[...] = _ln_rows(ALPHA * x_ref[...] + mix, g_ref[...], b_ref[...])


def _outproj(y_nsa, y_ml, x2d, w_out, g, b, tm):
    T = x2d.shape[0]
    wn = w_out[:512].reshape(NSA_GROUPS, NSA_HPG, NSA_DK, D_MODEL).transpose(1, 0, 2, 3).reshape(512, D_MODEL)
    w = jnp.concatenate([wn, w_out[512:]], axis=0).astype(BF16)
    row = lambda wd: pl.BlockSpec((tm, wd), lambda i: (i, 0))
    full = lambda a: pl.BlockSpec(a.shape, lambda i: (0,) * a.ndim)
    g2, b2 = g.reshape(1, -1), b.reshape(1, -1)
    return pl.pallas_call(
        _outproj_kernel, grid=(T // tm,),
        in_specs=[row(512), row(512), row(D_MODEL), full(w), full(g2), full(b2)],
        out_specs=row(D_MODEL), out_shape=jax.ShapeDtypeStruct((T, D_MODEL), F32),
        compiler_params=_params(("parallel",)), name="outproj")(y_nsa, y_ml, x2d, w, g2, b2)


def _memkv_kernel(m_ref, w_ref, o_ref):
    o_ref[...] = jnp.dot(m_ref[...].astype(BF16), w_ref[...], preferred_element_type=F32).astype(o_ref.dtype)


def _memkv(mem2d, wk, wv):
    w = jnp.concatenate([wk, wv], axis=1).astype(BF16)
    R = mem2d.shape[0]
    return pl.pallas_call(
        _memkv_kernel, grid=(R // MEM_LEN,),
        in_specs=[pl.BlockSpec((MEM_LEN, D_MODEL), lambda i: (i, 0)),
                  pl.BlockSpec(w.shape, lambda i: (0, 0))],
        out_specs=pl.BlockSpec((MEM_LEN, 2 * D_MODEL), lambda i: (i, 0)),
        out_shape=jax.ShapeDtypeStruct((R, 2 * D_MODEL), BF16),
        compiler_params=_params(("parallel",)), name="memkv")(mem2d, w)


def _xattn_kernel(x_ref, kv_ref, wq_ref, wo_ref, g_ref, b_ref, rw_ref, x2_ref, x2p_ref, sc_ref):
    x1 = x_ref[...]
    q = jnp.dot(x1.astype(BF16), wq_ref[...], preferred_element_type=F32).astype(BF16)
    outs = []
    for h in range(XA_HEADS):
        qh = q[:, h * XA_DH:(h + 1) * XA_DH]
        kh = kv_ref[:, h * XA_DH:(h + 1) * XA_DH]
        vh = kv_ref[:, D_MODEL + h * XA_DH:D_MODEL + (h + 1) * XA_DH]
        s = lax.dot_general(qh, kh, _DN_T, preferred_element_type=F32) * (XA_DH ** -0.5)
        p = jnp.exp(s - jnp.max(s, axis=-1, keepdims=True))
        p = p / jnp.sum(p, axis=-1, keepdims=True)
        outs.append(jnp.dot(p.astype(BF16), vh, preferred_element_type=F32).astype(BF16))
    o = jnp.concatenate(outs, axis=1)
    xa = jnp.dot(o, wo_ref[...], preferred_element_type=F32)
    x2 = _ln_rows(ALPHA * x1 + xa, g_ref[...], b_ref[...])
    x2_ref[...] = x2
    x2p_ref[...] = _pack_bf16_pairs(x2)
    xh = x2.astype(BF16)
    xl = (x2 - xh.astype(F32)).astype(BF16)
    wh = rw_ref[0]
    wl = rw_ref[1]
    logit = (lax.dot_general(wh, xh, _DN_T, preferred_element_type=F32) +
             lax.dot_general(wh, xl, _DN_T, preferred_element_type=F32) +
             lax.dot_general(wl, xh, _DN_T, preferred_element_type=F32))
    sc_ref[...] = jax.nn.sigmoid(logit)


def _xattn(x1, kv, wq, wo, g, b, router_w, S, tq):
    T = x1.shape[0]
    wqb, wob = wq.astype(BF16), wo.astype(BF16)
    rwt = router_w.T
    rh = rwt.astype(BF16)
    rw = jnp.stack([rh, (rwt - rh.astype(F32)).astype(BF16)])
    g2, b2 = g.reshape(1, -1), b.reshape(1, -1)
    full = lambda a: pl.BlockSpec(a.shape, lambda i: (0,) * a.ndim)
    per = S // tq
    return pl.pallas_call(
        _xattn_kernel, grid=(T // tq,),
        in_specs=[pl.BlockSpec((tq, D_MODEL), lambda i: (i, 0)),
                  pl.BlockSpec((MEM_LEN, 2 * D_MODEL), lambda i: (i // per, 0)),
                  full(wqb), full(wob), full(g2), full(b2), full(rw)],
        out_specs=(pl.BlockSpec((tq, D_MODEL), lambda i: (i, 0)),
                   pl.BlockSpec((tq, D_MODEL // 2), lambda i: (i, 0)),
                   pl.BlockSpec((MOE_E, tq), lambda i: (0, i))),
        out_shape=(jax.ShapeDtypeStruct((T, D_MODEL), F32), jax.ShapeDtypeStruct((T, D_MODEL // 2), I32),
                   jax.ShapeDtypeStruct((MOE_E, T), F32)),
        compiler_params=_params(("parallel",)), name="xattn")(x1, kv, wqb, wob, g2, b2, rw)


def _route_kernel(sc_ref, rb_ref, idx_ref, w_ref):
    E, G = MOE_E, MOE_GROUPS
    per = E // G
    scores = sc_ref[...]
    tr = scores.shape[1]
    biased = scores + rb_ref[...]
    g3 = biased.reshape(G, per, tr)
    j3 = lax.broadcasted_iota(I32, (G, per, tr), 1)
    m1 = jnp.max(g3, axis=1, keepdims=True)
    first = jnp.min(jnp.where(g3 == m1, j3, per), axis=1, keepdims=True)
    m2 = jnp.max(jnp.where(j3 == first, -jnp.inf, g3), axis=1, keepdims=True)
    gs = (m1 + m2).reshape(G, tr)
    gi = lax.broadcasted_iota(I32, (G, tr), 0)
    cnt = jnp.zeros((G, tr), F32)
    for m in range(G):
        row = gs[m:m + 1, :]
        cnt = cnt + jnp.where((row > gs) | ((row == gs) & (gi > m)), 1.0, 0.0)
    gmask = cnt < float(MOE_TOPK_GROUPS)
    masked = jnp.where(gmask[:, None, :], g3, NEG).reshape(E, tr)
    ei = lax.broadcasted_iota(I32, (E, tr), 0)
    idxs, ws = [], []
    for _ in range(MOE_K):
        mx = jnp.max(masked, axis=0, keepdims=True)
        ix = jnp.min(jnp.where(masked == mx, ei, E), axis=0, keepdims=True)
        hit = ei == ix
        ws.append(jnp.sum(jnp.where(hit, scores, 0.0), axis=0, keepdims=True))
        idxs.append(ix)
        masked = jnp.where(hit, -jnp.inf, masked)
    w = jnp.concatenate(ws, axis=0)
    idx_ref[...] = jnp.concatenate(idxs, axis=0)
    w_ref[...] = w / jnp.sum(w, axis=0, keepdims=True) * MOE_ROUTE_SCALE


def _route(scores_t, router_bias, tr):
    E, T = scores_t.shape
    rb = router_bias.reshape(E, 1)
    return pl.pallas_call(
        _route_kernel, grid=(T // tr,),
        in_specs=[pl.BlockSpec((E, tr), lambda i: (0, i)), pl.BlockSpec((E, 1), lambda i: (0, 0))],
        out_specs=(pl.BlockSpec((MOE_K, tr), lambda i: (0, i)), pl.BlockSpec((MOE_K, tr), lambda i: (0, i))),
        out_shape=(jax.ShapeDtypeStruct((MOE_K, T), I32), jax.ShapeDtypeStruct((MOE_K, T), F32)),
        compiler_params=_params(("parallel",)), name="route")(scores_t, rb)


def _rank_kernel(idx_ref, u_ref, rank_ref, cnt_ref, carry):
    E = MOE_E

    @pl.when(pl.program_id(0) == 0)
    def _():
        carry[...] = jnp.zeros_like(carry)

    idx = idx_ref[...]
    tp = idx.shape[1]
    ei = lax.broadcasted_iota(I32, (E, tp), 0)
    hits = [ei == idx[k:k + 1, :] for k in range(MOE_K)]
    onehot = jnp.zeros((E, tp), F32)
    for hit in hits:
        onehot = onehot + jnp.where(hit, 1.0, 0.0)
    pos = jnp.dot(onehot.astype(BF16), u_ref[...], preferred_element_type=F32) + carry[...]
    ranks = [jnp.sum(jnp.where(hit, pos, 0.0), axis=0, keepdims=True) for hit in hits]
    rank_ref[...] = jnp.concatenate(ranks, axis=0).astype(I32)
    total = carry[...] + jnp.sum(onehot, axis=1, keepdims=True)
    carry[...] = total
    cnt_ref[...] = jnp.broadcast_to(total, cnt_ref.shape).astype(I32)


def _rank(idx_t, tp):
    K, T = idx_t.shape
    u = jnp.triu(jnp.ones((tp, tp), F32), k=1).astype(BF16)
    rank, cnt = pl.pallas_call(
        _rank_kernel, grid=(T // tp,),
        in_specs=[pl.BlockSpec((K, tp), lambda i: (0, i)), pl.BlockSpec((tp, tp), lambda i: (0, 0))],
        out_specs=(pl.BlockSpec((K, tp), lambda i: (0, i)), pl.BlockSpec((MOE_E, LANES), lambda i: (0, 0))),
        out_shape=(jax.ShapeDtypeStruct((K, T), I32), jax.ShapeDtypeStruct((MOE_E, LANES), I32)),
        scratch_shapes=[pltpu.VMEM((MOE_E, 1), F32)],
        compiler_params=_params(("arbitrary",)), name="rank")(idx_t, u)
    return rank, cnt[:, 0]


def _dest_kernel(idx_ref, rank_ref, po_ref, dest_ref):
    idx = idx_ref[...]
    tp = idx.shape[1]
    ei = lax.broadcasted_iota(I32, (MOE_E, tp), 0)
    po = po_ref[...]
    base = [jnp.sum(jnp.where(ei == idx[k:k + 1, :], po, 0.0), axis=0, keepdims=True) for k in range(MOE_K)]
    dest_ref[...] = jnp.concatenate(base, axis=0).astype(I32) + rank_ref[...]


def _dest(idx_t, rank_t, poffs, tp):
    K, T = idx_t.shape
    po = poffs.astype(F32).reshape(MOE_E, 1)
    spec = pl.BlockSpec((K, tp), lambda i: (0, i))
    return pl.pallas_call(
        _dest_kernel, grid=(T // tp,),
        in_specs=[spec, spec, pl.BlockSpec((MOE_E, 1), lambda i: (0, 0))],
        out_specs=spec, out_shape=jax.ShapeDtypeStruct((K, T), I32),
        compiler_params=_params(("parallel",)), name="dest")(idx_t, rank_t, po)


def _pack_bf16_pairs(v):
    m = v.shape[1] // 2
    bits = lax.bitcast_convert_type(v.astype(BF16).astype(F32), jnp.uint32)
    return lax.bitcast_convert_type((bits[:, :m] >> 16) | (bits[:, m:] & jnp.uint32(0xFFFF0000)), I32)


def _unpack_bf16_pairs(w):
    w = lax.bitcast_convert_type(w, jnp.uint32)
    lo = lax.bitcast_convert_type(w << 16, F32)
    hi = lax.bitcast_convert_type(w & jnp.uint32(0xFFFF0000), F32)
    return lo, hi


def _row_scatter(rows, dest_t, n_rows):
    T, d = rows.shape
    K = dest_t.shape[0]
    nw = SC_CORES * SC_SUBCORES
    per_w = T // nw
    ch = SC_GATHER_ROWS
    mesh = plsc.VectorSubcoreMesh(core_axis_name="c", subcore_axis_name="s")

    @functools.partial(
        pl.kernel, mesh=mesh, out_type=jax.ShapeDtypeStruct((n_rows, d), I32),
        scratch_types=[pltpu.VMEM((ch,), I32), pltpu.VMEM((ch, d), I32), pltpu.SemaphoreType.DMA],
        name="row_scatter")
    def scatter(rows_hbm, dest_hbm, out_hbm, idx_v, rows_v, sem):
        wid = lax.axis_index("s") * SC_CORES + lax.axis_index("c")
        base = wid * per_w

        @pl.loop(0, per_w // ch)
        def _(i):
            off = pl.multiple_of(base + i * ch, ch)
            pltpu.sync_copy(rows_hbm.at[pl.ds(off, ch)], rows_v)
            for k in range(K):
                pltpu.sync_copy(dest_hbm.at[k, pl.ds(off, ch)], idx_v)
                pltpu.async_copy(rows_v, out_hbm.at[idx_v], sem).wait()

    return scatter(rows, dest_t)


def _expert_kernel(po_ref, cnt_ref, xs_hbm, w1_ref, w3_ref, w2_ref, ys_hbm,
                   xbuf, ybuf, w1b, w3b, w2b, gcnt, insem, outsem):
    e = pl.program_id(0)
    ne = pl.num_programs(0)
    n = cnt_ref[e]
    base = po_ref[e]
    rb = xbuf.shape[1]
    hw = D_MODEL // 2
    nblk = jnp.maximum((n + rb - 1) // rb, 1)
    w1b[...] = w1_ref[...].astype(BF16)
    w3b[...] = w3_ref[...].astype(BF16)
    w2b[...] = w2_ref[...].astype(BF16)

    def in_copy(r0, slot):
        return pltpu.make_async_copy(xs_hbm.at[pl.ds(pl.multiple_of(r0, rb), rb)], xbuf.at[slot], insem.at[slot])

    def out_copy(r0, slot):
        return pltpu.make_async_copy(ybuf.at[slot], ys_hbm.at[pl.ds(pl.multiple_of(r0, rb), rb)], outsem.at[slot])

    @pl.when(e == 0)
    def _():
        gcnt[0] = 0
        in_copy(base, 0).start()

    g0 = gcnt[0]

    def body(j, c):
        slot = (g0 + j) % 2

        @pl.when(j + 1 < nblk)
        def _():
            in_copy(base + (j + 1) * rb, 1 - slot).start()

        @pl.when((j + 1 == nblk) & (e + 1 < ne))
        def _():
            in_copy(po_ref[jnp.minimum(e + 1, ne - 1)], 1 - slot).start()

        in_copy(0, slot).wait()

        @pl.when(g0 + j >= 2)
        def _():
            out_copy(0, slot).wait()

        words = xbuf[slot]
        row = j * rb + lax.broadcasted_iota(I32, words.shape, 0)
        lo, hi = _unpack_bf16_pairs(jnp.where(row < n, words, 0))
        lo, hi = lo.astype(BF16), hi.astype(BF16)
        a = (jnp.dot(lo, w1b[0:hw, :], preferred_element_type=F32) +
             jnp.dot(hi, w1b[hw:, :], preferred_element_type=F32))
        g = (jnp.dot(lo, w3b[0:hw, :], preferred_element_type=F32) +
             jnp.dot(hi, w3b[hw:, :], preferred_element_type=F32))
        h = (a * jax.nn.sigmoid(a) * g).astype(BF16)
        ybuf[slot] = _pack_bf16_pairs(jnp.dot(h, w2b[...], preferred_element_type=F32))
        out_copy(base + j * rb, slot).start()
        return c

    lax.fori_loop(0, nblk, body, 0)
    total = g0 + nblk
    gcnt[0] = total

    @pl.when(e + 1 == ne)
    def _():
        @pl.when(total >= 2)
        def _():
            out_copy(0, total % 2).wait()

        out_copy(0, (total - 1) % 2).wait()


def _experts(poffs, counts, xs, w1, w3, w2):
    n_rows, wp = xs.shape
    rb = ROW_BLOCK
    wspec = lambda shape: pl.BlockSpec((None,) + shape, lambda e, po, cn: (e, 0, 0))
    gs = pltpu.PrefetchScalarGridSpec(
        num_scalar_prefetch=2, grid=(MOE_E,),
        in_specs=[pl.BlockSpec(memory_space=pl.ANY), wspec((D_MODEL, MOE_FF)), wspec((D_MODEL, MOE_FF)),
                  wspec((MOE_FF, D_MODEL))],
        out_specs=pl.BlockSpec(memory_space=pl.ANY),
        scratch_shapes=[pltpu.VMEM((2, rb, wp), I32), pltpu.VMEM((2, rb, wp), I32),
                        pltpu.VMEM((D_MODEL, MOE_FF), BF16), pltpu.VMEM((D_MODEL, MOE_FF), BF16),
                        pltpu.VMEM((MOE_FF, D_MODEL), BF16), pltpu.SMEM((1,), I32),
                        pltpu.SemaphoreType.DMA((2,)), pltpu.SemaphoreType.DMA((2,))])
    return pl.pallas_call(
        _expert_kernel, grid_spec=gs, out_shape=jax.ShapeDtypeStruct((n_rows, wp), I32),
        compiler_params=_params(("arbitrary",)), name="experts")(poffs, counts, xs, w1, w3, w2)


def _row_gather(table, idx):
    n, d = idx.shape[0], table.shape[1]
    nw = SC_CORES * SC_SUBCORES
    per_w = n // nw
    ch = SC_GATHER_ROWS // 2
    mesh = plsc.VectorSubcoreMesh(core_axis_name="c", subcore_axis_name="s")

    @functools.partial(
        pl.kernel, mesh=mesh, out_type=jax.ShapeDtypeStruct((n, d), I32),
        scratch_types=[pltpu.VMEM((ch,), I32), pltpu.VMEM((ch,), I32),
                       pltpu.VMEM((ch, d), I32), pltpu.VMEM((ch, d), I32),
                       pltpu.SemaphoreType.DMA, pltpu.SemaphoreType.DMA,
                       pltpu.SemaphoreType.DMA, pltpu.SemaphoreType.DMA],
        name="row_gather")
    def gather(table_hbm, idx_hbm, out_hbm, idx0, idx1, rows0, rows1, g0, g1, w0, w1):
        wid = lax.axis_index("s") * SC_CORES + lax.axis_index("c")
        base = wid * per_w

        @pl.loop(0, per_w // (2 * ch))
        def _(i):
            off0 = pl.multiple_of(base + 2 * i * ch, ch)
            off1 = pl.multiple_of(off0 + ch, ch)
            pltpu.sync_copy(idx_hbm.at[pl.ds(off0, ch)], idx0)
            c0 = pltpu.async_copy(table_hbm.at[idx0], rows0, g0)
            pltpu.sync_copy(idx_hbm.at[pl.ds(off1, ch)], idx1)
            c1 = pltpu.async_copy(table_hbm.at[idx1], rows1, g1)
            c0.wait()
            o0 = pltpu.async_copy(rows0, out_hbm.at[pl.ds(off0, ch)], w0)
            c1.wait()
            o1 = pltpu.async_copy(rows1, out_hbm.at[pl.ds(off1, ch)], w1)
            o0.wait()
            o1.wait()

    return gather(table, idx)


def _combine_kernel(w_ref, tile_ref, x_ref, ysg_ref, s1_ref, s3_ref, s2_ref, g_ref, b_ref, o_ref):
    x2 = x_ref[...]
    xb = x2.astype(BF16)
    a = jnp.dot(xb, s1_ref[...], preferred_element_type=F32)
    c = jnp.dot(xb, s3_ref[...], preferred_element_type=F32)
    shared = jnp.dot((a * jax.nn.sigmoid(a) * c).astype(BF16), s2_ref[...], preferred_element_type=F32)
    lo, hi = _unpack_bf16_pairs(ysg_ref[...])
    lo, hi = lo.astype(BF16), hi.astype(BF16)
    wrep = jnp.dot(w_ref[...], tile_ref[...], precision=lax.Precision.HIGHEST, preferred_element_type=F32)
    col = lax.broadcasted_iota(I32, wrep.shape, 1)
    row = lax.broadcasted_iota(I32, wrep.shape, 0)
    wsel = jnp.where(col // MOE_K == row, wrep, 0.0)
    wh = wsel.astype(BF16)
    wl = (wsel - wh.astype(F32)).astype(BF16)
    y_lo = jnp.dot(wh, lo, preferred_element_type=F32) + jnp.dot(wl, lo, preferred_element_type=F32)
    y_hi = jnp.dot(wh, hi, preferred_element_type=F32) + jnp.dot(wl, hi, preferred_element_type=F32)
    y = shared + jnp.concatenate([y_lo, y_hi], axis=1)
    o_ref[...] = _ln_rows(ALPHA * x2 + y, g_ref[...], b_ref[...])


def _combine(w_nat, x2, ysg, sw1, sw3, sw2, g, b, tc):
    T = x2.shape[0]
    wp = ysg.shape[1]
    full = lambda a: pl.BlockSpec(a.shape, lambda i: (0,) * a.ndim)
    s1, s3, s2 = sw1.astype(BF16), sw3.astype(BF16), sw2.astype(BF16)
    g2, b2 = g.reshape(1, -1), b.reshape(1, -1)
    tile = jnp.asarray(np.arange(tc * MOE_K)[None, :] % MOE_K == np.arange(MOE_K)[:, None], F32)
    return pl.pallas_call(
        _combine_kernel, grid=(T // tc,),
        in_specs=[pl.BlockSpec((tc, MOE_K), lambda i: (i, 0)), full(tile),
                  pl.BlockSpec((tc, D_MODEL), lambda i: (i, 0)),
                  pl.BlockSpec((tc * MOE_K, wp), lambda i: (i, 0)),
                  full(s1), full(s3), full(s2), full(g2), full(b2)],
        out_specs=pl.BlockSpec((tc, D_MODEL), lambda i: (i, 0)),
        out_shape=jax.ShapeDtypeStruct((T, D_MODEL), F32),
        compiler_params=_params(("parallel",)), name="combine",
    )(w_nat, tile, x2, ysg, s1, s3, s2, g2, b2)


def _moe(x2, x2p, scores_t, router_bias, w1, w3, w2, sw1, sw3, sw2, g, b, tiles):
    T = x2.shape[0]
    rb = ROW_BLOCK
    idx_t, w_t = _route(scores_t, router_bias, tiles['route'])
    rank_t, counts = _rank(idx_t, tiles['rank'])
    pcounts = jnp.maximum((counts + rb - 1) // rb, 1) * rb
    pends = jnp.cumsum(pcounts)
    poffs = (pends - pcounts).astype(I32)
    n_blocks = -(-T * MOE_K // rb) + MOE_E
    dest_t = _dest(idx_t, rank_t, poffs, tiles['rank'])
    xs = _row_scatter(x2p, dest_t, n_blocks * rb)
    ys = _experts(poffs, counts.astype(I32), xs, w1, w3, w2)
    ysg = _row_gather(ys, dest_t.T.reshape(-1))
    return _combine(w_t.T, x2, ysg, sw1, sw3, sw2, g, b, tiles['combine'])


def _tiles(B, S):
    T = B * S
    pick = lambda want, n: want if n % want == 0 else n
    return dict(proj=pick(512, T), nsa_q=pick(128, S), nsa_ck=pick(512, S), outproj=pick(512, T),
                mlstm_nb=2 if B % 2 == 0 else 1, xattn=pick(512, S), route=pick(512, T), rank=pick(512, T), scatter=pick(256, T),
                combine=pick(128, T))


def kernel(x, mem, w_in, nsa_pos_k, nsa_cmp_k_w1, nsa_cmp_k_w2, nsa_pos_v, nsa_cmp_v_w1, nsa_cmp_v_w2,
           mlstm_conv_w, mlstm_i_bias, mlstm_f_bias, mlstm_norm_g, w_out, ln1_g, ln1_b,
           xa_wq, xa_wk, xa_wv, xa_wo, ln2_g, ln2_b, router_w, router_bias,
           moe_w1, moe_w3, moe_w2, shared_w1, shared_w3, shared_w2, ln3_g, ln3_b):
    B, S, D = x.shape
    T = B * S
    tl = _tiles(B, S)
    xc = x.reshape(T, D)
    memc = mem.reshape(B * MEM_LEN, D)
    for l in range(w_in.shape[0]):
        (q, cmp, ksel, vsel, kwin, vwin, gates, mq, mk, mv, mo, mif) = _project(xc, _prep_w_in(w_in[l]), tl['proj'])
        kcvc = _compress(cmp, B, S, _prep_cmp(nsa_pos_k[l], nsa_cmp_k_w1[l], nsa_cmp_k_w2[l],
                                              nsa_pos_v[l], nsa_cmp_v_w1[l], nsa_cmp_v_w2[l]))
        y_nsa = _nsa(q, kcvc, ksel, vsel, kwin, vwin, gates, B, S, tl['nsa_q'], tl['nsa_ck'])
        y_ml = _mlstm(mq, mk, mv, mo, mif, mlstm_conv_w[l], mlstm_i_bias[l], mlstm_f_bias[l],
                      mlstm_norm_g[l], B, S, tl['mlstm_nb'])
        x1 = _outproj(y_nsa, y_ml, xc, w_out[l], ln1_g[l], ln1_b[l], tl['outproj'])
        kv = _memkv(memc, xa_wk[l], xa_wv[l])
        x2, x2p, scores_t = _xattn(x1, kv, xa_wq[l], xa_wo[l], ln2_g[l], ln2_b[l], router_w[l], S, tl['xattn'])
        xc = _moe(x2, x2p, scores_t, router_bias[l], moe_w1[l], moe_w3[l], moe_w2[l],
                  shared_w1[l], shared_w3[l], shared_w2[l], ln3_g[l], ln3_b[l], tl)
    return xc.reshape(B, S, D)
```

```python
import functools
import numpy as np
import jax
import jax.numpy as jnp
from jax import lax
from jax.experimental import pallas as pl
from jax.experimental.pallas import tpu as pltpu
from jax.experimental.pallas import tpu_sc as plsc

F32 = jnp.float32
BF16 = jnp.bfloat16
I32 = jnp.int32

D_MODEL = 1024
MEM_LEN = 256
NSA_HEADS = 8
NSA_GROUPS = 2
NSA_HPG = 4
NSA_DK = 64
NSA_CMP_LEN = 32
NSA_CMP_STRIDE = 16
NSA_SEL_BLOCK = 64
NSA_SEL_TOPN = 8
NSA_WINDOW = 512
ML_HEADS = 4
ML_DH = 128
ML_CHUNK = 64
ML_CONV = 4
XA_HEADS = 4
XA_DH = 256
MOE_E = 256
MOE_K = 8
MOE_GROUPS = 8
MOE_TOPK_GROUPS = 4
MOE_FF = 256
MOE_ROUTE_SCALE = 2.5
DEPTH = 1
ALPHA = (2.0 * DEPTH) ** 0.25
LN_EPS = 1e-5
NEG = -1e30
FORCE_BONUS = 1e4

LANES = 128
ROW_BLOCK = 512
VMEM_LIMIT = 56 * 1024 * 1024
SC_CORES = 2
SC_SUBCORES = 16
SC_GATHER_ROWS = 128

_DN_T = (((1,), (1,)), ((), ()))
_DN_TA = (((0,), (0,)), ((), ()))


def _params(sem):
    return pltpu.CompilerParams(dimension_semantics=sem, vmem_limit_bytes=VMEM_LIMIT)


def _ln_rows(v, g, b):
    mu = jnp.mean(v, axis=-1, keepdims=True)
    d = v - mu
    var = jnp.mean(d * d, axis=-1, keepdims=True)
    return d * lax.rsqrt(var + LN_EPS) * g + b


_SEGS = (('q', 512, BF16), ('cmp', 256, F32), ('ksel', 128, BF16), ('vsel', 128, BF16),
         ('kwin', 128, BF16), ('vwin', 128, BF16), ('gates', 128, F32), ('mq', 512, BF16),
         ('mk', 512, BF16), ('mv', 512, BF16), ('mo', 512, BF16), ('mif', 128, F32))


def _proj_kernel(x_ref, w_ref, *out_refs):
    xb = x_ref[...].astype(BF16)
    off = 0
    for o_ref, (_, wd, _) in zip(out_refs, _SEGS):
        o_ref[...] = jnp.dot(xb, w_ref[:, off:off + wd], preferred_element_type=F32).astype(o_ref.dtype)
        off += wd


def _prep_w_in(w):
    sizes = (512,) + (128,) * 6 + (24,) + (512,) * 4 + (4, 4)
    cuts = np.cumsum(sizes)[:-1].tolist()
    (wq, kc, vc, ks, vs, kw, vw, wg, mq, mk, mv, mo, mi, mf) = jnp.split(w, cuts, axis=1)
    wq = wq.reshape(D_MODEL, NSA_GROUPS, NSA_HPG, NSA_DK).transpose(0, 2, 1, 3).reshape(D_MODEL, 512)
    pad = lambda a: jnp.pad(a, ((0, 0), (0, LANES - a.shape[1])))
    segs = [wq, kc, vc, ks, vs, kw, vw, pad(wg), mq, mk, mv, mo, pad(jnp.concatenate([mi, mf], axis=1))]
    return jnp.concatenate(segs, axis=1).astype(BF16)


def _project(x2d, w_all, tm):
    T = x2d.shape[0]
    n = w_all.shape[1]
    out_shape = tuple(jax.ShapeDtypeStruct((T, wd), dt) for _, wd, dt in _SEGS)
    out_specs = tuple(pl.BlockSpec((tm, wd), lambda i: (i, 0)) for _, wd, _ in _SEGS)
    return pl.pallas_call(
        _proj_kernel, grid=(T // tm,),
        in_specs=[pl.BlockSpec((tm, D_MODEL), lambda i: (i, 0)),
                  pl.BlockSpec((D_MODEL, n), lambda i: (0, 0))],
        out_specs=out_specs, out_shape=out_shape,
        compiler_params=_params(("parallel",)), name="proj")(x2d, w_all)


def _cmp_kernel(r_ref, pa_ref, pb_ref, wa_ref, wb_ref, w2_ref, o_ref):
    r = r_ref[...]
    a = jnp.dot((r + pa_ref[...]).astype(BF16), wa_ref[...], preferred_element_type=F32)
    b = jnp.dot((r + pb_ref[...]).astype(BF16), wb_ref[...], preferred_element_type=F32)
    nr = r.shape[0]
    hid = a + pltpu.roll(b, nr - 1, 0)
    hid = hid * jax.nn.sigmoid(hid)
    out = jnp.dot(hid.astype(BF16), w2_ref[...], preferred_element_type=F32)
    row = lax.broadcasted_iota(I32, out.shape, 0)
    o_ref[...] = jnp.where(row < nr - 1, out, 0.0).astype(o_ref.dtype)


def _prep_cmp(pos_k, w1_k, w2_k, pos_v, w1_v, w2_v):
    eye = jnp.eye(NSA_GROUPS, dtype=F32)

    def expand_w1(w1, half):
        w = w1.reshape(NSA_CMP_LEN, NSA_DK, NSA_DK)[half * 16:(half + 1) * 16]
        return jnp.einsum('jde,gk->jgdke', w, eye).reshape(16, 128, 128)

    def both(fk, fv):
        z = jnp.zeros_like(fk)
        top = jnp.concatenate([fk, z], axis=-1)
        bot = jnp.concatenate([z, fv], axis=-1)
        return jnp.concatenate([top, bot], axis=-2)

    wa = both(expand_w1(w1_k, 0), expand_w1(w1_v, 0)).reshape(16 * 256, 256).astype(BF16)
    wb = both(expand_w1(w1_k, 1), expand_w1(w1_v, 1)).reshape(16 * 256, 256).astype(BF16)
    w2 = both(jnp.kron(eye, w2_k), jnp.kron(eye, w2_v)).astype(BF16)

    def pos_row(half):
        pk = jnp.tile(pos_k[half * 16:(half + 1) * 16], (1, NSA_GROUPS))
        pv = jnp.tile(pos_v[half * 16:(half + 1) * 16], (1, NSA_GROUPS))
        return jnp.concatenate([pk, pv], axis=1).reshape(1, 16 * 256)

    return pos_row(0), pos_row(1), wa, wb, w2


def _compress(cmp2d, B, S, prep):
    pa, pb, wa, wb, w2 = prep
    nr = S // NSA_CMP_STRIDE
    r = cmp2d.reshape(B, nr, NSA_CMP_STRIDE * 256)
    full = lambda a: pl.BlockSpec(a.shape, lambda b: (0,) * a.ndim)
    return pl.pallas_call(
        _cmp_kernel, grid=(B,),
        in_specs=[pl.BlockSpec((None, nr, NSA_CMP_STRIDE * 256), lambda b: (b, 0, 0)),
                  full(pa), full(pb), full(wa), full(wb), full(w2)],
        out_specs=pl.BlockSpec((None, nr, 256), lambda b: (b, 0, 0)),
        out_shape=jax.ShapeDtypeStruct((B, nr, 256), BF16),
        compiler_params=_params(("parallel",)), name="cmp")(r, pa, pb, wa, wb, w2)


def _nsa_consts(S):
    n_cmp = (S - NSA_CMP_LEN) // NSA_CMP_STRIDE + 1
    n_sel = S // NSA_SEL_BLOCK
    cs = np.arange(n_cmp) * NSA_CMP_STRIDE
    ss = np.arange(n_sel) * NSA_SEL_BLOCK
    ov = ((cs[:, None] < ss[None, :] + NSA_SEL_BLOCK) & (cs[:, None] + NSA_CMP_LEN > ss[None, :]))
    ovt = np.zeros((LANES, S // NSA_CMP_STRIDE), np.float32)
    ovt[:n_sel, :n_cmp] = ov.T
    e = np.zeros((LANES, S), np.float32)
    e[np.arange(S) // NSA_SEL_BLOCK, np.arange(S)] = 1.0
    return jnp.asarray(ovt, BF16), jnp.asarray(e, BF16)


def _nsa_kernel(q_ref, kcvc_ref, ksel_ref, vsel_ref, kwin_ref, vwin_ref, gates_ref, ovt_ref, e_ref,
                y_ref, bias_ref, *, tq, ck, n_sel):
    G, H = NSA_GROUPS, NSA_HPG
    GH = G * H
    M = GH * tq
    S = ksel_ref.shape[0]
    W = NSA_WINDOW
    ws = min(W + tq, S)
    t0 = pl.program_id(1) * tq
    gates = jax.nn.sigmoid(gates_ref[...])
    lane = lax.broadcasted_iota(I32, (tq, LANES), 1)
    t_col = t0 + lax.broadcasted_iota(I32, (tq, 1), 0)
    kc = kcvc_ref[:, 0:LANES]
    vc = kcvc_ref[:, LANES:2 * LANES]
    nc = kc.shape[0]
    qs = []
    for g in range(G):
        gmask = (lane // NSA_DK) == g
        for h in range(H):
            qh = q_ref[:, h * LANES:(h + 1) * LANES] * (NSA_DK ** -0.5)
            qs.append(jnp.where(gmask, qh, jnp.zeros_like(qh)))
    Q = jnp.concatenate(qs, axis=0).astype(BF16)

    s = lax.dot_general(Q, kc, _DN_T, preferred_element_type=F32)
    c_idx = lax.broadcasted_iota(I32, (tq, nc), 1)
    cmask = (c_idx * NSA_CMP_STRIDE + NSA_CMP_LEN - 1) <= t_col
    s3 = jnp.where(cmask[None], s.reshape(GH, tq, nc), NEG)
    p = jnp.exp(s3 - jnp.max(s3, axis=-1, keepdims=True))
    p = p / jnp.sum(p, axis=-1, keepdims=True)
    p = jnp.where(cmask[None], p, 0.0)
    o_cmp = jnp.dot(p.reshape(M, nc).astype(BF16), vc, preferred_element_type=F32).reshape(GH, tq, LANES)

    for g in range(G):
        psum = jnp.sum(p[g * H:(g + 1) * H], axis=0)
        hi = psum.astype(BF16)
        lo = (psum - hi.astype(F32)).astype(BF16)
        ovt = ovt_ref[...]
        pslt = (lax.dot_general(ovt, hi, _DN_T, preferred_element_type=F32) +
                lax.dot_general(ovt, lo, _DN_T, preferred_element_type=F32))
        imp_p = pslt[0:n_sel, :]
        n_i = lax.broadcasted_iota(I32, (n_sel, tq), 0)
        cur = (t0 + lax.broadcasted_iota(I32, (n_sel, tq), 1)) // NSA_SEL_BLOCK
        forced = (n_i == 0) | (n_i == cur) | (n_i == cur - 1)
        imp = jnp.where(n_i <= cur, imp_p + jnp.where(forced, FORCE_BONUS, 0.0), NEG)
        cnt = jnp.zeros((n_sel, tq), F32)
        for m in range(n_sel):
            row = imp[m:m + 1, :]
            beats = (row > imp) | ((row == imp) & (n_i > m))
            cnt = cnt + jnp.where(beats, 1.0, 0.0)
        selt = jnp.where(cnt < float(min(NSA_SEL_TOPN, n_sel)), 1.0, 0.0)
        selt = jnp.concatenate([selt, jnp.zeros((LANES - n_sel, tq), F32)], axis=0)
        sel = selt.T.astype(BF16)
        maskf = jnp.dot(sel, e_ref[...], preferred_element_type=F32)
        kpos = lax.broadcasted_iota(I32, (tq, S), 1)
        bias_ref[g] = jnp.where((maskf > 0.5) & (kpos <= t_col), 0.0, NEG)

    vlane = lax.broadcasted_iota(I32, (1, LANES), 1) // NSA_DK

    def pv_with_sums(pb, v):
        outs = []
        for g in range(G):
            vg = jnp.where(vlane == g, v, jnp.ones_like(v))
            outs.append(jnp.dot(pb[g * H * tq:(g + 1) * H * tq], vg, preferred_element_type=F32))
        return jnp.concatenate(outs, axis=0)

    def normalise(acc):
        outs = []
        for r in range(GH):
            c = NSA_DK * (1 - r // H)
            outs.append(acc[r] / acc[r][:, c:c + 1])
        return outs

    def sel_body(j, carry):
        m_i, acc = carry
        ks = pl.multiple_of(j * ck, ck)
        k = ksel_ref[pl.ds(ks, ck), :]
        v = vsel_ref[pl.ds(ks, ck), :]
        sj = lax.dot_general(Q, k, _DN_T, preferred_element_type=F32)
        sj = (sj.reshape(G, H, tq, ck) + bias_ref[:, :, pl.ds(ks, ck)][:, None]).reshape(GH, tq, ck)
        m_new = jnp.maximum(m_i, jnp.max(sj, axis=-1, keepdims=True))
        a = jnp.exp(m_i - m_new)
        pj = jnp.exp(sj - m_new).reshape(M, ck).astype(BF16)
        return m_new, a * acc + pv_with_sums(pj, v).reshape(GH, tq, LANES)

    init = (jnp.full((GH, tq, 1), NEG, F32), jnp.zeros((GH, tq, LANES), F32))
    _, acc = lax.fori_loop(0, (t0 + tq + ck - 1) // ck, sel_body, init)
    o_sel = normalise(acc)

    kst = pl.multiple_of(jnp.clip(t0 - W, 0, S - ws), LANES)
    kwn = kwin_ref[pl.ds(kst, ws), :]
    vwn = vwin_ref[pl.ds(kst, ws), :]
    sw = lax.dot_general(Q, kwn, _DN_T, preferred_element_type=F32)
    wpos = kst + lax.broadcasted_iota(I32, (tq, ws), 1)
    wmask = (wpos <= t_col) & (wpos > t_col - W)
    sw3 = jnp.where(wmask[None], sw.reshape(GH, tq, ws), NEG)
    pw = jnp.exp(sw3 - jnp.max(sw3, axis=-1, keepdims=True)).reshape(M, ws).astype(BF16)
    o_win = normalise(pv_with_sums(pw, vwn).reshape(GH, tq, LANES))

    g0mask = lane < NSA_DK
    for h in range(H):
        o_g = []
        for g in range(G):
            r = g * H + h
            c0 = r * 3
            o_g.append(gates[:, c0:c0 + 1] * o_cmp[r] + gates[:, c0 + 1:c0 + 2] * o_sel[r] +
                       gates[:, c0 + 2:c0 + 3] * o_win[r])
        y_ref[:, h * LANES:(h + 1) * LANES] = jnp.where(g0mask, o_g[0], o_g[1]).astype(y_ref.dtype)


def _nsa(q, kcvc, ksel, vsel, kwin, vwin, gates, B, S, tq, ck):
    T = B * S
    nq = S // tq
    ovt, e = _nsa_consts(S)
    seq = lambda a: a.reshape(B, S, LANES)
    kv_spec = pl.BlockSpec((None, S, LANES), lambda b, i: (b, 0, 0))
    kern = functools.partial(_nsa_kernel, tq=tq, ck=ck, n_sel=S // NSA_SEL_BLOCK)
    return pl.pallas_call(
        kern, grid=(B, nq),
        in_specs=[pl.BlockSpec((tq, 512), lambda b, i: (b * nq + i, 0)),
                  pl.BlockSpec((None,) + kcvc.shape[1:], lambda b, i: (b, 0, 0)),
                  kv_spec, kv_spec, kv_spec, kv_spec,
                  pl.BlockSpec((tq, LANES), lambda b, i: (b * nq + i, 0)),
                  pl.BlockSpec(ovt.shape, lambda b, i: (0, 0)),
                  pl.BlockSpec(e.shape, lambda b, i: (0, 0))],
        out_specs=pl.BlockSpec((tq, 512), lambda b, i: (b * nq + i, 0)),
        out_shape=jax.ShapeDtypeStruct((T, 512), BF16),
        scratch_shapes=[pltpu.VMEM((NSA_GROUPS, tq, S), F32)],
        compiler_params=_params(("parallel", "parallel")), name="nsa",
    )(q, kcvc, seq(ksel), seq(vsel), seq(kwin), seq(vwin), gates, ovt, e)


def _mlstm_kernel(q_ref, k_ref, v_ref, o_ref, gn_ref, gt_ref, cw_ref, bn_ref, bt_ref, ng_ref, tri_ref,
                  y_ref, c_scr, n_scr):
    H, dh, L = ML_HEADS, ML_DH, ML_CHUNK
    nb, S = q_ref.shape[0], q_ref.shape[1]
    nchunk = S // L
    c_scr[...] = jnp.zeros_like(c_scr)
    n_scr[...] = jnp.zeros_like(n_scr)
    row = lax.broadcasted_iota(I32, (L, H * dh), 0)
    li = lax.broadcasted_iota(I32, (L, L), 0)
    mi = lax.broadcasted_iota(I32, (L, L), 1)
    causal = mi <= li
    tril = tri_ref[0]
    triu = tri_ref[1]
    hp = lax.Precision.HIGHEST

    def conv_silu(ref, bi, c, wofs):
        r0 = pl.multiple_of(c * L, L)
        rp = pl.multiple_of(jnp.maximum(c - 1, 0) * L, L)
        cur = ref[bi, pl.ds(r0, L), :].astype(F32)
        prev = ref[bi, pl.ds(rp, L), :].astype(F32) * jnp.where(c > 0, 1.0, 0.0)
        acc = cur * cw_ref[ML_CONV - 1:ML_CONV, wofs:wofs + H * dh]
        for j in range(1, ML_CONV):
            sh = jnp.where(row < j, pltpu.roll(prev, j, 0), pltpu.roll(cur, j, 0))
            acc = acc + sh * cw_ref[ML_CONV - 1 - j:ML_CONV - j, wofs:wofs + H * dh]
        return acc * jax.nn.sigmoid(acc)

    def body(c, m_state):
        r0 = pl.multiple_of(c * L, L)
        new_m = []
        for bi in range(nb):
            qa = conv_silu(q_ref, bi, c, 0) * (dh ** -0.5)
            ka = conv_silu(k_ref, bi, c, H * dh)
            va = v_ref[bi, pl.ds(r0, L), :]
            oa = o_ref[bi, pl.ds(r0, L), :].astype(F32)
            gn = gn_ref[bi, pl.ds(r0, L), :] + bn_ref[...]
            gt = gt_ref[bi, :, c, :] + bt_ref[...]
            lf_n = jax.nn.log_sigmoid(gn)
            lf_t = jax.nn.log_sigmoid(gt)
            b_n = jnp.dot(tril, lf_n, precision=hp, preferred_element_type=F32)
            b_t = jnp.dot(lf_t, triu, precision=hp, preferred_element_type=F32)
            for h in range(H):
                st = bi * H + h
                q = qa[:, h * dh:(h + 1) * dh]
                k = ka[:, h * dh:(h + 1) * dh]
                v = va[:, h * dh:(h + 1) * dh]
                m_old = m_state[st]
                b_col = b_n[:, H + h:H + h + 1]
                i_col = gn[:, h:h + 1]
                b_row = b_t[H + h:H + h + 1, :]
                i_row = gt[h:h + 1, :]
                g_tot = b_t[H + h:H + h + 1, L - 1:L]
                d_log = jnp.where(causal, b_col - b_row + i_row, NEG)
                inter = b_col + m_old
                m_q = jnp.maximum(inter, jnp.max(d_log, axis=-1, keepdims=True))
                w_intra = jnp.exp(d_log - m_q)
                w_inter = jnp.exp(inter - m_q)
                qb = q.astype(BF16)
                s = lax.dot_general(qb, k.astype(BF16), _DN_T, preferred_element_type=F32) * w_intra
                cst = c_scr[st]
                nst = n_scr[st]
                num = (w_inter * jnp.dot(qb, cst.astype(BF16), preferred_element_type=F32) +
                       jnp.dot(s.astype(BF16), v, preferred_element_type=F32))
                den = w_inter * jnp.sum(q * nst, axis=-1, keepdims=True) + jnp.sum(s, axis=-1, keepdims=True)
                hv = num / jnp.maximum(jnp.abs(den), jnp.exp(-m_q))
                log_k = g_tot - b_col + i_col
                m_new = jnp.maximum(g_tot + m_old, jnp.max(log_k, axis=0, keepdims=True))
                wk = jnp.exp(log_k - m_new)
                decay = jnp.exp(g_tot + m_old - m_new)
                kw = k * wk
                c_scr[st] = decay * cst + lax.dot_general(kw.astype(BF16), v, _DN_TA, preferred_element_type=F32)
                n_scr[st] = decay * nst + jnp.sum(kw, axis=0, keepdims=True)
                new_m.append(m_new)
                mu = jnp.mean(hv, axis=-1, keepdims=True)
                dv = hv - mu
                var = jnp.mean(dv * dv, axis=-1, keepdims=True)
                hn = dv * lax.rsqrt(var + LN_EPS) * ng_ref[:, h * dh:(h + 1) * dh]
                og = jax.nn.sigmoid(oa[:, h * dh:(h + 1) * dh])
                y_ref[bi, pl.ds(r0, L), h * dh:(h + 1) * dh] = (og * hn).astype(y_ref.dtype)
        return tuple(new_m)

    lax.fori_loop(0, nchunk, body, tuple(jnp.zeros((1, 1), F32) for _ in range(nb * H)))


def _mlstm(mq, mk, mv, mo, mif, conv_w, i_bias, f_bias, norm_g, B, S, nb):
    T = B * S
    H, dh, L = ML_HEADS, ML_DH, ML_CHUNK
    W = H * dh
    gt = mif[:, :2 * H].reshape(B, S, 2 * H).transpose(0, 2, 1).reshape(B, 2 * H, S // L, L)
    cw = conv_w.reshape(ML_CONV, 2 * W)
    bias = jnp.concatenate([i_bias, f_bias])
    bn = jnp.pad(bias, (0, LANES - 2 * H)).reshape(1, LANES)
    bt = bias.reshape(2 * H, 1)
    ng = norm_g.reshape(1, W)
    tri = jnp.stack([jnp.tril(jnp.ones((L, L), F32)), jnp.triu(jnp.ones((L, L), F32))])
    seq = lambda a: a.reshape(B, S, a.shape[1])
    rows = lambda w: pl.BlockSpec((nb, S, w), lambda b: (b, 0, 0))
    full = lambda a: pl.BlockSpec(a.shape, lambda b: (0,) * a.ndim)
    y = pl.pallas_call(
        _mlstm_kernel, grid=(B // nb,),
        in_specs=[rows(W), rows(W), rows(W), rows(W), rows(LANES),
                  pl.BlockSpec((nb, 2 * H, S // L, L), lambda b: (b, 0, 0, 0)),
                  full(cw), full(bn), full(bt), full(ng), full(tri)],
        out_specs=rows(W),
        out_shape=jax.ShapeDtypeStruct((B, S, W), BF16),
        scratch_shapes=[pltpu.VMEM((nb * H, dh, dh), F32), pltpu.VMEM((nb * H, 1, dh), F32)],
        compiler_params=_params(("parallel",)), name="mlstm",
    )(seq(mq), seq(mk), seq(mv), seq(mo), seq(mif), gt, cw, bn, bt, ng, tri)
    return y.reshape(T, W)


def _outproj_kernel(yn_ref, ym_ref, x_ref, w_ref, g_ref, b_ref, o_ref):
    mix = (jnp.dot(yn_ref[...], w_ref[0:512, :], preferred_element_type=F32) +
           jnp.dot(ym_ref[...], w_ref[512:1024, :], preferred_element_type=F32))
    o_ref[...] = _ln_rows(ALPHA * x_ref[...] + mix, g_ref[...], b_ref[...])


def _outproj(y_nsa, y_ml, x2d, w_out, g, b, tm):
    T = x2d.shape[0]
    wn = w_out[:512].reshape(NSA_GROUPS, NSA_HPG, NSA_DK, D_MODEL).transpose(1, 0, 2, 3).reshape(512, D_MODEL)
    w = jnp.concatenate([wn, w_out[512:]], axis=0).astype(BF16)
    row = lambda wd: pl.BlockSpec((tm, wd), lambda i: (i, 0))
    full = lambda a: pl.BlockSpec(a.shape, lambda i: (0,) * a.ndim)
    g2, b2 = g.reshape(1, -1), b.reshape(1, -1)
    return pl.pallas_call(
        _outproj_kernel, grid=(T // tm,),
        in_specs=[row(512), row(512), row(D_MODEL), full(w), full(g2), full(b2)],
        out_specs=row(D_MODEL), out_shape=jax.ShapeDtypeStruct((T, D_MODEL), F32),
        compiler_params=_params(("parallel",)), name="outproj")(y_nsa, y_ml, x2d, w, g2, b2)


def _memkv_kernel(m_ref, w_ref, o_ref):
    o_ref[...] = jnp.dot(m_ref[...].astype(BF16), w_ref[...], preferred_element_type=F32).astype(o_ref.dtype)


def _memkv(mem2d, wk, wv):
    w = jnp.concatenate([wk, wv], axis=1).astype(BF16)
    R = mem2d.shape[0]
    return pl.pallas_call(
        _memkv_kernel, grid=(R // MEM_LEN,),
        in_specs=[pl.BlockSpec((MEM_LEN, D_MODEL), lambda i: (i, 0)),
                  pl.BlockSpec(w.shape, lambda i: (0, 0))],
        out_specs=pl.BlockSpec((MEM_LEN, 2 * D_MODEL), lambda i: (i, 0)),
        out_shape=jax.ShapeDtypeStruct((R, 2 * D_MODEL), BF16),
        compiler_params=_params(("parallel",)), name="memkv")(mem2d, w)


def _xattn_kernel(x_ref, kv_ref, wq_ref, wo_ref, g_ref, b_ref, rw_ref, x2_ref, x2p_ref, sc_ref):
    x1 = x_ref[...]
    q = jnp.dot(x1.astype(BF16), wq_ref[...], preferred_element_type=F32).astype(BF16)
    outs = []
    for h in range(XA_HEADS):
        qh = q[:, h * XA_DH:(h + 1) * XA_DH]
        kh = kv_ref[:, h * XA_DH:(h + 1) * XA_DH]
        vh = kv_ref[:, D_MODEL + h * XA_DH:D_MODEL + (h + 1) * XA_DH]
        s = lax.dot_general(qh, kh, _DN_T, preferred_element_type=F32) * (XA_DH ** -0.5)
        p = jnp.exp(s - jnp.max(s, axis=-1, keepdims=True))
        p = p / jnp.sum(p, axis=-1, keepdims=True)
        outs.append(jnp.dot(p.astype(BF16), vh, preferred_element_type=F32).astype(BF16))
    o = jnp.concatenate(outs, axis=1)
    xa = jnp.dot(o, wo_ref[...], preferred_element_type=F32)
    x2 = _ln_rows(ALPHA * x1 + xa, g_ref[...], b_ref[...])
    x2_ref[...] = x2
    x2p_ref[...] = _pack_bf16_pairs(x2)
    xh = x2.astype(BF16)
    xl = (x2 - xh.astype(F32)).astype(BF16)
    wh = rw_ref[0]
    wl = rw_ref[1]
    logit = (lax.dot_general(wh, xh, _DN_T, preferred_element_type=F32) +
             lax.dot_general(wh, xl, _DN_T, preferred_element_type=F32) +
             lax.dot_general(wl, xh, _DN_T, preferred_element_type=F32))
    sc_ref[...] = jax.nn.sigmoid(logit)


def _xattn(x1, kv, wq, wo, g, b, router_w, S, tq):
    T = x1.shape[0]
    wqb, wob = wq.astype(BF16), wo.astype(BF16)
    rwt = router_w.T
    rh = rwt.astype(BF16)
    rw = jnp.stack([rh, (rwt - rh.astype(F32)).astype(BF16)])
    g2, b2 = g.reshape(1, -1), b.reshape(1, -1)
    full = lambda a: pl.BlockSpec(a.shape, lambda i: (0,) * a.ndim)
    per = S // tq
    return pl.pallas_call(
        _xattn_kernel, grid=(T // tq,),
        in_specs=[pl.BlockSpec((tq, D_MODEL), lambda i: (i, 0)),
                  pl.BlockSpec((MEM_LEN, 2 * D_MODEL), lambda i: (i // per, 0)),
                  full(wqb), full(wob), full(g2), full(b2), full(rw)],
        out_specs=(pl.BlockSpec((tq, D_MODEL), lambda i: (i, 0)),
                   pl.BlockSpec((tq, D_MODEL // 2), lambda i: (i, 0)),
                   pl.BlockSpec((MOE_E, tq), lambda i: (0, i))),
        out_shape=(jax.ShapeDtypeStruct((T, D_MODEL), F32), jax.ShapeDtypeStruct((T, D_MODEL // 2), I32),
                   jax.ShapeDtypeStruct((MOE_E, T), F32)),
        compiler_params=_params(("parallel",)), name="xattn")(x1, kv, wqb, wob, g2, b2, rw)


def _route_kernel(sc_ref, rb_ref, idx_ref, w_ref):
    E, G = MOE_E, MOE_GROUPS
    per = E // G
    scores = sc_ref[...]
    tr = scores.shape[1]
    biased = scores + rb_ref[...]
    g3 = biased.reshape(G, per, tr)
    j3 = lax.broadcasted_iota(I32, (G, per, tr), 1)
    m1 = jnp.max(g3, axis=1, keepdims=True)
    first = jnp.min(jnp.where(g3 == m1, j3, per), axis=1, keepdims=True)
    m2 = jnp.max(jnp.where(j3 == first, -jnp.inf, g3), axis=1, keepdims=True)
    gs = (m1 + m2).reshape(G, tr)
    gi = lax.broadcasted_iota(I32, (G, tr), 0)
    cnt = jnp.zeros((G, tr), F32)
    for m in range(G):
        row = gs[m:m + 1, :]
        cnt = cnt + jnp.where((row > gs) | ((row == gs) & (gi > m)), 1.0, 0.0)
    gmask = cnt < float(MOE_TOPK_GROUPS)
    masked = jnp.where(gmask[:, None, :], g3, NEG).reshape(E, tr)
    ei = lax.broadcasted_iota(I32, (E, tr), 0)
    idxs, ws = [], []
    for _ in range(MOE_K):
        mx = jnp.max(masked, axis=0, keepdims=True)
        ix = jnp.min(jnp.where(masked == mx, ei, E), axis=0, keepdims=True)
        hit = ei == ix
        ws.append(jnp.sum(jnp.where(hit, scores, 0.0), axis=0, keepdims=True))
        idxs.append(ix)
        masked = jnp.where(hit, -jnp.inf, masked)
    w = jnp.concatenate(ws, axis=0)
    idx_ref[...] = jnp.concatenate(idxs, axis=0)
    w_ref[...] = w / jnp.sum(w, axis=0, keepdims=True) * MOE_ROUTE_SCALE


def _route(scores_t, router_bias, tr):
    E, T = scores_t.shape
    rb = router_bias.reshape(E, 1)
    return pl.pallas_call(
        _route_kernel, grid=(T // tr,),
        in_specs=[pl.BlockSpec((E, tr), lambda i: (0, i)), pl.BlockSpec((E, 1), lambda i: (0, 0))],
        out_specs=(pl.BlockSpec((MOE_K, tr), lambda i: (0, i)), pl.BlockSpec((MOE_K, tr), lambda i: (0, i))),
        out_shape=(jax.ShapeDtypeStruct((MOE_K, T), I32), jax.ShapeDtypeStruct((MOE_K, T), F32)),
        compiler_params=_params(("parallel",)), name="route")(scores_t, rb)


def _rank_kernel(idx_ref, u_ref, rank_ref, cnt_ref, carry):
    E = MOE_E

    @pl.when(pl.program_id(0) == 0)
    def _():
        carry[...] = jnp.zeros_like(carry)

    idx = idx_ref[...]
    tp = idx.shape[1]
    ei = lax.broadcasted_iota(I32, (E, tp), 0)
    hits = [ei == idx[k:k + 1, :] for k in range(MOE_K)]
    onehot = jnp.zeros((E, tp), F32)
    for hit in hits:
        onehot = onehot + jnp.where(hit, 1.0, 0.0)
    pos = jnp.dot(onehot.astype(BF16), u_ref[...], preferred_element_type=F32) + carry[...]
    ranks = [jnp.sum(jnp.where(hit, pos, 0.0), axis=0, keepdims=True) for hit in hits]
    rank_ref[...] = jnp.concatenate(ranks, axis=0).astype(I32)
    total = carry[...] + jnp.sum(onehot, axis=1, keepdims=True)
    carry[...] = total
    cnt_ref[...] = jnp.broadcast_to(total, cnt_ref.shape).astype(I32)


def _rank(idx_t, tp):
    K, T = idx_t.shape
    u = jnp.triu(jnp.ones((tp, tp), F32), k=1).astype(BF16)
    rank, cnt = pl.pallas_call(
        _rank_kernel, grid=(T // tp,),
        in_specs=[pl.BlockSpec((K, tp), lambda i: (0, i)), pl.BlockSpec((tp, tp), lambda i: (0, 0))],
        out_specs=(pl.BlockSpec((K, tp), lambda i: (0, i)), pl.BlockSpec((MOE_E, LANES), lambda i: (0, 0))),
        out_shape=(jax.ShapeDtypeStruct((K, T), I32), jax.ShapeDtypeStruct((MOE_E, LANES), I32)),
        scratch_shapes=[pltpu.VMEM((MOE_E, 1), F32)],
        compiler_params=_params(("arbitrary",)), name="rank")(idx_t, u)
    return rank, cnt[:, 0]


def _dest_kernel(idx_ref, rank_ref, po_ref, dest_ref):
    idx = idx_ref[...]
    tp = idx.shape[1]
    ei = lax.broadcasted_iota(I32, (MOE_E, tp), 0)
    po = po_ref[...]
    base = [jnp.sum(jnp.where(ei == idx[k:k + 1, :], po, 0.0), axis=0, keepdims=True) for k in range(MOE_K)]
    dest_ref[...] = jnp.concatenate(base, axis=0).astype(I32) + rank_ref[...]


def _dest(idx_t, rank_t, poffs, tp):
    K, T = idx_t.shape
    po = poffs.astype(F32).reshape(MOE_E, 1)
    spec = pl.BlockSpec((K, tp), lambda i: (0, i))
    return pl.pallas_call(
        _dest_kernel, grid=(T // tp,),
        in_specs=[spec, spec, pl.BlockSpec((MOE_E, 1), lambda i: (0, 0))],
        out_specs=spec, out_shape=jax.ShapeDtypeStruct((K, T), I32),
        compiler_params=_params(("parallel",)), name="dest")(idx_t, rank_t, po)


def _pack_bf16_pairs(v):
    m = v.shape[1] // 2
    bits = lax.bitcast_convert_type(v.astype(BF16).astype(F32), jnp.uint32)
    return lax.bitcast_convert_type((bits[:, :m] >> 16) | (bits[:, m:] & jnp.uint32(0xFFFF0000)), I32)


def _unpack_bf16_pairs(w):
    w = lax.bitcast_convert_type(w, jnp.uint32)
    lo = lax.bitcast_convert_type(w << 16, F32)
    hi = lax.bitcast_convert_type(w & jnp.uint32(0xFFFF0000), F32)
    return lo, hi


def _row_scatter(rows, dest_t, n_rows):
    T, d = rows.shape
    K = dest_t.shape[0]
    nw = SC_CORES * SC_SUBCORES
    per_w = T // nw
    ch = SC_GATHER_ROWS
    mesh = plsc.VectorSubcoreMesh(core_axis_name="c", subcore_axis_name="s")

    @functools.partial(
        pl.kernel, mesh=mesh, out_type=jax.ShapeDtypeStruct((n_rows, d), I32),
        scratch_types=[pltpu.VMEM((K, ch), I32), pltpu.VMEM((ch, d), I32), pltpu.SemaphoreType.DMA],
        name="row_scatter")
    def scatter(rows_hbm, dest_hbm, out_hbm, idx_v, rows_v, sem):
        wid = lax.axis_index("s") * SC_CORES + lax.axis_index("c")
        base = wid * per_w

        @pl.loop(0, per_w // ch)
        def _(i):
            off = pl.multiple_of(base + i * ch, ch)
            pltpu.sync_copy(rows_hbm.at[pl.ds(off, ch)], rows_v)
            pltpu.sync_copy(dest_hbm.at[:, pl.ds(off, ch)], idx_v)
            copies = [pltpu.async_copy(rows_v, out_hbm.at[idx_v.at[k]], sem) for k in range(K)]
            for cp in copies:
                cp.wait()

    return scatter(rows, dest_t)


def _expert_kernel(po_ref, cnt_ref, xs_hbm, w1_ref, w3_ref, w2_ref, ys_hbm,
                   xbuf, ybuf, w1b, w3b, w2b, gcnt, insem, outsem):
    e = pl.program_id(0)
    ne = pl.num_programs(0)
    n = cnt_ref[e]
    base = po_ref[e]
    rb = xbuf.shape[1]
    hw = D_MODEL // 2
    nblk = jnp.maximum((n + rb - 1) // rb, 1)
    w1b[...] = w1_ref[...].astype(BF16)
    w3b[...] = w3_ref[...].astype(BF16)
    w2b[...] = w2_ref[...].astype(BF16)

    def in_copy(r0, slot):
        return pltpu.make_async_copy(xs_hbm.at[pl.ds(pl.multiple_of(r0, rb), rb)], xbuf.at[slot], insem.at[slot])

    def out_copy(r0, slot):
        return pltpu.make_async_copy(ybuf.at[slot], ys_hbm.at[pl.ds(pl.multiple_of(r0, rb), rb)], outsem.at[slot])

    @pl.when(e == 0)
    def _():
        gcnt[0] = 0
        in_copy(base, 0).start()

    g0 = gcnt[0]

    def body(j, c):
        slot = (g0 + j) % 2

        @pl.when(j + 1 < nblk)
        def _():
            in_copy(base + (j + 1) * rb, 1 - slot).start()

        @pl.when((j + 1 == nblk) & (e + 1 < ne))
        def _():
            in_copy(po_ref[jnp.minimum(e + 1, ne - 1)], 1 - slot).start()

        in_copy(0, slot).wait()

        @pl.when(g0 + j >= 2)
        def _():
            out_copy(0, slot).wait()

        words = xbuf[slot]
        row = j * rb + lax.broadcasted_iota(I32, words.shape, 0)
        lo, hi = _unpack_bf16_pairs(jnp.where(row < n, words, 0))
        lo, hi = lo.astype(BF16), hi.astype(BF16)
        a = (jnp.dot(lo, w1b[0:hw, :], preferred_element_type=F32) +
             jnp.dot(hi, w1b[hw:, :], preferred_element_type=F32))
        g = (jnp.dot(lo, w3b[0:hw, :], preferred_element_type=F32) +
             jnp.dot(hi, w3b[hw:, :], preferred_element_type=F32))
        h = (a * jax.nn.sigmoid(a) * g).astype(BF16)
        ybuf[slot] = _pack_bf16_pairs(jnp.dot(h, w2b[...], preferred_element_type=F32))
        out_copy(base + j * rb, slot).start()
        return c

    lax.fori_loop(0, nblk, body, 0)
    total = g0 + nblk
    gcnt[0] = total

    @pl.when(e + 1 == ne)
    def _():
        @pl.when(total >= 2)
        def _():
            out_copy(0, total % 2).wait()

        out_copy(0, (total - 1) % 2).wait()


def _experts(poffs, counts, xs, w1, w3, w2):
    n_rows, wp = xs.shape
    rb = ROW_BLOCK
    wspec = lambda shape: pl.BlockSpec((None,) + shape, lambda e, po, cn: (e, 0, 0))
    gs = pltpu.PrefetchScalarGridSpec(
        num_scalar_prefetch=2, grid=(MOE_E,),
        in_specs=[pl.BlockSpec(memory_space=pl.ANY), wspec((D_MODEL, MOE_FF)), wspec((D_MODEL, MOE_FF)),
                  wspec((MOE_FF, D_MODEL))],
        out_specs=pl.BlockSpec(memory_space=pl.ANY),
        scratch_shapes=[pltpu.VMEM((2, rb, wp), I32), pltpu.VMEM((2, rb, wp), I32),
                        pltpu.VMEM((D_MODEL, MOE_FF), BF16), pltpu.VMEM((D_MODEL, MOE_FF), BF16),
                        pltpu.VMEM((MOE_FF, D_MODEL), BF16), pltpu.SMEM((1,), I32),
                        pltpu.SemaphoreType.DMA((2,)), pltpu.SemaphoreType.DMA((2,))])
    return pl.pallas_call(
        _expert_kernel, grid_spec=gs, out_shape=jax.ShapeDtypeStruct((n_rows, wp), I32),
        compiler_params=_params(("arbitrary",)), name="experts")(poffs, counts, xs, w1, w3, w2)


def _row_gather(table, idx):
    n, d = idx.shape[0], table.shape[1]
    nw = SC_CORES * SC_SUBCORES
    per_w = n // nw
    ch = SC_GATHER_ROWS // 2
    mesh = plsc.VectorSubcoreMesh(core_axis_name="c", subcore_axis_name="s")

    @functools.partial(
        pl.kernel, mesh=mesh, out_type=jax.ShapeDtypeStruct((n, d), I32),
        scratch_types=[pltpu.VMEM((ch,), I32), pltpu.VMEM((ch,), I32),
                       pltpu.VMEM((ch, d), I32), pltpu.VMEM((ch, d), I32),
                       pltpu.SemaphoreType.DMA, pltpu.SemaphoreType.DMA,
                       pltpu.SemaphoreType.DMA, pltpu.SemaphoreType.DMA],
        name="row_gather")
    def gather(table_hbm, idx_hbm, out_hbm, idx0, idx1, rows0, rows1, g0, g1, w0, w1):
        wid = lax.axis_index("s") * SC_CORES + lax.axis_index("c")
        base = wid * per_w

        @pl.loop(0, per_w // (2 * ch))
        def _(i):
            off0 = pl.multiple_of(base + 2 * i * ch, ch)
            off1 = pl.multiple_of(off0 + ch, ch)
            pltpu.sync_copy(idx_hbm.at[pl.ds(off0, ch)], idx0)
            c0 = pltpu.async_copy(table_hbm.at[idx0], rows0, g0)
            pltpu.sync_copy(idx_hbm.at[pl.ds(off1, ch)], idx1)
            c1 = pltpu.async_copy(table_hbm.at[idx1], rows1, g1)
            c0.wait()
            o0 = pltpu.async_copy(rows0, out_hbm.at[pl.ds(off0, ch)], w0)
            c1.wait()
            o1 = pltpu.async_copy(rows1, out_hbm.at[pl.ds(off1, ch)], w1)
            o0.wait()
            o1.wait()

    return gather(table, idx)


def _combine_kernel(w_ref, tile_ref, x_ref, ysg_ref, s1_ref, s3_ref, s2_ref, g_ref, b_ref, o_ref):
    x2 = x_ref[...]
    xb = x2.astype(BF16)
    a = jnp.dot(xb, s1_ref[...], preferred_element_type=F32)
    c = jnp.dot(xb, s3_ref[...], preferred_element_type=F32)
    shared = jnp.dot((a * jax.nn.sigmoid(a) * c).astype(BF16), s2_ref[...], preferred_element_type=F32)
    lo, hi = _unpack_bf16_pairs(ysg_ref[...])
    lo, hi = lo.astype(BF16), hi.astype(BF16)
    wrep = jnp.dot(w_ref[...], tile_ref[...], precision=lax.Precision.HIGHEST, preferred_element_type=F32)
    col = lax.broadcasted_iota(I32, wrep.shape, 1)
    row = lax.broadcasted_iota(I32, wrep.shape, 0)
    wsel = jnp.where(col // MOE_K == row, wrep, 0.0)
    wh = wsel.astype(BF16)
    wl = (wsel - wh.astype(F32)).astype(BF16)
    y_lo = jnp.dot(wh, lo, preferred_element_type=F32) + jnp.dot(wl, lo, preferred_element_type=F32)
    y_hi = jnp.dot(wh, hi, preferred_element_type=F32) + jnp.dot(wl, hi, preferred_element_type=F32)
    y = shared + jnp.concatenate([y_lo, y_hi], axis=1)
    o_ref[...] = _ln_rows(ALPHA * x2 + y, g_ref[...], b_ref[...])


def _combine(w_nat, x2, ysg, sw1, sw3, sw2, g, b, tc):
    T = x2.shape[0]
    wp = ysg.shape[1]
    full = lambda a: pl.BlockSpec(a.shape, lambda i: (0,) * a.ndim)
    s1, s3, s2 = sw1.astype(BF16), sw3.astype(BF16), sw2.astype(BF16)
    g2, b2 = g.reshape(1, -1), b.reshape(1, -1)
    tile = jnp.asarray(np.arange(tc * MOE_K)[None, :] % MOE_K == np.arange(MOE_K)[:, None], F32)
    return pl.pallas_call(
        _combine_kernel, grid=(T // tc,),
        in_specs=[pl.BlockSpec((tc, MOE_K), lambda i: (i, 0)), full(tile),
                  pl.BlockSpec((tc, D_MODEL), lambda i: (i, 0)),
                  pl.BlockSpec((tc * MOE_K, wp), lambda i: (i, 0)),
                  full(s1), full(s3), full(s2), full(g2), full(b2)],
        out_specs=pl.BlockSpec((tc, D_MODEL), lambda i: (i, 0)),
        out_shape=jax.ShapeDtypeStruct((T, D_MODEL), F32),
        compiler_params=_params(("parallel",)), name="combine",
    )(w_nat, tile, x2, ysg, s1, s3, s2, g2, b2)


def _moe(x2, x2p, scores_t, router_bias, w1, w3, w2, sw1, sw3, sw2, g, b, tiles):
    T = x2.shape[0]
    rb = ROW_BLOCK
    idx_t, w_t = _route(scores_t, router_bias, tiles['route'])
    rank_t, counts = _rank(idx_t, tiles['rank'])
    pcounts = jnp.maximum((counts + rb - 1) // rb, 1) * rb
    pends = jnp.cumsum(pcounts)
    poffs = (pends - pcounts).astype(I32)
    n_blocks = -(-T * MOE_K // rb) + MOE_E
    dest_t = _dest(idx_t, rank_t, poffs, tiles['rank'])
    xs = _row_scatter(x2p, dest_t, n_blocks * rb)
    ys = _experts(poffs, counts.astype(I32), xs, w1, w3, w2)
    ysg = _row_gather(ys, dest_t.T.reshape(-1))
    return _combine(w_t.T, x2, ysg, sw1, sw3, sw2, g, b, tiles['combine'])


def _tiles(B, S):
    T = B * S
    pick = lambda want, n: want if n % want == 0 else n
    return dict(proj=pick(512, T), nsa_q=pick(128, S), nsa_ck=pick(512, S), outproj=pick(512, T),
                mlstm_nb=2 if B % 2 == 0 else 1, xattn=pick(512, S), route=pick(512, T), rank=pick(512, T), scatter=pick(256, T),
                combine=pick(128, T))


def kernel(x, mem, w_in, nsa_pos_k, nsa_cmp_k_w1, nsa_cmp_k_w2, nsa_pos_v, nsa_cmp_v_w1, nsa_cmp_v_w2,
           mlstm_conv_w, mlstm_i_bias, mlstm_f_bias, mlstm_norm_g, w_out, ln1_g, ln1_b,
           xa_wq, xa_wk, xa_wv, xa_wo, ln2_g, ln2_b, router_w, router_bias,
           moe_w1, moe_w3, moe_w2, shared_w1, shared_w3, shared_w2, ln3_g, ln3_b):
    B, S, D = x.shape
    T = B * S
    tl = _tiles(B, S)
    xc = x.reshape(T, D)
    memc = mem.reshape(B * MEM_LEN, D)
    for l in range(w_in.shape[0]):
        (q, cmp, ksel, vsel, kwin, vwin, gates, mq, mk, mv, mo, mif) = _project(xc, _prep_w_in(w_in[l]), tl['proj'])
        kcvc = _compress(cmp, B, S, _prep_cmp(nsa_pos_k[l], nsa_cmp_k_w1[l], nsa_cmp_k_w2[l],
                                              nsa_pos_v[l], nsa_cmp_v_w1[l], nsa_cmp_v_w2[l]))
        y_nsa = _nsa(q, kcvc, ksel, vsel, kwin, vwin, gates, B, S, tl['nsa_q'], tl['nsa_ck'])
        y_ml = _mlstm(mq, mk, mv, mo, mif, mlstm_conv_w[l], mlstm_i_bias[l], mlstm_f_bias[l],
                      mlstm_norm_g[l], B, S, tl['mlstm_nb'])
        x1 = _outproj(y_nsa, y_ml, xc, w_out[l], ln1_g[l], ln1_b[l], tl['outproj'])
        kv = _memkv(memc, xa_wk[l], xa_wv[l])
        x2, x2p, scores_t = _xattn(x1, kv, xa_wq[l], xa_wo[l], ln2_g[l], ln2_b[l], router_w[l], S, tl['xattn'])
        xc = _moe(x2, x2p, scores_t, router_bias[l], moe_w1[l], moe_w3[l], moe_w2[l],
                  shared_w1[l], shared_w3[l], shared_w2[l], ln3_g[l], ln3_b[l], tl)
    return xc.reshape(B, S, D)
```

```python
import functools
import numpy as np
import jax
import jax.numpy as jnp
from jax import lax
from jax.experimental import pallas as pl
from jax.experimental.pallas import tpu as pltpu
from jax.experimental.pallas import tpu_sc as plsc

F32 = jnp.float32
BF16 = jnp.bfloat16
I32 = jnp.int32

D_MODEL = 1024
MEM_LEN = 256
NSA_HEADS = 8
NSA_GROUPS = 2
NSA_HPG = 4
NSA_DK = 64
NSA_CMP_LEN = 32
NSA_CMP_STRIDE = 16
NSA_SEL_BLOCK = 64
NSA_SEL_TOPN = 8
NSA_WINDOW = 512
ML_HEADS = 4
ML_DH = 128
ML_CHUNK = 64
ML_CONV = 4
XA_HEADS = 4
XA_DH = 256
MOE_E = 256
MOE_K = 8
MOE_GROUPS = 8
MOE_TOPK_GROUPS = 4
MOE_FF = 256
MOE_ROUTE_SCALE = 2.5
DEPTH = 1
ALPHA = (2.0 * DEPTH) ** 0.25
LN_EPS = 1e-5
NEG = -1e30
FORCE_BONUS = 1e4

LANES = 128
ROW_BLOCK = 512
VMEM_LIMIT = 56 * 1024 * 1024
SC_CORES = 2
SC_SUBCORES = 16
SC_GATHER_ROWS = 128

_DN_T = (((1,), (1,)), ((), ()))
_DN_TA = (((0,), (0,)), ((), ()))


def _params(sem):
    return pltpu.CompilerParams(dimension_semantics=sem, vmem_limit_bytes=VMEM_LIMIT)


def _ln_rows(v, g, b):
    mu = jnp.mean(v, axis=-1, keepdims=True)
    d = v - mu
    var = jnp.mean(d * d, axis=-1, keepdims=True)
    return d * lax.rsqrt(var + LN_EPS) * g + b


_SEGS = (('q', 512, BF16), ('cmp', 256, F32), ('ksel', 128, BF16), ('vsel', 128, BF16),
         ('kwin', 128, BF16), ('vwin', 128, BF16), ('gates', 128, F32), ('mq', 512, BF16),
         ('mk', 512, BF16), ('mv', 512, BF16), ('mo', 512, BF16), ('mif', 128, F32))


def _proj_kernel(x_ref, w_ref, *out_refs):
    xb = x_ref[...].astype(BF16)
    off = 0
    for o_ref, (_, wd, _) in zip(out_refs, _SEGS):
        o_ref[...] = jnp.dot(xb, w_ref[:, off:off + wd], preferred_element_type=F32).astype(o_ref.dtype)
        off += wd


def _prep_w_in(w):
    sizes = (512,) + (128,) * 6 + (24,) + (512,) * 4 + (4, 4)
    cuts = np.cumsum(sizes)[:-1].tolist()
    (wq, kc, vc, ks, vs, kw, vw, wg, mq, mk, mv, mo, mi, mf) = jnp.split(w, cuts, axis=1)
    wq = wq.reshape(D_MODEL, NSA_GROUPS, NSA_HPG, NSA_DK).transpose(0, 2, 1, 3).reshape(D_MODEL, 512)
    pad = lambda a: jnp.pad(a, ((0, 0), (0, LANES - a.shape[1])))
    segs = [wq, kc, vc, ks, vs, kw, vw, pad(wg), mq, mk, mv, mo, pad(jnp.concatenate([mi, mf], axis=1))]
    return jnp.concatenate(segs, axis=1).astype(BF16)


def _project(x2d, w_all, tm):
    T = x2d.shape[0]
    n = w_all.shape[1]
    out_shape = tuple(jax.ShapeDtypeStruct((T, wd), dt) for _, wd, dt in _SEGS)
    out_specs = tuple(pl.BlockSpec((tm, wd), lambda i: (i, 0)) for _, wd, _ in _SEGS)
    return pl.pallas_call(
        _proj_kernel, grid=(T // tm,),
        in_specs=[pl.BlockSpec((tm, D_MODEL), lambda i: (i, 0)),
                  pl.BlockSpec((D_MODEL, n), lambda i: (0, 0))],
        out_specs=out_specs, out_shape=out_shape,
        compiler_params=_params(("parallel",)), name="proj")(x2d, w_all)


def _cmp_kernel(r_ref, pa_ref, pb_ref, wa_ref, wb_ref, w2_ref, o_ref):
    r = r_ref[...]
    a = jnp.dot((r + pa_ref[...]).astype(BF16), wa_ref[...], preferred_element_type=F32)
    b = jnp.dot((r + pb_ref[...]).astype(BF16), wb_ref[...], preferred_element_type=F32)
    nr = r.shape[0]
    hid = a + pltpu.roll(b, nr - 1, 0)
    hid = hid * jax.nn.sigmoid(hid)
    out = jnp.dot(hid.astype(BF16), w2_ref[...], preferred_element_type=F32)
    row = lax.broadcasted_iota(I32, out.shape, 0)
    o_ref[...] = jnp.where(row < nr - 1, out, 0.0).astype(o_ref.dtype)


def _prep_cmp(pos_k, w1_k, w2_k, pos_v, w1_v, w2_v):
    eye = jnp.eye(NSA_GROUPS, dtype=F32)

    def expand_w1(w1, half):
        w = w1.reshape(NSA_CMP_LEN, NSA_DK, NSA_DK)[half * 16:(half + 1) * 16]
        return jnp.einsum('jde,gk->jgdke', w, eye).reshape(16, 128, 128)

    def both(fk, fv):
        z = jnp.zeros_like(fk)
        top = jnp.concatenate([fk, z], axis=-1)
        bot = jnp.concatenate([z, fv], axis=-1)
        return jnp.concatenate([top, bot], axis=-2)

    wa = both(expand_w1(w1_k, 0), expand_w1(w1_v, 0)).reshape(16 * 256, 256).astype(BF16)
    wb = both(expand_w1(w1_k, 1), expand_w1(w1_v, 1)).reshape(16 * 256, 256).astype(BF16)
    w2 = both(jnp.kron(eye, w2_k), jnp.kron(eye, w2_v)).astype(BF16)

    def pos_row(half):
        pk = jnp.tile(pos_k[half * 16:(half + 1) * 16], (1, NSA_GROUPS))
        pv = jnp.tile(pos_v[half * 16:(half + 1) * 16], (1, NSA_GROUPS))
        return jnp.concatenate([pk, pv], axis=1).reshape(1, 16 * 256)

    return pos_row(0), pos_row(1), wa, wb, w2


def _compress(cmp2d, B, S, prep):
    pa, pb, wa, wb, w2 = prep
    nr = S // NSA_CMP_STRIDE
    r = cmp2d.reshape(B, nr, NSA_CMP_STRIDE * 256)
    full = lambda a: pl.BlockSpec(a.shape, lambda b: (0,) * a.ndim)
    return pl.pallas_call(
        _cmp_kernel, grid=(B,),
        in_specs=[pl.BlockSpec((None, nr, NSA_CMP_STRIDE * 256), lambda b: (b, 0, 0)),
                  full(pa), full(pb), full(wa), full(wb), full(w2)],
        out_specs=pl.BlockSpec((None, nr, 256), lambda b: (b, 0, 0)),
        out_shape=jax.ShapeDtypeStruct((B, nr, 256), BF16),
        compiler_params=_params(("parallel",)), name="cmp")(r, pa, pb, wa, wb, w2)


def _nsa_consts(S):
    n_cmp = (S - NSA_CMP_LEN) // NSA_CMP_STRIDE + 1
    n_sel = S // NSA_SEL_BLOCK
    cs = np.arange(n_cmp) * NSA_CMP_STRIDE
    ss = np.arange(n_sel) * NSA_SEL_BLOCK
    ov = ((cs[:, None] < ss[None, :] + NSA_SEL_BLOCK) & (cs[:, None] + NSA_CMP_LEN > ss[None, :]))
    ovt = np.zeros((LANES, S // NSA_CMP_STRIDE), np.float32)
    ovt[:n_sel, :n_cmp] = ov.T
    e = np.zeros((LANES, S), np.float32)
    e[np.arange(S) // NSA_SEL_BLOCK, np.arange(S)] = 1.0
    return jnp.asarray(ovt, BF16), jnp.asarray(e, BF16)


def _nsa_kernel(q_ref, kcvc_ref, ksel_ref, vsel_ref, kwin_ref, vwin_ref, gates_ref, ovt_ref, e_ref,
                y_ref, bias_ref, *, tq, ck, n_sel):
    G, H = NSA_GROUPS, NSA_HPG
    GH = G * H
    M = GH * tq
    S = ksel_ref.shape[0]
    W = NSA_WINDOW
    ws = min(W + tq, S)
    t0 = pl.program_id(1) * tq
    gates = jax.nn.sigmoid(gates_ref[...])
    lane = lax.broadcasted_iota(I32, (tq, LANES), 1)
    t_col = t0 + lax.broadcasted_iota(I32, (tq, 1), 0)
    kc = kcvc_ref[:, 0:LANES]
    vc = kcvc_ref[:, LANES:2 * LANES]
    nc = kc.shape[0]
    qs = []
    for g in range(G):
        gmask = (lane // NSA_DK) == g
        for h in range(H):
            qh = q_ref[:, h * LANES:(h + 1) * LANES] * (NSA_DK ** -0.5)
            qs.append(jnp.where(gmask, qh, jnp.zeros_like(qh)))
    Q = jnp.concatenate(qs, axis=0).astype(BF16)

    s = lax.dot_general(Q, kc, _DN_T, preferred_element_type=F32)
    c_idx = lax.broadcasted_iota(I32, (tq, nc), 1)
    cmask = (c_idx * NSA_CMP_STRIDE + NSA_CMP_LEN - 1) <= t_col
    s3 = jnp.where(cmask[None], s.reshape(GH, tq, nc), NEG)
    p = jnp.exp(s3 - jnp.max(s3, axis=-1, keepdims=True))
    p = p / jnp.sum(p, axis=-1, keepdims=True)
    p = jnp.where(cmask[None], p, 0.0)
    o_cmp = jnp.dot(p.reshape(M, nc).astype(BF16), vc, preferred_element_type=F32).reshape(GH, tq, LANES)

    for g in range(G):
        psum = jnp.sum(p[g * H:(g + 1) * H], axis=0)
        hi = psum.astype(BF16)
        lo = (psum - hi.astype(F32)).astype(BF16)
        ovt = ovt_ref[...]
        pslt = (lax.dot_general(ovt, hi, _DN_T, preferred_element_type=F32) +
                lax.dot_general(ovt, lo, _DN_T, preferred_element_type=F32))
        imp_p = pslt[0:n_sel, :]
        n_i = lax.broadcasted_iota(I32, (n_sel, tq), 0)
        cur = (t0 + lax.broadcasted_iota(I32, (n_sel, tq), 1)) // NSA_SEL_BLOCK
        forced = (n_i == 0) | (n_i == cur) | (n_i == cur - 1)
        imp = jnp.where(n_i <= cur, imp_p + jnp.where(forced, FORCE_BONUS, 0.0), NEG)
        cnt = jnp.zeros((n_sel, tq), F32)
        for m in range(n_sel):
            row = imp[m:m + 1, :]
            beats = (row > imp) | ((row == imp) & (n_i > m))
            cnt = cnt + jnp.where(beats, 1.0, 0.0)
        selt = jnp.where(cnt < float(min(NSA_SEL_TOPN, n_sel)), 1.0, 0.0)
        selt = jnp.concatenate([selt, jnp.zeros((LANES - n_sel, tq), F32)], axis=0)
        sel = selt.T.astype(BF16)
        maskf = jnp.dot(sel, e_ref[...], preferred_element_type=F32)
        kpos = lax.broadcasted_iota(I32, (tq, S), 1)
        bias_ref[g] = jnp.where((maskf > 0.5) & (kpos <= t_col), 0.0, NEG)

    vlane = lax.broadcasted_iota(I32, (1, LANES), 1) // NSA_DK

    def pv_with_sums(pb, v):
        outs = []
        for g in range(G):
            vg = jnp.where(vlane == g, v, jnp.ones_like(v))
            outs.append(jnp.dot(pb[g * H * tq:(g + 1) * H * tq], vg, preferred_element_type=F32))
        return jnp.concatenate(outs, axis=0)

    def normalise(acc):
        outs = []
        for r in range(GH):
            c = NSA_DK * (1 - r // H)
            outs.append(acc[r] / acc[r][:, c:c + 1])
        return outs

    def sel_body(j, carry):
        m_i, acc = carry
        ks = pl.multiple_of(j * ck, ck)
        k = ksel_ref[pl.ds(ks, ck), :]
        v = vsel_ref[pl.ds(ks, ck), :]
        sj = lax.dot_general(Q, k, _DN_T, preferred_element_type=F32)
        sj = (sj.reshape(G, H, tq, ck) + bias_ref[:, :, pl.ds(ks, ck)][:, None]).reshape(GH, tq, ck)
        m_new = jnp.maximum(m_i, jnp.max(sj, axis=-1, keepdims=True))
        a = jnp.exp(m_i - m_new)
        pj = jnp.exp(sj - m_new).reshape(M, ck).astype(BF16)
        return m_new, a * acc + pv_with_sums(pj, v).reshape(GH, tq, LANES)

    init = (jnp.full((GH, tq, 1), NEG, F32), jnp.zeros((GH, tq, LANES), F32))
    _, acc = lax.fori_loop(0, (t0 + tq + ck - 1) // ck, sel_body, init)
    o_sel = normalise(acc)

    kst = pl.multiple_of(jnp.clip(t0 - W, 0, S - ws), LANES)
    kwn = kwin_ref[pl.ds(kst, ws), :]
    vwn = vwin_ref[pl.ds(kst, ws), :]
    sw = lax.dot_general(Q, kwn, _DN_T, preferred_element_type=F32)
    wpos = kst + lax.broadcasted_iota(I32, (tq, ws), 1)
    wmask = (wpos <= t_col) & (wpos > t_col - W)
    sw3 = jnp.where(wmask[None], sw.reshape(GH, tq, ws), NEG)
    pw = jnp.exp(sw3 - jnp.max(sw3, axis=-1, keepdims=True)).reshape(M, ws).astype(BF16)
    o_win = normalise(pv_with_sums(pw, vwn).reshape(GH, tq, LANES))

    g0mask = lane < NSA_DK
    for h in range(H):
        o_g = []
        for g in range(G):
            r = g * H + h
            c0 = r * 3
            o_g.append(gates[:, c0:c0 + 1] * o_cmp[r] + gates[:, c0 + 1:c0 + 2] * o_sel[r] +
                       gates[:, c0 + 2:c0 + 3] * o_win[r])
        y_ref[:, h * LANES:(h + 1) * LANES] = jnp.where(g0mask, o_g[0], o_g[1]).astype(y_ref.dtype)


def _nsa(q, kcvc, ksel, vsel, kwin, vwin, gates, B, S, tq, ck):
    T = B * S
    nq = S // tq
    ovt, e = _nsa_consts(S)
    seq = lambda a: a.reshape(B, S, LANES)
    kv_spec = pl.BlockSpec((None, S, LANES), lambda b, i: (b, 0, 0))
    kern = functools.partial(_nsa_kernel, tq=tq, ck=ck, n_sel=S // NSA_SEL_BLOCK)
    return pl.pallas_call(
        kern, grid=(B, nq),
        in_specs=[pl.BlockSpec((tq, 512), lambda b, i: (b * nq + i, 0)),
                  pl.BlockSpec((None,) + kcvc.shape[1:], lambda b, i: (b, 0, 0)),
                  kv_spec, kv_spec, kv_spec, kv_spec,
                  pl.BlockSpec((tq, LANES), lambda b, i: (b * nq + i, 0)),
                  pl.BlockSpec(ovt.shape, lambda b, i: (0, 0)),
                  pl.BlockSpec(e.shape, lambda b, i: (0, 0))],
        out_specs=pl.BlockSpec((tq, 512), lambda b, i: (b * nq + i, 0)),
        out_shape=jax.ShapeDtypeStruct((T, 512), BF16),
        scratch_shapes=[pltpu.VMEM((NSA_GROUPS, tq, S), F32)],
        compiler_params=_params(("parallel", "parallel")), name="nsa",
    )(q, kcvc, seq(ksel), seq(vsel), seq(kwin), seq(vwin), gates, ovt, e)


def _mlstm_kernel(q_ref, k_ref, v_ref, o_ref, gn_ref, gt_ref, cw_ref, bn_ref, bt_ref, ng_ref, tri_ref,
                  y_ref, c_scr, n_scr):
    H, dh, L = ML_HEADS, ML_DH, ML_CHUNK
    nb, S = q_ref.shape[0], q_ref.shape[1]
    nchunk = S // L
    c_scr[...] = jnp.zeros_like(c_scr)
    n_scr[...] = jnp.zeros_like(n_scr)
    row = lax.broadcasted_iota(I32, (L, H * dh), 0)
    li = lax.broadcasted_iota(I32, (L, L), 0)
    mi = lax.broadcasted_iota(I32, (L, L), 1)
    causal = mi <= li
    tril = tri_ref[0]
    triu = tri_ref[1]
    hp = lax.Precision.HIGHEST

    def conv_silu(ref, bi, c, wofs):
        r0 = pl.multiple_of(c * L, L)
        rp = pl.multiple_of(jnp.maximum(c - 1, 0) * L, L)
        cur = ref[bi, pl.ds(r0, L), :].astype(F32)
        prev = ref[bi, pl.ds(rp, L), :].astype(F32) * jnp.where(c > 0, 1.0, 0.0)
        acc = cur * cw_ref[ML_CONV - 1:ML_CONV, wofs:wofs + H * dh]
        for j in range(1, ML_CONV):
            sh = jnp.where(row < j, pltpu.roll(prev, j, 0), pltpu.roll(cur, j, 0))
            acc = acc + sh * cw_ref[ML_CONV - 1 - j:ML_CONV - j, wofs:wofs + H * dh]
        return acc * jax.nn.sigmoid(acc)

    def body(c, m_state):
        r0 = pl.multiple_of(c * L, L)
        new_m = []
        for bi in range(nb):
            qa = conv_silu(q_ref, bi, c, 0) * (dh ** -0.5)
            ka = conv_silu(k_ref, bi, c, H * dh)
            va = v_ref[bi, pl.ds(r0, L), :]
            oa = o_ref[bi, pl.ds(r0, L), :].astype(F32)
            gn = gn_ref[bi, pl.ds(r0, L), :] + bn_ref[...]
            gt = gt_ref[bi, :, c, :] + bt_ref[...]
            lf_n = jax.nn.log_sigmoid(gn)
            lf_t = jax.nn.log_sigmoid(gt)
            b_n = jnp.dot(tril, lf_n, precision=hp, preferred_element_type=F32)
            b_t = jnp.dot(lf_t, triu, precision=hp, preferred_element_type=F32)
            for h in range(H):
                st = bi * H + h
                q = qa[:, h * dh:(h + 1) * dh]
                k = ka[:, h * dh:(h + 1) * dh]
                v = va[:, h * dh:(h + 1) * dh]
                m_old = m_state[st]
                b_col = b_n[:, H + h:H + h + 1]
                i_col = gn[:, h:h + 1]
                b_row = b_t[H + h:H + h + 1, :]
                i_row = gt[h:h + 1, :]
                g_tot = b_t[H + h:H + h + 1, L - 1:L]
                d_log = jnp.where(causal, b_col - b_row + i_row, NEG)
                inter = b_col + m_old
                m_q = jnp.maximum(inter, jnp.max(d_log, axis=-1, keepdims=True))
                w_intra = jnp.exp(d_log - m_q)
                w_inter = jnp.exp(inter - m_q)
                qb = q.astype(BF16)
                s = lax.dot_general(qb, k.astype(BF16), _DN_T, preferred_element_type=F32) * w_intra
                cst = c_scr[st]
                nst = n_scr[st]
                num = (w_inter * jnp.dot(qb, cst.astype(BF16), preferred_element_type=F32) +
                       jnp.dot(s.astype(BF16), v, preferred_element_type=F32))
                den = w_inter * jnp.sum(q * nst, axis=-1, keepdims=True) + jnp.sum(s, axis=-1, keepdims=True)
                hv = num / jnp.maximum(jnp.abs(den), jnp.exp(-m_q))
                log_k = g_tot - b_col + i_col
                m_new = jnp.maximum(g_tot + m_old, jnp.max(log_k, axis=0, keepdims=True))
                wk = jnp.exp(log_k - m_new)
                decay = jnp.exp(g_tot + m_old - m_new)
                kw = k * wk
                c_scr[st] = decay * cst + lax.dot_general(kw.astype(BF16), v, _DN_TA, preferred_element_type=F32)
                n_scr[st] = decay * nst + jnp.sum(kw, axis=0, keepdims=True)
                new_m.append(m_new)
                mu = jnp.mean(hv, axis=-1, keepdims=True)
                dv = hv - mu
                var = jnp.mean(dv * dv, axis=-1, keepdims=True)
                hn = dv * lax.rsqrt(var + LN_EPS) * ng_ref[:, h * dh:(h + 1) * dh]
                og = jax.nn.sigmoid(oa[:, h * dh:(h + 1) * dh])
                y_ref[bi, pl.ds(r0, L), h * dh:(h + 1) * dh] = (og * hn).astype(y_ref.dtype)
        return tuple(new_m)

    lax.fori_loop(0, nchunk, body, tuple(jnp.zeros((1, 1), F32) for _ in range(nb * H)))


def _mlstm(mq, mk, mv, mo, mif, conv_w, i_bias, f_bias, norm_g, B, S, nb):
    T = B * S
    H, dh, L = ML_HEADS, ML_DH, ML_CHUNK
    W = H * dh
    gt = mif[:, :2 * H].reshape(B, S, 2 * H).transpose(0, 2, 1).reshape(B, 2 * H, S // L, L)
    cw = conv_w.reshape(ML_CONV, 2 * W)
    bias = jnp.concatenate([i_bias, f_bias])
    bn = jnp.pad(bias, (0, LANES - 2 * H)).reshape(1, LANES)
    bt = bias.reshape(2 * H, 1)
    ng = norm_g.reshape(1, W)
    tri = jnp.stack([jnp.tril(jnp.ones((L, L), F32)), jnp.triu(jnp.ones((L, L), F32))])
    seq = lambda a: a.reshape(B, S, a.shape[1])
    rows = lambda w: pl.BlockSpec((nb, S, w), lambda b: (b, 0, 0))
    full = lambda a: pl.BlockSpec(a.shape, lambda b: (0,) * a.ndim)
    y = pl.pallas_call(
        _mlstm_kernel, grid=(B // nb,),
        in_specs=[rows(W), rows(W), rows(W), rows(W), rows(LANES),
                  pl.BlockSpec((nb, 2 * H, S // L, L), lambda b: (b, 0, 0, 0)),
                  full(cw), full(bn), full(bt), full(ng), full(tri)],
        out_specs=rows(W),
        out_shape=jax.ShapeDtypeStruct((B, S, W), BF16),
        scratch_shapes=[pltpu.VMEM((nb * H, dh, dh), F32), pltpu.VMEM((nb * H, 1, dh), F32)],
        compiler_params=_params(("parallel",)), name="mlstm",
    )(seq(mq), seq(mk), seq(mv), seq(mo), seq(mif), gt, cw, bn, bt, ng, tri)
    return y.reshape(T, W)


def _outproj_kernel(yn_ref, ym_ref, x_ref, w_ref, g_ref, b_ref, o_ref):
    mix = (jnp.dot(yn_ref[...], w_ref[0:512, :], preferred_element_type=F32) +
           jnp.dot(ym_ref[...], w_ref[512:1024, :], preferred_element_type=F32))
    o_ref[...] = _ln_rows(ALPHA * x_ref[...] + mix, g_ref[...], b_ref[...])


def _outproj(y_nsa, y_ml, x2d, w_out, g, b, tm):
    T = x2d.shape[0]
    wn = w_out[:512].reshape(NSA_GROUPS, NSA_HPG, NSA_DK, D_MODEL).transpose(1, 0, 2, 3).reshape(512, D_MODEL)
    w = jnp.concatenate([wn, w_out[512:]], axis=0).astype(BF16)
    row = lambda wd: pl.BlockSpec((tm, wd), lambda i: (i, 0))
    full = lambda a: pl.BlockSpec(a.shape, lambda i: (0,) * a.ndim)
    g2, b2 = g.reshape(1, -1), b.reshape(1, -1)
    return pl.pallas_call(
        _outproj_kernel, grid=(T // tm,),
        in_specs=[row(512), row(512), row(D_MODEL), full(w), full(g2), full(b2)],
        out_specs=row(D_MODEL), out_shape=jax.ShapeDtypeStruct((T, D_MODEL), F32),
        compiler_params=_params(("parallel",)), name="outproj")(y_nsa, y_ml, x2d, w, g2, b2)


def _memkv_kernel(m_ref, w_ref, o_ref):
    o_ref[...] = jnp.dot(m_ref[...].astype(BF16), w_ref[...], preferred_element_type=F32).astype(o_ref.dtype)


def _memkv(mem2d, wk, wv):
    w = jnp.concatenate([wk, wv], axis=1).astype(BF16)
    R = mem2d.shape[0]
    return pl.pallas_call(
        _memkv_kernel, grid=(R // MEM_LEN,),
        in_specs=[pl.BlockSpec((MEM_LEN, D_MODEL), lambda i: (i, 0)),
                  pl.BlockSpec(w.shape, lambda i: (0, 0))],
        out_specs=pl.BlockSpec((MEM_LEN, 2 * D_MODEL), lambda i: (i, 0)),
        out_shape=jax.ShapeDtypeStruct((R, 2 * D_MODEL), BF16),
        compiler_params=_params(("parallel",)), name="memkv")(mem2d, w)


def _xattn_kernel(x_ref, kv_ref, wq_ref, wo_ref, g_ref, b_ref, rw_ref, x2_ref, x2p_ref, sc_ref):
    x1 = x_ref[...]
    q = jnp.dot(x1.astype(BF16), wq_ref[...], preferred_element_type=F32).astype(BF16)
    outs = []
    for h in range(XA_HEADS):
        qh = q[:, h * XA_DH:(h + 1) * XA_DH]
        kh = kv_ref[:, h * XA_DH:(h + 1) * XA_DH]
        vh = kv_ref[:, D_MODEL + h * XA_DH:D_MODEL + (h + 1) * XA_DH]
        s = lax.dot_general(qh, kh, _DN_T, preferred_element_type=F32) * (XA_DH ** -0.5)
        p = jnp.exp(s - jnp.max(s, axis=-1, keepdims=True))
        p = p / jnp.sum(p, axis=-1, keepdims=True)
        outs.append(jnp.dot(p.astype(BF16), vh, preferred_element_type=F32).astype(BF16))
    o = jnp.concatenate(outs, axis=1)
    xa = jnp.dot(o, wo_ref[...], preferred_element_type=F32)
    x2 = _ln_rows(ALPHA * x1 + xa, g_ref[...], b_ref[...])
    x2_ref[...] = x2
    x2p_ref[...] = _pack_bf16_pairs(x2)
    xh = x2.astype(BF16)
    xl = (x2 - xh.astype(F32)).astype(BF16)
    wh = rw_ref[0]
    wl = rw_ref[1]
    logit = (lax.dot_general(wh, xh, _DN_T, preferred_element_type=F32) +
             lax.dot_general(wh, xl, _DN_T, preferred_element_type=F32) +
             lax.dot_general(wl, xh, _DN_T, preferred_element_type=F32))
    sc_ref[...] = jax.nn.sigmoid(logit)


def _xattn(x1, kv, wq, wo, g, b, router_w, S, tq):
    T = x1.shape[0]
    wqb, wob = wq.astype(BF16), wo.astype(BF16)
    rwt = router_w.T
    rh = rwt.astype(BF16)
    rw = jnp.stack([rh, (rwt - rh.astype(F32)).astype(BF16)])
    g2, b2 = g.reshape(1, -1), b.reshape(1, -1)
    full = lambda a: pl.BlockSpec(a.shape, lambda i: (0,) * a.ndim)
    per = S // tq
    return pl.pallas_call(
        _xattn_kernel, grid=(T // tq,),
        in_specs=[pl.BlockSpec((tq, D_MODEL), lambda i: (i, 0)),
                  pl.BlockSpec((MEM_LEN, 2 * D_MODEL), lambda i: (i // per, 0)),
                  full(wqb), full(wob), full(g2), full(b2), full(rw)],
        out_specs=(pl.BlockSpec((tq, D_MODEL), lambda i: (i, 0)),
                   pl.BlockSpec((tq, D_MODEL // 2), lambda i: (i, 0)),
                   pl.BlockSpec((MOE_E, tq), lambda i: (0, i))),
        out_shape=(jax.ShapeDtypeStruct((T, D_MODEL), F32), jax.ShapeDtypeStruct((T, D_MODEL // 2), I32),
                   jax.ShapeDtypeStruct((MOE_E, T), F32)),
        compiler_params=_params(("parallel",)), name="xattn")(x1, kv, wqb, wob, g2, b2, rw)


def _route_kernel(sc_ref, rb_ref, idx_ref, w_ref):
    E, G = MOE_E, MOE_GROUPS
    per = E // G
    scores = sc_ref[...]
    tr = scores.shape[1]
    biased = scores + rb_ref[...]
    g3 = biased.reshape(G, per, tr)
    j3 = lax.broadcasted_iota(I32, (G, per, tr), 1)
    m1 = jnp.max(g3, axis=1, keepdims=True)
    first = jnp.min(jnp.where(g3 == m1, j3, per), axis=1, keepdims=True)
    m2 = jnp.max(jnp.where(j3 == first, -jnp.inf, g3), axis=1, keepdims=True)
    gs = (m1 + m2).reshape(G, tr)
    gi = lax.broadcasted_iota(I32, (G, tr), 0)
    cnt = jnp.zeros((G, tr), F32)
    for m in range(G):
        row = gs[m:m + 1, :]
        cnt = cnt + jnp.where((row > gs) | ((row == gs) & (gi > m)), 1.0, 0.0)
    gmask = cnt < float(MOE_TOPK_GROUPS)
    masked = jnp.where(gmask[:, None, :], g3, NEG).reshape(E, tr)
    ei = lax.broadcasted_iota(I32, (E, tr), 0)
    idxs, ws = [], []
    for _ in range(MOE_K):
        mx = jnp.max(masked, axis=0, keepdims=True)
        ix = jnp.min(jnp.where(masked == mx, ei, E), axis=0, keepdims=True)
        hit = ei == ix
        ws.append(jnp.sum(jnp.where(hit, scores, 0.0), axis=0, keepdims=True))
        idxs.append(ix)
        masked = jnp.where(hit, -jnp.inf, masked)
    w = jnp.concatenate(ws, axis=0)
    idx_ref[...] = jnp.concatenate(idxs, axis=0)
    w_ref[...] = w / jnp.sum(w, axis=0, keepdims=True) * MOE_ROUTE_SCALE


def _route(scores_t, router_bias, tr):
    E, T = scores_t.shape
    rb = router_bias.reshape(E, 1)
    return pl.pallas_call(
        _route_kernel, grid=(T // tr,),
        in_specs=[pl.BlockSpec((E, tr), lambda i: (0, i)), pl.BlockSpec((E, 1), lambda i: (0, 0))],
        out_specs=(pl.BlockSpec((MOE_K, tr), lambda i: (0, i)), pl.BlockSpec((MOE_K, tr), lambda i: (0, i))),
        out_shape=(jax.ShapeDtypeStruct((MOE_K, T), I32), jax.ShapeDtypeStruct((MOE_K, T), F32)),
        compiler_params=_params(("parallel",)), name="route")(scores_t, rb)


def _rank_kernel(idx_ref, u_ref, rank_ref, cnt_ref, carry):
    E = MOE_E

    @pl.when(pl.program_id(0) == 0)
    def _():
        carry[...] = jnp.zeros_like(carry)

    idx = idx_ref[...]
    tp = idx.shape[1]
    ei = lax.broadcasted_iota(I32, (E, tp), 0)
    hits = [ei == idx[k:k + 1, :] for k in range(MOE_K)]
    onehot = jnp.zeros((E, tp), F32)
    for hit in hits:
        onehot = onehot + jnp.where(hit, 1.0, 0.0)
    pos = jnp.dot(onehot.astype(BF16), u_ref[...], preferred_element_type=F32) + carry[...]
    ranks = [jnp.sum(jnp.where(hit, pos, 0.0), axis=0, keepdims=True) for hit in hits]
    rank_ref[...] = jnp.concatenate(ranks, axis=0).astype(I32)
    total = carry[...] + jnp.sum(onehot, axis=1, keepdims=True)
    carry[...] = total
    cnt_ref[...] = jnp.broadcast_to(total, cnt_ref.shape).astype(I32)


def _rank(idx_t, tp):
    K, T = idx_t.shape
    u = jnp.triu(jnp.ones((tp, tp), F32), k=1).astype(BF16)
    rank, cnt = pl.pallas_call(
        _rank_kernel, grid=(T // tp,),
        in_specs=[pl.BlockSpec((K, tp), lambda i: (0, i)), pl.BlockSpec((tp, tp), lambda i: (0, 0))],
        out_specs=(pl.BlockSpec((K, tp), lambda i: (0, i)), pl.BlockSpec((MOE_E, LANES), lambda i: (0, 0))),
        out_shape=(jax.ShapeDtypeStruct((K, T), I32), jax.ShapeDtypeStruct((MOE_E, LANES), I32)),
        scratch_shapes=[pltpu.VMEM((MOE_E, 1), F32)],
        compiler_params=_params(("arbitrary",)), name="rank")(idx_t, u)
    return rank, cnt[:, 0]


def _dest_kernel(idx_ref, rank_ref, po_ref, dest_ref):
    idx = idx_ref[...]
    tp = idx.shape[1]
    ei = lax.broadcasted_iota(I32, (MOE_E, tp), 0)
    po = po_ref[...]
    base = [jnp.sum(jnp.where(ei == idx[k:k + 1, :], po, 0.0), axis=0, keepdims=True) for k in range(MOE_K)]
    dest_ref[...] = jnp.concatenate(base, axis=0).astype(I32) + rank_ref[...]


def _dest(idx_t, rank_t, poffs, tp):
    K, T = idx_t.shape
    po = poffs.astype(F32).reshape(MOE_E, 1)
    spec = pl.BlockSpec((K, tp), lambda i: (0, i))
    return pl.pallas_call(
        _dest_kernel, grid=(T // tp,),
        in_specs=[spec, spec, pl.BlockSpec((MOE_E, 1), lambda i: (0, 0))],
        out_specs=spec, out_shape=jax.ShapeDtypeStruct((K, T), I32),
        compiler_params=_params(("parallel",)), name="dest")(idx_t, rank_t, po)


def _pack_bf16_pairs(v):
    m = v.shape[1] // 2
    bits = lax.bitcast_convert_type(v.astype(BF16).astype(F32), jnp.uint32)
    return lax.bitcast_convert_type((bits[:, :m] >> 16) | (bits[:, m:] & jnp.uint32(0xFFFF0000)), I32)


def _unpack_bf16_pairs(w):
    w = lax.bitcast_convert_type(w, jnp.uint32)
    lo = lax.bitcast_convert_type(w << 16, F32)
    hi = lax.bitcast_convert_type(w & jnp.uint32(0xFFFF0000), F32)
    return lo, hi


def _row_scatter(rows, dest_t, n_rows):
    T, d = rows.shape
    K = dest_t.shape[0]
    nw = SC_CORES * SC_SUBCORES
    per_w = T // nw
    ch = SC_GATHER_ROWS
    mesh = plsc.VectorSubcoreMesh(core_axis_name="c", subcore_axis_name="s")

    @functools.partial(
        pl.kernel, mesh=mesh, out_type=jax.ShapeDtypeStruct((n_rows, d), I32),
        scratch_types=[pltpu.VMEM((K, ch), I32), pltpu.VMEM((ch, d), I32), pltpu.SemaphoreType.DMA],
        name="row_scatter")
    def scatter(rows_hbm, dest_hbm, out_hbm, idx_v, rows_v, sem):
        wid = lax.axis_index("s") * SC_CORES + lax.axis_index("c")
        base = wid * per_w

        @pl.loop(0, per_w // ch)
        def _(i):
            off = pl.multiple_of(base + i * ch, ch)
            pltpu.sync_copy(rows_hbm.at[pl.ds(off, ch)], rows_v)
            pltpu.sync_copy(dest_hbm.at[:, pl.ds(off, ch)], idx_v)
            copies = [pltpu.async_copy(rows_v, out_hbm.at[idx_v.at[k]], sem) for k in range(K)]
            for cp in copies:
                cp.wait()

    return scatter(rows, dest_t)


def _expert_kernel(po_ref, cnt_ref, xs_hbm, w1_ref, w3_ref, w2_ref, ys_hbm,
                   xbuf, ybuf, w1b, w3b, w2b, gcnt, insem, outsem):
    e = pl.program_id(0)
    ne = pl.num_programs(0)
    n = cnt_ref[e]
    base = po_ref[e]
    rb = xbuf.shape[1]
    hw = D_MODEL // 2
    nblk = jnp.maximum((n + rb - 1) // rb, 1)
    w1b[...] = w1_ref[...].astype(BF16)
    w3b[...] = w3_ref[...].astype(BF16)
    w2b[...] = w2_ref[...].astype(BF16)

    def in_copy(r0, slot):
        return pltpu.make_async_copy(xs_hbm.at[pl.ds(pl.multiple_of(r0, rb), rb)], xbuf.at[slot], insem.at[slot])

    def out_copy(r0, slot):
        return pltpu.make_async_copy(ybuf.at[slot], ys_hbm.at[pl.ds(pl.multiple_of(r0, rb), rb)], outsem.at[slot])

    @pl.when(e == 0)
    def _():
        gcnt[0] = 0
        in_copy(base, 0).start()

    g0 = gcnt[0]

    def body(j, c):
        slot = (g0 + j) % 2

        @pl.when(j + 1 < nblk)
        def _():
            in_copy(base + (j + 1) * rb, 1 - slot).start()

        @pl.when((j + 1 == nblk) & (e + 1 < ne))
        def _():
            in_copy(po_ref[jnp.minimum(e + 1, ne - 1)], 1 - slot).start()

        in_copy(0, slot).wait()

        @pl.when(g0 + j >= 2)
        def _():
            out_copy(0, slot).wait()

        words = xbuf[slot]
        row = j * rb + lax.broadcasted_iota(I32, words.shape, 0)
        lo, hi = _unpack_bf16_pairs(jnp.where(row < n, words, 0))
        lo, hi = lo.astype(BF16), hi.astype(BF16)
        a = (jnp.dot(lo, w1b[0:hw, :], preferred_element_type=F32) +
             jnp.dot(hi, w1b[hw:, :], preferred_element_type=F32))
        g = (jnp.dot(lo, w3b[0:hw, :], preferred_element_type=F32) +
             jnp.dot(hi, w3b[hw:, :], preferred_element_type=F32))
        h = (a * jax.nn.sigmoid(a) * g).astype(BF16)
        ybuf[slot] = _pack_bf16_pairs(jnp.dot(h, w2b[...], preferred_element_type=F32))
        out_copy(base + j * rb, slot).start()
        return c

    lax.fori_loop(0, nblk, body, 0)
    total = g0 + nblk
    gcnt[0] = total

    @pl.when(e + 1 == ne)
    def _():
        @pl.when(total >= 2)
        def _():
            out_copy(0, total % 2).wait()

        out_copy(0, (total - 1) % 2).wait()


def _experts(poffs, counts, xs, w1, w3, w2):
    n_rows, wp = xs.shape
    rb = ROW_BLOCK
    wspec = lambda shape: pl.BlockSpec((None,) + shape, lambda e, po, cn: (e, 0, 0))
    gs = pltpu.PrefetchScalarGridSpec(
        num_scalar_prefetch=2, grid=(MOE_E,),
        in_specs=[pl.BlockSpec(memory_space=pl.ANY), wspec((D_MODEL, MOE_FF)), wspec((D_MODEL, MOE_FF)),
                  wspec((MOE_FF, D_MODEL))],
        out_specs=pl.BlockSpec(memory_space=pl.ANY),
        scratch_shapes=[pltpu.VMEM((2, rb, wp), I32), pltpu.VMEM((2, rb, wp), I32),
                        pltpu.VMEM((D_MODEL, MOE_FF), BF16), pltpu.VMEM((D_MODEL, MOE_FF), BF16),
                        pltpu.VMEM((MOE_FF, D_MODEL), BF16), pltpu.SMEM((1,), I32),
                        pltpu.SemaphoreType.DMA((2,)), pltpu.SemaphoreType.DMA((2,))])
    return pl.pallas_call(
        _expert_kernel, grid_spec=gs, out_shape=jax.ShapeDtypeStruct((n_rows, wp), I32),
        compiler_params=_params(("arbitrary",)), name="experts")(poffs, counts, xs, w1, w3, w2)


def _row_gather(table, idx):
    n, d = idx.shape[0], table.shape[1]
    nw = SC_CORES * SC_SUBCORES
    per_w = n // nw
    ch = SC_GATHER_ROWS // 2
    mesh = plsc.VectorSubcoreMesh(core_axis_name="c", subcore_axis_name="s")

    @functools.partial(
        pl.kernel, mesh=mesh, out_type=jax.ShapeDtypeStruct((n, d), I32),
        scratch_types=[pltpu.VMEM((ch,), I32), pltpu.VMEM((ch,), I32),
                       pltpu.VMEM((ch, d), I32), pltpu.VMEM((ch, d), I32),
                       pltpu.SemaphoreType.DMA, pltpu.SemaphoreType.DMA,
                       pltpu.SemaphoreType.DMA, pltpu.SemaphoreType.DMA],
        name="row_gather")
    def gather(table_hbm, idx_hbm, out_hbm, idx0, idx1, rows0, rows1, g0, g1, w0, w1):
        wid = lax.axis_index("s") * SC_CORES + lax.axis_index("c")
        base = wid * per_w

        @pl.loop(0, per_w // (2 * ch))
        def _(i):
            off0 = pl.multiple_of(base + 2 * i * ch, ch)
            off1 = pl.multiple_of(off0 + ch, ch)
            pltpu.sync_copy(idx_hbm.at[pl.ds(off0, ch)], idx0)
            c0 = pltpu.async_copy(table_hbm.at[idx0], rows0, g0)
            pltpu.sync_copy(idx_hbm.at[pl.ds(off1, ch)], idx1)
            c1 = pltpu.async_copy(table_hbm.at[idx1], rows1, g1)
            c0.wait()
            o0 = pltpu.async_copy(rows0, out_hbm.at[pl.ds(off0, ch)], w0)
            c1.wait()
            o1 = pltpu.async_copy(rows1, out_hbm.at[pl.ds(off1, ch)], w1)
            o0.wait()
            o1.wait()

    return gather(table, idx)


def _shared_kernel(x_ref, s1_ref, s3_ref, s2_ref, o_ref):
    xb = x_ref[...].astype(BF16)
    a = jnp.dot(xb, s1_ref[...], preferred_element_type=F32)
    c = jnp.dot(xb, s3_ref[...], preferred_element_type=F32)
    o_ref[...] = jnp.dot((a * jax.nn.sigmoid(a) * c).astype(BF16), s2_ref[...],
                         preferred_element_type=F32).astype(o_ref.dtype)


def _shared(x2, sw1, sw3, sw2, tm):
    T = x2.shape[0]
    s1, s3, s2 = sw1.astype(BF16), sw3.astype(BF16), sw2.astype(BF16)
    full = lambda a: pl.BlockSpec(a.shape, lambda i: (0,) * a.ndim)
    row = pl.BlockSpec((tm, D_MODEL), lambda i: (i, 0))
    return pl.pallas_call(
        _shared_kernel, grid=(T // tm,), in_specs=[row, full(s1), full(s3), full(s2)], out_specs=row,
        out_shape=jax.ShapeDtypeStruct((T, D_MODEL), BF16),
        compiler_params=_params(("parallel",)), name="shared")(x2, s1, s3, s2)


def _combine_kernel(w_ref, tile_ref, x_ref, sh_ref, ysg_ref, g_ref, b_ref, o_ref):
    lo, hi = _unpack_bf16_pairs(ysg_ref[...])
    lo, hi = lo.astype(BF16), hi.astype(BF16)
    wrep = jnp.dot(w_ref[...], tile_ref[...], precision=lax.Precision.HIGHEST, preferred_element_type=F32)
    col = lax.broadcasted_iota(I32, wrep.shape, 1)
    row = lax.broadcasted_iota(I32, wrep.shape, 0)
    wsel = jnp.where(col // MOE_K == row, wrep, 0.0)
    wh = wsel.astype(BF16)
    wl = (wsel - wh.astype(F32)).astype(BF16)
    y_lo = jnp.dot(wh, lo, preferred_element_type=F32) + jnp.dot(wl, lo, preferred_element_type=F32)
    y_hi = jnp.dot(wh, hi, preferred_element_type=F32) + jnp.dot(wl, hi, preferred_element_type=F32)
    y = sh_ref[...].astype(F32) + jnp.concatenate([y_lo, y_hi], axis=1)
    o_ref[...] = _ln_rows(ALPHA * x_ref[...] + y, g_ref[...], b_ref[...])


def _combine(w_nat, x2, shared, ysg, g, b, tc):
    T = x2.shape[0]
    wp = ysg.shape[1]
    full = lambda a: pl.BlockSpec(a.shape, lambda i: (0,) * a.ndim)
    g2, b2 = g.reshape(1, -1), b.reshape(1, -1)
    tile = jnp.asarray(np.arange(tc * MOE_K)[None, :] % MOE_K == np.arange(MOE_K)[:, None], F32)
    row = pl.BlockSpec((tc, D_MODEL), lambda i: (i, 0))
    return pl.pallas_call(
        _combine_kernel, grid=(T // tc,),
        in_specs=[pl.BlockSpec((tc, MOE_K), lambda i: (i, 0)), full(tile), row, row,
                  pl.BlockSpec((tc * MOE_K, wp), lambda i: (i, 0)), full(g2), full(b2)],
        out_specs=row, out_shape=jax.ShapeDtypeStruct((T, D_MODEL), F32),
        compiler_params=_params(("parallel",)), name="combine",
    )(w_nat, tile, x2, shared, ysg, g2, b2)


def _moe(x2, x2p, scores_t, router_bias, w1, w3, w2, sw1, sw3, sw2, g, b, tiles):
    T = x2.shape[0]
    rb = ROW_BLOCK
    idx_t, w_t = _route(scores_t, router_bias, tiles['route'])
    rank_t, counts = _rank(idx_t, tiles['rank'])
    pcounts = jnp.maximum((counts + rb - 1) // rb, 1) * rb
    pends = jnp.cumsum(pcounts)
    poffs = (pends - pcounts).astype(I32)
    n_blocks = -(-T * MOE_K // rb) + MOE_E
    dest_t = _dest(idx_t, rank_t, poffs, tiles['rank'])
    xs = _row_scatter(x2p, dest_t, n_blocks * rb)
    shared = _shared(x2, sw1, sw3, sw2, tiles['outproj'])
    ys = _experts(poffs, counts.astype(I32), xs, w1, w3, w2)
    ysg = _row_gather(ys, dest_t.T.reshape(-1))
    return _combine(w_t.T, x2, shared, ysg, g, b, tiles['combine'])


def _tiles(B, S):
    T = B * S
    pick = lambda want, n: want if n % want == 0 else n
    return dict(proj=pick(512, T), nsa_q=pick(128, S), nsa_ck=pick(512, S), outproj=pick(512, T),
                mlstm_nb=2 if B % 2 == 0 else 1, xattn=pick(512, S), route=pick(512, T), rank=pick(512, T), scatter=pick(256, T),
                combine=pick(128, T))


def kernel(x, mem, w_in, nsa_pos_k, nsa_cmp_k_w1, nsa_cmp_k_w2, nsa_pos_v, nsa_cmp_v_w1, nsa_cmp_v_w2,
           mlstm_conv_w, mlstm_i_bias, mlstm_f_bias, mlstm_norm_g, w_out, ln1_g, ln1_b,
           xa_wq, xa_wk, xa_wv, xa_wo, ln2_g, ln2_b, router_w, router_bias,
           moe_w1, moe_w3, moe_w2, shared_w1, shared_w3, shared_w2, ln3_g, ln3_b):
    B, S, D = x.shape
    T = B * S
    tl = _tiles(B, S)
    xc = x.reshape(T, D)
    memc = mem.reshape(B * MEM_LEN, D)
    for l in range(w_in.shape[0]):
        (q, cmp, ksel, vsel, kwin, vwin, gates, mq, mk, mv, mo, mif) = _project(xc, _prep_w_in(w_in[l]), tl['proj'])
        kcvc = _compress(cmp, B, S, _prep_cmp(nsa_pos_k[l], nsa_cmp_k_w1[l], nsa_cmp_k_w2[l],
                                              nsa_pos_v[l], nsa_cmp_v_w1[l], nsa_cmp_v_w2[l]))
        y_nsa = _nsa(q, kcvc, ksel, vsel, kwin, vwin, gates, B, S, tl['nsa_q'], tl['nsa_ck'])
        y_ml = _mlstm(mq, mk, mv, mo, mif, mlstm_conv_w[l], mlstm_i_bias[l], mlstm_f_bias[l],
                      mlstm_norm_g[l], B, S, tl['mlstm_nb'])
        x1 = _outproj(y_nsa, y_ml, xc, w_out[l], ln1_g[l], ln1_b[l], tl['outproj'])
        kv = _memkv(memc, xa_wk[l], xa_wv[l])
        x2, x2p, scores_t = _xattn(x1, kv, xa_wq[l], xa_wo[l], ln2_g[l], ln2_b[l], router_w[l], S, tl['xattn'])
        xc = _moe(x2, x2p, scores_t, router_bias[l], moe_w1[l], moe_w3[l], moe_w2[l],
                  shared_w1[l], shared_w3[l], shared_w2[l], ln3_g[l], ln3_b[l], tl)
    return xc.reshape(B, S, D)
```

```python
import functools
import numpy as np
import jax
import jax.numpy as jnp
from jax import lax
from jax.experimental import pallas as pl
from jax.experimental.pallas import tpu as pltpu
from jax.experimental.pallas import tpu_sc as plsc

F32 = jnp.float32
BF16 = jnp.bfloat16
I32 = jnp.int32

D_MODEL = 1024
MEM_LEN = 256
NSA_HEADS = 8
NSA_GROUPS = 2
NSA_HPG = 4
NSA_DK = 64
NSA_CMP_LEN = 32
NSA_CMP_STRIDE = 16
NSA_SEL_BLOCK = 64
NSA_SEL_TOPN = 8
NSA_WINDOW = 512
ML_HEADS = 4
ML_DH = 128
ML_CHUNK = 64
ML_CONV = 4
XA_HEADS = 4
XA_DH = 256
MOE_E = 256
MOE_K = 8
MOE_GROUPS = 8
MOE_TOPK_GROUPS = 4
MOE_FF = 256
MOE_ROUTE_SCALE = 2.5
DEPTH = 1
ALPHA = (2.0 * DEPTH) ** 0.25
LN_EPS = 1e-5
NEG = -1e30
FORCE_BONUS = 1e4

LANES = 128
ROW_BLOCK = 512
VMEM_LIMIT = 56 * 1024 * 1024
SC_CORES = 2
SC_SUBCORES = 16
SC_GATHER_ROWS = 128

_DN_T = (((1,), (1,)), ((), ()))
_DN_TA = (((0,), (0,)), ((), ()))


def _params(sem):
    return pltpu.CompilerParams(dimension_semantics=sem, vmem_limit_bytes=VMEM_LIMIT)


def _ln_rows(v, g, b):
    mu = jnp.mean(v, axis=-1, keepdims=True)
    d = v - mu
    var = jnp.mean(d * d, axis=-1, keepdims=True)
    return d * lax.rsqrt(var + LN_EPS) * g + b


_SEGS = (('q', 512, BF16), ('cmp', 256, F32), ('ksel', 128, BF16), ('vsel', 128, BF16),
         ('kwin', 128, BF16), ('vwin', 128, BF16), ('gates', 128, F32), ('mq', 512, BF16),
         ('mk', 512, BF16), ('mv', 512, BF16), ('mo', 512, BF16), ('mif', 128, F32))


def _proj_kernel(x_ref, w_ref, *out_refs):
    xb = x_ref[...].astype(BF16)
    off = 0
    for o_ref, (_, wd, _) in zip(out_refs, _SEGS):
        o_ref[...] = jnp.dot(xb, w_ref[:, off:off + wd], preferred_element_type=F32).astype(o_ref.dtype)
        off += wd


def _prep_w_in(w):
    sizes = (512,) + (128,) * 6 + (24,) + (512,) * 4 + (4, 4)
    cuts = np.cumsum(sizes)[:-1].tolist()
    (wq, kc, vc, ks, vs, kw, vw, wg, mq, mk, mv, mo, mi, mf) = jnp.split(w, cuts, axis=1)
    wq = wq.reshape(D_MODEL, NSA_GROUPS, NSA_HPG, NSA_DK).transpose(0, 2, 1, 3).reshape(D_MODEL, 512)
    pad = lambda a: jnp.pad(a, ((0, 0), (0, LANES - a.shape[1])))
    segs = [wq, kc, vc, ks, vs, kw, vw, pad(wg), mq, mk, mv, mo, pad(jnp.concatenate([mi, mf], axis=1))]
    return jnp.concatenate(segs, axis=1).astype(BF16)


def _project(x2d, w_all, tm):
    T = x2d.shape[0]
    n = w_all.shape[1]
    out_shape = tuple(jax.ShapeDtypeStruct((T, wd), dt) for _, wd, dt in _SEGS)
    out_specs = tuple(pl.BlockSpec((tm, wd), lambda i: (i, 0)) for _, wd, _ in _SEGS)
    return pl.pallas_call(
        _proj_kernel, grid=(T // tm,),
        in_specs=[pl.BlockSpec((tm, D_MODEL), lambda i: (i, 0)),
                  pl.BlockSpec((D_MODEL, n), lambda i: (0, 0))],
        out_specs=out_specs, out_shape=out_shape,
        compiler_params=_params(("parallel",)), name="proj")(x2d, w_all)


def _cmp_kernel(r_ref, pa_ref, pb_ref, wa_ref, wb_ref, w2_ref, o_ref):
    r = r_ref[...]
    a = jnp.dot((r + pa_ref[...]).astype(BF16), wa_ref[...], preferred_element_type=F32)
    b = jnp.dot((r + pb_ref[...]).astype(BF16), wb_ref[...], preferred_element_type=F32)
    nr = r.shape[0]
    hid = a + pltpu.roll(b, nr - 1, 0)
    hid = hid * jax.nn.sigmoid(hid)
    out = jnp.dot(hid.astype(BF16), w2_ref[...], preferred_element_type=F32)
    row = lax.broadcasted_iota(I32, out.shape, 0)
    o_ref[...] = jnp.where(row < nr - 1, out, 0.0).astype(o_ref.dtype)


def _prep_cmp(pos_k, w1_k, w2_k, pos_v, w1_v, w2_v):
    eye = jnp.eye(NSA_GROUPS, dtype=F32)

    def expand_w1(w1, half):
        w = w1.reshape(NSA_CMP_LEN, NSA_DK, NSA_DK)[half * 16:(half + 1) * 16]
        return jnp.einsum('jde,gk->jgdke', w, eye).reshape(16, 128, 128)

    def both(fk, fv):
        z = jnp.zeros_like(fk)
        top = jnp.concatenate([fk, z], axis=-1)
        bot = jnp.concatenate([z, fv], axis=-1)
        return jnp.concatenate([top, bot], axis=-2)

    wa = both(expand_w1(w1_k, 0), expand_w1(w1_v, 0)).reshape(16 * 256, 256).astype(BF16)
    wb = both(expand_w1(w1_k, 1), expand_w1(w1_v, 1)).reshape(16 * 256, 256).astype(BF16)
    w2 = both(jnp.kron(eye, w2_k), jnp.kron(eye, w2_v)).astype(BF16)

    def pos_row(half):
        pk = jnp.tile(pos_k[half * 16:(half + 1) * 16], (1, NSA_GROUPS))
        pv = jnp.tile(pos_v[half * 16:(half + 1) * 16], (1, NSA_GROUPS))
        return jnp.concatenate([pk, pv], axis=1).reshape(1, 16 * 256)

    return pos_row(0), pos_row(1), wa, wb, w2


def _compress(cmp2d, B, S, prep):
    pa, pb, wa, wb, w2 = prep
    nr = S // NSA_CMP_STRIDE
    r = cmp2d.reshape(B, nr, NSA_CMP_STRIDE * 256)
    full = lambda a: pl.BlockSpec(a.shape, lambda b: (0,) * a.ndim)
    return pl.pallas_call(
        _cmp_kernel, grid=(B,),
        in_specs=[pl.BlockSpec((None, nr, NSA_CMP_STRIDE * 256), lambda b: (b, 0, 0)),
                  full(pa), full(pb), full(wa), full(wb), full(w2)],
        out_specs=pl.BlockSpec((None, nr, 256), lambda b: (b, 0, 0)),
        out_shape=jax.ShapeDtypeStruct((B, nr, 256), BF16),
        compiler_params=_params(("parallel",)), name="cmp")(r, pa, pb, wa, wb, w2)


def _nsa_consts(S):
    n_cmp = (S - NSA_CMP_LEN) // NSA_CMP_STRIDE + 1
    n_sel = S // NSA_SEL_BLOCK
    cs = np.arange(n_cmp) * NSA_CMP_STRIDE
    ss = np.arange(n_sel) * NSA_SEL_BLOCK
    ov = ((cs[:, None] < ss[None, :] + NSA_SEL_BLOCK) & (cs[:, None] + NSA_CMP_LEN > ss[None, :]))
    ovt = np.zeros((LANES, S // NSA_CMP_STRIDE), np.float32)
    ovt[:n_sel, :n_cmp] = ov.T
    e = np.zeros((LANES, S), np.float32)
    e[np.arange(S) // NSA_SEL_BLOCK, np.arange(S)] = 1.0
    return jnp.asarray(ovt, BF16), jnp.asarray(e, BF16)


def _nsa_kernel(q_ref, kcvc_ref, ksel_ref, vsel_ref, kwin_ref, vwin_ref, gates_ref, ovt_ref, e_ref,
                y_ref, bias_ref, *, tq, ck, n_sel):
    G, H = NSA_GROUPS, NSA_HPG
    GH = G * H
    M = GH * tq
    S = ksel_ref.shape[0]
    W = NSA_WINDOW
    ws = min(W + tq, S)
    t0 = pl.program_id(1) * tq
    gates = jax.nn.sigmoid(gates_ref[...])
    lane = lax.broadcasted_iota(I32, (tq, LANES), 1)
    t_col = t0 + lax.broadcasted_iota(I32, (tq, 1), 0)
    kc = kcvc_ref[:, 0:LANES]
    vc = kcvc_ref[:, LANES:2 * LANES]
    nc = kc.shape[0]
    qs = []
    for g in range(G):
        gmask = (lane // NSA_DK) == g
        for h in range(H):
            qh = q_ref[:, h * LANES:(h + 1) * LANES] * (NSA_DK ** -0.5)
            qs.append(jnp.where(gmask, qh, jnp.zeros_like(qh)))
    Q = jnp.concatenate(qs, axis=0).astype(BF16)

    s = lax.dot_general(Q, kc, _DN_T, preferred_element_type=F32)
    c_idx = lax.broadcasted_iota(I32, (tq, nc), 1)
    cmask = (c_idx * NSA_CMP_STRIDE + NSA_CMP_LEN - 1) <= t_col
    s3 = jnp.where(cmask[None], s.reshape(GH, tq, nc), NEG)
    p = jnp.exp(s3 - jnp.max(s3, axis=-1, keepdims=True))
    p = p / jnp.sum(p, axis=-1, keepdims=True)
    p = jnp.where(cmask[None], p, 0.0)
    o_cmp = jnp.dot(p.reshape(M, nc).astype(BF16), vc, preferred_element_type=F32).reshape(GH, tq, LANES)

    for g in range(G):
        psum = jnp.sum(p[g * H:(g + 1) * H], axis=0)
        hi = psum.astype(BF16)
        lo = (psum - hi.astype(F32)).astype(BF16)
        ovt = ovt_ref[...]
        pslt = (lax.dot_general(ovt, hi, _DN_T, preferred_element_type=F32) +
                lax.dot_general(ovt, lo, _DN_T, preferred_element_type=F32))
        imp_p = pslt[0:n_sel, :]
        n_i = lax.broadcasted_iota(I32, (n_sel, tq), 0)
        cur = (t0 + lax.broadcasted_iota(I32, (n_sel, tq), 1)) // NSA_SEL_BLOCK
        forced = (n_i == 0) | (n_i == cur) | (n_i == cur - 1)
        imp = jnp.where(n_i <= cur, imp_p + jnp.where(forced, FORCE_BONUS, 0.0), NEG)
        cnt = jnp.zeros((n_sel, tq), F32)
        for m in range(n_sel):
            row = imp[m:m + 1, :]
            beats = (row > imp) | ((row == imp) & (n_i > m))
            cnt = cnt + jnp.where(beats, 1.0, 0.0)
        selt = jnp.where(cnt < float(min(NSA_SEL_TOPN, n_sel)), 1.0, 0.0)
        selt = jnp.concatenate([selt, jnp.zeros((LANES - n_sel, tq), F32)], axis=0)
        sel = selt.T.astype(BF16)
        maskf = jnp.dot(sel, e_ref[...], preferred_element_type=F32)
        kpos = lax.broadcasted_iota(I32, (tq, S), 1)
        bias_ref[g] = jnp.where((maskf > 0.5) & (kpos <= t_col), 0.0, NEG)

    vlane = lax.broadcasted_iota(I32, (1, LANES), 1) // NSA_DK

    def pv_with_sums(pb, v):
        outs = []
        for g in range(G):
            vg = jnp.where(vlane == g, v, jnp.ones_like(v))
            outs.append(jnp.dot(pb[g * H * tq:(g + 1) * H * tq], vg, preferred_element_type=F32))
        return jnp.concatenate(outs, axis=0)

    def normalise(acc):
        outs = []
        for r in range(GH):
            c = NSA_DK * (1 - r // H)
            outs.append(acc[r] / acc[r][:, c:c + 1])
        return outs

    def sel_body(j, carry):
        m_i, acc = carry
        ks = pl.multiple_of(j * ck, ck)
        k = ksel_ref[pl.ds(ks, ck), :]
        v = vsel_ref[pl.ds(ks, ck), :]
        sj = lax.dot_general(Q, k, _DN_T, preferred_element_type=F32)
        sj = (sj.reshape(G, H, tq, ck) + bias_ref[:, :, pl.ds(ks, ck)][:, None]).reshape(GH, tq, ck)
        m_new = jnp.maximum(m_i, jnp.max(sj, axis=-1, keepdims=True))
        a = jnp.exp(m_i - m_new)
        pj = jnp.exp(sj - m_new).reshape(M, ck).astype(BF16)
        return m_new, a * acc + pv_with_sums(pj, v).reshape(GH, tq, LANES)

    init = (jnp.full((GH, tq, 1), NEG, F32), jnp.zeros((GH, tq, LANES), F32))
    _, acc = lax.fori_loop(0, (t0 + tq + ck - 1) // ck, sel_body, init)
    o_sel = normalise(acc)

    kst = pl.multiple_of(jnp.clip(t0 - W, 0, S - ws), LANES)
    kwn = kwin_ref[pl.ds(kst, ws), :]
    vwn = vwin_ref[pl.ds(kst, ws), :]
    sw = lax.dot_general(Q, kwn, _DN_T, preferred_element_type=F32)
    wpos = kst + lax.broadcasted_iota(I32, (tq, ws), 1)
    wmask = (wpos <= t_col) & (wpos > t_col - W)
    sw3 = jnp.where(wmask[None], sw.reshape(GH, tq, ws), NEG)
    pw = jnp.exp(sw3 - jnp.max(sw3, axis=-1, keepdims=True)).reshape(M, ws).astype(BF16)
    o_win = normalise(pv_with_sums(pw, vwn).reshape(GH, tq, LANES))

    g0mask = lane < NSA_DK
    for h in range(H):
        o_g = []
        for g in range(G):
            r = g * H + h
            c0 = r * 3
            o_g.append(gates[:, c0:c0 + 1] * o_cmp[r] + gates[:, c0 + 1:c0 + 2] * o_sel[r] +
                       gates[:, c0 + 2:c0 + 3] * o_win[r])
        y_ref[:, h * LANES:(h + 1) * LANES] = jnp.where(g0mask, o_g[0], o_g[1]).astype(y_ref.dtype)


def _nsa(q, kcvc, ksel, vsel, kwin, vwin, gates, B, S, tq, ck):
    T = B * S
    nq = S // tq
    ovt, e = _nsa_consts(S)
    seq = lambda a: a.reshape(B, S, LANES)
    kv_spec = pl.BlockSpec((None, S, LANES), lambda b, i: (b, 0, 0))
    kern = functools.partial(_nsa_kernel, tq=tq, ck=ck, n_sel=S // NSA_SEL_BLOCK)
    return pl.pallas_call(
        kern, grid=(B, nq),
        in_specs=[pl.BlockSpec((tq, 512), lambda b, i: (b * nq + i, 0)),
                  pl.BlockSpec((None,) + kcvc.shape[1:], lambda b, i: (b, 0, 0)),
                  kv_spec, kv_spec, kv_spec, kv_spec,
                  pl.BlockSpec((tq, LANES), lambda b, i: (b * nq + i, 0)),
                  pl.BlockSpec(ovt.shape, lambda b, i: (0, 0)),
                  pl.BlockSpec(e.shape, lambda b, i: (0, 0))],
        out_specs=pl.BlockSpec((tq, 512), lambda b, i: (b * nq + i, 0)),
        out_shape=jax.ShapeDtypeStruct((T, 512), BF16),
        scratch_shapes=[pltpu.VMEM((NSA_GROUPS, tq, S), F32)],
        compiler_params=_params(("parallel", "parallel")), name="nsa",
    )(q, kcvc, seq(ksel), seq(vsel), seq(kwin), seq(vwin), gates, ovt, e)


def _mlstm_kernel(q_ref, k_ref, v_ref, o_ref, gn_ref, gt_ref, cw_ref, bn_ref, bt_ref, ng_ref, tri_ref,
                  y_ref, c_scr, n_scr):
    H, dh, L = ML_HEADS, ML_DH, ML_CHUNK
    nb, S = q_ref.shape[0], q_ref.shape[1]
    nchunk = S // L
    c_scr[...] = jnp.zeros_like(c_scr)
    n_scr[...] = jnp.zeros_like(n_scr)
    row = lax.broadcasted_iota(I32, (L, H * dh), 0)
    li = lax.broadcasted_iota(I32, (L, L), 0)
    mi = lax.broadcasted_iota(I32, (L, L), 1)
    causal = mi <= li
    tril = tri_ref[0]
    triu = tri_ref[1]
    hp = lax.Precision.HIGHEST

    def conv_silu(ref, bi, c, wofs):
        r0 = pl.multiple_of(c * L, L)
        rp = pl.multiple_of(jnp.maximum(c - 1, 0) * L, L)
        cur = ref[bi, pl.ds(r0, L), :].astype(F32)
        prev = ref[bi, pl.ds(rp, L), :].astype(F32) * jnp.where(c > 0, 1.0, 0.0)
        acc = cur * cw_ref[ML_CONV - 1:ML_CONV, wofs:wofs + H * dh]
        for j in range(1, ML_CONV):
            sh = jnp.where(row < j, pltpu.roll(prev, j, 0), pltpu.roll(cur, j, 0))
            acc = acc + sh * cw_ref[ML_CONV - 1 - j:ML_CONV - j, wofs:wofs + H * dh]
        return acc * jax.nn.sigmoid(acc)

    def body(c, m_state):
        r0 = pl.multiple_of(c * L, L)
        new_m = []
        for bi in range(nb):
            qa = conv_silu(q_ref, bi, c, 0) * (dh ** -0.5)
            ka = conv_silu(k_ref, bi, c, H * dh)
            va = v_ref[bi, pl.ds(r0, L), :]
            oa = o_ref[bi, pl.ds(r0, L), :].astype(F32)
            gn = gn_ref[bi, pl.ds(r0, L), :] + bn_ref[...]
            gt = gt_ref[bi, :, c, :] + bt_ref[...]
            lf_n = jax.nn.log_sigmoid(gn)
            lf_t = jax.nn.log_sigmoid(gt)
            b_n = jnp.dot(tril, lf_n, precision=hp, preferred_element_type=F32)
            b_t = jnp.dot(lf_t, triu, precision=hp, preferred_element_type=F32)
            for h in range(H):
                st = bi * H + h
                q = qa[:, h * dh:(h + 1) * dh]
                k = ka[:, h * dh:(h + 1) * dh]
                v = va[:, h * dh:(h + 1) * dh]
                m_old = m_state[st]
                b_col = b_n[:, H + h:H + h + 1]
                i_col = gn[:, h:h + 1]
                b_row = b_t[H + h:H + h + 1, :]
                i_row = gt[h:h + 1, :]
                g_tot = b_t[H + h:H + h + 1, L - 1:L]
                d_log = jnp.where(causal, b_col - b_row + i_row, NEG)
                inter = b_col + m_old
                m_q = jnp.maximum(inter, jnp.max(d_log, axis=-1, keepdims=True))
                w_intra = jnp.exp(d_log - m_q)
                w_inter = jnp.exp(inter - m_q)
                qb = q.astype(BF16)
                s = lax.dot_general(qb, k.astype(BF16), _DN_T, preferred_element_type=F32) * w_intra
                cst = c_scr[st]
                nst = n_scr[st]
                num = (w_inter * jnp.dot(qb, cst.astype(BF16), preferred_element_type=F32) +
                       jnp.dot(s.astype(BF16), v, preferred_element_type=F32))
                den = w_inter * jnp.sum(q * nst, axis=-1, keepdims=True) + jnp.sum(s, axis=-1, keepdims=True)
                hv = num / jnp.maximum(jnp.abs(den), jnp.exp(-m_q))
                log_k = g_tot - b_col + i_col
                m_new = jnp.maximum(g_tot + m_old, jnp.max(log_k, axis=0, keepdims=True))
                wk = jnp.exp(log_k - m_new)
                decay = jnp.exp(g_tot + m_old - m_new)
                kw = k * wk
                c_scr[st] = decay * cst + lax.dot_general(kw.astype(BF16), v, _DN_TA, preferred_element_type=F32)
                n_scr[st] = decay * nst + jnp.sum(kw, axis=0, keepdims=True)
                new_m.append(m_new)
                mu = jnp.mean(hv, axis=-1, keepdims=True)
                dv = hv - mu
                var = jnp.mean(dv * dv, axis=-1, keepdims=True)
                hn = dv * lax.rsqrt(var + LN_EPS) * ng_ref[:, h * dh:(h + 1) * dh]
                og = jax.nn.sigmoid(oa[:, h * dh:(h + 1) * dh])
                y_ref[bi, pl.ds(r0, L), h * dh:(h + 1) * dh] = (og * hn).astype(y_ref.dtype)
        return tuple(new_m)

    lax.fori_loop(0, nchunk, body, tuple(jnp.zeros((1, 1), F32) for _ in range(nb * H)))


def _mlstm(mq, mk, mv, mo, mif, conv_w, i_bias, f_bias, norm_g, B, S, nb):
    T = B * S
    H, dh, L = ML_HEADS, ML_DH, ML_CHUNK
    W = H * dh
    gt = mif[:, :2 * H].reshape(B, S, 2 * H).transpose(0, 2, 1).reshape(B, 2 * H, S // L, L)
    cw = conv_w.reshape(ML_CONV, 2 * W)
    bias = jnp.concatenate([i_bias, f_bias])
    bn = jnp.pad(bias, (0, LANES - 2 * H)).reshape(1, LANES)
    bt = bias.reshape(2 * H, 1)
    ng = norm_g.reshape(1, W)
    tri = jnp.stack([jnp.tril(jnp.ones((L, L), F32)), jnp.triu(jnp.ones((L, L), F32))])
    seq = lambda a: a.reshape(B, S, a.shape[1])
    rows = lambda w: pl.BlockSpec((nb, S, w), lambda b: (b, 0, 0))
    full = lambda a: pl.BlockSpec(a.shape, lambda b: (0,) * a.ndim)
    y = pl.pallas_call(
        _mlstm_kernel, grid=(B // nb,),
        in_specs=[rows(W), rows(W), rows(W), rows(W), rows(LANES),
                  pl.BlockSpec((nb, 2 * H, S // L, L), lambda b: (b, 0, 0, 0)),
                  full(cw), full(bn), full(bt), full(ng), full(tri)],
        out_specs=rows(W),
        out_shape=jax.ShapeDtypeStruct((B, S, W), BF16),
        scratch_shapes=[pltpu.VMEM((nb * H, dh, dh), F32), pltpu.VMEM((nb * H, 1, dh), F32)],
        compiler_params=_params(("parallel",)), name="mlstm",
    )(seq(mq), seq(mk), seq(mv), seq(mo), seq(mif), gt, cw, bn, bt, ng, tri)
    return y.reshape(T, W)


def _outproj_kernel(yn_ref, ym_ref, x_ref, w_ref, g_ref, b_ref, o_ref):
    mix = (jnp.dot(yn_ref[...], w_ref[0:512, :], preferred_element_type=F32) +
           jnp.dot(ym_ref[...], w_ref[512:1024, :], preferred_element_type=F32))
    o_ref[...] = _ln_rows(ALPHA * x_ref[...] + mix, g_ref[...], b_ref[...])


def _outproj(y_nsa, y_ml, x2d, w_out, g, b, tm):
    T = x2d.shape[0]
    wn = w_out[:512].reshape(NSA_GROUPS, NSA_HPG, NSA_DK, D_MODEL).transpose(1, 0, 2, 3).reshape(512, D_MODEL)
    w = jnp.concatenate([wn, w_out[512:]], axis=0).astype(BF16)
    row = lambda wd: pl.BlockSpec((tm, wd), lambda i: (i, 0))
    full = lambda a: pl.BlockSpec(a.shape, lambda i: (0,) * a.ndim)
    g2, b2 = g.reshape(1, -1), b.reshape(1, -1)
    return pl.pallas_call(
        _outproj_kernel, grid=(T // tm,),
        in_specs=[row(512), row(512), row(D_MODEL), full(w), full(g2), full(b2)],
        out_specs=row(D_MODEL), out_shape=jax.ShapeDtypeStruct((T, D_MODEL), F32),
        compiler_params=_params(("parallel",)), name="outproj")(y_nsa, y_ml, x2d, w, g2, b2)


def _memkv_kernel(m_ref, w_ref, o_ref):
    o_ref[...] = jnp.dot(m_ref[...].astype(BF16), w_ref[...], preferred_element_type=F32).astype(o_ref.dtype)


def _memkv(mem2d, wk, wv):
    w = jnp.concatenate([wk, wv], axis=1).astype(BF16)
    R = mem2d.shape[0]
    return pl.pallas_call(
        _memkv_kernel, grid=(R // MEM_LEN,),
        in_specs=[pl.BlockSpec((MEM_LEN, D_MODEL), lambda i: (i, 0)),
                  pl.BlockSpec(w.shape, lambda i: (0, 0))],
        out_specs=pl.BlockSpec((MEM_LEN, 2 * D_MODEL), lambda i: (i, 0)),
        out_shape=jax.ShapeDtypeStruct((R, 2 * D_MODEL), BF16),
        compiler_params=_params(("parallel",)), name="memkv")(mem2d, w)


def _xattn_kernel(x_ref, kv_ref, wq_ref, wo_ref, g_ref, b_ref, rw_ref, x2_ref, x2p_ref, sc_ref):
    x1 = x_ref[...]
    q = jnp.dot(x1.astype(BF16), wq_ref[...], preferred_element_type=F32).astype(BF16)
    outs = []
    for h in range(XA_HEADS):
        qh = q[:, h * XA_DH:(h + 1) * XA_DH]
        kh = kv_ref[:, h * XA_DH:(h + 1) * XA_DH]
        vh = kv_ref[:, D_MODEL + h * XA_DH:D_MODEL + (h + 1) * XA_DH]
        s = lax.dot_general(qh, kh, _DN_T, preferred_element_type=F32) * (XA_DH ** -0.5)
        p = jnp.exp(s - jnp.max(s, axis=-1, keepdims=True))
        p = p / jnp.sum(p, axis=-1, keepdims=True)
        outs.append(jnp.dot(p.astype(BF16), vh, preferred_element_type=F32).astype(BF16))
    o = jnp.concatenate(outs, axis=1)
    xa = jnp.dot(o, wo_ref[...], preferred_element_type=F32)
    x2 = _ln_rows(ALPHA * x1 + xa, g_ref[...], b_ref[...])
    x2_ref[...] = x2
    x2p_ref[...] = _pack_bf16_pairs(x2)
    xh = x2.astype(BF16)
    xl = (x2 - xh.astype(F32)).astype(BF16)
    wh = rw_ref[0]
    wl = rw_ref[1]
    logit = (lax.dot_general(wh, xh, _DN_T, preferred_element_type=F32) +
             lax.dot_general(wh, xl, _DN_T, preferred_element_type=F32) +
             lax.dot_general(wl, xh, _DN_T, preferred_element_type=F32))
    sc_ref[...] = jax.nn.sigmoid(logit)


def _xattn(x1, kv, wq, wo, g, b, router_w, S, tq):
    T = x1.shape[0]
    wqb, wob = wq.astype(BF16), wo.astype(BF16)
    rwt = router_w.T
    rh = rwt.astype(BF16)
    rw = jnp.stack([rh, (rwt - rh.astype(F32)).astype(BF16)])
    g2, b2 = g.reshape(1, -1), b.reshape(1, -1)
    full = lambda a: pl.BlockSpec(a.shape, lambda i: (0,) * a.ndim)
    per = S // tq
    return pl.pallas_call(
        _xattn_kernel, grid=(T // tq,),
        in_specs=[pl.BlockSpec((tq, D_MODEL), lambda i: (i, 0)),
                  pl.BlockSpec((MEM_LEN, 2 * D_MODEL), lambda i: (i // per, 0)),
                  full(wqb), full(wob), full(g2), full(b2), full(rw)],
        out_specs=(pl.BlockSpec((tq, D_MODEL), lambda i: (i, 0)),
                   pl.BlockSpec((tq, D_MODEL // 2), lambda i: (i, 0)),
                   pl.BlockSpec((MOE_E, tq), lambda i: (0, i))),
        out_shape=(jax.ShapeDtypeStruct((T, D_MODEL), F32), jax.ShapeDtypeStruct((T, D_MODEL // 2), I32),
                   jax.ShapeDtypeStruct((MOE_E, T), F32)),
        compiler_params=_params(("parallel",)), name="xattn")(x1, kv, wqb, wob, g2, b2, rw)


def _route_kernel(sc_ref, rb_ref, idx_ref, w_ref):
    E, G = MOE_E, MOE_GROUPS
    per = E // G
    scores = sc_ref[...]
    tr = scores.shape[1]
    biased = scores + rb_ref[...]
    g3 = biased.reshape(G, per, tr)
    j3 = lax.broadcasted_iota(I32, (G, per, tr), 1)
    m1 = jnp.max(g3, axis=1, keepdims=True)
    first = jnp.min(jnp.where(g3 == m1, j3, per), axis=1, keepdims=True)
    m2 = jnp.max(jnp.where(j3 == first, -jnp.inf, g3), axis=1, keepdims=True)
    gs = (m1 + m2).reshape(G, tr)
    gi = lax.broadcasted_iota(I32, (G, tr), 0)
    cnt = jnp.zeros((G, tr), F32)
    for m in range(G):
        row = gs[m:m + 1, :]
        cnt = cnt + jnp.where((row > gs) | ((row == gs) & (gi > m)), 1.0, 0.0)
    gmask = cnt < float(MOE_TOPK_GROUPS)
    masked = jnp.where(gmask[:, None, :], g3, NEG).reshape(E, tr)
    ei = lax.broadcasted_iota(I32, (E, tr), 0)
    idxs, ws = [], []
    for _ in range(MOE_K):
        mx = jnp.max(masked, axis=0, keepdims=True)
        ix = jnp.min(jnp.where(masked == mx, ei, E), axis=0, keepdims=True)
        hit = ei == ix
        ws.append(jnp.sum(jnp.where(hit, scores, 0.0), axis=0, keepdims=True))
        idxs.append(ix)
        masked = jnp.where(hit, -jnp.inf, masked)
    w = jnp.concatenate(ws, axis=0)
    idx_ref[...] = jnp.concatenate(idxs, axis=0)
    w_ref[...] = w / jnp.sum(w, axis=0, keepdims=True) * MOE_ROUTE_SCALE


def _route(scores_t, router_bias, tr):
    E, T = scores_t.shape
    rb = router_bias.reshape(E, 1)
    return pl.pallas_call(
        _route_kernel, grid=(T // tr,),
        in_specs=[pl.BlockSpec((E, tr), lambda i: (0, i)), pl.BlockSpec((E, 1), lambda i: (0, 0))],
        out_specs=(pl.BlockSpec((MOE_K, tr), lambda i: (0, i)), pl.BlockSpec((MOE_K, tr), lambda i: (0, i))),
        out_shape=(jax.ShapeDtypeStruct((MOE_K, T), I32), jax.ShapeDtypeStruct((MOE_K, T), F32)),
        compiler_params=_params(("parallel",)), name="route")(scores_t, rb)


def _rank_kernel(idx_ref, u_ref, rank_ref, cnt_ref, carry):
    E = MOE_E

    @pl.when(pl.program_id(0) == 0)
    def _():
        carry[...] = jnp.zeros_like(carry)

    idx = idx_ref[...]
    tp = idx.shape[1]
    ei = lax.broadcasted_iota(I32, (E, tp), 0)
    hits = [ei == idx[k:k + 1, :] for k in range(MOE_K)]
    onehot = jnp.zeros((E, tp), F32)
    for hit in hits:
        onehot = onehot + jnp.where(hit, 1.0, 0.0)
    pos = jnp.dot(onehot.astype(BF16), u_ref[...], preferred_element_type=F32) + carry[...]
    ranks = [jnp.sum(jnp.where(hit, pos, 0.0), axis=0, keepdims=True) for hit in hits]
    rank_ref[...] = jnp.concatenate(ranks, axis=0).astype(I32)
    total = carry[...] + jnp.sum(onehot, axis=1, keepdims=True)
    carry[...] = total
    cnt_ref[...] = jnp.broadcast_to(total, cnt_ref.shape).astype(I32)


def _rank(idx_t, tp):
    K, T = idx_t.shape
    u = jnp.triu(jnp.ones((tp, tp), F32), k=1).astype(BF16)
    rank, cnt = pl.pallas_call(
        _rank_kernel, grid=(T // tp,),
        in_specs=[pl.BlockSpec((K, tp), lambda i: (0, i)), pl.BlockSpec((tp, tp), lambda i: (0, 0))],
        out_specs=(pl.BlockSpec((K, tp), lambda i: (0, i)), pl.BlockSpec((MOE_E, LANES), lambda i: (0, 0))),
        out_shape=(jax.ShapeDtypeStruct((K, T), I32), jax.ShapeDtypeStruct((MOE_E, LANES), I32)),
        scratch_shapes=[pltpu.VMEM((MOE_E, 1), F32)],
        compiler_params=_params(("arbitrary",)), name="rank")(idx_t, u)
    return rank, cnt[:, 0]


def _dest_kernel(idx_ref, rank_ref, po_ref, dest_ref):
    idx = idx_ref[...]
    tp = idx.shape[1]
    ei = lax.broadcasted_iota(I32, (MOE_E, tp), 0)
    po = po_ref[...]
    base = [jnp.sum(jnp.where(ei == idx[k:k + 1, :], po, 0.0), axis=0, keepdims=True) for k in range(MOE_K)]
    dest_ref[...] = jnp.concatenate(base, axis=0).astype(I32) + rank_ref[...]


def _dest(idx_t, rank_t, poffs, tp):
    K, T = idx_t.shape
    po = poffs.astype(F32).reshape(MOE_E, 1)
    spec = pl.BlockSpec((K, tp), lambda i: (0, i))
    return pl.pallas_call(
        _dest_kernel, grid=(T // tp,),
        in_specs=[spec, spec, pl.BlockSpec((MOE_E, 1), lambda i: (0, 0))],
        out_specs=spec, out_shape=jax.ShapeDtypeStruct((K, T), I32),
        compiler_params=_params(("parallel",)), name="dest")(idx_t, rank_t, po)


def _pack_bf16_pairs(v):
    m = v.shape[1] // 2
    bits = lax.bitcast_convert_type(v.astype(BF16).astype(F32), jnp.uint32)
    return lax.bitcast_convert_type((bits[:, :m] >> 16) | (bits[:, m:] & jnp.uint32(0xFFFF0000)), I32)


def _unpack_bf16_pairs(w):
    w = lax.bitcast_convert_type(w, jnp.uint32)
    lo = lax.bitcast_convert_type(w << 16, F32)
    hi = lax.bitcast_convert_type(w & jnp.uint32(0xFFFF0000), F32)
    return lo, hi


def _row_scatter(rows, dest_t, n_rows):
    T, d = rows.shape
    K = dest_t.shape[0]
    nw = SC_CORES * SC_SUBCORES
    per_w = T // nw
    ch = SC_GATHER_ROWS
    mesh = plsc.VectorSubcoreMesh(core_axis_name="c", subcore_axis_name="s")

    @functools.partial(
        pl.kernel, mesh=mesh, out_type=jax.ShapeDtypeStruct((n_rows, d), I32),
        scratch_types=[pltpu.VMEM((K, ch), I32), pltpu.VMEM((ch, d), I32), pltpu.SemaphoreType.DMA],
        name="row_scatter")
    def scatter(rows_hbm, dest_hbm, out_hbm, idx_v, rows_v, sem):
        wid = lax.axis_index("s") * SC_CORES + lax.axis_index("c")
        base = wid * per_w

        @pl.loop(0, per_w // ch)
        def _(i):
            off = pl.multiple_of(base + i * ch, ch)
            pltpu.sync_copy(rows_hbm.at[pl.ds(off, ch)], rows_v)
            pltpu.sync_copy(dest_hbm.at[:, pl.ds(off, ch)], idx_v)
            copies = [pltpu.async_copy(rows_v, out_hbm.at[idx_v.at[k]], sem) for k in range(K)]
            for cp in copies:
                cp.wait()

    return scatter(rows, dest_t)


def _expert_kernel(po_ref, cnt_ref, xs_hbm, w1_ref, w3_ref, w2_ref, ys_hbm,
                   xbuf, ybuf, w1b, w3b, w2b, gcnt, insem, outsem):
    e = pl.program_id(0)
    ne = pl.num_programs(0)
    n = cnt_ref[e]
    base = po_ref[e]
    rb = xbuf.shape[1]
    hw = D_MODEL // 2
    nblk = jnp.maximum((n + rb - 1) // rb, 1)
    w1b[...] = w1_ref[...].astype(BF16)
    w3b[...] = w3_ref[...].astype(BF16)
    w2b[...] = w2_ref[...].astype(BF16)

    def in_copy(r0, slot):
        return pltpu.make_async_copy(xs_hbm.at[pl.ds(pl.multiple_of(r0, rb), rb)], xbuf.at[slot], insem.at[slot])

    def out_copy(r0, slot):
        return pltpu.make_async_copy(ybuf.at[slot], ys_hbm.at[pl.ds(pl.multiple_of(r0, rb), rb)], outsem.at[slot])

    @pl.when(e == 0)
    def _():
        gcnt[0] = 0
        in_copy(base, 0).start()

    g0 = gcnt[0]

    def body(j, c):
        slot = (g0 + j) % 2

        @pl.when(j + 1 < nblk)
        def _():
            in_copy(base + (j + 1) * rb, 1 - slot).start()

        @pl.when((j + 1 == nblk) & (e + 1 < ne))
        def _():
            in_copy(po_ref[jnp.minimum(e + 1, ne - 1)], 1 - slot).start()

        in_copy(0, slot).wait()

        @pl.when(g0 + j >= 2)
        def _():
            out_copy(0, slot).wait()

        words = xbuf[slot]
        row = j * rb + lax.broadcasted_iota(I32, words.shape, 0)
        lo, hi = _unpack_bf16_pairs(jnp.where(row < n, words, 0))
        lo, hi = lo.astype(BF16), hi.astype(BF16)
        a = (jnp.dot(lo, w1b[0:hw, :], preferred_element_type=F32) +
             jnp.dot(hi, w1b[hw:, :], preferred_element_type=F32))
        g = (jnp.dot(lo, w3b[0:hw, :], preferred_element_type=F32) +
             jnp.dot(hi, w3b[hw:, :], preferred_element_type=F32))
        h = (a * jax.nn.sigmoid(a) * g).astype(BF16)
        ybuf[slot] = _pack_bf16_pairs(jnp.dot(h, w2b[...], preferred_element_type=F32))
        out_copy(base + j * rb, slot).start()
        return c

    lax.fori_loop(0, nblk, body, 0)
    total = g0 + nblk
    gcnt[0] = total

    @pl.when(e + 1 == ne)
    def _():
        @pl.when(total >= 2)
        def _():
            out_copy(0, total % 2).wait()

        out_copy(0, (total - 1) % 2).wait()


def _experts(poffs, counts, xs, w1, w3, w2):
    n_rows, wp = xs.shape
    rb = ROW_BLOCK
    wspec = lambda shape: pl.BlockSpec((None,) + shape, lambda e, po, cn: (e, 0, 0))
    gs = pltpu.PrefetchScalarGridSpec(
        num_scalar_prefetch=2, grid=(MOE_E,),
        in_specs=[pl.BlockSpec(memory_space=pl.ANY), wspec((D_MODEL, MOE_FF)), wspec((D_MODEL, MOE_FF)),
                  wspec((MOE_FF, D_MODEL))],
        out_specs=pl.BlockSpec(memory_space=pl.ANY),
        scratch_shapes=[pltpu.VMEM((2, rb, wp), I32), pltpu.VMEM((2, rb, wp), I32),
                        pltpu.VMEM((D_MODEL, MOE_FF), BF16), pltpu.VMEM((D_MODEL, MOE_FF), BF16),
                        pltpu.VMEM((MOE_FF, D_MODEL), BF16), pltpu.SMEM((1,), I32),
                        pltpu.SemaphoreType.DMA((2,)), pltpu.SemaphoreType.DMA((2,))])
    return pl.pallas_call(
        _expert_kernel, grid_spec=gs, out_shape=jax.ShapeDtypeStruct((n_rows, wp), I32),
        compiler_params=_params(("arbitrary",)), name="experts")(poffs, counts, xs, w1, w3, w2)


def _row_gather(table, idx):
    n, d = idx.shape[0], table.shape[1]
    nw = SC_CORES * SC_SUBCORES
    per_w = n // nw
    ch = SC_GATHER_ROWS // 2
    mesh = plsc.VectorSubcoreMesh(core_axis_name="c", subcore_axis_name="s")

    @functools.partial(
        pl.kernel, mesh=mesh, out_type=jax.ShapeDtypeStruct((n, d), I32),
        scratch_types=[pltpu.VMEM((ch,), I32), pltpu.VMEM((ch,), I32),
                       pltpu.VMEM((ch, d), I32), pltpu.VMEM((ch, d), I32),
                       pltpu.SemaphoreType.DMA, pltpu.SemaphoreType.DMA,
                       pltpu.SemaphoreType.DMA, pltpu.SemaphoreType.DMA],
        name="row_gather")
    def gather(table_hbm, idx_hbm, out_hbm, idx0, idx1, rows0, rows1, g0, g1, w0, w1):
        wid = lax.axis_index("s") * SC_CORES + lax.axis_index("c")
        base = wid * per_w

        @pl.loop(0, per_w // (2 * ch))
        def _(i):
            off0 = pl.multiple_of(base + 2 * i * ch, ch)
            off1 = pl.multiple_of(off0 + ch, ch)
            pltpu.sync_copy(idx_hbm.at[pl.ds(off0, ch)], idx0)
            c0 = pltpu.async_copy(table_hbm.at[idx0], rows0, g0)
            pltpu.sync_copy(idx_hbm.at[pl.ds(off1, ch)], idx1)
            c1 = pltpu.async_copy(table_hbm.at[idx1], rows1, g1)
            c0.wait()
            o0 = pltpu.async_copy(rows0, out_hbm.at[pl.ds(off0, ch)], w0)
            c1.wait()
            o1 = pltpu.async_copy(rows1, out_hbm.at[pl.ds(off1, ch)], w1)
            o0.wait()
            o1.wait()

    return gather(table, idx)


def _combine_kernel(w_ref, x_ref, ysg_ref, s1_ref, s3_ref, s2_ref, g_ref, b_ref, o_ref):
    x2 = x_ref[...]
    xb = x2.astype(BF16)
    a = jnp.dot(xb, s1_ref[...], preferred_element_type=F32)
    c = jnp.dot(xb, s3_ref[...], preferred_element_type=F32)
    shared = jnp.dot((a * jax.nn.sigmoid(a) * c).astype(BF16), s2_ref[...], preferred_element_type=F32)
    w = w_ref[...]
    hw = ysg_ref.shape[2]
    y_lo = jnp.zeros((x2.shape[0], hw), F32)
    y_hi = jnp.zeros((x2.shape[0], hw), F32)
    for k in range(MOE_K):
        lo, hi = _unpack_bf16_pairs(ysg_ref[k])
        y_lo = y_lo + w[:, k:k + 1] * lo
        y_hi = y_hi + w[:, k:k + 1] * hi
    y = shared + jnp.concatenate([y_lo, y_hi], axis=1)
    o_ref[...] = _ln_rows(ALPHA * x2 + y, g_ref[...], b_ref[...])


def _combine(w_nat, x2, ysg, sw1, sw3, sw2, g, b, tc):
    T = x2.shape[0]
    wp = ysg.shape[2]
    full = lambda a: pl.BlockSpec(a.shape, lambda i: (0,) * a.ndim)
    s1, s3, s2 = sw1.astype(BF16), sw3.astype(BF16), sw2.astype(BF16)
    g2, b2 = g.reshape(1, -1), b.reshape(1, -1)
    return pl.pallas_call(
        _combine_kernel, grid=(T // tc,),
        in_specs=[pl.BlockSpec((tc, MOE_K), lambda i: (i, 0)),
                  pl.BlockSpec((tc, D_MODEL), lambda i: (i, 0)),
                  pl.BlockSpec((MOE_K, tc, wp), lambda i: (0, i, 0)),
                  full(s1), full(s3), full(s2), full(g2), full(b2)],
        out_specs=pl.BlockSpec((tc, D_MODEL), lambda i: (i, 0)),
        out_shape=jax.ShapeDtypeStruct((T, D_MODEL), F32),
        compiler_params=_params(("parallel",)), name="combine",
    )(w_nat, x2, ysg, s1, s3, s2, g2, b2)


def _moe(x2, x2p, scores_t, router_bias, w1, w3, w2, sw1, sw3, sw2, g, b, tiles):
    T = x2.shape[0]
    rb = ROW_BLOCK
    idx_t, w_t = _route(scores_t, router_bias, tiles['route'])
    rank_t, counts = _rank(idx_t, tiles['rank'])
    pcounts = jnp.maximum((counts + rb - 1) // rb, 1) * rb
    pends = jnp.cumsum(pcounts)
    poffs = (pends - pcounts).astype(I32)
    n_blocks = -(-T * MOE_K // rb) + MOE_E
    dest_t = _dest(idx_t, rank_t, poffs, tiles['rank'])
    xs = _row_scatter(x2p, dest_t, n_blocks * rb)
    ys = _experts(poffs, counts.astype(I32), xs, w1, w3, w2)
    ysg = _row_gather(ys, dest_t.reshape(-1)).reshape(MOE_K, T, -1)
    return _combine(w_t.T, x2, ysg, sw1, sw3, sw2, g, b, tiles['combine'])


def _tiles(B, S):
    T = B * S
    pick = lambda want, n: want if n % want == 0 else n
    return dict(proj=pick(512, T), nsa_q=pick(128, S), nsa_ck=pick(512, S), outproj=pick(512, T),
                mlstm_nb=2 if B % 2 == 0 else 1, xattn=pick(512, S), route=pick(512, T), rank=pick(512, T), scatter=pick(256, T),
                combine=pick(128, T))


def kernel(x, mem, w_in, nsa_pos_k, nsa_cmp_k_w1, nsa_cmp_k_w2, nsa_pos_v, nsa_cmp_v_w1, nsa_cmp_v_w2,
           mlstm_conv_w, mlstm_i_bias, mlstm_f_bias, mlstm_norm_g, w_out, ln1_g, ln1_b,
           xa_wq, xa_wk, xa_wv, xa_wo, ln2_g, ln2_b, router_w, router_bias,
           moe_w1, moe_w3, moe_w2, shared_w1, shared_w3, shared_w2, ln3_g, ln3_b):
    B, S, D = x.shape
    T = B * S
    tl = _tiles(B, S)
    xc = x.reshape(T, D)
    memc = mem.reshape(B * MEM_LEN, D)
    for l in range(w_in.shape[0]):
        (q, cmp, ksel, vsel, kwin, vwin, gates, mq, mk, mv, mo, mif) = _project(xc, _prep_w_in(w_in[l]), tl['proj'])
        kcvc = _compress(cmp, B, S, _prep_cmp(nsa_pos_k[l], nsa_cmp_k_w1[l], nsa_cmp_k_w2[l],
                                              nsa_pos_v[l], nsa_cmp_v_w1[l], nsa_cmp_v_w2[l]))
        y_nsa = _nsa(q, kcvc, ksel, vsel, kwin, vwin, gates, B, S, tl['nsa_q'], tl['nsa_ck'])
        y_ml = _mlstm(mq, mk, mv, mo, mif, mlstm_conv_w[l], mlstm_i_bias[l], mlstm_f_bias[l],
                      mlstm_norm_g[l], B, S, tl['mlstm_nb'])
        x1 = _outproj(y_nsa, y_ml, xc, w_out[l], ln1_g[l], ln1_b[l], tl['outproj'])
        kv = _memkv(memc, xa_wk[l], xa_wv[l])
        x2, x2p, scores_t = _xattn(x1, kv, xa_wq[l], xa_wo[l], ln2_g[l], ln2_b[l], router_w[l], S, tl['xattn'])
        xc = _moe(x2, x2p, scores_t, router_bias[l], moe_w1[l], moe_w3[l], moe_w2[l],
                  shared_w1[l], shared_w3[l], shared_w2[l], ln3_g[l], ln3_b[l], tl)
    return xc.reshape(B, S, D)
```

```python
import functools
import numpy as np
import jax
import jax.numpy as jnp
from jax import lax
from jax.experimental import pallas as pl
from jax.experimental.pallas import tpu as pltpu
from jax.experimental.pallas import tpu_sc as plsc

F32 = jnp.float32
BF16 = jnp.bfloat16
I32 = jnp.int32

D_MODEL = 1024
MEM_LEN = 256
NSA_HEADS = 8
NSA_GROUPS = 2
NSA_HPG = 4
NSA_DK = 64
NSA_CMP_LEN = 32
NSA_CMP_STRIDE = 16
NSA_SEL_BLOCK = 64
NSA_SEL_TOPN = 8
NSA_WINDOW = 512
ML_HEADS = 4
ML_DH = 128
ML_CHUNK = 64
ML_CONV = 4
XA_HEADS = 4
XA_DH = 256
MOE_E = 256
MOE_K = 8
MOE_GROUPS = 8
MOE_TOPK_GROUPS = 4
MOE_FF = 256
MOE_ROUTE_SCALE = 2.5
DEPTH = 1
ALPHA = (2.0 * DEPTH) ** 0.25
LN_EPS = 1e-5
NEG = -1e30
FORCE_BONUS = 1e4

LANES = 128
ROW_BLOCK = 512
VMEM_LIMIT = 56 * 1024 * 1024
SC_CORES = 2
SC_SUBCORES = 16
SC_GATHER_ROWS = 128

_DN_T = (((1,), (1,)), ((), ()))
_DN_TA = (((0,), (0,)), ((), ()))


def _params(sem):
    return pltpu.CompilerParams(dimension_semantics=sem, vmem_limit_bytes=VMEM_LIMIT)


def _ln_rows(v, g, b):
    mu = jnp.mean(v, axis=-1, keepdims=True)
    d = v - mu
    var = jnp.mean(d * d, axis=-1, keepdims=True)
    return d * lax.rsqrt(var + LN_EPS) * g + b


_SEGS = (('q', 512, BF16), ('cmp', 256, F32), ('ksel', 128, BF16), ('vsel', 128, BF16),
         ('kwin', 128, BF16), ('vwin', 128, BF16), ('gates', 128, F32), ('mq', 512, BF16),
         ('mk', 512, BF16), ('mv', 512, BF16), ('mo', 512, BF16), ('mif', 128, F32))


def _proj_kernel(x_ref, w_ref, *out_refs):
    xb = x_ref[...].astype(BF16)
    off = 0
    for o_ref, (_, wd, _) in zip(out_refs, _SEGS):
        o_ref[...] = jnp.dot(xb, w_ref[:, off:off + wd], preferred_element_type=F32).astype(o_ref.dtype)
        off += wd


def _prep_w_in(w):
    sizes = (512,) + (128,) * 6 + (24,) + (512,) * 4 + (4, 4)
    cuts = np.cumsum(sizes)[:-1].tolist()
    (wq, kc, vc, ks, vs, kw, vw, wg, mq, mk, mv, mo, mi, mf) = jnp.split(w, cuts, axis=1)
    wq = wq.reshape(D_MODEL, NSA_GROUPS, NSA_HPG, NSA_DK).transpose(0, 2, 1, 3).reshape(D_MODEL, 512)
    pad = lambda a: jnp.pad(a, ((0, 0), (0, LANES - a.shape[1])))
    segs = [wq, kc, vc, ks, vs, kw, vw, pad(wg), mq, mk, mv, mo, pad(jnp.concatenate([mi, mf], axis=1))]
    return jnp.concatenate(segs, axis=1).astype(BF16)


def _project(x2d, w_all, tm):
    T = x2d.shape[0]
    n = w_all.shape[1]
    out_shape = tuple(jax.ShapeDtypeStruct((T, wd), dt) for _, wd, dt in _SEGS)
    out_specs = tuple(pl.BlockSpec((tm, wd), lambda i: (i, 0)) for _, wd, _ in _SEGS)
    return pl.pallas_call(
        _proj_kernel, grid=(T // tm,),
        in_specs=[pl.BlockSpec((tm, D_MODEL), lambda i: (i, 0)),
                  pl.BlockSpec((D_MODEL, n), lambda i: (0, 0))],
        out_specs=out_specs, out_shape=out_shape,
        compiler_params=_params(("parallel",)), name="proj")(x2d, w_all)


def _cmp_kernel(r_ref, pa_ref, pb_ref, wa_ref, wb_ref, w2_ref, o_ref):
    r = r_ref[...]
    a = jnp.dot((r + pa_ref[...]).astype(BF16), wa_ref[...], preferred_element_type=F32)
    b = jnp.dot((r + pb_ref[...]).astype(BF16), wb_ref[...], preferred_element_type=F32)
    nr = r.shape[0]
    hid = a + pltpu.roll(b, nr - 1, 0)
    hid = hid * jax.nn.sigmoid(hid)
    out = jnp.dot(hid.astype(BF16), w2_ref[...], preferred_element_type=F32)
    row = lax.broadcasted_iota(I32, out.shape, 0)
    o_ref[...] = jnp.where(row < nr - 1, out, 0.0).astype(o_ref.dtype)


def _prep_cmp(pos_k, w1_k, w2_k, pos_v, w1_v, w2_v):
    eye = jnp.eye(NSA_GROUPS, dtype=F32)

    def expand_w1(w1, half):
        w = w1.reshape(NSA_CMP_LEN, NSA_DK, NSA_DK)[half * 16:(half + 1) * 16]
        return jnp.einsum('jde,gk->jgdke', w, eye).reshape(16, 128, 128)

    def both(fk, fv):
        z = jnp.zeros_like(fk)
        top = jnp.concatenate([fk, z], axis=-1)
        bot = jnp.concatenate([z, fv], axis=-1)
        return jnp.concatenate([top, bot], axis=-2)

    wa = both(expand_w1(w1_k, 0), expand_w1(w1_v, 0)).reshape(16 * 256, 256).astype(BF16)
    wb = both(expand_w1(w1_k, 1), expand_w1(w1_v, 1)).reshape(16 * 256, 256).astype(BF16)
    w2 = both(jnp.kron(eye, w2_k), jnp.kron(eye, w2_v)).astype(BF16)

    def pos_row(half):
        pk = jnp.tile(pos_k[half * 16:(half + 1) * 16], (1, NSA_GROUPS))
        pv = jnp.tile(pos_v[half * 16:(half + 1) * 16], (1, NSA_GROUPS))
        return jnp.concatenate([pk, pv], axis=1).reshape(1, 16 * 256)

    return pos_row(0), pos_row(1), wa, wb, w2


def _compress(cmp2d, B, S, prep):
    pa, pb, wa, wb, w2 = prep
    nr = S // NSA_CMP_STRIDE
    r = cmp2d.reshape(B, nr, NSA_CMP_STRIDE * 256)
    full = lambda a: pl.BlockSpec(a.shape, lambda b: (0,) * a.ndim)
    return pl.pallas_call(
        _cmp_kernel, grid=(B,),
        in_specs=[pl.BlockSpec((None, nr, NSA_CMP_STRIDE * 256), lambda b: (b, 0, 0)),
                  full(pa), full(pb), full(wa), full(wb), full(w2)],
        out_specs=pl.BlockSpec((None, nr, 256), lambda b: (b, 0, 0)),
        out_shape=jax.ShapeDtypeStruct((B, nr, 256), BF16),
        compiler_params=_params(("parallel",)), name="cmp")(r, pa, pb, wa, wb, w2)


def _nsa_consts(S):
    n_cmp = (S - NSA_CMP_LEN) // NSA_CMP_STRIDE + 1
    n_sel = S // NSA_SEL_BLOCK
    cs = np.arange(n_cmp) * NSA_CMP_STRIDE
    ss = np.arange(n_sel) * NSA_SEL_BLOCK
    ov = ((cs[:, None] < ss[None, :] + NSA_SEL_BLOCK) & (cs[:, None] + NSA_CMP_LEN > ss[None, :]))
    ovt = np.zeros((LANES, S // NSA_CMP_STRIDE), np.float32)
    ovt[:n_sel, :n_cmp] = ov.T
    e = np.zeros((LANES, S), np.float32)
    e[np.arange(S) // NSA_SEL_BLOCK, np.arange(S)] = 1.0
    return jnp.asarray(ovt, BF16), jnp.asarray(e, BF16)


def _nsa_kernel(q_ref, kcvc_ref, ksel_ref, vsel_ref, kwin_ref, vwin_ref, gates_ref, ovt_ref, e_ref,
                y_ref, bias_ref, *, tq, ck, n_sel):
    G, H = NSA_GROUPS, NSA_HPG
    GH = G * H
    M = GH * tq
    S = ksel_ref.shape[0]
    W = NSA_WINDOW
    ws = min(W + tq, S)
    t0 = pl.program_id(1) * tq
    gates = jax.nn.sigmoid(gates_ref[...])
    lane = lax.broadcasted_iota(I32, (tq, LANES), 1)
    t_col = t0 + lax.broadcasted_iota(I32, (tq, 1), 0)
    kc = kcvc_ref[:, 0:LANES]
    vc = kcvc_ref[:, LANES:2 * LANES]
    nc = kc.shape[0]
    qs = []
    for g in range(G):
        gmask = (lane // NSA_DK) == g
        for h in range(H):
            qh = q_ref[:, h * LANES:(h + 1) * LANES] * (NSA_DK ** -0.5)
            qs.append(jnp.where(gmask, qh, jnp.zeros_like(qh)))
    Q = jnp.concatenate(qs, axis=0).astype(BF16)

    s = lax.dot_general(Q, kc, _DN_T, preferred_element_type=F32)
    c_idx = lax.broadcasted_iota(I32, (tq, nc), 1)
    cmask = (c_idx * NSA_CMP_STRIDE + NSA_CMP_LEN - 1) <= t_col
    s3 = jnp.where(cmask[None], s.reshape(GH, tq, nc), NEG)
    p = jnp.exp(s3 - jnp.max(s3, axis=-1, keepdims=True))
    p = p / jnp.sum(p, axis=-1, keepdims=True)
    p = jnp.where(cmask[None], p, 0.0)
    o_cmp = jnp.dot(p.reshape(M, nc).astype(BF16), vc, preferred_element_type=F32).reshape(GH, tq, LANES)

    for g in range(G):
        psum = jnp.sum(p[g * H:(g + 1) * H], axis=0)
        hi = psum.astype(BF16)
        lo = (psum - hi.astype(F32)).astype(BF16)
        ovt = ovt_ref[...]
        pslt = (lax.dot_general(ovt, hi, _DN_T, preferred_element_type=F32) +
                lax.dot_general(ovt, lo, _DN_T, preferred_element_type=F32))
        imp_p = pslt[0:n_sel, :]
        n_i = lax.broadcasted_iota(I32, (n_sel, tq), 0)
        cur = (t0 + lax.broadcasted_iota(I32, (n_sel, tq), 1)) // NSA_SEL_BLOCK
        forced = (n_i == 0) | (n_i == cur) | (n_i == cur - 1)
        imp = jnp.where(n_i <= cur, imp_p + jnp.where(forced, FORCE_BONUS, 0.0), NEG)
        cnt = jnp.zeros((n_sel, tq), F32)
        for m in range(n_sel):
            row = imp[m:m + 1, :]
            beats = (row > imp) | ((row == imp) & (n_i > m))
            cnt = cnt + jnp.where(beats, 1.0, 0.0)
        selt = jnp.where(cnt < float(min(NSA_SEL_TOPN, n_sel)), 1.0, 0.0)
        selt = jnp.concatenate([selt, jnp.zeros((LANES - n_sel, tq), F32)], axis=0)
        sel = selt.T.astype(BF16)
        maskf = jnp.dot(sel, e_ref[...], preferred_element_type=F32)
        kpos = lax.broadcasted_iota(I32, (tq, S), 1)
        bias_ref[g] = jnp.where((maskf > 0.5) & (kpos <= t_col), 0.0, NEG)

    vlane = lax.broadcasted_iota(I32, (1, LANES), 1) // NSA_DK

    def pv_with_sums(pb, v):
        outs = []
        for g in range(G):
            vg = jnp.where(vlane == g, v, jnp.ones_like(v))
            outs.append(jnp.dot(pb[g * H * tq:(g + 1) * H * tq], vg, preferred_element_type=F32))
        return jnp.concatenate(outs, axis=0)

    def normalise(acc):
        outs = []
        for r in range(GH):
            c = NSA_DK * (1 - r // H)
            outs.append(acc[r] / acc[r][:, c:c + 1])
        return outs

    def sel_body(j, carry):
        m_i, acc = carry
        ks = pl.multiple_of(j * ck, ck)
        k = ksel_ref[pl.ds(ks, ck), :]
        v = vsel_ref[pl.ds(ks, ck), :]
        sj = lax.dot_general(Q, k, _DN_T, preferred_element_type=F32)
        sj = (sj.reshape(G, H, tq, ck) + bias_ref[:, :, pl.ds(ks, ck)][:, None]).reshape(GH, tq, ck)
        m_new = jnp.maximum(m_i, jnp.max(sj, axis=-1, keepdims=True))
        a = jnp.exp(m_i - m_new)
        pj = jnp.exp(sj - m_new).reshape(M, ck).astype(BF16)
        return m_new, a * acc + pv_with_sums(pj, v).reshape(GH, tq, LANES)

    init = (jnp.full((GH, tq, 1), NEG, F32), jnp.zeros((GH, tq, LANES), F32))
    _, acc = lax.fori_loop(0, (t0 + tq + ck - 1) // ck, sel_body, init)
    o_sel = normalise(acc)

    kst = pl.multiple_of(jnp.clip(t0 - W, 0, S - ws), LANES)
    kwn = kwin_ref[pl.ds(kst, ws), :]
    vwn = vwin_ref[pl.ds(kst, ws), :]
    sw = lax.dot_general(Q, kwn, _DN_T, preferred_element_type=F32)
    wpos = kst + lax.broadcasted_iota(I32, (tq, ws), 1)
    wmask = (wpos <= t_col) & (wpos > t_col - W)
    sw3 = jnp.where(wmask[None], sw.reshape(GH, tq, ws), NEG)
    pw = jnp.exp(sw3 - jnp.max(sw3, axis=-1, keepdims=True)).reshape(M, ws).astype(BF16)
    o_win = normalise(pv_with_sums(pw, vwn).reshape(GH, tq, LANES))

    g0mask = lane < NSA_DK
    for h in range(H):
        o_g = []
        for g in range(G):
            r = g * H + h
            c0 = r * 3
            o_g.append(gates[:, c0:c0 + 1] * o_cmp[r] + gates[:, c0 + 1:c0 + 2] * o_sel[r] +
                       gates[:, c0 + 2:c0 + 3] * o_win[r])
        y_ref[:, h * LANES:(h + 1) * LANES] = jnp.where(g0mask, o_g[0], o_g[1]).astype(y_ref.dtype)


def _nsa(q, kcvc, ksel, vsel, kwin, vwin, gates, B, S, tq, ck):
    T = B * S
    nq = S // tq
    ovt, e = _nsa_consts(S)
    seq = lambda a: a.reshape(B, S, LANES)
    kv_spec = pl.BlockSpec((None, S, LANES), lambda b, i: (b, 0, 0))
    kern = functools.partial(_nsa_kernel, tq=tq, ck=ck, n_sel=S // NSA_SEL_BLOCK)
    return pl.pallas_call(
        kern, grid=(B, nq),
        in_specs=[pl.BlockSpec((tq, 512), lambda b, i: (b * nq + i, 0)),
                  pl.BlockSpec((None,) + kcvc.shape[1:], lambda b, i: (b, 0, 0)),
                  kv_spec, kv_spec, kv_spec, kv_spec,
                  pl.BlockSpec((tq, LANES), lambda b, i: (b * nq + i, 0)),
                  pl.BlockSpec(ovt.shape, lambda b, i: (0, 0)),
                  pl.BlockSpec(e.shape, lambda b, i: (0, 0))],
        out_specs=pl.BlockSpec((tq, 512), lambda b, i: (b * nq + i, 0)),
        out_shape=jax.ShapeDtypeStruct((T, 512), BF16),
        scratch_shapes=[pltpu.VMEM((NSA_GROUPS, tq, S), F32)],
        compiler_params=_params(("parallel", "parallel")), name="nsa",
    )(q, kcvc, seq(ksel), seq(vsel), seq(kwin), seq(vwin), gates, ovt, e)


def _mlstm_kernel(q_ref, k_ref, v_ref, o_ref, gn_ref, gt_ref, cw_ref, bn_ref, bt_ref, ng_ref, tri_ref,
                  y_ref, c_scr, n_scr):
    H, dh, L = ML_HEADS, ML_DH, ML_CHUNK
    nb, S = q_ref.shape[0], q_ref.shape[1]
    nchunk = S // L
    c_scr[...] = jnp.zeros_like(c_scr)
    n_scr[...] = jnp.zeros_like(n_scr)
    row = lax.broadcasted_iota(I32, (L, H * dh), 0)
    li = lax.broadcasted_iota(I32, (L, L), 0)
    mi = lax.broadcasted_iota(I32, (L, L), 1)
    causal = mi <= li
    tril = tri_ref[0]
    triu = tri_ref[1]
    hp = lax.Precision.HIGHEST

    def conv_silu(ref, bi, c, wofs):
        r0 = pl.multiple_of(c * L, L)
        rp = pl.multiple_of(jnp.maximum(c - 1, 0) * L, L)
        cur = ref[bi, pl.ds(r0, L), :].astype(F32)
        prev = ref[bi, pl.ds(rp, L), :].astype(F32) * jnp.where(c > 0, 1.0, 0.0)
        acc = cur * cw_ref[ML_CONV - 1:ML_CONV, wofs:wofs + H * dh]
        for j in range(1, ML_CONV):
            sh = jnp.where(row < j, pltpu.roll(prev, j, 0), pltpu.roll(cur, j, 0))
            acc = acc + sh * cw_ref[ML_CONV - 1 - j:ML_CONV - j, wofs:wofs + H * dh]
        return acc * jax.nn.sigmoid(acc)

    def body(c, m_state):
        r0 = pl.multiple_of(c * L, L)
        new_m = []
        for bi in range(nb):
            qa = conv_silu(q_ref, bi, c, 0) * (dh ** -0.5)
            ka = conv_silu(k_ref, bi, c, H * dh)
            va = v_ref[bi, pl.ds(r0, L), :]
            oa = o_ref[bi, pl.ds(r0, L), :].astype(F32)
            gn = gn_ref[bi, pl.ds(r0, L), :] + bn_ref[...]
            gt = gt_ref[bi, :, c, :] + bt_ref[...]
            lf_n = jax.nn.log_sigmoid(gn)
            lf_t = jax.nn.log_sigmoid(gt)
            b_n = jnp.dot(tril, lf_n, precision=hp, preferred_element_type=F32)
            b_t = jnp.dot(lf_t, triu, precision=hp, preferred_element_type=F32)
            for h in range(H):
                st = bi * H + h
                q = qa[:, h * dh:(h + 1) * dh]
                k = ka[:, h * dh:(h + 1) * dh]
                v = va[:, h * dh:(h + 1) * dh]
                m_old = m_state[st]
                b_col = b_n[:, H + h:H + h + 1]
                i_col = gn[:, h:h + 1]
                b_row = b_t[H + h:H + h + 1, :]
                i_row = gt[h:h + 1, :]
                g_tot = b_t[H + h:H + h + 1, L - 1:L]
                d_log = jnp.where(causal, b_col - b_row + i_row, NEG)
                inter = b_col + m_old
                m_q = jnp.maximum(inter, jnp.max(d_log, axis=-1, keepdims=True))
                w_intra = jnp.exp(d_log - m_q)
                w_inter = jnp.exp(inter - m_q)
                qb = q.astype(BF16)
                s = lax.dot_general(qb, k.astype(BF16), _DN_T, preferred_element_type=F32) * w_intra
                cst = c_scr[st]
                nst = n_scr[st]
                num = (w_inter * jnp.dot(qb, cst.astype(BF16), preferred_element_type=F32) +
                       jnp.dot(s.astype(BF16), v, preferred_element_type=F32))
                den = w_inter * jnp.sum(q * nst, axis=-1, keepdims=True) + jnp.sum(s, axis=-1, keepdims=True)
                hv = num / jnp.maximum(jnp.abs(den), jnp.exp(-m_q))
                log_k = g_tot - b_col + i_col
                m_new = jnp.maximum(g_tot + m_old, jnp.max(log_k, axis=0, keepdims=True))
                wk = jnp.exp(log_k - m_new)
                decay = jnp.exp(g_tot + m_old - m_new)
                kw = k * wk
                c_scr[st] = decay * cst + lax.dot_general(kw.astype(BF16), v, _DN_TA, preferred_element_type=F32)
                n_scr[st] = decay * nst + jnp.sum(kw, axis=0, keepdims=True)
                new_m.append(m_new)
                mu = jnp.mean(hv, axis=-1, keepdims=True)
                dv = hv - mu
                var = jnp.mean(dv * dv, axis=-1, keepdims=True)
                hn = dv * lax.rsqrt(var + LN_EPS) * ng_ref[:, h * dh:(h + 1) * dh]
                og = jax.nn.sigmoid(oa[:, h * dh:(h + 1) * dh])
                y_ref[bi, pl.ds(r0, L), h * dh:(h + 1) * dh] = (og * hn).astype(y_ref.dtype)
        return tuple(new_m)

    lax.fori_loop(0, nchunk, body, tuple(jnp.zeros((1, 1), F32) for _ in range(nb * H)))


def _mlstm(mq, mk, mv, mo, mif, conv_w, i_bias, f_bias, norm_g, B, S, nb):
    T = B * S
    H, dh, L = ML_HEADS, ML_DH, ML_CHUNK
    W = H * dh
    gt = mif[:, :2 * H].reshape(B, S, 2 * H).transpose(0, 2, 1).reshape(B, 2 * H, S // L, L)
    cw = conv_w.reshape(ML_CONV, 2 * W)
    bias = jnp.concatenate([i_bias, f_bias])
    bn = jnp.pad(bias, (0, LANES - 2 * H)).reshape(1, LANES)
    bt = bias.reshape(2 * H, 1)
    ng = norm_g.reshape(1, W)
    tri = jnp.stack([jnp.tril(jnp.ones((L, L), F32)), jnp.triu(jnp.ones((L, L), F32))])
    seq = lambda a: a.reshape(B, S, a.shape[1])
    rows = lambda w: pl.BlockSpec((nb, S, w), lambda b: (b, 0, 0))
    full = lambda a: pl.BlockSpec(a.shape, lambda b: (0,) * a.ndim)
    y = pl.pallas_call(
        _mlstm_kernel, grid=(B // nb,),
        in_specs=[rows(W), rows(W), rows(W), rows(W), rows(LANES),
                  pl.BlockSpec((nb, 2 * H, S // L, L), lambda b: (b, 0, 0, 0)),
                  full(cw), full(bn), full(bt), full(ng), full(tri)],
        out_specs=rows(W),
        out_shape=jax.ShapeDtypeStruct((B, S, W), BF16),
        scratch_shapes=[pltpu.VMEM((nb * H, dh, dh), F32), pltpu.VMEM((nb * H, 1, dh), F32)],
        compiler_params=_params(("parallel",)), name="mlstm",
    )(seq(mq), seq(mk), seq(mv), seq(mo), seq(mif), gt, cw, bn, bt, ng, tri)
    return y.reshape(T, W)


def _outproj_kernel(yn_ref, ym_ref, x_ref, w_ref, g_ref, b_ref, o_ref):
    mix = (jnp.dot(yn_ref[...], w_ref[0:512, :], preferred_element_type=F32) +
           jnp.dot(ym_ref[...], w_ref[512:1024, :], preferred_element_type=F32))
    o_ref[...] = _ln_rows(ALPHA * x_ref[...] + mix, g_ref[...], b_ref[...])


def _outproj(y_nsa, y_ml, x2d, w_out, g, b, tm):
    T = x2d.shape[0]
    wn = w_out[:512].reshape(NSA_GROUPS, NSA_HPG, NSA_DK, D_MODEL).transpose(1, 0, 2, 3).reshape(512, D_MODEL)
    w = jnp.concatenate([wn, w_out[512:]], axis=0).astype(BF16)
    row = lambda wd: pl.BlockSpec((tm, wd), lambda i: (i, 0))
    full = lambda a: pl.BlockSpec(a.shape, lambda i: (0,) * a.ndim)
    g2, b2 = g.reshape(1, -1), b.reshape(1, -1)
    return pl.pallas_call(
        _outproj_kernel, grid=(T // tm,),
        in_specs=[row(512), row(512), row(D_MODEL), full(w), full(g2), full(b2)],
        out_specs=row(D_MODEL), out_shape=jax.ShapeDtypeStruct((T, D_MODEL), F32),
        compiler_params=_params(("parallel",)), name="outproj")(y_nsa, y_ml, x2d, w, g2, b2)


def _memkv_kernel(m_ref, w_ref, o_ref):
    o_ref[...] = jnp.dot(m_ref[...].astype(BF16), w_ref[...], preferred_element_type=F32).astype(o_ref.dtype)


def _memkv(mem2d, wk, wv):
    w = jnp.concatenate([wk, wv], axis=1).astype(BF16)
    R = mem2d.shape[0]
    return pl.pallas_call(
        _memkv_kernel, grid=(R // MEM_LEN,),
        in_specs=[pl.BlockSpec((MEM_LEN, D_MODEL), lambda i: (i, 0)),
                  pl.BlockSpec(w.shape, lambda i: (0, 0))],
        out_specs=pl.BlockSpec((MEM_LEN, 2 * D_MODEL), lambda i: (i, 0)),
        out_shape=jax.ShapeDtypeStruct((R, 2 * D_MODEL), BF16),
        compiler_params=_params(("parallel",)), name="memkv")(mem2d, w)


def _xattn_kernel(x_ref, kv_ref, wq_ref, wo_ref, g_ref, b_ref, rw_ref, x2_ref, x2p_ref, sc_ref):
    x1 = x_ref[...]
    q = jnp.dot(x1.astype(BF16), wq_ref[...], preferred_element_type=F32).astype(BF16)
    outs = []
    for h in range(XA_HEADS):
        qh = q[:, h * XA_DH:(h + 1) * XA_DH]
        kh = kv_ref[:, h * XA_DH:(h + 1) * XA_DH]
        vh = kv_ref[:, D_MODEL + h * XA_DH:D_MODEL + (h + 1) * XA_DH]
        s = lax.dot_general(qh, kh, _DN_T, preferred_element_type=F32) * (XA_DH ** -0.5)
        p = jnp.exp(s - jnp.max(s, axis=-1, keepdims=True))
        p = p / jnp.sum(p, axis=-1, keepdims=True)
        outs.append(jnp.dot(p.astype(BF16), vh, preferred_element_type=F32).astype(BF16))
    o = jnp.concatenate(outs, axis=1)
    xa = jnp.dot(o, wo_ref[...], preferred_element_type=F32)
    x2 = _ln_rows(ALPHA * x1 + xa, g_ref[...], b_ref[...])
    x2_ref[...] = x2
    x2p_ref[...] = _pack_bf16_pairs(x2)
    xh = x2.astype(BF16)
    xl = (x2 - xh.astype(F32)).astype(BF16)
    wh = rw_ref[0]
    wl = rw_ref[1]
    logit = (lax.dot_general(wh, xh, _DN_T, preferred_element_type=F32) +
             lax.dot_general(wh, xl, _DN_T, preferred_element_type=F32) +
             lax.dot_general(wl, xh, _DN_T, preferred_element_type=F32))
    sc_ref[...] = jax.nn.sigmoid(logit)


def _xattn(x1, kv, wq, wo, g, b, router_w, S, tq):
    T = x1.shape[0]
    wqb, wob = wq.astype(BF16), wo.astype(BF16)
    rwt = router_w.T
    rh = rwt.astype(BF16)
    rw = jnp.stack([rh, (rwt - rh.astype(F32)).astype(BF16)])
    g2, b2 = g.reshape(1, -1), b.reshape(1, -1)
    full = lambda a: pl.BlockSpec(a.shape, lambda i: (0,) * a.ndim)
    per = S // tq
    return pl.pallas_call(
        _xattn_kernel, grid=(T // tq,),
        in_specs=[pl.BlockSpec((tq, D_MODEL), lambda i: (i, 0)),
                  pl.BlockSpec((MEM_LEN, 2 * D_MODEL), lambda i: (i // per, 0)),
                  full(wqb), full(wob), full(g2), full(b2), full(rw)],
        out_specs=(pl.BlockSpec((tq, D_MODEL), lambda i: (i, 0)),
                   pl.BlockSpec((tq, D_MODEL // 2), lambda i: (i, 0)),
                   pl.BlockSpec((MOE_E, tq), lambda i: (0, i))),
        out_shape=(jax.ShapeDtypeStruct((T, D_MODEL), F32), jax.ShapeDtypeStruct((T, D_MODEL // 2), I32),
                   jax.ShapeDtypeStruct((MOE_E, T), F32)),
        compiler_params=_params(("parallel",)), name="xattn")(x1, kv, wqb, wob, g2, b2, rw)


def _route_kernel(sc_ref, rb_ref, idx_ref, w_ref):
    E, G = MOE_E, MOE_GROUPS
    per = E // G
    scores = sc_ref[...]
    tr = scores.shape[1]
    biased = scores + rb_ref[...]
    g3 = biased.reshape(G, per, tr)
    j3 = lax.broadcasted_iota(I32, (G, per, tr), 1)
    m1 = jnp.max(g3, axis=1, keepdims=True)
    first = jnp.min(jnp.where(g3 == m1, j3, per), axis=1, keepdims=True)
    m2 = jnp.max(jnp.where(j3 == first, -jnp.inf, g3), axis=1, keepdims=True)
    gs = (m1 + m2).reshape(G, tr)
    gi = lax.broadcasted_iota(I32, (G, tr), 0)
    cnt = jnp.zeros((G, tr), F32)
    for m in range(G):
        row = gs[m:m + 1, :]
        cnt = cnt + jnp.where((row > gs) | ((row == gs) & (gi > m)), 1.0, 0.0)
    gmask = cnt < float(MOE_TOPK_GROUPS)
    masked = jnp.where(gmask[:, None, :], g3, NEG).reshape(E, tr)
    ei = lax.broadcasted_iota(I32, (E, tr), 0)
    idxs, ws = [], []
    for _ in range(MOE_K):
        mx = jnp.max(masked, axis=0, keepdims=True)
        ix = jnp.min(jnp.where(masked == mx, ei, E), axis=0, keepdims=True)
        hit = ei == ix
        ws.append(jnp.sum(jnp.where(hit, scores, 0.0), axis=0, keepdims=True))
        idxs.append(ix)
        masked = jnp.where(hit, -jnp.inf, masked)
    w = jnp.concatenate(ws, axis=0)
    idx_ref[...] = jnp.concatenate(idxs, axis=0)
    w_ref[...] = w / jnp.sum(w, axis=0, keepdims=True) * MOE_ROUTE_SCALE


def _route(scores_t, router_bias, tr):
    E, T = scores_t.shape
    rb = router_bias.reshape(E, 1)
    return pl.pallas_call(
        _route_kernel, grid=(T // tr,),
        in_specs=[pl.BlockSpec((E, tr), lambda i: (0, i)), pl.BlockSpec((E, 1), lambda i: (0, 0))],
        out_specs=(pl.BlockSpec((MOE_K, tr), lambda i: (0, i)), pl.BlockSpec((MOE_K, tr), lambda i: (0, i))),
        out_shape=(jax.ShapeDtypeStruct((MOE_K, T), I32), jax.ShapeDtypeStruct((MOE_K, T), F32)),
        compiler_params=_params(("parallel",)), name="route")(scores_t, rb)


def _rank_kernel(idx_ref, u_ref, rank_ref, cnt_ref, carry):
    E = MOE_E

    @pl.when(pl.program_id(0) == 0)
    def _():
        carry[...] = jnp.zeros_like(carry)

    idx = idx_ref[...]
    tp = idx.shape[1]
    ei = lax.broadcasted_iota(I32, (E, tp), 0)
    hits = [ei == idx[k:k + 1, :] for k in range(MOE_K)]
    onehot = jnp.zeros((E, tp), F32)
    for hit in hits:
        onehot = onehot + jnp.where(hit, 1.0, 0.0)
    pos = jnp.dot(onehot.astype(BF16), u_ref[...], preferred_element_type=F32) + carry[...]
    ranks = [jnp.sum(jnp.where(hit, pos, 0.0), axis=0, keepdims=True) for hit in hits]
    rank_ref[...] = jnp.concatenate(ranks, axis=0).astype(I32)
    total = carry[...] + jnp.sum(onehot, axis=1, keepdims=True)
    carry[...] = total
    cnt_ref[...] = jnp.broadcast_to(total, cnt_ref.shape).astype(I32)


def _rank(idx_t, tp):
    K, T = idx_t.shape
    u = jnp.triu(jnp.ones((tp, tp), F32), k=1).astype(BF16)
    rank, cnt = pl.pallas_call(
        _rank_kernel, grid=(T // tp,),
        in_specs=[pl.BlockSpec((K, tp), lambda i: (0, i)), pl.BlockSpec((tp, tp), lambda i: (0, 0))],
        out_specs=(pl.BlockSpec((K, tp), lambda i: (0, i)), pl.BlockSpec((MOE_E, LANES), lambda i: (0, 0))),
        out_shape=(jax.ShapeDtypeStruct((K, T), I32), jax.ShapeDtypeStruct((MOE_E, LANES), I32)),
        scratch_shapes=[pltpu.VMEM((MOE_E, 1), F32)],
        compiler_params=_params(("arbitrary",)), name="rank")(idx_t, u)
    return rank, cnt[:, 0]


def _dest_kernel(idx_ref, rank_ref, po_ref, dest_ref):
    idx = idx_ref[...]
    tp = idx.shape[1]
    ei = lax.broadcasted_iota(I32, (MOE_E, tp), 0)
    po = po_ref[...]
    base = [jnp.sum(jnp.where(ei == idx[k:k + 1, :], po, 0.0), axis=0, keepdims=True) for k in range(MOE_K)]
    dest_ref[...] = jnp.concatenate(base, axis=0).astype(I32) + rank_ref[...]


def _dest(idx_t, rank_t, poffs, tp):
    K, T = idx_t.shape
    po = poffs.astype(F32).reshape(MOE_E, 1)
    spec = pl.BlockSpec((K, tp), lambda i: (0, i))
    return pl.pallas_call(
        _dest_kernel, grid=(T // tp,),
        in_specs=[spec, spec, pl.BlockSpec((MOE_E, 1), lambda i: (0, 0))],
        out_specs=spec, out_shape=jax.ShapeDtypeStruct((K, T), I32),
        compiler_params=_params(("parallel",)), name="dest")(idx_t, rank_t, po)


def _pack_bf16_pairs(v):
    m = v.shape[1] // 2
    bits = lax.bitcast_convert_type(v.astype(BF16).astype(F32), jnp.uint32)
    return lax.bitcast_convert_type((bits[:, :m] >> 16) | (bits[:, m:] & jnp.uint32(0xFFFF0000)), I32)


def _unpack_bf16_pairs(w):
    w = lax.bitcast_convert_type(w, jnp.uint32)
    lo = lax.bitcast_convert_type(w << 16, F32)
    hi = lax.bitcast_convert_type(w & jnp.uint32(0xFFFF0000), F32)
    return lo, hi


def _row_scatter(rows, dest_t, n_rows):
    T, d = rows.shape
    K = dest_t.shape[0]
    nw = SC_CORES * SC_SUBCORES
    per_w = T // nw
    ch = SC_GATHER_ROWS
    mesh = plsc.VectorSubcoreMesh(core_axis_name="c", subcore_axis_name="s")

    @functools.partial(
        pl.kernel, mesh=mesh, out_type=jax.ShapeDtypeStruct((n_rows, d), I32),
        scratch_types=[pltpu.VMEM((K, ch), I32), pltpu.VMEM((ch, d), I32), pltpu.SemaphoreType.DMA],
        name="row_scatter")
    def scatter(rows_hbm, dest_hbm, out_hbm, idx_v, rows_v, sem):
        wid = lax.axis_index("s") * SC_CORES + lax.axis_index("c")
        base = wid * per_w

        @pl.loop(0, per_w // ch)
        def _(i):
            off = pl.multiple_of(base + i * ch, ch)
            pltpu.sync_copy(rows_hbm.at[pl.ds(off, ch)], rows_v)
            pltpu.sync_copy(dest_hbm.at[:, pl.ds(off, ch)], idx_v)
            copies = [pltpu.async_copy(rows_v, out_hbm.at[idx_v.at[k]], sem) for k in range(K)]
            for cp in copies:
                cp.wait()

    return scatter(rows, dest_t)


def _expert_kernel(po_ref, cnt_ref, xs_hbm, w1_ref, w3_ref, w2_ref, ys_hbm,
                   xbuf, ybuf, w1b, w3b, w2b, gcnt, insem, outsem):
    e = pl.program_id(0)
    ne = pl.num_programs(0)
    n = cnt_ref[e]
    base = po_ref[e]
    rb = xbuf.shape[1]
    hw = D_MODEL // 2
    nblk = jnp.maximum((n + rb - 1) // rb, 1)
    w1b[...] = w1_ref[...].astype(BF16)
    w3b[...] = w3_ref[...].astype(BF16)
    w2b[...] = w2_ref[...].astype(BF16)

    def in_copy(r0, slot):
        return pltpu.make_async_copy(xs_hbm.at[pl.ds(pl.multiple_of(r0, rb), rb)], xbuf.at[slot], insem.at[slot])

    def out_copy(r0, slot):
        return pltpu.make_async_copy(ybuf.at[slot], ys_hbm.at[pl.ds(pl.multiple_of(r0, rb), rb)], outsem.at[slot])

    @pl.when(e == 0)
    def _():
        gcnt[0] = 0
        in_copy(base, 0).start()

    g0 = gcnt[0]

    def body(j, c):
        slot = (g0 + j) % 2

        @pl.when(j + 1 < nblk)
        def _():
            in_copy(base + (j + 1) * rb, 1 - slot).start()

        @pl.when((j + 1 == nblk) & (e + 1 < ne))
        def _():
            in_copy(po_ref[jnp.minimum(e + 1, ne - 1)], 1 - slot).start()

        in_copy(0, slot).wait()

        @pl.when(g0 + j >= 2)
        def _():
            out_copy(0, slot).wait()

        words = xbuf[slot]
        row = j * rb + lax.broadcasted_iota(I32, words.shape, 0)
        lo, hi = _unpack_bf16_pairs(jnp.where(row < n, words, 0))
        lo, hi = lo.astype(BF16), hi.astype(BF16)
        a = (jnp.dot(lo, w1b[0:hw, :], preferred_element_type=F32) +
             jnp.dot(hi, w1b[hw:, :], preferred_element_type=F32))
        g = (jnp.dot(lo, w3b[0:hw, :], preferred_element_type=F32) +
             jnp.dot(hi, w3b[hw:, :], preferred_element_type=F32))
        h = (a * jax.nn.sigmoid(a) * g).astype(BF16)
        ybuf[slot] = _pack_bf16_pairs(jnp.dot(h, w2b[...], preferred_element_type=F32))
        out_copy(base + j * rb, slot).start()
        return c

    lax.fori_loop(0, nblk, body, 0)
    total = g0 + nblk
    gcnt[0] = total

    @pl.when(e + 1 == ne)
    def _():
        @pl.when(total >= 2)
        def _():
            out_copy(0, total % 2).wait()

        out_copy(0, (total - 1) % 2).wait()


def _experts(poffs, counts, xs, w1, w3, w2):
    n_rows, wp = xs.shape
    rb = ROW_BLOCK
    wspec = lambda shape: pl.BlockSpec((None,) + shape, lambda e, po, cn: (e, 0, 0))
    gs = pltpu.PrefetchScalarGridSpec(
        num_scalar_prefetch=2, grid=(MOE_E,),
        in_specs=[pl.BlockSpec(memory_space=pl.ANY), wspec((D_MODEL, MOE_FF)), wspec((D_MODEL, MOE_FF)),
                  wspec((MOE_FF, D_MODEL))],
        out_specs=pl.BlockSpec(memory_space=pl.ANY),
        scratch_shapes=[pltpu.VMEM((2, rb, wp), I32), pltpu.VMEM((2, rb, wp), I32),
                        pltpu.VMEM((D_MODEL, MOE_FF), BF16), pltpu.VMEM((D_MODEL, MOE_FF), BF16),
                        pltpu.VMEM((MOE_FF, D_MODEL), BF16), pltpu.SMEM((1,), I32),
                        pltpu.SemaphoreType.DMA((2,)), pltpu.SemaphoreType.DMA((2,))])
    return pl.pallas_call(
        _expert_kernel, grid_spec=gs, out_shape=jax.ShapeDtypeStruct((n_rows, wp), I32),
        compiler_params=_params(("arbitrary",)), name="experts")(poffs, counts, xs, w1, w3, w2)


def _row_gather(table, idx):
    n, d = idx.shape[0], table.shape[1]
    nw = SC_CORES * SC_SUBCORES
    per_w = n // nw
    ch = SC_GATHER_ROWS // 2
    mesh = plsc.VectorSubcoreMesh(core_axis_name="c", subcore_axis_name="s")

    @functools.partial(
        pl.kernel, mesh=mesh, out_type=jax.ShapeDtypeStruct((n, d), I32),
        scratch_types=[pltpu.VMEM((ch,), I32), pltpu.VMEM((ch,), I32),
                       pltpu.VMEM((ch, d), I32), pltpu.VMEM((ch, d), I32),
                       pltpu.SemaphoreType.DMA, pltpu.SemaphoreType.DMA,
                       pltpu.SemaphoreType.DMA, pltpu.SemaphoreType.DMA],
        name="row_gather")
    def gather(table_hbm, idx_hbm, out_hbm, idx0, idx1, rows0, rows1, g0, g1, w0, w1):
        wid = lax.axis_index("s") * SC_CORES + lax.axis_index("c")
        base = wid * per_w

        @pl.loop(0, per_w // (2 * ch))
        def _(i):
            off0 = pl.multiple_of(base + 2 * i * ch, ch)
            off1 = pl.multiple_of(off0 + ch, ch)
            pltpu.sync_copy(idx_hbm.at[pl.ds(off0, ch)], idx0)
            c0 = pltpu.async_copy(table_hbm.at[idx0], rows0, g0)
            pltpu.sync_copy(idx_hbm.at[pl.ds(off1, ch)], idx1)
            c1 = pltpu.async_copy(table_hbm.at[idx1], rows1, g1)
            c0.wait()
            o0 = pltpu.async_copy(rows0, out_hbm.at[pl.ds(off0, ch)], w0)
            c1.wait()
            o1 = pltpu.async_copy(rows1, out_hbm.at[pl.ds(off1, ch)], w1)
            o0.wait()
            o1.wait()

    return gather(table, idx)


def _combine_kernel(w_ref, x_ref, ysg_ref, s1_ref, s3_ref, s2_ref, g_ref, b_ref, o_ref):
    x2 = x_ref[...]
    xb = x2.astype(BF16)
    a = jnp.dot(xb, s1_ref[...], preferred_element_type=F32)
    c = jnp.dot(xb, s3_ref[...], preferred_element_type=F32)
    shared = jnp.dot((a * jax.nn.sigmoid(a) * c).astype(BF16), s2_ref[...], preferred_element_type=F32)
    w = w_ref[...]
    hw = ysg_ref.shape[2]
    y_lo = jnp.zeros((x2.shape[0], hw), F32)
    y_hi = jnp.zeros((x2.shape[0], hw), F32)
    for k in range(MOE_K):
        lo, hi = _unpack_bf16_pairs(ysg_ref[k])
        y_lo = y_lo + w[:, k:k + 1] * lo
        y_hi = y_hi + w[:, k:k + 1] * hi
    y = shared + jnp.concatenate([y_lo, y_hi], axis=1)
    o_ref[...] = _ln_rows(ALPHA * x2 + y, g_ref[...], b_ref[...])


def _combine(w_nat, x2, ysg, sw1, sw3, sw2, g, b, tc):
    T = x2.shape[0]
    wp = ysg.shape[2]
    full = lambda a: pl.BlockSpec(a.shape, lambda i: (0,) * a.ndim)
    s1, s3, s2 = sw1.astype(BF16), sw3.astype(BF16), sw2.astype(BF16)
    g2, b2 = g.reshape(1, -1), b.reshape(1, -1)
    return pl.pallas_call(
        _combine_kernel, grid=(T // tc,),
        in_specs=[pl.BlockSpec((tc, MOE_K), lambda i: (i, 0)),
                  pl.BlockSpec((tc, D_MODEL), lambda i: (i, 0)),
                  pl.BlockSpec((MOE_K, tc, wp), lambda i: (0, i, 0)),
                  full(s1), full(s3), full(s2), full(g2), full(b2)],
        out_specs=pl.BlockSpec((tc, D_MODEL), lambda i: (i, 0)),
        out_shape=jax.ShapeDtypeStruct((T, D_MODEL), F32),
        compiler_params=_params(("parallel",)), name="combine",
    )(w_nat, x2, ysg, s1, s3, s2, g2, b2)


def _moe(x2, x2p, scores_t, router_bias, w1, w3, w2, sw1, sw3, sw2, g, b, tiles):
    T = x2.shape[0]
    rb = ROW_BLOCK
    idx_t, w_t = _route(scores_t, router_bias, tiles['route'])
    rank_t, counts = _rank(idx_t, tiles['rank'])
    pcounts = jnp.maximum((counts + rb - 1) // rb, 1) * rb
    pends = jnp.cumsum(pcounts)
    poffs = (pends - pcounts).astype(I32)
    n_blocks = -(-T * MOE_K // rb) + MOE_E
    dest_t = _dest(idx_t, rank_t, poffs, tiles['rank'])
    xs = _row_scatter(x2p, dest_t, n_blocks * rb)
    ys = _experts(poffs, counts.astype(I32), xs, w1, w3, w2)
    ysg = _row_gather(ys, dest_t.reshape(-1)).reshape(MOE_K, T, -1)
    return _combine(w_t.T, x2, ysg, sw1, sw3, sw2, g, b, tiles['combine'])


def _tiles(B, S):
    T = B * S
    pick = lambda want, n: want if n % want == 0 else n
    return dict(proj=pick(512, T), nsa_q=pick(128, S), nsa_ck=pick(512, S), outproj=pick(1024, T),
                mlstm_nb=2 if B % 2 == 0 else 1, xattn=pick(512, S), route=pick(1024, T), rank=pick(512, T),
                combine=pick(256, T))


def kernel(x, mem, w_in, nsa_pos_k, nsa_cmp_k_w1, nsa_cmp_k_w2, nsa_pos_v, nsa_cmp_v_w1, nsa_cmp_v_w2,
           mlstm_conv_w, mlstm_i_bias, mlstm_f_bias, mlstm_norm_g, w_out, ln1_g, ln1_b,
           xa_wq, xa_wk, xa_wv, xa_wo, ln2_g, ln2_b, router_w, router_bias,
           moe_w1, moe_w3, moe_w2, shared_w1, shared_w3, shared_w2, ln3_g, ln3_b):
    B, S, D = x.shape
    T = B * S
    tl = _tiles(B, S)
    xc = x.reshape(T, D)
    memc = mem.reshape(B * MEM_LEN, D)
    for l in range(w_in.shape[0]):
        (q, cmp, ksel, vsel, kwin, vwin, gates, mq, mk, mv, mo, mif) = _project(xc, _prep_w_in(w_in[l]), tl['proj'])
        kcvc = _compress(cmp, B, S, _prep_cmp(nsa_pos_k[l], nsa_cmp_k_w1[l], nsa_cmp_k_w2[l],
                                              nsa_pos_v[l], nsa_cmp_v_w1[l], nsa_cmp_v_w2[l]))
        y_nsa = _nsa(q, kcvc, ksel, vsel, kwin, vwin, gates, B, S, tl['nsa_q'], tl['nsa_ck'])
        y_ml = _mlstm(mq, mk, mv, mo, mif, mlstm_conv_w[l], mlstm_i_bias[l], mlstm_f_bias[l],
                      mlstm_norm_g[l], B, S, tl['mlstm_nb'])
        x1 = _outproj(y_nsa, y_ml, xc, w_out[l], ln1_g[l], ln1_b[l], tl['outproj'])
        kv = _memkv(memc, xa_wk[l], xa_wv[l])
        x2, x2p, scores_t = _xattn(x1, kv, xa_wq[l], xa_wo[l], ln2_g[l], ln2_b[l], router_w[l], S, tl['xattn'])
        xc = _moe(x2, x2p, scores_t, router_bias[l], moe_w1[l], moe_w3[l], moe_w2[l],
                  shared_w1[l], shared_w3[l], shared_w2[l], ln3_g[l], ln3_b[l], tl)
    return xc.reshape(B, S, D)
```

```python
import functools
import numpy as np
import jax
import jax.numpy as jnp
from jax import lax
from jax.experimental import pallas as pl
from jax.experimental.pallas import tpu as pltpu
from jax.experimental.pallas import tpu_sc as plsc

F32 = jnp.float32
BF16 = jnp.bfloat16
I32 = jnp.int32

D_MODEL = 1024
MEM_LEN = 256
NSA_HEADS = 8
NSA_GROUPS = 2
NSA_HPG = 4
NSA_DK = 64
NSA_CMP_LEN = 32
NSA_CMP_STRIDE = 16
NSA_SEL_BLOCK = 64
NSA_SEL_TOPN = 8
NSA_WINDOW = 512
ML_HEADS = 4
ML_DH = 128
ML_CHUNK = 64
ML_CONV = 4
XA_HEADS = 4
XA_DH = 256
MOE_E = 256
MOE_K = 8
MOE_GROUPS = 8
MOE_TOPK_GROUPS = 4
MOE_FF = 256
MOE_ROUTE_SCALE = 2.5
DEPTH = 1
ALPHA = (2.0 * DEPTH) ** 0.25
LN_EPS = 1e-5
NEG = -1e30
FORCE_BONUS = 1e4

LANES = 128
ROW_BLOCK = 512
VMEM_LIMIT = 56 * 1024 * 1024
SC_CORES = 2
SC_SUBCORES = 16
SC_GATHER_ROWS = 128

_DN_T = (((1,), (1,)), ((), ()))
_DN_TA = (((0,), (0,)), ((), ()))


def _params(sem):
    return pltpu.CompilerParams(dimension_semantics=sem, vmem_limit_bytes=VMEM_LIMIT)


def _ln_rows(v, g, b):
    mu = jnp.mean(v, axis=-1, keepdims=True)
    d = v - mu
    var = jnp.mean(d * d, axis=-1, keepdims=True)
    return d * lax.rsqrt(var + LN_EPS) * g + b


_SEGS = (('q', 512, BF16), ('cmp', 256, F32), ('ksel', 128, BF16), ('vsel', 128, BF16),
         ('kwin', 128, BF16), ('vwin', 128, BF16), ('gates', 128, F32), ('mq', 512, BF16),
         ('mk', 512, BF16), ('mv', 512, BF16), ('mo', 512, BF16), ('mif', 128, F32))


def _proj_kernel(x_ref, w_ref, *out_refs):
    xb = x_ref[...].astype(BF16)
    off = 0
    for o_ref, (_, wd, _) in zip(out_refs, _SEGS):
        o_ref[...] = jnp.dot(xb, w_ref[:, off:off + wd], preferred_element_type=F32).astype(o_ref.dtype)
        off += wd


def _prep_w_in(w):
    sizes = (512,) + (128,) * 6 + (24,) + (512,) * 4 + (4, 4)
    cuts = np.cumsum(sizes)[:-1].tolist()
    (wq, kc, vc, ks, vs, kw, vw, wg, mq, mk, mv, mo, mi, mf) = jnp.split(w, cuts, axis=1)
    wq = wq.reshape(D_MODEL, NSA_GROUPS, NSA_HPG, NSA_DK).transpose(0, 2, 1, 3).reshape(D_MODEL, 512)
    pad = lambda a: jnp.pad(a, ((0, 0), (0, LANES - a.shape[1])))
    segs = [wq, kc, vc, ks, vs, kw, vw, pad(wg), mq, mk, mv, mo, pad(jnp.concatenate([mi, mf], axis=1))]
    return jnp.concatenate(segs, axis=1).astype(BF16)


def _project(x2d, w_all, tm):
    T = x2d.shape[0]
    n = w_all.shape[1]
    out_shape = tuple(jax.ShapeDtypeStruct((T, wd), dt) for _, wd, dt in _SEGS)
    out_specs = tuple(pl.BlockSpec((tm, wd), lambda i: (i, 0)) for _, wd, _ in _SEGS)
    return pl.pallas_call(
        _proj_kernel, grid=(T // tm,),
        in_specs=[pl.BlockSpec((tm, D_MODEL), lambda i: (i, 0)),
                  pl.BlockSpec((D_MODEL, n), lambda i: (0, 0))],
        out_specs=out_specs, out_shape=out_shape,
        compiler_params=_params(("parallel",)), name="proj")(x2d, w_all)


def _cmp_kernel(r_ref, pa_ref, pb_ref, wa_ref, wb_ref, w2_ref, o_ref):
    r = r_ref[...]
    a = jnp.dot((r + pa_ref[...]).astype(BF16), wa_ref[...], preferred_element_type=F32)
    b = jnp.dot((r + pb_ref[...]).astype(BF16), wb_ref[...], preferred_element_type=F32)
    nr = r.shape[0]
    hid = a + pltpu.roll(b, nr - 1, 0)
    hid = hid * jax.nn.sigmoid(hid)
    out = jnp.dot(hid.astype(BF16), w2_ref[...], preferred_element_type=F32)
    row = lax.broadcasted_iota(I32, out.shape, 0)
    o_ref[...] = jnp.where(row < nr - 1, out, 0.0).astype(o_ref.dtype)


def _prep_cmp(pos_k, w1_k, w2_k, pos_v, w1_v, w2_v):
    eye = jnp.eye(NSA_GROUPS, dtype=F32)

    def expand_w1(w1, half):
        w = w1.reshape(NSA_CMP_LEN, NSA_DK, NSA_DK)[half * 16:(half + 1) * 16]
        return jnp.einsum('jde,gk->jgdke', w, eye).reshape(16, 128, 128)

    def both(fk, fv):
        z = jnp.zeros_like(fk)
        top = jnp.concatenate([fk, z], axis=-1)
        bot = jnp.concatenate([z, fv], axis=-1)
        return jnp.concatenate([top, bot], axis=-2)

    wa = both(expand_w1(w1_k, 0), expand_w1(w1_v, 0)).reshape(16 * 256, 256).astype(BF16)
    wb = both(expand_w1(w1_k, 1), expand_w1(w1_v, 1)).reshape(16 * 256, 256).astype(BF16)
    w2 = both(jnp.kron(eye, w2_k), jnp.kron(eye, w2_v)).astype(BF16)

    def pos_row(half):
        pk = jnp.tile(pos_k[half * 16:(half + 1) * 16], (1, NSA_GROUPS))
        pv = jnp.tile(pos_v[half * 16:(half + 1) * 16], (1, NSA_GROUPS))
        return jnp.concatenate([pk, pv], axis=1).reshape(1, 16 * 256)

    return pos_row(0), pos_row(1), wa, wb, w2


def _compress(cmp2d, B, S, prep):
    pa, pb, wa, wb, w2 = prep
    nr = S // NSA_CMP_STRIDE
    r = cmp2d.reshape(B, nr, NSA_CMP_STRIDE * 256)
    full = lambda a: pl.BlockSpec(a.shape, lambda b: (0,) * a.ndim)
    return pl.pallas_call(
        _cmp_kernel, grid=(B,),
        in_specs=[pl.BlockSpec((None, nr, NSA_CMP_STRIDE * 256), lambda b: (b, 0, 0)),
                  full(pa), full(pb), full(wa), full(wb), full(w2)],
        out_specs=pl.BlockSpec((None, nr, 256), lambda b: (b, 0, 0)),
        out_shape=jax.ShapeDtypeStruct((B, nr, 256), BF16),
        compiler_params=_params(("parallel",)), name="cmp")(r, pa, pb, wa, wb, w2)


def _nsa_consts(S):
    n_cmp = (S - NSA_CMP_LEN) // NSA_CMP_STRIDE + 1
    n_sel = S // NSA_SEL_BLOCK
    cs = np.arange(n_cmp) * NSA_CMP_STRIDE
    ss = np.arange(n_sel) * NSA_SEL_BLOCK
    ov = ((cs[:, None] < ss[None, :] + NSA_SEL_BLOCK) & (cs[:, None] + NSA_CMP_LEN > ss[None, :]))
    ovt = np.zeros((LANES, S // NSA_CMP_STRIDE), np.float32)
    ovt[:n_sel, :n_cmp] = ov.T
    e = np.zeros((LANES, S), np.float32)
    e[np.arange(S) // NSA_SEL_BLOCK, np.arange(S)] = 1.0
    return jnp.asarray(ovt, BF16), jnp.asarray(e, BF16)


def _nsa_kernel(q_ref, kcvc_ref, ksel_ref, vsel_ref, kwin_ref, vwin_ref, gates_ref, ovt_ref, e_ref,
                y_ref, bias_ref, *, tq, ck, n_sel):
    G, H = NSA_GROUPS, NSA_HPG
    GH = G * H
    M = GH * tq
    S = ksel_ref.shape[0]
    W = NSA_WINDOW
    ws = min(W + tq, S)
    t0 = pl.program_id(1) * tq
    gates = jax.nn.sigmoid(gates_ref[...])
    lane = lax.broadcasted_iota(I32, (tq, LANES), 1)
    t_col = t0 + lax.broadcasted_iota(I32, (tq, 1), 0)
    kc = kcvc_ref[:, 0:LANES]
    vc = kcvc_ref[:, LANES:2 * LANES]
    nc = kc.shape[0]
    qs = []
    for g in range(G):
        gmask = (lane // NSA_DK) == g
        for h in range(H):
            qh = q_ref[:, h * LANES:(h + 1) * LANES] * (NSA_DK ** -0.5)
            qs.append(jnp.where(gmask, qh, jnp.zeros_like(qh)))
    Q = jnp.concatenate(qs, axis=0).astype(BF16)

    s = lax.dot_general(Q, kc, _DN_T, preferred_element_type=F32)
    c_idx = lax.broadcasted_iota(I32, (tq, nc), 1)
    cmask = (c_idx * NSA_CMP_STRIDE + NSA_CMP_LEN - 1) <= t_col
    s3 = jnp.where(cmask[None], s.reshape(GH, tq, nc), NEG)
    p = jnp.exp(s3 - jnp.max(s3, axis=-1, keepdims=True))
    p = p / jnp.sum(p, axis=-1, keepdims=True)
    p = jnp.where(cmask[None], p, 0.0)
    o_cmp = jnp.dot(p.reshape(M, nc).astype(BF16), vc, preferred_element_type=F32).reshape(GH, tq, LANES)

    for g in range(G):
        psum = jnp.sum(p[g * H:(g + 1) * H], axis=0)
        hi = psum.astype(BF16)
        lo = (psum - hi.astype(F32)).astype(BF16)
        ovt = ovt_ref[...]
        pslt = (lax.dot_general(ovt, hi, _DN_T, preferred_element_type=F32) +
                lax.dot_general(ovt, lo, _DN_T, preferred_element_type=F32))
        imp_p = pslt[0:n_sel, :]
        n_i = lax.broadcasted_iota(I32, (n_sel, tq), 0)
        cur = (t0 + lax.broadcasted_iota(I32, (n_sel, tq), 1)) // NSA_SEL_BLOCK
        forced = (n_i == 0) | (n_i == cur) | (n_i == cur - 1)
        imp = jnp.where(n_i <= cur, imp_p + jnp.where(forced, FORCE_BONUS, 0.0), NEG)
        cnt = jnp.zeros((n_sel, tq), F32)
        for m in range(n_sel):
            row = imp[m:m + 1, :]
            beats = (row > imp) | ((row == imp) & (n_i > m))
            cnt = cnt + jnp.where(beats, 1.0, 0.0)
        selt = jnp.where(cnt < float(min(NSA_SEL_TOPN, n_sel)), 1.0, 0.0)
        selt = jnp.concatenate([selt, jnp.zeros((LANES - n_sel, tq), F32)], axis=0)
        sel = selt.T.astype(BF16)
        maskf = jnp.dot(sel, e_ref[...], preferred_element_type=F32)
        kpos = lax.broadcasted_iota(I32, (tq, S), 1)
        bias_ref[g] = jnp.where((maskf > 0.5) & (kpos <= t_col), 0.0, NEG)

    vlane = lax.broadcasted_iota(I32, (1, LANES), 1) // NSA_DK

    def pv_with_sums(pb, v):
        outs = []
        for g in range(G):
            vg = jnp.where(vlane == g, v, jnp.ones_like(v))
            outs.append(jnp.dot(pb[g * H * tq:(g + 1) * H * tq], vg, preferred_element_type=F32))
        return jnp.concatenate(outs, axis=0)

    def normalise(acc):
        outs = []
        for r in range(GH):
            c = NSA_DK * (1 - r // H)
            outs.append(acc[r] / acc[r][:, c:c + 1])
        return outs

    def sel_body(j, carry):
        m_i, acc = carry
        ks = pl.multiple_of(j * ck, ck)
        k = ksel_ref[pl.ds(ks, ck), :]
        v = vsel_ref[pl.ds(ks, ck), :]
        sj = lax.dot_general(Q, k, _DN_T, preferred_element_type=F32)
        sj = (sj.reshape(G, H, tq, ck) + bias_ref[:, :, pl.ds(ks, ck)][:, None]).reshape(GH, tq, ck)
        m_new = jnp.maximum(m_i, jnp.max(sj, axis=-1, keepdims=True))
        a = jnp.exp(m_i - m_new)
        pj = jnp.exp((sj - m_new).astype(BF16)).reshape(M, ck)
        return m_new, a * acc + pv_with_sums(pj, v).reshape(GH, tq, LANES)

    init = (jnp.full((GH, tq, 1), NEG, F32), jnp.zeros((GH, tq, LANES), F32))
    _, acc = lax.fori_loop(0, (t0 + tq + ck - 1) // ck, sel_body, init)
    o_sel = normalise(acc)

    kst = pl.multiple_of(jnp.clip(t0 - W, 0, S - ws), LANES)
    kwn = kwin_ref[pl.ds(kst, ws), :]
    vwn = vwin_ref[pl.ds(kst, ws), :]
    sw = lax.dot_general(Q, kwn, _DN_T, preferred_element_type=F32)
    wpos = kst + lax.broadcasted_iota(I32, (tq, ws), 1)
    wmask = (wpos <= t_col) & (wpos > t_col - W)
    sw3 = jnp.where(wmask[None], sw.reshape(GH, tq, ws), NEG)
    pw = jnp.exp((sw3 - jnp.max(sw3, axis=-1, keepdims=True)).astype(BF16)).reshape(M, ws)
    o_win = normalise(pv_with_sums(pw, vwn).reshape(GH, tq, LANES))

    g0mask = lane < NSA_DK
    for h in range(H):
        o_g = []
        for g in range(G):
            r = g * H + h
            c0 = r * 3
            o_g.append(gates[:, c0:c0 + 1] * o_cmp[r] + gates[:, c0 + 1:c0 + 2] * o_sel[r] +
                       gates[:, c0 + 2:c0 + 3] * o_win[r])
        y_ref[:, h * LANES:(h + 1) * LANES] = jnp.where(g0mask, o_g[0], o_g[1]).astype(y_ref.dtype)


def _nsa(q, kcvc, ksel, vsel, kwin, vwin, gates, B, S, tq, ck):
    T = B * S
    nq = S // tq
    ovt, e = _nsa_consts(S)
    seq = lambda a: a.reshape(B, S, LANES)
    kv_spec = pl.BlockSpec((None, S, LANES), lambda b, i: (b, 0, 0))
    kern = functools.partial(_nsa_kernel, tq=tq, ck=ck, n_sel=S // NSA_SEL_BLOCK)
    return pl.pallas_call(
        kern, grid=(B, nq),
        in_specs=[pl.BlockSpec((tq, 512), lambda b, i: (b * nq + i, 0)),
                  pl.BlockSpec((None,) + kcvc.shape[1:], lambda b, i: (b, 0, 0)),
                  kv_spec, kv_spec, kv_spec, kv_spec,
                  pl.BlockSpec((tq, LANES), lambda b, i: (b * nq + i, 0)),
                  pl.BlockSpec(ovt.shape, lambda b, i: (0, 0)),
                  pl.BlockSpec(e.shape, lambda b, i: (0, 0))],
        out_specs=pl.BlockSpec((tq, 512), lambda b, i: (b * nq + i, 0)),
        out_shape=jax.ShapeDtypeStruct((T, 512), BF16),
        scratch_shapes=[pltpu.VMEM((NSA_GROUPS, tq, S), F32)],
        compiler_params=_params(("parallel", "parallel")), name="nsa",
    )(q, kcvc, seq(ksel), seq(vsel), seq(kwin), seq(vwin), gates, ovt, e)


def _mlstm_kernel(q_ref, k_ref, v_ref, o_ref, gn_ref, gt_ref, cw_ref, bn_ref, bt_ref, ng_ref, tri_ref,
                  y_ref, c_scr, n_scr):
    H, dh, L = ML_HEADS, ML_DH, ML_CHUNK
    nb, S = q_ref.shape[0], q_ref.shape[1]
    nchunk = S // L
    c_scr[...] = jnp.zeros_like(c_scr)
    n_scr[...] = jnp.zeros_like(n_scr)
    row = lax.broadcasted_iota(I32, (L, H * dh), 0)
    li = lax.broadcasted_iota(I32, (L, L), 0)
    mi = lax.broadcasted_iota(I32, (L, L), 1)
    causal = mi <= li
    tril = tri_ref[0]
    triu = tri_ref[1]
    hp = lax.Precision.HIGHEST

    def conv_silu(ref, bi, c, wofs):
        r0 = pl.multiple_of(c * L, L)
        rp = pl.multiple_of(jnp.maximum(c - 1, 0) * L, L)
        cur = ref[bi, pl.ds(r0, L), :].astype(F32)
        prev = ref[bi, pl.ds(rp, L), :].astype(F32) * jnp.where(c > 0, 1.0, 0.0)
        acc = cur * cw_ref[ML_CONV - 1:ML_CONV, wofs:wofs + H * dh]
        for j in range(1, ML_CONV):
            sh = jnp.where(row < j, pltpu.roll(prev, j, 0), pltpu.roll(cur, j, 0))
            acc = acc + sh * cw_ref[ML_CONV - 1 - j:ML_CONV - j, wofs:wofs + H * dh]
        return acc * jax.nn.sigmoid(acc)

    def body(c, m_state):
        r0 = pl.multiple_of(c * L, L)
        new_m = []
        for bi in range(nb):
            qa = conv_silu(q_ref, bi, c, 0) * (dh ** -0.5)
            ka = conv_silu(k_ref, bi, c, H * dh)
            va = v_ref[bi, pl.ds(r0, L), :]
            oa = o_ref[bi, pl.ds(r0, L), :].astype(F32)
            gn = gn_ref[bi, pl.ds(r0, L), :] + bn_ref[...]
            gt = gt_ref[bi, :, c, :] + bt_ref[...]
            lf_n = jax.nn.log_sigmoid(gn)
            lf_t = jax.nn.log_sigmoid(gt)
            b_n = jnp.dot(tril, lf_n, precision=hp, preferred_element_type=F32)
            b_t = jnp.dot(lf_t, triu, precision=hp, preferred_element_type=F32)
            for h in range(H):
                st = bi * H + h
                q = qa[:, h * dh:(h + 1) * dh]
                k = ka[:, h * dh:(h + 1) * dh]
                v = va[:, h * dh:(h + 1) * dh]
                m_old = m_state[st]
                b_col = b_n[:, H + h:H + h + 1]
                i_col = gn[:, h:h + 1]
                b_row = b_t[H + h:H + h + 1, :]
                i_row = gt[h:h + 1, :]
                g_tot = b_t[H + h:H + h + 1, L - 1:L]
                d_log = jnp.where(causal, b_col - b_row + i_row, NEG)
                inter = b_col + m_old
                m_q = jnp.maximum(inter, jnp.max(d_log, axis=-1, keepdims=True))
                w_intra = jnp.exp(d_log - m_q)
                w_inter = jnp.exp(inter - m_q)
                qb = q.astype(BF16)
                s = lax.dot_general(qb, k.astype(BF16), _DN_T, preferred_element_type=F32) * w_intra
                cst = c_scr[st]
                nst = n_scr[st]
                num = (w_inter * jnp.dot(qb, cst.astype(BF16), preferred_element_type=F32) +
                       jnp.dot(s.astype(BF16), v, preferred_element_type=F32))
                den = w_inter * jnp.sum(q * nst, axis=-1, keepdims=True) + jnp.sum(s, axis=-1, keepdims=True)
                hv = num / jnp.maximum(jnp.abs(den), jnp.exp(-m_q))
                log_k = g_tot - b_col + i_col
                m_new = jnp.maximum(g_tot + m_old, jnp.max(log_k, axis=0, keepdims=True))
                wk = jnp.exp(log_k - m_new)
                decay = jnp.exp(g_tot + m_old - m_new)
                kw = k * wk
                c_scr[st] = decay * cst + lax.dot_general(kw.astype(BF16), v, _DN_TA, preferred_element_type=F32)
                n_scr[st] = decay * nst + jnp.sum(kw, axis=0, keepdims=True)
                new_m.append(m_new)
                mu = jnp.mean(hv, axis=-1, keepdims=True)
                dv = hv - mu
                var = jnp.mean(dv * dv, axis=-1, keepdims=True)
                hn = dv * lax.rsqrt(var + LN_EPS) * ng_ref[:, h * dh:(h + 1) * dh]
                og = jax.nn.sigmoid(oa[:, h * dh:(h + 1) * dh])
                y_ref[bi, pl.ds(r0, L), h * dh:(h + 1) * dh] = (og * hn).astype(y_ref.dtype)
        return tuple(new_m)

    lax.fori_loop(0, nchunk, body, tuple(jnp.zeros((1, 1), F32) for _ in range(nb * H)))


def _mlstm(mq, mk, mv, mo, mif, conv_w, i_bias, f_bias, norm_g, B, S, nb):
    T = B * S
    H, dh, L = ML_HEADS, ML_DH, ML_CHUNK
    W = H * dh
    gt = mif[:, :2 * H].reshape(B, S, 2 * H).transpose(0, 2, 1).reshape(B, 2 * H, S // L, L)
    cw = conv_w.reshape(ML_CONV, 2 * W)
    bias = jnp.concatenate([i_bias, f_bias])
    bn = jnp.pad(bias, (0, LANES - 2 * H)).reshape(1, LANES)
    bt = bias.reshape(2 * H, 1)
    ng = norm_g.reshape(1, W)
    tri = jnp.stack([jnp.tril(jnp.ones((L, L), F32)), jnp.triu(jnp.ones((L, L), F32))])
    seq = lambda a: a.reshape(B, S, a.shape[1])
    rows = lambda w: pl.BlockSpec((nb, S, w), lambda b: (b, 0, 0))
    full = lambda a: pl.BlockSpec(a.shape, lambda b: (0,) * a.ndim)
    y = pl.pallas_call(
        _mlstm_kernel, grid=(B // nb,),
        in_specs=[rows(W), rows(W), rows(W), rows(W), rows(LANES),
                  pl.BlockSpec((nb, 2 * H, S // L, L), lambda b: (b, 0, 0, 0)),
                  full(cw), full(bn), full(bt), full(ng), full(tri)],
        out_specs=rows(W),
        out_shape=jax.ShapeDtypeStruct((B, S, W), BF16),
        scratch_shapes=[pltpu.VMEM((nb * H, dh, dh), F32), pltpu.VMEM((nb * H, 1, dh), F32)],
        compiler_params=_params(("parallel",)), name="mlstm",
    )(seq(mq), seq(mk), seq(mv), seq(mo), seq(mif), gt, cw, bn, bt, ng, tri)
    return y.reshape(T, W)


def _outproj_kernel(yn_ref, ym_ref, x_ref, w_ref, g_ref, b_ref, o_ref):
    mix = (jnp.dot(yn_ref[...], w_ref[0:512, :], preferred_element_type=F32) +
           jnp.dot(ym_ref[...], w_ref[512:1024, :], preferred_element_type=F32))
    o_ref[...] = _ln_rows(ALPHA * x_ref[...] + mix, g_ref[...], b_ref[...])


def _outproj(y_nsa, y_ml, x2d, w_out, g, b, tm):
    T = x2d.shape[0]
    wn = w_out[:512].reshape(NSA_GROUPS, NSA_HPG, NSA_DK, D_MODEL).transpose(1, 0, 2, 3).reshape(512, D_MODEL)
    w = jnp.concatenate([wn, w_out[512:]], axis=0).astype(BF16)
    row = lambda wd: pl.BlockSpec((tm, wd), lambda i: (i, 0))
    full = lambda a: pl.BlockSpec(a.shape, lambda i: (0,) * a.ndim)
    g2, b2 = g.reshape(1, -1), b.reshape(1, -1)
    return pl.pallas_call(
        _outproj_kernel, grid=(T // tm,),
        in_specs=[row(512), row(512), row(D_MODEL), full(w), full(g2), full(b2)],
        out_specs=row(D_MODEL), out_shape=jax.ShapeDtypeStruct((T, D_MODEL), F32),
        compiler_params=_params(("parallel",)), name="outproj")(y_nsa, y_ml, x2d, w, g2, b2)


def _memkv_kernel(m_ref, w_ref, o_ref):
    o_ref[...] = jnp.dot(m_ref[...].astype(BF16), w_ref[...], preferred_element_type=F32).astype(o_ref.dtype)


def _memkv(mem2d, wk, wv):
    w = jnp.concatenate([wk, wv], axis=1).astype(BF16)
    R = mem2d.shape[0]
    return pl.pallas_call(
        _memkv_kernel, grid=(R // MEM_LEN,),
        in_specs=[pl.BlockSpec((MEM_LEN, D_MODEL), lambda i: (i, 0)),
                  pl.BlockSpec(w.shape, lambda i: (0, 0))],
        out_specs=pl.BlockSpec((MEM_LEN, 2 * D_MODEL), lambda i: (i, 0)),
        out_shape=jax.ShapeDtypeStruct((R, 2 * D_MODEL), BF16),
        compiler_params=_params(("parallel",)), name="memkv")(mem2d, w)


def _xattn_kernel(x_ref, kv_ref, wq_ref, wo_ref, g_ref, b_ref, rw_ref, x2_ref, x2p_ref, sc_ref):
    x1 = x_ref[...]
    q = jnp.dot(x1.astype(BF16), wq_ref[...], preferred_element_type=F32).astype(BF16)
    outs = []
    for h in range(XA_HEADS):
        qh = q[:, h * XA_DH:(h + 1) * XA_DH]
        kh = kv_ref[:, h * XA_DH:(h + 1) * XA_DH]
        vh = kv_ref[:, D_MODEL + h * XA_DH:D_MODEL + (h + 1) * XA_DH]
        s = lax.dot_general(qh, kh, _DN_T, preferred_element_type=F32) * (XA_DH ** -0.5)
        p = jnp.exp(s - jnp.max(s, axis=-1, keepdims=True))
        p = p / jnp.sum(p, axis=-1, keepdims=True)
        outs.append(jnp.dot(p.astype(BF16), vh, preferred_element_type=F32).astype(BF16))
    o = jnp.concatenate(outs, axis=1)
    xa = jnp.dot(o, wo_ref[...], preferred_element_type=F32)
    x2 = _ln_rows(ALPHA * x1 + xa, g_ref[...], b_ref[...])
    x2_ref[...] = x2
    x2p_ref[...] = _pack_bf16_pairs(x2)
    xh = x2.astype(BF16)
    xl = (x2 - xh.astype(F32)).astype(BF16)
    wh = rw_ref[0]
    wl = rw_ref[1]
    logit = (lax.dot_general(wh, xh, _DN_T, preferred_element_type=F32) +
             lax.dot_general(wh, xl, _DN_T, preferred_element_type=F32) +
             lax.dot_general(wl, xh, _DN_T, preferred_element_type=F32))
    sc_ref[...] = jax.nn.sigmoid(logit)


def _xattn(x1, kv, wq, wo, g, b, router_w, S, tq):
    T = x1.shape[0]
    wqb, wob = wq.astype(BF16), wo.astype(BF16)
    rwt = router_w.T
    rh = rwt.astype(BF16)
    rw = jnp.stack([rh, (rwt - rh.astype(F32)).astype(BF16)])
    g2, b2 = g.reshape(1, -1), b.reshape(1, -1)
    full = lambda a: pl.BlockSpec(a.shape, lambda i: (0,) * a.ndim)
    per = S // tq
    return pl.pallas_call(
        _xattn_kernel, grid=(T // tq,),
        in_specs=[pl.BlockSpec((tq, D_MODEL), lambda i: (i, 0)),
                  pl.BlockSpec((MEM_LEN, 2 * D_MODEL), lambda i: (i // per, 0)),
                  full(wqb), full(wob), full(g2), full(b2), full(rw)],
        out_specs=(pl.BlockSpec((tq, D_MODEL), lambda i: (i, 0)),
                   pl.BlockSpec((tq, D_MODEL // 2), lambda i: (i, 0)),
                   pl.BlockSpec((MOE_E, tq), lambda i: (0, i))),
        out_shape=(jax.ShapeDtypeStruct((T, D_MODEL), F32), jax.ShapeDtypeStruct((T, D_MODEL // 2), I32),
                   jax.ShapeDtypeStruct((MOE_E, T), F32)),
        compiler_params=_params(("parallel",)), name="xattn")(x1, kv, wqb, wob, g2, b2, rw)


def _route_kernel(sc_ref, rb_ref, idx_ref, w_ref):
    E, G = MOE_E, MOE_GROUPS
    per = E // G
    scores = sc_ref[...]
    tr = scores.shape[1]
    biased = scores + rb_ref[...]
    g3 = biased.reshape(G, per, tr)
    j3 = lax.broadcasted_iota(I32, (G, per, tr), 1)
    m1 = jnp.max(g3, axis=1, keepdims=True)
    first = jnp.min(jnp.where(g3 == m1, j3, per), axis=1, keepdims=True)
    m2 = jnp.max(jnp.where(j3 == first, -jnp.inf, g3), axis=1, keepdims=True)
    gs = (m1 + m2).reshape(G, tr)
    gi = lax.broadcasted_iota(I32, (G, tr), 0)
    cnt = jnp.zeros((G, tr), F32)
    for m in range(G):
        row = gs[m:m + 1, :]
        cnt = cnt + jnp.where((row > gs) | ((row == gs) & (gi > m)), 1.0, 0.0)
    gmask = cnt < float(MOE_TOPK_GROUPS)
    masked = jnp.where(gmask[:, None, :], g3, NEG).reshape(E, tr)
    ei = lax.broadcasted_iota(I32, (E, tr), 0)
    idxs, ws = [], []
    for _ in range(MOE_K):
        mx = jnp.max(masked, axis=0, keepdims=True)
        ix = jnp.min(jnp.where(masked == mx, ei, E), axis=0, keepdims=True)
        hit = ei == ix
        ws.append(jnp.sum(jnp.where(hit, scores, 0.0), axis=0, keepdims=True))
        idxs.append(ix)
        masked = jnp.where(hit, -jnp.inf, masked)
    w = jnp.concatenate(ws, axis=0)
    idx_ref[...] = jnp.concatenate(idxs, axis=0)
    w_ref[...] = w / jnp.sum(w, axis=0, keepdims=True) * MOE_ROUTE_SCALE


def _route(scores_t, router_bias, tr):
    E, T = scores_t.shape
    rb = router_bias.reshape(E, 1)
    return pl.pallas_call(
        _route_kernel, grid=(T // tr,),
        in_specs=[pl.BlockSpec((E, tr), lambda i: (0, i)), pl.BlockSpec((E, 1), lambda i: (0, 0))],
        out_specs=(pl.BlockSpec((MOE_K, tr), lambda i: (0, i)), pl.BlockSpec((MOE_K, tr), lambda i: (0, i))),
        out_shape=(jax.ShapeDtypeStruct((MOE_K, T), I32), jax.ShapeDtypeStruct((MOE_K, T), F32)),
        compiler_params=_params(("parallel",)), name="route")(scores_t, rb)


def _rank_kernel(idx_ref, u_ref, rank_ref, cnt_ref, carry):
    E = MOE_E

    @pl.when(pl.program_id(0) == 0)
    def _():
        carry[...] = jnp.zeros_like(carry)

    idx = idx_ref[...]
    tp = idx.shape[1]
    ei = lax.broadcasted_iota(I32, (E, tp), 0)
    hits = [ei == idx[k:k + 1, :] for k in range(MOE_K)]
    onehot = jnp.zeros((E, tp), F32)
    for hit in hits:
        onehot = onehot + jnp.where(hit, 1.0, 0.0)
    pos = jnp.dot(onehot.astype(BF16), u_ref[...], preferred_element_type=F32) + carry[...]
    ranks = [jnp.sum(jnp.where(hit, pos, 0.0), axis=0, keepdims=True) for hit in hits]
    rank_ref[...] = jnp.concatenate(ranks, axis=0).astype(I32)
    total = carry[...] + jnp.sum(onehot, axis=1, keepdims=True)
    carry[...] = total
    cnt_ref[...] = jnp.broadcast_to(total, cnt_ref.shape).astype(I32)


def _rank(idx_t, tp):
    K, T = idx_t.shape
    u = jnp.triu(jnp.ones((tp, tp), F32), k=1).astype(BF16)
    rank, cnt = pl.pallas_call(
        _rank_kernel, grid=(T // tp,),
        in_specs=[pl.BlockSpec((K, tp), lambda i: (0, i)), pl.BlockSpec((tp, tp), lambda i: (0, 0))],
        out_specs=(pl.BlockSpec((K, tp), lambda i: (0, i)), pl.BlockSpec((MOE_E, LANES), lambda i: (0, 0))),
        out_shape=(jax.ShapeDtypeStruct((K, T), I32), jax.ShapeDtypeStruct((MOE_E, LANES), I32)),
        scratch_shapes=[pltpu.VMEM((MOE_E, 1), F32)],
        compiler_params=_params(("arbitrary",)), name="rank")(idx_t, u)
    return rank, cnt[:, 0]


def _dest_kernel(idx_ref, rank_ref, po_ref, dest_ref):
    idx = idx_ref[...]
    tp = idx.shape[1]
    ei = lax.broadcasted_iota(I32, (MOE_E, tp), 0)
    po = po_ref[...]
    base = [jnp.sum(jnp.where(ei == idx[k:k + 1, :], po, 0.0), axis=0, keepdims=True) for k in range(MOE_K)]
    dest_ref[...] = jnp.concatenate(base, axis=0).astype(I32) + rank_ref[...]


def _dest(idx_t, rank_t, poffs, tp):
    K, T = idx_t.shape
    po = poffs.astype(F32).reshape(MOE_E, 1)
    spec = pl.BlockSpec((K, tp), lambda i: (0, i))
    return pl.pallas_call(
        _dest_kernel, grid=(T // tp,),
        in_specs=[spec, spec, pl.BlockSpec((MOE_E, 1), lambda i: (0, 0))],
        out_specs=spec, out_shape=jax.ShapeDtypeStruct((K, T), I32),
        compiler_params=_params(("parallel",)), name="dest")(idx_t, rank_t, po)


def _pack_bf16_pairs(v):
    m = v.shape[1] // 2
    bits = lax.bitcast_convert_type(v.astype(BF16).astype(F32), jnp.uint32)
    return lax.bitcast_convert_type((bits[:, :m] >> 16) | (bits[:, m:] & jnp.uint32(0xFFFF0000)), I32)


def _unpack_bf16_pairs(w):
    w = lax.bitcast_convert_type(w, jnp.uint32)
    lo = lax.bitcast_convert_type(w << 16, F32)
    hi = lax.bitcast_convert_type(w & jnp.uint32(0xFFFF0000), F32)
    return lo, hi


def _row_scatter(rows, dest_t, n_rows):
    T, d = rows.shape
    K = dest_t.shape[0]
    nw = SC_CORES * SC_SUBCORES
    per_w = T // nw
    ch = SC_GATHER_ROWS
    mesh = plsc.VectorSubcoreMesh(core_axis_name="c", subcore_axis_name="s")

    @functools.partial(
        pl.kernel, mesh=mesh, out_type=jax.ShapeDtypeStruct((n_rows, d), I32),
        scratch_types=[pltpu.VMEM((K, ch), I32), pltpu.VMEM((ch, d), I32), pltpu.SemaphoreType.DMA],
        name="row_scatter")
    def scatter(rows_hbm, dest_hbm, out_hbm, idx_v, rows_v, sem):
        wid = lax.axis_index("s") * SC_CORES + lax.axis_index("c")
        base = wid * per_w

        @pl.loop(0, per_w // ch)
        def _(i):
            off = pl.multiple_of(base + i * ch, ch)
            pltpu.sync_copy(rows_hbm.at[pl.ds(off, ch)], rows_v)
            pltpu.sync_copy(dest_hbm.at[:, pl.ds(off, ch)], idx_v)
            copies = [pltpu.async_copy(rows_v, out_hbm.at[idx_v.at[k]], sem) for k in range(K)]
            for cp in copies:
                cp.wait()

    return scatter(rows, dest_t)


def _expert_kernel(po_ref, cnt_ref, xs_hbm, w1_ref, w3_ref, w2_ref, ys_hbm,
                   xbuf, ybuf, w1b, w3b, w2b, gcnt, insem, outsem):
    e = pl.program_id(0)
    ne = pl.num_programs(0)
    n = cnt_ref[e]
    base = po_ref[e]
    rb = xbuf.shape[1]
    hw = D_MODEL // 2
    nblk = jnp.maximum((n + rb - 1) // rb, 1)
    w1b[...] = w1_ref[...].astype(BF16)
    w3b[...] = w3_ref[...].astype(BF16)
    w2b[...] = w2_ref[...].astype(BF16)

    def in_copy(r0, slot):
        return pltpu.make_async_copy(xs_hbm.at[pl.ds(pl.multiple_of(r0, rb), rb)], xbuf.at[slot], insem.at[slot])

    def out_copy(r0, slot):
        return pltpu.make_async_copy(ybuf.at[slot], ys_hbm.at[pl.ds(pl.multiple_of(r0, rb), rb)], outsem.at[slot])

    @pl.when(e == 0)
    def _():
        gcnt[0] = 0
        in_copy(base, 0).start()

    g0 = gcnt[0]

    def body(j, c):
        slot = (g0 + j) % 2

        @pl.when(j + 1 < nblk)
        def _():
            in_copy(base + (j + 1) * rb, 1 - slot).start()

        @pl.when((j + 1 == nblk) & (e + 1 < ne))
        def _():
            in_copy(po_ref[jnp.minimum(e + 1, ne - 1)], 1 - slot).start()

        in_copy(0, slot).wait()

        @pl.when(g0 + j >= 2)
        def _():
            out_copy(0, slot).wait()

        words = xbuf[slot]
        row = j * rb + lax.broadcasted_iota(I32, words.shape, 0)
        lo, hi = _unpack_bf16_pairs(jnp.where(row < n, words, 0))
        lo, hi = lo.astype(BF16), hi.astype(BF16)
        a = (jnp.dot(lo, w1b[0:hw, :], preferred_element_type=F32) +
             jnp.dot(hi, w1b[hw:, :], preferred_element_type=F32))
        g = (jnp.dot(lo, w3b[0:hw, :], preferred_element_type=F32) +
             jnp.dot(hi, w3b[hw:, :], preferred_element_type=F32))
        h = (a * jax.nn.sigmoid(a) * g).astype(BF16)
        ybuf[slot] = _pack_bf16_pairs(jnp.dot(h, w2b[...], preferred_element_type=F32))
        out_copy(base + j * rb, slot).start()
        return c

    lax.fori_loop(0, nblk, body, 0)
    total = g0 + nblk
    gcnt[0] = total

    @pl.when(e + 1 == ne)
    def _():
        @pl.when(total >= 2)
        def _():
            out_copy(0, total % 2).wait()

        out_copy(0, (total - 1) % 2).wait()


def _experts(poffs, counts, xs, w1, w3, w2):
    n_rows, wp = xs.shape
    rb = ROW_BLOCK
    wspec = lambda shape: pl.BlockSpec((None,) + shape, lambda e, po, cn: (e, 0, 0))
    gs = pltpu.PrefetchScalarGridSpec(
        num_scalar_prefetch=2, grid=(MOE_E,),
        in_specs=[pl.BlockSpec(memory_space=pl.ANY), wspec((D_MODEL, MOE_FF)), wspec((D_MODEL, MOE_FF)),
                  wspec((MOE_FF, D_MODEL))],
        out_specs=pl.BlockSpec(memory_space=pl.ANY),
        scratch_shapes=[pltpu.VMEM((2, rb, wp), I32), pltpu.VMEM((2, rb, wp), I32),
                        pltpu.VMEM((D_MODEL, MOE_FF), BF16), pltpu.VMEM((D_MODEL, MOE_FF), BF16),
                        pltpu.VMEM((MOE_FF, D_MODEL), BF16), pltpu.SMEM((1,), I32),
                        pltpu.SemaphoreType.DMA((2,)), pltpu.SemaphoreType.DMA((2,))])
    return pl.pallas_call(
        _expert_kernel, grid_spec=gs, out_shape=jax.ShapeDtypeStruct((n_rows, wp), I32),
        compiler_params=_params(("arbitrary",)), name="experts")(poffs, counts, xs, w1, w3, w2)


def _row_gather(table, idx):
    n, d = idx.shape[0], table.shape[1]
    nw = SC_CORES * SC_SUBCORES
    per_w = n // nw
    ch = SC_GATHER_ROWS // 2
    mesh = plsc.VectorSubcoreMesh(core_axis_name="c", subcore_axis_name="s")

    @functools.partial(
        pl.kernel, mesh=mesh, out_type=jax.ShapeDtypeStruct((n, d), I32),
        scratch_types=[pltpu.VMEM((ch,), I32), pltpu.VMEM((ch,), I32),
                       pltpu.VMEM((ch, d), I32), pltpu.VMEM((ch, d), I32),
                       pltpu.SemaphoreType.DMA, pltpu.SemaphoreType.DMA,
                       pltpu.SemaphoreType.DMA, pltpu.SemaphoreType.DMA],
        name="row_gather")
    def gather(table_hbm, idx_hbm, out_hbm, idx0, idx1, rows0, rows1, g0, g1, w0, w1):
        wid = lax.axis_index("s") * SC_CORES + lax.axis_index("c")
        base = wid * per_w

        @pl.loop(0, per_w // (2 * ch))
        def _(i):
            off0 = pl.multiple_of(base + 2 * i * ch, ch)
            off1 = pl.multiple_of(off0 + ch, ch)
            pltpu.sync_copy(idx_hbm.at[pl.ds(off0, ch)], idx0)
            c0 = pltpu.async_copy(table_hbm.at[idx0], rows0, g0)
            pltpu.sync_copy(idx_hbm.at[pl.ds(off1, ch)], idx1)
            c1 = pltpu.async_copy(table_hbm.at[idx1], rows1, g1)
            c0.wait()
            o0 = pltpu.async_copy(rows0, out_hbm.at[pl.ds(off0, ch)], w0)
            c1.wait()
            o1 = pltpu.async_copy(rows1, out_hbm.at[pl.ds(off1, ch)], w1)
            o0.wait()
            o1.wait()

    return gather(table, idx)


def _combine_kernel(w_ref, x_ref, ysg_ref, s1_ref, s3_ref, s2_ref, g_ref, b_ref, o_ref):
    x2 = x_ref[...]
    xb = x2.astype(BF16)
    a = jnp.dot(xb, s1_ref[...], preferred_element_type=F32)
    c = jnp.dot(xb, s3_ref[...], preferred_element_type=F32)
    shared = jnp.dot((a * jax.nn.sigmoid(a) * c).astype(BF16), s2_ref[...], preferred_element_type=F32)
    w = w_ref[...]
    hw = ysg_ref.shape[2]
    y_lo = jnp.zeros((x2.shape[0], hw), F32)
    y_hi = jnp.zeros((x2.shape[0], hw), F32)
    for k in range(MOE_K):
        lo, hi = _unpack_bf16_pairs(ysg_ref[k])
        y_lo = y_lo + w[:, k:k + 1] * lo
        y_hi = y_hi + w[:, k:k + 1] * hi
    y = shared + jnp.concatenate([y_lo, y_hi], axis=1)
    o_ref[...] = _ln_rows(ALPHA * x2 + y, g_ref[...], b_ref[...])


def _combine(w_nat, x2, ysg, sw1, sw3, sw2, g, b, tc):
    T = x2.shape[0]
    wp = ysg.shape[2]
    full = lambda a: pl.BlockSpec(a.shape, lambda i: (0,) * a.ndim)
    s1, s3, s2 = sw1.astype(BF16), sw3.astype(BF16), sw2.astype(BF16)
    g2, b2 = g.reshape(1, -1), b.reshape(1, -1)
    return pl.pallas_call(
        _combine_kernel, grid=(T // tc,),
        in_specs=[pl.BlockSpec((tc, MOE_K), lambda i: (i, 0)),
                  pl.BlockSpec((tc, D_MODEL), lambda i: (i, 0)),
                  pl.BlockSpec((MOE_K, tc, wp), lambda i: (0, i, 0)),
                  full(s1), full(s3), full(s2), full(g2), full(b2)],
        out_specs=pl.BlockSpec((tc, D_MODEL), lambda i: (i, 0)),
        out_shape=jax.ShapeDtypeStruct((T, D_MODEL), F32),
        compiler_params=_params(("parallel",)), name="combine",
    )(w_nat, x2, ysg, s1, s3, s2, g2, b2)


def _moe(x2, x2p, scores_t, router_bias, w1, w3, w2, sw1, sw3, sw2, g, b, tiles):
    T = x2.shape[0]
    rb = ROW_BLOCK
    idx_t, w_t = _route(scores_t, router_bias, tiles['route'])
    rank_t, counts = _rank(idx_t, tiles['rank'])
    pcounts = jnp.maximum((counts + rb - 1) // rb, 1) * rb
    pends = jnp.cumsum(pcounts)
    poffs = (pends - pcounts).astype(I32)
    n_blocks = -(-T * MOE_K // rb) + MOE_E
    dest_t = _dest(idx_t, rank_t, poffs, tiles['rank'])
    xs = _row_scatter(x2p, dest_t, n_blocks * rb)
    ys = _experts(poffs, counts.astype(I32), xs, w1, w3, w2)
    ysg = _row_gather(ys, dest_t.reshape(-1)).reshape(MOE_K, T, -1)
    return _combine(w_t.T, x2, ysg, sw1, sw3, sw2, g, b, tiles['combine'])


def _tiles(B, S):
    T = B * S
    pick = lambda want, n: want if n % want == 0 else n
    return dict(proj=pick(512, T), nsa_q=pick(128, S), nsa_ck=pick(512, S), outproj=pick(1024, T),
                mlstm_nb=2 if B % 2 == 0 else 1, xattn=pick(1024, S), route=pick(1024, T), rank=pick(512, T),
                combine=pick(512, T))


def kernel(x, mem, w_in, nsa_pos_k, nsa_cmp_k_w1, nsa_cmp_k_w2, nsa_pos_v, nsa_cmp_v_w1, nsa_cmp_v_w2,
           mlstm_conv_w, mlstm_i_bias, mlstm_f_bias, mlstm_norm_g, w_out, ln1_g, ln1_b,
           xa_wq, xa_wk, xa_wv, xa_wo, ln2_g, ln2_b, router_w, router_bias,
           moe_w1, moe_w3, moe_w2, shared_w1, shared_w3, shared_w2, ln3_g, ln3_b):
    B, S, D = x.shape
    T = B * S
    tl = _tiles(B, S)
    xc = x.reshape(T, D)
    memc = mem.reshape(B * MEM_LEN, D)
    for l in range(w_in.shape[0]):
        (q, cmp, ksel, vsel, kwin, vwin, gates, mq, mk, mv, mo, mif) = _project(xc, _prep_w_in(w_in[l]), tl['proj'])
        kcvc = _compress(cmp, B, S, _prep_cmp(nsa_pos_k[l], nsa_cmp_k_w1[l], nsa_cmp_k_w2[l],
                                              nsa_pos_v[l], nsa_cmp_v_w1[l], nsa_cmp_v_w2[l]))
        y_nsa = _nsa(q, kcvc, ksel, vsel, kwin, vwin, gates, B, S, tl['nsa_q'], tl['nsa_ck'])
        y_ml = _mlstm(mq, mk, mv, mo, mif, mlstm_conv_w[l], mlstm_i_bias[l], mlstm_f_bias[l],
                      mlstm_norm_g[l], B, S, tl['mlstm_nb'])
        x1 = _outproj(y_nsa, y_ml, xc, w_out[l], ln1_g[l], ln1_b[l], tl['outproj'])
        kv = _memkv(memc, xa_wk[l], xa_wv[l])
        x2, x2p, scores_t = _xattn(x1, kv, xa_wq[l], xa_wo[l], ln2_g[l], ln2_b[l], router_w[l], S, tl['xattn'])
        xc = _moe(x2, x2p, scores_t, router_bias[l], moe_w1[l], moe_w3[l], moe_w2[l],
                  shared_w1[l], shared_w3[l], shared_w2[l], ln3_g[l], ln3_b[l], tl)
    return xc.reshape(B, S, D)
```

```python
import functools
import numpy as np
import jax
import jax.numpy as jnp
from jax import lax
from jax.experimental import pallas as pl
from jax.experimental.pallas import tpu as pltpu
from jax.experimental.pallas import tpu_sc as plsc

F32 = jnp.float32
BF16 = jnp.bfloat16
I32 = jnp.int32

D_MODEL = 1024
MEM_LEN = 256
NSA_HEADS = 8
NSA_GROUPS = 2
NSA_HPG = 4
NSA_DK = 64
NSA_CMP_LEN = 32
NSA_CMP_STRIDE = 16
NSA_SEL_BLOCK = 64
NSA_SEL_TOPN = 8
NSA_WINDOW = 512
ML_HEADS = 4
ML_DH = 128
ML_CHUNK = 64
ML_CONV = 4
XA_HEADS = 4
XA_DH = 256
MOE_E = 256
MOE_K = 8
MOE_GROUPS = 8
MOE_TOPK_GROUPS = 4
MOE_FF = 256
MOE_ROUTE_SCALE = 2.5
DEPTH = 1
ALPHA = (2.0 * DEPTH) ** 0.25
LN_EPS = 1e-5
NEG = -1e30
FORCE_BONUS = 1e4

LANES = 128
ROW_BLOCK = 256
VMEM_LIMIT = 56 * 1024 * 1024
SC_CORES = 2
SC_SUBCORES = 16
SC_GATHER_ROWS = 128

_DN_T = (((1,), (1,)), ((), ()))
_DN_TA = (((0,), (0,)), ((), ()))


def _params(sem):
    return pltpu.CompilerParams(dimension_semantics=sem, vmem_limit_bytes=VMEM_LIMIT)


def _ln_rows(v, g, b):
    mu = jnp.mean(v, axis=-1, keepdims=True)
    d = v - mu
    var = jnp.mean(d * d, axis=-1, keepdims=True)
    return d * lax.rsqrt(var + LN_EPS) * g + b


_SEGS = (('q', 512, BF16), ('cmp', 256, F32), ('ksel', 128, BF16), ('vsel', 128, BF16),
         ('kwin', 128, BF16), ('vwin', 128, BF16), ('gates', 128, F32), ('mq', 512, BF16),
         ('mk', 512, BF16), ('mv', 512, BF16), ('mo', 512, BF16), ('mif', 128, F32))


def _proj_kernel(x_ref, w_ref, *out_refs):
    xb = x_ref[...].astype(BF16)
    off = 0
    for o_ref, (_, wd, _) in zip(out_refs, _SEGS):
        o_ref[...] = jnp.dot(xb, w_ref[:, off:off + wd], preferred_element_type=F32).astype(o_ref.dtype)
        off += wd


def _prep_w_in(w):
    sizes = (512,) + (128,) * 6 + (24,) + (512,) * 4 + (4, 4)
    cuts = np.cumsum(sizes)[:-1].tolist()
    (wq, kc, vc, ks, vs, kw, vw, wg, mq, mk, mv, mo, mi, mf) = jnp.split(w, cuts, axis=1)
    wq = wq.reshape(D_MODEL, NSA_GROUPS, NSA_HPG, NSA_DK).transpose(0, 2, 1, 3).reshape(D_MODEL, 512)
    pad = lambda a: jnp.pad(a, ((0, 0), (0, LANES - a.shape[1])))
    segs = [wq, kc, vc, ks, vs, kw, vw, pad(wg), mq, mk, mv, mo, pad(jnp.concatenate([mi, mf], axis=1))]
    return jnp.concatenate(segs, axis=1).astype(BF16)


def _project(x2d, w_all, tm):
    T = x2d.shape[0]
    n = w_all.shape[1]
    out_shape = tuple(jax.ShapeDtypeStruct((T, wd), dt) for _, wd, dt in _SEGS)
    out_specs = tuple(pl.BlockSpec((tm, wd), lambda i: (i, 0)) for _, wd, _ in _SEGS)
    return pl.pallas_call(
        _proj_kernel, grid=(T // tm,),
        in_specs=[pl.BlockSpec((tm, D_MODEL), lambda i: (i, 0)),
                  pl.BlockSpec((D_MODEL, n), lambda i: (0, 0))],
        out_specs=out_specs, out_shape=out_shape,
        compiler_params=_params(("parallel",)), name="proj")(x2d, w_all)


def _cmp_kernel(r_ref, pa_ref, pb_ref, wa_ref, wb_ref, w2_ref, o_ref):
    r = r_ref[...]
    a = jnp.dot((r + pa_ref[...]).astype(BF16), wa_ref[...], preferred_element_type=F32)
    b = jnp.dot((r + pb_ref[...]).astype(BF16), wb_ref[...], preferred_element_type=F32)
    nr = r.shape[0]
    hid = a + pltpu.roll(b, nr - 1, 0)
    hid = hid * jax.nn.sigmoid(hid)
    out = jnp.dot(hid.astype(BF16), w2_ref[...], preferred_element_type=F32)
    row = lax.broadcasted_iota(I32, out.shape, 0)
    o_ref[...] = jnp.where(row < nr - 1, out, 0.0).astype(o_ref.dtype)


def _prep_cmp(pos_k, w1_k, w2_k, pos_v, w1_v, w2_v):
    eye = jnp.eye(NSA_GROUPS, dtype=F32)

    def expand_w1(w1, half):
        w = w1.reshape(NSA_CMP_LEN, NSA_DK, NSA_DK)[half * 16:(half + 1) * 16]
        return jnp.einsum('jde,gk->jgdke', w, eye).reshape(16, 128, 128)

    def both(fk, fv):
        z = jnp.zeros_like(fk)
        top = jnp.concatenate([fk, z], axis=-1)
        bot = jnp.concatenate([z, fv], axis=-1)
        return jnp.concatenate([top, bot], axis=-2)

    wa = both(expand_w1(w1_k, 0), expand_w1(w1_v, 0)).reshape(16 * 256, 256).astype(BF16)
    wb = both(expand_w1(w1_k, 1), expand_w1(w1_v, 1)).reshape(16 * 256, 256).astype(BF16)
    w2 = both(jnp.kron(eye, w2_k), jnp.kron(eye, w2_v)).astype(BF16)

    def pos_row(half):
        pk = jnp.tile(pos_k[half * 16:(half + 1) * 16], (1, NSA_GROUPS))
        pv = jnp.tile(pos_v[half * 16:(half + 1) * 16], (1, NSA_GROUPS))
        return jnp.concatenate([pk, pv], axis=1).reshape(1, 16 * 256)

    return pos_row(0), pos_row(1), wa, wb, w2


def _compress(cmp2d, B, S, prep):
    pa, pb, wa, wb, w2 = prep
    nr = S // NSA_CMP_STRIDE
    r = cmp2d.reshape(B, nr, NSA_CMP_STRIDE * 256)
    full = lambda a: pl.BlockSpec(a.shape, lambda b: (0,) * a.ndim)
    return pl.pallas_call(
        _cmp_kernel, grid=(B,),
        in_specs=[pl.BlockSpec((None, nr, NSA_CMP_STRIDE * 256), lambda b: (b, 0, 0)),
                  full(pa), full(pb), full(wa), full(wb), full(w2)],
        out_specs=pl.BlockSpec((None, nr, 256), lambda b: (b, 0, 0)),
        out_shape=jax.ShapeDtypeStruct((B, nr, 256), BF16),
        compiler_params=_params(("parallel",)), name="cmp")(r, pa, pb, wa, wb, w2)


def _nsa_consts(S):
    n_cmp = (S - NSA_CMP_LEN) // NSA_CMP_STRIDE + 1
    n_sel = S // NSA_SEL_BLOCK
    cs = np.arange(n_cmp) * NSA_CMP_STRIDE
    ss = np.arange(n_sel) * NSA_SEL_BLOCK
    ov = ((cs[:, None] < ss[None, :] + NSA_SEL_BLOCK) & (cs[:, None] + NSA_CMP_LEN > ss[None, :]))
    ovt = np.zeros((LANES, S // NSA_CMP_STRIDE), np.float32)
    ovt[:n_sel, :n_cmp] = ov.T
    e = np.zeros((LANES, S), np.float32)
    e[np.arange(S) // NSA_SEL_BLOCK, np.arange(S)] = 1.0
    return jnp.asarray(ovt, BF16), jnp.asarray(e, BF16)


def _nsa_kernel(q_ref, kcvc_ref, ksel_ref, vsel_ref, kwin_ref, vwin_ref, gates_ref, ovt_ref, e_ref,
                y_ref, bias_ref, *, tq, ck, n_sel):
    G, H = NSA_GROUPS, NSA_HPG
    GH = G * H
    M = GH * tq
    S = ksel_ref.shape[0]
    W = NSA_WINDOW
    ws = min(W + tq, S)
    t0 = pl.program_id(1) * tq
    gates = jax.nn.sigmoid(gates_ref[...])
    lane = lax.broadcasted_iota(I32, (tq, LANES), 1)
    t_col = t0 + lax.broadcasted_iota(I32, (tq, 1), 0)
    kc = kcvc_ref[:, 0:LANES]
    vc = kcvc_ref[:, LANES:2 * LANES]
    nc = kc.shape[0]
    qs = []
    for g in range(G):
        gmask = (lane // NSA_DK) == g
        for h in range(H):
            qh = q_ref[:, h * LANES:(h + 1) * LANES] * (NSA_DK ** -0.5)
            qs.append(jnp.where(gmask, qh, jnp.zeros_like(qh)))
    Q = jnp.concatenate(qs, axis=0).astype(BF16)

    s = lax.dot_general(Q, kc, _DN_T, preferred_element_type=F32)
    c_idx = lax.broadcasted_iota(I32, (tq, nc), 1)
    cmask = (c_idx * NSA_CMP_STRIDE + NSA_CMP_LEN - 1) <= t_col
    s3 = jnp.where(cmask[None], s.reshape(GH, tq, nc), NEG)
    p = jnp.exp(s3 - jnp.max(s3, axis=-1, keepdims=True))
    p = p / jnp.sum(p, axis=-1, keepdims=True)
    p = jnp.where(cmask[None], p, 0.0)
    o_cmp = jnp.dot(p.reshape(M, nc).astype(BF16), vc, preferred_element_type=F32).reshape(GH, tq, LANES)

    for g in range(G):
        psum = jnp.sum(p[g * H:(g + 1) * H], axis=0)
        hi = psum.astype(BF16)
        lo = (psum - hi.astype(F32)).astype(BF16)
        ovt = ovt_ref[...]
        pslt = (lax.dot_general(ovt, hi, _DN_T, preferred_element_type=F32) +
                lax.dot_general(ovt, lo, _DN_T, preferred_element_type=F32))
        imp_p = pslt[0:n_sel, :]
        n_i = lax.broadcasted_iota(I32, (n_sel, tq), 0)
        cur = (t0 + lax.broadcasted_iota(I32, (n_sel, tq), 1)) // NSA_SEL_BLOCK
        forced = (n_i == 0) | (n_i == cur) | (n_i == cur - 1)
        imp = jnp.where(n_i <= cur, imp_p + jnp.where(forced, FORCE_BONUS, 0.0), NEG)
        cnt = jnp.zeros((n_sel, tq), F32)
        for m in range(n_sel):
            row = imp[m:m + 1, :]
            beats = (row > imp) | ((row == imp) & (n_i > m))
            cnt = cnt + jnp.where(beats, 1.0, 0.0)
        selt = jnp.where(cnt < float(min(NSA_SEL_TOPN, n_sel)), 1.0, 0.0)
        selt = jnp.concatenate([selt, jnp.zeros((LANES - n_sel, tq), F32)], axis=0)
        sel = selt.T.astype(BF16)
        maskf = jnp.dot(sel, e_ref[...], preferred_element_type=F32)
        kpos = lax.broadcasted_iota(I32, (tq, S), 1)
        bias_ref[g] = jnp.where((maskf > 0.5) & (kpos <= t_col), 0.0, NEG)

    vlane = lax.broadcasted_iota(I32, (1, LANES), 1) // NSA_DK

    def pv_with_sums(pb, v):
        outs = []
        for g in range(G):
            vg = jnp.where(vlane == g, v, jnp.ones_like(v))
            outs.append(jnp.dot(pb[g * H * tq:(g + 1) * H * tq], vg, preferred_element_type=F32))
        return jnp.concatenate(outs, axis=0)

    def normalise(acc):
        outs = []
        for r in range(GH):
            c = NSA_DK * (1 - r // H)
            outs.append(acc[r] / acc[r][:, c:c + 1])
        return outs

    def sel_body(j, carry):
        m_i, acc = carry
        ks = pl.multiple_of(j * ck, ck)
        k = ksel_ref[pl.ds(ks, ck), :]
        v = vsel_ref[pl.ds(ks, ck), :]
        sj = lax.dot_general(Q, k, _DN_T, preferred_element_type=F32)
        sj = (sj.reshape(G, H, tq, ck) + bias_ref[:, :, pl.ds(ks, ck)][:, None]).reshape(GH, tq, ck)
        m_new = jnp.maximum(m_i, jnp.max(sj, axis=-1, keepdims=True))
        a = jnp.exp(m_i - m_new)
        pj = jnp.exp((sj - m_new).astype(BF16)).reshape(M, ck)
        return m_new, a * acc + pv_with_sums(pj, v).reshape(GH, tq, LANES)

    init = (jnp.full((GH, tq, 1), NEG, F32), jnp.zeros((GH, tq, LANES), F32))
    _, acc = lax.fori_loop(0, (t0 + tq + ck - 1) // ck, sel_body, init)
    o_sel = normalise(acc)

    kst = pl.multiple_of(jnp.clip(t0 - W, 0, S - ws), LANES)
    kwn = kwin_ref[pl.ds(kst, ws), :]
    vwn = vwin_ref[pl.ds(kst, ws), :]
    sw = lax.dot_general(Q, kwn, _DN_T, preferred_element_type=F32)
    wpos = kst + lax.broadcasted_iota(I32, (tq, ws), 1)
    wmask = (wpos <= t_col) & (wpos > t_col - W)
    sw3 = jnp.where(wmask[None], sw.reshape(GH, tq, ws), NEG)
    pw = jnp.exp((sw3 - jnp.max(sw3, axis=-1, keepdims=True)).astype(BF16)).reshape(M, ws)
    o_win = normalise(pv_with_sums(pw, vwn).reshape(GH, tq, LANES))

    g0mask = lane < NSA_DK
    for h in range(H):
        o_g = []
        for g in range(G):
            r = g * H + h
            c0 = r * 3
            o_g.append(gates[:, c0:c0 + 1] * o_cmp[r] + gates[:, c0 + 1:c0 + 2] * o_sel[r] +
                       gates[:, c0 + 2:c0 + 3] * o_win[r])
        y_ref[:, h * LANES:(h + 1) * LANES] = jnp.where(g0mask, o_g[0], o_g[1]).astype(y_ref.dtype)


def _nsa(q, kcvc, ksel, vsel, kwin, vwin, gates, B, S, tq, ck):
    T = B * S
    nq = S // tq
    ovt, e = _nsa_consts(S)
    seq = lambda a: a.reshape(B, S, LANES)
    kv_spec = pl.BlockSpec((None, S, LANES), lambda b, i: (b, 0, 0))
    kern = functools.partial(_nsa_kernel, tq=tq, ck=ck, n_sel=S // NSA_SEL_BLOCK)
    return pl.pallas_call(
        kern, grid=(B, nq),
        in_specs=[pl.BlockSpec((tq, 512), lambda b, i: (b * nq + i, 0)),
                  pl.BlockSpec((None,) + kcvc.shape[1:], lambda b, i: (b, 0, 0)),
                  kv_spec, kv_spec, kv_spec, kv_spec,
                  pl.BlockSpec((tq, LANES), lambda b, i: (b * nq + i, 0)),
                  pl.BlockSpec(ovt.shape, lambda b, i: (0, 0)),
                  pl.BlockSpec(e.shape, lambda b, i: (0, 0))],
        out_specs=pl.BlockSpec((tq, 512), lambda b, i: (b * nq + i, 0)),
        out_shape=jax.ShapeDtypeStruct((T, 512), BF16),
        scratch_shapes=[pltpu.VMEM((NSA_GROUPS, tq, S), F32)],
        compiler_params=_params(("parallel", "parallel")), name="nsa",
    )(q, kcvc, seq(ksel), seq(vsel), seq(kwin), seq(vwin), gates, ovt, e)


def _mlstm_kernel(q_ref, k_ref, v_ref, o_ref, gn_ref, gt_ref, cw_ref, bn_ref, bt_ref, ng_ref, tri_ref,
                  y_ref, c_scr, n_scr):
    H, dh, L = ML_HEADS, ML_DH, ML_CHUNK
    nb, S = q_ref.shape[0], q_ref.shape[1]
    nchunk = S // L
    c_scr[...] = jnp.zeros_like(c_scr)
    n_scr[...] = jnp.zeros_like(n_scr)
    row = lax.broadcasted_iota(I32, (L, H * dh), 0)
    li = lax.broadcasted_iota(I32, (L, L), 0)
    mi = lax.broadcasted_iota(I32, (L, L), 1)
    causal = mi <= li
    tril = tri_ref[0]
    triu = tri_ref[1]
    hp = lax.Precision.HIGHEST

    def conv_silu(ref, bi, c, wofs):
        r0 = pl.multiple_of(c * L, L)
        rp = pl.multiple_of(jnp.maximum(c - 1, 0) * L, L)
        cur = ref[bi, pl.ds(r0, L), :].astype(F32)
        prev = ref[bi, pl.ds(rp, L), :].astype(F32) * jnp.where(c > 0, 1.0, 0.0)
        acc = cur * cw_ref[ML_CONV - 1:ML_CONV, wofs:wofs + H * dh]
        for j in range(1, ML_CONV):
            sh = jnp.where(row < j, pltpu.roll(prev, j, 0), pltpu.roll(cur, j, 0))
            acc = acc + sh * cw_ref[ML_CONV - 1 - j:ML_CONV - j, wofs:wofs + H * dh]
        return acc * jax.nn.sigmoid(acc)

    def body(c, m_state):
        r0 = pl.multiple_of(c * L, L)
        new_m = []
        for bi in range(nb):
            qa = conv_silu(q_ref, bi, c, 0) * (dh ** -0.5)
            ka = conv_silu(k_ref, bi, c, H * dh)
            va = v_ref[bi, pl.ds(r0, L), :]
            oa = o_ref[bi, pl.ds(r0, L), :].astype(F32)
            gn = gn_ref[bi, pl.ds(r0, L), :] + bn_ref[...]
            gt = gt_ref[bi, :, c, :] + bt_ref[...]
            lf_n = jax.nn.log_sigmoid(gn)
            lf_t = jax.nn.log_sigmoid(gt)
            b_n = jnp.dot(tril, lf_n, precision=hp, preferred_element_type=F32)
            b_t = jnp.dot(lf_t, triu, precision=hp, preferred_element_type=F32)
            for h in range(H):
                st = bi * H + h
                q = qa[:, h * dh:(h + 1) * dh]
                k = ka[:, h * dh:(h + 1) * dh]
                v = va[:, h * dh:(h + 1) * dh]
                m_old = m_state[st]
                b_col = b_n[:, H + h:H + h + 1]
                i_col = gn[:, h:h + 1]
                b_row = b_t[H + h:H + h + 1, :]
                i_row = gt[h:h + 1, :]
                g_tot = b_t[H + h:H + h + 1, L - 1:L]
                d_log = jnp.where(causal, b_col - b_row + i_row, NEG)
                inter = b_col + m_old
                m_q = jnp.maximum(inter, jnp.max(d_log, axis=-1, keepdims=True))
                w_intra = jnp.exp(d_log - m_q)
                w_inter = jnp.exp(inter - m_q)
                qb = q.astype(BF16)
                s = lax.dot_general(qb, k.astype(BF16), _DN_T, preferred_element_type=F32) * w_intra
                cst = c_scr[st]
                nst = n_scr[st]
                num = (w_inter * jnp.dot(qb, cst.astype(BF16), preferred_element_type=F32) +
                       jnp.dot(s.astype(BF16), v, preferred_element_type=F32))
                den = w_inter * jnp.sum(q * nst, axis=-1, keepdims=True) + jnp.sum(s, axis=-1, keepdims=True)
                hv = num / jnp.maximum(jnp.abs(den), jnp.exp(-m_q))
                log_k = g_tot - b_col + i_col
                m_new = jnp.maximum(g_tot + m_old, jnp.max(log_k, axis=0, keepdims=True))
                wk = jnp.exp(log_k - m_new)
                decay = jnp.exp(g_tot + m_old - m_new)
                kw = k * wk
                c_scr[st] = decay * cst + lax.dot_general(kw.astype(BF16), v, _DN_TA, preferred_element_type=F32)
                n_scr[st] = decay * nst + jnp.sum(kw, axis=0, keepdims=True)
                new_m.append(m_new)
                mu = jnp.mean(hv, axis=-1, keepdims=True)
                dv = hv - mu
                var = jnp.mean(dv * dv, axis=-1, keepdims=True)
                hn = dv * lax.rsqrt(var + LN_EPS) * ng_ref[:, h * dh:(h + 1) * dh]
                og = jax.nn.sigmoid(oa[:, h * dh:(h + 1) * dh])
                y_ref[bi, pl.ds(r0, L), h * dh:(h + 1) * dh] = (og * hn).astype(y_ref.dtype)
        return tuple(new_m)

    lax.fori_loop(0, nchunk, body, tuple(jnp.zeros((1, 1), F32) for _ in range(nb * H)))


def _mlstm(mq, mk, mv, mo, mif, conv_w, i_bias, f_bias, norm_g, B, S, nb):
    T = B * S
    H, dh, L = ML_HEADS, ML_DH, ML_CHUNK
    W = H * dh
    gt = mif[:, :2 * H].reshape(B, S, 2 * H).transpose(0, 2, 1).reshape(B, 2 * H, S // L, L)
    cw = conv_w.reshape(ML_CONV, 2 * W)
    bias = jnp.concatenate([i_bias, f_bias])
    bn = jnp.pad(bias, (0, LANES - 2 * H)).reshape(1, LANES)
    bt = bias.reshape(2 * H, 1)
    ng = norm_g.reshape(1, W)
    tri = jnp.stack([jnp.tril(jnp.ones((L, L), F32)), jnp.triu(jnp.ones((L, L), F32))])
    seq = lambda a: a.reshape(B, S, a.shape[1])
    rows = lambda w: pl.BlockSpec((nb, S, w), lambda b: (b, 0, 0))
    full = lambda a: pl.BlockSpec(a.shape, lambda b: (0,) * a.ndim)
    y = pl.pallas_call(
        _mlstm_kernel, grid=(B // nb,),
        in_specs=[rows(W), rows(W), rows(W), rows(W), rows(LANES),
                  pl.BlockSpec((nb, 2 * H, S // L, L), lambda b: (b, 0, 0, 0)),
                  full(cw), full(bn), full(bt), full(ng), full(tri)],
        out_specs=rows(W),
        out_shape=jax.ShapeDtypeStruct((B, S, W), BF16),
        scratch_shapes=[pltpu.VMEM((nb * H, dh, dh), F32), pltpu.VMEM((nb * H, 1, dh), F32)],
        compiler_params=_params(("parallel",)), name="mlstm",
    )(seq(mq), seq(mk), seq(mv), seq(mo), seq(mif), gt, cw, bn, bt, ng, tri)
    return y.reshape(T, W)


def _outproj_kernel(yn_ref, ym_ref, x_ref, w_ref, g_ref, b_ref, o_ref):
    mix = (jnp.dot(yn_ref[...], w_ref[0:512, :], preferred_element_type=F32) +
           jnp.dot(ym_ref[...], w_ref[512:1024, :], preferred_element_type=F32))
    o_ref[...] = _ln_rows(ALPHA * x_ref[...] + mix, g_ref[...], b_ref[...])


def _outproj(y_nsa, y_ml, x2d, w_out, g, b, tm):
    T = x2d.shape[0]
    wn = w_out[:512].reshape(NSA_GROUPS, NSA_HPG, NSA_DK, D_MODEL).transpose(1, 0, 2, 3).reshape(512, D_MODEL)
    w = jnp.concatenate([wn, w_out[512:]], axis=0).astype(BF16)
    row = lambda wd: pl.BlockSpec((tm, wd), lambda i: (i, 0))
    full = lambda a: pl.BlockSpec(a.shape, lambda i: (0,) * a.ndim)
    g2, b2 = g.reshape(1, -1), b.reshape(1, -1)
    return pl.pallas_call(
        _outproj_kernel, grid=(T // tm,),
        in_specs=[row(512), row(512), row(D_MODEL), full(w), full(g2), full(b2)],
        out_specs=row(D_MODEL), out_shape=jax.ShapeDtypeStruct((T, D_MODEL), F32),
        compiler_params=_params(("parallel",)), name="outproj")(y_nsa, y_ml, x2d, w, g2, b2)


def _memkv_kernel(m_ref, w_ref, o_ref):
    o_ref[...] = jnp.dot(m_ref[...].astype(BF16), w_ref[...], preferred_element_type=F32).astype(o_ref.dtype)


def _memkv(mem2d, wk, wv):
    w = jnp.concatenate([wk, wv], axis=1).astype(BF16)
    R = mem2d.shape[0]
    return pl.pallas_call(
        _memkv_kernel, grid=(R // MEM_LEN,),
        in_specs=[pl.BlockSpec((MEM_LEN, D_MODEL), lambda i: (i, 0)),
                  pl.BlockSpec(w.shape, lambda i: (0, 0))],
        out_specs=pl.BlockSpec((MEM_LEN, 2 * D_MODEL), lambda i: (i, 0)),
        out_shape=jax.ShapeDtypeStruct((R, 2 * D_MODEL), BF16),
        compiler_params=_params(("parallel",)), name="memkv")(mem2d, w)


def _xattn_kernel(x_ref, kv_ref, wq_ref, wo_ref, g_ref, b_ref, rw_ref, x2_ref, x2p_ref, sc_ref):
    x1 = x_ref[...]
    q = jnp.dot(x1.astype(BF16), wq_ref[...], preferred_element_type=F32).astype(BF16)
    outs = []
    for h in range(XA_HEADS):
        qh = q[:, h * XA_DH:(h + 1) * XA_DH]
        kh = kv_ref[:, h * XA_DH:(h + 1) * XA_DH]
        vh = kv_ref[:, D_MODEL + h * XA_DH:D_MODEL + (h + 1) * XA_DH]
        s = lax.dot_general(qh, kh, _DN_T, preferred_element_type=F32) * (XA_DH ** -0.5)
        p = jnp.exp(s - jnp.max(s, axis=-1, keepdims=True))
        p = p / jnp.sum(p, axis=-1, keepdims=True)
        outs.append(jnp.dot(p.astype(BF16), vh, preferred_element_type=F32).astype(BF16))
    o = jnp.concatenate(outs, axis=1)
    xa = jnp.dot(o, wo_ref[...], preferred_element_type=F32)
    x2 = _ln_rows(ALPHA * x1 + xa, g_ref[...], b_ref[...])
    x2_ref[...] = x2
    x2p_ref[...] = _pack_bf16_pairs(x2)
    xh = x2.astype(BF16)
    xl = (x2 - xh.astype(F32)).astype(BF16)
    wh = rw_ref[0]
    wl = rw_ref[1]
    logit = (lax.dot_general(wh, xh, _DN_T, preferred_element_type=F32) +
             lax.dot_general(wh, xl, _DN_T, preferred_element_type=F32) +
             lax.dot_general(wl, xh, _DN_T, preferred_element_type=F32))
    sc_ref[...] = jax.nn.sigmoid(logit)


def _xattn(x1, kv, wq, wo, g, b, router_w, S, tq):
    T = x1.shape[0]
    wqb, wob = wq.astype(BF16), wo.astype(BF16)
    rwt = router_w.T
    rh = rwt.astype(BF16)
    rw = jnp.stack([rh, (rwt - rh.astype(F32)).astype(BF16)])
    g2, b2 = g.reshape(1, -1), b.reshape(1, -1)
    full = lambda a: pl.BlockSpec(a.shape, lambda i: (0,) * a.ndim)
    per = S // tq
    return pl.pallas_call(
        _xattn_kernel, grid=(T // tq,),
        in_specs=[pl.BlockSpec((tq, D_MODEL), lambda i: (i, 0)),
                  pl.BlockSpec((MEM_LEN, 2 * D_MODEL), lambda i: (i // per, 0)),
                  full(wqb), full(wob), full(g2), full(b2), full(rw)],
        out_specs=(pl.BlockSpec((tq, D_MODEL), lambda i: (i, 0)),
                   pl.BlockSpec((tq, D_MODEL // 2), lambda i: (i, 0)),
                   pl.BlockSpec((MOE_E, tq), lambda i: (0, i))),
        out_shape=(jax.ShapeDtypeStruct((T, D_MODEL), F32), jax.ShapeDtypeStruct((T, D_MODEL // 2), I32),
                   jax.ShapeDtypeStruct((MOE_E, T), F32)),
        compiler_params=_params(("parallel",)), name="xattn")(x1, kv, wqb, wob, g2, b2, rw)


def _route_kernel(sc_ref, rb_ref, idx_ref, w_ref):
    E, G = MOE_E, MOE_GROUPS
    per = E // G
    scores = sc_ref[...]
    tr = scores.shape[1]
    biased = scores + rb_ref[...]
    g3 = biased.reshape(G, per, tr)
    j3 = lax.broadcasted_iota(I32, (G, per, tr), 1)
    m1 = jnp.max(g3, axis=1, keepdims=True)
    first = jnp.min(jnp.where(g3 == m1, j3, per), axis=1, keepdims=True)
    m2 = jnp.max(jnp.where(j3 == first, -jnp.inf, g3), axis=1, keepdims=True)
    gs = (m1 + m2).reshape(G, tr)
    gi = lax.broadcasted_iota(I32, (G, tr), 0)
    cnt = jnp.zeros((G, tr), F32)
    for m in range(G):
        row = gs[m:m + 1, :]
        cnt = cnt + jnp.where((row > gs) | ((row == gs) & (gi > m)), 1.0, 0.0)
    gmask = cnt < float(MOE_TOPK_GROUPS)
    masked = jnp.where(gmask[:, None, :], g3, NEG).reshape(E, tr)
    ei = lax.broadcasted_iota(I32, (E, tr), 0)
    idxs, ws = [], []
    for _ in range(MOE_K):
        mx = jnp.max(masked, axis=0, keepdims=True)
        ix = jnp.min(jnp.where(masked == mx, ei, E), axis=0, keepdims=True)
        hit = ei == ix
        ws.append(jnp.sum(jnp.where(hit, scores, 0.0), axis=0, keepdims=True))
        idxs.append(ix)
        masked = jnp.where(hit, -jnp.inf, masked)
    w = jnp.concatenate(ws, axis=0)
    idx_ref[...] = jnp.concatenate(idxs, axis=0)
    w_ref[...] = w / jnp.sum(w, axis=0, keepdims=True) * MOE_ROUTE_SCALE


def _route(scores_t, router_bias, tr):
    E, T = scores_t.shape
    rb = router_bias.reshape(E, 1)
    return pl.pallas_call(
        _route_kernel, grid=(T // tr,),
        in_specs=[pl.BlockSpec((E, tr), lambda i: (0, i)), pl.BlockSpec((E, 1), lambda i: (0, 0))],
        out_specs=(pl.BlockSpec((MOE_K, tr), lambda i: (0, i)), pl.BlockSpec((MOE_K, tr), lambda i: (0, i))),
        out_shape=(jax.ShapeDtypeStruct((MOE_K, T), I32), jax.ShapeDtypeStruct((MOE_K, T), F32)),
        compiler_params=_params(("parallel",)), name="route")(scores_t, rb)


def _rank_kernel(idx_ref, u_ref, rank_ref, cnt_ref, carry):
    E = MOE_E

    @pl.when(pl.program_id(0) == 0)
    def _():
        carry[...] = jnp.zeros_like(carry)

    idx = idx_ref[...]
    tp = idx.shape[1]
    ei = lax.broadcasted_iota(I32, (E, tp), 0)
    hits = [ei == idx[k:k + 1, :] for k in range(MOE_K)]
    onehot = jnp.zeros((E, tp), F32)
    for hit in hits:
        onehot = onehot + jnp.where(hit, 1.0, 0.0)
    pos = jnp.dot(onehot.astype(BF16), u_ref[...], preferred_element_type=F32) + carry[...]
    ranks = [jnp.sum(jnp.where(hit, pos, 0.0), axis=0, keepdims=True) for hit in hits]
    rank_ref[...] = jnp.concatenate(ranks, axis=0).astype(I32)
    total = carry[...] + jnp.sum(onehot, axis=1, keepdims=True)
    carry[...] = total
    cnt_ref[...] = jnp.broadcast_to(total, cnt_ref.shape).astype(I32)


def _rank(idx_t, tp):
    K, T = idx_t.shape
    u = jnp.triu(jnp.ones((tp, tp), F32), k=1).astype(BF16)
    rank, cnt = pl.pallas_call(
        _rank_kernel, grid=(T // tp,),
        in_specs=[pl.BlockSpec((K, tp), lambda i: (0, i)), pl.BlockSpec((tp, tp), lambda i: (0, 0))],
        out_specs=(pl.BlockSpec((K, tp), lambda i: (0, i)), pl.BlockSpec((MOE_E, LANES), lambda i: (0, 0))),
        out_shape=(jax.ShapeDtypeStruct((K, T), I32), jax.ShapeDtypeStruct((MOE_E, LANES), I32)),
        scratch_shapes=[pltpu.VMEM((MOE_E, 1), F32)],
        compiler_params=_params(("arbitrary",)), name="rank")(idx_t, u)
    return rank, cnt[:, 0]


def _dest_kernel(idx_ref, rank_ref, po_ref, dest_ref):
    idx = idx_ref[...]
    tp = idx.shape[1]
    ei = lax.broadcasted_iota(I32, (MOE_E, tp), 0)
    po = po_ref[...]
    base = [jnp.sum(jnp.where(ei == idx[k:k + 1, :], po, 0.0), axis=0, keepdims=True) for k in range(MOE_K)]
    dest_ref[...] = jnp.concatenate(base, axis=0).astype(I32) + rank_ref[...]


def _dest(idx_t, rank_t, poffs, tp):
    K, T = idx_t.shape
    po = poffs.astype(F32).reshape(MOE_E, 1)
    spec = pl.BlockSpec((K, tp), lambda i: (0, i))
    return pl.pallas_call(
        _dest_kernel, grid=(T // tp,),
        in_specs=[spec, spec, pl.BlockSpec((MOE_E, 1), lambda i: (0, 0))],
        out_specs=spec, out_shape=jax.ShapeDtypeStruct((K, T), I32),
        compiler_params=_params(("parallel",)), name="dest")(idx_t, rank_t, po)


def _pack_bf16_pairs(v):
    m = v.shape[1] // 2
    bits = lax.bitcast_convert_type(v.astype(BF16).astype(F32), jnp.uint32)
    return lax.bitcast_convert_type((bits[:, :m] >> 16) | (bits[:, m:] & jnp.uint32(0xFFFF0000)), I32)


def _unpack_bf16_pairs(w):
    w = lax.bitcast_convert_type(w, jnp.uint32)
    lo = lax.bitcast_convert_type(w << 16, F32)
    hi = lax.bitcast_convert_type(w & jnp.uint32(0xFFFF0000), F32)
    return lo, hi


def _row_scatter(rows, dest_t, n_rows):
    T, d = rows.shape
    K = dest_t.shape[0]
    nw = SC_CORES * SC_SUBCORES
    per_w = T // nw
    ch = SC_GATHER_ROWS
    mesh = plsc.VectorSubcoreMesh(core_axis_name="c", subcore_axis_name="s")

    @functools.partial(
        pl.kernel, mesh=mesh, out_type=jax.ShapeDtypeStruct((n_rows, d), I32),
        scratch_types=[pltpu.VMEM((K, ch), I32), pltpu.VMEM((ch, d), I32), pltpu.SemaphoreType.DMA],
        name="row_scatter")
    def scatter(rows_hbm, dest_hbm, out_hbm, idx_v, rows_v, sem):
        wid = lax.axis_index("s") * SC_CORES + lax.axis_index("c")
        base = wid * per_w

        @pl.loop(0, per_w // ch)
        def _(i):
            off = pl.multiple_of(base + i * ch, ch)
            pltpu.sync_copy(rows_hbm.at[pl.ds(off, ch)], rows_v)
            pltpu.sync_copy(dest_hbm.at[:, pl.ds(off, ch)], idx_v)
            copies = [pltpu.async_copy(rows_v, out_hbm.at[idx_v.at[k]], sem) for k in range(K)]
            for cp in copies:
                cp.wait()

    return scatter(rows, dest_t)


RING_AHEAD = 3
RING_OUT = 3


def _expert_kernel(po_ref, cnt_ref, tot_ref, xs_hbm, w1_ref, w3_ref, w2_ref, ys_hbm,
                   xbuf, ybuf, w1b, w3b, w2b, insem, outsem):
    e = pl.program_id(0)
    n = cnt_ref[e]
    rb = xbuf.shape[1]
    ns = xbuf.shape[0]
    hw = D_MODEL // 2
    nblk = jnp.maximum((n + rb - 1) // rb, 1)
    g0 = po_ref[e] // rb
    total = tot_ref[0]
    w1b[...] = w1_ref[...].astype(BF16)
    w3b[...] = w3_ref[...].astype(BF16)
    w2b[...] = w2_ref[...].astype(BF16)

    def in_copy(g, slot):
        return pltpu.make_async_copy(xs_hbm.at[pl.ds(pl.multiple_of(g * rb, rb), rb)], xbuf.at[slot], insem.at[slot])

    def out_copy(g, slot):
        return pltpu.make_async_copy(ybuf.at[slot], ys_hbm.at[pl.ds(pl.multiple_of(g * rb, rb), rb)],
                                     outsem.at[slot])

    @pl.when(e == 0)
    def _():
        for d in range(RING_AHEAD):
            @pl.when(d < total)
            def _():
                in_copy(d, d).start()

    def body(j, c):
        g = g0 + j
        slot = g % ns
        oslot = g % RING_OUT

        @pl.when(g + RING_AHEAD < total)
        def _():
            in_copy(g + RING_AHEAD, (g + RING_AHEAD) % ns).start()

        in_copy(0, slot).wait()

        @pl.when(g >= RING_OUT)
        def _():
            out_copy(0, oslot).wait()

        words = xbuf[slot]
        row = j * rb + lax.broadcasted_iota(I32, words.shape, 0)
        lo, hi = _unpack_bf16_pairs(jnp.where(row < n, words, 0))
        lo, hi = lo.astype(BF16), hi.astype(BF16)
        a = (jnp.dot(lo, w1b[0:hw, :], preferred_element_type=F32) +
             jnp.dot(hi, w1b[hw:, :], preferred_element_type=F32))
        u = (jnp.dot(lo, w3b[0:hw, :], preferred_element_type=F32) +
             jnp.dot(hi, w3b[hw:, :], preferred_element_type=F32))
        h = (a * jax.nn.sigmoid(a) * u).astype(BF16)
        ybuf[oslot] = _pack_bf16_pairs(jnp.dot(h, w2b[...], preferred_element_type=F32))
        out_copy(g, oslot).start()
        return c

    lax.fori_loop(0, nblk, body, 0)

    @pl.when(e + 1 == pl.num_programs(0))
    def _():
        for i in range(RING_OUT):
            @pl.when(total > i)
            def _():
                out_copy(0, (total - 1 - i) % RING_OUT).wait()


def _experts(poffs, counts, total_blocks, xs, w1, w3, w2):
    n_rows, wp = xs.shape
    rb = ROW_BLOCK
    wspec = lambda shape: pl.BlockSpec((None,) + shape, lambda e, po, cn, tb: (e, 0, 0))
    gs = pltpu.PrefetchScalarGridSpec(
        num_scalar_prefetch=3, grid=(MOE_E,),
        in_specs=[pl.BlockSpec(memory_space=pl.ANY), wspec((D_MODEL, MOE_FF)), wspec((D_MODEL, MOE_FF)),
                  wspec((MOE_FF, D_MODEL))],
        out_specs=pl.BlockSpec(memory_space=pl.ANY),
        scratch_shapes=[pltpu.VMEM((RING_AHEAD + 1, rb, wp), I32), pltpu.VMEM((RING_OUT, rb, wp), I32),
                        pltpu.VMEM((D_MODEL, MOE_FF), BF16), pltpu.VMEM((D_MODEL, MOE_FF), BF16),
                        pltpu.VMEM((MOE_FF, D_MODEL), BF16),
                        pltpu.SemaphoreType.DMA((RING_AHEAD + 1,)), pltpu.SemaphoreType.DMA((RING_OUT,))])
    return pl.pallas_call(
        _expert_kernel, grid_spec=gs, out_shape=jax.ShapeDtypeStruct((n_rows, wp), I32),
        compiler_params=_params(("arbitrary",)), name="experts")(poffs, counts, total_blocks, xs, w1, w3, w2)


def _row_gather(table, idx):
    n, d = idx.shape[0], table.shape[1]
    nw = SC_CORES * SC_SUBCORES
    per_w = n // nw
    ch = SC_GATHER_ROWS // 2
    mesh = plsc.VectorSubcoreMesh(core_axis_name="c", subcore_axis_name="s")

    @functools.partial(
        pl.kernel, mesh=mesh, out_type=jax.ShapeDtypeStruct((n, d), I32),
        scratch_types=[pltpu.VMEM((ch,), I32), pltpu.VMEM((ch,), I32),
                       pltpu.VMEM((ch, d), I32), pltpu.VMEM((ch, d), I32),
                       pltpu.SemaphoreType.DMA, pltpu.SemaphoreType.DMA,
                       pltpu.SemaphoreType.DMA, pltpu.SemaphoreType.DMA],
        name="row_gather")
    def gather(table_hbm, idx_hbm, out_hbm, idx0, idx1, rows0, rows1, g0, g1, w0, w1):
        wid = lax.axis_index("s") * SC_CORES + lax.axis_index("c")
        base = wid * per_w

        @pl.loop(0, per_w // (2 * ch))
        def _(i):
            off0 = pl.multiple_of(base + 2 * i * ch, ch)
            off1 = pl.multiple_of(off0 + ch, ch)
            pltpu.sync_copy(idx_hbm.at[pl.ds(off0, ch)], idx0)
            c0 = pltpu.async_copy(table_hbm.at[idx0], rows0, g0)
            pltpu.sync_copy(idx_hbm.at[pl.ds(off1, ch)], idx1)
            c1 = pltpu.async_copy(table_hbm.at[idx1], rows1, g1)
            c0.wait()
            o0 = pltpu.async_copy(rows0, out_hbm.at[pl.ds(off0, ch)], w0)
            c1.wait()
            o1 = pltpu.async_copy(rows1, out_hbm.at[pl.ds(off1, ch)], w1)
            o0.wait()
            o1.wait()

    return gather(table, idx)


def _combine_kernel(w_ref, x_ref, ysg_ref, s1_ref, s3_ref, s2_ref, g_ref, b_ref, o_ref):
    x2 = x_ref[...]
    xb = x2.astype(BF16)
    a = jnp.dot(xb, s1_ref[...], preferred_element_type=F32)
    c = jnp.dot(xb, s3_ref[...], preferred_element_type=F32)
    shared = jnp.dot((a * jax.nn.sigmoid(a) * c).astype(BF16), s2_ref[...], preferred_element_type=F32)
    w = w_ref[...]
    hw = ysg_ref.shape[2]
    y_lo = jnp.zeros((x2.shape[0], hw), F32)
    y_hi = jnp.zeros((x2.shape[0], hw), F32)
    for k in range(MOE_K):
        lo, hi = _unpack_bf16_pairs(ysg_ref[k])
        y_lo = y_lo + w[:, k:k + 1] * lo
        y_hi = y_hi + w[:, k:k + 1] * hi
    y = shared + jnp.concatenate([y_lo, y_hi], axis=1)
    o_ref[...] = _ln_rows(ALPHA * x2 + y, g_ref[...], b_ref[...])


def _combine(w_nat, x2, ysg, sw1, sw3, sw2, g, b, tc):
    T = x2.shape[0]
    wp = ysg.shape[2]
    full = lambda a: pl.BlockSpec(a.shape, lambda i: (0,) * a.ndim)
    s1, s3, s2 = sw1.astype(BF16), sw3.astype(BF16), sw2.astype(BF16)
    g2, b2 = g.reshape(1, -1), b.reshape(1, -1)
    return pl.pallas_call(
        _combine_kernel, grid=(T // tc,),
        in_specs=[pl.BlockSpec((tc, MOE_K), lambda i: (i, 0)),
                  pl.BlockSpec((tc, D_MODEL), lambda i: (i, 0)),
                  pl.BlockSpec((MOE_K, tc, wp), lambda i: (0, i, 0)),
                  full(s1), full(s3), full(s2), full(g2), full(b2)],
        out_specs=pl.BlockSpec((tc, D_MODEL), lambda i: (i, 0)),
        out_shape=jax.ShapeDtypeStruct((T, D_MODEL), F32),
        compiler_params=_params(("parallel",)), name="combine",
    )(w_nat, x2, ysg, s1, s3, s2, g2, b2)


def _moe(x2, x2p, scores_t, router_bias, w1, w3, w2, sw1, sw3, sw2, g, b, tiles):
    T = x2.shape[0]
    rb = ROW_BLOCK
    idx_t, w_t = _route(scores_t, router_bias, tiles['route'])
    rank_t, counts = _rank(idx_t, tiles['rank'])
    pcounts = jnp.maximum((counts + rb - 1) // rb, 1) * rb
    pends = jnp.cumsum(pcounts)
    poffs = (pends - pcounts).astype(I32)
    n_blocks = -(-T * MOE_K // rb) + MOE_E
    dest_t = _dest(idx_t, rank_t, poffs, tiles['rank'])
    xs = _row_scatter(x2p, dest_t, n_blocks * rb)
    total_blocks = (pends[-1:] // rb).astype(I32)
    ys = _experts(poffs, counts.astype(I32), total_blocks, xs, w1, w3, w2)
    ysg = _row_gather(ys, dest_t.reshape(-1)).reshape(MOE_K, T, -1)
    return _combine(w_t.T, x2, ysg, sw1, sw3, sw2, g, b, tiles['combine'])


def _tiles(B, S):
    T = B * S
    pick = lambda want, n: want if n % want == 0 else n
    return dict(proj=pick(512, T), nsa_q=pick(128, S), nsa_ck=pick(512, S), outproj=pick(1024, T),
                mlstm_nb=2 if B % 2 == 0 else 1, xattn=pick(1024, S), route=pick(1024, T), rank=pick(512, T),
                combine=pick(512, T))


def kernel(x, mem, w_in, nsa_pos_k, nsa_cmp_k_w1, nsa_cmp_k_w2, nsa_pos_v, nsa_cmp_v_w1, nsa_cmp_v_w2,
           mlstm_conv_w, mlstm_i_bias, mlstm_f_bias, mlstm_norm_g, w_out, ln1_g, ln1_b,
           xa_wq, xa_wk, xa_wv, xa_wo, ln2_g, ln2_b, router_w, router_bias,
           moe_w1, moe_w3, moe_w2, shared_w1, shared_w3, shared_w2, ln3_g, ln3_b):
    B, S, D = x.shape
    T = B * S
    tl = _tiles(B, S)
    xc = x.reshape(T, D)
    memc = mem.reshape(B * MEM_LEN, D)
    for l in range(w_in.shape[0]):
        (q, cmp, ksel, vsel, kwin, vwin, gates, mq, mk, mv, mo, mif) = _project(xc, _prep_w_in(w_in[l]), tl['proj'])
        kcvc = _compress(cmp, B, S, _prep_cmp(nsa_pos_k[l], nsa_cmp_k_w1[l], nsa_cmp_k_w2[l],
                                              nsa_pos_v[l], nsa_cmp_v_w1[l], nsa_cmp_v_w2[l]))
        y_nsa = _nsa(q, kcvc, ksel, vsel, kwin, vwin, gates, B, S, tl['nsa_q'], tl['nsa_ck'])
        y_ml = _mlstm(mq, mk, mv, mo, mif, mlstm_conv_w[l], mlstm_i_bias[l], mlstm_f_bias[l],
                      mlstm_norm_g[l], B, S, tl['mlstm_nb'])
        x1 = _outproj(y_nsa, y_ml, xc, w_out[l], ln1_g[l], ln1_b[l], tl['outproj'])
        kv = _memkv(memc, xa_wk[l], xa_wv[l])
        x2, x2p, scores_t = _xattn(x1, kv, xa_wq[l], xa_wo[l], ln2_g[l], ln2_b[l], router_w[l], S, tl['xattn'])
        xc = _moe(x2, x2p, scores_t, router_bias[l], moe_w1[l], moe_w3[l], moe_w2[l],
                  shared_w1[l], shared_w3[l], shared_w2[l], ln3_g[l], ln3_b[l], tl)
    return xc.reshape(B, S, D)
```

```python
import functools
import numpy as np
import jax
import jax.numpy as jnp
from jax import lax
from jax.experimental import pallas as pl
from jax.experimental.pallas import tpu as pltpu
from jax.experimental.pallas import tpu_sc as plsc

F32 = jnp.float32
BF16 = jnp.bfloat16
I32 = jnp.int32

D_MODEL = 1024
MEM_LEN = 256
NSA_HEADS = 8
NSA_GROUPS = 2
NSA_HPG = 4
NSA_DK = 64
NSA_CMP_LEN = 32
NSA_CMP_STRIDE = 16
NSA_SEL_BLOCK = 64
NSA_SEL_TOPN = 8
NSA_WINDOW = 512
ML_HEADS = 4
ML_DH = 128
ML_CHUNK = 64
ML_CONV = 4
XA_HEADS = 4
XA_DH = 256
MOE_E = 256
MOE_K = 8
MOE_GROUPS = 8
MOE_TOPK_GROUPS = 4
MOE_FF = 256
MOE_ROUTE_SCALE = 2.5
DEPTH = 1
ALPHA = (2.0 * DEPTH) ** 0.25
LN_EPS = 1e-5
NEG = -1e30
FORCE_BONUS = 1e4

LANES = 128
ROW_BLOCK = 256
VMEM_LIMIT = 56 * 1024 * 1024
SC_CORES = 2
SC_SUBCORES = 16
SC_GATHER_ROWS = 128

_DN_T = (((1,), (1,)), ((), ()))
_DN_TA = (((0,), (0,)), ((), ()))


def _params(sem):
    return pltpu.CompilerParams(dimension_semantics=sem, vmem_limit_bytes=VMEM_LIMIT)


def _ln_rows(v, g, b):
    mu = jnp.mean(v, axis=-1, keepdims=True)
    d = v - mu
    var = jnp.mean(d * d, axis=-1, keepdims=True)
    return d * lax.rsqrt(var + LN_EPS) * g + b


_SEGS = (('q', 512, BF16), ('cmp', 256, F32), ('ksel', 128, BF16), ('vsel', 128, BF16),
         ('kwin', 128, BF16), ('vwin', 128, BF16), ('gates', 128, F32), ('mq', 512, BF16),
         ('mk', 512, BF16), ('mv', 512, BF16), ('mo', 512, BF16), ('mif', 128, F32))


def _proj_kernel(x_ref, w_ref, *out_refs):
    xb = x_ref[...].astype(BF16)
    off = 0
    for o_ref, (_, wd, _) in zip(out_refs, _SEGS):
        o_ref[...] = jnp.dot(xb, w_ref[:, off:off + wd], preferred_element_type=F32).astype(o_ref.dtype)
        off += wd


def _prep_w_in(w):
    sizes = (512,) + (128,) * 6 + (24,) + (512,) * 4 + (4, 4)
    cuts = np.cumsum(sizes)[:-1].tolist()
    (wq, kc, vc, ks, vs, kw, vw, wg, mq, mk, mv, mo, mi, mf) = jnp.split(w, cuts, axis=1)
    wq = wq.reshape(D_MODEL, NSA_GROUPS, NSA_HPG, NSA_DK).transpose(0, 2, 1, 3).reshape(D_MODEL, 512)
    pad = lambda a: jnp.pad(a, ((0, 0), (0, LANES - a.shape[1])))
    segs = [wq, kc, vc, ks, vs, kw, vw, pad(wg), mq, mk, mv, mo, pad(jnp.concatenate([mi, mf], axis=1))]
    return jnp.concatenate(segs, axis=1).astype(BF16)


def _project(x2d, w_all, tm):
    T = x2d.shape[0]
    n = w_all.shape[1]
    out_shape = tuple(jax.ShapeDtypeStruct((T, wd), dt) for _, wd, dt in _SEGS)
    out_specs = tuple(pl.BlockSpec((tm, wd), lambda i: (i, 0)) for _, wd, _ in _SEGS)
    return pl.pallas_call(
        _proj_kernel, grid=(T // tm,),
        in_specs=[pl.BlockSpec((tm, D_MODEL), lambda i: (i, 0)),
                  pl.BlockSpec((D_MODEL, n), lambda i: (0, 0))],
        out_specs=out_specs, out_shape=out_shape,
        compiler_params=_params(("parallel",)), name="proj")(x2d, w_all)


def _cmp_kernel(r_ref, pa_ref, pb_ref, wa_ref, wb_ref, w2_ref, o_ref):
    r = r_ref[...]
    a = jnp.dot((r + pa_ref[...]).astype(BF16), wa_ref[...], preferred_element_type=F32)
    b = jnp.dot((r + pb_ref[...]).astype(BF16), wb_ref[...], preferred_element_type=F32)
    nr = r.shape[0]
    hid = a + pltpu.roll(b, nr - 1, 0)
    hid = hid * jax.nn.sigmoid(hid)
    out = jnp.dot(hid.astype(BF16), w2_ref[...], preferred_element_type=F32)
    row = lax.broadcasted_iota(I32, out.shape, 0)
    o_ref[...] = jnp.where(row < nr - 1, out, 0.0).astype(o_ref.dtype)


def _prep_cmp(pos_k, w1_k, w2_k, pos_v, w1_v, w2_v):
    eye = jnp.eye(NSA_GROUPS, dtype=F32)

    def expand_w1(w1, half):
        w = w1.reshape(NSA_CMP_LEN, NSA_DK, NSA_DK)[half * 16:(half + 1) * 16]
        return jnp.einsum('jde,gk->jgdke', w, eye).reshape(16, 128, 128)

    def both(fk, fv):
        z = jnp.zeros_like(fk)
        top = jnp.concatenate([fk, z], axis=-1)
        bot = jnp.concatenate([z, fv], axis=-1)
        return jnp.concatenate([top, bot], axis=-2)

    wa = both(expand_w1(w1_k, 0), expand_w1(w1_v, 0)).reshape(16 * 256, 256).astype(BF16)
    wb = both(expand_w1(w1_k, 1), expand_w1(w1_v, 1)).reshape(16 * 256, 256).astype(BF16)
    w2 = both(jnp.kron(eye, w2_k), jnp.kron(eye, w2_v)).astype(BF16)

    def pos_row(half):
        pk = jnp.tile(pos_k[half * 16:(half + 1) * 16], (1, NSA_GROUPS))
        pv = jnp.tile(pos_v[half * 16:(half + 1) * 16], (1, NSA_GROUPS))
        return jnp.concatenate([pk, pv], axis=1).reshape(1, 16 * 256)

    return pos_row(0), pos_row(1), wa, wb, w2


def _compress(cmp2d, B, S, prep):
    pa, pb, wa, wb, w2 = prep
    nr = S // NSA_CMP_STRIDE
    r = cmp2d.reshape(B, nr, NSA_CMP_STRIDE * 256)
    full = lambda a: pl.BlockSpec(a.shape, lambda b: (0,) * a.ndim)
    return pl.pallas_call(
        _cmp_kernel, grid=(B,),
        in_specs=[pl.BlockSpec((None, nr, NSA_CMP_STRIDE * 256), lambda b: (b, 0, 0)),
                  full(pa), full(pb), full(wa), full(wb), full(w2)],
        out_specs=pl.BlockSpec((None, nr, 256), lambda b: (b, 0, 0)),
        out_shape=jax.ShapeDtypeStruct((B, nr, 256), BF16),
        compiler_params=_params(("parallel",)), name="cmp")(r, pa, pb, wa, wb, w2)


def _nsa_consts(S):
    n_cmp = (S - NSA_CMP_LEN) // NSA_CMP_STRIDE + 1
    n_sel = S // NSA_SEL_BLOCK
    cs = np.arange(n_cmp) * NSA_CMP_STRIDE
    ss = np.arange(n_sel) * NSA_SEL_BLOCK
    ov = ((cs[:, None] < ss[None, :] + NSA_SEL_BLOCK) & (cs[:, None] + NSA_CMP_LEN > ss[None, :]))
    ovt = np.zeros((LANES, S // NSA_CMP_STRIDE), np.float32)
    ovt[:n_sel, :n_cmp] = ov.T
    e = np.zeros((LANES, S), np.float32)
    e[np.arange(S) // NSA_SEL_BLOCK, np.arange(S)] = 1.0
    return jnp.asarray(ovt, BF16), jnp.asarray(e, BF16)


def _nsa_kernel(q_ref, kcvc_ref, ksel_ref, vsel_ref, kwin_ref, vwin_ref, gates_ref, ovt_ref, e_ref,
                y_ref, bias_ref, *, tq, ck, n_sel):
    G, H = NSA_GROUPS, NSA_HPG
    GH = G * H
    M = GH * tq
    S = ksel_ref.shape[0]
    W = NSA_WINDOW
    ws = min(W + tq, S)
    t0 = pl.program_id(1) * tq
    gates = jax.nn.sigmoid(gates_ref[...])
    lane = lax.broadcasted_iota(I32, (tq, LANES), 1)
    t_col = t0 + lax.broadcasted_iota(I32, (tq, 1), 0)
    kc = kcvc_ref[:, 0:LANES]
    vc = kcvc_ref[:, LANES:2 * LANES]
    nc = kc.shape[0]
    qs = []
    for g in range(G):
        gmask = (lane // NSA_DK) == g
        for h in range(H):
            qh = q_ref[:, h * LANES:(h + 1) * LANES] * (NSA_DK ** -0.5)
            qs.append(jnp.where(gmask, qh, jnp.zeros_like(qh)))
    Q = jnp.concatenate(qs, axis=0).astype(BF16)

    s = lax.dot_general(Q, kc, _DN_T, preferred_element_type=F32)
    c_idx = lax.broadcasted_iota(I32, (tq, nc), 1)
    cmask = (c_idx * NSA_CMP_STRIDE + NSA_CMP_LEN - 1) <= t_col
    s3 = jnp.where(cmask[None], s.reshape(GH, tq, nc), NEG)
    p = jnp.exp(s3 - jnp.max(s3, axis=-1, keepdims=True))
    p = p / jnp.sum(p, axis=-1, keepdims=True)
    p = jnp.where(cmask[None], p, 0.0)
    o_cmp = jnp.dot(p.reshape(M, nc).astype(BF16), vc, preferred_element_type=F32).reshape(GH, tq, LANES)

    for g in range(G):
        psum = jnp.sum(p[g * H:(g + 1) * H], axis=0)
        hi = psum.astype(BF16)
        lo = (psum - hi.astype(F32)).astype(BF16)
        ovt = ovt_ref[...]
        pslt = (lax.dot_general(ovt, hi, _DN_T, preferred_element_type=F32) +
                lax.dot_general(ovt, lo, _DN_T, preferred_element_type=F32))
        imp_p = pslt[0:n_sel, :]
        n_i = lax.broadcasted_iota(I32, (n_sel, tq), 0)
        cur = (t0 + lax.broadcasted_iota(I32, (n_sel, tq), 1)) // NSA_SEL_BLOCK
        forced = (n_i == 0) | (n_i == cur) | (n_i == cur - 1)
        imp = jnp.where(n_i <= cur, imp_p + jnp.where(forced, FORCE_BONUS, 0.0), NEG)
        cnt = jnp.zeros((n_sel, tq), F32)
        for m in range(n_sel):
            row = imp[m:m + 1, :]
            beats = (row > imp) | ((row == imp) & (n_i > m))
            cnt = cnt + jnp.where(beats, 1.0, 0.0)
        selt = jnp.where(cnt < float(min(NSA_SEL_TOPN, n_sel)), 1.0, 0.0)
        selt = jnp.concatenate([selt, jnp.zeros((LANES - n_sel, tq), F32)], axis=0)
        sel = selt.T.astype(BF16)
        maskf = jnp.dot(sel, e_ref[...], preferred_element_type=F32)
        kpos = lax.broadcasted_iota(I32, (tq, S), 1)
        bias_ref[g] = jnp.where((maskf > 0.5) & (kpos <= t_col), 0.0, NEG)

    vlane = lax.broadcasted_iota(I32, (1, LANES), 1) // NSA_DK

    def pv_with_sums(pb, v):
        outs = []
        for g in range(G):
            vg = jnp.where(vlane == g, v, jnp.ones_like(v))
            outs.append(jnp.dot(pb[g * H * tq:(g + 1) * H * tq], vg, preferred_element_type=F32))
        return jnp.concatenate(outs, axis=0)

    def normalise(acc):
        outs = []
        for r in range(GH):
            c = NSA_DK * (1 - r // H)
            outs.append(acc[r] / acc[r][:, c:c + 1])
        return outs

    def sel_body(j, carry):
        m_i, acc = carry
        ks = pl.multiple_of(j * ck, ck)
        k = ksel_ref[pl.ds(ks, ck), :]
        v = vsel_ref[pl.ds(ks, ck), :]
        sj = lax.dot_general(Q, k, _DN_T, preferred_element_type=F32)
        sj = (sj.reshape(G, H, tq, ck) + bias_ref[:, :, pl.ds(ks, ck)][:, None]).reshape(GH, tq, ck)
        m_new = jnp.maximum(m_i, jnp.max(sj, axis=-1, keepdims=True))
        a = jnp.exp(m_i - m_new)
        pj = jnp.exp((sj - m_new).astype(BF16)).reshape(M, ck)
        return m_new, a * acc + pv_with_sums(pj, v).reshape(GH, tq, LANES)

    init = (jnp.full((GH, tq, 1), NEG, F32), jnp.zeros((GH, tq, LANES), F32))
    _, acc = lax.fori_loop(0, (t0 + tq + ck - 1) // ck, sel_body, init)
    o_sel = normalise(acc)

    kst = pl.multiple_of(jnp.clip(t0 - W, 0, S - ws), LANES)
    kwn = kwin_ref[pl.ds(kst, ws), :]
    vwn = vwin_ref[pl.ds(kst, ws), :]
    sw = lax.dot_general(Q, kwn, _DN_T, preferred_element_type=F32)
    wpos = kst + lax.broadcasted_iota(I32, (tq, ws), 1)
    wmask = (wpos <= t_col) & (wpos > t_col - W)
    sw3 = jnp.where(wmask[None], sw.reshape(GH, tq, ws), NEG)
    pw = jnp.exp((sw3 - jnp.max(sw3, axis=-1, keepdims=True)).astype(BF16)).reshape(M, ws)
    o_win = normalise(pv_with_sums(pw, vwn).reshape(GH, tq, LANES))

    g0mask = lane < NSA_DK
    for h in range(H):
        o_g = []
        for g in range(G):
            r = g * H + h
            c0 = r * 3
            o_g.append(gates[:, c0:c0 + 1] * o_cmp[r] + gates[:, c0 + 1:c0 + 2] * o_sel[r] +
                       gates[:, c0 + 2:c0 + 3] * o_win[r])
        y_ref[:, h * LANES:(h + 1) * LANES] = jnp.where(g0mask, o_g[0], o_g[1]).astype(y_ref.dtype)


def _nsa(q, kcvc, ksel, vsel, kwin, vwin, gates, B, S, tq, ck):
    T = B * S
    nq = S // tq
    ovt, e = _nsa_consts(S)
    seq = lambda a: a.reshape(B, S, LANES)
    kv_spec = pl.BlockSpec((None, S, LANES), lambda b, i: (b, 0, 0))
    kern = functools.partial(_nsa_kernel, tq=tq, ck=ck, n_sel=S // NSA_SEL_BLOCK)
    return pl.pallas_call(
        kern, grid=(B, nq),
        in_specs=[pl.BlockSpec((tq, 512), lambda b, i: (b * nq + i, 0)),
                  pl.BlockSpec((None,) + kcvc.shape[1:], lambda b, i: (b, 0, 0)),
                  kv_spec, kv_spec, kv_spec, kv_spec,
                  pl.BlockSpec((tq, LANES), lambda b, i: (b * nq + i, 0)),
                  pl.BlockSpec(ovt.shape, lambda b, i: (0, 0)),
                  pl.BlockSpec(e.shape, lambda b, i: (0, 0))],
        out_specs=pl.BlockSpec((tq, 512), lambda b, i: (b * nq + i, 0)),
        out_shape=jax.ShapeDtypeStruct((T, 512), BF16),
        scratch_shapes=[pltpu.VMEM((NSA_GROUPS, tq, S), F32)],
        compiler_params=_params(("parallel", "parallel")), name="nsa",
    )(q, kcvc, seq(ksel), seq(vsel), seq(kwin), seq(vwin), gates, ovt, e)


def _mlstm_kernel(q_ref, k_ref, v_ref, o_ref, gn_ref, gt_ref, cw_ref, bn_ref, bt_ref, ng_ref, tri_ref,
                  y_ref, c_scr, n_scr):
    H, dh, L = ML_HEADS, ML_DH, ML_CHUNK
    nb, S = q_ref.shape[0], q_ref.shape[1]
    nchunk = S // L
    c_scr[...] = jnp.zeros_like(c_scr)
    n_scr[...] = jnp.zeros_like(n_scr)
    row = lax.broadcasted_iota(I32, (L, H * dh), 0)
    li = lax.broadcasted_iota(I32, (L, L), 0)
    mi = lax.broadcasted_iota(I32, (L, L), 1)
    causal = mi <= li
    tril = tri_ref[0]
    triu = tri_ref[1]
    hp = lax.Precision.HIGHEST

    def conv_silu(ref, bi, c, wofs):
        r0 = pl.multiple_of(c * L, L)
        rp = pl.multiple_of(jnp.maximum(c - 1, 0) * L, L)
        cur = ref[bi, pl.ds(r0, L), :].astype(F32)
        prev = ref[bi, pl.ds(rp, L), :].astype(F32) * jnp.where(c > 0, 1.0, 0.0)
        acc = cur * cw_ref[ML_CONV - 1:ML_CONV, wofs:wofs + H * dh]
        for j in range(1, ML_CONV):
            sh = jnp.where(row < j, pltpu.roll(prev, j, 0), pltpu.roll(cur, j, 0))
            acc = acc + sh * cw_ref[ML_CONV - 1 - j:ML_CONV - j, wofs:wofs + H * dh]
        return acc * jax.nn.sigmoid(acc)

    def body(c, m_state):
        r0 = pl.multiple_of(c * L, L)
        new_m = []
        for bi in range(nb):
            qa = conv_silu(q_ref, bi, c, 0) * (dh ** -0.5)
            ka = conv_silu(k_ref, bi, c, H * dh)
            va = v_ref[bi, pl.ds(r0, L), :]
            oa = o_ref[bi, pl.ds(r0, L), :].astype(F32)
            gn = gn_ref[bi, pl.ds(r0, L), :] + bn_ref[...]
            gt = gt_ref[bi, :, c, :] + bt_ref[...]
            lf_n = jax.nn.log_sigmoid(gn)
            lf_t = jax.nn.log_sigmoid(gt)
            b_n = jnp.dot(tril, lf_n, precision=hp, preferred_element_type=F32)
            b_t = jnp.dot(lf_t, triu, precision=hp, preferred_element_type=F32)
            for h in range(H):
                st = bi * H + h
                q = qa[:, h * dh:(h + 1) * dh]
                k = ka[:, h * dh:(h + 1) * dh]
                v = va[:, h * dh:(h + 1) * dh]
                m_old = m_state[st]
                b_col = b_n[:, H + h:H + h + 1]
                i_col = gn[:, h:h + 1]
                b_row = b_t[H + h:H + h + 1, :]
                i_row = gt[h:h + 1, :]
                g_tot = b_t[H + h:H + h + 1, L - 1:L]
                d_log = jnp.where(causal, b_col - b_row + i_row, NEG)
                inter = b_col + m_old
                m_q = jnp.maximum(inter, jnp.max(d_log, axis=-1, keepdims=True))
                w_intra = jnp.exp(d_log - m_q)
                w_inter = jnp.exp(inter - m_q)
                qb = q.astype(BF16)
                s = lax.dot_general(qb, k.astype(BF16), _DN_T, preferred_element_type=F32) * w_intra
                cst = c_scr[st]
                nst = n_scr[st]
                num = (w_inter * jnp.dot(qb, cst.astype(BF16), preferred_element_type=F32) +
                       jnp.dot(s.astype(BF16), v, preferred_element_type=F32))
                den = w_inter * jnp.sum(q * nst, axis=-1, keepdims=True) + jnp.sum(s, axis=-1, keepdims=True)
                hv = num / jnp.maximum(jnp.abs(den), jnp.exp(-m_q))
                log_k = g_tot - b_col + i_col
                m_new = jnp.maximum(g_tot + m_old, jnp.max(log_k, axis=0, keepdims=True))
                wk = jnp.exp(log_k - m_new)
                decay = jnp.exp(g_tot + m_old - m_new)
                kw = k * wk
                c_scr[st] = decay * cst + lax.dot_general(kw.astype(BF16), v, _DN_TA, preferred_element_type=F32)
                n_scr[st] = decay * nst + jnp.sum(kw, axis=0, keepdims=True)
                new_m.append(m_new)
                mu = jnp.mean(hv, axis=-1, keepdims=True)
                dv = hv - mu
                var = jnp.mean(dv * dv, axis=-1, keepdims=True)
                hn = dv * lax.rsqrt(var + LN_EPS) * ng_ref[:, h * dh:(h + 1) * dh]
                og = jax.nn.sigmoid(oa[:, h * dh:(h + 1) * dh])
                y_ref[bi, pl.ds(r0, L), h * dh:(h + 1) * dh] = (og * hn).astype(y_ref.dtype)
        return tuple(new_m)

    lax.fori_loop(0, nchunk, body, tuple(jnp.zeros((1, 1), F32) for _ in range(nb * H)))


def _mlstm(mq, mk, mv, mo, mif, conv_w, i_bias, f_bias, norm_g, B, S, nb):
    T = B * S
    H, dh, L = ML_HEADS, ML_DH, ML_CHUNK
    W = H * dh
    gt = mif[:, :2 * H].reshape(B, S, 2 * H).transpose(0, 2, 1).reshape(B, 2 * H, S // L, L)
    cw = conv_w.reshape(ML_CONV, 2 * W)
    bias = jnp.concatenate([i_bias, f_bias])
    bn = jnp.pad(bias, (0, LANES - 2 * H)).reshape(1, LANES)
    bt = bias.reshape(2 * H, 1)
    ng = norm_g.reshape(1, W)
    tri = jnp.stack([jnp.tril(jnp.ones((L, L), F32)), jnp.triu(jnp.ones((L, L), F32))])
    seq = lambda a: a.reshape(B, S, a.shape[1])
    rows = lambda w: pl.BlockSpec((nb, S, w), lambda b: (b, 0, 0))
    full = lambda a: pl.BlockSpec(a.shape, lambda b: (0,) * a.ndim)
    y = pl.pallas_call(
        _mlstm_kernel, grid=(B // nb,),
        in_specs=[rows(W), rows(W), rows(W), rows(W), rows(LANES),
                  pl.BlockSpec((nb, 2 * H, S // L, L), lambda b: (b, 0, 0, 0)),
                  full(cw), full(bn), full(bt), full(ng), full(tri)],
        out_specs=rows(W),
        out_shape=jax.ShapeDtypeStruct((B, S, W), BF16),
        scratch_shapes=[pltpu.VMEM((nb * H, dh, dh), F32), pltpu.VMEM((nb * H, 1, dh), F32)],
        compiler_params=_params(("parallel",)), name="mlstm",
    )(seq(mq), seq(mk), seq(mv), seq(mo), seq(mif), gt, cw, bn, bt, ng, tri)
    return y.reshape(T, W)


def _outproj_kernel(yn_ref, ym_ref, x_ref, w_ref, g_ref, b_ref, o_ref):
    mix = (jnp.dot(yn_ref[...], w_ref[0:512, :], preferred_element_type=F32) +
           jnp.dot(ym_ref[...], w_ref[512:1024, :], preferred_element_type=F32))
    o_ref[...] = _ln_rows(ALPHA * x_ref[...] + mix, g_ref[...], b_ref[...])


def _outproj(y_nsa, y_ml, x2d, w_out, g, b, tm):
    T = x2d.shape[0]
    wn = w_out[:512].reshape(NSA_GROUPS, NSA_HPG, NSA_DK, D_MODEL).transpose(1, 0, 2, 3).reshape(512, D_MODEL)
    w = jnp.concatenate([wn, w_out[512:]], axis=0).astype(BF16)
    row = lambda wd: pl.BlockSpec((tm, wd), lambda i: (i, 0))
    full = lambda a: pl.BlockSpec(a.shape, lambda i: (0,) * a.ndim)
    g2, b2 = g.reshape(1, -1), b.reshape(1, -1)
    return pl.pallas_call(
        _outproj_kernel, grid=(T // tm,),
        in_specs=[row(512), row(512), row(D_MODEL), full(w), full(g2), full(b2)],
        out_specs=row(D_MODEL), out_shape=jax.ShapeDtypeStruct((T, D_MODEL), F32),
        compiler_params=_params(("parallel",)), name="outproj")(y_nsa, y_ml, x2d, w, g2, b2)


def _memkv_kernel(m_ref, w_ref, o_ref):
    o_ref[...] = jnp.dot(m_ref[...].astype(BF16), w_ref[...], preferred_element_type=F32).astype(o_ref.dtype)


def _memkv(mem2d, wk, wv):
    w = jnp.concatenate([wk, wv], axis=1).astype(BF16)
    R = mem2d.shape[0]
    return pl.pallas_call(
        _memkv_kernel, grid=(R // MEM_LEN,),
        in_specs=[pl.BlockSpec((MEM_LEN, D_MODEL), lambda i: (i, 0)),
                  pl.BlockSpec(w.shape, lambda i: (0, 0))],
        out_specs=pl.BlockSpec((MEM_LEN, 2 * D_MODEL), lambda i: (i, 0)),
        out_shape=jax.ShapeDtypeStruct((R, 2 * D_MODEL), BF16),
        compiler_params=_params(("parallel",)), name="memkv")(mem2d, w)


def _xattn_kernel(x_ref, kv_ref, wq_ref, wo_ref, g_ref, b_ref, rw_ref, x2_ref, x2p_ref, sc_ref):
    x1 = x_ref[...]
    q = jnp.dot(x1.astype(BF16), wq_ref[...], preferred_element_type=F32).astype(BF16)
    outs = []
    for h in range(XA_HEADS):
        qh = q[:, h * XA_DH:(h + 1) * XA_DH]
        kh = kv_ref[:, h * XA_DH:(h + 1) * XA_DH]
        vh = kv_ref[:, D_MODEL + h * XA_DH:D_MODEL + (h + 1) * XA_DH]
        s = lax.dot_general(qh, kh, _DN_T, preferred_element_type=F32) * (XA_DH ** -0.5)
        p = jnp.exp(s - jnp.max(s, axis=-1, keepdims=True))
        p = p / jnp.sum(p, axis=-1, keepdims=True)
        outs.append(jnp.dot(p.astype(BF16), vh, preferred_element_type=F32).astype(BF16))
    o = jnp.concatenate(outs, axis=1)
    xa = jnp.dot(o, wo_ref[...], preferred_element_type=F32)
    x2 = _ln_rows(ALPHA * x1 + xa, g_ref[...], b_ref[...])
    x2_ref[...] = x2
    x2p_ref[...] = _pack_bf16_pairs(x2)
    xh = x2.astype(BF16)
    xl = (x2 - xh.astype(F32)).astype(BF16)
    wh = rw_ref[0]
    wl = rw_ref[1]
    logit = (lax.dot_general(wh, xh, _DN_T, preferred_element_type=F32) +
             lax.dot_general(wh, xl, _DN_T, preferred_element_type=F32) +
             lax.dot_general(wl, xh, _DN_T, preferred_element_type=F32))
    sc_ref[...] = jax.nn.sigmoid(logit)


def _xattn(x1, kv, wq, wo, g, b, router_w, S, tq):
    T = x1.shape[0]
    wqb, wob = wq.astype(BF16), wo.astype(BF16)
    rwt = router_w.T
    rh = rwt.astype(BF16)
    rw = jnp.stack([rh, (rwt - rh.astype(F32)).astype(BF16)])
    g2, b2 = g.reshape(1, -1), b.reshape(1, -1)
    full = lambda a: pl.BlockSpec(a.shape, lambda i: (0,) * a.ndim)
    per = S // tq
    return pl.pallas_call(
        _xattn_kernel, grid=(T // tq,),
        in_specs=[pl.BlockSpec((tq, D_MODEL), lambda i: (i, 0)),
                  pl.BlockSpec((MEM_LEN, 2 * D_MODEL), lambda i: (i // per, 0)),
                  full(wqb), full(wob), full(g2), full(b2), full(rw)],
        out_specs=(pl.BlockSpec((tq, D_MODEL), lambda i: (i, 0)),
                   pl.BlockSpec((tq, D_MODEL // 2), lambda i: (i, 0)),
                   pl.BlockSpec((MOE_E, tq), lambda i: (0, i))),
        out_shape=(jax.ShapeDtypeStruct((T, D_MODEL), F32), jax.ShapeDtypeStruct((T, D_MODEL // 2), I32),
                   jax.ShapeDtypeStruct((MOE_E, T), F32)),
        compiler_params=_params(("parallel",)), name="xattn")(x1, kv, wqb, wob, g2, b2, rw)


def _route_kernel(sc_ref, rb_ref, idx_ref, w_ref):
    E, G = MOE_E, MOE_GROUPS
    per = E // G
    scores = sc_ref[...]
    tr = scores.shape[1]
    biased = scores + rb_ref[...]
    g3 = biased.reshape(G, per, tr)
    j3 = lax.broadcasted_iota(I32, (G, per, tr), 1)
    m1 = jnp.max(g3, axis=1, keepdims=True)
    first = jnp.min(jnp.where(g3 == m1, j3, per), axis=1, keepdims=True)
    m2 = jnp.max(jnp.where(j3 == first, -jnp.inf, g3), axis=1, keepdims=True)
    gs = (m1 + m2).reshape(G, tr)
    gi = lax.broadcasted_iota(I32, (G, tr), 0)
    cnt = jnp.zeros((G, tr), F32)
    for m in range(G):
        row = gs[m:m + 1, :]
        cnt = cnt + jnp.where((row > gs) | ((row == gs) & (gi > m)), 1.0, 0.0)
    gmask = cnt < float(MOE_TOPK_GROUPS)
    masked = jnp.where(gmask[:, None, :], g3, NEG).reshape(E, tr)
    ei = lax.broadcasted_iota(I32, (E, tr), 0)
    idxs, ws = [], []
    for _ in range(MOE_K):
        mx = jnp.max(masked, axis=0, keepdims=True)
        ix = jnp.min(jnp.where(masked == mx, ei, E), axis=0, keepdims=True)
        hit = ei == ix
        ws.append(jnp.sum(jnp.where(hit, scores, 0.0), axis=0, keepdims=True))
        idxs.append(ix)
        masked = jnp.where(hit, -jnp.inf, masked)
    w = jnp.concatenate(ws, axis=0)
    idx_ref[...] = jnp.concatenate(idxs, axis=0)
    w_ref[...] = w / jnp.sum(w, axis=0, keepdims=True) * MOE_ROUTE_SCALE


def _route(scores_t, router_bias, tr):
    E, T = scores_t.shape
    rb = router_bias.reshape(E, 1)
    return pl.pallas_call(
        _route_kernel, grid=(T // tr,),
        in_specs=[pl.BlockSpec((E, tr), lambda i: (0, i)), pl.BlockSpec((E, 1), lambda i: (0, 0))],
        out_specs=(pl.BlockSpec((MOE_K, tr), lambda i: (0, i)), pl.BlockSpec((MOE_K, tr), lambda i: (0, i))),
        out_shape=(jax.ShapeDtypeStruct((MOE_K, T), I32), jax.ShapeDtypeStruct((MOE_K, T), F32)),
        compiler_params=_params(("parallel",)), name="route")(scores_t, rb)


def _rank_kernel(idx_ref, u_ref, rank_ref, cnt_ref, carry):
    E = MOE_E

    @pl.when(pl.program_id(0) == 0)
    def _():
        carry[...] = jnp.zeros_like(carry)

    idx = idx_ref[...]
    tp = idx.shape[1]
    ei = lax.broadcasted_iota(I32, (E, tp), 0)
    hits = [ei == idx[k:k + 1, :] for k in range(MOE_K)]
    onehot = jnp.zeros((E, tp), F32)
    for hit in hits:
        onehot = onehot + jnp.where(hit, 1.0, 0.0)
    pos = jnp.dot(onehot.astype(BF16), u_ref[...], preferred_element_type=F32) + carry[...]
    ranks = [jnp.sum(jnp.where(hit, pos, 0.0), axis=0, keepdims=True) for hit in hits]
    rank_ref[...] = jnp.concatenate(ranks, axis=0).astype(I32)
    total = carry[...] + jnp.sum(onehot, axis=1, keepdims=True)
    carry[...] = total
    cnt_ref[...] = jnp.broadcast_to(total, cnt_ref.shape).astype(I32)


def _rank(idx_t, tp):
    K, T = idx_t.shape
    u = jnp.triu(jnp.ones((tp, tp), F32), k=1).astype(BF16)
    rank, cnt = pl.pallas_call(
        _rank_kernel, grid=(T // tp,),
        in_specs=[pl.BlockSpec((K, tp), lambda i: (0, i)), pl.BlockSpec((tp, tp), lambda i: (0, 0))],
        out_specs=(pl.BlockSpec((K, tp), lambda i: (0, i)), pl.BlockSpec((MOE_E, LANES), lambda i: (0, 0))),
        out_shape=(jax.ShapeDtypeStruct((K, T), I32), jax.ShapeDtypeStruct((MOE_E, LANES), I32)),
        scratch_shapes=[pltpu.VMEM((MOE_E, 1), F32)],
        compiler_params=_params(("arbitrary",)), name="rank")(idx_t, u)
    return rank, cnt[:, 0]


def _dest_kernel(idx_ref, rank_ref, po_ref, dest_ref):
    idx = idx_ref[...]
    tp = idx.shape[1]
    ei = lax.broadcasted_iota(I32, (MOE_E, tp), 0)
    po = po_ref[...]
    base = [jnp.sum(jnp.where(ei == idx[k:k + 1, :], po, 0.0), axis=0, keepdims=True) for k in range(MOE_K)]
    dest_ref[...] = jnp.concatenate(base, axis=0).astype(I32) + rank_ref[...]


def _dest(idx_t, rank_t, poffs, tp):
    K, T = idx_t.shape
    po = poffs.astype(F32).reshape(MOE_E, 1)
    spec = pl.BlockSpec((K, tp), lambda i: (0, i))
    return pl.pallas_call(
        _dest_kernel, grid=(T // tp,),
        in_specs=[spec, spec, pl.BlockSpec((MOE_E, 1), lambda i: (0, 0))],
        out_specs=spec, out_shape=jax.ShapeDtypeStruct((K, T), I32),
        compiler_params=_params(("parallel",)), name="dest")(idx_t, rank_t, po)


def _pack_bf16_pairs(v):
    m = v.shape[1] // 2
    bits = lax.bitcast_convert_type(v.astype(BF16).astype(F32), jnp.uint32)
    return lax.bitcast_convert_type((bits[:, :m] >> 16) | (bits[:, m:] & jnp.uint32(0xFFFF0000)), I32)


def _unpack_bf16_pairs(w):
    w = lax.bitcast_convert_type(w, jnp.uint32)
    lo = lax.bitcast_convert_type(w << 16, F32)
    hi = lax.bitcast_convert_type(w & jnp.uint32(0xFFFF0000), F32)
    return lo, hi


def _row_scatter(rows, dest_t, n_rows):
    T, d = rows.shape
    K = dest_t.shape[0]
    nw = SC_CORES * SC_SUBCORES
    per_w = T // nw
    ch = SC_GATHER_ROWS
    mesh = plsc.VectorSubcoreMesh(core_axis_name="c", subcore_axis_name="s")

    @functools.partial(
        pl.kernel, mesh=mesh, out_type=jax.ShapeDtypeStruct((n_rows, d), I32),
        scratch_types=[pltpu.VMEM((K, ch), I32), pltpu.VMEM((ch, d), I32), pltpu.SemaphoreType.DMA],
        name="row_scatter")
    def scatter(rows_hbm, dest_hbm, out_hbm, idx_v, rows_v, sem):
        wid = lax.axis_index("s") * SC_CORES + lax.axis_index("c")
        base = wid * per_w

        @pl.loop(0, per_w // ch)
        def _(i):
            off = pl.multiple_of(base + i * ch, ch)
            pltpu.sync_copy(rows_hbm.at[pl.ds(off, ch)], rows_v)
            pltpu.sync_copy(dest_hbm.at[:, pl.ds(off, ch)], idx_v)
            copies = [pltpu.async_copy(rows_v, out_hbm.at[idx_v.at[k]], sem) for k in range(K)]
            for cp in copies:
                cp.wait()

    return scatter(rows, dest_t)


RING_AHEAD = 5
RING_OUT = 3


def _expert_kernel(po_ref, cnt_ref, tot_ref, xs_hbm, w1_ref, w3_ref, w2_ref, ys_hbm,
                   xbuf, ybuf, w1b, w3b, w2b, insem, outsem):
    e = pl.program_id(0)
    n = cnt_ref[e]
    rb = xbuf.shape[1]
    ns = xbuf.shape[0]
    hw = D_MODEL // 2
    nblk = jnp.maximum((n + rb - 1) // rb, 1)
    g0 = po_ref[e] // rb
    total = tot_ref[0]
    w1b[...] = w1_ref[...].astype(BF16)
    w3b[...] = w3_ref[...].astype(BF16)
    w2b[...] = w2_ref[...].astype(BF16)

    def in_copy(g, slot):
        return pltpu.make_async_copy(xs_hbm.at[pl.ds(pl.multiple_of(g * rb, rb), rb)], xbuf.at[slot], insem.at[slot])

    def out_copy(g, slot):
        return pltpu.make_async_copy(ybuf.at[slot], ys_hbm.at[pl.ds(pl.multiple_of(g * rb, rb), rb)],
                                     outsem.at[slot])

    @pl.when(e == 0)
    def _():
        for d in range(RING_AHEAD):
            @pl.when(d < total)
            def _():
                in_copy(d, d).start()

    def body(j, c):
        g = g0 + j
        slot = g % ns
        oslot = g % RING_OUT

        @pl.when(g + RING_AHEAD < total)
        def _():
            in_copy(g + RING_AHEAD, (g + RING_AHEAD) % ns).start()

        in_copy(0, slot).wait()

        @pl.when(g >= RING_OUT)
        def _():
            out_copy(0, oslot).wait()

        words = xbuf[slot]
        row = j * rb + lax.broadcasted_iota(I32, words.shape, 0)
        lo, hi = _unpack_bf16_pairs(jnp.where(row < n, words, 0))
        lo, hi = lo.astype(BF16), hi.astype(BF16)
        a = (jnp.dot(lo, w1b[0:hw, :], preferred_element_type=F32) +
             jnp.dot(hi, w1b[hw:, :], preferred_element_type=F32))
        u = (jnp.dot(lo, w3b[0:hw, :], preferred_element_type=F32) +
             jnp.dot(hi, w3b[hw:, :], preferred_element_type=F32))
        h = (a * jax.nn.sigmoid(a) * u).astype(BF16)
        ybuf[oslot] = _pack_bf16_pairs(jnp.dot(h, w2b[...], preferred_element_type=F32))
        out_copy(g, oslot).start()
        return c

    lax.fori_loop(0, nblk, body, 0)

    @pl.when(e + 1 == pl.num_programs(0))
    def _():
        for i in range(RING_OUT):
            @pl.when(total > i)
            def _():
                out_copy(0, (total - 1 - i) % RING_OUT).wait()


def _experts(poffs, counts, total_blocks, xs, w1, w3, w2):
    n_rows, wp = xs.shape
    rb = ROW_BLOCK
    wspec = lambda shape: pl.BlockSpec((None,) + shape, lambda e, po, cn, tb: (e, 0, 0))
    gs = pltpu.PrefetchScalarGridSpec(
        num_scalar_prefetch=3, grid=(MOE_E,),
        in_specs=[pl.BlockSpec(memory_space=pl.ANY), wspec((D_MODEL, MOE_FF)), wspec((D_MODEL, MOE_FF)),
                  wspec((MOE_FF, D_MODEL))],
        out_specs=pl.BlockSpec(memory_space=pl.ANY),
        scratch_shapes=[pltpu.VMEM((RING_AHEAD + 1, rb, wp), I32), pltpu.VMEM((RING_OUT, rb, wp), I32),
                        pltpu.VMEM((D_MODEL, MOE_FF), BF16), pltpu.VMEM((D_MODEL, MOE_FF), BF16),
                        pltpu.VMEM((MOE_FF, D_MODEL), BF16),
                        pltpu.SemaphoreType.DMA((RING_AHEAD + 1,)), pltpu.SemaphoreType.DMA((RING_OUT,))])
    return pl.pallas_call(
        _expert_kernel, grid_spec=gs, out_shape=jax.ShapeDtypeStruct((n_rows, wp), I32),
        compiler_params=_params(("arbitrary",)), name="experts")(poffs, counts, total_blocks, xs, w1, w3, w2)


def _row_gather(table, idx):
    n, d = idx.shape[0], table.shape[1]
    nw = SC_CORES * SC_SUBCORES
    per_w = n // nw
    ch = SC_GATHER_ROWS // 2
    mesh = plsc.VectorSubcoreMesh(core_axis_name="c", subcore_axis_name="s")

    @functools.partial(
        pl.kernel, mesh=mesh, out_type=jax.ShapeDtypeStruct((n, d), I32),
        scratch_types=[pltpu.VMEM((ch,), I32), pltpu.VMEM((ch,), I32),
                       pltpu.VMEM((ch, d), I32), pltpu.VMEM((ch, d), I32),
                       pltpu.SemaphoreType.DMA, pltpu.SemaphoreType.DMA,
                       pltpu.SemaphoreType.DMA, pltpu.SemaphoreType.DMA],
        name="row_gather")
    def gather(table_hbm, idx_hbm, out_hbm, idx0, idx1, rows0, rows1, g0, g1, w0, w1):
        wid = lax.axis_index("s") * SC_CORES + lax.axis_index("c")
        base = wid * per_w

        @pl.loop(0, per_w // (2 * ch))
        def _(i):
            off0 = pl.multiple_of(base + 2 * i * ch, ch)
            off1 = pl.multiple_of(off0 + ch, ch)
            pltpu.sync_copy(idx_hbm.at[pl.ds(off0, ch)], idx0)
            c0 = pltpu.async_copy(table_hbm.at[idx0], rows0, g0)
            pltpu.sync_copy(idx_hbm.at[pl.ds(off1, ch)], idx1)
            c1 = pltpu.async_copy(table_hbm.at[idx1], rows1, g1)
            c0.wait()
            o0 = pltpu.async_copy(rows0, out_hbm.at[pl.ds(off0, ch)], w0)
            c1.wait()
            o1 = pltpu.async_copy(rows1, out_hbm.at[pl.ds(off1, ch)], w1)
            o0.wait()
            o1.wait()

    return gather(table, idx)


def _combine_kernel(w_ref, x_ref, ysg_ref, s1_ref, s3_ref, s2_ref, g_ref, b_ref, o_ref):
    x2 = x_ref[...]
    xb = x2.astype(BF16)
    a = jnp.dot(xb, s1_ref[...], preferred_element_type=F32)
    c = jnp.dot(xb, s3_ref[...], preferred_element_type=F32)
    shared = jnp.dot((a * jax.nn.sigmoid(a) * c).astype(BF16), s2_ref[...], preferred_element_type=F32)
    w = w_ref[...]
    hw = ysg_ref.shape[2]
    y_lo = jnp.zeros((x2.shape[0], hw), F32)
    y_hi = jnp.zeros((x2.shape[0], hw), F32)
    for k in range(MOE_K):
        lo, hi = _unpack_bf16_pairs(ysg_ref[k])
        y_lo = y_lo + w[:, k:k + 1] * lo
        y_hi = y_hi + w[:, k:k + 1] * hi
    y = shared + jnp.concatenate([y_lo, y_hi], axis=1)
    o_ref[...] = _ln_rows(ALPHA * x2 + y, g_ref[...], b_ref[...])


def _combine(w_nat, x2, ysg, sw1, sw3, sw2, g, b, tc):
    T = x2.shape[0]
    wp = ysg.shape[2]
    full = lambda a: pl.BlockSpec(a.shape, lambda i: (0,) * a.ndim)
    s1, s3, s2 = sw1.astype(BF16), sw3.astype(BF16), sw2.astype(BF16)
    g2, b2 = g.reshape(1, -1), b.reshape(1, -1)
    return pl.pallas_call(
        _combine_kernel, grid=(T // tc,),
        in_specs=[pl.BlockSpec((tc, MOE_K), lambda i: (i, 0)),
                  pl.BlockSpec((tc, D_MODEL), lambda i: (i, 0)),
                  pl.BlockSpec((MOE_K, tc, wp), lambda i: (0, i, 0)),
                  full(s1), full(s3), full(s2), full(g2), full(b2)],
        out_specs=pl.BlockSpec((tc, D_MODEL), lambda i: (i, 0)),
        out_shape=jax.ShapeDtypeStruct((T, D_MODEL), F32),
        compiler_params=_params(("parallel",)), name="combine",
    )(w_nat, x2, ysg, s1, s3, s2, g2, b2)


def _moe(x2, x2p, scores_t, router_bias, w1, w3, w2, sw1, sw3, sw2, g, b, tiles):
    T = x2.shape[0]
    rb = ROW_BLOCK
    idx_t, w_t = _route(scores_t, router_bias, tiles['route'])
    rank_t, counts = _rank(idx_t, tiles['rank'])
    pcounts = jnp.maximum((counts + rb - 1) // rb, 1) * rb
    pends = jnp.cumsum(pcounts)
    poffs = (pends - pcounts).astype(I32)
    n_blocks = -(-T * MOE_K // rb) + MOE_E
    dest_t = _dest(idx_t, rank_t, poffs, tiles['rank'])
    xs = _row_scatter(x2p, dest_t, n_blocks * rb)
    total_blocks = (pends[-1:] // rb).astype(I32)
    ys = _experts(poffs, counts.astype(I32), total_blocks, xs, w1, w3, w2)
    ysg = _row_gather(ys, dest_t.reshape(-1)).reshape(MOE_K, T, -1)
    return _combine(w_t.T, x2, ysg, sw1, sw3, sw2, g, b, tiles['combine'])


def _tiles(B, S):
    T = B * S
    pick = lambda want, n: want if n % want == 0 else n
    return dict(proj=pick(512, T), nsa_q=pick(128, S), nsa_ck=pick(512, S), outproj=pick(1024, T),
                mlstm_nb=2 if B % 2 == 0 else 1, xattn=pick(1024, S), route=pick(1024, T), rank=pick(512, T),
                combine=pick(512, T))


def kernel(x, mem, w_in, nsa_pos_k, nsa_cmp_k_w1, nsa_cmp_k_w2, nsa_pos_v, nsa_cmp_v_w1, nsa_cmp_v_w2,
           mlstm_conv_w, mlstm_i_bias, mlstm_f_bias, mlstm_norm_g, w_out, ln1_g, ln1_b,
           xa_wq, xa_wk, xa_wv, xa_wo, ln2_g, ln2_b, router_w, router_bias,
           moe_w1, moe_w3, moe_w2, shared_w1, shared_w3, shared_w2, ln3_g, ln3_b):
    B, S, D = x.shape
    T = B * S
    tl = _tiles(B, S)
    xc = x.reshape(T, D)
    memc = mem.reshape(B * MEM_LEN, D)
    for l in range(w_in.shape[0]):
        (q, cmp, ksel, vsel, kwin, vwin, gates, mq, mk, mv, mo, mif) = _project(xc, _prep_w_in(w_in[l]), tl['proj'])
        kcvc = _compress(cmp, B, S, _prep_cmp(nsa_pos_k[l], nsa_cmp_k_w1[l], nsa_cmp_k_w2[l],
                                              nsa_pos_v[l], nsa_cmp_v_w1[l], nsa_cmp_v_w2[l]))
        y_nsa = _nsa(q, kcvc, ksel, vsel, kwin, vwin, gates, B, S, tl['nsa_q'], tl['nsa_ck'])
        y_ml = _mlstm(mq, mk, mv, mo, mif, mlstm_conv_w[l], mlstm_i_bias[l], mlstm_f_bias[l],
                      mlstm_norm_g[l], B, S, tl['mlstm_nb'])
        x1 = _outproj(y_nsa, y_ml, xc, w_out[l], ln1_g[l], ln1_b[l], tl['outproj'])
        kv = _memkv(memc, xa_wk[l], xa_wv[l])
        x2, x2p, scores_t = _xattn(x1, kv, xa_wq[l], xa_wo[l], ln2_g[l], ln2_b[l], router_w[l], S, tl['xattn'])
        xc = _moe(x2, x2p, scores_t, router_bias[l], moe_w1[l], moe_w3[l], moe_w2[l],
                  shared_w1[l], shared_w3[l], shared_w2[l], ln3_g[l], ln3_b[l], tl)
    return xc.reshape(B, S, D)
```

```python
import functools
import numpy as np
import jax
import jax.numpy as jnp
from jax import lax
from jax.experimental import pallas as pl
from jax.experimental.pallas import tpu as pltpu
from jax.experimental.pallas import tpu_sc as plsc

F32 = jnp.float32
BF16 = jnp.bfloat16
I32 = jnp.int32

D_MODEL = 1024
MEM_LEN = 256
NSA_HEADS = 8
NSA_GROUPS = 2
NSA_HPG = 4
NSA_DK = 64
NSA_CMP_LEN = 32
NSA_CMP_STRIDE = 16
NSA_SEL_BLOCK = 64
NSA_SEL_TOPN = 8
NSA_WINDOW = 512
ML_HEADS = 4
ML_DH = 128
ML_CHUNK = 64
ML_CONV = 4
XA_HEADS = 4
XA_DH = 256
MOE_E = 256
MOE_K = 8
MOE_GROUPS = 8
MOE_TOPK_GROUPS = 4
MOE_FF = 256
MOE_ROUTE_SCALE = 2.5
DEPTH = 1
ALPHA = (2.0 * DEPTH) ** 0.25
LN_EPS = 1e-5
NEG = -1e30
FORCE_BONUS = 1e4

LANES = 128
ROW_BLOCK = 512
VMEM_LIMIT = 56 * 1024 * 1024
SC_CORES = 2
SC_SUBCORES = 16
SC_GATHER_ROWS = 128

_DN_T = (((1,), (1,)), ((), ()))
_DN_TA = (((0,), (0,)), ((), ()))


def _params(sem):
    return pltpu.CompilerParams(dimension_semantics=sem, vmem_limit_bytes=VMEM_LIMIT)


def _ln_rows(v, g, b):
    mu = jnp.mean(v, axis=-1, keepdims=True)
    d = v - mu
    var = jnp.mean(d * d, axis=-1, keepdims=True)
    return d * lax.rsqrt(var + LN_EPS) * g + b


_SEGS = (('q', 512, BF16), ('cmp', 256, F32), ('ksel', 128, BF16), ('vsel', 128, BF16),
         ('kwin', 128, BF16), ('vwin', 128, BF16), ('gates', 128, F32), ('mq', 512, BF16),
         ('mk', 512, BF16), ('mv', 512, BF16), ('mo', 512, BF16), ('mif', 128, F32))


def _proj_kernel(x_ref, w_ref, *out_refs):
    xb = x_ref[...].astype(BF16)
    off = 0
    for o_ref, (_, wd, _) in zip(out_refs, _SEGS):
        o_ref[...] = jnp.dot(xb, w_ref[:, off:off + wd], preferred_element_type=F32).astype(o_ref.dtype)
        off += wd


def _prep_w_in(w):
    sizes = (512,) + (128,) * 6 + (24,) + (512,) * 4 + (4, 4)
    cuts = np.cumsum(sizes)[:-1].tolist()
    (wq, kc, vc, ks, vs, kw, vw, wg, mq, mk, mv, mo, mi, mf) = jnp.split(w, cuts, axis=1)
    wq = wq.reshape(D_MODEL, NSA_GROUPS, NSA_HPG, NSA_DK).transpose(0, 2, 1, 3).reshape(D_MODEL, 512)
    pad = lambda a: jnp.pad(a, ((0, 0), (0, LANES - a.shape[1])))
    segs = [wq, kc, vc, ks, vs, kw, vw, pad(wg), mq, mk, mv, mo, pad(jnp.concatenate([mi, mf], axis=1))]
    return jnp.concatenate(segs, axis=1).astype(BF16)


def _project(x2d, w_all, tm):
    T = x2d.shape[0]
    n = w_all.shape[1]
    out_shape = tuple(jax.ShapeDtypeStruct((T, wd), dt) for _, wd, dt in _SEGS)
    out_specs = tuple(pl.BlockSpec((tm, wd), lambda i: (i, 0)) for _, wd, _ in _SEGS)
    return pl.pallas_call(
        _proj_kernel, grid=(T // tm,),
        in_specs=[pl.BlockSpec((tm, D_MODEL), lambda i: (i, 0)),
                  pl.BlockSpec((D_MODEL, n), lambda i: (0, 0))],
        out_specs=out_specs, out_shape=out_shape,
        compiler_params=_params(("parallel",)), name="proj")(x2d, w_all)


def _cmp_kernel(r_ref, pa_ref, pb_ref, wa_ref, wb_ref, w2_ref, o_ref):
    r = r_ref[...]
    a = jnp.dot((r + pa_ref[...]).astype(BF16), wa_ref[...], preferred_element_type=F32)
    b = jnp.dot((r + pb_ref[...]).astype(BF16), wb_ref[...], preferred_element_type=F32)
    nr = r.shape[0]
    hid = a + pltpu.roll(b, nr - 1, 0)
    hid = hid * jax.nn.sigmoid(hid)
    out = jnp.dot(hid.astype(BF16), w2_ref[...], preferred_element_type=F32)
    row = lax.broadcasted_iota(I32, out.shape, 0)
    o_ref[...] = jnp.where(row < nr - 1, out, 0.0).astype(o_ref.dtype)


def _prep_cmp(pos_k, w1_k, w2_k, pos_v, w1_v, w2_v):
    eye = jnp.eye(NSA_GROUPS, dtype=F32)

    def expand_w1(w1, half):
        w = w1.reshape(NSA_CMP_LEN, NSA_DK, NSA_DK)[half * 16:(half + 1) * 16]
        return jnp.einsum('jde,gk->jgdke', w, eye).reshape(16, 128, 128)

    def both(fk, fv):
        z = jnp.zeros_like(fk)
        top = jnp.concatenate([fk, z], axis=-1)
        bot = jnp.concatenate([z, fv], axis=-1)
        return jnp.concatenate([top, bot], axis=-2)

    wa = both(expand_w1(w1_k, 0), expand_w1(w1_v, 0)).reshape(16 * 256, 256).astype(BF16)
    wb = both(expand_w1(w1_k, 1), expand_w1(w1_v, 1)).reshape(16 * 256, 256).astype(BF16)
    w2 = both(jnp.kron(eye, w2_k), jnp.kron(eye, w2_v)).astype(BF16)

    def pos_row(half):
        pk = jnp.tile(pos_k[half * 16:(half + 1) * 16], (1, NSA_GROUPS))
        pv = jnp.tile(pos_v[half * 16:(half + 1) * 16], (1, NSA_GROUPS))
        return jnp.concatenate([pk, pv], axis=1).reshape(1, 16 * 256)

    return pos_row(0), pos_row(1), wa, wb, w2


def _compress(cmp2d, B, S, prep):
    pa, pb, wa, wb, w2 = prep
    nr = S // NSA_CMP_STRIDE
    r = cmp2d.reshape(B, nr, NSA_CMP_STRIDE * 256)
    full = lambda a: pl.BlockSpec(a.shape, lambda b: (0,) * a.ndim)
    return pl.pallas_call(
        _cmp_kernel, grid=(B,),
        in_specs=[pl.BlockSpec((None, nr, NSA_CMP_STRIDE * 256), lambda b: (b, 0, 0)),
                  full(pa), full(pb), full(wa), full(wb), full(w2)],
        out_specs=pl.BlockSpec((None, nr, 256), lambda b: (b, 0, 0)),
        out_shape=jax.ShapeDtypeStruct((B, nr, 256), BF16),
        compiler_params=_params(("parallel",)), name="cmp")(r, pa, pb, wa, wb, w2)


def _nsa_consts(S):
    n_cmp = (S - NSA_CMP_LEN) // NSA_CMP_STRIDE + 1
    n_sel = S // NSA_SEL_BLOCK
    cs = np.arange(n_cmp) * NSA_CMP_STRIDE
    ss = np.arange(n_sel) * NSA_SEL_BLOCK
    ov = ((cs[:, None] < ss[None, :] + NSA_SEL_BLOCK) & (cs[:, None] + NSA_CMP_LEN > ss[None, :]))
    ovt = np.zeros((LANES, S // NSA_CMP_STRIDE), np.float32)
    ovt[:n_sel, :n_cmp] = ov.T
    e = np.zeros((LANES, S), np.float32)
    e[np.arange(S) // NSA_SEL_BLOCK, np.arange(S)] = 1.0
    return jnp.asarray(ovt, BF16), jnp.asarray(e, BF16)


def _nsa_kernel(q_ref, kcvc_ref, ksel_ref, vsel_ref, kwin_ref, vwin_ref, gates_ref, ovt_ref, e_ref,
                y_ref, bias_ref, *, tq, ck, n_sel):
    G, H = NSA_GROUPS, NSA_HPG
    GH = G * H
    M = GH * tq
    S = ksel_ref.shape[0]
    W = NSA_WINDOW
    ws = min(W + tq, S)
    t0 = pl.program_id(1) * tq
    gates = jax.nn.sigmoid(gates_ref[...])
    lane = lax.broadcasted_iota(I32, (tq, LANES), 1)
    t_col = t0 + lax.broadcasted_iota(I32, (tq, 1), 0)
    kc = kcvc_ref[:, 0:LANES]
    vc = kcvc_ref[:, LANES:2 * LANES]
    nc = kc.shape[0]
    qs = []
    for g in range(G):
        gmask = (lane // NSA_DK) == g
        for h in range(H):
            qh = q_ref[:, h * LANES:(h + 1) * LANES] * (NSA_DK ** -0.5)
            qs.append(jnp.where(gmask, qh, jnp.zeros_like(qh)))
    Q = jnp.concatenate(qs, axis=0).astype(BF16)

    s = lax.dot_general(Q, kc, _DN_T, preferred_element_type=F32)
    c_idx = lax.broadcasted_iota(I32, (tq, nc), 1)
    cmask = (c_idx * NSA_CMP_STRIDE + NSA_CMP_LEN - 1) <= t_col
    s3 = jnp.where(cmask[None], s.reshape(GH, tq, nc), NEG)
    p = jnp.exp(s3 - jnp.max(s3, axis=-1, keepdims=True))
    p = p / jnp.sum(p, axis=-1, keepdims=True)
    p = jnp.where(cmask[None], p, 0.0)
    o_cmp = jnp.dot(p.reshape(M, nc).astype(BF16), vc, preferred_element_type=F32).reshape(GH, tq, LANES)

    for g in range(G):
        psum = jnp.sum(p[g * H:(g + 1) * H], axis=0)
        hi = psum.astype(BF16)
        lo = (psum - hi.astype(F32)).astype(BF16)
        ovt = ovt_ref[...]
        pslt = (lax.dot_general(ovt, hi, _DN_T, preferred_element_type=F32) +
                lax.dot_general(ovt, lo, _DN_T, preferred_element_type=F32))
        imp_p = pslt[0:n_sel, :]
        n_i = lax.broadcasted_iota(I32, (n_sel, tq), 0)
        cur = (t0 + lax.broadcasted_iota(I32, (n_sel, tq), 1)) // NSA_SEL_BLOCK
        forced = (n_i == 0) | (n_i == cur) | (n_i == cur - 1)
        imp = jnp.where(n_i <= cur, imp_p + jnp.where(forced, FORCE_BONUS, 0.0), NEG)
        cnt = jnp.zeros((n_sel, tq), F32)
        for m in range(n_sel):
            row = imp[m:m + 1, :]
            beats = (row > imp) | ((row == imp) & (n_i > m))
            cnt = cnt + jnp.where(beats, 1.0, 0.0)
        selt = jnp.where(cnt < float(min(NSA_SEL_TOPN, n_sel)), 1.0, 0.0)
        selt = jnp.concatenate([selt, jnp.zeros((LANES - n_sel, tq), F32)], axis=0)
        sel = selt.T.astype(BF16)
        maskf = jnp.dot(sel, e_ref[...], preferred_element_type=F32)
        kpos = lax.broadcasted_iota(I32, (tq, S), 1)
        bias_ref[g] = jnp.where((maskf > 0.5) & (kpos <= t_col), 0.0, NEG)

    vlane = lax.broadcasted_iota(I32, (1, LANES), 1) // NSA_DK

    def pv_with_sums(pb, v):
        outs = []
        for g in range(G):
            vg = jnp.where(vlane == g, v, jnp.ones_like(v))
            outs.append(jnp.dot(pb[g * H * tq:(g + 1) * H * tq], vg, preferred_element_type=F32))
        return jnp.concatenate(outs, axis=0)

    def normalise(acc):
        outs = []
        for r in range(GH):
            c = NSA_DK * (1 - r // H)
            outs.append(acc[r] / acc[r][:, c:c + 1])
        return outs

    def sel_body(j, carry):
        m_i, acc = carry
        ks = pl.multiple_of(j * ck, ck)
        k = ksel_ref[pl.ds(ks, ck), :]
        v = vsel_ref[pl.ds(ks, ck), :]
        sj = lax.dot_general(Q, k, _DN_T, preferred_element_type=F32)
        sj = (sj.reshape(G, H, tq, ck) + bias_ref[:, :, pl.ds(ks, ck)][:, None]).reshape(GH, tq, ck)
        m_new = jnp.maximum(m_i, jnp.max(sj, axis=-1, keepdims=True))
        a = jnp.exp(m_i - m_new)
        pj = jnp.exp((sj - m_new).astype(BF16)).reshape(M, ck)
        return m_new, a * acc + pv_with_sums(pj, v).reshape(GH, tq, LANES)

    init = (jnp.full((GH, tq, 1), NEG, F32), jnp.zeros((GH, tq, LANES), F32))
    _, acc = lax.fori_loop(0, (t0 + tq + ck - 1) // ck, sel_body, init)
    o_sel = normalise(acc)

    kst = pl.multiple_of(jnp.clip(t0 - W, 0, S - ws), LANES)
    kwn = kwin_ref[pl.ds(kst, ws), :]
    vwn = vwin_ref[pl.ds(kst, ws), :]
    sw = lax.dot_general(Q, kwn, _DN_T, preferred_element_type=F32)
    wpos = kst + lax.broadcasted_iota(I32, (tq, ws), 1)
    wmask = (wpos <= t_col) & (wpos > t_col - W)
    sw3 = jnp.where(wmask[None], sw.reshape(GH, tq, ws), NEG)
    pw = jnp.exp((sw3 - jnp.max(sw3, axis=-1, keepdims=True)).astype(BF16)).reshape(M, ws)
    o_win = normalise(pv_with_sums(pw, vwn).reshape(GH, tq, LANES))

    g0mask = lane < NSA_DK
    for h in range(H):
        o_g = []
        for g in range(G):
            r = g * H + h
            c0 = r * 3
            o_g.append(gates[:, c0:c0 + 1] * o_cmp[r] + gates[:, c0 + 1:c0 + 2] * o_sel[r] +
                       gates[:, c0 + 2:c0 + 3] * o_win[r])
        y_ref[:, h * LANES:(h + 1) * LANES] = jnp.where(g0mask, o_g[0], o_g[1]).astype(y_ref.dtype)


def _nsa(q, kcvc, ksel, vsel, kwin, vwin, gates, B, S, tq, ck):
    T = B * S
    nq = S // tq
    ovt, e = _nsa_consts(S)
    seq = lambda a: a.reshape(B, S, LANES)
    kv_spec = pl.BlockSpec((None, S, LANES), lambda b, i: (b, 0, 0))
    kern = functools.partial(_nsa_kernel, tq=tq, ck=ck, n_sel=S // NSA_SEL_BLOCK)
    return pl.pallas_call(
        kern, grid=(B, nq),
        in_specs=[pl.BlockSpec((tq, 512), lambda b, i: (b * nq + i, 0)),
                  pl.BlockSpec((None,) + kcvc.shape[1:], lambda b, i: (b, 0, 0)),
                  kv_spec, kv_spec, kv_spec, kv_spec,
                  pl.BlockSpec((tq, LANES), lambda b, i: (b * nq + i, 0)),
                  pl.BlockSpec(ovt.shape, lambda b, i: (0, 0)),
                  pl.BlockSpec(e.shape, lambda b, i: (0, 0))],
        out_specs=pl.BlockSpec((tq, 512), lambda b, i: (b * nq + i, 0)),
        out_shape=jax.ShapeDtypeStruct((T, 512), BF16),
        scratch_shapes=[pltpu.VMEM((NSA_GROUPS, tq, S), F32)],
        compiler_params=_params(("parallel", "parallel")), name="nsa",
    )(q, kcvc, seq(ksel), seq(vsel), seq(kwin), seq(vwin), gates, ovt, e)


def _mlstm_kernel(q_ref, k_ref, v_ref, o_ref, gn_ref, gt_ref, cw_ref, bn_ref, bt_ref, ng_ref, tri_ref,
                  y_ref, c_scr, n_scr):
    H, dh, L = ML_HEADS, ML_DH, ML_CHUNK
    nb, S = q_ref.shape[0], q_ref.shape[1]
    nchunk = S // L
    c_scr[...] = jnp.zeros_like(c_scr)
    n_scr[...] = jnp.zeros_like(n_scr)
    row = lax.broadcasted_iota(I32, (L, H * dh), 0)
    li = lax.broadcasted_iota(I32, (L, L), 0)
    mi = lax.broadcasted_iota(I32, (L, L), 1)
    causal = mi <= li
    tril = tri_ref[0]
    triu = tri_ref[1]
    hp = lax.Precision.HIGHEST

    def conv_silu(ref, bi, c, wofs):
        r0 = pl.multiple_of(c * L, L)
        rp = pl.multiple_of(jnp.maximum(c - 1, 0) * L, L)
        cur = ref[bi, pl.ds(r0, L), :].astype(F32)
        prev = ref[bi, pl.ds(rp, L), :].astype(F32) * jnp.where(c > 0, 1.0, 0.0)
        acc = cur * cw_ref[ML_CONV - 1:ML_CONV, wofs:wofs + H * dh]
        for j in range(1, ML_CONV):
            sh = jnp.where(row < j, pltpu.roll(prev, j, 0), pltpu.roll(cur, j, 0))
            acc = acc + sh * cw_ref[ML_CONV - 1 - j:ML_CONV - j, wofs:wofs + H * dh]
        return acc * jax.nn.sigmoid(acc)

    def body(c, m_state):
        r0 = pl.multiple_of(c * L, L)
        new_m = []
        for bi in range(nb):
            qa = conv_silu(q_ref, bi, c, 0) * (dh ** -0.5)
            ka = conv_silu(k_ref, bi, c, H * dh)
            va = v_ref[bi, pl.ds(r0, L), :]
            oa = o_ref[bi, pl.ds(r0, L), :].astype(F32)
            gn = gn_ref[bi, pl.ds(r0, L), :] + bn_ref[...]
            gt = gt_ref[bi, :, c, :] + bt_ref[...]
            lf_n = jax.nn.log_sigmoid(gn)
            lf_t = jax.nn.log_sigmoid(gt)
            b_n = jnp.dot(tril, lf_n, precision=hp, preferred_element_type=F32)
            b_t = jnp.dot(lf_t, triu, precision=hp, preferred_element_type=F32)
            for h in range(H):
                st = bi * H + h
                q = qa[:, h * dh:(h + 1) * dh]
                k = ka[:, h * dh:(h + 1) * dh]
                v = va[:, h * dh:(h + 1) * dh]
                m_old = m_state[st]
                b_col = b_n[:, H + h:H + h + 1]
                i_col = gn[:, h:h + 1]
                b_row = b_t[H + h:H + h + 1, :]
                i_row = gt[h:h + 1, :]
                g_tot = b_t[H + h:H + h + 1, L - 1:L]
                d_log = jnp.where(causal, b_col - b_row + i_row, NEG)
                inter = b_col + m_old
                m_q = jnp.maximum(inter, jnp.max(d_log, axis=-1, keepdims=True))
                w_intra = jnp.exp(d_log - m_q)
                w_inter = jnp.exp(inter - m_q)
                qb = q.astype(BF16)
                s = lax.dot_general(qb, k.astype(BF16), _DN_T, preferred_element_type=F32) * w_intra
                cst = c_scr[st]
                nst = n_scr[st]
                num = (w_inter * jnp.dot(qb, cst.astype(BF16), preferred_element_type=F32) +
                       jnp.dot(s.astype(BF16), v, preferred_element_type=F32))
                den = w_inter * jnp.sum(q * nst, axis=-1, keepdims=True) + jnp.sum(s, axis=-1, keepdims=True)
                hv = num / jnp.maximum(jnp.abs(den), jnp.exp(-m_q))
                log_k = g_tot - b_col + i_col
                m_new = jnp.maximum(g_tot + m_old, jnp.max(log_k, axis=0, keepdims=True))
                wk = jnp.exp(log_k - m_new)
                decay = jnp.exp(g_tot + m_old - m_new)
                kw = k * wk
                c_scr[st] = decay * cst + lax.dot_general(kw.astype(BF16), v, _DN_TA, preferred_element_type=F32)
                n_scr[st] = decay * nst + jnp.sum(kw, axis=0, keepdims=True)
                new_m.append(m_new)
                mu = jnp.mean(hv, axis=-1, keepdims=True)
                dv = hv - mu
                var = jnp.mean(dv * dv, axis=-1, keepdims=True)
                hn = dv * lax.rsqrt(var + LN_EPS) * ng_ref[:, h * dh:(h + 1) * dh]
                og = jax.nn.sigmoid(oa[:, h * dh:(h + 1) * dh])
                y_ref[bi, pl.ds(r0, L), h * dh:(h + 1) * dh] = (og * hn).astype(y_ref.dtype)
        return tuple(new_m)

    lax.fori_loop(0, nchunk, body, tuple(jnp.zeros((1, 1), F32) for _ in range(nb * H)))


def _mlstm(mq, mk, mv, mo, mif, conv_w, i_bias, f_bias, norm_g, B, S, nb):
    T = B * S
    H, dh, L = ML_HEADS, ML_DH, ML_CHUNK
    W = H * dh
    gt = mif[:, :2 * H].reshape(B, S, 2 * H).transpose(0, 2, 1).reshape(B, 2 * H, S // L, L)
    cw = conv_w.reshape(ML_CONV, 2 * W)
    bias = jnp.concatenate([i_bias, f_bias])
    bn = jnp.pad(bias, (0, LANES - 2 * H)).reshape(1, LANES)
    bt = bias.reshape(2 * H, 1)
    ng = norm_g.reshape(1, W)
    tri = jnp.stack([jnp.tril(jnp.ones((L, L), F32)), jnp.triu(jnp.ones((L, L), F32))])
    seq = lambda a: a.reshape(B, S, a.shape[1])
    rows = lambda w: pl.BlockSpec((nb, S, w), lambda b: (b, 0, 0))
    full = lambda a: pl.BlockSpec(a.shape, lambda b: (0,) * a.ndim)
    y = pl.pallas_call(
        _mlstm_kernel, grid=(B // nb,),
        in_specs=[rows(W), rows(W), rows(W), rows(W), rows(LANES),
                  pl.BlockSpec((nb, 2 * H, S // L, L), lambda b: (b, 0, 0, 0)),
                  full(cw), full(bn), full(bt), full(ng), full(tri)],
        out_specs=rows(W),
        out_shape=jax.ShapeDtypeStruct((B, S, W), BF16),
        scratch_shapes=[pltpu.VMEM((nb * H, dh, dh), F32), pltpu.VMEM((nb * H, 1, dh), F32)],
        compiler_params=_params(("parallel",)), name="mlstm",
    )(seq(mq), seq(mk), seq(mv), seq(mo), seq(mif), gt, cw, bn, bt, ng, tri)
    return y.reshape(T, W)


def _outproj_kernel(yn_ref, ym_ref, x_ref, w_ref, g_ref, b_ref, o_ref):
    mix = (jnp.dot(yn_ref[...], w_ref[0:512, :], preferred_element_type=F32) +
           jnp.dot(ym_ref[...], w_ref[512:1024, :], preferred_element_type=F32))
    o_ref[...] = _ln_rows(ALPHA * x_ref[...] + mix, g_ref[...], b_ref[...])


def _outproj(y_nsa, y_ml, x2d, w_out, g, b, tm):
    T = x2d.shape[0]
    wn = w_out[:512].reshape(NSA_GROUPS, NSA_HPG, NSA_DK, D_MODEL).transpose(1, 0, 2, 3).reshape(512, D_MODEL)
    w = jnp.concatenate([wn, w_out[512:]], axis=0).astype(BF16)
    row = lambda wd: pl.BlockSpec((tm, wd), lambda i: (i, 0))
    full = lambda a: pl.BlockSpec(a.shape, lambda i: (0,) * a.ndim)
    g2, b2 = g.reshape(1, -1), b.reshape(1, -1)
    return pl.pallas_call(
        _outproj_kernel, grid=(T // tm,),
        in_specs=[row(512), row(512), row(D_MODEL), full(w), full(g2), full(b2)],
        out_specs=row(D_MODEL), out_shape=jax.ShapeDtypeStruct((T, D_MODEL), F32),
        compiler_params=_params(("parallel",)), name="outproj")(y_nsa, y_ml, x2d, w, g2, b2)


def _memkv_kernel(m_ref, w_ref, o_ref):
    o_ref[...] = jnp.dot(m_ref[...].astype(BF16), w_ref[...], preferred_element_type=F32).astype(o_ref.dtype)


def _memkv(mem2d, wk, wv):
    w = jnp.concatenate([wk, wv], axis=1).astype(BF16)
    R = mem2d.shape[0]
    return pl.pallas_call(
        _memkv_kernel, grid=(R // MEM_LEN,),
        in_specs=[pl.BlockSpec((MEM_LEN, D_MODEL), lambda i: (i, 0)),
                  pl.BlockSpec(w.shape, lambda i: (0, 0))],
        out_specs=pl.BlockSpec((MEM_LEN, 2 * D_MODEL), lambda i: (i, 0)),
        out_shape=jax.ShapeDtypeStruct((R, 2 * D_MODEL), BF16),
        compiler_params=_params(("parallel",)), name="memkv")(mem2d, w)


def _xattn_kernel(x_ref, kv_ref, wq_ref, wo_ref, g_ref, b_ref, rw_ref, x2_ref, x2p_ref, sc_ref):
    x1 = x_ref[...]
    q = jnp.dot(x1.astype(BF16), wq_ref[...], preferred_element_type=F32).astype(BF16)
    outs = []
    for h in range(XA_HEADS):
        qh = q[:, h * XA_DH:(h + 1) * XA_DH]
        kh = kv_ref[:, h * XA_DH:(h + 1) * XA_DH]
        vh = kv_ref[:, D_MODEL + h * XA_DH:D_MODEL + (h + 1) * XA_DH]
        s = lax.dot_general(qh, kh, _DN_T, preferred_element_type=F32) * (XA_DH ** -0.5)
        p = jnp.exp(s - jnp.max(s, axis=-1, keepdims=True))
        p = p / jnp.sum(p, axis=-1, keepdims=True)
        outs.append(jnp.dot(p.astype(BF16), vh, preferred_element_type=F32).astype(BF16))
    o = jnp.concatenate(outs, axis=1)
    xa = jnp.dot(o, wo_ref[...], preferred_element_type=F32)
    x2 = _ln_rows(ALPHA * x1 + xa, g_ref[...], b_ref[...])
    x2_ref[...] = x2
    x2p_ref[...] = _pack_bf16_pairs(x2)
    xh = x2.astype(BF16)
    xl = (x2 - xh.astype(F32)).astype(BF16)
    wh = rw_ref[0]
    wl = rw_ref[1]
    logit = (lax.dot_general(wh, xh, _DN_T, preferred_element_type=F32) +
             lax.dot_general(wh, xl, _DN_T, preferred_element_type=F32) +
             lax.dot_general(wl, xh, _DN_T, preferred_element_type=F32))
    sc_ref[...] = jax.nn.sigmoid(logit)


def _xattn(x1, kv, wq, wo, g, b, router_w, S, tq):
    T = x1.shape[0]
    wqb, wob = wq.astype(BF16), wo.astype(BF16)
    rwt = router_w.T
    rh = rwt.astype(BF16)
    rw = jnp.stack([rh, (rwt - rh.astype(F32)).astype(BF16)])
    g2, b2 = g.reshape(1, -1), b.reshape(1, -1)
    full = lambda a: pl.BlockSpec(a.shape, lambda i: (0,) * a.ndim)
    per = S // tq
    return pl.pallas_call(
        _xattn_kernel, grid=(T // tq,),
        in_specs=[pl.BlockSpec((tq, D_MODEL), lambda i: (i, 0)),
                  pl.BlockSpec((MEM_LEN, 2 * D_MODEL), lambda i: (i // per, 0)),
                  full(wqb), full(wob), full(g2), full(b2), full(rw)],
        out_specs=(pl.BlockSpec((tq, D_MODEL), lambda i: (i, 0)),
                   pl.BlockSpec((tq, D_MODEL // 2), lambda i: (i, 0)),
                   pl.BlockSpec((MOE_E, tq), lambda i: (0, i))),
        out_shape=(jax.ShapeDtypeStruct((T, D_MODEL), F32), jax.ShapeDtypeStruct((T, D_MODEL // 2), I32),
                   jax.ShapeDtypeStruct((MOE_E, T), F32)),
        compiler_params=_params(("parallel",)), name="xattn")(x1, kv, wqb, wob, g2, b2, rw)


def _route_kernel(sc_ref, rb_ref, idx_ref, w_ref):
    E, G = MOE_E, MOE_GROUPS
    per = E // G
    scores = sc_ref[...]
    tr = scores.shape[1]
    biased = scores + rb_ref[...]
    g3 = biased.reshape(G, per, tr)
    j3 = lax.broadcasted_iota(I32, (G, per, tr), 1)
    m1 = jnp.max(g3, axis=1, keepdims=True)
    first = jnp.min(jnp.where(g3 == m1, j3, per), axis=1, keepdims=True)
    m2 = jnp.max(jnp.where(j3 == first, -jnp.inf, g3), axis=1, keepdims=True)
    gs = (m1 + m2).reshape(G, tr)
    gi = lax.broadcasted_iota(I32, (G, tr), 0)
    cnt = jnp.zeros((G, tr), F32)
    for m in range(G):
        row = gs[m:m + 1, :]
        cnt = cnt + jnp.where((row > gs) | ((row == gs) & (gi > m)), 1.0, 0.0)
    gmask = cnt < float(MOE_TOPK_GROUPS)
    masked = jnp.where(gmask[:, None, :], g3, NEG).reshape(E, tr)
    ei = lax.broadcasted_iota(I32, (E, tr), 0)
    idxs, ws = [], []
    for _ in range(MOE_K):
        mx = jnp.max(masked, axis=0, keepdims=True)
        ix = jnp.min(jnp.where(masked == mx, ei, E), axis=0, keepdims=True)
        hit = ei == ix
        ws.append(jnp.sum(jnp.where(hit, scores, 0.0), axis=0, keepdims=True))
        idxs.append(ix)
        masked = jnp.where(hit, -jnp.inf, masked)
    w = jnp.concatenate(ws, axis=0)
    idx_ref[...] = jnp.concatenate(idxs, axis=0)
    w_ref[...] = w / jnp.sum(w, axis=0, keepdims=True) * MOE_ROUTE_SCALE


def _route(scores_t, router_bias, tr):
    E, T = scores_t.shape
    rb = router_bias.reshape(E, 1)
    return pl.pallas_call(
        _route_kernel, grid=(T // tr,),
        in_specs=[pl.BlockSpec((E, tr), lambda i: (0, i)), pl.BlockSpec((E, 1), lambda i: (0, 0))],
        out_specs=(pl.BlockSpec((MOE_K, tr), lambda i: (0, i)), pl.BlockSpec((MOE_K, tr), lambda i: (0, i))),
        out_shape=(jax.ShapeDtypeStruct((MOE_K, T), I32), jax.ShapeDtypeStruct((MOE_K, T), F32)),
        compiler_params=_params(("parallel",)), name="route")(scores_t, rb)


def _rank_kernel(idx_ref, u_ref, rank_ref, cnt_ref, carry):
    E = MOE_E

    @pl.when(pl.program_id(0) == 0)
    def _():
        carry[...] = jnp.zeros_like(carry)

    idx = idx_ref[...]
    tp = idx.shape[1]
    ei = lax.broadcasted_iota(I32, (E, tp), 0)
    hits = [ei == idx[k:k + 1, :] for k in range(MOE_K)]
    onehot = jnp.zeros((E, tp), F32)
    for hit in hits:
        onehot = onehot + jnp.where(hit, 1.0, 0.0)
    pos = jnp.dot(onehot.astype(BF16), u_ref[...], preferred_element_type=F32) + carry[...]
    ranks = [jnp.sum(jnp.where(hit, pos, 0.0), axis=0, keepdims=True) for hit in hits]
    rank_ref[...] = jnp.concatenate(ranks, axis=0).astype(I32)
    total = carry[...] + jnp.sum(onehot, axis=1, keepdims=True)
    carry[...] = total
    cnt_ref[...] = jnp.broadcast_to(total, cnt_ref.shape).astype(I32)


def _rank(idx_t, tp):
    K, T = idx_t.shape
    u = jnp.triu(jnp.ones((tp, tp), F32), k=1).astype(BF16)
    rank, cnt = pl.pallas_call(
        _rank_kernel, grid=(T // tp,),
        in_specs=[pl.BlockSpec((K, tp), lambda i: (0, i)), pl.BlockSpec((tp, tp), lambda i: (0, 0))],
        out_specs=(pl.BlockSpec((K, tp), lambda i: (0, i)), pl.BlockSpec((MOE_E, LANES), lambda i: (0, 0))),
        out_shape=(jax.ShapeDtypeStruct((K, T), I32), jax.ShapeDtypeStruct((MOE_E, LANES), I32)),
        scratch_shapes=[pltpu.VMEM((MOE_E, 1), F32)],
        compiler_params=_params(("arbitrary",)), name="rank")(idx_t, u)
    return rank, cnt[:, 0]


def _dest_kernel(idx_ref, rank_ref, po_ref, dest_ref):
    idx = idx_ref[...]
    tp = idx.shape[1]
    ei = lax.broadcasted_iota(I32, (MOE_E, tp), 0)
    po = po_ref[...]
    base = [jnp.sum(jnp.where(ei == idx[k:k + 1, :], po, 0.0), axis=0, keepdims=True) for k in range(MOE_K)]
    dest_ref[...] = jnp.concatenate(base, axis=0).astype(I32) + rank_ref[...]


def _dest(idx_t, rank_t, poffs, tp):
    K, T = idx_t.shape
    po = poffs.astype(F32).reshape(MOE_E, 1)
    spec = pl.BlockSpec((K, tp), lambda i: (0, i))
    return pl.pallas_call(
        _dest_kernel, grid=(T // tp,),
        in_specs=[spec, spec, pl.BlockSpec((MOE_E, 1), lambda i: (0, 0))],
        out_specs=spec, out_shape=jax.ShapeDtypeStruct((K, T), I32),
        compiler_params=_params(("parallel",)), name="dest")(idx_t, rank_t, po)


def _pack_bf16_pairs(v):
    m = v.shape[1] // 2
    bits = lax.bitcast_convert_type(v.astype(BF16).astype(F32), jnp.uint32)
    return lax.bitcast_convert_type((bits[:, :m] >> 16) | (bits[:, m:] & jnp.uint32(0xFFFF0000)), I32)


def _unpack_bf16_pairs(w):
    w = lax.bitcast_convert_type(w, jnp.uint32)
    lo = lax.bitcast_convert_type(w << 16, F32)
    hi = lax.bitcast_convert_type(w & jnp.uint32(0xFFFF0000), F32)
    return lo, hi


def _row_scatter(rows, dest_t, n_rows):
    T, d = rows.shape
    K = dest_t.shape[0]
    nw = SC_CORES * SC_SUBCORES
    per_w = T // nw
    ch = SC_GATHER_ROWS
    mesh = plsc.VectorSubcoreMesh(core_axis_name="c", subcore_axis_name="s")

    @functools.partial(
        pl.kernel, mesh=mesh, out_type=jax.ShapeDtypeStruct((n_rows, d), I32),
        scratch_types=[pltpu.VMEM((K, ch), I32), pltpu.VMEM((ch, d), I32), pltpu.SemaphoreType.DMA],
        name="row_scatter")
    def scatter(rows_hbm, dest_hbm, out_hbm, idx_v, rows_v, sem):
        wid = lax.axis_index("s") * SC_CORES + lax.axis_index("c")
        base = wid * per_w

        @pl.loop(0, per_w // ch)
        def _(i):
            off = pl.multiple_of(base + i * ch, ch)
            pltpu.sync_copy(rows_hbm.at[pl.ds(off, ch)], rows_v)
            pltpu.sync_copy(dest_hbm.at[:, pl.ds(off, ch)], idx_v)
            copies = [pltpu.async_copy(rows_v, out_hbm.at[idx_v.at[k]], sem) for k in range(K)]
            for cp in copies:
                cp.wait()

    return scatter(rows, dest_t)


RING_AHEAD = 3
RING_OUT = 3


def _expert_kernel(po_ref, cnt_ref, tot_ref, xs_hbm, w1_ref, w3_ref, w2_ref, ys_hbm,
                   xbuf, ybuf, w1b, w3b, w2b, insem, outsem):
    e = pl.program_id(0)
    n = cnt_ref[e]
    rb = xbuf.shape[1]
    ns = xbuf.shape[0]
    hw = D_MODEL // 2
    nblk = jnp.maximum((n + rb - 1) // rb, 1)
    g0 = po_ref[e] // rb
    total = tot_ref[0]
    w1b[...] = w1_ref[...].astype(BF16)
    w3b[...] = w3_ref[...].astype(BF16)
    w2b[...] = w2_ref[...].astype(BF16)

    def in_copy(g, slot):
        return pltpu.make_async_copy(xs_hbm.at[pl.ds(pl.multiple_of(g * rb, rb), rb)], xbuf.at[slot], insem.at[slot])

    def out_copy(g, slot):
        return pltpu.make_async_copy(ybuf.at[slot], ys_hbm.at[pl.ds(pl.multiple_of(g * rb, rb), rb)],
                                     outsem.at[slot])

    @pl.when(e == 0)
    def _():
        for d in range(RING_AHEAD):
            @pl.when(d < total)
            def _():
                in_copy(d, d).start()

    def body(j, c):
        g = g0 + j
        slot = g % ns
        oslot = g % RING_OUT

        @pl.when(g + RING_AHEAD < total)
        def _():
            in_copy(g + RING_AHEAD, (g + RING_AHEAD) % ns).start()

        in_copy(0, slot).wait()

        @pl.when(g >= RING_OUT)
        def _():
            out_copy(0, oslot).wait()

        words = xbuf[slot]
        row = j * rb + lax.broadcasted_iota(I32, words.shape, 0)
        lo, hi = _unpack_bf16_pairs(jnp.where(row < n, words, 0))
        lo, hi = lo.astype(BF16), hi.astype(BF16)
        a = (jnp.dot(lo, w1b[0:hw, :], preferred_element_type=F32) +
             jnp.dot(hi, w1b[hw:, :], preferred_element_type=F32))
        u = (jnp.dot(lo, w3b[0:hw, :], preferred_element_type=F32) +
             jnp.dot(hi, w3b[hw:, :], preferred_element_type=F32))
        h = (a * jax.nn.sigmoid(a) * u).astype(BF16)
        ybuf[oslot] = _pack_bf16_pairs(jnp.dot(h, w2b[...], preferred_element_type=F32))
        out_copy(g, oslot).start()
        return c

    lax.fori_loop(0, nblk, body, 0)

    @pl.when(e + 1 == pl.num_programs(0))
    def _():
        for i in range(RING_OUT):
            @pl.when(total > i)
            def _():
                out_copy(0, (total - 1 - i) % RING_OUT).wait()


def _experts(poffs, counts, total_blocks, xs, w1, w3, w2):
    n_rows, wp = xs.shape
    rb = ROW_BLOCK
    wspec = lambda shape: pl.BlockSpec((None,) + shape, lambda e, po, cn, tb: (e, 0, 0))
    gs = pltpu.PrefetchScalarGridSpec(
        num_scalar_prefetch=3, grid=(MOE_E,),
        in_specs=[pl.BlockSpec(memory_space=pl.ANY), wspec((D_MODEL, MOE_FF)), wspec((D_MODEL, MOE_FF)),
                  wspec((MOE_FF, D_MODEL))],
        out_specs=pl.BlockSpec(memory_space=pl.ANY),
        scratch_shapes=[pltpu.VMEM((RING_AHEAD + 1, rb, wp), I32), pltpu.VMEM((RING_OUT, rb, wp), I32),
                        pltpu.VMEM((D_MODEL, MOE_FF), BF16), pltpu.VMEM((D_MODEL, MOE_FF), BF16),
                        pltpu.VMEM((MOE_FF, D_MODEL), BF16),
                        pltpu.SemaphoreType.DMA((RING_AHEAD + 1,)), pltpu.SemaphoreType.DMA((RING_OUT,))])
    return pl.pallas_call(
        _expert_kernel, grid_spec=gs, out_shape=jax.ShapeDtypeStruct((n_rows, wp), I32),
        compiler_params=_params(("arbitrary",)), name="experts")(poffs, counts, total_blocks, xs, w1, w3, w2)


def _row_gather(table, idx):
    n, d = idx.shape[0], table.shape[1]
    nw = SC_CORES * SC_SUBCORES
    per_w = n // nw
    ch = SC_GATHER_ROWS // 2
    mesh = plsc.VectorSubcoreMesh(core_axis_name="c", subcore_axis_name="s")

    @functools.partial(
        pl.kernel, mesh=mesh, out_type=jax.ShapeDtypeStruct((n, d), I32),
        scratch_types=[pltpu.VMEM((ch,), I32), pltpu.VMEM((ch,), I32),
                       pltpu.VMEM((ch, d), I32), pltpu.VMEM((ch, d), I32),
                       pltpu.SemaphoreType.DMA, pltpu.SemaphoreType.DMA,
                       pltpu.SemaphoreType.DMA, pltpu.SemaphoreType.DMA],
        name="row_gather")
    def gather(table_hbm, idx_hbm, out_hbm, idx0, idx1, rows0, rows1, g0, g1, w0, w1):
        wid = lax.axis_index("s") * SC_CORES + lax.axis_index("c")
        base = wid * per_w

        @pl.loop(0, per_w // (2 * ch))
        def _(i):
            off0 = pl.multiple_of(base + 2 * i * ch, ch)
            off1 = pl.multiple_of(off0 + ch, ch)
            pltpu.sync_copy(idx_hbm.at[pl.ds(off0, ch)], idx0)
            c0 = pltpu.async_copy(table_hbm.at[idx0], rows0, g0)
            pltpu.sync_copy(idx_hbm.at[pl.ds(off1, ch)], idx1)
            c1 = pltpu.async_copy(table_hbm.at[idx1], rows1, g1)
            c0.wait()
            o0 = pltpu.async_copy(rows0, out_hbm.at[pl.ds(off0, ch)], w0)
            c1.wait()
            o1 = pltpu.async_copy(rows1, out_hbm.at[pl.ds(off1, ch)], w1)
            o0.wait()
            o1.wait()

    return gather(table, idx)


def _combine_kernel(w_ref, x_ref, ysg_ref, s1_ref, s3_ref, s2_ref, g_ref, b_ref, o_ref):
    x2 = x_ref[...]
    xb = x2.astype(BF16)
    a = jnp.dot(xb, s1_ref[...], preferred_element_type=F32)
    c = jnp.dot(xb, s3_ref[...], preferred_element_type=F32)
    shared = jnp.dot((a * jax.nn.sigmoid(a) * c).astype(BF16), s2_ref[...], preferred_element_type=F32)
    w = w_ref[...]
    hw = ysg_ref.shape[2]
    y_lo = jnp.zeros((x2.shape[0], hw), F32)
    y_hi = jnp.zeros((x2.shape[0], hw), F32)
    for k in range(MOE_K):
        lo, hi = _unpack_bf16_pairs(ysg_ref[k])
        y_lo = y_lo + w[:, k:k + 1] * lo
        y_hi = y_hi + w[:, k:k + 1] * hi
    y = shared + jnp.concatenate([y_lo, y_hi], axis=1)
    o_ref[...] = _ln_rows(ALPHA * x2 + y, g_ref[...], b_ref[...])


def _combine(w_nat, x2, ysg, sw1, sw3, sw2, g, b, tc):
    T = x2.shape[0]
    wp = ysg.shape[2]
    full = lambda a: pl.BlockSpec(a.shape, lambda i: (0,) * a.ndim)
    s1, s3, s2 = sw1.astype(BF16), sw3.astype(BF16), sw2.astype(BF16)
    g2, b2 = g.reshape(1, -1), b.reshape(1, -1)
    return pl.pallas_call(
        _combine_kernel, grid=(T // tc,),
        in_specs=[pl.BlockSpec((tc, MOE_K), lambda i: (i, 0)),
                  pl.BlockSpec((tc, D_MODEL), lambda i: (i, 0)),
                  pl.BlockSpec((MOE_K, tc, wp), lambda i: (0, i, 0)),
                  full(s1), full(s3), full(s2), full(g2), full(b2)],
        out_specs=pl.BlockSpec((tc, D_MODEL), lambda i: (i, 0)),
        out_shape=jax.ShapeDtypeStruct((T, D_MODEL), F32),
        compiler_params=_params(("parallel",)), name="combine",
    )(w_nat, x2, ysg, s1, s3, s2, g2, b2)


def _moe(x2, x2p, scores_t, router_bias, w1, w3, w2, sw1, sw3, sw2, g, b, tiles):
    T = x2.shape[0]
    rb = ROW_BLOCK
    idx_t, w_t = _route(scores_t, router_bias, tiles['route'])
    rank_t, counts = _rank(idx_t, tiles['rank'])
    pcounts = jnp.maximum((counts + rb - 1) // rb, 1) * rb
    pends = jnp.cumsum(pcounts)
    poffs = (pends - pcounts).astype(I32)
    n_blocks = -(-T * MOE_K // rb) + MOE_E
    dest_t = _dest(idx_t, rank_t, poffs, tiles['rank'])
    xs = _row_scatter(x2p, dest_t, n_blocks * rb)
    total_blocks = (pends[-1:] // rb).astype(I32)
    ys = _experts(poffs, counts.astype(I32), total_blocks, xs, w1, w3, w2)
    ysg = _row_gather(ys, dest_t.reshape(-1)).reshape(MOE_K, T, -1)
    return _combine(w_t.T, x2, ysg, sw1, sw3, sw2, g, b, tiles['combine'])


def _tiles(B, S):
    T = B * S
    pick = lambda want, n: want if n % want == 0 else n
    return dict(proj=pick(512, T), nsa_q=pick(128, S), nsa_ck=pick(512, S), outproj=pick(1024, T),
                mlstm_nb=2 if B % 2 == 0 else 1, xattn=pick(1024, S), route=pick(1024, T), rank=pick(512, T),
                combine=pick(512, T))


def kernel(x, mem, w_in, nsa_pos_k, nsa_cmp_k_w1, nsa_cmp_k_w2, nsa_pos_v, nsa_cmp_v_w1, nsa_cmp_v_w2,
           mlstm_conv_w, mlstm_i_bias, mlstm_f_bias, mlstm_norm_g, w_out, ln1_g, ln1_b,
           xa_wq, xa_wk, xa_wv, xa_wo, ln2_g, ln2_b, router_w, router_bias,
           moe_w1, moe_w3, moe_w2, shared_w1, shared_w3, shared_w2, ln3_g, ln3_b):
    B, S, D = x.shape
    T = B * S
    tl = _tiles(B, S)
    xc = x.reshape(T, D)
    memc = mem.reshape(B * MEM_LEN, D)
    for l in range(w_in.shape[0]):
        (q, cmp, ksel, vsel, kwin, vwin, gates, mq, mk, mv, mo, mif) = _project(xc, _prep_w_in(w_in[l]), tl['proj'])
        kcvc = _compress(cmp, B, S, _prep_cmp(nsa_pos_k[l], nsa_cmp_k_w1[l], nsa_cmp_k_w2[l],
                                              nsa_pos_v[l], nsa_cmp_v_w1[l], nsa_cmp_v_w2[l]))
        y_nsa = _nsa(q, kcvc, ksel, vsel, kwin, vwin, gates, B, S, tl['nsa_q'], tl['nsa_ck'])
        y_ml = _mlstm(mq, mk, mv, mo, mif, mlstm_conv_w[l], mlstm_i_bias[l], mlstm_f_bias[l],
                      mlstm_norm_g[l], B, S, tl['mlstm_nb'])
        x1 = _outproj(y_nsa, y_ml, xc, w_out[l], ln1_g[l], ln1_b[l], tl['outproj'])
        kv = _memkv(memc, xa_wk[l], xa_wv[l])
        x2, x2p, scores_t = _xattn(x1, kv, xa_wq[l], xa_wo[l], ln2_g[l], ln2_b[l], router_w[l], S, tl['xattn'])
        xc = _moe(x2, x2p, scores_t, router_bias[l], moe_w1[l], moe_w3[l], moe_w2[l],
                  shared_w1[l], shared_w3[l], shared_w2[l], ln3_g[l], ln3_b[l], tl)
    return xc.reshape(B, S, D)
```

```python
import functools
import numpy as np
import jax
import jax.numpy as jnp
from jax import lax
from jax.experimental import pallas as pl
from jax.experimental.pallas import tpu as pltpu
from jax.experimental.pallas import tpu_sc as plsc

F32 = jnp.float32
BF16 = jnp.bfloat16
I32 = jnp.int32

D_MODEL = 1024
MEM_LEN = 256
NSA_HEADS = 8
NSA_GROUPS = 2
NSA_HPG = 4
NSA_DK = 64
NSA_CMP_LEN = 32
NSA_CMP_STRIDE = 16
NSA_SEL_BLOCK = 64
NSA_SEL_TOPN = 8
NSA_WINDOW = 512
ML_HEADS = 4
ML_DH = 128
ML_CHUNK = 64
ML_CONV = 4
XA_HEADS = 4
XA_DH = 256
MOE_E = 256
MOE_K = 8
MOE_GROUPS = 8
MOE_TOPK_GROUPS = 4
MOE_FF = 256
MOE_ROUTE_SCALE = 2.5
DEPTH = 1
ALPHA = (2.0 * DEPTH) ** 0.25
LN_EPS = 1e-5
NEG = -1e30
FORCE_BONUS = 1e4

LANES = 128
ROW_BLOCK = 512
VMEM_LIMIT = 56 * 1024 * 1024
SC_CORES = 2
SC_SUBCORES = 16
SC_GATHER_ROWS = 128

_DN_T = (((1,), (1,)), ((), ()))
_DN_TA = (((0,), (0,)), ((), ()))


def _params(sem):
    return pltpu.CompilerParams(dimension_semantics=sem, vmem_limit_bytes=VMEM_LIMIT)


def _ln_rows(v, g, b):
    mu = jnp.mean(v, axis=-1, keepdims=True)
    d = v - mu
    var = jnp.mean(d * d, axis=-1, keepdims=True)
    return d * lax.rsqrt(var + LN_EPS) * g + b


_SEGS = (('q', 512, BF16), ('cmp', 256, F32), ('ksel', 128, BF16), ('vsel', 128, BF16),
         ('kwin', 128, BF16), ('vwin', 128, BF16), ('gates', 128, F32), ('mq', 512, BF16),
         ('mk', 512, BF16), ('mv', 512, BF16), ('mo', 512, BF16), ('mif', 128, F32))


def _proj_kernel(x_ref, w_ref, *out_refs):
    xb = x_ref[...].astype(BF16)
    off = 0
    for o_ref, (_, wd, _) in zip(out_refs, _SEGS):
        o_ref[...] = jnp.dot(xb, w_ref[:, off:off + wd], preferred_element_type=F32).astype(o_ref.dtype)
        off += wd


def _prep_w_in(w):
    sizes = (512,) + (128,) * 6 + (24,) + (512,) * 4 + (4, 4)
    cuts = np.cumsum(sizes)[:-1].tolist()
    (wq, kc, vc, ks, vs, kw, vw, wg, mq, mk, mv, mo, mi, mf) = jnp.split(w, cuts, axis=1)
    wq = wq.reshape(D_MODEL, NSA_GROUPS, NSA_HPG, NSA_DK).transpose(0, 2, 1, 3).reshape(D_MODEL, 512)
    pad = lambda a: jnp.pad(a, ((0, 0), (0, LANES - a.shape[1])))
    segs = [wq, kc, vc, ks, vs, kw, vw, pad(wg), mq, mk, mv, mo, pad(jnp.concatenate([mi, mf], axis=1))]
    return jnp.concatenate(segs, axis=1).astype(BF16)


def _project(x2d, w_all, tm):
    T = x2d.shape[0]
    n = w_all.shape[1]
    out_shape = tuple(jax.ShapeDtypeStruct((T, wd), dt) for _, wd, dt in _SEGS)
    out_specs = tuple(pl.BlockSpec((tm, wd), lambda i: (i, 0)) for _, wd, _ in _SEGS)
    return pl.pallas_call(
        _proj_kernel, grid=(T // tm,),
        in_specs=[pl.BlockSpec((tm, D_MODEL), lambda i: (i, 0)),
                  pl.BlockSpec((D_MODEL, n), lambda i: (0, 0))],
        out_specs=out_specs, out_shape=out_shape,
        compiler_params=_params(("parallel",)), name="proj")(x2d, w_all)


def _cmp_kernel(r_ref, pa_ref, pb_ref, wa_ref, wb_ref, w2_ref, o_ref):
    r = r_ref[...]
    a = jnp.dot((r + pa_ref[...]).astype(BF16), wa_ref[...], preferred_element_type=F32)
    b = jnp.dot((r + pb_ref[...]).astype(BF16), wb_ref[...], preferred_element_type=F32)
    nr = r.shape[0]
    hid = a + pltpu.roll(b, nr - 1, 0)
    hid = hid * jax.nn.sigmoid(hid)
    out = jnp.dot(hid.astype(BF16), w2_ref[...], preferred_element_type=F32)
    row = lax.broadcasted_iota(I32, out.shape, 0)
    o_ref[...] = jnp.where(row < nr - 1, out, 0.0).astype(o_ref.dtype)


def _prep_cmp(pos_k, w1_k, w2_k, pos_v, w1_v, w2_v):
    eye = jnp.eye(NSA_GROUPS, dtype=F32)

    def expand_w1(w1, half):
        w = w1.reshape(NSA_CMP_LEN, NSA_DK, NSA_DK)[half * 16:(half + 1) * 16]
        return jnp.einsum('jde,gk->jgdke', w, eye).reshape(16, 128, 128)

    def both(fk, fv):
        z = jnp.zeros_like(fk)
        top = jnp.concatenate([fk, z], axis=-1)
        bot = jnp.concatenate([z, fv], axis=-1)
        return jnp.concatenate([top, bot], axis=-2)

    wa = both(expand_w1(w1_k, 0), expand_w1(w1_v, 0)).reshape(16 * 256, 256).astype(BF16)
    wb = both(expand_w1(w1_k, 1), expand_w1(w1_v, 1)).reshape(16 * 256, 256).astype(BF16)
    w2 = both(jnp.kron(eye, w2_k), jnp.kron(eye, w2_v)).astype(BF16)

    def pos_row(half):
        pk = jnp.tile(pos_k[half * 16:(half + 1) * 16], (1, NSA_GROUPS))
        pv = jnp.tile(pos_v[half * 16:(half + 1) * 16], (1, NSA_GROUPS))
        return jnp.concatenate([pk, pv], axis=1).reshape(1, 16 * 256)

    return pos_row(0), pos_row(1), wa, wb, w2


def _compress(cmp2d, B, S, prep):
    pa, pb, wa, wb, w2 = prep
    nr = S // NSA_CMP_STRIDE
    r = cmp2d.reshape(B, nr, NSA_CMP_STRIDE * 256)
    full = lambda a: pl.BlockSpec(a.shape, lambda b: (0,) * a.ndim)
    return pl.pallas_call(
        _cmp_kernel, grid=(B,),
        in_specs=[pl.BlockSpec((None, nr, NSA_CMP_STRIDE * 256), lambda b: (b, 0, 0)),
                  full(pa), full(pb), full(wa), full(wb), full(w2)],
        out_specs=pl.BlockSpec((None, nr, 256), lambda b: (b, 0, 0)),
        out_shape=jax.ShapeDtypeStruct((B, nr, 256), BF16),
        compiler_params=_params(("parallel",)), name="cmp")(r, pa, pb, wa, wb, w2)


def _nsa_consts(S):
    n_cmp = (S - NSA_CMP_LEN) // NSA_CMP_STRIDE + 1
    n_sel = S // NSA_SEL_BLOCK
    cs = np.arange(n_cmp) * NSA_CMP_STRIDE
    ss = np.arange(n_sel) * NSA_SEL_BLOCK
    ov = ((cs[:, None] < ss[None, :] + NSA_SEL_BLOCK) & (cs[:, None] + NSA_CMP_LEN > ss[None, :]))
    ovt = np.zeros((LANES, S // NSA_CMP_STRIDE), np.float32)
    ovt[:n_sel, :n_cmp] = ov.T
    e = np.zeros((LANES, S), np.float32)
    e[np.arange(S) // NSA_SEL_BLOCK, np.arange(S)] = 1.0
    return jnp.asarray(ovt, BF16), jnp.asarray(e, BF16)


def _nsa_kernel(q_ref, kcvc_ref, ksel_ref, vsel_ref, kwin_ref, vwin_ref, gates_ref, ovt_ref, e_ref,
                y_ref, bias_ref, *, tq, ck, n_sel):
    G, H = NSA_GROUPS, NSA_HPG
    GH = G * H
    M = GH * tq
    S = ksel_ref.shape[0]
    W = NSA_WINDOW
    ws = min(W + tq, S)
    t0 = pl.program_id(1) * tq
    gates = jax.nn.sigmoid(gates_ref[...])
    lane = lax.broadcasted_iota(I32, (tq, LANES), 1)
    t_col = t0 + lax.broadcasted_iota(I32, (tq, 1), 0)
    kc = kcvc_ref[:, 0:LANES]
    vc = kcvc_ref[:, LANES:2 * LANES]
    nc = kc.shape[0]
    qs = []
    for g in range(G):
        gmask = (lane // NSA_DK) == g
        for h in range(H):
            qh = q_ref[:, h * LANES:(h + 1) * LANES] * (NSA_DK ** -0.5)
            qs.append(jnp.where(gmask, qh, jnp.zeros_like(qh)))
    Q = jnp.concatenate(qs, axis=0).astype(BF16)

    s = lax.dot_general(Q, kc, _DN_T, preferred_element_type=F32)
    c_idx = lax.broadcasted_iota(I32, (tq, nc), 1)
    cmask = (c_idx * NSA_CMP_STRIDE + NSA_CMP_LEN - 1) <= t_col
    s3 = jnp.where(cmask[None], s.reshape(GH, tq, nc), NEG)
    p = jnp.exp(s3 - jnp.max(s3, axis=-1, keepdims=True))
    p = p / jnp.sum(p, axis=-1, keepdims=True)
    p = jnp.where(cmask[None], p, 0.0)
    o_cmp = jnp.dot(p.reshape(M, nc).astype(BF16), vc, preferred_element_type=F32).reshape(GH, tq, LANES)

    for g in range(G):
        psum = jnp.sum(p[g * H:(g + 1) * H], axis=0)
        hi = psum.astype(BF16)
        lo = (psum - hi.astype(F32)).astype(BF16)
        ovt = ovt_ref[...]
        pslt = (lax.dot_general(ovt, hi, _DN_T, preferred_element_type=F32) +
                lax.dot_general(ovt, lo, _DN_T, preferred_element_type=F32))
        imp_p = pslt[0:n_sel, :]
        n_i = lax.broadcasted_iota(I32, (n_sel, tq), 0)
        cur = (t0 + lax.broadcasted_iota(I32, (n_sel, tq), 1)) // NSA_SEL_BLOCK
        forced = (n_i == 0) | (n_i == cur) | (n_i == cur - 1)
        imp = jnp.where(n_i <= cur, imp_p + jnp.where(forced, FORCE_BONUS, 0.0), NEG)
        cnt = jnp.zeros((n_sel, tq), F32)
        for m in range(n_sel):
            row = imp[m:m + 1, :]
            beats = (row > imp) | ((row == imp) & (n_i > m))
            cnt = cnt + jnp.where(beats, 1.0, 0.0)
        selt = jnp.where(cnt < float(min(NSA_SEL_TOPN, n_sel)), 1.0, 0.0)
        selt = jnp.concatenate([selt, jnp.zeros((LANES - n_sel, tq), F32)], axis=0)
        sel = selt.T.astype(BF16)
        maskf = jnp.dot(sel, e_ref[...], preferred_element_type=F32)
        kpos = lax.broadcasted_iota(I32, (tq, S), 1)
        bias_ref[g] = jnp.where((maskf > 0.5) & (kpos <= t_col), 0.0, NEG)

    vlane = lax.broadcasted_iota(I32, (1, LANES), 1) // NSA_DK

    def pv_with_sums(pb, v):
        outs = []
        for g in range(G):
            vg = jnp.where(vlane == g, v, jnp.ones_like(v))
            outs.append(jnp.dot(pb[g * H * tq:(g + 1) * H * tq], vg, preferred_element_type=F32))
        return jnp.concatenate(outs, axis=0)

    def normalise(acc):
        outs = []
        for r in range(GH):
            c = NSA_DK * (1 - r // H)
            outs.append(acc[r] / acc[r][:, c:c + 1])
        return outs

    def sel_body(j, carry):
        m_i, acc = carry
        ks = pl.multiple_of(j * ck, ck)
        k = ksel_ref[pl.ds(ks, ck), :]
        v = vsel_ref[pl.ds(ks, ck), :]
        sj = lax.dot_general(Q, k, _DN_T, preferred_element_type=F32)
        sj = (sj.reshape(G, H, tq, ck) + bias_ref[:, :, pl.ds(ks, ck)][:, None]).reshape(GH, tq, ck)
        m_new = jnp.maximum(m_i, jnp.max(sj, axis=-1, keepdims=True))
        a = jnp.exp(m_i - m_new)
        pj = jnp.exp((sj - m_new).astype(BF16)).reshape(M, ck)
        return m_new, a * acc + pv_with_sums(pj, v).reshape(GH, tq, LANES)

    init = (jnp.full((GH, tq, 1), NEG, F32), jnp.zeros((GH, tq, LANES), F32))
    _, acc = lax.fori_loop(0, (t0 + tq + ck - 1) // ck, sel_body, init)
    o_sel = normalise(acc)

    kst = pl.multiple_of(jnp.clip(t0 - W, 0, S - ws), LANES)
    kwn = kwin_ref[pl.ds(kst, ws), :]
    vwn = vwin_ref[pl.ds(kst, ws), :]
    sw = lax.dot_general(Q, kwn, _DN_T, preferred_element_type=F32)
    wpos = kst + lax.broadcasted_iota(I32, (tq, ws), 1)
    wmask = (wpos <= t_col) & (wpos > t_col - W)
    sw3 = jnp.where(wmask[None], sw.reshape(GH, tq, ws), NEG)
    pw = jnp.exp((sw3 - jnp.max(sw3, axis=-1, keepdims=True)).astype(BF16)).reshape(M, ws)
    o_win = normalise(pv_with_sums(pw, vwn).reshape(GH, tq, LANES))

    g0mask = lane < NSA_DK
    for h in range(H):
        o_g = []
        for g in range(G):
            r = g * H + h
            c0 = r * 3
            o_g.append(gates[:, c0:c0 + 1] * o_cmp[r] + gates[:, c0 + 1:c0 + 2] * o_sel[r] +
                       gates[:, c0 + 2:c0 + 3] * o_win[r])
        y_ref[:, h * LANES:(h + 1) * LANES] = jnp.where(g0mask, o_g[0], o_g[1]).astype(y_ref.dtype)


def _nsa(q, kcvc, ksel, vsel, kwin, vwin, gates, B, S, tq, ck):
    T = B * S
    nq = S // tq
    ovt, e = _nsa_consts(S)
    seq = lambda a: a.reshape(B, S, LANES)
    kv_spec = pl.BlockSpec((None, S, LANES), lambda b, i: (b, 0, 0))
    kern = functools.partial(_nsa_kernel, tq=tq, ck=ck, n_sel=S // NSA_SEL_BLOCK)
    return pl.pallas_call(
        kern, grid=(B, nq),
        in_specs=[pl.BlockSpec((tq, 512), lambda b, i: (b * nq + i, 0)),
                  pl.BlockSpec((None,) + kcvc.shape[1:], lambda b, i: (b, 0, 0)),
                  kv_spec, kv_spec, kv_spec, kv_spec,
                  pl.BlockSpec((tq, LANES), lambda b, i: (b * nq + i, 0)),
                  pl.BlockSpec(ovt.shape, lambda b, i: (0, 0)),
                  pl.BlockSpec(e.shape, lambda b, i: (0, 0))],
        out_specs=pl.BlockSpec((tq, 512), lambda b, i: (b * nq + i, 0)),
        out_shape=jax.ShapeDtypeStruct((T, 512), BF16),
        scratch_shapes=[pltpu.VMEM((NSA_GROUPS, tq, S), F32)],
        compiler_params=_params(("parallel", "parallel")), name="nsa",
    )(q, kcvc, seq(ksel), seq(vsel), seq(kwin), seq(vwin), gates, ovt, e)


def _mlstm_kernel(q_ref, k_ref, v_ref, o_ref, gn_ref, gt_ref, cw_ref, bn_ref, bt_ref, ng_ref, tri_ref,
                  y_ref, c_scr, n_scr):
    H, dh, L = ML_HEADS, ML_DH, ML_CHUNK
    nb, S = q_ref.shape[0], q_ref.shape[1]
    nchunk = S // L
    c_scr[...] = jnp.zeros_like(c_scr)
    n_scr[...] = jnp.zeros_like(n_scr)
    row = lax.broadcasted_iota(I32, (L, H * dh), 0)
    li = lax.broadcasted_iota(I32, (L, L), 0)
    mi = lax.broadcasted_iota(I32, (L, L), 1)
    causal = mi <= li
    tril = tri_ref[0]
    triu = tri_ref[1]
    hp = lax.Precision.HIGHEST

    def conv_silu(ref, bi, c, wofs):
        r0 = pl.multiple_of(c * L, L)
        rp = pl.multiple_of(jnp.maximum(c - 1, 0) * L, L)
        cur = ref[bi, pl.ds(r0, L), :].astype(F32)
        prev = ref[bi, pl.ds(rp, L), :].astype(F32) * jnp.where(c > 0, 1.0, 0.0)
        acc = cur * cw_ref[ML_CONV - 1:ML_CONV, wofs:wofs + H * dh]
        for j in range(1, ML_CONV):
            sh = jnp.where(row < j, pltpu.roll(prev, j, 0), pltpu.roll(cur, j, 0))
            acc = acc + sh * cw_ref[ML_CONV - 1 - j:ML_CONV - j, wofs:wofs + H * dh]
        return acc * jax.nn.sigmoid(acc)

    def body(c, m_state):
        r0 = pl.multiple_of(c * L, L)
        new_m = []
        for bi in range(nb):
            qa = conv_silu(q_ref, bi, c, 0) * (dh ** -0.5)
            ka = conv_silu(k_ref, bi, c, H * dh)
            va = v_ref[bi, pl.ds(r0, L), :]
            oa = o_ref[bi, pl.ds(r0, L), :].astype(F32)
            gn = gn_ref[bi, pl.ds(r0, L), :] + bn_ref[...]
            gt = gt_ref[bi, :, c, :] + bt_ref[...]
            lf_n = jax.nn.log_sigmoid(gn)
            lf_t = jax.nn.log_sigmoid(gt)
            b_n = jnp.dot(tril, lf_n, precision=hp, preferred_element_type=F32)
            b_t = jnp.dot(lf_t, triu, precision=hp, preferred_element_type=F32)
            for h in range(H):
                st = bi * H + h
                q = qa[:, h * dh:(h + 1) * dh]
                k = ka[:, h * dh:(h + 1) * dh]
                v = va[:, h * dh:(h + 1) * dh]
                m_old = m_state[st]
                b_col = b_n[:, H + h:H + h + 1]
                i_col = gn[:, h:h + 1]
                b_row = b_t[H + h:H + h + 1, :]
                i_row = gt[h:h + 1, :]
                g_tot = b_t[H + h:H + h + 1, L - 1:L]
                d_log = jnp.where(causal, b_col - b_row + i_row, NEG)
                inter = b_col + m_old
                m_q = jnp.maximum(inter, jnp.max(d_log, axis=-1, keepdims=True))
                w_intra = jnp.exp(d_log - m_q)
                w_inter = jnp.exp(inter - m_q)
                qb = q.astype(BF16)
                s = lax.dot_general(qb, k.astype(BF16), _DN_T, preferred_element_type=F32) * w_intra
                cst = c_scr[st]
                nst = n_scr[st]
                num = (w_inter * jnp.dot(qb, cst.astype(BF16), preferred_element_type=F32) +
                       jnp.dot(s.astype(BF16), v, preferred_element_type=F32))
                den = w_inter * jnp.sum(q * nst, axis=-1, keepdims=True) + jnp.sum(s, axis=-1, keepdims=True)
                hv = num / jnp.maximum(jnp.abs(den), jnp.exp(-m_q))
                log_k = g_tot - b_col + i_col
                m_new = jnp.maximum(g_tot + m_old, jnp.max(log_k, axis=0, keepdims=True))
                wk = jnp.exp(log_k - m_new)
                decay = jnp.exp(g_tot + m_old - m_new)
                kw = k * wk
                c_scr[st] = decay * cst + lax.dot_general(kw.astype(BF16), v, _DN_TA, preferred_element_type=F32)
                n_scr[st] = decay * nst + jnp.sum(kw, axis=0, keepdims=True)
                new_m.append(m_new)
                mu = jnp.mean(hv, axis=-1, keepdims=True)
                dv = hv - mu
                var = jnp.mean(dv * dv, axis=-1, keepdims=True)
                hn = dv * lax.rsqrt(var + LN_EPS) * ng_ref[:, h * dh:(h + 1) * dh]
                og = jax.nn.sigmoid(oa[:, h * dh:(h + 1) * dh])
                y_ref[bi, pl.ds(r0, L), h * dh:(h + 1) * dh] = (og * hn).astype(y_ref.dtype)
        return tuple(new_m)

    lax.fori_loop(0, nchunk, body, tuple(jnp.zeros((1, 1), F32) for _ in range(nb * H)))


def _mlstm(mq, mk, mv, mo, mif, conv_w, i_bias, f_bias, norm_g, B, S, nb):
    T = B * S
    H, dh, L = ML_HEADS, ML_DH, ML_CHUNK
    W = H * dh
    gt = mif[:, :2 * H].reshape(B, S, 2 * H).transpose(0, 2, 1).reshape(B, 2 * H, S // L, L)
    cw = conv_w.reshape(ML_CONV, 2 * W)
    bias = jnp.concatenate([i_bias, f_bias])
    bn = jnp.pad(bias, (0, LANES - 2 * H)).reshape(1, LANES)
    bt = bias.reshape(2 * H, 1)
    ng = norm_g.reshape(1, W)
    tri = jnp.stack([jnp.tril(jnp.ones((L, L), F32)), jnp.triu(jnp.ones((L, L), F32))])
    seq = lambda a: a.reshape(B, S, a.shape[1])
    rows = lambda w: pl.BlockSpec((nb, S, w), lambda b: (b, 0, 0))
    full = lambda a: pl.BlockSpec(a.shape, lambda b: (0,) * a.ndim)
    y = pl.pallas_call(
        _mlstm_kernel, grid=(B // nb,),
        in_specs=[rows(W), rows(W), rows(W), rows(W), rows(LANES),
                  pl.BlockSpec((nb, 2 * H, S // L, L), lambda b: (b, 0, 0, 0)),
                  full(cw), full(bn), full(bt), full(ng), full(tri)],
        out_specs=rows(W),
        out_shape=jax.ShapeDtypeStruct((B, S, W), BF16),
        scratch_shapes=[pltpu.VMEM((nb * H, dh, dh), F32), pltpu.VMEM((nb * H, 1, dh), F32)],
        compiler_params=_params(("parallel",)), name="mlstm",
    )(seq(mq), seq(mk), seq(mv), seq(mo), seq(mif), gt, cw, bn, bt, ng, tri)
    return y.reshape(T, W)


def _outproj_kernel(yn_ref, ym_ref, x_ref, w_ref, g_ref, b_ref, o_ref):
    mix = (jnp.dot(yn_ref[...], w_ref[0:512, :], preferred_element_type=F32) +
           jnp.dot(ym_ref[...], w_ref[512:1024, :], preferred_element_type=F32))
    o_ref[...] = _ln_rows(ALPHA * x_ref[...] + mix, g_ref[...], b_ref[...])


def _outproj(y_nsa, y_ml, x2d, w_out, g, b, tm):
    T = x2d.shape[0]
    wn = w_out[:512].reshape(NSA_GROUPS, NSA_HPG, NSA_DK, D_MODEL).transpose(1, 0, 2, 3).reshape(512, D_MODEL)
    w = jnp.concatenate([wn, w_out[512:]], axis=0).astype(BF16)
    row = lambda wd: pl.BlockSpec((tm, wd), lambda i: (i, 0))
    full = lambda a: pl.BlockSpec(a.shape, lambda i: (0,) * a.ndim)
    g2, b2 = g.reshape(1, -1), b.reshape(1, -1)
    return pl.pallas_call(
        _outproj_kernel, grid=(T // tm,),
        in_specs=[row(512), row(512), row(D_MODEL), full(w), full(g2), full(b2)],
        out_specs=row(D_MODEL), out_shape=jax.ShapeDtypeStruct((T, D_MODEL), F32),
        compiler_params=_params(("parallel",)), name="outproj")(y_nsa, y_ml, x2d, w, g2, b2)


def _memkv_kernel(m_ref, w_ref, o_ref):
    o_ref[...] = jnp.dot(m_ref[...].astype(BF16), w_ref[...], preferred_element_type=F32).astype(o_ref.dtype)


def _memkv(mem2d, wk, wv):
    w = jnp.concatenate([wk, wv], axis=1).astype(BF16)
    R = mem2d.shape[0]
    return pl.pallas_call(
        _memkv_kernel, grid=(R // MEM_LEN,),
        in_specs=[pl.BlockSpec((MEM_LEN, D_MODEL), lambda i: (i, 0)),
                  pl.BlockSpec(w.shape, lambda i: (0, 0))],
        out_specs=pl.BlockSpec((MEM_LEN, 2 * D_MODEL), lambda i: (i, 0)),
        out_shape=jax.ShapeDtypeStruct((R, 2 * D_MODEL), BF16),
        compiler_params=_params(("parallel",)), name="memkv")(mem2d, w)


def _xattn_kernel(x_ref, kv_ref, wq_ref, wo_ref, g_ref, b_ref, rw_ref, x2_ref, x2p_ref, sc_ref):
    x1 = x_ref[...]
    q = jnp.dot(x1.astype(BF16), wq_ref[...], preferred_element_type=F32).astype(BF16)
    outs = []
    for h in range(XA_HEADS):
        qh = q[:, h * XA_DH:(h + 1) * XA_DH]
        kh = kv_ref[:, h * XA_DH:(h + 1) * XA_DH]
        vh = kv_ref[:, D_MODEL + h * XA_DH:D_MODEL + (h + 1) * XA_DH]
        s = lax.dot_general(qh, kh, _DN_T, preferred_element_type=F32) * (XA_DH ** -0.5)
        p = jnp.exp(s - jnp.max(s, axis=-1, keepdims=True))
        p = p / jnp.sum(p, axis=-1, keepdims=True)
        outs.append(jnp.dot(p.astype(BF16), vh, preferred_element_type=F32).astype(BF16))
    o = jnp.concatenate(outs, axis=1)
    xa = jnp.dot(o, wo_ref[...], preferred_element_type=F32)
    x2 = _ln_rows(ALPHA * x1 + xa, g_ref[...], b_ref[...])
    x2_ref[...] = x2
    x2p_ref[...] = _pack_bf16_pairs(x2)
    xh = x2.astype(BF16)
    xl = (x2 - xh.astype(F32)).astype(BF16)
    wh = rw_ref[0]
    wl = rw_ref[1]
    logit = (lax.dot_general(wh, xh, _DN_T, preferred_element_type=F32) +
             lax.dot_general(wh, xl, _DN_T, preferred_element_type=F32) +
             lax.dot_general(wl, xh, _DN_T, preferred_element_type=F32))
    sc_ref[...] = jax.nn.sigmoid(logit)


def _xattn(x1, kv, wq, wo, g, b, router_w, S, tq):
    T = x1.shape[0]
    wqb, wob = wq.astype(BF16), wo.astype(BF16)
    rwt = router_w.T
    rh = rwt.astype(BF16)
    rw = jnp.stack([rh, (rwt - rh.astype(F32)).astype(BF16)])
    g2, b2 = g.reshape(1, -1), b.reshape(1, -1)
    full = lambda a: pl.BlockSpec(a.shape, lambda i: (0,) * a.ndim)
    per = S // tq
    return pl.pallas_call(
        _xattn_kernel, grid=(T // tq,),
        in_specs=[pl.BlockSpec((tq, D_MODEL), lambda i: (i, 0)),
                  pl.BlockSpec((MEM_LEN, 2 * D_MODEL), lambda i: (i // per, 0)),
                  full(wqb), full(wob), full(g2), full(b2), full(rw)],
        out_specs=(pl.BlockSpec((tq, D_MODEL), lambda i: (i, 0)),
                   pl.BlockSpec((tq, D_MODEL // 2), lambda i: (i, 0)),
                   pl.BlockSpec((MOE_E, tq), lambda i: (0, i))),
        out_shape=(jax.ShapeDtypeStruct((T, D_MODEL), F32), jax.ShapeDtypeStruct((T, D_MODEL // 2), I32),
                   jax.ShapeDtypeStruct((MOE_E, T), F32)),
        compiler_params=_params(("parallel",)), name="xattn")(x1, kv, wqb, wob, g2, b2, rw)


def _route_kernel(sc_ref, rb_ref, idx_ref, w_ref):
    E, G = MOE_E, MOE_GROUPS
    per = E // G
    scores = sc_ref[...]
    tr = scores.shape[1]
    biased = scores + rb_ref[...]
    g3 = biased.reshape(G, per, tr)
    j3 = lax.broadcasted_iota(I32, (G, per, tr), 1)
    m1 = jnp.max(g3, axis=1, keepdims=True)
    first = jnp.min(jnp.where(g3 == m1, j3, per), axis=1, keepdims=True)
    m2 = jnp.max(jnp.where(j3 == first, -jnp.inf, g3), axis=1, keepdims=True)
    gs = (m1 + m2).reshape(G, tr)
    gi = lax.broadcasted_iota(I32, (G, tr), 0)
    cnt = jnp.zeros((G, tr), F32)
    for m in range(G):
        row = gs[m:m + 1, :]
        cnt = cnt + jnp.where((row > gs) | ((row == gs) & (gi > m)), 1.0, 0.0)
    gmask = cnt < float(MOE_TOPK_GROUPS)
    masked = jnp.where(gmask[:, None, :], g3, NEG).reshape(E, tr)
    ei = lax.broadcasted_iota(I32, (E, tr), 0)
    idxs, ws = [], []
    for _ in range(MOE_K):
        mx = jnp.max(masked, axis=0, keepdims=True)
        ix = jnp.min(jnp.where(masked == mx, ei, E), axis=0, keepdims=True)
        hit = ei == ix
        ws.append(jnp.sum(jnp.where(hit, scores, 0.0), axis=0, keepdims=True))
        idxs.append(ix)
        masked = jnp.where(hit, -jnp.inf, masked)
    w = jnp.concatenate(ws, axis=0)
    idx_ref[...] = jnp.concatenate(idxs, axis=0)
    w_ref[...] = w / jnp.sum(w, axis=0, keepdims=True) * MOE_ROUTE_SCALE


def _route(scores_t, router_bias, tr):
    E, T = scores_t.shape
    rb = router_bias.reshape(E, 1)
    return pl.pallas_call(
        _route_kernel, grid=(T // tr,),
        in_specs=[pl.BlockSpec((E, tr), lambda i: (0, i)), pl.BlockSpec((E, 1), lambda i: (0, 0))],
        out_specs=(pl.BlockSpec((MOE_K, tr), lambda i: (0, i)), pl.BlockSpec((MOE_K, tr), lambda i: (0, i))),
        out_shape=(jax.ShapeDtypeStruct((MOE_K, T), I32), jax.ShapeDtypeStruct((MOE_K, T), F32)),
        compiler_params=_params(("parallel",)), name="route")(scores_t, rb)


def _rank_kernel(idx_ref, u_ref, rank_ref, cnt_ref, carry):
    E = MOE_E

    @pl.when(pl.program_id(0) == 0)
    def _():
        carry[...] = jnp.zeros_like(carry)

    idx = idx_ref[...]
    tp = idx.shape[1]
    ei = lax.broadcasted_iota(I32, (E, tp), 0)
    hits = [ei == idx[k:k + 1, :] for k in range(MOE_K)]
    onehot = jnp.zeros((E, tp), F32)
    for hit in hits:
        onehot = onehot + jnp.where(hit, 1.0, 0.0)
    pos = jnp.dot(onehot.astype(BF16), u_ref[...], preferred_element_type=F32) + carry[...]
    ranks = [jnp.sum(jnp.where(hit, pos, 0.0), axis=0, keepdims=True) for hit in hits]
    rank_ref[...] = jnp.concatenate(ranks, axis=0).astype(I32)
    total = carry[...] + jnp.sum(onehot, axis=1, keepdims=True)
    carry[...] = total
    cnt_ref[...] = jnp.broadcast_to(total, cnt_ref.shape).astype(I32)


def _rank(idx_t, tp):
    K, T = idx_t.shape
    u = jnp.triu(jnp.ones((tp, tp), F32), k=1).astype(BF16)
    rank, cnt = pl.pallas_call(
        _rank_kernel, grid=(T // tp,),
        in_specs=[pl.BlockSpec((K, tp), lambda i: (0, i)), pl.BlockSpec((tp, tp), lambda i: (0, 0))],
        out_specs=(pl.BlockSpec((K, tp), lambda i: (0, i)), pl.BlockSpec((MOE_E, LANES), lambda i: (0, 0))),
        out_shape=(jax.ShapeDtypeStruct((K, T), I32), jax.ShapeDtypeStruct((MOE_E, LANES), I32)),
        scratch_shapes=[pltpu.VMEM((MOE_E, 1), F32)],
        compiler_params=_params(("arbitrary",)), name="rank")(idx_t, u)
    return rank, cnt[:, 0]


def _dest_kernel(idx_ref, rank_ref, po_ref, dest_ref):
    idx = idx_ref[...]
    tp = idx.shape[1]
    ei = lax.broadcasted_iota(I32, (MOE_E, tp), 0)
    po = po_ref[...]
    base = [jnp.sum(jnp.where(ei == idx[k:k + 1, :], po, 0.0), axis=0, keepdims=True) for k in range(MOE_K)]
    dest_ref[...] = jnp.concatenate(base, axis=0).astype(I32) + rank_ref[...]


def _dest(idx_t, rank_t, poffs, tp):
    K, T = idx_t.shape
    po = poffs.astype(F32).reshape(MOE_E, 1)
    spec = pl.BlockSpec((K, tp), lambda i: (0, i))
    return pl.pallas_call(
        _dest_kernel, grid=(T // tp,),
        in_specs=[spec, spec, pl.BlockSpec((MOE_E, 1), lambda i: (0, 0))],
        out_specs=spec, out_shape=jax.ShapeDtypeStruct((K, T), I32),
        compiler_params=_params(("parallel",)), name="dest")(idx_t, rank_t, po)


def _pack_bf16_pairs(v):
    m = v.shape[1] // 2
    bits = lax.bitcast_convert_type(v.astype(BF16).astype(F32), jnp.uint32)
    return lax.bitcast_convert_type((bits[:, :m] >> 16) | (bits[:, m:] & jnp.uint32(0xFFFF0000)), I32)


def _unpack_bf16_pairs(w):
    w = lax.bitcast_convert_type(w, jnp.uint32)
    lo = lax.bitcast_convert_type(w << 16, F32)
    hi = lax.bitcast_convert_type(w & jnp.uint32(0xFFFF0000), F32)
    return lo, hi


def _row_scatter(rows, dest_t, n_rows):
    T, d = rows.shape
    K = dest_t.shape[0]
    nw = SC_CORES * SC_SUBCORES
    per_w = T // nw
    ch = SC_GATHER_ROWS
    mesh = plsc.VectorSubcoreMesh(core_axis_name="c", subcore_axis_name="s")

    @functools.partial(
        pl.kernel, mesh=mesh, out_type=jax.ShapeDtypeStruct((n_rows, d), I32),
        scratch_types=[pltpu.VMEM((K, ch), I32), pltpu.VMEM((ch, d), I32), pltpu.SemaphoreType.DMA],
        name="row_scatter")
    def scatter(rows_hbm, dest_hbm, out_hbm, idx_v, rows_v, sem):
        wid = lax.axis_index("s") * SC_CORES + lax.axis_index("c")
        base = wid * per_w

        @pl.loop(0, per_w // ch)
        def _(i):
            off = pl.multiple_of(base + i * ch, ch)
            pltpu.sync_copy(rows_hbm.at[pl.ds(off, ch)], rows_v)
            pltpu.sync_copy(dest_hbm.at[:, pl.ds(off, ch)], idx_v)
            copies = [pltpu.async_copy(rows_v, out_hbm.at[idx_v.at[k]], sem) for k in range(K)]
            for cp in copies:
                cp.wait()

    return scatter(rows, dest_t)


RING_AHEAD = 3
RING_OUT = 3


def _expert_kernel(po_ref, cnt_ref, tot_ref, xs_hbm, w1_ref, w3_ref, w2_ref, ys_hbm,
                   xbuf, ybuf, w1b, w3b, w2b, insem, outsem):
    e = pl.program_id(0)
    n = cnt_ref[e]
    rb = xbuf.shape[1]
    ns = xbuf.shape[0]
    hw = D_MODEL // 2
    nblk = jnp.maximum((n + rb - 1) // rb, 1)
    g0 = po_ref[e] // rb
    total = tot_ref[0]
    w1b[...] = w1_ref[...].astype(BF16)
    w3b[...] = w3_ref[...].astype(BF16)
    w2b[...] = w2_ref[...].astype(BF16)

    def in_copy(g, slot):
        return pltpu.make_async_copy(xs_hbm.at[pl.ds(pl.multiple_of(g * rb, rb), rb)], xbuf.at[slot], insem.at[slot])

    def out_copy(g, slot):
        return pltpu.make_async_copy(ybuf.at[slot], ys_hbm.at[pl.ds(pl.multiple_of(g * rb, rb), rb)],
                                     outsem.at[slot])

    @pl.when(e == 0)
    def _():
        for d in range(RING_AHEAD):
            @pl.when(d < total)
            def _():
                in_copy(d, d).start()

    def body(j, c):
        g = g0 + j
        slot = g % ns
        oslot = g % RING_OUT

        @pl.when(g + RING_AHEAD < total)
        def _():
            in_copy(g + RING_AHEAD, (g + RING_AHEAD) % ns).start()

        in_copy(0, slot).wait()

        @pl.when(g >= RING_OUT)
        def _():
            out_copy(0, oslot).wait()

        words = xbuf[slot]
        row = j * rb + lax.broadcasted_iota(I32, words.shape, 0)
        lo, hi = _unpack_bf16_pairs(jnp.where(row < n, words, 0))
        lo, hi = lo.astype(BF16), hi.astype(BF16)
        a = (jnp.dot(lo, w1b[0:hw, :], preferred_element_type=F32) +
             jnp.dot(hi, w1b[hw:, :], preferred_element_type=F32))
        u = (jnp.dot(lo, w3b[0:hw, :], preferred_element_type=F32) +
             jnp.dot(hi, w3b[hw:, :], preferred_element_type=F32))
        h = (a * jax.nn.sigmoid(a) * u).astype(BF16)
        ybuf[oslot] = _pack_bf16_pairs(jnp.dot(h, w2b[...], preferred_element_type=F32))
        out_copy(g, oslot).start()
        return c

    lax.fori_loop(0, nblk, body, 0)

    @pl.when(e + 1 == pl.num_programs(0))
    def _():
        for i in range(RING_OUT):
            @pl.when(total > i)
            def _():
                out_copy(0, (total - 1 - i) % RING_OUT).wait()


def _experts(poffs, counts, total_blocks, xs, w1, w3, w2):
    n_rows, wp = xs.shape
    rb = ROW_BLOCK
    wspec = lambda shape: pl.BlockSpec((None,) + shape, lambda e, po, cn, tb: (e, 0, 0))
    gs = pltpu.PrefetchScalarGridSpec(
        num_scalar_prefetch=3, grid=(MOE_E,),
        in_specs=[pl.BlockSpec(memory_space=pl.ANY), wspec((D_MODEL, MOE_FF)), wspec((D_MODEL, MOE_FF)),
                  wspec((MOE_FF, D_MODEL))],
        out_specs=pl.BlockSpec(memory_space=pl.ANY),
        scratch_shapes=[pltpu.VMEM((RING_AHEAD + 1, rb, wp), I32), pltpu.VMEM((RING_OUT, rb, wp), I32),
                        pltpu.VMEM((D_MODEL, MOE_FF), BF16), pltpu.VMEM((D_MODEL, MOE_FF), BF16),
                        pltpu.VMEM((MOE_FF, D_MODEL), BF16),
                        pltpu.SemaphoreType.DMA((RING_AHEAD + 1,)), pltpu.SemaphoreType.DMA((RING_OUT,))])
    return pl.pallas_call(
        _expert_kernel, grid_spec=gs, out_shape=jax.ShapeDtypeStruct((n_rows, wp), I32),
        compiler_params=_params(("arbitrary",)), name="experts")(poffs, counts, total_blocks, xs, w1, w3, w2)


def _row_gather(table, idx):
    n, d = idx.shape[0], table.shape[1]
    nw = SC_CORES * SC_SUBCORES
    per_w = n // nw
    ch = SC_GATHER_ROWS // 2
    mesh = plsc.VectorSubcoreMesh(core_axis_name="c", subcore_axis_name="s")

    @functools.partial(
        pl.kernel, mesh=mesh, out_type=jax.ShapeDtypeStruct((n, d), I32),
        scratch_types=[pltpu.VMEM((ch,), I32), pltpu.VMEM((ch,), I32),
                       pltpu.VMEM((ch, d), I32), pltpu.VMEM((ch, d), I32),
                       pltpu.SemaphoreType.DMA, pltpu.SemaphoreType.DMA,
                       pltpu.SemaphoreType.DMA, pltpu.SemaphoreType.DMA],
        name="row_gather")
    def gather(table_hbm, idx_hbm, out_hbm, idx0, idx1, rows0, rows1, g0, g1, w0, w1):
        wid = lax.axis_index("s") * SC_CORES + lax.axis_index("c")
        base = wid * per_w

        @pl.loop(0, per_w // (2 * ch))
        def _(i):
            off0 = pl.multiple_of(base + 2 * i * ch, ch)
            off1 = pl.multiple_of(off0 + ch, ch)
            pltpu.sync_copy(idx_hbm.at[pl.ds(off0, ch)], idx0)
            c0 = pltpu.async_copy(table_hbm.at[idx0], rows0, g0)
            pltpu.sync_copy(idx_hbm.at[pl.ds(off1, ch)], idx1)
            c1 = pltpu.async_copy(table_hbm.at[idx1], rows1, g1)
            c0.wait()
            o0 = pltpu.async_copy(rows0, out_hbm.at[pl.ds(off0, ch)], w0)
            c1.wait()
            o1 = pltpu.async_copy(rows1, out_hbm.at[pl.ds(off1, ch)], w1)
            o0.wait()
            o1.wait()

    return gather(table, idx)


def _combine_kernel(w_ref, x_ref, ysg_ref, s1_ref, s3_ref, s2_ref, g_ref, b_ref, *rest):
    o_ref = rest[-1]
    x2 = x_ref[...]
    xb = x2.astype(BF16)
    a = jnp.dot(xb, s1_ref[...], preferred_element_type=F32)
    c = jnp.dot(xb, s3_ref[...], preferred_element_type=F32)
    shared = jnp.dot((a * jax.nn.sigmoid(a) * c).astype(BF16), s2_ref[...], preferred_element_type=F32)
    w = w_ref[...]
    hw = ysg_ref.shape[2]
    y_lo = jnp.zeros((x2.shape[0], hw), F32)
    y_hi = jnp.zeros((x2.shape[0], hw), F32)
    for k in range(MOE_K):
        lo, hi = _unpack_bf16_pairs(ysg_ref[k])
        y_lo = y_lo + w[:, k:k + 1] * lo
        y_hi = y_hi + w[:, k:k + 1] * hi
    y = shared + jnp.concatenate([y_lo, y_hi], axis=1)
    o_ref[...] = _ln_rows(ALPHA * x2 + y, g_ref[...], b_ref[...])


def _combine(w_nat, x2, ysg, prev, first_tile, sw1, sw3, sw2, g, b, tc):
    T = x2.shape[0]
    wp = ysg.shape[2]
    full = lambda a: pl.BlockSpec(a.shape, lambda i: (0,) * a.ndim)
    row = lambda wd: pl.BlockSpec((tc, wd), lambda i: (i + first_tile, 0))
    s1, s3, s2 = sw1.astype(BF16), sw3.astype(BF16), sw2.astype(BF16)
    g2, b2 = g.reshape(1, -1), b.reshape(1, -1)
    args = [w_nat, x2, ysg, s1, s3, s2, g2, b2]
    in_specs = [row(MOE_K), row(D_MODEL), pl.BlockSpec((MOE_K, tc, wp), lambda i: (0, i, 0)),
                full(s1), full(s3), full(s2), full(g2), full(b2)]
    aliases = {}
    if prev is not None:
        aliases = {len(args): 0}
        args.append(prev)
        in_specs.append(pl.BlockSpec(memory_space=pl.ANY))
    return pl.pallas_call(
        _combine_kernel, grid=(ysg.shape[1] // tc,), in_specs=in_specs,
        out_specs=row(D_MODEL), out_shape=jax.ShapeDtypeStruct((T, D_MODEL), F32),
        input_output_aliases=aliases,
        compiler_params=_params(("parallel",)), name="combine",
    )(*args)


def _moe(x2, x2p, scores_t, router_bias, w1, w3, w2, sw1, sw3, sw2, g, b, tiles):
    T = x2.shape[0]
    rb = ROW_BLOCK
    idx_t, w_t = _route(scores_t, router_bias, tiles['route'])
    rank_t, counts = _rank(idx_t, tiles['rank'])
    pcounts = jnp.maximum((counts + rb - 1) // rb, 1) * rb
    pends = jnp.cumsum(pcounts)
    poffs = (pends - pcounts).astype(I32)
    n_blocks = -(-T * MOE_K // rb) + MOE_E
    dest_t = _dest(idx_t, rank_t, poffs, tiles['rank'])
    xs = _row_scatter(x2p, dest_t, n_blocks * rb)
    total_blocks = (pends[-1:] // rb).astype(I32)
    ys = _experts(poffs, counts.astype(I32), total_blocks, xs, w1, w3, w2)
    tc = tiles['combine']
    n_split = 2 if T % (2 * tc) == 0 and (T // 2 * MOE_K) % (SC_CORES * SC_SUBCORES * SC_GATHER_ROWS) == 0 else 1
    th = T // n_split
    w_nat = w_t.T
    out = None
    for p in range(n_split):
        ysg = _row_gather(ys, dest_t[:, p * th:(p + 1) * th].reshape(-1)).reshape(MOE_K, th, -1)
        out = _combine(w_nat, x2, ysg, out, p * th // tc, sw1, sw3, sw2, g, b, tc)
    return out


def _tiles(B, S):
    T = B * S
    pick = lambda want, n: want if n % want == 0 else n
    return dict(proj=pick(512, T), nsa_q=pick(128, S), nsa_ck=pick(512, S), outproj=pick(1024, T),
                mlstm_nb=2 if B % 2 == 0 else 1, xattn=pick(1024, S), route=pick(1024, T), rank=pick(512, T),
                combine=pick(512, T))


def kernel(x, mem, w_in, nsa_pos_k, nsa_cmp_k_w1, nsa_cmp_k_w2, nsa_pos_v, nsa_cmp_v_w1, nsa_cmp_v_w2,
           mlstm_conv_w, mlstm_i_bias, mlstm_f_bias, mlstm_norm_g, w_out, ln1_g, ln1_b,
           xa_wq, xa_wk, xa_wv, xa_wo, ln2_g, ln2_b, router_w, router_bias,
           moe_w1, moe_w3, moe_w2, shared_w1, shared_w3, shared_w2, ln3_g, ln3_b):
    B, S, D = x.shape
    T = B * S
    tl = _tiles(B, S)
    xc = x.reshape(T, D)
    memc = mem.reshape(B * MEM_LEN, D)
    for l in range(w_in.shape[0]):
        (q, cmp, ksel, vsel, kwin, vwin, gates, mq, mk, mv, mo, mif) = _project(xc, _prep_w_in(w_in[l]), tl['proj'])
        kcvc = _compress(cmp, B, S, _prep_cmp(nsa_pos_k[l], nsa_cmp_k_w1[l], nsa_cmp_k_w2[l],
                                              nsa_pos_v[l], nsa_cmp_v_w1[l], nsa_cmp_v_w2[l]))
        y_nsa = _nsa(q, kcvc, ksel, vsel, kwin, vwin, gates, B, S, tl['nsa_q'], tl['nsa_ck'])
        y_ml = _mlstm(mq, mk, mv, mo, mif, mlstm_conv_w[l], mlstm_i_bias[l], mlstm_f_bias[l],
                      mlstm_norm_g[l], B, S, tl['mlstm_nb'])
        x1 = _outproj(y_nsa, y_ml, xc, w_out[l], ln1_g[l], ln1_b[l], tl['outproj'])
        kv = _memkv(memc, xa_wk[l], xa_wv[l])
        x2, x2p, scores_t = _xattn(x1, kv, xa_wq[l], xa_wo[l], ln2_g[l], ln2_b[l], router_w[l], S, tl['xattn'])
        xc = _moe(x2, x2p, scores_t, router_bias[l], moe_w1[l], moe_w3[l], moe_w2[l],
                  shared_w1[l], shared_w3[l], shared_w2[l], ln3_g[l], ln3_b[l], tl)
    return xc.reshape(B, S, D)
```

```python
import functools
import numpy as np
import jax
import jax.numpy as jnp
from jax import lax
from jax.experimental import pallas as pl
from jax.experimental.pallas import tpu as pltpu
from jax.experimental.pallas import tpu_sc as plsc

F32 = jnp.float32
BF16 = jnp.bfloat16
I32 = jnp.int32

D_MODEL = 1024
MEM_LEN = 256
NSA_HEADS = 8
NSA_GROUPS = 2
NSA_HPG = 4
NSA_DK = 64
NSA_CMP_LEN = 32
NSA_CMP_STRIDE = 16
NSA_SEL_BLOCK = 64
NSA_SEL_TOPN = 8
NSA_WINDOW = 512
ML_HEADS = 4
ML_DH = 128
ML_CHUNK = 64
ML_CONV = 4
XA_HEADS = 4
XA_DH = 256
MOE_E = 256
MOE_K = 8
MOE_GROUPS = 8
MOE_TOPK_GROUPS = 4
MOE_FF = 256
MOE_ROUTE_SCALE = 2.5
DEPTH = 1
ALPHA = (2.0 * DEPTH) ** 0.25
LN_EPS = 1e-5
NEG = -1e30
FORCE_BONUS = 1e4

LANES = 128
ROW_BLOCK = 512
VMEM_LIMIT = 56 * 1024 * 1024
SC_CORES = 2
SC_SUBCORES = 16
SC_GATHER_ROWS = 128

_DN_T = (((1,), (1,)), ((), ()))
_DN_TA = (((0,), (0,)), ((), ()))


def _params(sem):
    return pltpu.CompilerParams(dimension_semantics=sem, vmem_limit_bytes=VMEM_LIMIT)


def _ln_rows(v, g, b):
    mu = jnp.mean(v, axis=-1, keepdims=True)
    d = v - mu
    var = jnp.mean(d * d, axis=-1, keepdims=True)
    return d * lax.rsqrt(var + LN_EPS) * g + b


_SEGS = (('q', 512, BF16), ('cmp', 256, F32), ('ksel', 128, BF16), ('vsel', 128, BF16),
         ('kwin', 128, BF16), ('vwin', 128, BF16), ('gates', 128, F32), ('mq', 512, BF16),
         ('mk', 512, BF16), ('mv', 512, BF16), ('mo', 512, BF16), ('mif', 128, F32))


def _proj_kernel(x_ref, w_ref, *out_refs):
    xb = x_ref[...].astype(BF16)
    off = 0
    for o_ref, (_, wd, _) in zip(out_refs, _SEGS):
        o_ref[...] = jnp.dot(xb, w_ref[:, off:off + wd], preferred_element_type=F32).astype(o_ref.dtype)
        off += wd


def _prep_w_in(w):
    sizes = (512,) + (128,) * 6 + (24,) + (512,) * 4 + (4, 4)
    cuts = np.cumsum(sizes)[:-1].tolist()
    (wq, kc, vc, ks, vs, kw, vw, wg, mq, mk, mv, mo, mi, mf) = jnp.split(w, cuts, axis=1)
    wq = wq.reshape(D_MODEL, NSA_GROUPS, NSA_HPG, NSA_DK).transpose(0, 2, 1, 3).reshape(D_MODEL, 512)
    pad = lambda a: jnp.pad(a, ((0, 0), (0, LANES - a.shape[1])))
    segs = [wq, kc, vc, ks, vs, kw, vw, pad(wg), mq, mk, mv, mo, pad(jnp.concatenate([mi, mf], axis=1))]
    return jnp.concatenate(segs, axis=1).astype(BF16)


def _project(x2d, w_all, tm):
    T = x2d.shape[0]
    n = w_all.shape[1]
    out_shape = tuple(jax.ShapeDtypeStruct((T, wd), dt) for _, wd, dt in _SEGS)
    out_specs = tuple(pl.BlockSpec((tm, wd), lambda i: (i, 0)) for _, wd, _ in _SEGS)
    return pl.pallas_call(
        _proj_kernel, grid=(T // tm,),
        in_specs=[pl.BlockSpec((tm, D_MODEL), lambda i: (i, 0)),
                  pl.BlockSpec((D_MODEL, n), lambda i: (0, 0))],
        out_specs=out_specs, out_shape=out_shape,
        compiler_params=_params(("parallel",)), name="proj")(x2d, w_all)


def _cmp_kernel(r_ref, pa_ref, pb_ref, wa_ref, wb_ref, w2_ref, o_ref):
    r = r_ref[...]
    a = jnp.dot((r + pa_ref[...]).astype(BF16), wa_ref[...], preferred_element_type=F32)
    b = jnp.dot((r + pb_ref[...]).astype(BF16), wb_ref[...], preferred_element_type=F32)
    nr = r.shape[0]
    hid = a + pltpu.roll(b, nr - 1, 0)
    hid = hid * jax.nn.sigmoid(hid)
    out = jnp.dot(hid.astype(BF16), w2_ref[...], preferred_element_type=F32)
    row = lax.broadcasted_iota(I32, out.shape, 0)
    o_ref[...] = jnp.where(row < nr - 1, out, 0.0).astype(o_ref.dtype)


def _prep_cmp(pos_k, w1_k, w2_k, pos_v, w1_v, w2_v):
    eye = jnp.eye(NSA_GROUPS, dtype=F32)

    def expand_w1(w1, half):
        w = w1.reshape(NSA_CMP_LEN, NSA_DK, NSA_DK)[half * 16:(half + 1) * 16]
        return jnp.einsum('jde,gk->jgdke', w, eye).reshape(16, 128, 128)

    def both(fk, fv):
        z = jnp.zeros_like(fk)
        top = jnp.concatenate([fk, z], axis=-1)
        bot = jnp.concatenate([z, fv], axis=-1)
        return jnp.concatenate([top, bot], axis=-2)

    wa = both(expand_w1(w1_k, 0), expand_w1(w1_v, 0)).reshape(16 * 256, 256).astype(BF16)
    wb = both(expand_w1(w1_k, 1), expand_w1(w1_v, 1)).reshape(16 * 256, 256).astype(BF16)
    w2 = both(jnp.kron(eye, w2_k), jnp.kron(eye, w2_v)).astype(BF16)

    def pos_row(half):
        pk = jnp.tile(pos_k[half * 16:(half + 1) * 16], (1, NSA_GROUPS))
        pv = jnp.tile(pos_v[half * 16:(half + 1) * 16], (1, NSA_GROUPS))
        return jnp.concatenate([pk, pv], axis=1).reshape(1, 16 * 256)

    return pos_row(0), pos_row(1), wa, wb, w2


def _compress(cmp2d, B, S, prep):
    pa, pb, wa, wb, w2 = prep
    nr = S // NSA_CMP_STRIDE
    r = cmp2d.reshape(B, nr, NSA_CMP_STRIDE * 256)
    full = lambda a: pl.BlockSpec(a.shape, lambda b: (0,) * a.ndim)
    return pl.pallas_call(
        _cmp_kernel, grid=(B,),
        in_specs=[pl.BlockSpec((None, nr, NSA_CMP_STRIDE * 256), lambda b: (b, 0, 0)),
                  full(pa), full(pb), full(wa), full(wb), full(w2)],
        out_specs=pl.BlockSpec((None, nr, 256), lambda b: (b, 0, 0)),
        out_shape=jax.ShapeDtypeStruct((B, nr, 256), BF16),
        compiler_params=_params(("parallel",)), name="cmp")(r, pa, pb, wa, wb, w2)


def _nsa_consts(S):
    n_cmp = (S - NSA_CMP_LEN) // NSA_CMP_STRIDE + 1
    n_sel = S // NSA_SEL_BLOCK
    cs = np.arange(n_cmp) * NSA_CMP_STRIDE
    ss = np.arange(n_sel) * NSA_SEL_BLOCK
    ov = ((cs[:, None] < ss[None, :] + NSA_SEL_BLOCK) & (cs[:, None] + NSA_CMP_LEN > ss[None, :]))
    ovt = np.zeros((LANES, S // NSA_CMP_STRIDE), np.float32)
    ovt[:n_sel, :n_cmp] = ov.T
    e = np.zeros((LANES, S), np.float32)
    e[np.arange(S) // NSA_SEL_BLOCK, np.arange(S)] = 1.0
    return jnp.asarray(ovt, BF16), jnp.asarray(e, BF16)


def _nsa_kernel(q_ref, kcvc_ref, ksel_ref, vsel_ref, kwin_ref, vwin_ref, gates_ref, ovt_ref, e_ref,
                y_ref, bias_ref, *, tq, ck, n_sel):
    G, H = NSA_GROUPS, NSA_HPG
    GH = G * H
    M = GH * tq
    S = ksel_ref.shape[0]
    W = NSA_WINDOW
    ws = min(W + tq, S)
    t0 = pl.program_id(1) * tq
    gates = jax.nn.sigmoid(gates_ref[...])
    lane = lax.broadcasted_iota(I32, (tq, LANES), 1)
    t_col = t0 + lax.broadcasted_iota(I32, (tq, 1), 0)
    kc = kcvc_ref[:, 0:LANES]
    vc = kcvc_ref[:, LANES:2 * LANES]
    nc = kc.shape[0]
    qs = []
    for g in range(G):
        gmask = (lane // NSA_DK) == g
        for h in range(H):
            qh = q_ref[:, h * LANES:(h + 1) * LANES] * (NSA_DK ** -0.5)
            qs.append(jnp.where(gmask, qh, jnp.zeros_like(qh)))
    Q = jnp.concatenate(qs, axis=0).astype(BF16)

    s = lax.dot_general(Q, kc, _DN_T, preferred_element_type=F32)
    c_idx = lax.broadcasted_iota(I32, (tq, nc), 1)
    cmask = (c_idx * NSA_CMP_STRIDE + NSA_CMP_LEN - 1) <= t_col
    s3 = jnp.where(cmask[None], s.reshape(GH, tq, nc), NEG)
    p = jnp.exp(s3 - jnp.max(s3, axis=-1, keepdims=True))
    p = p / jnp.sum(p, axis=-1, keepdims=True)
    p = jnp.where(cmask[None], p, 0.0)
    o_cmp = jnp.dot(p.reshape(M, nc).astype(BF16), vc, preferred_element_type=F32).reshape(GH, tq, LANES)

    for g in range(G):
        psum = jnp.sum(p[g * H:(g + 1) * H], axis=0)
        hi = psum.astype(BF16)
        lo = (psum - hi.astype(F32)).astype(BF16)
        ovt = ovt_ref[...]
        pslt = (lax.dot_general(ovt, hi, _DN_T, preferred_element_type=F32) +
                lax.dot_general(ovt, lo, _DN_T, preferred_element_type=F32))
        imp_p = pslt[0:n_sel, :]
        n_i = lax.broadcasted_iota(I32, (n_sel, tq), 0)
        cur = (t0 + lax.broadcasted_iota(I32, (n_sel, tq), 1)) // NSA_SEL_BLOCK
        forced = (n_i == 0) | (n_i == cur) | (n_i == cur - 1)
        imp = jnp.where(n_i <= cur, imp_p + jnp.where(forced, FORCE_BONUS, 0.0), NEG)
        cnt = jnp.zeros((n_sel, tq), F32)
        for m in range(n_sel):
            row = imp[m:m + 1, :]
            beats = (row > imp) | ((row == imp) & (n_i > m))
            cnt = cnt + jnp.where(beats, 1.0, 0.0)
        selt = jnp.where(cnt < float(min(NSA_SEL_TOPN, n_sel)), 1.0, 0.0)
        selt = jnp.concatenate([selt, jnp.zeros((LANES - n_sel, tq), F32)], axis=0)
        sel = selt.T.astype(BF16)
        maskf = jnp.dot(sel, e_ref[...], preferred_element_type=F32)
        kpos = lax.broadcasted_iota(I32, (tq, S), 1)
        bias_ref[g] = jnp.where((maskf > 0.5) & (kpos <= t_col), 0.0, NEG)

    vlane = lax.broadcasted_iota(I32, (1, LANES), 1) // NSA_DK

    def pv_with_sums(pb, v):
        outs = []
        for g in range(G):
            vg = jnp.where(vlane == g, v, jnp.ones_like(v))
            outs.append(jnp.dot(pb[g * H * tq:(g + 1) * H * tq], vg, preferred_element_type=F32))
        return jnp.concatenate(outs, axis=0)

    def normalise(acc):
        outs = []
        for r in range(GH):
            c = NSA_DK * (1 - r // H)
            outs.append(acc[r] / acc[r][:, c:c + 1])
        return outs

    def sel_body(j, carry):
        m_i, acc = carry
        ks = pl.multiple_of(j * ck, ck)
        k = ksel_ref[pl.ds(ks, ck), :]
        v = vsel_ref[pl.ds(ks, ck), :]
        sj = lax.dot_general(Q, k, _DN_T, preferred_element_type=F32)
        sj = (sj.reshape(G, H, tq, ck) + bias_ref[:, :, pl.ds(ks, ck)][:, None]).reshape(GH, tq, ck)
        m_new = jnp.maximum(m_i, jnp.max(sj, axis=-1, keepdims=True))
        a = jnp.exp(m_i - m_new)
        pj = jnp.exp((sj - m_new).astype(BF16)).reshape(M, ck)
        return m_new, a * acc + pv_with_sums(pj, v).reshape(GH, tq, LANES)

    init = (jnp.full((GH, tq, 1), NEG, F32), jnp.zeros((GH, tq, LANES), F32))
    _, acc = lax.fori_loop(0, (t0 + tq + ck - 1) // ck, sel_body, init)
    o_sel = normalise(acc)

    kst = pl.multiple_of(jnp.clip(t0 - W, 0, S - ws), LANES)
    kwn = kwin_ref[pl.ds(kst, ws), :]
    vwn = vwin_ref[pl.ds(kst, ws), :]
    sw = lax.dot_general(Q, kwn, _DN_T, preferred_element_type=F32)
    wpos = kst + lax.broadcasted_iota(I32, (tq, ws), 1)
    wmask = (wpos <= t_col) & (wpos > t_col - W)
    sw3 = jnp.where(wmask[None], sw.reshape(GH, tq, ws), NEG)
    pw = jnp.exp((sw3 - jnp.max(sw3, axis=-1, keepdims=True)).astype(BF16)).reshape(M, ws)
    o_win = normalise(pv_with_sums(pw, vwn).reshape(GH, tq, LANES))

    g0mask = lane < NSA_DK
    for h in range(H):
        o_g = []
        for g in range(G):
            r = g * H + h
            c0 = r * 3
            o_g.append(gates[:, c0:c0 + 1] * o_cmp[r] + gates[:, c0 + 1:c0 + 2] * o_sel[r] +
                       gates[:, c0 + 2:c0 + 3] * o_win[r])
        y_ref[:, h * LANES:(h + 1) * LANES] = jnp.where(g0mask, o_g[0], o_g[1]).astype(y_ref.dtype)


def _nsa(q, kcvc, ksel, vsel, kwin, vwin, gates, B, S, tq, ck):
    T = B * S
    nq = S // tq
    ovt, e = _nsa_consts(S)
    seq = lambda a: a.reshape(B, S, LANES)
    kv_spec = pl.BlockSpec((None, S, LANES), lambda b, i: (b, 0, 0))
    kern = functools.partial(_nsa_kernel, tq=tq, ck=ck, n_sel=S // NSA_SEL_BLOCK)
    return pl.pallas_call(
        kern, grid=(B, nq),
        in_specs=[pl.BlockSpec((tq, 512), lambda b, i: (b * nq + i, 0)),
                  pl.BlockSpec((None,) + kcvc.shape[1:], lambda b, i: (b, 0, 0)),
                  kv_spec, kv_spec, kv_spec, kv_spec,
                  pl.BlockSpec((tq, LANES), lambda b, i: (b * nq + i, 0)),
                  pl.BlockSpec(ovt.shape, lambda b, i: (0, 0)),
                  pl.BlockSpec(e.shape, lambda b, i: (0, 0))],
        out_specs=pl.BlockSpec((tq, 512), lambda b, i: (b * nq + i, 0)),
        out_shape=jax.ShapeDtypeStruct((T, 512), BF16),
        scratch_shapes=[pltpu.VMEM((NSA_GROUPS, tq, S), F32)],
        compiler_params=_params(("parallel", "parallel")), name="nsa",
    )(q, kcvc, seq(ksel), seq(vsel), seq(kwin), seq(vwin), gates, ovt, e)


def _mlstm_kernel(q_ref, k_ref, v_ref, o_ref, gn_ref, gt_ref, cw_ref, bn_ref, bt_ref, ng_ref, tri_ref,
                  y_ref, c_scr, n_scr):
    H, dh, L = ML_HEADS, ML_DH, ML_CHUNK
    nb, S = q_ref.shape[0], q_ref.shape[1]
    nchunk = S // L
    c_scr[...] = jnp.zeros_like(c_scr)
    n_scr[...] = jnp.zeros_like(n_scr)
    row = lax.broadcasted_iota(I32, (L, H * dh), 0)
    li = lax.broadcasted_iota(I32, (L, L), 0)
    mi = lax.broadcasted_iota(I32, (L, L), 1)
    causal = mi <= li
    tril = tri_ref[0]
    triu = tri_ref[1]
    hp = lax.Precision.HIGHEST

    def conv_silu(ref, bi, c, wofs):
        r0 = pl.multiple_of(c * L, L)
        rp = pl.multiple_of(jnp.maximum(c - 1, 0) * L, L)
        cur = ref[bi, pl.ds(r0, L), :].astype(F32)
        prev = ref[bi, pl.ds(rp, L), :].astype(F32) * jnp.where(c > 0, 1.0, 0.0)
        acc = cur * cw_ref[ML_CONV - 1:ML_CONV, wofs:wofs + H * dh]
        for j in range(1, ML_CONV):
            sh = jnp.where(row < j, pltpu.roll(prev, j, 0), pltpu.roll(cur, j, 0))
            acc = acc + sh * cw_ref[ML_CONV - 1 - j:ML_CONV - j, wofs:wofs + H * dh]
        return acc * jax.nn.sigmoid(acc)

    def body(c, m_state):
        r0 = pl.multiple_of(c * L, L)
        new_m = []
        for bi in range(nb):
            qa = conv_silu(q_ref, bi, c, 0) * (dh ** -0.5)
            ka = conv_silu(k_ref, bi, c, H * dh)
            va = v_ref[bi, pl.ds(r0, L), :]
            oa = o_ref[bi, pl.ds(r0, L), :].astype(F32)
            gn = gn_ref[bi, pl.ds(r0, L), :] + bn_ref[...]
            gt = gt_ref[bi, :, c, :] + bt_ref[...]
            lf_n = jax.nn.log_sigmoid(gn)
            lf_t = jax.nn.log_sigmoid(gt)
            b_n = jnp.dot(tril, lf_n, precision=hp, preferred_element_type=F32)
            b_t = jnp.dot(lf_t, triu, precision=hp, preferred_element_type=F32)
            for h in range(H):
                st = bi * H + h
                q = qa[:, h * dh:(h + 1) * dh]
                k = ka[:, h * dh:(h + 1) * dh]
                v = va[:, h * dh:(h + 1) * dh]
                m_old = m_state[st]
                b_col = b_n[:, H + h:H + h + 1]
                i_col = gn[:, h:h + 1]
                b_row = b_t[H + h:H + h + 1, :]
                i_row = gt[h:h + 1, :]
                g_tot = b_t[H + h:H + h + 1, L - 1:L]
                d_log = jnp.where(causal, b_col - b_row + i_row, NEG)
                inter = b_col + m_old
                m_q = jnp.maximum(inter, jnp.max(d_log, axis=-1, keepdims=True))
                w_intra = jnp.exp(d_log - m_q)
                w_inter = jnp.exp(inter - m_q)
                qb = q.astype(BF16)
                s = lax.dot_general(qb, k.astype(BF16), _DN_T, preferred_element_type=F32) * w_intra
                cst = c_scr[st]
                nst = n_scr[st]
                num = (w_inter * jnp.dot(qb, cst.astype(BF16), preferred_element_type=F32) +
                       jnp.dot(s.astype(BF16), v, preferred_element_type=F32))
                den = w_inter * jnp.sum(q * nst, axis=-1, keepdims=True) + jnp.sum(s, axis=-1, keepdims=True)
                hv = num / jnp.maximum(jnp.abs(den), jnp.exp(-m_q))
                log_k = g_tot - b_col + i_col
                m_new = jnp.maximum(g_tot + m_old, jnp.max(log_k, axis=0, keepdims=True))
                wk = jnp.exp(log_k - m_new)
                decay = jnp.exp(g_tot + m_old - m_new)
                kw = k * wk
                c_scr[st] = decay * cst + lax.dot_general(kw.astype(BF16), v, _DN_TA, preferred_element_type=F32)
                n_scr[st] = decay * nst + jnp.sum(kw, axis=0, keepdims=True)
                new_m.append(m_new)
                mu = jnp.mean(hv, axis=-1, keepdims=True)
                dv = hv - mu
                var = jnp.mean(dv * dv, axis=-1, keepdims=True)
                hn = dv * lax.rsqrt(var + LN_EPS) * ng_ref[:, h * dh:(h + 1) * dh]
                og = jax.nn.sigmoid(oa[:, h * dh:(h + 1) * dh])
                y_ref[bi, pl.ds(r0, L), h * dh:(h + 1) * dh] = (og * hn).astype(y_ref.dtype)
        return tuple(new_m)

    lax.fori_loop(0, nchunk, body, tuple(jnp.zeros((1, 1), F32) for _ in range(nb * H)))


def _mlstm(mq, mk, mv, mo, mif, conv_w, i_bias, f_bias, norm_g, B, S, nb):
    T = B * S
    H, dh, L = ML_HEADS, ML_DH, ML_CHUNK
    W = H * dh
    gt = mif[:, :2 * H].reshape(B, S, 2 * H).transpose(0, 2, 1).reshape(B, 2 * H, S // L, L)
    cw = conv_w.reshape(ML_CONV, 2 * W)
    bias = jnp.concatenate([i_bias, f_bias])
    bn = jnp.pad(bias, (0, LANES - 2 * H)).reshape(1, LANES)
    bt = bias.reshape(2 * H, 1)
    ng = norm_g.reshape(1, W)
    tri = jnp.stack([jnp.tril(jnp.ones((L, L), F32)), jnp.triu(jnp.ones((L, L), F32))])
    seq = lambda a: a.reshape(B, S, a.shape[1])
    rows = lambda w: pl.BlockSpec((nb, S, w), lambda b: (b, 0, 0))
    full = lambda a: pl.BlockSpec(a.shape, lambda b: (0,) * a.ndim)
    y = pl.pallas_call(
        _mlstm_kernel, grid=(B // nb,),
        in_specs=[rows(W), rows(W), rows(W), rows(W), rows(LANES),
                  pl.BlockSpec((nb, 2 * H, S // L, L), lambda b: (b, 0, 0, 0)),
                  full(cw), full(bn), full(bt), full(ng), full(tri)],
        out_specs=rows(W),
        out_shape=jax.ShapeDtypeStruct((B, S, W), BF16),
        scratch_shapes=[pltpu.VMEM((nb * H, dh, dh), F32), pltpu.VMEM((nb * H, 1, dh), F32)],
        compiler_params=_params(("parallel",)), name="mlstm",
    )(seq(mq), seq(mk), seq(mv), seq(mo), seq(mif), gt, cw, bn, bt, ng, tri)
    return y.reshape(T, W)


def _outproj_kernel(yn_ref, ym_ref, x_ref, w_ref, g_ref, b_ref, o_ref):
    mix = (jnp.dot(yn_ref[...], w_ref[0:512, :], preferred_element_type=F32) +
           jnp.dot(ym_ref[...], w_ref[512:1024, :], preferred_element_type=F32))
    o_ref[...] = _ln_rows(ALPHA * x_ref[...] + mix, g_ref[...], b_ref[...])


def _outproj(y_nsa, y_ml, x2d, w_out, g, b, tm):
    T = x2d.shape[0]
    wn = w_out[:512].reshape(NSA_GROUPS, NSA_HPG, NSA_DK, D_MODEL).transpose(1, 0, 2, 3).reshape(512, D_MODEL)
    w = jnp.concatenate([wn, w_out[512:]], axis=0).astype(BF16)
    row = lambda wd: pl.BlockSpec((tm, wd), lambda i: (i, 0))
    full = lambda a: pl.BlockSpec(a.shape, lambda i: (0,) * a.ndim)
    g2, b2 = g.reshape(1, -1), b.reshape(1, -1)
    return pl.pallas_call(
        _outproj_kernel, grid=(T // tm,),
        in_specs=[row(512), row(512), row(D_MODEL), full(w), full(g2), full(b2)],
        out_specs=row(D_MODEL), out_shape=jax.ShapeDtypeStruct((T, D_MODEL), F32),
        compiler_params=_params(("parallel",)), name="outproj")(y_nsa, y_ml, x2d, w, g2, b2)


def _memkv_kernel(m_ref, w_ref, o_ref):
    o_ref[...] = jnp.dot(m_ref[...].astype(BF16), w_ref[...], preferred_element_type=F32).astype(o_ref.dtype)


def _memkv(mem2d, wk, wv):
    w = jnp.concatenate([wk, wv], axis=1).astype(BF16)
    R = mem2d.shape[0]
    return pl.pallas_call(
        _memkv_kernel, grid=(R // MEM_LEN,),
        in_specs=[pl.BlockSpec((MEM_LEN, D_MODEL), lambda i: (i, 0)),
                  pl.BlockSpec(w.shape, lambda i: (0, 0))],
        out_specs=pl.BlockSpec((MEM_LEN, 2 * D_MODEL), lambda i: (i, 0)),
        out_shape=jax.ShapeDtypeStruct((R, 2 * D_MODEL), BF16),
        compiler_params=_params(("parallel",)), name="memkv")(mem2d, w)


def _xattn_kernel(x_ref, kv_ref, wq_ref, wo_ref, g_ref, b_ref, rw_ref, x2_ref, x2p_ref, sc_ref):
    x1 = x_ref[...]
    q = jnp.dot(x1.astype(BF16), wq_ref[...], preferred_element_type=F32).astype(BF16)
    outs = []
    for h in range(XA_HEADS):
        qh = q[:, h * XA_DH:(h + 1) * XA_DH]
        kh = kv_ref[:, h * XA_DH:(h + 1) * XA_DH]
        vh = kv_ref[:, D_MODEL + h * XA_DH:D_MODEL + (h + 1) * XA_DH]
        s = lax.dot_general(qh, kh, _DN_T, preferred_element_type=F32) * (XA_DH ** -0.5)
        p = jnp.exp(s - jnp.max(s, axis=-1, keepdims=True))
        p = p / jnp.sum(p, axis=-1, keepdims=True)
        outs.append(jnp.dot(p.astype(BF16), vh, preferred_element_type=F32).astype(BF16))
    o = jnp.concatenate(outs, axis=1)
    xa = jnp.dot(o, wo_ref[...], preferred_element_type=F32)
    x2 = _ln_rows(ALPHA * x1 + xa, g_ref[...], b_ref[...])
    x2_ref[...] = x2
    x2p_ref[...] = _pack_bf16_pairs(x2)
    xh = x2.astype(BF16)
    xl = (x2 - xh.astype(F32)).astype(BF16)
    wh = rw_ref[0]
    wl = rw_ref[1]
    logit = (lax.dot_general(wh, xh, _DN_T, preferred_element_type=F32) +
             lax.dot_general(wh, xl, _DN_T, preferred_element_type=F32) +
             lax.dot_general(wl, xh, _DN_T, preferred_element_type=F32))
    sc_ref[...] = jax.nn.sigmoid(logit)


def _xattn(x1, kv, wq, wo, g, b, router_w, S, tq):
    T = x1.shape[0]
    wqb, wob = wq.astype(BF16), wo.astype(BF16)
    rwt = router_w.T
    rh = rwt.astype(BF16)
    rw = jnp.stack([rh, (rwt - rh.astype(F32)).astype(BF16)])
    g2, b2 = g.reshape(1, -1), b.reshape(1, -1)
    full = lambda a: pl.BlockSpec(a.shape, lambda i: (0,) * a.ndim)
    per = S // tq
    return pl.pallas_call(
        _xattn_kernel, grid=(T // tq,),
        in_specs=[pl.BlockSpec((tq, D_MODEL), lambda i: (i, 0)),
                  pl.BlockSpec((MEM_LEN, 2 * D_MODEL), lambda i: (i // per, 0)),
                  full(wqb), full(wob), full(g2), full(b2), full(rw)],
        out_specs=(pl.BlockSpec((tq, D_MODEL), lambda i: (i, 0)),
                   pl.BlockSpec((tq, D_MODEL // 2), lambda i: (i, 0)),
                   pl.BlockSpec((MOE_E, tq), lambda i: (0, i))),
        out_shape=(jax.ShapeDtypeStruct((T, D_MODEL), F32), jax.ShapeDtypeStruct((T, D_MODEL // 2), I32),
                   jax.ShapeDtypeStruct((MOE_E, T), F32)),
        compiler_params=_params(("parallel",)), name="xattn")(x1, kv, wqb, wob, g2, b2, rw)


def _route_kernel(sc_ref, rb_ref, idx_ref, w_ref):
    E, G = MOE_E, MOE_GROUPS
    per = E // G
    scores = sc_ref[...]
    tr = scores.shape[1]
    biased = scores + rb_ref[...]
    g3 = biased.reshape(G, per, tr)
    j3 = lax.broadcasted_iota(I32, (G, per, tr), 1)
    m1 = jnp.max(g3, axis=1, keepdims=True)
    first = jnp.min(jnp.where(g3 == m1, j3, per), axis=1, keepdims=True)
    m2 = jnp.max(jnp.where(j3 == first, -jnp.inf, g3), axis=1, keepdims=True)
    gs = (m1 + m2).reshape(G, tr)
    gi = lax.broadcasted_iota(I32, (G, tr), 0)
    cnt = jnp.zeros((G, tr), F32)
    for m in range(G):
        row = gs[m:m + 1, :]
        cnt = cnt + jnp.where((row > gs) | ((row == gs) & (gi > m)), 1.0, 0.0)
    gmask = cnt < float(MOE_TOPK_GROUPS)
    masked = jnp.where(gmask[:, None, :], g3, NEG).reshape(E, tr)
    ei = lax.broadcasted_iota(I32, (E, tr), 0)
    idxs, ws = [], []
    for _ in range(MOE_K):
        mx = jnp.max(masked, axis=0, keepdims=True)
        ix = jnp.min(jnp.where(masked == mx, ei, E), axis=0, keepdims=True)
        hit = ei == ix
        ws.append(jnp.sum(jnp.where(hit, scores, 0.0), axis=0, keepdims=True))
        idxs.append(ix)
        masked = jnp.where(hit, -jnp.inf, masked)
    w = jnp.concatenate(ws, axis=0)
    idx_ref[...] = jnp.concatenate(idxs, axis=0)
    w_ref[...] = w / jnp.sum(w, axis=0, keepdims=True) * MOE_ROUTE_SCALE


def _route(scores_t, router_bias, tr):
    E, T = scores_t.shape
    rb = router_bias.reshape(E, 1)
    return pl.pallas_call(
        _route_kernel, grid=(T // tr,),
        in_specs=[pl.BlockSpec((E, tr), lambda i: (0, i)), pl.BlockSpec((E, 1), lambda i: (0, 0))],
        out_specs=(pl.BlockSpec((MOE_K, tr), lambda i: (0, i)), pl.BlockSpec((MOE_K, tr), lambda i: (0, i))),
        out_shape=(jax.ShapeDtypeStruct((MOE_K, T), I32), jax.ShapeDtypeStruct((MOE_K, T), F32)),
        compiler_params=_params(("parallel",)), name="route")(scores_t, rb)


def _rank_kernel(idx_ref, u_ref, rank_ref, cnt_ref, carry):
    E = MOE_E

    @pl.when(pl.program_id(0) == 0)
    def _():
        carry[...] = jnp.zeros_like(carry)

    idx = idx_ref[...]
    tp = idx.shape[1]
    ei = lax.broadcasted_iota(I32, (E, tp), 0)
    hits = [ei == idx[k:k + 1, :] for k in range(MOE_K)]
    onehot = jnp.zeros((E, tp), F32)
    for hit in hits:
        onehot = onehot + jnp.where(hit, 1.0, 0.0)
    pos = jnp.dot(onehot.astype(BF16), u_ref[...], preferred_element_type=F32) + carry[...]
    ranks = [jnp.sum(jnp.where(hit, pos, 0.0), axis=0, keepdims=True) for hit in hits]
    rank_ref[...] = jnp.concatenate(ranks, axis=0).astype(I32)
    total = carry[...] + jnp.sum(onehot, axis=1, keepdims=True)
    carry[...] = total
    cnt_ref[...] = jnp.broadcast_to(total, cnt_ref.shape).astype(I32)


def _rank(idx_t, tp):
    K, T = idx_t.shape
    u = jnp.triu(jnp.ones((tp, tp), F32), k=1).astype(BF16)
    rank, cnt = pl.pallas_call(
        _rank_kernel, grid=(T // tp,),
        in_specs=[pl.BlockSpec((K, tp), lambda i: (0, i)), pl.BlockSpec((tp, tp), lambda i: (0, 0))],
        out_specs=(pl.BlockSpec((K, tp), lambda i: (0, i)), pl.BlockSpec((MOE_E, LANES), lambda i: (0, 0))),
        out_shape=(jax.ShapeDtypeStruct((K, T), I32), jax.ShapeDtypeStruct((MOE_E, LANES), I32)),
        scratch_shapes=[pltpu.VMEM((MOE_E, 1), F32)],
        compiler_params=_params(("arbitrary",)), name="rank")(idx_t, u)
    return rank, cnt[:, 0]


def _dest_kernel(idx_ref, rank_ref, po_ref, dest_ref):
    idx = idx_ref[...]
    tp = idx.shape[1]
    ei = lax.broadcasted_iota(I32, (MOE_E, tp), 0)
    po = po_ref[...]
    base = [jnp.sum(jnp.where(ei == idx[k:k + 1, :], po, 0.0), axis=0, keepdims=True) for k in range(MOE_K)]
    dest_ref[...] = jnp.concatenate(base, axis=0).astype(I32) + rank_ref[...]


def _dest(idx_t, rank_t, poffs, tp):
    K, T = idx_t.shape
    po = poffs.astype(F32).reshape(MOE_E, 1)
    spec = pl.BlockSpec((K, tp), lambda i: (0, i))
    return pl.pallas_call(
        _dest_kernel, grid=(T // tp,),
        in_specs=[spec, spec, pl.BlockSpec((MOE_E, 1), lambda i: (0, 0))],
        out_specs=spec, out_shape=jax.ShapeDtypeStruct((K, T), I32),
        compiler_params=_params(("parallel",)), name="dest")(idx_t, rank_t, po)


def _pack_bf16_pairs(v):
    m = v.shape[1] // 2
    bits = lax.bitcast_convert_type(v.astype(BF16).astype(F32), jnp.uint32)
    return lax.bitcast_convert_type((bits[:, :m] >> 16) | (bits[:, m:] & jnp.uint32(0xFFFF0000)), I32)


def _unpack_bf16_pairs(w):
    w = lax.bitcast_convert_type(w, jnp.uint32)
    lo = lax.bitcast_convert_type(w << 16, F32)
    hi = lax.bitcast_convert_type(w & jnp.uint32(0xFFFF0000), F32)
    return lo, hi


def _row_scatter(rows, dest_t, n_rows):
    T, d = rows.shape
    K = dest_t.shape[0]
    nw = SC_CORES * SC_SUBCORES
    per_w = T // nw
    ch = SC_GATHER_ROWS
    mesh = plsc.VectorSubcoreMesh(core_axis_name="c", subcore_axis_name="s")

    @functools.partial(
        pl.kernel, mesh=mesh, out_type=jax.ShapeDtypeStruct((n_rows, d), I32),
        scratch_types=[pltpu.VMEM((K, ch), I32), pltpu.VMEM((ch, d), I32), pltpu.SemaphoreType.DMA],
        name="row_scatter")
    def scatter(rows_hbm, dest_hbm, out_hbm, idx_v, rows_v, sem):
        wid = lax.axis_index("s") * SC_CORES + lax.axis_index("c")
        base = wid * per_w

        @pl.loop(0, per_w // ch)
        def _(i):
            off = pl.multiple_of(base + i * ch, ch)
            pltpu.sync_copy(rows_hbm.at[pl.ds(off, ch)], rows_v)
            pltpu.sync_copy(dest_hbm.at[:, pl.ds(off, ch)], idx_v)
            copies = [pltpu.async_copy(rows_v, out_hbm.at[idx_v.at[k]], sem) for k in range(K)]
            for cp in copies:
                cp.wait()

    return scatter(rows, dest_t)


RING_AHEAD = 4
RING_OUT = 3


def _expert_kernel(po_ref, cnt_ref, tot_ref, xs_hbm, w1_ref, w3_ref, w2_ref, ys_hbm,
                   xbuf, ybuf, w1b, w3b, w2b, insem, outsem):
    e = pl.program_id(0)
    n = cnt_ref[e]
    rb = xbuf.shape[1]
    ns = xbuf.shape[0]
    hw = D_MODEL // 2
    nblk = jnp.maximum((n + rb - 1) // rb, 1)
    g0 = po_ref[e] // rb
    total = tot_ref[0]
    w1b[...] = w1_ref[...].astype(BF16)
    w3b[...] = w3_ref[...].astype(BF16)
    w2b[...] = w2_ref[...].astype(BF16)

    def in_copy(g, slot):
        return pltpu.make_async_copy(xs_hbm.at[pl.ds(pl.multiple_of(g * rb, rb), rb)], xbuf.at[slot], insem.at[slot])

    def out_copy(g, slot):
        return pltpu.make_async_copy(ybuf.at[slot], ys_hbm.at[pl.ds(pl.multiple_of(g * rb, rb), rb)],
                                     outsem.at[slot])

    @pl.when(e == 0)
    def _():
        for d in range(RING_AHEAD):
            @pl.when(d < total)
            def _():
                in_copy(d, d).start()

    def body(j, c):
        g = g0 + j
        slot = g % ns
        oslot = g % RING_OUT

        @pl.when(g + RING_AHEAD < total)
        def _():
            in_copy(g + RING_AHEAD, (g + RING_AHEAD) % ns).start()

        in_copy(0, slot).wait()

        @pl.when(g >= RING_OUT)
        def _():
            out_copy(0, oslot).wait()

        words = xbuf[slot]
        row = j * rb + lax.broadcasted_iota(I32, words.shape, 0)
        lo, hi = _unpack_bf16_pairs(jnp.where(row < n, words, 0))
        lo, hi = lo.astype(BF16), hi.astype(BF16)
        a = (jnp.dot(lo, w1b[0:hw, :], preferred_element_type=F32) +
             jnp.dot(hi, w1b[hw:, :], preferred_element_type=F32))
        u = (jnp.dot(lo, w3b[0:hw, :], preferred_element_type=F32) +
             jnp.dot(hi, w3b[hw:, :], preferred_element_type=F32))
        h = (a * jax.nn.sigmoid(a) * u).astype(BF16)
        ybuf[oslot] = _pack_bf16_pairs(jnp.dot(h, w2b[...], preferred_element_type=F32))
        out_copy(g, oslot).start()
        return c

    lax.fori_loop(0, nblk, body, 0)

    @pl.when(e + 1 == pl.num_programs(0))
    def _():
        for i in range(RING_OUT):
            @pl.when(total > i)
            def _():
                out_copy(0, (total - 1 - i) % RING_OUT).wait()


def _experts(poffs, counts, total_blocks, xs, w1, w3, w2):
    n_rows, wp = xs.shape
    rb = ROW_BLOCK
    wspec = lambda shape: pl.BlockSpec((None,) + shape, lambda e, po, cn, tb: (e, 0, 0))
    gs = pltpu.PrefetchScalarGridSpec(
        num_scalar_prefetch=3, grid=(MOE_E,),
        in_specs=[pl.BlockSpec(memory_space=pl.ANY), wspec((D_MODEL, MOE_FF)), wspec((D_MODEL, MOE_FF)),
                  wspec((MOE_FF, D_MODEL))],
        out_specs=pl.BlockSpec(memory_space=pl.ANY),
        scratch_shapes=[pltpu.VMEM((RING_AHEAD + 1, rb, wp), I32), pltpu.VMEM((RING_OUT, rb, wp), I32),
                        pltpu.VMEM((D_MODEL, MOE_FF), BF16), pltpu.VMEM((D_MODEL, MOE_FF), BF16),
                        pltpu.VMEM((MOE_FF, D_MODEL), BF16),
                        pltpu.SemaphoreType.DMA((RING_AHEAD + 1,)), pltpu.SemaphoreType.DMA((RING_OUT,))])
    return pl.pallas_call(
        _expert_kernel, grid_spec=gs, out_shape=jax.ShapeDtypeStruct((n_rows, wp), I32),
        compiler_params=_params(("arbitrary",)), name="experts")(poffs, counts, total_blocks, xs, w1, w3, w2)


def _row_gather(table, idx):
    n, d = idx.shape[0], table.shape[1]
    nw = SC_CORES * SC_SUBCORES
    per_w = n // nw
    ch = SC_GATHER_ROWS // 2
    mesh = plsc.VectorSubcoreMesh(core_axis_name="c", subcore_axis_name="s")

    @functools.partial(
        pl.kernel, mesh=mesh, out_type=jax.ShapeDtypeStruct((n, d), I32),
        scratch_types=[pltpu.VMEM((ch,), I32), pltpu.VMEM((ch,), I32),
                       pltpu.VMEM((ch, d), I32), pltpu.VMEM((ch, d), I32),
                       pltpu.SemaphoreType.DMA, pltpu.SemaphoreType.DMA,
                       pltpu.SemaphoreType.DMA, pltpu.SemaphoreType.DMA],
        name="row_gather")
    def gather(table_hbm, idx_hbm, out_hbm, idx0, idx1, rows0, rows1, g0, g1, w0, w1):
        wid = lax.axis_index("s") * SC_CORES + lax.axis_index("c")
        base = wid * per_w

        @pl.loop(0, per_w // (2 * ch))
        def _(i):
            off0 = pl.multiple_of(base + 2 * i * ch, ch)
            off1 = pl.multiple_of(off0 + ch, ch)
            pltpu.sync_copy(idx_hbm.at[pl.ds(off0, ch)], idx0)
            c0 = pltpu.async_copy(table_hbm.at[idx0], rows0, g0)
            pltpu.sync_copy(idx_hbm.at[pl.ds(off1, ch)], idx1)
            c1 = pltpu.async_copy(table_hbm.at[idx1], rows1, g1)
            c0.wait()
            o0 = pltpu.async_copy(rows0, out_hbm.at[pl.ds(off0, ch)], w0)
            c1.wait()
            o1 = pltpu.async_copy(rows1, out_hbm.at[pl.ds(off1, ch)], w1)
            o0.wait()
            o1.wait()

    return gather(table, idx)


def _combine_kernel(w_ref, x_ref, ysg_ref, s1_ref, s3_ref, s2_ref, g_ref, b_ref, *rest):
    o_ref = rest[-1]
    x2 = x_ref[...]
    xb = x2.astype(BF16)
    a = jnp.dot(xb, s1_ref[...], preferred_element_type=F32)
    c = jnp.dot(xb, s3_ref[...], preferred_element_type=F32)
    shared = jnp.dot((a * jax.nn.sigmoid(a) * c).astype(BF16), s2_ref[...], preferred_element_type=F32)
    w = w_ref[...]
    hw = ysg_ref.shape[2]
    y_lo = jnp.zeros((x2.shape[0], hw), F32)
    y_hi = jnp.zeros((x2.shape[0], hw), F32)
    for k in range(MOE_K):
        lo, hi = _unpack_bf16_pairs(ysg_ref[k])
        y_lo = y_lo + w[:, k:k + 1] * lo
        y_hi = y_hi + w[:, k:k + 1] * hi
    y = shared + jnp.concatenate([y_lo, y_hi], axis=1)
    o_ref[...] = _ln_rows(ALPHA * x2 + y, g_ref[...], b_ref[...])


def _combine(w_nat, x2, ysg, prev, first_tile, sw1, sw3, sw2, g, b, tc):
    T = x2.shape[0]
    wp = ysg.shape[2]
    full = lambda a: pl.BlockSpec(a.shape, lambda i: (0,) * a.ndim)
    row = lambda wd: pl.BlockSpec((tc, wd), lambda i: (i + first_tile, 0))
    s1, s3, s2 = sw1.astype(BF16), sw3.astype(BF16), sw2.astype(BF16)
    g2, b2 = g.reshape(1, -1), b.reshape(1, -1)
    args = [w_nat, x2, ysg, s1, s3, s2, g2, b2]
    in_specs = [row(MOE_K), row(D_MODEL), pl.BlockSpec((MOE_K, tc, wp), lambda i: (0, i, 0)),
                full(s1), full(s3), full(s2), full(g2), full(b2)]
    aliases = {}
    if prev is not None:
        aliases = {len(args): 0}
        args.append(prev)
        in_specs.append(pl.BlockSpec(memory_space=pl.ANY))
    return pl.pallas_call(
        _combine_kernel, grid=(ysg.shape[1] // tc,), in_specs=in_specs,
        out_specs=row(D_MODEL), out_shape=jax.ShapeDtypeStruct((T, D_MODEL), F32),
        input_output_aliases=aliases,
        compiler_params=_params(("parallel",)), name="combine",
    )(*args)


def _moe(x2, x2p, scores_t, router_bias, w1, w3, w2, sw1, sw3, sw2, g, b, tiles):
    T = x2.shape[0]
    rb = ROW_BLOCK
    idx_t, w_t = _route(scores_t, router_bias, tiles['route'])
    rank_t, counts = _rank(idx_t, tiles['rank'])
    pcounts = jnp.maximum((counts + rb - 1) // rb, 1) * rb
    pends = jnp.cumsum(pcounts)
    poffs = (pends - pcounts).astype(I32)
    n_blocks = -(-T * MOE_K // rb) + MOE_E
    dest_t = _dest(idx_t, rank_t, poffs, tiles['rank'])
    xs = _row_scatter(x2p, dest_t, n_blocks * rb)
    total_blocks = (pends[-1:] // rb).astype(I32)
    ys = _experts(poffs, counts.astype(I32), total_blocks, xs, w1, w3, w2)
    tc = tiles['combine']
    n_split = 2 if T % (2 * tc) == 0 and (T // 2 * MOE_K) % (SC_CORES * SC_SUBCORES * SC_GATHER_ROWS) == 0 else 1
    th = T // n_split
    w_nat = w_t.T
    out = None
    for p in range(n_split):
        ysg = _row_gather(ys, dest_t[:, p * th:(p + 1) * th].reshape(-1)).reshape(MOE_K, th, -1)
        out = _combine(w_nat, x2, ysg, out, p * th // tc, sw1, sw3, sw2, g, b, tc)
    return out


def _tiles(B, S):
    T = B * S
    pick = lambda want, n: want if n % want == 0 else n
    return dict(proj=pick(512, T), nsa_q=pick(128, S), nsa_ck=pick(512, S), outproj=pick(1024, T),
                mlstm_nb=2 if B % 2 == 0 else 1, xattn=pick(1024, S), route=pick(1024, T), rank=pick(512, T),
                combine=pick(512, T))


def kernel(x, mem, w_in, nsa_pos_k, nsa_cmp_k_w1, nsa_cmp_k_w2, nsa_pos_v, nsa_cmp_v_w1, nsa_cmp_v_w2,
           mlstm_conv_w, mlstm_i_bias, mlstm_f_bias, mlstm_norm_g, w_out, ln1_g, ln1_b,
           xa_wq, xa_wk, xa_wv, xa_wo, ln2_g, ln2_b, router_w, router_bias,
           moe_w1, moe_w3, moe_w2, shared_w1, shared_w3, shared_w2, ln3_g, ln3_b):
    B, S, D = x.shape
    T = B * S
    tl = _tiles(B, S)
    xc = x.reshape(T, D)
    memc = mem.reshape(B * MEM_LEN, D)
    for l in range(w_in.shape[0]):
        (q, cmp, ksel, vsel, kwin, vwin, gates, mq, mk, mv, mo, mif) = _project(xc, _prep_w_in(w_in[l]), tl['proj'])
        kcvc = _compress(cmp, B, S, _prep_cmp(nsa_pos_k[l], nsa_cmp_k_w1[l], nsa_cmp_k_w2[l],
                                              nsa_pos_v[l], nsa_cmp_v_w1[l], nsa_cmp_v_w2[l]))
        y_nsa = _nsa(q, kcvc, ksel, vsel, kwin, vwin, gates, B, S, tl['nsa_q'], tl['nsa_ck'])
        y_ml = _mlstm(mq, mk, mv, mo, mif, mlstm_conv_w[l], mlstm_i_bias[l], mlstm_f_bias[l],
                      mlstm_norm_g[l], B, S, tl['mlstm_nb'])
        x1 = _outproj(y_nsa, y_ml, xc, w_out[l], ln1_g[l], ln1_b[l], tl['outproj'])
        kv = _memkv(memc, xa_wk[l], xa_wv[l])
        x2, x2p, scores_t = _xattn(x1, kv, xa_wq[l], xa_wo[l], ln2_g[l], ln2_b[l], router_w[l], S, tl['xattn'])
        xc = _moe(x2, x2p, scores_t, router_bias[l], moe_w1[l], moe_w3[l], moe_w2[l],
                  shared_w1[l], shared_w3[l], shared_w2[l], ln3_g[l], ln3_b[l], tl)
    return xc.reshape(B, S, D)
```

```python
import functools
import numpy as np
import jax
import jax.numpy as jnp
from jax import lax
from jax.experimental import pallas as pl
from jax.experimental.pallas import tpu as pltpu
from jax.experimental.pallas import tpu_sc as plsc

F32 = jnp.float32
BF16 = jnp.bfloat16
I32 = jnp.int32

D_MODEL = 1024
MEM_LEN = 256
NSA_HEADS = 8
NSA_GROUPS = 2
NSA_HPG = 4
NSA_DK = 64
NSA_CMP_LEN = 32
NSA_CMP_STRIDE = 16
NSA_SEL_BLOCK = 64
NSA_SEL_TOPN = 8
NSA_WINDOW = 512
ML_HEADS = 4
ML_DH = 128
ML_CHUNK = 64
ML_CONV = 4
XA_HEADS = 4
XA_DH = 256
MOE_E = 256
MOE_K = 8
MOE_GROUPS = 8
MOE_TOPK_GROUPS = 4
MOE_FF = 256
MOE_ROUTE_SCALE = 2.5
DEPTH = 1
ALPHA = (2.0 * DEPTH) ** 0.25
LN_EPS = 1e-5
NEG = -1e30
FORCE_BONUS = 1e4

LANES = 128
ROW_BLOCK = 512
VMEM_LIMIT = 56 * 1024 * 1024
SC_CORES = 2
SC_SUBCORES = 16
SC_GATHER_ROWS = 128

_DN_T = (((1,), (1,)), ((), ()))
_DN_TA = (((0,), (0,)), ((), ()))


def _params(sem):
    return pltpu.CompilerParams(dimension_semantics=sem, vmem_limit_bytes=VMEM_LIMIT)


def _ln_rows(v, g, b):
    mu = jnp.mean(v, axis=-1, keepdims=True)
    d = v - mu
    var = jnp.mean(d * d, axis=-1, keepdims=True)
    return d * lax.rsqrt(var + LN_EPS) * g + b


_SEGS = (('q', 512, BF16), ('cmp', 256, F32), ('ksel', 128, BF16), ('vsel', 128, BF16),
         ('kwin', 128, BF16), ('vwin', 128, BF16), ('gates', 128, F32), ('mq', 512, BF16),
         ('mk', 512, BF16), ('mv', 512, BF16), ('mo', 512, BF16), ('mif', 128, F32))


def _proj_kernel(x_ref, w_ref, *out_refs):
    xb = x_ref[...].astype(BF16)
    off = 0
    for o_ref, (_, wd, _) in zip(out_refs, _SEGS):
        o_ref[...] = jnp.dot(xb, w_ref[:, off:off + wd], preferred_element_type=F32).astype(o_ref.dtype)
        off += wd


def _prep_w_in(w):
    sizes = (512,) + (128,) * 6 + (24,) + (512,) * 4 + (4, 4)
    cuts = np.cumsum(sizes)[:-1].tolist()
    (wq, kc, vc, ks, vs, kw, vw, wg, mq, mk, mv, mo, mi, mf) = jnp.split(w, cuts, axis=1)
    wq = wq.reshape(D_MODEL, NSA_GROUPS, NSA_HPG, NSA_DK).transpose(0, 2, 1, 3).reshape(D_MODEL, 512)
    pad = lambda a: jnp.pad(a, ((0, 0), (0, LANES - a.shape[1])))
    segs = [wq, kc, vc, ks, vs, kw, vw, pad(wg), mq, mk, mv, mo, pad(jnp.concatenate([mi, mf], axis=1))]
    return jnp.concatenate(segs, axis=1).astype(BF16)


def _project(x2d, w_all, tm):
    T = x2d.shape[0]
    n = w_all.shape[1]
    out_shape = tuple(jax.ShapeDtypeStruct((T, wd), dt) for _, wd, dt in _SEGS)
    out_specs = tuple(pl.BlockSpec((tm, wd), lambda i: (i, 0)) for _, wd, _ in _SEGS)
    return pl.pallas_call(
        _proj_kernel, grid=(T // tm,),
        in_specs=[pl.BlockSpec((tm, D_MODEL), lambda i: (i, 0)),
                  pl.BlockSpec((D_MODEL, n), lambda i: (0, 0))],
        out_specs=out_specs, out_shape=out_shape,
        compiler_params=_params(("parallel",)), name="proj")(x2d, w_all)


def _cmp_kernel(r_ref, pa_ref, pb_ref, wa_ref, wb_ref, w2_ref, o_ref):
    r = r_ref[...]
    a = jnp.dot((r + pa_ref[...]).astype(BF16), wa_ref[...], preferred_element_type=F32)
    b = jnp.dot((r + pb_ref[...]).astype(BF16), wb_ref[...], preferred_element_type=F32)
    nr = r.shape[0]
    hid = a + pltpu.roll(b, nr - 1, 0)
    hid = hid * jax.nn.sigmoid(hid)
    out = jnp.dot(hid.astype(BF16), w2_ref[...], preferred_element_type=F32)
    row = lax.broadcasted_iota(I32, out.shape, 0)
    o_ref[...] = jnp.where(row < nr - 1, out, 0.0).astype(o_ref.dtype)


def _prep_cmp(pos_k, w1_k, w2_k, pos_v, w1_v, w2_v):
    eye = jnp.eye(NSA_GROUPS, dtype=F32)

    def expand_w1(w1, half):
        w = w1.reshape(NSA_CMP_LEN, NSA_DK, NSA_DK)[half * 16:(half + 1) * 16]
        return jnp.einsum('jde,gk->jgdke', w, eye).reshape(16, 128, 128)

    def both(fk, fv):
        z = jnp.zeros_like(fk)
        top = jnp.concatenate([fk, z], axis=-1)
        bot = jnp.concatenate([z, fv], axis=-1)
        return jnp.concatenate([top, bot], axis=-2)

    wa = both(expand_w1(w1_k, 0), expand_w1(w1_v, 0)).reshape(16 * 256, 256).astype(BF16)
    wb = both(expand_w1(w1_k, 1), expand_w1(w1_v, 1)).reshape(16 * 256, 256).astype(BF16)
    w2 = both(jnp.kron(eye, w2_k), jnp.kron(eye, w2_v)).astype(BF16)

    def pos_row(half):
        pk = jnp.tile(pos_k[half * 16:(half + 1) * 16], (1, NSA_GROUPS))
        pv = jnp.tile(pos_v[half * 16:(half + 1) * 16], (1, NSA_GROUPS))
        return jnp.concatenate([pk, pv], axis=1).reshape(1, 16 * 256)

    return pos_row(0), pos_row(1), wa, wb, w2


def _compress(cmp2d, B, S, prep):
    pa, pb, wa, wb, w2 = prep
    nr = S // NSA_CMP_STRIDE
    r = cmp2d.reshape(B, nr, NSA_CMP_STRIDE * 256)
    full = lambda a: pl.BlockSpec(a.shape, lambda b: (0,) * a.ndim)
    return pl.pallas_call(
        _cmp_kernel, grid=(B,),
        in_specs=[pl.BlockSpec((None, nr, NSA_CMP_STRIDE * 256), lambda b: (b, 0, 0)),
                  full(pa), full(pb), full(wa), full(wb), full(w2)],
        out_specs=pl.BlockSpec((None, nr, 256), lambda b: (b, 0, 0)),
        out_shape=jax.ShapeDtypeStruct((B, nr, 256), BF16),
        compiler_params=_params(("parallel",)), name="cmp")(r, pa, pb, wa, wb, w2)


def _nsa_consts(S):
    n_cmp = (S - NSA_CMP_LEN) // NSA_CMP_STRIDE + 1
    n_sel = S // NSA_SEL_BLOCK
    cs = np.arange(n_cmp) * NSA_CMP_STRIDE
    ss = np.arange(n_sel) * NSA_SEL_BLOCK
    ov = ((cs[:, None] < ss[None, :] + NSA_SEL_BLOCK) & (cs[:, None] + NSA_CMP_LEN > ss[None, :]))
    ovt = np.zeros((LANES, S // NSA_CMP_STRIDE), np.float32)
    ovt[:n_sel, :n_cmp] = ov.T
    e = np.zeros((LANES, S), np.float32)
    e[np.arange(S) // NSA_SEL_BLOCK, np.arange(S)] = 1.0
    return jnp.asarray(ovt, BF16), jnp.asarray(e, BF16)


def _nsa_kernel(q_ref, kcvc_ref, ksel_ref, vsel_ref, kwin_ref, vwin_ref, gates_ref, ovt_ref, e_ref,
                y_ref, bias_ref, *, tq, ck, n_sel):
    G, H = NSA_GROUPS, NSA_HPG
    GH = G * H
    M = GH * tq
    S = ksel_ref.shape[0]
    W = NSA_WINDOW
    ws = min(W + tq, S)
    t0 = pl.program_id(1) * tq
    gates = jax.nn.sigmoid(gates_ref[...])
    lane = lax.broadcasted_iota(I32, (tq, LANES), 1)
    t_col = t0 + lax.broadcasted_iota(I32, (tq, 1), 0)
    kc = kcvc_ref[:, 0:LANES]
    vc = kcvc_ref[:, LANES:2 * LANES]
    nc = kc.shape[0]
    qs = []
    for g in range(G):
        gmask = (lane // NSA_DK) == g
        for h in range(H):
            qh = q_ref[:, h * LANES:(h + 1) * LANES] * (NSA_DK ** -0.5)
            qs.append(jnp.where(gmask, qh, jnp.zeros_like(qh)))
    Q = jnp.concatenate(qs, axis=0).astype(BF16)

    s = lax.dot_general(Q, kc, _DN_T, preferred_element_type=F32)
    c_idx = lax.broadcasted_iota(I32, (tq, nc), 1)
    cmask = (c_idx * NSA_CMP_STRIDE + NSA_CMP_LEN - 1) <= t_col
    s3 = jnp.where(cmask[None], s.reshape(GH, tq, nc), NEG)
    p = jnp.exp(s3 - jnp.max(s3, axis=-1, keepdims=True))
    p = p / jnp.sum(p, axis=-1, keepdims=True)
    p = jnp.where(cmask[None], p, 0.0)
    o_cmp = jnp.dot(p.reshape(M, nc).astype(BF16), vc, preferred_element_type=F32).reshape(GH, tq, LANES)

    for g in range(G):
        psum = jnp.sum(p[g * H:(g + 1) * H], axis=0)
        hi = psum.astype(BF16)
        lo = (psum - hi.astype(F32)).astype(BF16)
        ovt = ovt_ref[...]
        pslt = (lax.dot_general(ovt, hi, _DN_T, preferred_element_type=F32) +
                lax.dot_general(ovt, lo, _DN_T, preferred_element_type=F32))
        imp_p = pslt[0:n_sel, :]
        n_i = lax.broadcasted_iota(I32, (n_sel, tq), 0)
        cur = (t0 + lax.broadcasted_iota(I32, (n_sel, tq), 1)) // NSA_SEL_BLOCK
        forced = (n_i == 0) | (n_i == cur) | (n_i == cur - 1)
        imp = jnp.where(n_i <= cur, imp_p + jnp.where(forced, FORCE_BONUS, 0.0), NEG)
        cnt = jnp.zeros((n_sel, tq), F32)
        for m in range(n_sel):
            row = imp[m:m + 1, :]
            beats = (row > imp) | ((row == imp) & (n_i > m))
            cnt = cnt + jnp.where(beats, 1.0, 0.0)
        selt = jnp.where(cnt < float(min(NSA_SEL_TOPN, n_sel)), 1.0, 0.0)
        selt = jnp.concatenate([selt, jnp.zeros((LANES - n_sel, tq), F32)], axis=0)
        sel = selt.T.astype(BF16)
        maskf = jnp.dot(sel, e_ref[...], preferred_element_type=F32)
        kpos = lax.broadcasted_iota(I32, (tq, S), 1)
        bias_ref[g] = jnp.where((maskf > 0.5) & (kpos <= t_col), 0.0, NEG)

    vlane = lax.broadcasted_iota(I32, (1, LANES), 1) // NSA_DK

    def pv_with_sums(pb, v):
        outs = []
        for g in range(G):
            vg = jnp.where(vlane == g, v, jnp.ones_like(v))
            outs.append(jnp.dot(pb[g * H * tq:(g + 1) * H * tq], vg, preferred_element_type=F32))
        return jnp.concatenate(outs, axis=0)

    def normalise(acc):
        outs = []
        for r in range(GH):
            c = NSA_DK * (1 - r // H)
            outs.append(acc[r] / acc[r][:, c:c + 1])
        return outs

    def sel_body(j, carry):
        m_i, acc = carry
        ks = pl.multiple_of(j * ck, ck)
        k = ksel_ref[pl.ds(ks, ck), :]
        v = vsel_ref[pl.ds(ks, ck), :]
        sj = lax.dot_general(Q, k, _DN_T, preferred_element_type=F32)
        sj = (sj.reshape(G, H, tq, ck) + bias_ref[:, :, pl.ds(ks, ck)][:, None]).reshape(GH, tq, ck)
        m_new = jnp.maximum(m_i, jnp.max(sj, axis=-1, keepdims=True))
        a = jnp.exp(m_i - m_new)
        pj = jnp.exp((sj - m_new).astype(BF16)).reshape(M, ck)
        return m_new, a * acc + pv_with_sums(pj, v).reshape(GH, tq, LANES)

    init = (jnp.full((GH, tq, 1), NEG, F32), jnp.zeros((GH, tq, LANES), F32))
    _, acc = lax.fori_loop(0, (t0 + tq + ck - 1) // ck, sel_body, init)
    o_sel = normalise(acc)

    kst = pl.multiple_of(jnp.clip(t0 - W, 0, S - ws), LANES)
    kwn = kwin_ref[pl.ds(kst, ws), :]
    vwn = vwin_ref[pl.ds(kst, ws), :]
    sw = lax.dot_general(Q, kwn, _DN_T, preferred_element_type=F32)
    wpos = kst + lax.broadcasted_iota(I32, (tq, ws), 1)
    wmask = (wpos <= t_col) & (wpos > t_col - W)
    sw3 = jnp.where(wmask[None], sw.reshape(GH, tq, ws), NEG)
    pw = jnp.exp((sw3 - jnp.max(sw3, axis=-1, keepdims=True)).astype(BF16)).reshape(M, ws)
    o_win = normalise(pv_with_sums(pw, vwn).reshape(GH, tq, LANES))

    g0mask = lane < NSA_DK
    for h in range(H):
        o_g = []
        for g in range(G):
            r = g * H + h
            c0 = r * 3
            o_g.append(gates[:, c0:c0 + 1] * o_cmp[r] + gates[:, c0 + 1:c0 + 2] * o_sel[r] +
                       gates[:, c0 + 2:c0 + 3] * o_win[r])
        y_ref[:, h * LANES:(h + 1) * LANES] = jnp.where(g0mask, o_g[0], o_g[1]).astype(y_ref.dtype)


def _nsa(q, kcvc, ksel, vsel, kwin, vwin, gates, B, S, tq, ck):
    T = B * S
    nq = S // tq
    ovt, e = _nsa_consts(S)
    seq = lambda a: a.reshape(B, S, LANES)
    kv_spec = pl.BlockSpec((None, S, LANES), lambda b, i: (b, 0, 0))
    kern = functools.partial(_nsa_kernel, tq=tq, ck=ck, n_sel=S // NSA_SEL_BLOCK)
    return pl.pallas_call(
        kern, grid=(B, nq),
        in_specs=[pl.BlockSpec((tq, 512), lambda b, i: (b * nq + i, 0)),
                  pl.BlockSpec((None,) + kcvc.shape[1:], lambda b, i: (b, 0, 0)),
                  kv_spec, kv_spec, kv_spec, kv_spec,
                  pl.BlockSpec((tq, LANES), lambda b, i: (b * nq + i, 0)),
                  pl.BlockSpec(ovt.shape, lambda b, i: (0, 0)),
                  pl.BlockSpec(e.shape, lambda b, i: (0, 0))],
        out_specs=pl.BlockSpec((tq, 512), lambda b, i: (b * nq + i, 0)),
        out_shape=jax.ShapeDtypeStruct((T, 512), BF16),
        scratch_shapes=[pltpu.VMEM((NSA_GROUPS, tq, S), F32)],
        compiler_params=_params(("parallel", "parallel")), name="nsa",
    )(q, kcvc, seq(ksel), seq(vsel), seq(kwin), seq(vwin), gates, ovt, e)


def _mlstm_kernel(q_ref, k_ref, v_ref, o_ref, gn_ref, gt_ref, cw_ref, bn_ref, bt_ref, ng_ref, tri_ref,
                  y_ref, c_scr, n_scr):
    H, dh, L = ML_HEADS, ML_DH, ML_CHUNK
    nb, S = q_ref.shape[0], q_ref.shape[1]
    nchunk = S // L
    c_scr[...] = jnp.zeros_like(c_scr)
    n_scr[...] = jnp.zeros_like(n_scr)
    row = lax.broadcasted_iota(I32, (L, H * dh), 0)
    li = lax.broadcasted_iota(I32, (L, L), 0)
    mi = lax.broadcasted_iota(I32, (L, L), 1)
    causal = mi <= li
    tril = tri_ref[0]
    triu = tri_ref[1]
    hp = lax.Precision.HIGHEST

    def conv_silu(ref, bi, c, wofs):
        r0 = pl.multiple_of(c * L, L)
        rp = pl.multiple_of(jnp.maximum(c - 1, 0) * L, L)
        cur = ref[bi, pl.ds(r0, L), :].astype(F32)
        prev = ref[bi, pl.ds(rp, L), :].astype(F32) * jnp.where(c > 0, 1.0, 0.0)
        acc = cur * cw_ref[ML_CONV - 1:ML_CONV, wofs:wofs + H * dh]
        for j in range(1, ML_CONV):
            sh = jnp.where(row < j, pltpu.roll(prev, j, 0), pltpu.roll(cur, j, 0))
            acc = acc + sh * cw_ref[ML_CONV - 1 - j:ML_CONV - j, wofs:wofs + H * dh]
        return acc * jax.nn.sigmoid(acc)

    def body(c, m_state):
        r0 = pl.multiple_of(c * L, L)
        new_m = []
        for bi in range(nb):
            qa = conv_silu(q_ref, bi, c, 0) * (dh ** -0.5)
            ka = conv_silu(k_ref, bi, c, H * dh)
            va = v_ref[bi, pl.ds(r0, L), :]
            oa = o_ref[bi, pl.ds(r0, L), :].astype(F32)
            gn = gn_ref[bi, pl.ds(r0, L), :] + bn_ref[...]
            gt = gt_ref[bi, :, c, :] + bt_ref[...]
            lf_n = jax.nn.log_sigmoid(gn)
            lf_t = jax.nn.log_sigmoid(gt)
            b_n = jnp.dot(tril, lf_n, precision=hp, preferred_element_type=F32)
            b_t = jnp.dot(lf_t, triu, precision=hp, preferred_element_type=F32)
            for h in range(H):
                st = bi * H + h
                q = qa[:, h * dh:(h + 1) * dh]
                k = ka[:, h * dh:(h + 1) * dh]
                v = va[:, h * dh:(h + 1) * dh]
                m_old = m_state[st]
                b_col = b_n[:, H + h:H + h + 1]
                i_col = gn[:, h:h + 1]
                b_row = b_t[H + h:H + h + 1, :]
                i_row = gt[h:h + 1, :]
                g_tot = b_t[H + h:H + h + 1, L - 1:L]
                d_log = jnp.where(causal, b_col - b_row + i_row, NEG)
                inter = b_col + m_old
                m_q = jnp.maximum(inter, jnp.max(d_log, axis=-1, keepdims=True))
                w_intra = jnp.exp(d_log - m_q)
                w_inter = jnp.exp(inter - m_q)
                qb = q.astype(BF16)
                s = lax.dot_general(qb, k.astype(BF16), _DN_T, preferred_element_type=F32) * w_intra
                cst = c_scr[st]
                nst = n_scr[st]
                num = (w_inter * jnp.dot(qb, cst.astype(BF16), preferred_element_type=F32) +
                       jnp.dot(s.astype(BF16), v, preferred_element_type=F32))
                den = w_inter * jnp.sum(q * nst, axis=-1, keepdims=True) + jnp.sum(s, axis=-1, keepdims=True)
                hv = num / jnp.maximum(jnp.abs(den), jnp.exp(-m_q))
                log_k = g_tot - b_col + i_col
                m_new = jnp.maximum(g_tot + m_old, jnp.max(log_k, axis=0, keepdims=True))
                wk = jnp.exp(log_k - m_new)
                decay = jnp.exp(g_tot + m_old - m_new)
                kw = k * wk
                c_scr[st] = decay * cst + lax.dot_general(kw.astype(BF16), v, _DN_TA, preferred_element_type=F32)
                n_scr[st] = decay * nst + jnp.sum(kw, axis=0, keepdims=True)
                new_m.append(m_new)
                mu = jnp.mean(hv, axis=-1, keepdims=True)
                dv = hv - mu
                var = jnp.mean(dv * dv, axis=-1, keepdims=True)
                hn = dv * lax.rsqrt(var + LN_EPS) * ng_ref[:, h * dh:(h + 1) * dh]
                og = jax.nn.sigmoid(oa[:, h * dh:(h + 1) * dh])
                y_ref[bi, pl.ds(r0, L), h * dh:(h + 1) * dh] = (og * hn).astype(y_ref.dtype)
        return tuple(new_m)

    lax.fori_loop(0, nchunk, body, tuple(jnp.zeros((1, 1), F32) for _ in range(nb * H)))


def _mlstm(mq, mk, mv, mo, mif, conv_w, i_bias, f_bias, norm_g, B, S, nb):
    T = B * S
    H, dh, L = ML_HEADS, ML_DH, ML_CHUNK
    W = H * dh
    gt = mif[:, :2 * H].reshape(B, S, 2 * H).transpose(0, 2, 1).reshape(B, 2 * H, S // L, L)
    cw = conv_w.reshape(ML_CONV, 2 * W)
    bias = jnp.concatenate([i_bias, f_bias])
    bn = jnp.pad(bias, (0, LANES - 2 * H)).reshape(1, LANES)
    bt = bias.reshape(2 * H, 1)
    ng = norm_g.reshape(1, W)
    tri = jnp.stack([jnp.tril(jnp.ones((L, L), F32)), jnp.triu(jnp.ones((L, L), F32))])
    seq = lambda a: a.reshape(B, S, a.shape[1])
    rows = lambda w: pl.BlockSpec((nb, S, w), lambda b: (b, 0, 0))
    full = lambda a: pl.BlockSpec(a.shape, lambda b: (0,) * a.ndim)
    y = pl.pallas_call(
        _mlstm_kernel, grid=(B // nb,),
        in_specs=[rows(W), rows(W), rows(W), rows(W), rows(LANES),
                  pl.BlockSpec((nb, 2 * H, S // L, L), lambda b: (b, 0, 0, 0)),
                  full(cw), full(bn), full(bt), full(ng), full(tri)],
        out_specs=rows(W),
        out_shape=jax.ShapeDtypeStruct((B, S, W), BF16),
        scratch_shapes=[pltpu.VMEM((nb * H, dh, dh), F32), pltpu.VMEM((nb * H, 1, dh), F32)],
        compiler_params=_params(("parallel",)), name="mlstm",
    )(seq(mq), seq(mk), seq(mv), seq(mo), seq(mif), gt, cw, bn, bt, ng, tri)
    return y.reshape(T, W)


def _outproj_kernel(yn_ref, ym_ref, x_ref, w_ref, g_ref, b_ref, o_ref):
    mix = (jnp.dot(yn_ref[...], w_ref[0:512, :], preferred_element_type=F32) +
           jnp.dot(ym_ref[...], w_ref[512:1024, :], preferred_element_type=F32))
    o_ref[...] = _ln_rows(ALPHA * x_ref[...] + mix, g_ref[...], b_ref[...])


def _outproj(y_nsa, y_ml, x2d, w_out, g, b, tm):
    T = x2d.shape[0]
    wn = w_out[:512].reshape(NSA_GROUPS, NSA_HPG, NSA_DK, D_MODEL).transpose(1, 0, 2, 3).reshape(512, D_MODEL)
    w = jnp.concatenate([wn, w_out[512:]], axis=0).astype(BF16)
    row = lambda wd: pl.BlockSpec((tm, wd), lambda i: (i, 0))
    full = lambda a: pl.BlockSpec(a.shape, lambda i: (0,) * a.ndim)
    g2, b2 = g.reshape(1, -1), b.reshape(1, -1)
    return pl.pallas_call(
        _outproj_kernel, grid=(T // tm,),
        in_specs=[row(512), row(512), row(D_MODEL), full(w), full(g2), full(b2)],
        out_specs=row(D_MODEL), out_shape=jax.ShapeDtypeStruct((T, D_MODEL), F32),
        compiler_params=_params(("parallel",)), name="outproj")(y_nsa, y_ml, x2d, w, g2, b2)


def _memkv_kernel(m_ref, w_ref, o_ref):
    o_ref[...] = jnp.dot(m_ref[...].astype(BF16), w_ref[...], preferred_element_type=F32).astype(o_ref.dtype)


def _memkv(mem2d, wk, wv):
    w = jnp.concatenate([wk, wv], axis=1).astype(BF16)
    R = mem2d.shape[0]
    return pl.pallas_call(
        _memkv_kernel, grid=(R // MEM_LEN,),
        in_specs=[pl.BlockSpec((MEM_LEN, D_MODEL), lambda i: (i, 0)),
                  pl.BlockSpec(w.shape, lambda i: (0, 0))],
        out_specs=pl.BlockSpec((MEM_LEN, 2 * D_MODEL), lambda i: (i, 0)),
        out_shape=jax.ShapeDtypeStruct((R, 2 * D_MODEL), BF16),
        compiler_params=_params(("parallel",)), name="memkv")(mem2d, w)


def _xattn_kernel(x_ref, kv_ref, wq_ref, wo_ref, g_ref, b_ref, rw_ref, x2_ref, x2p_ref, sc_ref):
    x1 = x_ref[...]
    q = jnp.dot(x1.astype(BF16), wq_ref[...], preferred_element_type=F32).astype(BF16)
    outs = []
    for h in range(XA_HEADS):
        qh = q[:, h * XA_DH:(h + 1) * XA_DH]
        kh = kv_ref[:, h * XA_DH:(h + 1) * XA_DH]
        vh = kv_ref[:, D_MODEL + h * XA_DH:D_MODEL + (h + 1) * XA_DH]
        s = lax.dot_general(qh, kh, _DN_T, preferred_element_type=F32) * (XA_DH ** -0.5)
        p = jnp.exp(s - jnp.max(s, axis=-1, keepdims=True))
        p = p / jnp.sum(p, axis=-1, keepdims=True)
        outs.append(jnp.dot(p.astype(BF16), vh, preferred_element_type=F32).astype(BF16))
    o = jnp.concatenate(outs, axis=1)
    xa = jnp.dot(o, wo_ref[...], preferred_element_type=F32)
    x2 = _ln_rows(ALPHA * x1 + xa, g_ref[...], b_ref[...])
    x2_ref[...] = x2
    x2p_ref[...] = _pack_bf16_pairs(x2)
    xh = x2.astype(BF16)
    xl = (x2 - xh.astype(F32)).astype(BF16)
    wh = rw_ref[0]
    wl = rw_ref[1]
    logit = (lax.dot_general(wh, xh, _DN_T, preferred_element_type=F32) +
             lax.dot_general(wh, xl, _DN_T, preferred_element_type=F32) +
             lax.dot_general(wl, xh, _DN_T, preferred_element_type=F32))
    sc_ref[...] = jax.nn.sigmoid(logit)


def _xattn(x1, kv, wq, wo, g, b, router_w, S, tq):
    T = x1.shape[0]
    wqb, wob = wq.astype(BF16), wo.astype(BF16)
    rwt = router_w.T
    rh = rwt.astype(BF16)
    rw = jnp.stack([rh, (rwt - rh.astype(F32)).astype(BF16)])
    g2, b2 = g.reshape(1, -1), b.reshape(1, -1)
    full = lambda a: pl.BlockSpec(a.shape, lambda i: (0,) * a.ndim)
    per = S // tq
    return pl.pallas_call(
        _xattn_kernel, grid=(T // tq,),
        in_specs=[pl.BlockSpec((tq, D_MODEL), lambda i: (i, 0)),
                  pl.BlockSpec((MEM_LEN, 2 * D_MODEL), lambda i: (i // per, 0)),
                  full(wqb), full(wob), full(g2), full(b2), full(rw)],
        out_specs=(pl.BlockSpec((tq, D_MODEL), lambda i: (i, 0)),
                   pl.BlockSpec((tq, D_MODEL // 2), lambda i: (i, 0)),
                   pl.BlockSpec((MOE_E, tq), lambda i: (0, i))),
        out_shape=(jax.ShapeDtypeStruct((T, D_MODEL), F32), jax.ShapeDtypeStruct((T, D_MODEL // 2), I32),
                   jax.ShapeDtypeStruct((MOE_E, T), F32)),
        compiler_params=_params(("parallel",)), name="xattn")(x1, kv, wqb, wob, g2, b2, rw)


def _route_kernel(sc_ref, rb_ref, idx_ref, w_ref):
    E, G = MOE_E, MOE_GROUPS
    per = E // G
    scores = sc_ref[...]
    tr = scores.shape[1]
    biased = scores + rb_ref[...]
    g3 = biased.reshape(G, per, tr)
    j3 = lax.broadcasted_iota(I32, (G, per, tr), 1)
    m1 = jnp.max(g3, axis=1, keepdims=True)
    first = jnp.min(jnp.where(g3 == m1, j3, per), axis=1, keepdims=True)
    m2 = jnp.max(jnp.where(j3 == first, -jnp.inf, g3), axis=1, keepdims=True)
    gs = (m1 + m2).reshape(G, tr)
    gi = lax.broadcasted_iota(I32, (G, tr), 0)
    cnt = jnp.zeros((G, tr), F32)
    for m in range(G):
        row = gs[m:m + 1, :]
        cnt = cnt + jnp.where((row > gs) | ((row == gs) & (gi > m)), 1.0, 0.0)
    gmask = cnt < float(MOE_TOPK_GROUPS)
    masked = jnp.where(gmask[:, None, :], g3, NEG).reshape(E, tr)
    ei = lax.broadcasted_iota(I32, (E, tr), 0)
    idxs, ws = [], []
    for _ in range(MOE_K):
        mx = jnp.max(masked, axis=0, keepdims=True)
        ix = jnp.min(jnp.where(masked == mx, ei, E), axis=0, keepdims=True)
        hit = ei == ix
        ws.append(jnp.sum(jnp.where(hit, scores, 0.0), axis=0, keepdims=True))
        idxs.append(ix)
        masked = jnp.where(hit, -jnp.inf, masked)
    w = jnp.concatenate(ws, axis=0)
    idx_ref[...] = jnp.concatenate(idxs, axis=0)
    w_ref[...] = w / jnp.sum(w, axis=0, keepdims=True) * MOE_ROUTE_SCALE


def _route(scores_t, router_bias, tr):
    E, T = scores_t.shape
    rb = router_bias.reshape(E, 1)
    return pl.pallas_call(
        _route_kernel, grid=(T // tr,),
        in_specs=[pl.BlockSpec((E, tr), lambda i: (0, i)), pl.BlockSpec((E, 1), lambda i: (0, 0))],
        out_specs=(pl.BlockSpec((MOE_K, tr), lambda i: (0, i)), pl.BlockSpec((MOE_K, tr), lambda i: (0, i))),
        out_shape=(jax.ShapeDtypeStruct((MOE_K, T), I32), jax.ShapeDtypeStruct((MOE_K, T), F32)),
        compiler_params=_params(("parallel",)), name="route")(scores_t, rb)


def _rank_kernel(idx_ref, u_ref, rank_ref, cnt_ref, carry):
    E = MOE_E

    @pl.when(pl.program_id(0) == 0)
    def _():
        carry[...] = jnp.zeros_like(carry)

    idx = idx_ref[...]
    tp = idx.shape[1]
    ei = lax.broadcasted_iota(I32, (E, tp), 0)
    hits = [ei == idx[k:k + 1, :] for k in range(MOE_K)]
    onehot = jnp.zeros((E, tp), F32)
    for hit in hits:
        onehot = onehot + jnp.where(hit, 1.0, 0.0)
    pos = jnp.dot(onehot.astype(BF16), u_ref[...], preferred_element_type=F32) + carry[...]
    ranks = [jnp.sum(jnp.where(hit, pos, 0.0), axis=0, keepdims=True) for hit in hits]
    rank_ref[...] = jnp.concatenate(ranks, axis=0).astype(I32)
    total = carry[...] + jnp.sum(onehot, axis=1, keepdims=True)
    carry[...] = total
    cnt_ref[...] = jnp.broadcast_to(total, cnt_ref.shape).astype(I32)


def _rank(idx_t, tp):
    K, T = idx_t.shape
    u = jnp.triu(jnp.ones((tp, tp), F32), k=1).astype(BF16)
    rank, cnt = pl.pallas_call(
        _rank_kernel, grid=(T // tp,),
        in_specs=[pl.BlockSpec((K, tp), lambda i: (0, i)), pl.BlockSpec((tp, tp), lambda i: (0, 0))],
        out_specs=(pl.BlockSpec((K, tp), lambda i: (0, i)), pl.BlockSpec((MOE_E, LANES), lambda i: (0, 0))),
        out_shape=(jax.ShapeDtypeStruct((K, T), I32), jax.ShapeDtypeStruct((MOE_E, LANES), I32)),
        scratch_shapes=[pltpu.VMEM((MOE_E, 1), F32)],
        compiler_params=_params(("arbitrary",)), name="rank")(idx_t, u)
    return rank, cnt[:, 0]


def _dest_kernel(idx_ref, rank_ref, po_ref, dest_ref):
    idx = idx_ref[...]
    tp = idx.shape[1]
    ei = lax.broadcasted_iota(I32, (MOE_E, tp), 0)
    po = po_ref[...]
    base = [jnp.sum(jnp.where(ei == idx[k:k + 1, :], po, 0.0), axis=0, keepdims=True) for k in range(MOE_K)]
    dest_ref[...] = jnp.concatenate(base, axis=0).astype(I32) + rank_ref[...]


def _dest(idx_t, rank_t, poffs, tp):
    K, T = idx_t.shape
    po = poffs.astype(F32).reshape(MOE_E, 1)
    spec = pl.BlockSpec((K, tp), lambda i: (0, i))
    return pl.pallas_call(
        _dest_kernel, grid=(T // tp,),
        in_specs=[spec, spec, pl.BlockSpec((MOE_E, 1), lambda i: (0, 0))],
        out_specs=spec, out_shape=jax.ShapeDtypeStruct((K, T), I32),
        compiler_params=_params(("parallel",)), name="dest")(idx_t, rank_t, po)


def _pack_bf16_pairs(v):
    m = v.shape[1] // 2
    bits = lax.bitcast_convert_type(v.astype(BF16).astype(F32), jnp.uint32)
    return lax.bitcast_convert_type((bits[:, :m] >> 16) | (bits[:, m:] & jnp.uint32(0xFFFF0000)), I32)


def _unpack_bf16_pairs(w):
    w = lax.bitcast_convert_type(w, jnp.uint32)
    lo = lax.bitcast_convert_type(w << 16, F32)
    hi = lax.bitcast_convert_type(w & jnp.uint32(0xFFFF0000), F32)
    return lo, hi


def _row_scatter(rows, dest_t, n_rows):
    T, d = rows.shape
    K = dest_t.shape[0]
    nw = SC_CORES * SC_SUBCORES
    per_w = T // nw
    ch = SC_GATHER_ROWS
    mesh = plsc.VectorSubcoreMesh(core_axis_name="c", subcore_axis_name="s")

    @functools.partial(
        pl.kernel, mesh=mesh, out_type=jax.ShapeDtypeStruct((n_rows, d), I32),
        scratch_types=[pltpu.VMEM((K, ch), I32), pltpu.VMEM((ch, d), I32), pltpu.SemaphoreType.DMA],
        name="row_scatter")
    def scatter(rows_hbm, dest_hbm, out_hbm, idx_v, rows_v, sem):
        wid = lax.axis_index("s") * SC_CORES + lax.axis_index("c")
        base = wid * per_w

        @pl.loop(0, per_w // ch)
        def _(i):
            off = pl.multiple_of(base + i * ch, ch)
            pltpu.sync_copy(rows_hbm.at[pl.ds(off, ch)], rows_v)
            pltpu.sync_copy(dest_hbm.at[:, pl.ds(off, ch)], idx_v)
            copies = [pltpu.async_copy(rows_v, out_hbm.at[idx_v.at[k]], sem) for k in range(K)]
            for cp in copies:
                cp.wait()

    return scatter(rows, dest_t)


RING_AHEAD = 3
RING_OUT = 3


def _expert_kernel(po_ref, cnt_ref, tot_ref, xs_hbm, w1_ref, w3_ref, w2_ref, ys_hbm,
                   xbuf, ybuf, w1b, w3b, w2b, insem, outsem):
    e = pl.program_id(0)
    n = cnt_ref[e]
    rb = xbuf.shape[1]
    ns = xbuf.shape[0]
    hw = D_MODEL // 2
    nblk = jnp.maximum((n + rb - 1) // rb, 1)
    g0 = po_ref[e] // rb
    total = tot_ref[0]
    w1b[...] = w1_ref[...].astype(BF16)
    w3b[...] = w3_ref[...].astype(BF16)
    w2b[...] = w2_ref[...].astype(BF16)

    def in_copy(g, slot):
        return pltpu.make_async_copy(xs_hbm.at[pl.ds(pl.multiple_of(g * rb, rb), rb)], xbuf.at[slot], insem.at[slot])

    def out_copy(g, slot):
        return pltpu.make_async_copy(ybuf.at[slot], ys_hbm.at[pl.ds(pl.multiple_of(g * rb, rb), rb)],
                                     outsem.at[slot])

    @pl.when(e == 0)
    def _():
        for d in range(RING_AHEAD):
            @pl.when(d < total)
            def _():
                in_copy(d, d).start()

    def body(j, c):
        g = g0 + j
        slot = g % ns
        oslot = g % RING_OUT

        @pl.when(g + RING_AHEAD < total)
        def _():
            in_copy(g + RING_AHEAD, (g + RING_AHEAD) % ns).start()

        in_copy(0, slot).wait()

        @pl.when(g >= RING_OUT)
        def _():
            out_copy(0, oslot).wait()

        words = xbuf[slot]
        row = j * rb + lax.broadcasted_iota(I32, words.shape, 0)
        lo, hi = _unpack_bf16_pairs(jnp.where(row < n, words, 0))
        lo, hi = lo.astype(BF16), hi.astype(BF16)
        a = (jnp.dot(lo, w1b[0:hw, :], preferred_element_type=F32) +
             jnp.dot(hi, w1b[hw:, :], preferred_element_type=F32))
        u = (jnp.dot(lo, w3b[0:hw, :], preferred_element_type=F32) +
             jnp.dot(hi, w3b[hw:, :], preferred_element_type=F32))
        h = (a * jax.nn.sigmoid(a) * u).astype(BF16)
        ybuf[oslot] = _pack_bf16_pairs(jnp.dot(h, w2b[...], preferred_element_type=F32))
        out_copy(g, oslot).start()
        return c

    lax.fori_loop(0, nblk, body, 0)

    @pl.when(e + 1 == pl.num_programs(0))
    def _():
        for i in range(RING_OUT):
            @pl.when(total > i)
            def _():
                out_copy(0, (total - 1 - i) % RING_OUT).wait()


def _experts(poffs, counts, total_blocks, xs, w1, w3, w2):
    n_rows, wp = xs.shape
    rb = ROW_BLOCK
    wspec = lambda shape: pl.BlockSpec((None,) + shape, lambda e, po, cn, tb: (e, 0, 0))
    gs = pltpu.PrefetchScalarGridSpec(
        num_scalar_prefetch=3, grid=(MOE_E,),
        in_specs=[pl.BlockSpec(memory_space=pl.ANY), wspec((D_MODEL, MOE_FF)), wspec((D_MODEL, MOE_FF)),
                  wspec((MOE_FF, D_MODEL))],
        out_specs=pl.BlockSpec(memory_space=pl.ANY),
        scratch_shapes=[pltpu.VMEM((RING_AHEAD + 1, rb, wp), I32), pltpu.VMEM((RING_OUT, rb, wp), I32),
                        pltpu.VMEM((D_MODEL, MOE_FF), BF16), pltpu.VMEM((D_MODEL, MOE_FF), BF16),
                        pltpu.VMEM((MOE_FF, D_MODEL), BF16),
                        pltpu.SemaphoreType.DMA((RING_AHEAD + 1,)), pltpu.SemaphoreType.DMA((RING_OUT,))])
    return pl.pallas_call(
        _expert_kernel, grid_spec=gs, out_shape=jax.ShapeDtypeStruct((n_rows, wp), I32),
        compiler_params=_params(("arbitrary",)), name="experts")(poffs, counts, total_blocks, xs, w1, w3, w2)


def _row_gather(table, idx):
    n, d = idx.shape[0], table.shape[1]
    nw = SC_CORES * SC_SUBCORES
    per_w = n // nw
    ch = SC_GATHER_ROWS // 2
    mesh = plsc.VectorSubcoreMesh(core_axis_name="c", subcore_axis_name="s")

    @functools.partial(
        pl.kernel, mesh=mesh, out_type=jax.ShapeDtypeStruct((n, d), I32),
        scratch_types=[pltpu.VMEM((ch,), I32), pltpu.VMEM((ch,), I32),
                       pltpu.VMEM((ch, d), I32), pltpu.VMEM((ch, d), I32),
                       pltpu.SemaphoreType.DMA, pltpu.SemaphoreType.DMA,
                       pltpu.SemaphoreType.DMA, pltpu.SemaphoreType.DMA],
        name="row_gather")
    def gather(table_hbm, idx_hbm, out_hbm, idx0, idx1, rows0, rows1, g0, g1, w0, w1):
        wid = lax.axis_index("s") * SC_CORES + lax.axis_index("c")
        base = wid * per_w

        @pl.loop(0, per_w // (2 * ch))
        def _(i):
            off0 = pl.multiple_of(base + 2 * i * ch, ch)
            off1 = pl.multiple_of(off0 + ch, ch)
            pltpu.sync_copy(idx_hbm.at[pl.ds(off0, ch)], idx0)
            c0 = pltpu.async_copy(table_hbm.at[idx0], rows0, g0)
            pltpu.sync_copy(idx_hbm.at[pl.ds(off1, ch)], idx1)
            c1 = pltpu.async_copy(table_hbm.at[idx1], rows1, g1)
            c0.wait()
            o0 = pltpu.async_copy(rows0, out_hbm.at[pl.ds(off0, ch)], w0)
            c1.wait()
            o1 = pltpu.async_copy(rows1, out_hbm.at[pl.ds(off1, ch)], w1)
            o0.wait()
            o1.wait()

    return gather(table, idx)


def _combine_kernel(w_ref, x_ref, ysg_ref, s1_ref, s3_ref, s2_ref, g_ref, b_ref, *rest):
    o_ref = rest[-1]
    x2 = x_ref[...]
    xb = x2.astype(BF16)
    a = jnp.dot(xb, s1_ref[...], preferred_element_type=F32)
    c = jnp.dot(xb, s3_ref[...], preferred_element_type=F32)
    shared = jnp.dot((a * jax.nn.sigmoid(a) * c).astype(BF16), s2_ref[...], preferred_element_type=F32)
    w = w_ref[...]
    hw = ysg_ref.shape[2]
    y_lo = jnp.zeros((x2.shape[0], hw), F32)
    y_hi = jnp.zeros((x2.shape[0], hw), F32)
    for k in range(MOE_K):
        lo, hi = _unpack_bf16_pairs(ysg_ref[k])
        y_lo = y_lo + w[:, k:k + 1] * lo
        y_hi = y_hi + w[:, k:k + 1] * hi
    y = shared + jnp.concatenate([y_lo, y_hi], axis=1)
    o_ref[...] = _ln_rows(ALPHA * x2 + y, g_ref[...], b_ref[...])


def _combine(w_nat, x2, ysg, prev, first_tile, sw1, sw3, sw2, g, b, tc):
    T = x2.shape[0]
    wp = ysg.shape[2]
    full = lambda a: pl.BlockSpec(a.shape, lambda i: (0,) * a.ndim)
    row = lambda wd: pl.BlockSpec((tc, wd), lambda i: (i + first_tile, 0))
    s1, s3, s2 = sw1.astype(BF16), sw3.astype(BF16), sw2.astype(BF16)
    g2, b2 = g.reshape(1, -1), b.reshape(1, -1)
    args = [w_nat, x2, ysg, s1, s3, s2, g2, b2]
    in_specs = [row(MOE_K), row(D_MODEL), pl.BlockSpec((MOE_K, tc, wp), lambda i: (0, i, 0)),
                full(s1), full(s3), full(s2), full(g2), full(b2)]
    aliases = {}
    if prev is not None:
        aliases = {len(args): 0}
        args.append(prev)
        in_specs.append(pl.BlockSpec(memory_space=pl.ANY))
    return pl.pallas_call(
        _combine_kernel, grid=(ysg.shape[1] // tc,), in_specs=in_specs,
        out_specs=row(D_MODEL), out_shape=jax.ShapeDtypeStruct((T, D_MODEL), F32),
        input_output_aliases=aliases,
        compiler_params=_params(("parallel",)), name="combine",
    )(*args)


def _moe(x2, x2p, scores_t, router_bias, w1, w3, w2, sw1, sw3, sw2, g, b, tiles):
    T = x2.shape[0]
    rb = ROW_BLOCK
    idx_t, w_t = _route(scores_t, router_bias, tiles['route'])
    rank_t, counts = _rank(idx_t, tiles['rank'])
    pcounts = jnp.maximum((counts + rb - 1) // rb, 1) * rb
    pends = jnp.cumsum(pcounts)
    poffs = (pends - pcounts).astype(I32)
    n_blocks = -(-T * MOE_K // rb) + MOE_E
    dest_t = _dest(idx_t, rank_t, poffs, tiles['rank'])
    xs = _row_scatter(x2p, dest_t, n_blocks * rb)
    total_blocks = (pends[-1:] // rb).astype(I32)
    ys = _experts(poffs, counts.astype(I32), total_blocks, xs, w1, w3, w2)
    tc = tiles['combine']
    n_split = 2 if T % (2 * tc) == 0 and (T // 2 * MOE_K) % (SC_CORES * SC_SUBCORES * SC_GATHER_ROWS) == 0 else 1
    th = T // n_split
    w_nat = w_t.T
    out = None
    for p in range(n_split):
        ysg = _row_gather(ys, dest_t[:, p * th:(p + 1) * th].reshape(-1)).reshape(MOE_K, th, -1)
        out = _combine(w_nat, x2, ysg, out, p * th // tc, sw1, sw3, sw2, g, b, tc)
    return out


def _tiles(B, S):
    T = B * S
    pick = lambda want, n: want if n % want == 0 else n
    return dict(proj=pick(512, T), nsa_q=pick(128, S), nsa_ck=pick(512, S), outproj=pick(1024, T),
                mlstm_nb=2 if B % 2 == 0 else 1, xattn=pick(1024, S), route=pick(1024, T), rank=pick(512, T),
                combine=pick(512, T))


def kernel(x, mem, w_in, nsa_pos_k, nsa_cmp_k_w1, nsa_cmp_k_w2, nsa_pos_v, nsa_cmp_v_w1, nsa_cmp_v_w2,
           mlstm_conv_w, mlstm_i_bias, mlstm_f_bias, mlstm_norm_g, w_out, ln1_g, ln1_b,
           xa_wq, xa_wk, xa_wv, xa_wo, ln2_g, ln2_b, router_w, router_bias,
           moe_w1, moe_w3, moe_w2, shared_w1, shared_w3, shared_w2, ln3_g, ln3_b):
    B, S, D = x.shape
    T = B * S
    tl = _tiles(B, S)
    xc = x.reshape(T, D)
    memc = mem.reshape(B * MEM_LEN, D)
    for l in range(w_in.shape[0]):
        (q, cmp, ksel, vsel, kwin, vwin, gates, mq, mk, mv, mo, mif) = _project(xc, _prep_w_in(w_in[l]), tl['proj'])
        kcvc = _compress(cmp, B, S, _prep_cmp(nsa_pos_k[l], nsa_cmp_k_w1[l], nsa_cmp_k_w2[l],
                                              nsa_pos_v[l], nsa_cmp_v_w1[l], nsa_cmp_v_w2[l]))
        y_nsa = _nsa(q, kcvc, ksel, vsel, kwin, vwin, gates, B, S, tl['nsa_q'], tl['nsa_ck'])
        y_ml = _mlstm(mq, mk, mv, mo, mif, mlstm_conv_w[l], mlstm_i_bias[l], mlstm_f_bias[l],
                      mlstm_norm_g[l], B, S, tl['mlstm_nb'])
        x1 = _outproj(y_nsa, y_ml, xc, w_out[l], ln1_g[l], ln1_b[l], tl['outproj'])
        kv = _memkv(memc, xa_wk[l], xa_wv[l])
        x2, x2p, scores_t = _xattn(x1, kv, xa_wq[l], xa_wo[l], ln2_g[l], ln2_b[l], router_w[l], S, tl['xattn'])
        xc = _moe(x2, x2p, scores_t, router_bias[l], moe_w1[l], moe_w3[l], moe_w2[l],
                  shared_w1[l], shared_w3[l], shared_w2[l], ln3_g[l], ln3_b[l], tl)
    return xc.reshape(B, S, D)
```

```python
import functools
import numpy as np
import jax
import jax.numpy as jnp
from jax import lax
from jax.experimental import pallas as pl
from jax.experimental.pallas import tpu as pltpu
from jax.experimental.pallas import tpu_sc as plsc

F32 = jnp.float32
BF16 = jnp.bfloat16
I32 = jnp.int32

D_MODEL = 1024
MEM_LEN = 256
NSA_HEADS = 8
NSA_GROUPS = 2
NSA_HPG = 4
NSA_DK = 64
NSA_CMP_LEN = 32
NSA_CMP_STRIDE = 16
NSA_SEL_BLOCK = 64
NSA_SEL_TOPN = 8
NSA_WINDOW = 512
ML_HEADS = 4
ML_DH = 128
ML_CHUNK = 64
ML_CONV = 4
XA_HEADS = 4
XA_DH = 256
MOE_E = 256
MOE_K = 8
MOE_GROUPS = 8
MOE_TOPK_GROUPS = 4
MOE_FF = 256
MOE_ROUTE_SCALE = 2.5
DEPTH = 1
ALPHA = (2.0 * DEPTH) ** 0.25
LN_EPS = 1e-5
NEG = -1e30
FORCE_BONUS = 1e4

LANES = 128
ROW_BLOCK = 512
VMEM_LIMIT = 56 * 1024 * 1024
SC_CORES = 2
SC_SUBCORES = 16
SC_GATHER_ROWS = 128

_DN_T = (((1,), (1,)), ((), ()))
_DN_TA = (((0,), (0,)), ((), ()))


def _params(sem):
    return pltpu.CompilerParams(dimension_semantics=sem, vmem_limit_bytes=VMEM_LIMIT)


def _ln_rows(v, g, b):
    mu = jnp.mean(v, axis=-1, keepdims=True)
    d = v - mu
    var = jnp.mean(d * d, axis=-1, keepdims=True)
    return d * lax.rsqrt(var + LN_EPS) * g + b


_SEGS = (('q', 512, BF16), ('cmp', 256, F32), ('ksel', 128, BF16), ('vsel', 128, BF16),
         ('kwin', 128, BF16), ('vwin', 128, BF16), ('gates', 128, F32), ('mq', 512, BF16),
         ('mk', 512, BF16), ('mv', 512, BF16), ('mo', 512, BF16), ('mif', 128, F32))


def _proj_kernel(x_ref, w_ref, *out_refs):
    xb = x_ref[...].astype(BF16)
    off = 0
    for o_ref, (_, wd, _) in zip(out_refs, _SEGS):
        o_ref[...] = jnp.dot(xb, w_ref[:, off:off + wd], preferred_element_type=F32).astype(o_ref.dtype)
        off += wd


def _prep_w_in(w):
    sizes = (512,) + (128,) * 6 + (24,) + (512,) * 4 + (4, 4)
    cuts = np.cumsum(sizes)[:-1].tolist()
    (wq, kc, vc, ks, vs, kw, vw, wg, mq, mk, mv, mo, mi, mf) = jnp.split(w, cuts, axis=1)
    wq = wq.reshape(D_MODEL, NSA_GROUPS, NSA_HPG, NSA_DK).transpose(0, 2, 1, 3).reshape(D_MODEL, 512)
    pad = lambda a: jnp.pad(a, ((0, 0), (0, LANES - a.shape[1])))
    segs = [wq, kc, vc, ks, vs, kw, vw, pad(wg), mq, mk, mv, mo, pad(jnp.concatenate([mi, mf], axis=1))]
    return jnp.concatenate(segs, axis=1).astype(BF16)


def _project(x2d, w_all, tm):
    T = x2d.shape[0]
    n = w_all.shape[1]
    out_shape = tuple(jax.ShapeDtypeStruct((T, wd), dt) for _, wd, dt in _SEGS)
    out_specs = tuple(pl.BlockSpec((tm, wd), lambda i: (i, 0)) for _, wd, _ in _SEGS)
    return pl.pallas_call(
        _proj_kernel, grid=(T // tm,),
        in_specs=[pl.BlockSpec((tm, D_MODEL), lambda i: (i, 0)),
                  pl.BlockSpec((D_MODEL, n), lambda i: (0, 0))],
        out_specs=out_specs, out_shape=out_shape,
        compiler_params=_params(("parallel",)), name="proj")(x2d, w_all)


def _cmp_kernel(r_ref, pa_ref, pb_ref, wa_ref, wb_ref, w2_ref, o_ref):
    r = r_ref[...]
    a = jnp.dot((r + pa_ref[...]).astype(BF16), wa_ref[...], preferred_element_type=F32)
    b = jnp.dot((r + pb_ref[...]).astype(BF16), wb_ref[...], preferred_element_type=F32)
    nr = r.shape[0]
    hid = a + pltpu.roll(b, nr - 1, 0)
    hid = hid * jax.nn.sigmoid(hid)
    out = jnp.dot(hid.astype(BF16), w2_ref[...], preferred_element_type=F32)
    row = lax.broadcasted_iota(I32, out.shape, 0)
    o_ref[...] = jnp.where(row < nr - 1, out, 0.0).astype(o_ref.dtype)


def _prep_cmp(pos_k, w1_k, w2_k, pos_v, w1_v, w2_v):
    eye = jnp.eye(NSA_GROUPS, dtype=F32)

    def expand_w1(w1, half):
        w = w1.reshape(NSA_CMP_LEN, NSA_DK, NSA_DK)[half * 16:(half + 1) * 16]
        return jnp.einsum('jde,gk->jgdke', w, eye).reshape(16, 128, 128)

    def both(fk, fv):
        z = jnp.zeros_like(fk)
        top = jnp.concatenate([fk, z], axis=-1)
        bot = jnp.concatenate([z, fv], axis=-1)
        return jnp.concatenate([top, bot], axis=-2)

    wa = both(expand_w1(w1_k, 0), expand_w1(w1_v, 0)).reshape(16 * 256, 256).astype(BF16)
    wb = both(expand_w1(w1_k, 1), expand_w1(w1_v, 1)).reshape(16 * 256, 256).astype(BF16)
    w2 = both(jnp.kron(eye, w2_k), jnp.kron(eye, w2_v)).astype(BF16)

    def pos_row(half):
        pk = jnp.tile(pos_k[half * 16:(half + 1) * 16], (1, NSA_GROUPS))
        pv = jnp.tile(pos_v[half * 16:(half + 1) * 16], (1, NSA_GROUPS))
        return jnp.concatenate([pk, pv], axis=1).reshape(1, 16 * 256)

    return pos_row(0), pos_row(1), wa, wb, w2


def _compress(cmp2d, B, S, prep):
    pa, pb, wa, wb, w2 = prep
    nr = S // NSA_CMP_STRIDE
    r = cmp2d.reshape(B, nr, NSA_CMP_STRIDE * 256)
    full = lambda a: pl.BlockSpec(a.shape, lambda b: (0,) * a.ndim)
    return pl.pallas_call(
        _cmp_kernel, grid=(B,),
        in_specs=[pl.BlockSpec((None, nr, NSA_CMP_STRIDE * 256), lambda b: (b, 0, 0)),
                  full(pa), full(pb), full(wa), full(wb), full(w2)],
        out_specs=pl.BlockSpec((None, nr, 256), lambda b: (b, 0, 0)),
        out_shape=jax.ShapeDtypeStruct((B, nr, 256), BF16),
        compiler_params=_params(("parallel",)), name="cmp")(r, pa, pb, wa, wb, w2)


def _nsa_consts(S):
    n_cmp = (S - NSA_CMP_LEN) // NSA_CMP_STRIDE + 1
    n_sel = S // NSA_SEL_BLOCK
    cs = np.arange(n_cmp) * NSA_CMP_STRIDE
    ss = np.arange(n_sel) * NSA_SEL_BLOCK
    ov = ((cs[:, None] < ss[None, :] + NSA_SEL_BLOCK) & (cs[:, None] + NSA_CMP_LEN > ss[None, :]))
    ovt = np.zeros((LANES, S // NSA_CMP_STRIDE), np.float32)
    ovt[:n_sel, :n_cmp] = ov.T
    e = np.zeros((LANES, S), np.float32)
    e[np.arange(S) // NSA_SEL_BLOCK, np.arange(S)] = 1.0
    return jnp.asarray(ovt, BF16), jnp.asarray(e, BF16)


def _nsa_kernel(q_ref, kcvc_ref, ksel_ref, vsel_ref, kwin_ref, vwin_ref, gates_ref, ovt_ref, e_ref,
                y_ref, bias_ref, *, tq, ck, n_sel):
    G, H = NSA_GROUPS, NSA_HPG
    GH = G * H
    M = GH * tq
    S = ksel_ref.shape[0]
    W = NSA_WINDOW
    ws = min(W + tq, S)
    t0 = pl.program_id(1) * tq
    gates = jax.nn.sigmoid(gates_ref[...])
    lane = lax.broadcasted_iota(I32, (tq, LANES), 1)
    t_col = t0 + lax.broadcasted_iota(I32, (tq, 1), 0)
    kc = kcvc_ref[:, 0:LANES]
    vc = kcvc_ref[:, LANES:2 * LANES]
    nc = kc.shape[0]
    qs = []
    for g in range(G):
        gmask = (lane // NSA_DK) == g
        for h in range(H):
            qh = q_ref[:, h * LANES:(h + 1) * LANES] * (NSA_DK ** -0.5)
            qs.append(jnp.where(gmask, qh, jnp.zeros_like(qh)))
    Q = jnp.concatenate(qs, axis=0).astype(BF16)

    s = lax.dot_general(Q, kc, _DN_T, preferred_element_type=F32)
    c_idx = lax.broadcasted_iota(I32, (tq, nc), 1)
    cmask = (c_idx * NSA_CMP_STRIDE + NSA_CMP_LEN - 1) <= t_col
    s3 = jnp.where(cmask[None], s.reshape(GH, tq, nc), NEG)
    p = jnp.exp(s3 - jnp.max(s3, axis=-1, keepdims=True))
    p = p / jnp.sum(p, axis=-1, keepdims=True)
    p = jnp.where(cmask[None], p, 0.0)
    o_cmp = jnp.dot(p.reshape(M, nc).astype(BF16), vc, preferred_element_type=F32).reshape(GH, tq, LANES)

    for g in range(G):
        psum = jnp.sum(p[g * H:(g + 1) * H], axis=0)
        hi = psum.astype(BF16)
        lo = (psum - hi.astype(F32)).astype(BF16)
        ovt = ovt_ref[...]
        pslt = (lax.dot_general(ovt, hi, _DN_T, preferred_element_type=F32) +
                lax.dot_general(ovt, lo, _DN_T, preferred_element_type=F32))
        imp_p = pslt[0:n_sel, :]
        n_i = lax.broadcasted_iota(I32, (n_sel, tq), 0)
        cur = (t0 + lax.broadcasted_iota(I32, (n_sel, tq), 1)) // NSA_SEL_BLOCK
        forced = (n_i == 0) | (n_i == cur) | (n_i == cur - 1)
        imp = jnp.where(n_i <= cur, imp_p + jnp.where(forced, FORCE_BONUS, 0.0), NEG)
        cnt = jnp.zeros((n_sel, tq), F32)
        for m in range(n_sel):
            row = imp[m:m + 1, :]
            beats = (row > imp) | ((row == imp) & (n_i > m))
            cnt = cnt + jnp.where(beats, 1.0, 0.0)
        selt = jnp.where(cnt < float(min(NSA_SEL_TOPN, n_sel)), 1.0, 0.0)
        selt = jnp.concatenate([selt, jnp.zeros((LANES - n_sel, tq), F32)], axis=0)
        sel = selt.T.astype(BF16)
        maskf = jnp.dot(sel, e_ref[...], preferred_element_type=F32)
        kpos = lax.broadcasted_iota(I32, (tq, S), 1)
        bias_ref[g] = jnp.where((maskf > 0.5) & (kpos <= t_col), 0.0, NEG)

    vlane = lax.broadcasted_iota(I32, (1, LANES), 1) // NSA_DK

    def pv_with_sums(pb, v):
        outs = []
        for g in range(G):
            vg = jnp.where(vlane == g, v, jnp.ones_like(v))
            outs.append(jnp.dot(pb[g * H * tq:(g + 1) * H * tq], vg, preferred_element_type=F32))
        return jnp.concatenate(outs, axis=0)

    def normalise(acc):
        outs = []
        for r in range(GH):
            c = NSA_DK * (1 - r // H)
            outs.append(acc[r] / acc[r][:, c:c + 1])
        return outs

    def sel_body(j, carry):
        m_i, acc = carry
        ks = pl.multiple_of(j * ck, ck)
        k = ksel_ref[pl.ds(ks, ck), :]
        v = vsel_ref[pl.ds(ks, ck), :]
        sj = lax.dot_general(Q, k, _DN_T, preferred_element_type=F32)
        sj = (sj.reshape(G, H, tq, ck) + bias_ref[:, :, pl.ds(ks, ck)][:, None]).reshape(GH, tq, ck)
        m_new = jnp.maximum(m_i, jnp.max(sj, axis=-1, keepdims=True))
        a = jnp.exp(m_i - m_new)
        pj = jnp.exp((sj - m_new).astype(BF16)).reshape(M, ck)
        return m_new, a * acc + pv_with_sums(pj, v).reshape(GH, tq, LANES)

    init = (jnp.full((GH, tq, 1), NEG, F32), jnp.zeros((GH, tq, LANES), F32))
    _, acc = lax.fori_loop(0, (t0 + tq + ck - 1) // ck, sel_body, init)
    o_sel = normalise(acc)

    kst = pl.multiple_of(jnp.clip(t0 - W, 0, S - ws), LANES)
    kwn = kwin_ref[pl.ds(kst, ws), :]
    vwn = vwin_ref[pl.ds(kst, ws), :]
    sw = lax.dot_general(Q, kwn, _DN_T, preferred_element_type=F32)
    wpos = kst + lax.broadcasted_iota(I32, (tq, ws), 1)
    wmask = (wpos <= t_col) & (wpos > t_col - W)
    sw3 = jnp.where(wmask[None], sw.reshape(GH, tq, ws), NEG)
    pw = jnp.exp((sw3 - jnp.max(sw3, axis=-1, keepdims=True)).astype(BF16)).reshape(M, ws)
    o_win = normalise(pv_with_sums(pw, vwn).reshape(GH, tq, LANES))

    g0mask = lane < NSA_DK
    for h in range(H):
        o_g = []
        for g in range(G):
            r = g * H + h
            c0 = r * 3
            o_g.append(gates[:, c0:c0 + 1] * o_cmp[r] + gates[:, c0 + 1:c0 + 2] * o_sel[r] +
                       gates[:, c0 + 2:c0 + 3] * o_win[r])
        y_ref[:, h * LANES:(h + 1) * LANES] = jnp.where(g0mask, o_g[0], o_g[1]).astype(y_ref.dtype)


def _nsa(q, kcvc, ksel, vsel, kwin, vwin, gates, B, S, tq, ck):
    T = B * S
    nq = S // tq
    ovt, e = _nsa_consts(S)
    seq = lambda a: a.reshape(B, S, LANES)
    kv_spec = pl.BlockSpec((None, S, LANES), lambda b, i: (b, 0, 0))
    kern = functools.partial(_nsa_kernel, tq=tq, ck=ck, n_sel=S // NSA_SEL_BLOCK)
    return pl.pallas_call(
        kern, grid=(B, nq),
        in_specs=[pl.BlockSpec((tq, 512), lambda b, i: (b * nq + i, 0)),
                  pl.BlockSpec((None,) + kcvc.shape[1:], lambda b, i: (b, 0, 0)),
                  kv_spec, kv_spec, kv_spec, kv_spec,
                  pl.BlockSpec((tq, LANES), lambda b, i: (b * nq + i, 0)),
                  pl.BlockSpec(ovt.shape, lambda b, i: (0, 0)),
                  pl.BlockSpec(e.shape, lambda b, i: (0, 0))],
        out_specs=pl.BlockSpec((tq, 512), lambda b, i: (b * nq + i, 0)),
        out_shape=jax.ShapeDtypeStruct((T, 512), BF16),
        scratch_shapes=[pltpu.VMEM((NSA_GROUPS, tq, S), F32)],
        compiler_params=_params(("parallel", "parallel")), name="nsa",
    )(q, kcvc, seq(ksel), seq(vsel), seq(kwin), seq(vwin), gates, ovt, e)


def _mlstm_kernel(q_ref, k_ref, v_ref, o_ref, gn_ref, gt_ref, cw_ref, bn_ref, bt_ref, ng_ref, tri_ref,
                  y_ref, c_scr, n_scr):
    H, dh, L = ML_HEADS, ML_DH, ML_CHUNK
    nb, S = q_ref.shape[0], q_ref.shape[1]
    nchunk = S // L
    c_scr[...] = jnp.zeros_like(c_scr)
    n_scr[...] = jnp.zeros_like(n_scr)
    row = lax.broadcasted_iota(I32, (L, H * dh), 0)
    li = lax.broadcasted_iota(I32, (L, L), 0)
    mi = lax.broadcasted_iota(I32, (L, L), 1)
    causal = mi <= li
    tril = tri_ref[0]
    triu = tri_ref[1]
    hp = lax.Precision.HIGHEST

    def conv_silu(ref, bi, c, wofs):
        r0 = pl.multiple_of(c * L, L)
        rp = pl.multiple_of(jnp.maximum(c - 1, 0) * L, L)
        cur = ref[bi, pl.ds(r0, L), :].astype(F32)
        prev = ref[bi, pl.ds(rp, L), :].astype(F32) * jnp.where(c > 0, 1.0, 0.0)
        acc = cur * cw_ref[ML_CONV - 1:ML_CONV, wofs:wofs + H * dh]
        for j in range(1, ML_CONV):
            sh = jnp.where(row < j, pltpu.roll(prev, j, 0), pltpu.roll(cur, j, 0))
            acc = acc + sh * cw_ref[ML_CONV - 1 - j:ML_CONV - j, wofs:wofs + H * dh]
        return acc * jax.nn.sigmoid(acc)

    def body(c, m_state):
        r0 = pl.multiple_of(c * L, L)
        new_m = []
        for bi in range(nb):
            qa = conv_silu(q_ref, bi, c, 0) * (dh ** -0.5)
            ka = conv_silu(k_ref, bi, c, H * dh)
            va = v_ref[bi, pl.ds(r0, L), :]
            oa = o_ref[bi, pl.ds(r0, L), :].astype(F32)
            gn = gn_ref[bi, pl.ds(r0, L), :] + bn_ref[...]
            gt = gt_ref[bi, :, c, :] + bt_ref[...]
            lf_n = jax.nn.log_sigmoid(gn)
            lf_t = jax.nn.log_sigmoid(gt)
            b_n = jnp.dot(tril, lf_n, precision=hp, preferred_element_type=F32)
            b_t = jnp.dot(lf_t, triu, precision=hp, preferred_element_type=F32)
            for h in range(H):
                st = bi * H + h
                q = qa[:, h * dh:(h + 1) * dh]
                k = ka[:, h * dh:(h + 1) * dh]
                v = va[:, h * dh:(h + 1) * dh]
                m_old = m_state[st]
                b_col = b_n[:, H + h:H + h + 1]
                i_col = gn[:, h:h + 1]
                b_row = b_t[H + h:H + h + 1, :]
                i_row = gt[h:h + 1, :]
                g_tot = b_t[H + h:H + h + 1, L - 1:L]
                d_log = jnp.where(causal, b_col - b_row + i_row, NEG)
                inter = b_col + m_old
                m_q = jnp.maximum(inter, jnp.max(d_log, axis=-1, keepdims=True))
                w_intra = jnp.exp(d_log - m_q)
                w_inter = jnp.exp(inter - m_q)
                qb = q.astype(BF16)
                s = lax.dot_general(qb, k.astype(BF16), _DN_T, preferred_element_type=F32) * w_intra
                cst = c_scr[st]
                nst = n_scr[st]
                num = (w_inter * jnp.dot(qb, cst.astype(BF16), preferred_element_type=F32) +
                       jnp.dot(s.astype(BF16), v, preferred_element_type=F32))
                den = w_inter * jnp.sum(q * nst, axis=-1, keepdims=True) + jnp.sum(s, axis=-1, keepdims=True)
                hv = num / jnp.maximum(jnp.abs(den), jnp.exp(-m_q))
                log_k = g_tot - b_col + i_col
                m_new = jnp.maximum(g_tot + m_old, jnp.max(log_k, axis=0, keepdims=True))
                wk = jnp.exp(log_k - m_new)
                decay = jnp.exp(g_tot + m_old - m_new)
                kw = k * wk
                c_scr[st] = decay * cst + lax.dot_general(kw.astype(BF16), v, _DN_TA, preferred_element_type=F32)
                n_scr[st] = decay * nst + jnp.sum(kw, axis=0, keepdims=True)
                new_m.append(m_new)
                mu = jnp.mean(hv, axis=-1, keepdims=True)
                dv = hv - mu
                var = jnp.mean(dv * dv, axis=-1, keepdims=True)
                hn = dv * lax.rsqrt(var + LN_EPS) * ng_ref[:, h * dh:(h + 1) * dh]
                og = jax.nn.sigmoid(oa[:, h * dh:(h + 1) * dh])
                y_ref[bi, pl.ds(r0, L), h * dh:(h + 1) * dh] = (og * hn).astype(y_ref.dtype)
        return tuple(new_m)

    lax.fori_loop(0, nchunk, body, tuple(jnp.zeros((1, 1), F32) for _ in range(nb * H)))


def _mlstm(mq, mk, mv, mo, mif, conv_w, i_bias, f_bias, norm_g, B, S, nb):
    T = B * S
    H, dh, L = ML_HEADS, ML_DH, ML_CHUNK
    W = H * dh
    gt = mif[:, :2 * H].reshape(B, S, 2 * H).transpose(0, 2, 1).reshape(B, 2 * H, S // L, L)
    cw = conv_w.reshape(ML_CONV, 2 * W)
    bias = jnp.concatenate([i_bias, f_bias])
    bn = jnp.pad(bias, (0, LANES - 2 * H)).reshape(1, LANES)
    bt = bias.reshape(2 * H, 1)
    ng = norm_g.reshape(1, W)
    tri = jnp.stack([jnp.tril(jnp.ones((L, L), F32)), jnp.triu(jnp.ones((L, L), F32))])
    seq = lambda a: a.reshape(B, S, a.shape[1])
    rows = lambda w: pl.BlockSpec((nb, S, w), lambda b: (b, 0, 0))
    full = lambda a: pl.BlockSpec(a.shape, lambda b: (0,) * a.ndim)
    y = pl.pallas_call(
        _mlstm_kernel, grid=(B // nb,),
        in_specs=[rows(W), rows(W), rows(W), rows(W), rows(LANES),
                  pl.BlockSpec((nb, 2 * H, S // L, L), lambda b: (b, 0, 0, 0)),
                  full(cw), full(bn), full(bt), full(ng), full(tri)],
        out_specs=rows(W),
        out_shape=jax.ShapeDtypeStruct((B, S, W), BF16),
        scratch_shapes=[pltpu.VMEM((nb * H, dh, dh), F32), pltpu.VMEM((nb * H, 1, dh), F32)],
        compiler_params=_params(("parallel",)), name="mlstm",
    )(seq(mq), seq(mk), seq(mv), seq(mo), seq(mif), gt, cw, bn, bt, ng, tri)
    return y.reshape(T, W)


def _outproj_kernel(yn_ref, ym_ref, x_ref, w_ref, g_ref, b_ref, o_ref):
    mix = (jnp.dot(yn_ref[...], w_ref[0:512, :], preferred_element_type=F32) +
           jnp.dot(ym_ref[...], w_ref[512:1024, :], preferred_element_type=F32))
    o_ref[...] = _ln_rows(ALPHA * x_ref[...] + mix, g_ref[...], b_ref[...])


def _outproj(y_nsa, y_ml, x2d, w_out, g, b, tm):
    T = x2d.shape[0]
    wn = w_out[:512].reshape(NSA_GROUPS, NSA_HPG, NSA_DK, D_MODEL).transpose(1, 0, 2, 3).reshape(512, D_MODEL)
    w = jnp.concatenate([wn, w_out[512:]], axis=0).astype(BF16)
    row = lambda wd: pl.BlockSpec((tm, wd), lambda i: (i, 0))
    full = lambda a: pl.BlockSpec(a.shape, lambda i: (0,) * a.ndim)
    g2, b2 = g.reshape(1, -1), b.reshape(1, -1)
    return pl.pallas_call(
        _outproj_kernel, grid=(T // tm,),
        in_specs=[row(512), row(512), row(D_MODEL), full(w), full(g2), full(b2)],
        out_specs=row(D_MODEL), out_shape=jax.ShapeDtypeStruct((T, D_MODEL), F32),
        compiler_params=_params(("parallel",)), name="outproj")(y_nsa, y_ml, x2d, w, g2, b2)


def _memkv_kernel(m_ref, w_ref, o_ref):
    o_ref[...] = jnp.dot(m_ref[...].astype(BF16), w_ref[...], preferred_element_type=F32).astype(o_ref.dtype)


def _memkv(mem2d, wk, wv):
    w = jnp.concatenate([wk, wv], axis=1).astype(BF16)
    R = mem2d.shape[0]
    return pl.pallas_call(
        _memkv_kernel, grid=(R // MEM_LEN,),
        in_specs=[pl.BlockSpec((MEM_LEN, D_MODEL), lambda i: (i, 0)),
                  pl.BlockSpec(w.shape, lambda i: (0, 0))],
        out_specs=pl.BlockSpec((MEM_LEN, 2 * D_MODEL), lambda i: (i, 0)),
        out_shape=jax.ShapeDtypeStruct((R, 2 * D_MODEL), BF16),
        compiler_params=_params(("parallel",)), name="memkv")(mem2d, w)


def _xattn_kernel(x_ref, kv_ref, wq_ref, wo_ref, g_ref, b_ref, rw_ref, x2_ref, x2p_ref, sc_ref):
    x1 = x_ref[...]
    q = jnp.dot(x1.astype(BF16), wq_ref[...], preferred_element_type=F32).astype(BF16)
    outs = []
    for h in range(XA_HEADS):
        qh = q[:, h * XA_DH:(h + 1) * XA_DH]
        kh = kv_ref[:, h * XA_DH:(h + 1) * XA_DH]
        vh = kv_ref[:, D_MODEL + h * XA_DH:D_MODEL + (h + 1) * XA_DH]
        s = lax.dot_general(qh, kh, _DN_T, preferred_element_type=F32) * (XA_DH ** -0.5)
        p = jnp.exp(s - jnp.max(s, axis=-1, keepdims=True))
        p = p / jnp.sum(p, axis=-1, keepdims=True)
        outs.append(jnp.dot(p.astype(BF16), vh, preferred_element_type=F32).astype(BF16))
    o = jnp.concatenate(outs, axis=1)
    xa = jnp.dot(o, wo_ref[...], preferred_element_type=F32)
    x2 = _ln_rows(ALPHA * x1 + xa, g_ref[...], b_ref[...])
    x2_ref[...] = x2
    x2p_ref[...] = _pack_bf16_pairs(x2)
    xh = x2.astype(BF16)
    xl = (x2 - xh.astype(F32)).astype(BF16)
    wh = rw_ref[0]
    wl = rw_ref[1]
    logit = (lax.dot_general(wh, xh, _DN_T, preferred_element_type=F32) +
             lax.dot_general(wh, xl, _DN_T, preferred_element_type=F32) +
             lax.dot_general(wl, xh, _DN_T, preferred_element_type=F32))
    sc_ref[...] = jax.nn.sigmoid(logit)


def _xattn(x1, kv, wq, wo, g, b, router_w, S, tq):
    T = x1.shape[0]
    wqb, wob = wq.astype(BF16), wo.astype(BF16)
    rwt = router_w.T
    rh = rwt.astype(BF16)
    rw = jnp.stack([rh, (rwt - rh.astype(F32)).astype(BF16)])
    g2, b2 = g.reshape(1, -1), b.reshape(1, -1)
    full = lambda a: pl.BlockSpec(a.shape, lambda i: (0,) * a.ndim)
    per = S // tq
    return pl.pallas_call(
        _xattn_kernel, grid=(T // tq,),
        in_specs=[pl.BlockSpec((tq, D_MODEL), lambda i: (i, 0)),
                  pl.BlockSpec((MEM_LEN, 2 * D_MODEL), lambda i: (i // per, 0)),
                  full(wqb), full(wob), full(g2), full(b2), full(rw)],
        out_specs=(pl.BlockSpec((tq, D_MODEL), lambda i: (i, 0)),
                   pl.BlockSpec((tq, D_MODEL // 2), lambda i: (i, 0)),
                   pl.BlockSpec((MOE_E, tq), lambda i: (0, i))),
        out_shape=(jax.ShapeDtypeStruct((T, D_MODEL), F32), jax.ShapeDtypeStruct((T, D_MODEL // 2), I32),
                   jax.ShapeDtypeStruct((MOE_E, T), F32)),
        compiler_params=_params(("parallel",)), name="xattn")(x1, kv, wqb, wob, g2, b2, rw)


def _route_kernel(sc_ref, rb_ref, idx_ref, w_ref):
    E, G = MOE_E, MOE_GROUPS
    per = E // G
    scores = sc_ref[...]
    tr = scores.shape[1]
    biased = scores + rb_ref[...]
    g3 = biased.reshape(G, per, tr)
    j3 = lax.broadcasted_iota(I32, (G, per, tr), 1)
    m1 = jnp.max(g3, axis=1, keepdims=True)
    first = jnp.min(jnp.where(g3 == m1, j3, per), axis=1, keepdims=True)
    m2 = jnp.max(jnp.where(j3 == first, -jnp.inf, g3), axis=1, keepdims=True)
    gs = (m1 + m2).reshape(G, tr)
    gi = lax.broadcasted_iota(I32, (G, tr), 0)
    cnt = jnp.zeros((G, tr), F32)
    for m in range(G):
        row = gs[m:m + 1, :]
        cnt = cnt + jnp.where((row > gs) | ((row == gs) & (gi > m)), 1.0, 0.0)
    gmask = cnt < float(MOE_TOPK_GROUPS)
    masked = jnp.where(gmask[:, None, :], g3, NEG).reshape(E, tr)
    ei = lax.broadcasted_iota(I32, (E, tr), 0)
    idxs, ws = [], []
    for _ in range(MOE_K):
        mx = jnp.max(masked, axis=0, keepdims=True)
        ix = jnp.min(jnp.where(masked == mx, ei, E), axis=0, keepdims=True)
        hit = ei == ix
        ws.append(jnp.sum(jnp.where(hit, scores, 0.0), axis=0, keepdims=True))
        idxs.append(ix)
        masked = jnp.where(hit, -jnp.inf, masked)
    w = jnp.concatenate(ws, axis=0)
    idx_ref[...] = jnp.concatenate(idxs, axis=0)
    w_ref[...] = w / jnp.sum(w, axis=0, keepdims=True) * MOE_ROUTE_SCALE


def _route(scores_t, router_bias, tr):
    E, T = scores_t.shape
    rb = router_bias.reshape(E, 1)
    return pl.pallas_call(
        _route_kernel, grid=(T // tr,),
        in_specs=[pl.BlockSpec((E, tr), lambda i: (0, i)), pl.BlockSpec((E, 1), lambda i: (0, 0))],
        out_specs=(pl.BlockSpec((MOE_K, tr), lambda i: (0, i)), pl.BlockSpec((MOE_K, tr), lambda i: (0, i))),
        out_shape=(jax.ShapeDtypeStruct((MOE_K, T), I32), jax.ShapeDtypeStruct((MOE_K, T), F32)),
        compiler_params=_params(("parallel",)), name="route")(scores_t, rb)


def _rank_kernel(idx_ref, u_ref, rank_ref, cnt_ref, carry):
    E = MOE_E

    @pl.when(pl.program_id(0) == 0)
    def _():
        carry[...] = jnp.zeros_like(carry)

    idx = idx_ref[...]
    tp = idx.shape[1]
    ei = lax.broadcasted_iota(I32, (E, tp), 0)
    hits = [ei == idx[k:k + 1, :] for k in range(MOE_K)]
    onehot = jnp.zeros((E, tp), F32)
    for hit in hits:
        onehot = onehot + jnp.where(hit, 1.0, 0.0)
    pos = jnp.dot(onehot.astype(BF16), u_ref[...], preferred_element_type=F32) + carry[...]
    ranks = [jnp.sum(jnp.where(hit, pos, 0.0), axis=0, keepdims=True) for hit in hits]
    rank_ref[...] = jnp.concatenate(ranks, axis=0).astype(I32)
    total = carry[...] + jnp.sum(onehot, axis=1, keepdims=True)
    carry[...] = total
    cnt_ref[...] = jnp.broadcast_to(total, cnt_ref.shape).astype(I32)


def _rank(idx_t, tp):
    K, T = idx_t.shape
    u = jnp.triu(jnp.ones((tp, tp), F32), k=1).astype(BF16)
    rank, cnt = pl.pallas_call(
        _rank_kernel, grid=(T // tp,),
        in_specs=[pl.BlockSpec((K, tp), lambda i: (0, i)), pl.BlockSpec((tp, tp), lambda i: (0, 0))],
        out_specs=(pl.BlockSpec((K, tp), lambda i: (0, i)), pl.BlockSpec((MOE_E, LANES), lambda i: (0, 0))),
        out_shape=(jax.ShapeDtypeStruct((K, T), I32), jax.ShapeDtypeStruct((MOE_E, LANES), I32)),
        scratch_shapes=[pltpu.VMEM((MOE_E, 1), F32)],
        compiler_params=_params(("arbitrary",)), name="rank")(idx_t, u)
    return rank, cnt[:, 0]


def _dest_kernel(idx_ref, rank_ref, po_ref, dest_ref):
    idx = idx_ref[...]
    tp = idx.shape[1]
    ei = lax.broadcasted_iota(I32, (MOE_E, tp), 0)
    po = po_ref[...]
    base = [jnp.sum(jnp.where(ei == idx[k:k + 1, :], po, 0.0), axis=0, keepdims=True) for k in range(MOE_K)]
    dest_ref[...] = jnp.concatenate(base, axis=0).astype(I32) + rank_ref[...]


def _dest(idx_t, rank_t, poffs, tp):
    K, T = idx_t.shape
    po = poffs.astype(F32).reshape(MOE_E, 1)
    spec = pl.BlockSpec((K, tp), lambda i: (0, i))
    return pl.pallas_call(
        _dest_kernel, grid=(T // tp,),
        in_specs=[spec, spec, pl.BlockSpec((MOE_E, 1), lambda i: (0, 0))],
        out_specs=spec, out_shape=jax.ShapeDtypeStruct((K, T), I32),
        compiler_params=_params(("parallel",)), name="dest")(idx_t, rank_t, po)


def _pack_bf16_pairs(v):
    m = v.shape[1] // 2
    bits = lax.bitcast_convert_type(v.astype(BF16).astype(F32), jnp.uint32)
    return lax.bitcast_convert_type((bits[:, :m] >> 16) | (bits[:, m:] & jnp.uint32(0xFFFF0000)), I32)


def _unpack_bf16_pairs(w):
    w = lax.bitcast_convert_type(w, jnp.uint32)
    lo = lax.bitcast_convert_type(w << 16, F32)
    hi = lax.bitcast_convert_type(w & jnp.uint32(0xFFFF0000), F32)
    return lo, hi


def _row_scatter(rows, dest_t, n_rows):
    T, d = rows.shape
    K = dest_t.shape[0]
    nw = SC_CORES * SC_SUBCORES
    per_w = T // nw
    ch = SC_GATHER_ROWS
    mesh = plsc.VectorSubcoreMesh(core_axis_name="c", subcore_axis_name="s")

    @functools.partial(
        pl.kernel, mesh=mesh, out_type=jax.ShapeDtypeStruct((n_rows, d), I32),
        scratch_types=[pltpu.VMEM((K, ch), I32), pltpu.VMEM((ch, d), I32), pltpu.SemaphoreType.DMA],
        name="row_scatter")
    def scatter(rows_hbm, dest_hbm, out_hbm, idx_v, rows_v, sem):
        wid = lax.axis_index("s") * SC_CORES + lax.axis_index("c")
        base = wid * per_w

        @pl.loop(0, per_w // ch)
        def _(i):
            off = pl.multiple_of(base + i * ch, ch)
            pltpu.sync_copy(rows_hbm.at[pl.ds(off, ch)], rows_v)
            pltpu.sync_copy(dest_hbm.at[:, pl.ds(off, ch)], idx_v)
            copies = [pltpu.async_copy(rows_v, out_hbm.at[idx_v.at[k]], sem) for k in range(K)]
            for cp in copies:
                cp.wait()

    return scatter(rows, dest_t)


RING_AHEAD = 3
RING_OUT = 3


def _expert_kernel(po_ref, cnt_ref, tot_ref, xs_hbm, w1_ref, w3_ref, w2_ref, ys_hbm,
                   xbuf, ybuf, w1b, w3b, w2b, insem, outsem):
    e = pl.program_id(0)
    n = cnt_ref[e]
    rb = xbuf.shape[1]
    ns = xbuf.shape[0]
    hw = D_MODEL // 2
    nblk = jnp.maximum((n + rb - 1) // rb, 1)
    g0 = po_ref[e] // rb
    total = tot_ref[0]
    w1b[...] = w1_ref[...].astype(BF16)
    w3b[...] = w3_ref[...].astype(BF16)
    w2b[...] = w2_ref[...].astype(BF16)

    def in_copy(g, slot):
        return pltpu.make_async_copy(xs_hbm.at[pl.ds(pl.multiple_of(g * rb, rb), rb)], xbuf.at[slot], insem.at[slot])

    def out_copy(g, slot):
        return pltpu.make_async_copy(ybuf.at[slot], ys_hbm.at[pl.ds(pl.multiple_of(g * rb, rb), rb)],
                                     outsem.at[slot])

    @pl.when(e == 0)
    def _():
        for d in range(RING_AHEAD):
            @pl.when(d < total)
            def _():
                in_copy(d, d).start()

    def body(j, c):
        g = g0 + j
        slot = g % ns
        oslot = g % RING_OUT

        @pl.when(g + RING_AHEAD < total)
        def _():
            in_copy(g + RING_AHEAD, (g + RING_AHEAD) % ns).start()

        in_copy(0, slot).wait()

        @pl.when(g >= RING_OUT)
        def _():
            out_copy(0, oslot).wait()

        words = xbuf[slot]
        row = j * rb + lax.broadcasted_iota(I32, words.shape, 0)
        lo, hi = _unpack_bf16_pairs(jnp.where(row < n, words, 0))
        lo, hi = lo.astype(BF16), hi.astype(BF16)
        a = (jnp.dot(lo, w1b[0:hw, :], preferred_element_type=F32) +
             jnp.dot(hi, w1b[hw:, :], preferred_element_type=F32))
        u = (jnp.dot(lo, w3b[0:hw, :], preferred_element_type=F32) +
             jnp.dot(hi, w3b[hw:, :], preferred_element_type=F32))
        h = (a * jax.nn.sigmoid(a) * u).astype(BF16)
        ybuf[oslot] = _pack_bf16_pairs(jnp.dot(h, w2b[...], preferred_element_type=F32))
        out_copy(g, oslot).start()
        return c

    lax.fori_loop(0, nblk, body, 0)

    @pl.when(e + 1 == pl.num_programs(0))
    def _():
        for i in range(RING_OUT):
            @pl.when(total > i)
            def _():
                out_copy(0, (total - 1 - i) % RING_OUT).wait()


def _experts(poffs, counts, total_blocks, xs, w1, w3, w2):
    n_rows, wp = xs.shape
    rb = ROW_BLOCK
    wspec = lambda shape: pl.BlockSpec((None,) + shape, lambda e, po, cn, tb: (e, 0, 0))
    gs = pltpu.PrefetchScalarGridSpec(
        num_scalar_prefetch=3, grid=(MOE_E,),
        in_specs=[pl.BlockSpec(memory_space=pl.ANY), wspec((D_MODEL, MOE_FF)), wspec((D_MODEL, MOE_FF)),
                  wspec((MOE_FF, D_MODEL))],
        out_specs=pl.BlockSpec(memory_space=pl.ANY),
        scratch_shapes=[pltpu.VMEM((RING_AHEAD + 1, rb, wp), I32), pltpu.VMEM((RING_OUT, rb, wp), I32),
                        pltpu.VMEM((D_MODEL, MOE_FF), BF16), pltpu.VMEM((D_MODEL, MOE_FF), BF16),
                        pltpu.VMEM((MOE_FF, D_MODEL), BF16),
                        pltpu.SemaphoreType.DMA((RING_AHEAD + 1,)), pltpu.SemaphoreType.DMA((RING_OUT,))])
    return pl.pallas_call(
        _expert_kernel, grid_spec=gs, out_shape=jax.ShapeDtypeStruct((n_rows, wp), I32),
        compiler_params=_params(("arbitrary",)), name="experts")(poffs, counts, total_blocks, xs, w1, w3, w2)


def _row_gather(table, idx):
    n, d = idx.shape[0], table.shape[1]
    nw = SC_CORES * SC_SUBCORES
    per_w = n // nw
    ch = SC_GATHER_ROWS // 2
    mesh = plsc.VectorSubcoreMesh(core_axis_name="c", subcore_axis_name="s")

    @functools.partial(
        pl.kernel, mesh=mesh, out_type=jax.ShapeDtypeStruct((n, d), I32),
        scratch_types=[pltpu.VMEM((ch,), I32), pltpu.VMEM((ch,), I32),
                       pltpu.VMEM((ch, d), I32), pltpu.VMEM((ch, d), I32),
                       pltpu.SemaphoreType.DMA, pltpu.SemaphoreType.DMA,
                       pltpu.SemaphoreType.DMA, pltpu.SemaphoreType.DMA],
        name="row_gather")
    def gather(table_hbm, idx_hbm, out_hbm, idx0, idx1, rows0, rows1, g0, g1, w0, w1):
        wid = lax.axis_index("s") * SC_CORES + lax.axis_index("c")
        base = wid * per_w

        @pl.loop(0, per_w // (2 * ch))
        def _(i):
            off0 = pl.multiple_of(base + 2 * i * ch, ch)
            off1 = pl.multiple_of(off0 + ch, ch)
            pltpu.sync_copy(idx_hbm.at[pl.ds(off0, ch)], idx0)
            c0 = pltpu.async_copy(table_hbm.at[idx0], rows0, g0)
            pltpu.sync_copy(idx_hbm.at[pl.ds(off1, ch)], idx1)
            c1 = pltpu.async_copy(table_hbm.at[idx1], rows1, g1)
            c0.wait()
            o0 = pltpu.async_copy(rows0, out_hbm.at[pl.ds(off0, ch)], w0)
            c1.wait()
            o1 = pltpu.async_copy(rows1, out_hbm.at[pl.ds(off1, ch)], w1)
            o0.wait()
            o1.wait()

    return gather(table, idx)


def _combine_kernel(w_ref, x_ref, ysg_ref, s1_ref, s3_ref, s2_ref, g_ref, b_ref, *rest):
    o_ref = rest[-1]
    x2 = x_ref[...]
    xb = x2.astype(BF16)
    a = jnp.dot(xb, s1_ref[...], preferred_element_type=F32)
    c = jnp.dot(xb, s3_ref[...], preferred_element_type=F32)
    shared = jnp.dot((a * jax.nn.sigmoid(a) * c).astype(BF16), s2_ref[...], preferred_element_type=F32)
    w = w_ref[...]
    hw = ysg_ref.shape[2]
    y_lo = jnp.zeros((x2.shape[0], hw), F32)
    y_hi = jnp.zeros((x2.shape[0], hw), F32)
    for k in range(MOE_K):
        lo, hi = _unpack_bf16_pairs(ysg_ref[k])
        y_lo = y_lo + w[:, k:k + 1] * lo
        y_hi = y_hi + w[:, k:k + 1] * hi
    y = shared + jnp.concatenate([y_lo, y_hi], axis=1)
    o_ref[...] = _ln_rows(ALPHA * x2 + y, g_ref[...], b_ref[...])


def _combine(w_nat, x2, ysg, prev, first_tile, sw1, sw3, sw2, g, b, tc):
    T = x2.shape[0]
    wp = ysg.shape[2]
    full = lambda a: pl.BlockSpec(a.shape, lambda i: (0,) * a.ndim)
    row = lambda wd: pl.BlockSpec((tc, wd), lambda i: (i + first_tile, 0))
    s1, s3, s2 = sw1.astype(BF16), sw3.astype(BF16), sw2.astype(BF16)
    g2, b2 = g.reshape(1, -1), b.reshape(1, -1)
    args = [w_nat, x2, ysg, s1, s3, s2, g2, b2]
    in_specs = [row(MOE_K), row(D_MODEL), pl.BlockSpec((MOE_K, tc, wp), lambda i: (0, i, 0)),
                full(s1), full(s3), full(s2), full(g2), full(b2)]
    aliases = {}
    if prev is not None:
        aliases = {len(args): 0}
        args.append(prev)
        in_specs.append(pl.BlockSpec(memory_space=pl.ANY))
    return pl.pallas_call(
        _combine_kernel, grid=(ysg.shape[1] // tc,), in_specs=in_specs,
        out_specs=row(D_MODEL), out_shape=jax.ShapeDtypeStruct((T, D_MODEL), F32),
        input_output_aliases=aliases,
        compiler_params=_params(("parallel",)), name="combine",
    )(*args)


def _moe(x2, x2p, scores_t, router_bias, w1, w3, w2, sw1, sw3, sw2, g, b, tiles):
    T = x2.shape[0]
    rb = ROW_BLOCK
    idx_t, w_t = _route(scores_t, router_bias, tiles['route'])
    rank_t, counts = _rank(idx_t, tiles['rank'])
    pcounts = jnp.maximum((counts + rb - 1) // rb, 1) * rb
    pends = jnp.cumsum(pcounts)
    poffs = (pends - pcounts).astype(I32)
    n_blocks = -(-T * MOE_K // rb) + MOE_E
    dest_t = _dest(idx_t, rank_t, poffs, tiles['rank'])
    xs = _row_scatter(x2p, dest_t, n_blocks * rb)
    total_blocks = (pends[-1:] // rb).astype(I32)
    ys = _experts(poffs, counts.astype(I32), total_blocks, xs, w1, w3, w2)
    tc = tiles['combine']
    gather_unit = SC_CORES * SC_SUBCORES * SC_GATHER_ROWS
    n_split = next((s for s in (4, 2) if T % (s * tc) == 0 and (T // s * MOE_K) % gather_unit == 0), 1)
    th = T // n_split
    w_nat = w_t.T
    out = None
    for p in range(n_split):
        ysg = _row_gather(ys, dest_t[:, p * th:(p + 1) * th].reshape(-1)).reshape(MOE_K, th, -1)
        out = _combine(w_nat, x2, ysg, out, p * th // tc, sw1, sw3, sw2, g, b, tc)
    return out


def _tiles(B, S):
    T = B * S
    pick = lambda want, n: want if n % want == 0 else n
    return dict(proj=pick(512, T), nsa_q=pick(128, S), nsa_ck=pick(512, S), outproj=pick(1024, T),
                mlstm_nb=2 if B % 2 == 0 else 1, xattn=pick(1024, S), route=pick(1024, T), rank=pick(512, T),
                combine=pick(512, T))


def kernel(x, mem, w_in, nsa_pos_k, nsa_cmp_k_w1, nsa_cmp_k_w2, nsa_pos_v, nsa_cmp_v_w1, nsa_cmp_v_w2,
           mlstm_conv_w, mlstm_i_bias, mlstm_f_bias, mlstm_norm_g, w_out, ln1_g, ln1_b,
           xa_wq, xa_wk, xa_wv, xa_wo, ln2_g, ln2_b, router_w, router_bias,
           moe_w1, moe_w3, moe_w2, shared_w1, shared_w3, shared_w2, ln3_g, ln3_b):
    B, S, D = x.shape
    T = B * S
    tl = _tiles(B, S)
    xc = x.reshape(T, D)
    memc = mem.reshape(B * MEM_LEN, D)
    for l in range(w_in.shape[0]):
        (q, cmp, ksel, vsel, kwin, vwin, gates, mq, mk, mv, mo, mif) = _project(xc, _prep_w_in(w_in[l]), tl['proj'])
        kcvc = _compress(cmp, B, S, _prep_cmp(nsa_pos_k[l], nsa_cmp_k_w1[l], nsa_cmp_k_w2[l],
                                              nsa_pos_v[l], nsa_cmp_v_w1[l], nsa_cmp_v_w2[l]))
        y_nsa = _nsa(q, kcvc, ksel, vsel, kwin, vwin, gates, B, S, tl['nsa_q'], tl['nsa_ck'])
        y_ml = _mlstm(mq, mk, mv, mo, mif, mlstm_conv_w[l], mlstm_i_bias[l], mlstm_f_bias[l],
                      mlstm_norm_g[l], B, S, tl['mlstm_nb'])
        x1 = _outproj(y_nsa, y_ml, xc, w_out[l], ln1_g[l], ln1_b[l], tl['outproj'])
        kv = _memkv(memc, xa_wk[l], xa_wv[l])
        x2, x2p, scores_t = _xattn(x1, kv, xa_wq[l], xa_wo[l], ln2_g[l], ln2_b[l], router_w[l], S, tl['xattn'])
        xc = _moe(x2, x2p, scores_t, router_bias[l], moe_w1[l], moe_w3[l], moe_w2[l],
                  shared_w1[l], shared_w3[l], shared_w2[l], ln3_g[l], ln3_b[l], tl)
    return xc.reshape(B, S, D)
```

```python
import functools
import numpy as np
import jax
import jax.numpy as jnp
from jax import lax
from jax.experimental import pallas as pl
from jax.experimental.pallas import tpu as pltpu
from jax.experimental.pallas import tpu_sc as plsc

F32 = jnp.float32
BF16 = jnp.bfloat16
I32 = jnp.int32

D_MODEL = 1024
MEM_LEN = 256
NSA_HEADS = 8
NSA_GROUPS = 2
NSA_HPG = 4
NSA_DK = 64
NSA_CMP_LEN = 32
NSA_CMP_STRIDE = 16
NSA_SEL_BLOCK = 64
NSA_SEL_TOPN = 8
NSA_WINDOW = 512
ML_HEADS = 4
ML_DH = 128
ML_CHUNK = 64
ML_CONV = 4
XA_HEADS = 4
XA_DH = 256
MOE_E = 256
MOE_K = 8
MOE_GROUPS = 8
MOE_TOPK_GROUPS = 4
MOE_FF = 256
MOE_ROUTE_SCALE = 2.5
DEPTH = 1
ALPHA = (2.0 * DEPTH) ** 0.25
LN_EPS = 1e-5
NEG = -1e30
FORCE_BONUS = 1e4

LANES = 128
ROW_BLOCK = 512
VMEM_LIMIT = 56 * 1024 * 1024
SC_CORES = 2
SC_SUBCORES = 16
SC_GATHER_ROWS = 128

_DN_T = (((1,), (1,)), ((), ()))
_DN_TA = (((0,), (0,)), ((), ()))


def _params(sem):
    return pltpu.CompilerParams(dimension_semantics=sem, vmem_limit_bytes=VMEM_LIMIT)


def _ln_rows(v, g, b):
    mu = jnp.mean(v, axis=-1, keepdims=True)
    d = v - mu
    var = jnp.mean(d * d, axis=-1, keepdims=True)
    return d * lax.rsqrt(var + LN_EPS) * g + b


_SEGS = (('q', 512, BF16), ('cmp', 256, F32), ('ksel', 128, BF16), ('vsel', 128, BF16),
         ('kwin', 128, BF16), ('vwin', 128, BF16), ('gates', 128, F32), ('mq', 512, BF16),
         ('mk', 512, BF16), ('mv', 512, BF16), ('mo', 512, BF16), ('mif', 128, F32))


def _proj_kernel(x_ref, w_ref, *out_refs):
    xb = x_ref[...].astype(BF16)
    off = 0
    for o_ref, (_, wd, _) in zip(out_refs, _SEGS):
        o_ref[...] = jnp.dot(xb, w_ref[:, off:off + wd], preferred_element_type=F32).astype(o_ref.dtype)
        off += wd


def _prep_w_in(w):
    sizes = (512,) + (128,) * 6 + (24,) + (512,) * 4 + (4, 4)
    cuts = np.cumsum(sizes)[:-1].tolist()
    (wq, kc, vc, ks, vs, kw, vw, wg, mq, mk, mv, mo, mi, mf) = jnp.split(w, cuts, axis=1)
    wq = wq.reshape(D_MODEL, NSA_GROUPS, NSA_HPG, NSA_DK).transpose(0, 2, 1, 3).reshape(D_MODEL, 512)
    pad = lambda a: jnp.pad(a, ((0, 0), (0, LANES - a.shape[1])))
    segs = [wq, kc, vc, ks, vs, kw, vw, pad(wg), mq, mk, mv, mo, pad(jnp.concatenate([mi, mf], axis=1))]
    return jnp.concatenate(segs, axis=1).astype(BF16)


def _project(x2d, w_all, tm):
    T = x2d.shape[0]
    n = w_all.shape[1]
    out_shape = tuple(jax.ShapeDtypeStruct((T, wd), dt) for _, wd, dt in _SEGS)
    out_specs = tuple(pl.BlockSpec((tm, wd), lambda i: (i, 0)) for _, wd, _ in _SEGS)
    return pl.pallas_call(
        _proj_kernel, grid=(T // tm,),
        in_specs=[pl.BlockSpec((tm, D_MODEL), lambda i: (i, 0)),
                  pl.BlockSpec((D_MODEL, n), lambda i: (0, 0))],
        out_specs=out_specs, out_shape=out_shape,
        compiler_params=_params(("parallel",)), name="proj")(x2d, w_all)


def _cmp_kernel(r_ref, pa_ref, pb_ref, wa_ref, wb_ref, w2_ref, o_ref):
    r = r_ref[...]
    a = jnp.dot((r + pa_ref[...]).astype(BF16), wa_ref[...], preferred_element_type=F32)
    b = jnp.dot((r + pb_ref[...]).astype(BF16), wb_ref[...], preferred_element_type=F32)
    nr = r.shape[0]
    hid = a + pltpu.roll(b, nr - 1, 0)
    hid = hid * jax.nn.sigmoid(hid)
    out = jnp.dot(hid.astype(BF16), w2_ref[...], preferred_element_type=F32)
    row = lax.broadcasted_iota(I32, out.shape, 0)
    o_ref[...] = jnp.where(row < nr - 1, out, 0.0).astype(o_ref.dtype)


def _prep_cmp(pos_k, w1_k, w2_k, pos_v, w1_v, w2_v):
    eye = jnp.eye(NSA_GROUPS, dtype=F32)

    def expand_w1(w1, half):
        w = w1.reshape(NSA_CMP_LEN, NSA_DK, NSA_DK)[half * 16:(half + 1) * 16]
        return jnp.einsum('jde,gk->jgdke', w, eye).reshape(16, 128, 128)

    def both(fk, fv):
        z = jnp.zeros_like(fk)
        top = jnp.concatenate([fk, z], axis=-1)
        bot = jnp.concatenate([z, fv], axis=-1)
        return jnp.concatenate([top, bot], axis=-2)

    wa = both(expand_w1(w1_k, 0), expand_w1(w1_v, 0)).reshape(16 * 256, 256).astype(BF16)
    wb = both(expand_w1(w1_k, 1), expand_w1(w1_v, 1)).reshape(16 * 256, 256).astype(BF16)
    w2 = both(jnp.kron(eye, w2_k), jnp.kron(eye, w2_v)).astype(BF16)

    def pos_row(half):
        pk = jnp.tile(pos_k[half * 16:(half + 1) * 16], (1, NSA_GROUPS))
        pv = jnp.tile(pos_v[half * 16:(half + 1) * 16], (1, NSA_GROUPS))
        return jnp.concatenate([pk, pv], axis=1).reshape(1, 16 * 256)

    return pos_row(0), pos_row(1), wa, wb, w2


def _compress(cmp2d, B, S, prep):
    pa, pb, wa, wb, w2 = prep
    nr = S // NSA_CMP_STRIDE
    r = cmp2d.reshape(B, nr, NSA_CMP_STRIDE * 256)
    full = lambda a: pl.BlockSpec(a.shape, lambda b: (0,) * a.ndim)
    return pl.pallas_call(
        _cmp_kernel, grid=(B,),
        in_specs=[pl.BlockSpec((None, nr, NSA_CMP_STRIDE * 256), lambda b: (b, 0, 0)),
                  full(pa), full(pb), full(wa), full(wb), full(w2)],
        out_specs=pl.BlockSpec((None, nr, 256), lambda b: (b, 0, 0)),
        out_shape=jax.ShapeDtypeStruct((B, nr, 256), BF16),
        compiler_params=_params(("parallel",)), name="cmp")(r, pa, pb, wa, wb, w2)


def _nsa_consts(S):
    n_cmp = (S - NSA_CMP_LEN) // NSA_CMP_STRIDE + 1
    n_sel = S // NSA_SEL_BLOCK
    cs = np.arange(n_cmp) * NSA_CMP_STRIDE
    ss = np.arange(n_sel) * NSA_SEL_BLOCK
    ov = ((cs[:, None] < ss[None, :] + NSA_SEL_BLOCK) & (cs[:, None] + NSA_CMP_LEN > ss[None, :]))
    ovt = np.zeros((LANES, S // NSA_CMP_STRIDE), np.float32)
    ovt[:n_sel, :n_cmp] = ov.T
    e = np.zeros((LANES, S), np.float32)
    e[np.arange(S) // NSA_SEL_BLOCK, np.arange(S)] = 1.0
    return jnp.asarray(ovt, BF16), jnp.asarray(e, BF16)


def _nsa_kernel(q_ref, kcvc_ref, ksel_ref, vsel_ref, kwin_ref, vwin_ref, gates_ref, ovt_ref, e_ref,
                y_ref, bias_ref, *, tq, ck, n_sel):
    G, H = NSA_GROUPS, NSA_HPG
    GH = G * H
    M = GH * tq
    S = ksel_ref.shape[0]
    W = NSA_WINDOW
    ws = min(W + tq, S)
    t0 = pl.program_id(1) * tq
    gates = jax.nn.sigmoid(gates_ref[...])
    lane = lax.broadcasted_iota(I32, (tq, LANES), 1)
    t_col = t0 + lax.broadcasted_iota(I32, (tq, 1), 0)
    kc = kcvc_ref[:, 0:LANES]
    vc = kcvc_ref[:, LANES:2 * LANES]
    nc = kc.shape[0]
    qs = []
    for g in range(G):
        gmask = (lane // NSA_DK) == g
        for h in range(H):
            qh = q_ref[:, h * LANES:(h + 1) * LANES] * (NSA_DK ** -0.5)
            qs.append(jnp.where(gmask, qh, jnp.zeros_like(qh)))
    Q = jnp.concatenate(qs, axis=0).astype(BF16)

    s = lax.dot_general(Q, kc, _DN_T, preferred_element_type=F32)
    c_idx = lax.broadcasted_iota(I32, (tq, nc), 1)
    cmask = (c_idx * NSA_CMP_STRIDE + NSA_CMP_LEN - 1) <= t_col
    s3 = jnp.where(cmask[None], s.reshape(GH, tq, nc), NEG)
    p = jnp.exp(s3 - jnp.max(s3, axis=-1, keepdims=True))
    p = p / jnp.sum(p, axis=-1, keepdims=True)
    p = jnp.where(cmask[None], p, 0.0)
    o_cmp = jnp.dot(p.reshape(M, nc).astype(BF16), vc, preferred_element_type=F32).reshape(GH, tq, LANES)

    for g in range(G):
        psum = jnp.sum(p[g * H:(g + 1) * H], axis=0)
        hi = psum.astype(BF16)
        lo = (psum - hi.astype(F32)).astype(BF16)
        ovt = ovt_ref[...]
        pslt = (lax.dot_general(ovt, hi, _DN_T, preferred_element_type=F32) +
                lax.dot_general(ovt, lo, _DN_T, preferred_element_type=F32))
        imp_p = pslt[0:n_sel, :]
        n_i = lax.broadcasted_iota(I32, (n_sel, tq), 0)
        cur = (t0 + lax.broadcasted_iota(I32, (n_sel, tq), 1)) // NSA_SEL_BLOCK
        forced = (n_i == 0) | (n_i == cur) | (n_i == cur - 1)
        imp = jnp.where(n_i <= cur, imp_p + jnp.where(forced, FORCE_BONUS, 0.0), NEG)
        cnt = jnp.zeros((n_sel, tq), F32)
        for m in range(n_sel):
            row = imp[m:m + 1, :]
            beats = (row > imp) | ((row == imp) & (n_i > m))
            cnt = cnt + jnp.where(beats, 1.0, 0.0)
        selt = jnp.where(cnt < float(min(NSA_SEL_TOPN, n_sel)), 1.0, 0.0)
        selt = jnp.concatenate([selt, jnp.zeros((LANES - n_sel, tq), F32)], axis=0)
        sel = selt.T.astype(BF16)
        maskf = jnp.dot(sel, e_ref[...], preferred_element_type=F32)
        kpos = lax.broadcasted_iota(I32, (tq, S), 1)
        bias_ref[g] = jnp.where((maskf > 0.5) & (kpos <= t_col), 0.0, NEG)

    vlane = lax.broadcasted_iota(I32, (1, LANES), 1) // NSA_DK

    def pv_with_sums(pb, v):
        outs = []
        for g in range(G):
            vg = jnp.where(vlane == g, v, jnp.ones_like(v))
            outs.append(jnp.dot(pb[g * H * tq:(g + 1) * H * tq], vg, preferred_element_type=F32))
        return jnp.concatenate(outs, axis=0)

    def normalise(acc):
        outs = []
        for r in range(GH):
            c = NSA_DK * (1 - r // H)
            outs.append(acc[r] / acc[r][:, c:c + 1])
        return outs

    def sel_body(j, carry):
        m_i, acc = carry
        ks = pl.multiple_of(j * ck, ck)
        k = ksel_ref[pl.ds(ks, ck), :]
        v = vsel_ref[pl.ds(ks, ck), :]
        sj = lax.dot_general(Q, k, _DN_T, preferred_element_type=F32)
        sj = (sj.reshape(G, H, tq, ck) + bias_ref[:, :, pl.ds(ks, ck)][:, None]).reshape(GH, tq, ck)
        sb = sj.astype(BF16)
        m_new = jnp.maximum(m_i, jnp.max(sb, axis=-1, keepdims=True).astype(F32))
        a = jnp.exp(m_i - m_new)
        pj = jnp.exp(sb - m_new.astype(BF16)).reshape(M, ck)
        return m_new, a * acc + pv_with_sums(pj, v).reshape(GH, tq, LANES)

    init = (jnp.full((GH, tq, 1), NEG, F32), jnp.zeros((GH, tq, LANES), F32))
    _, acc = lax.fori_loop(0, (t0 + tq + ck - 1) // ck, sel_body, init)
    o_sel = normalise(acc)

    kst = pl.multiple_of(jnp.clip(t0 - W, 0, S - ws), LANES)
    kwn = kwin_ref[pl.ds(kst, ws), :]
    vwn = vwin_ref[pl.ds(kst, ws), :]
    sw = lax.dot_general(Q, kwn, _DN_T, preferred_element_type=F32)
    wpos = kst + lax.broadcasted_iota(I32, (tq, ws), 1)
    wmask = (wpos <= t_col) & (wpos > t_col - W)
    sw3 = jnp.where(wmask[None], sw.reshape(GH, tq, ws), NEG).astype(BF16)
    pw = jnp.exp(sw3 - jnp.max(sw3, axis=-1, keepdims=True)).reshape(M, ws)
    o_win = normalise(pv_with_sums(pw, vwn).reshape(GH, tq, LANES))

    g0mask = lane < NSA_DK
    for h in range(H):
        o_g = []
        for g in range(G):
            r = g * H + h
            c0 = r * 3
            o_g.append(gates[:, c0:c0 + 1] * o_cmp[r] + gates[:, c0 + 1:c0 + 2] * o_sel[r] +
                       gates[:, c0 + 2:c0 + 3] * o_win[r])
        y_ref[:, h * LANES:(h + 1) * LANES] = jnp.where(g0mask, o_g[0], o_g[1]).astype(y_ref.dtype)


def _nsa(q, kcvc, ksel, vsel, kwin, vwin, gates, B, S, tq, ck):
    T = B * S
    nq = S // tq
    ovt, e = _nsa_consts(S)
    seq = lambda a: a.reshape(B, S, LANES)
    kv_spec = pl.BlockSpec((None, S, LANES), lambda b, i: (b, 0, 0))
    kern = functools.partial(_nsa_kernel, tq=tq, ck=ck, n_sel=S // NSA_SEL_BLOCK)
    return pl.pallas_call(
        kern, grid=(B, nq),
        in_specs=[pl.BlockSpec((tq, 512), lambda b, i: (b * nq + i, 0)),
                  pl.BlockSpec((None,) + kcvc.shape[1:], lambda b, i: (b, 0, 0)),
                  kv_spec, kv_spec, kv_spec, kv_spec,
                  pl.BlockSpec((tq, LANES), lambda b, i: (b * nq + i, 0)),
                  pl.BlockSpec(ovt.shape, lambda b, i: (0, 0)),
                  pl.BlockSpec(e.shape, lambda b, i: (0, 0))],
        out_specs=pl.BlockSpec((tq, 512), lambda b, i: (b * nq + i, 0)),
        out_shape=jax.ShapeDtypeStruct((T, 512), BF16),
        scratch_shapes=[pltpu.VMEM((NSA_GROUPS, tq, S), F32)],
        compiler_params=_params(("parallel", "parallel")), name="nsa",
    )(q, kcvc, seq(ksel), seq(vsel), seq(kwin), seq(vwin), gates, ovt, e)


def _mlstm_kernel(q_ref, k_ref, v_ref, o_ref, gn_ref, gt_ref, cw_ref, bn_ref, bt_ref, ng_ref, tri_ref,
                  y_ref, c_scr, n_scr):
    H, dh, L = ML_HEADS, ML_DH, ML_CHUNK
    nb, S = q_ref.shape[0], q_ref.shape[1]
    nchunk = S // L
    c_scr[...] = jnp.zeros_like(c_scr)
    n_scr[...] = jnp.zeros_like(n_scr)
    row = lax.broadcasted_iota(I32, (L, H * dh), 0)
    li = lax.broadcasted_iota(I32, (L, L), 0)
    mi = lax.broadcasted_iota(I32, (L, L), 1)
    causal = mi <= li
    tril = tri_ref[0]
    triu = tri_ref[1]
    hp = lax.Precision.HIGHEST

    def conv_silu(ref, bi, c, wofs):
        r0 = pl.multiple_of(c * L, L)
        rp = pl.multiple_of(jnp.maximum(c - 1, 0) * L, L)
        cur = ref[bi, pl.ds(r0, L), :].astype(F32)
        prev = ref[bi, pl.ds(rp, L), :].astype(F32) * jnp.where(c > 0, 1.0, 0.0)
        acc = cur * cw_ref[ML_CONV - 1:ML_CONV, wofs:wofs + H * dh]
        for j in range(1, ML_CONV):
            sh = jnp.where(row < j, pltpu.roll(prev, j, 0), pltpu.roll(cur, j, 0))
            acc = acc + sh * cw_ref[ML_CONV - 1 - j:ML_CONV - j, wofs:wofs + H * dh]
        return acc * jax.nn.sigmoid(acc)

    def body(c, m_state):
        r0 = pl.multiple_of(c * L, L)
        new_m = []
        for bi in range(nb):
            qa = conv_silu(q_ref, bi, c, 0) * (dh ** -0.5)
            ka = conv_silu(k_ref, bi, c, H * dh)
            va = v_ref[bi, pl.ds(r0, L), :]
            oa = o_ref[bi, pl.ds(r0, L), :].astype(F32)
            gn = gn_ref[bi, pl.ds(r0, L), :] + bn_ref[...]
            gt = gt_ref[bi, :, c, :] + bt_ref[...]
            lf_n = jax.nn.log_sigmoid(gn)
            lf_t = jax.nn.log_sigmoid(gt)
            b_n = jnp.dot(tril, lf_n, precision=hp, preferred_element_type=F32)
            b_t = jnp.dot(lf_t, triu, precision=hp, preferred_element_type=F32)
            for h in range(H):
                st = bi * H + h
                q = qa[:, h * dh:(h + 1) * dh]
                k = ka[:, h * dh:(h + 1) * dh]
                v = va[:, h * dh:(h + 1) * dh]
                m_old = m_state[st]
                b_col = b_n[:, H + h:H + h + 1]
                i_col = gn[:, h:h + 1]
                b_row = b_t[H + h:H + h + 1, :]
                i_row = gt[h:h + 1, :]
                g_tot = b_t[H + h:H + h + 1, L - 1:L]
                d_log = jnp.where(causal, b_col - b_row + i_row, NEG)
                inter = b_col + m_old
                m_q = jnp.maximum(inter, jnp.max(d_log, axis=-1, keepdims=True))
                w_intra = jnp.exp(d_log - m_q)
                w_inter = jnp.exp(inter - m_q)
                qb = q.astype(BF16)
                s = lax.dot_general(qb, k.astype(BF16), _DN_T, preferred_element_type=F32) * w_intra
                cst = c_scr[st]
                nst = n_scr[st]
                num = (w_inter * jnp.dot(qb, cst.astype(BF16), preferred_element_type=F32) +
                       jnp.dot(s.astype(BF16), v, preferred_element_type=F32))
                den = w_inter * jnp.sum(q * nst, axis=-1, keepdims=True) + jnp.sum(s, axis=-1, keepdims=True)
                hv = num / jnp.maximum(jnp.abs(den), jnp.exp(-m_q))
                log_k = g_tot - b_col + i_col
                m_new = jnp.maximum(g_tot + m_old, jnp.max(log_k, axis=0, keepdims=True))
                wk = jnp.exp(log_k - m_new)
                decay = jnp.exp(g_tot + m_old - m_new)
                kw = k * wk
                c_scr[st] = decay * cst + lax.dot_general(kw.astype(BF16), v, _DN_TA, preferred_element_type=F32)
                n_scr[st] = decay * nst + jnp.sum(kw, axis=0, keepdims=True)
                new_m.append(m_new)
                mu = jnp.mean(hv, axis=-1, keepdims=True)
                dv = hv - mu
                var = jnp.mean(dv * dv, axis=-1, keepdims=True)
                hn = dv * lax.rsqrt(var + LN_EPS) * ng_ref[:, h * dh:(h + 1) * dh]
                og = jax.nn.sigmoid(oa[:, h * dh:(h + 1) * dh])
                y_ref[bi, pl.ds(r0, L), h * dh:(h + 1) * dh] = (og * hn).astype(y_ref.dtype)
        return tuple(new_m)

    lax.fori_loop(0, nchunk, body, tuple(jnp.zeros((1, 1), F32) for _ in range(nb * H)))


def _mlstm(mq, mk, mv, mo, mif, conv_w, i_bias, f_bias, norm_g, B, S, nb):
    T = B * S
    H, dh, L = ML_HEADS, ML_DH, ML_CHUNK
    W = H * dh
    gt = mif[:, :2 * H].reshape(B, S, 2 * H).transpose(0, 2, 1).reshape(B, 2 * H, S // L, L)
    cw = conv_w.reshape(ML_CONV, 2 * W)
    bias = jnp.concatenate([i_bias, f_bias])
    bn = jnp.pad(bias, (0, LANES - 2 * H)).reshape(1, LANES)
    bt = bias.reshape(2 * H, 1)
    ng = norm_g.reshape(1, W)
    tri = jnp.stack([jnp.tril(jnp.ones((L, L), F32)), jnp.triu(jnp.ones((L, L), F32))])
    seq = lambda a: a.reshape(B, S, a.shape[1])
    rows = lambda w: pl.BlockSpec((nb, S, w), lambda b: (b, 0, 0))
    full = lambda a: pl.BlockSpec(a.shape, lambda b: (0,) * a.ndim)
    y = pl.pallas_call(
        _mlstm_kernel, grid=(B // nb,),
        in_specs=[rows(W), rows(W), rows(W), rows(W), rows(LANES),
                  pl.BlockSpec((nb, 2 * H, S // L, L), lambda b: (b, 0, 0, 0)),
                  full(cw), full(bn), full(bt), full(ng), full(tri)],
        out_specs=rows(W),
        out_shape=jax.ShapeDtypeStruct((B, S, W), BF16),
        scratch_shapes=[pltpu.VMEM((nb * H, dh, dh), F32), pltpu.VMEM((nb * H, 1, dh), F32)],
        compiler_params=_params(("parallel",)), name="mlstm",
    )(seq(mq), seq(mk), seq(mv), seq(mo), seq(mif), gt, cw, bn, bt, ng, tri)
    return y.reshape(T, W)


def _outproj_kernel(yn_ref, ym_ref, x_ref, w_ref, g_ref, b_ref, o_ref):
    mix = (jnp.dot(yn_ref[...], w_ref[0:512, :], preferred_element_type=F32) +
           jnp.dot(ym_ref[...], w_ref[512:1024, :], preferred_element_type=F32))
    o_ref[...] = _ln_rows(ALPHA * x_ref[...] + mix, g_ref[...], b_ref[...])


def _outproj(y_nsa, y_ml, x2d, w_out, g, b, tm):
    T = x2d.shape[0]
    wn = w_out[:512].reshape(NSA_GROUPS, NSA_HPG, NSA_DK, D_MODEL).transpose(1, 0, 2, 3).reshape(512, D_MODEL)
    w = jnp.concatenate([wn, w_out[512:]], axis=0).astype(BF16)
    row = lambda wd: pl.BlockSpec((tm, wd), lambda i: (i, 0))
    full = lambda a: pl.BlockSpec(a.shape, lambda i: (0,) * a.ndim)
    g2, b2 = g.reshape(1, -1), b.reshape(1, -1)
    return pl.pallas_call(
        _outproj_kernel, grid=(T // tm,),
        in_specs=[row(512), row(512), row(D_MODEL), full(w), full(g2), full(b2)],
        out_specs=row(D_MODEL), out_shape=jax.ShapeDtypeStruct((T, D_MODEL), F32),
        compiler_params=_params(("parallel",)), name="outproj")(y_nsa, y_ml, x2d, w, g2, b2)


def _memkv_kernel(m_ref, w_ref, o_ref):
    o_ref[...] = jnp.dot(m_ref[...].astype(BF16), w_ref[...], preferred_element_type=F32).astype(o_ref.dtype)


def _memkv(mem2d, wk, wv):
    w = jnp.concatenate([wk, wv], axis=1).astype(BF16)
    R = mem2d.shape[0]
    return pl.pallas_call(
        _memkv_kernel, grid=(R // MEM_LEN,),
        in_specs=[pl.BlockSpec((MEM_LEN, D_MODEL), lambda i: (i, 0)),
                  pl.BlockSpec(w.shape, lambda i: (0, 0))],
        out_specs=pl.BlockSpec((MEM_LEN, 2 * D_MODEL), lambda i: (i, 0)),
        out_shape=jax.ShapeDtypeStruct((R, 2 * D_MODEL), BF16),
        compiler_params=_params(("parallel",)), name="memkv")(mem2d, w)


def _xattn_kernel(x_ref, kv_ref, wq_ref, wo_ref, g_ref, b_ref, rw_ref, x2_ref, x2p_ref, sc_ref):
    x1 = x_ref[...]
    q = jnp.dot(x1.astype(BF16), wq_ref[...], preferred_element_type=F32).astype(BF16)
    outs = []
    for h in range(XA_HEADS):
        qh = q[:, h * XA_DH:(h + 1) * XA_DH]
        kh = kv_ref[:, h * XA_DH:(h + 1) * XA_DH]
        vh = kv_ref[:, D_MODEL + h * XA_DH:D_MODEL + (h + 1) * XA_DH]
        s = lax.dot_general(qh, kh, _DN_T, preferred_element_type=F32) * (XA_DH ** -0.5)
        p = jnp.exp(s - jnp.max(s, axis=-1, keepdims=True))
        p = p / jnp.sum(p, axis=-1, keepdims=True)
        outs.append(jnp.dot(p.astype(BF16), vh, preferred_element_type=F32).astype(BF16))
    o = jnp.concatenate(outs, axis=1)
    xa = jnp.dot(o, wo_ref[...], preferred_element_type=F32)
    x2 = _ln_rows(ALPHA * x1 + xa, g_ref[...], b_ref[...])
    x2_ref[...] = x2
    x2p_ref[...] = _pack_bf16_pairs(x2)
    xh = x2.astype(BF16)
    xl = (x2 - xh.astype(F32)).astype(BF16)
    wh = rw_ref[0]
    wl = rw_ref[1]
    logit = (lax.dot_general(wh, xh, _DN_T, preferred_element_type=F32) +
             lax.dot_general(wh, xl, _DN_T, preferred_element_type=F32) +
             lax.dot_general(wl, xh, _DN_T, preferred_element_type=F32))
    sc_ref[...] = jax.nn.sigmoid(logit)


def _xattn(x1, kv, wq, wo, g, b, router_w, S, tq):
    T = x1.shape[0]
    wqb, wob = wq.astype(BF16), wo.astype(BF16)
    rwt = router_w.T
    rh = rwt.astype(BF16)
    rw = jnp.stack([rh, (rwt - rh.astype(F32)).astype(BF16)])
    g2, b2 = g.reshape(1, -1), b.reshape(1, -1)
    full = lambda a: pl.BlockSpec(a.shape, lambda i: (0,) * a.ndim)
    per = S // tq
    return pl.pallas_call(
        _xattn_kernel, grid=(T // tq,),
        in_specs=[pl.BlockSpec((tq, D_MODEL), lambda i: (i, 0)),
                  pl.BlockSpec((MEM_LEN, 2 * D_MODEL), lambda i: (i // per, 0)),
                  full(wqb), full(wob), full(g2), full(b2), full(rw)],
        out_specs=(pl.BlockSpec((tq, D_MODEL), lambda i: (i, 0)),
                   pl.BlockSpec((tq, D_MODEL // 2), lambda i: (i, 0)),
                   pl.BlockSpec((MOE_E, tq), lambda i: (0, i))),
        out_shape=(jax.ShapeDtypeStruct((T, D_MODEL), F32), jax.ShapeDtypeStruct((T, D_MODEL // 2), I32),
                   jax.ShapeDtypeStruct((MOE_E, T), F32)),
        compiler_params=_params(("parallel",)), name="xattn")(x1, kv, wqb, wob, g2, b2, rw)


def _route_kernel(sc_ref, rb_ref, idx_ref, w_ref):
    E, G = MOE_E, MOE_GROUPS
    per = E // G
    scores = sc_ref[...]
    tr = scores.shape[1]
    biased = scores + rb_ref[...]
    g3 = biased.reshape(G, per, tr)
    j3 = lax.broadcasted_iota(I32, (G, per, tr), 1)
    m1 = jnp.max(g3, axis=1, keepdims=True)
    first = jnp.min(jnp.where(g3 == m1, j3, per), axis=1, keepdims=True)
    m2 = jnp.max(jnp.where(j3 == first, -jnp.inf, g3), axis=1, keepdims=True)
    gs = (m1 + m2).reshape(G, tr)
    gi = lax.broadcasted_iota(I32, (G, tr), 0)
    cnt = jnp.zeros((G, tr), F32)
    for m in range(G):
        row = gs[m:m + 1, :]
        cnt = cnt + jnp.where((row > gs) | ((row == gs) & (gi > m)), 1.0, 0.0)
    gmask = cnt < float(MOE_TOPK_GROUPS)
    masked = jnp.where(gmask[:, None, :], g3, NEG).reshape(E, tr)
    ei = lax.broadcasted_iota(I32, (E, tr), 0)
    idxs, ws = [], []
    for _ in range(MOE_K):
        mx = jnp.max(masked, axis=0, keepdims=True)
        ix = jnp.min(jnp.where(masked == mx, ei, E), axis=0, keepdims=True)
        hit = ei == ix
        ws.append(jnp.sum(jnp.where(hit, scores, 0.0), axis=0, keepdims=True))
        idxs.append(ix)
        masked = jnp.where(hit, -jnp.inf, masked)
    w = jnp.concatenate(ws, axis=0)
    idx_ref[...] = jnp.concatenate(idxs, axis=0)
    w_ref[...] = w / jnp.sum(w, axis=0, keepdims=True) * MOE_ROUTE_SCALE


def _route(scores_t, router_bias, tr):
    E, T = scores_t.shape
    rb = router_bias.reshape(E, 1)
    return pl.pallas_call(
        _route_kernel, grid=(T // tr,),
        in_specs=[pl.BlockSpec((E, tr), lambda i: (0, i)), pl.BlockSpec((E, 1), lambda i: (0, 0))],
        out_specs=(pl.BlockSpec((MOE_K, tr), lambda i: (0, i)), pl.BlockSpec((MOE_K, tr), lambda i: (0, i))),
        out_shape=(jax.ShapeDtypeStruct((MOE_K, T), I32), jax.ShapeDtypeStruct((MOE_K, T), F32)),
        compiler_params=_params(("parallel",)), name="route")(scores_t, rb)


def _rank_kernel(idx_ref, u_ref, rank_ref, cnt_ref, carry):
    E = MOE_E

    @pl.when(pl.program_id(0) == 0)
    def _():
        carry[...] = jnp.zeros_like(carry)

    idx = idx_ref[...]
    tp = idx.shape[1]
    ei = lax.broadcasted_iota(I32, (E, tp), 0)
    hits = [ei == idx[k:k + 1, :] for k in range(MOE_K)]
    onehot = jnp.zeros((E, tp), F32)
    for hit in hits:
        onehot = onehot + jnp.where(hit, 1.0, 0.0)
    pos = jnp.dot(onehot.astype(BF16), u_ref[...], preferred_element_type=F32) + carry[...]
    ranks = [jnp.sum(jnp.where(hit, pos, 0.0), axis=0, keepdims=True) for hit in hits]
    rank_ref[...] = jnp.concatenate(ranks, axis=0).astype(I32)
    total = carry[...] + jnp.sum(onehot, axis=1, keepdims=True)
    carry[...] = total
    cnt_ref[...] = jnp.broadcast_to(total, cnt_ref.shape).astype(I32)


def _rank(idx_t, tp):
    K, T = idx_t.shape
    u = jnp.triu(jnp.ones((tp, tp), F32), k=1).astype(BF16)
    rank, cnt = pl.pallas_call(
        _rank_kernel, grid=(T // tp,),
        in_specs=[pl.BlockSpec((K, tp), lambda i: (0, i)), pl.BlockSpec((tp, tp), lambda i: (0, 0))],
        out_specs=(pl.BlockSpec((K, tp), lambda i: (0, i)), pl.BlockSpec((MOE_E, LANES), lambda i: (0, 0))),
        out_shape=(jax.ShapeDtypeStruct((K, T), I32), jax.ShapeDtypeStruct((MOE_E, LANES), I32)),
        scratch_shapes=[pltpu.VMEM((MOE_E, 1), F32)],
        compiler_params=_params(("arbitrary",)), name="rank")(idx_t, u)
    return rank, cnt[:, 0]


def _dest_kernel(idx_ref, rank_ref, po_ref, dest_ref):
    idx = idx_ref[...]
    tp = idx.shape[1]
    ei = lax.broadcasted_iota(I32, (MOE_E, tp), 0)
    po = po_ref[...]
    base = [jnp.sum(jnp.where(ei == idx[k:k + 1, :], po, 0.0), axis=0, keepdims=True) for k in range(MOE_K)]
    dest_ref[...] = jnp.concatenate(base, axis=0).astype(I32) + rank_ref[...]


def _dest(idx_t, rank_t, poffs, tp):
    K, T = idx_t.shape
    po = poffs.astype(F32).reshape(MOE_E, 1)
    spec = pl.BlockSpec((K, tp), lambda i: (0, i))
    return pl.pallas_call(
        _dest_kernel, grid=(T // tp,),
        in_specs=[spec, spec, pl.BlockSpec((MOE_E, 1), lambda i: (0, 0))],
        out_specs=spec, out_shape=jax.ShapeDtypeStruct((K, T), I32),
        compiler_params=_params(("parallel",)), name="dest")(idx_t, rank_t, po)


def _pack_bf16_pairs(v):
    m = v.shape[1] // 2
    bits = lax.bitcast_convert_type(v.astype(BF16).astype(F32), jnp.uint32)
    return lax.bitcast_convert_type((bits[:, :m] >> 16) | (bits[:, m:] & jnp.uint32(0xFFFF0000)), I32)


def _unpack_bf16_pairs(w):
    w = lax.bitcast_convert_type(w, jnp.uint32)
    lo = lax.bitcast_convert_type(w << 16, F32)
    hi = lax.bitcast_convert_type(w & jnp.uint32(0xFFFF0000), F32)
    return lo, hi


def _row_scatter(rows, dest_t, n_rows):
    T, d = rows.shape
    K = dest_t.shape[0]
    nw = SC_CORES * SC_SUBCORES
    per_w = T // nw
    ch = SC_GATHER_ROWS
    mesh = plsc.VectorSubcoreMesh(core_axis_name="c", subcore_axis_name="s")

    @functools.partial(
        pl.kernel, mesh=mesh, out_type=jax.ShapeDtypeStruct((n_rows, d), I32),
        scratch_types=[pltpu.VMEM((K, ch), I32), pltpu.VMEM((ch, d), I32), pltpu.SemaphoreType.DMA],
        name="row_scatter")
    def scatter(rows_hbm, dest_hbm, out_hbm, idx_v, rows_v, sem):
        wid = lax.axis_index("s") * SC_CORES + lax.axis_index("c")
        base = wid * per_w

        @pl.loop(0, per_w // ch)
        def _(i):
            off = pl.multiple_of(base + i * ch, ch)
            pltpu.sync_copy(rows_hbm.at[pl.ds(off, ch)], rows_v)
            pltpu.sync_copy(dest_hbm.at[:, pl.ds(off, ch)], idx_v)
            copies = [pltpu.async_copy(rows_v, out_hbm.at[idx_v.at[k]], sem) for k in range(K)]
            for cp in copies:
                cp.wait()

    return scatter(rows, dest_t)


RING_AHEAD = 3
RING_OUT = 3


def _expert_kernel(po_ref, cnt_ref, tot_ref, xs_hbm, w1_ref, w3_ref, w2_ref, ys_hbm,
                   xbuf, ybuf, w1b, w3b, w2b, insem, outsem):
    e = pl.program_id(0)
    n = cnt_ref[e]
    rb = xbuf.shape[1]
    ns = xbuf.shape[0]
    hw = D_MODEL // 2
    nblk = jnp.maximum((n + rb - 1) // rb, 1)
    g0 = po_ref[e] // rb
    total = tot_ref[0]
    w1b[...] = w1_ref[...].astype(BF16)
    w3b[...] = w3_ref[...].astype(BF16)
    w2b[...] = w2_ref[...].astype(BF16)

    def in_copy(g, slot):
        return pltpu.make_async_copy(xs_hbm.at[pl.ds(pl.multiple_of(g * rb, rb), rb)], xbuf.at[slot], insem.at[slot])

    def out_copy(g, slot):
        return pltpu.make_async_copy(ybuf.at[slot], ys_hbm.at[pl.ds(pl.multiple_of(g * rb, rb), rb)],
                                     outsem.at[slot])

    @pl.when(e == 0)
    def _():
        for d in range(RING_AHEAD):
            @pl.when(d < total)
            def _():
                in_copy(d, d).start()

    def body(j, c):
        g = g0 + j
        slot = g % ns
        oslot = g % RING_OUT

        @pl.when(g + RING_AHEAD < total)
        def _():
            in_copy(g + RING_AHEAD, (g + RING_AHEAD) % ns).start()

        in_copy(0, slot).wait()

        @pl.when(g >= RING_OUT)
        def _():
            out_copy(0, oslot).wait()

        words = xbuf[slot]
        row = j * rb + lax.broadcasted_iota(I32, words.shape, 0)
        lo, hi = _unpack_bf16_pairs(jnp.where(row < n, words, 0))
        lo, hi = lo.astype(BF16), hi.astype(BF16)
        a = (jnp.dot(lo, w1b[0:hw, :], preferred_element_type=F32) +
             jnp.dot(hi, w1b[hw:, :], preferred_element_type=F32))
        u = (jnp.dot(lo, w3b[0:hw, :], preferred_element_type=F32) +
             jnp.dot(hi, w3b[hw:, :], preferred_element_type=F32))
        h = (a * jax.nn.sigmoid(a) * u).astype(BF16)
        ybuf[oslot] = _pack_bf16_pairs(jnp.dot(h, w2b[...], preferred_element_type=F32))
        out_copy(g, oslot).start()
        return c

    lax.fori_loop(0, nblk, body, 0)

    @pl.when(e + 1 == pl.num_programs(0))
    def _():
        for i in range(RING_OUT):
            @pl.when(total > i)
            def _():
                out_copy(0, (total - 1 - i) % RING_OUT).wait()


def _experts(poffs, counts, total_blocks, xs, w1, w3, w2):
    n_rows, wp = xs.shape
    rb = ROW_BLOCK
    wspec = lambda shape: pl.BlockSpec((None,) + shape, lambda e, po, cn, tb: (e, 0, 0))
    gs = pltpu.PrefetchScalarGridSpec(
        num_scalar_prefetch=3, grid=(MOE_E,),
        in_specs=[pl.BlockSpec(memory_space=pl.ANY), wspec((D_MODEL, MOE_FF)), wspec((D_MODEL, MOE_FF)),
                  wspec((MOE_FF, D_MODEL))],
        out_specs=pl.BlockSpec(memory_space=pl.ANY),
        scratch_shapes=[pltpu.VMEM((RING_AHEAD + 1, rb, wp), I32), pltpu.VMEM((RING_OUT, rb, wp), I32),
                        pltpu.VMEM((D_MODEL, MOE_FF), BF16), pltpu.VMEM((D_MODEL, MOE_FF), BF16),
                        pltpu.VMEM((MOE_FF, D_MODEL), BF16),
                        pltpu.SemaphoreType.DMA((RING_AHEAD + 1,)), pltpu.SemaphoreType.DMA((RING_OUT,))])
    return pl.pallas_call(
        _expert_kernel, grid_spec=gs, out_shape=jax.ShapeDtypeStruct((n_rows, wp), I32),
        compiler_params=_params(("arbitrary",)), name="experts")(poffs, counts, total_blocks, xs, w1, w3, w2)


def _row_gather(table, idx):
    n, d = idx.shape[0], table.shape[1]
    nw = SC_CORES * SC_SUBCORES
    per_w = n // nw
    ch = SC_GATHER_ROWS // 2
    mesh = plsc.VectorSubcoreMesh(core_axis_name="c", subcore_axis_name="s")

    @functools.partial(
        pl.kernel, mesh=mesh, out_type=jax.ShapeDtypeStruct((n, d), I32),
        scratch_types=[pltpu.VMEM((ch,), I32), pltpu.VMEM((ch,), I32),
                       pltpu.VMEM((ch, d), I32), pltpu.VMEM((ch, d), I32),
                       pltpu.SemaphoreType.DMA, pltpu.SemaphoreType.DMA,
                       pltpu.SemaphoreType.DMA, pltpu.SemaphoreType.DMA],
        name="row_gather")
    def gather(table_hbm, idx_hbm, out_hbm, idx0, idx1, rows0, rows1, g0, g1, w0, w1):
        wid = lax.axis_index("s") * SC_CORES + lax.axis_index("c")
        base = wid * per_w

        @pl.loop(0, per_w // (2 * ch))
        def _(i):
            off0 = pl.multiple_of(base + 2 * i * ch, ch)
            off1 = pl.multiple_of(off0 + ch, ch)
            pltpu.sync_copy(idx_hbm.at[pl.ds(off0, ch)], idx0)
            c0 = pltpu.async_copy(table_hbm.at[idx0], rows0, g0)
            pltpu.sync_copy(idx_hbm.at[pl.ds(off1, ch)], idx1)
            c1 = pltpu.async_copy(table_hbm.at[idx1], rows1, g1)
            c0.wait()
            o0 = pltpu.async_copy(rows0, out_hbm.at[pl.ds(off0, ch)], w0)
            c1.wait()
            o1 = pltpu.async_copy(rows1, out_hbm.at[pl.ds(off1, ch)], w1)
            o0.wait()
            o1.wait()

    return gather(table, idx)


def _combine_kernel(w_ref, x_ref, ysg_ref, s1_ref, s3_ref, s2_ref, g_ref, b_ref, *rest):
    o_ref = rest[-1]
    x2 = x_ref[...]
    xb = x2.astype(BF16)
    a = jnp.dot(xb, s1_ref[...], preferred_element_type=F32)
    c = jnp.dot(xb, s3_ref[...], preferred_element_type=F32)
    shared = jnp.dot((a * jax.nn.sigmoid(a) * c).astype(BF16), s2_ref[...], preferred_element_type=F32)
    w = w_ref[...]
    hw = ysg_ref.shape[2]
    y_lo = jnp.zeros((x2.shape[0], hw), F32)
    y_hi = jnp.zeros((x2.shape[0], hw), F32)
    for k in range(MOE_K):
        lo, hi = _unpack_bf16_pairs(ysg_ref[k])
        y_lo = y_lo + w[:, k:k + 1] * lo
        y_hi = y_hi + w[:, k:k + 1] * hi
    y = shared + jnp.concatenate([y_lo, y_hi], axis=1)
    o_ref[...] = _ln_rows(ALPHA * x2 + y, g_ref[...], b_ref[...])


def _combine(w_nat, x2, ysg, prev, first_tile, sw1, sw3, sw2, g, b, tc):
    T = x2.shape[0]
    wp = ysg.shape[2]
    full = lambda a: pl.BlockSpec(a.shape, lambda i: (0,) * a.ndim)
    row = lambda wd: pl.BlockSpec((tc, wd), lambda i: (i + first_tile, 0))
    s1, s3, s2 = sw1.astype(BF16), sw3.astype(BF16), sw2.astype(BF16)
    g2, b2 = g.reshape(1, -1), b.reshape(1, -1)
    args = [w_nat, x2, ysg, s1, s3, s2, g2, b2]
    in_specs = [row(MOE_K), row(D_MODEL), pl.BlockSpec((MOE_K, tc, wp), lambda i: (0, i, 0)),
                full(s1), full(s3), full(s2), full(g2), full(b2)]
    aliases = {}
    if prev is not None:
        aliases = {len(args): 0}
        args.append(prev)
        in_specs.append(pl.BlockSpec(memory_space=pl.ANY))
    return pl.pallas_call(
        _combine_kernel, grid=(ysg.shape[1] // tc,), in_specs=in_specs,
        out_specs=row(D_MODEL), out_shape=jax.ShapeDtypeStruct((T, D_MODEL), F32),
        input_output_aliases=aliases,
        compiler_params=_params(("parallel",)), name="combine",
    )(*args)


def _moe(x2, x2p, scores_t, router_bias, w1, w3, w2, sw1, sw3, sw2, g, b, tiles):
    T = x2.shape[0]
    rb = ROW_BLOCK
    idx_t, w_t = _route(scores_t, router_bias, tiles['route'])
    rank_t, counts = _rank(idx_t, tiles['rank'])
    pcounts = jnp.maximum((counts + rb - 1) // rb, 1) * rb
    pends = jnp.cumsum(pcounts)
    poffs = (pends - pcounts).astype(I32)
    n_blocks = -(-T * MOE_K // rb) + MOE_E
    dest_t = _dest(idx_t, rank_t, poffs, tiles['rank'])
    xs = _row_scatter(x2p, dest_t, n_blocks * rb)
    total_blocks = (pends[-1:] // rb).astype(I32)
    ys = _experts(poffs, counts.astype(I32), total_blocks, xs, w1, w3, w2)
    tc = tiles['combine']
    gather_unit = SC_CORES * SC_SUBCORES * SC_GATHER_ROWS
    n_split = next((s for s in (4, 2) if T % (s * tc) == 0 and (T // s * MOE_K) % gather_unit == 0), 1)
    th = T // n_split
    w_nat = w_t.T
    out = None
    for p in range(n_split):
        ysg = _row_gather(ys, dest_t[:, p * th:(p + 1) * th].reshape(-1)).reshape(MOE_K, th, -1)
        out = _combine(w_nat, x2, ysg, out, p * th // tc, sw1, sw3, sw2, g, b, tc)
    return out


def _tiles(B, S):
    T = B * S
    pick = lambda want, n: want if n % want == 0 else n
    return dict(proj=pick(512, T), nsa_q=pick(128, S), nsa_ck=pick(512, S), outproj=pick(1024, T),
                mlstm_nb=2 if B % 2 == 0 else 1, xattn=pick(1024, S), route=pick(1024, T), rank=pick(512, T),
                combine=pick(512, T))


def kernel(x, mem, w_in, nsa_pos_k, nsa_cmp_k_w1, nsa_cmp_k_w2, nsa_pos_v, nsa_cmp_v_w1, nsa_cmp_v_w2,
           mlstm_conv_w, mlstm_i_bias, mlstm_f_bias, mlstm_norm_g, w_out, ln1_g, ln1_b,
           xa_wq, xa_wk, xa_wv, xa_wo, ln2_g, ln2_b, router_w, router_bias,
           moe_w1, moe_w3, moe_w2, shared_w1, shared_w3, shared_w2, ln3_g, ln3_b):
    B, S, D = x.shape
    T = B * S
    tl = _tiles(B, S)
    xc = x.reshape(T, D)
    memc = mem.reshape(B * MEM_LEN, D)
    for l in range(w_in.shape[0]):
        (q, cmp, ksel, vsel, kwin, vwin, gates, mq, mk, mv, mo, mif) = _project(xc, _prep_w_in(w_in[l]), tl['proj'])
        kcvc = _compress(cmp, B, S, _prep_cmp(nsa_pos_k[l], nsa_cmp_k_w1[l], nsa_cmp_k_w2[l],
                                              nsa_pos_v[l], nsa_cmp_v_w1[l], nsa_cmp_v_w2[l]))
        y_nsa = _nsa(q, kcvc, ksel, vsel, kwin, vwin, gates, B, S, tl['nsa_q'], tl['nsa_ck'])
        y_ml = _mlstm(mq, mk, mv, mo, mif, mlstm_conv_w[l], mlstm_i_bias[l], mlstm_f_bias[l],
                      mlstm_norm_g[l], B, S, tl['mlstm_nb'])
        x1 = _outproj(y_nsa, y_ml, xc, w_out[l], ln1_g[l], ln1_b[l], tl['outproj'])
        kv = _memkv(memc, xa_wk[l], xa_wv[l])
        x2, x2p, scores_t = _xattn(x1, kv, xa_wq[l], xa_wo[l], ln2_g[l], ln2_b[l], router_w[l], S, tl['xattn'])
        xc = _moe(x2, x2p, scores_t, router_bias[l], moe_w1[l], moe_w3[l], moe_w2[l],
                  shared_w1[l], shared_w3[l], shared_w2[l], ln3_g[l], ln3_b[l], tl)
    return xc.reshape(B, S, D)
```

```python
import functools
import numpy as np
import jax
import jax.numpy as jnp
from jax import lax
from jax.experimental import pallas as pl
from jax.experimental.pallas import tpu as pltpu
from jax.experimental.pallas import tpu_sc as plsc

F32 = jnp.float32
BF16 = jnp.bfloat16
I32 = jnp.int32

D_MODEL = 1024
MEM_LEN = 256
NSA_HEADS = 8
NSA_GROUPS = 2
NSA_HPG = 4
NSA_DK = 64
NSA_CMP_LEN = 32
NSA_CMP_STRIDE = 16
NSA_SEL_BLOCK = 64
NSA_SEL_TOPN = 8
NSA_WINDOW = 512
ML_HEADS = 4
ML_DH = 128
ML_CHUNK = 64
ML_CONV = 4
XA_HEADS = 4
XA_DH = 256
MOE_E = 256
MOE_K = 8
MOE_GROUPS = 8
MOE_TOPK_GROUPS = 4
MOE_FF = 256
MOE_ROUTE_SCALE = 2.5
DEPTH = 1
ALPHA = (2.0 * DEPTH) ** 0.25
LN_EPS = 1e-5
NEG = -1e30
FORCE_BONUS = 1e4

LANES = 128
ROW_BLOCK = 512
VMEM_LIMIT = 56 * 1024 * 1024
SC_CORES = 2
SC_SUBCORES = 16
SC_GATHER_ROWS = 128

_DN_T = (((1,), (1,)), ((), ()))
_DN_TA = (((0,), (0,)), ((), ()))


def _params(sem):
    return pltpu.CompilerParams(dimension_semantics=sem, vmem_limit_bytes=VMEM_LIMIT)


def _ln_rows(v, g, b):
    mu = jnp.mean(v, axis=-1, keepdims=True)
    d = v - mu
    var = jnp.mean(d * d, axis=-1, keepdims=True)
    return d * lax.rsqrt(var + LN_EPS) * g + b


_SEGS = (('q', 512, BF16), ('cmp', 256, F32), ('ksel', 128, BF16), ('vsel', 128, BF16),
         ('kwin', 128, BF16), ('vwin', 128, BF16), ('gates', 128, F32), ('mq', 512, BF16),
         ('mk', 512, BF16), ('mv', 512, BF16), ('mo', 512, BF16), ('mif', 128, F32))


def _proj_kernel(x_ref, w_ref, *out_refs):
    xb = x_ref[...].astype(BF16)
    off = 0
    for o_ref, (_, wd, _) in zip(out_refs, _SEGS):
        o_ref[...] = jnp.dot(xb, w_ref[:, off:off + wd], preferred_element_type=F32).astype(o_ref.dtype)
        off += wd


def _prep_w_in(w):
    sizes = (512,) + (128,) * 6 + (24,) + (512,) * 4 + (4, 4)
    cuts = np.cumsum(sizes)[:-1].tolist()
    (wq, kc, vc, ks, vs, kw, vw, wg, mq, mk, mv, mo, mi, mf) = jnp.split(w, cuts, axis=1)
    wq = wq.reshape(D_MODEL, NSA_GROUPS, NSA_HPG, NSA_DK).transpose(0, 2, 1, 3).reshape(D_MODEL, 512)
    pad = lambda a: jnp.pad(a, ((0, 0), (0, LANES - a.shape[1])))
    segs = [wq, kc, vc, ks, vs, kw, vw, pad(wg), mq, mk, mv, mo, pad(jnp.concatenate([mi, mf], axis=1))]
    return jnp.concatenate(segs, axis=1).astype(BF16)


def _project(x2d, w_all, tm):
    T = x2d.shape[0]
    n = w_all.shape[1]
    out_shape = tuple(jax.ShapeDtypeStruct((T, wd), dt) for _, wd, dt in _SEGS)
    out_specs = tuple(pl.BlockSpec((tm, wd), lambda i: (i, 0)) for _, wd, _ in _SEGS)
    return pl.pallas_call(
        _proj_kernel, grid=(T // tm,),
        in_specs=[pl.BlockSpec((tm, D_MODEL), lambda i: (i, 0)),
                  pl.BlockSpec((D_MODEL, n), lambda i: (0, 0))],
        out_specs=out_specs, out_shape=out_shape,
        compiler_params=_params(("parallel",)), name="proj")(x2d, w_all)


def _cmp_kernel(r_ref, pa_ref, pb_ref, wa_ref, wb_ref, w2_ref, o_ref):
    r = r_ref[...]
    a = jnp.dot((r + pa_ref[...]).astype(BF16), wa_ref[...], preferred_element_type=F32)
    b = jnp.dot((r + pb_ref[...]).astype(BF16), wb_ref[...], preferred_element_type=F32)
    nr = r.shape[0]
    hid = a + pltpu.roll(b, nr - 1, 0)
    hid = hid * jax.nn.sigmoid(hid)
    out = jnp.dot(hid.astype(BF16), w2_ref[...], preferred_element_type=F32)
    row = lax.broadcasted_iota(I32, out.shape, 0)
    o_ref[...] = jnp.where(row < nr - 1, out, 0.0).astype(o_ref.dtype)


def _prep_cmp(pos_k, w1_k, w2_k, pos_v, w1_v, w2_v):
    eye = jnp.eye(NSA_GROUPS, dtype=F32)

    def expand_w1(w1, half):
        w = w1.reshape(NSA_CMP_LEN, NSA_DK, NSA_DK)[half * 16:(half + 1) * 16]
        return jnp.einsum('jde,gk->jgdke', w, eye).reshape(16, 128, 128)

    def both(fk, fv):
        z = jnp.zeros_like(fk)
        top = jnp.concatenate([fk, z], axis=-1)
        bot = jnp.concatenate([z, fv], axis=-1)
        return jnp.concatenate([top, bot], axis=-2)

    wa = both(expand_w1(w1_k, 0), expand_w1(w1_v, 0)).reshape(16 * 256, 256).astype(BF16)
    wb = both(expand_w1(w1_k, 1), expand_w1(w1_v, 1)).reshape(16 * 256, 256).astype(BF16)
    w2 = both(jnp.kron(eye, w2_k), jnp.kron(eye, w2_v)).astype(BF16)

    def pos_row(half):
        pk = jnp.tile(pos_k[half * 16:(half + 1) * 16], (1, NSA_GROUPS))
        pv = jnp.tile(pos_v[half * 16:(half + 1) * 16], (1, NSA_GROUPS))
        return jnp.concatenate([pk, pv], axis=1).reshape(1, 16 * 256)

    return pos_row(0), pos_row(1), wa, wb, w2


def _compress(cmp2d, B, S, prep):
    pa, pb, wa, wb, w2 = prep
    nr = S // NSA_CMP_STRIDE
    r = cmp2d.reshape(B, nr, NSA_CMP_STRIDE * 256)
    full = lambda a: pl.BlockSpec(a.shape, lambda b: (0,) * a.ndim)
    return pl.pallas_call(
        _cmp_kernel, grid=(B,),
        in_specs=[pl.BlockSpec((None, nr, NSA_CMP_STRIDE * 256), lambda b: (b, 0, 0)),
                  full(pa), full(pb), full(wa), full(wb), full(w2)],
        out_specs=pl.BlockSpec((None, nr, 256), lambda b: (b, 0, 0)),
        out_shape=jax.ShapeDtypeStruct((B, nr, 256), BF16),
        compiler_params=_params(("parallel",)), name="cmp")(r, pa, pb, wa, wb, w2)


def _nsa_consts(S):
    n_cmp = (S - NSA_CMP_LEN) // NSA_CMP_STRIDE + 1
    n_sel = S // NSA_SEL_BLOCK
    cs = np.arange(n_cmp) * NSA_CMP_STRIDE
    ss = np.arange(n_sel) * NSA_SEL_BLOCK
    ov = ((cs[:, None] < ss[None, :] + NSA_SEL_BLOCK) & (cs[:, None] + NSA_CMP_LEN > ss[None, :]))
    ovt = np.zeros((LANES, S // NSA_CMP_STRIDE), np.float32)
    ovt[:n_sel, :n_cmp] = ov.T
    e = np.zeros((LANES, S), np.float32)
    e[np.arange(S) // NSA_SEL_BLOCK, np.arange(S)] = 1.0
    return jnp.asarray(ovt, BF16), jnp.asarray(e, BF16)


def _nsa_kernel(q_ref, kcvc_ref, ksel_ref, vsel_ref, kwin_ref, vwin_ref, gates_ref, ovt_ref, e_ref,
                y_ref, bias_ref, *, tq, ck, n_sel):
    G, H = NSA_GROUPS, NSA_HPG
    GH = G * H
    M = GH * tq
    S = ksel_ref.shape[0]
    W = NSA_WINDOW
    ws = min(W + tq, S)
    t0 = pl.program_id(1) * tq
    gates = jax.nn.sigmoid(gates_ref[...])
    lane = lax.broadcasted_iota(I32, (tq, LANES), 1)
    t_col = t0 + lax.broadcasted_iota(I32, (tq, 1), 0)
    kc = kcvc_ref[:, 0:LANES]
    vc = kcvc_ref[:, LANES:2 * LANES]
    nc = kc.shape[0]
    qs = []
    for g in range(G):
        gmask = (lane // NSA_DK) == g
        for h in range(H):
            qh = q_ref[:, h * LANES:(h + 1) * LANES] * (NSA_DK ** -0.5)
            qs.append(jnp.where(gmask, qh, jnp.zeros_like(qh)))
    Q = jnp.concatenate(qs, axis=0).astype(BF16)

    s = lax.dot_general(Q, kc, _DN_T, preferred_element_type=F32)
    c_idx = lax.broadcasted_iota(I32, (tq, nc), 1)
    cmask = (c_idx * NSA_CMP_STRIDE + NSA_CMP_LEN - 1) <= t_col
    s3 = jnp.where(cmask[None], s.reshape(GH, tq, nc), NEG)
    p = jnp.exp(s3 - jnp.max(s3, axis=-1, keepdims=True))
    p = p / jnp.sum(p, axis=-1, keepdims=True)
    p = jnp.where(cmask[None], p, 0.0)
    o_cmp = jnp.dot(p.reshape(M, nc).astype(BF16), vc, preferred_element_type=F32).reshape(GH, tq, LANES)

    for g in range(G):
        psum = jnp.sum(p[g * H:(g + 1) * H], axis=0)
        hi = psum.astype(BF16)
        lo = (psum - hi.astype(F32)).astype(BF16)
        ovt = ovt_ref[...]
        pslt = (lax.dot_general(ovt, hi, _DN_T, preferred_element_type=F32) +
                lax.dot_general(ovt, lo, _DN_T, preferred_element_type=F32))
        imp_p = pslt[0:n_sel, :]
        n_i = lax.broadcasted_iota(I32, (n_sel, tq), 0)
        cur = (t0 + lax.broadcasted_iota(I32, (n_sel, tq), 1)) // NSA_SEL_BLOCK
        forced = (n_i == 0) | (n_i == cur) | (n_i == cur - 1)
        imp = jnp.where(n_i <= cur, imp_p + jnp.where(forced, FORCE_BONUS, 0.0), NEG)
        cnt = jnp.zeros((n_sel, tq), F32)
        for m in range(n_sel):
            row = imp[m:m + 1, :]
            beats = (row > imp) | ((row == imp) & (n_i > m))
            cnt = cnt + jnp.where(beats, 1.0, 0.0)
        selt = jnp.where(cnt < float(min(NSA_SEL_TOPN, n_sel)), 1.0, 0.0)
        selt = jnp.concatenate([selt, jnp.zeros((LANES - n_sel, tq), F32)], axis=0)
        sel = selt.T.astype(BF16)
        maskf = jnp.dot(sel, e_ref[...], preferred_element_type=F32)
        kpos = lax.broadcasted_iota(I32, (tq, S), 1)
        bias_ref[g] = jnp.where((maskf > 0.5) & (kpos <= t_col), 0.0, NEG).astype(BF16)

    vlane = lax.broadcasted_iota(I32, (1, LANES), 1) // NSA_DK

    def pv_with_sums(pb, v):
        outs = []
        for g in range(G):
            vg = jnp.where(vlane == g, v, jnp.ones_like(v))
            outs.append(jnp.dot(pb[g * H * tq:(g + 1) * H * tq], vg, preferred_element_type=F32))
        return jnp.concatenate(outs, axis=0)

    def normalise(acc):
        outs = []
        for r in range(GH):
            c = NSA_DK * (1 - r // H)
            outs.append(acc[r] / acc[r][:, c:c + 1])
        return outs

    def sel_body(j, carry):
        m_i, acc = carry
        ks = pl.multiple_of(j * ck, ck)
        k = ksel_ref[pl.ds(ks, ck), :]
        v = vsel_ref[pl.ds(ks, ck), :]
        sj = lax.dot_general(Q, k, _DN_T, preferred_element_type=F32)
        sb = (sj.astype(BF16).reshape(G, H, tq, ck) + bias_ref[:, :, pl.ds(ks, ck)][:, None]).reshape(GH, tq, ck)
        m_new = jnp.maximum(m_i, jnp.max(sb, axis=-1, keepdims=True).astype(F32))
        a = jnp.exp(m_i - m_new)
        pj = jnp.exp(sb - m_new.astype(BF16)).reshape(M, ck)
        return m_new, a * acc + pv_with_sums(pj, v).reshape(GH, tq, LANES)

    init = (jnp.full((GH, tq, 1), NEG, F32), jnp.zeros((GH, tq, LANES), F32))
    _, acc = lax.fori_loop(0, (t0 + tq + ck - 1) // ck, sel_body, init)
    o_sel = normalise(acc)

    kst = pl.multiple_of(jnp.clip(t0 - W, 0, S - ws), LANES)
    kwn = kwin_ref[pl.ds(kst, ws), :]
    vwn = vwin_ref[pl.ds(kst, ws), :]
    sw = lax.dot_general(Q, kwn, _DN_T, preferred_element_type=F32)
    wpos = kst + lax.broadcasted_iota(I32, (tq, ws), 1)
    wmask = (wpos <= t_col) & (wpos > t_col - W)
    sw3 = jnp.where(wmask[None], sw.reshape(GH, tq, ws), NEG).astype(BF16)
    pw = jnp.exp(sw3 - jnp.max(sw3, axis=-1, keepdims=True)).reshape(M, ws)
    o_win = normalise(pv_with_sums(pw, vwn).reshape(GH, tq, LANES))

    g0mask = lane < NSA_DK
    for h in range(H):
        o_g = []
        for g in range(G):
            r = g * H + h
            c0 = r * 3
            o_g.append(gates[:, c0:c0 + 1] * o_cmp[r] + gates[:, c0 + 1:c0 + 2] * o_sel[r] +
                       gates[:, c0 + 2:c0 + 3] * o_win[r])
        y_ref[:, h * LANES:(h + 1) * LANES] = jnp.where(g0mask, o_g[0], o_g[1]).astype(y_ref.dtype)


def _nsa(q, kcvc, ksel, vsel, kwin, vwin, gates, B, S, tq, ck):
    T = B * S
    nq = S // tq
    ovt, e = _nsa_consts(S)
    seq = lambda a: a.reshape(B, S, LANES)
    kv_spec = pl.BlockSpec((None, S, LANES), lambda b, i: (b, 0, 0))
    kern = functools.partial(_nsa_kernel, tq=tq, ck=ck, n_sel=S // NSA_SEL_BLOCK)
    return pl.pallas_call(
        kern, grid=(B, nq),
        in_specs=[pl.BlockSpec((tq, 512), lambda b, i: (b * nq + i, 0)),
                  pl.BlockSpec((None,) + kcvc.shape[1:], lambda b, i: (b, 0, 0)),
                  kv_spec, kv_spec, kv_spec, kv_spec,
                  pl.BlockSpec((tq, LANES), lambda b, i: (b * nq + i, 0)),
                  pl.BlockSpec(ovt.shape, lambda b, i: (0, 0)),
                  pl.BlockSpec(e.shape, lambda b, i: (0, 0))],
        out_specs=pl.BlockSpec((tq, 512), lambda b, i: (b * nq + i, 0)),
        out_shape=jax.ShapeDtypeStruct((T, 512), BF16),
        scratch_shapes=[pltpu.VMEM((NSA_GROUPS, tq, S), BF16)],
        compiler_params=_params(("parallel", "parallel")), name="nsa",
    )(q, kcvc, seq(ksel), seq(vsel), seq(kwin), seq(vwin), gates, ovt, e)


def _mlstm_kernel(q_ref, k_ref, v_ref, o_ref, gn_ref, gt_ref, cw_ref, bn_ref, bt_ref, ng_ref, tri_ref,
                  y_ref, c_scr, n_scr):
    H, dh, L = ML_HEADS, ML_DH, ML_CHUNK
    nb, S = q_ref.shape[0], q_ref.shape[1]
    nchunk = S // L
    c_scr[...] = jnp.zeros_like(c_scr)
    n_scr[...] = jnp.zeros_like(n_scr)
    row = lax.broadcasted_iota(I32, (L, H * dh), 0)
    li = lax.broadcasted_iota(I32, (L, L), 0)
    mi = lax.broadcasted_iota(I32, (L, L), 1)
    causal = mi <= li
    tril = tri_ref[0]
    triu = tri_ref[1]
    hp = lax.Precision.HIGHEST

    def conv_silu(ref, bi, c, wofs):
        r0 = pl.multiple_of(c * L, L)
        rp = pl.multiple_of(jnp.maximum(c - 1, 0) * L, L)
        cur = ref[bi, pl.ds(r0, L), :].astype(F32)
        prev = ref[bi, pl.ds(rp, L), :].astype(F32) * jnp.where(c > 0, 1.0, 0.0)
        acc = cur * cw_ref[ML_CONV - 1:ML_CONV, wofs:wofs + H * dh]
        for j in range(1, ML_CONV):
            sh = jnp.where(row < j, pltpu.roll(prev, j, 0), pltpu.roll(cur, j, 0))
            acc = acc + sh * cw_ref[ML_CONV - 1 - j:ML_CONV - j, wofs:wofs + H * dh]
        return acc * jax.nn.sigmoid(acc)

    def body(c, m_state):
        r0 = pl.multiple_of(c * L, L)
        new_m = []
        for bi in range(nb):
            qa = conv_silu(q_ref, bi, c, 0) * (dh ** -0.5)
            ka = conv_silu(k_ref, bi, c, H * dh)
            va = v_ref[bi, pl.ds(r0, L), :]
            oa = o_ref[bi, pl.ds(r0, L), :].astype(F32)
            gn = gn_ref[bi, pl.ds(r0, L), :] + bn_ref[...]
            gt = gt_ref[bi, :, c, :] + bt_ref[...]
            lf_n = jax.nn.log_sigmoid(gn)
            lf_t = jax.nn.log_sigmoid(gt)
            b_n = jnp.dot(tril, lf_n, precision=hp, preferred_element_type=F32)
            b_t = jnp.dot(lf_t, triu, precision=hp, preferred_element_type=F32)
            for h in range(H):
                st = bi * H + h
                q = qa[:, h * dh:(h + 1) * dh]
                k = ka[:, h * dh:(h + 1) * dh]
                v = va[:, h * dh:(h + 1) * dh]
                m_old = m_state[st]
                b_col = b_n[:, H + h:H + h + 1]
                i_col = gn[:, h:h + 1]
                b_row = b_t[H + h:H + h + 1, :]
                i_row = gt[h:h + 1, :]
                g_tot = b_t[H + h:H + h + 1, L - 1:L]
                d_log = jnp.where(causal, b_col - b_row + i_row, NEG)
                inter = b_col + m_old
                m_q = jnp.maximum(inter, jnp.max(d_log, axis=-1, keepdims=True))
                w_intra = jnp.exp(d_log - m_q)
                w_inter = jnp.exp(inter - m_q)
                qb = q.astype(BF16)
                s = lax.dot_general(qb, k.astype(BF16), _DN_T, preferred_element_type=F32) * w_intra
                cst = c_scr[st]
                nst = n_scr[st]
                num = (w_inter * jnp.dot(qb, cst.astype(BF16), preferred_element_type=F32) +
                       jnp.dot(s.astype(BF16), v, preferred_element_type=F32))
                den = w_inter * jnp.sum(q * nst, axis=-1, keepdims=True) + jnp.sum(s, axis=-1, keepdims=True)
                hv = num / jnp.maximum(jnp.abs(den), jnp.exp(-m_q))
                log_k = g_tot - b_col + i_col
                m_new = jnp.maximum(g_tot + m_old, jnp.max(log_k, axis=0, keepdims=True))
                wk = jnp.exp(log_k - m_new)
                decay = jnp.exp(g_tot + m_old - m_new)
                kw = k * wk
                c_scr[st] = decay * cst + lax.dot_general(kw.astype(BF16), v, _DN_TA, preferred_element_type=F32)
                n_scr[st] = decay * nst + jnp.sum(kw, axis=0, keepdims=True)
                new_m.append(m_new)
                mu = jnp.mean(hv, axis=-1, keepdims=True)
                dv = hv - mu
                var = jnp.mean(dv * dv, axis=-1, keepdims=True)
                hn = dv * lax.rsqrt(var + LN_EPS) * ng_ref[:, h * dh:(h + 1) * dh]
                og = jax.nn.sigmoid(oa[:, h * dh:(h + 1) * dh])
                y_ref[bi, pl.ds(r0, L), h * dh:(h + 1) * dh] = (og * hn).astype(y_ref.dtype)
        return tuple(new_m)

    lax.fori_loop(0, nchunk, body, tuple(jnp.zeros((1, 1), F32) for _ in range(nb * H)))


def _mlstm(mq, mk, mv, mo, mif, conv_w, i_bias, f_bias, norm_g, B, S, nb):
    T = B * S
    H, dh, L = ML_HEADS, ML_DH, ML_CHUNK
    W = H * dh
    gt = mif[:, :2 * H].reshape(B, S, 2 * H).transpose(0, 2, 1).reshape(B, 2 * H, S // L, L)
    cw = conv_w.reshape(ML_CONV, 2 * W)
    bias = jnp.concatenate([i_bias, f_bias])
    bn = jnp.pad(bias, (0, LANES - 2 * H)).reshape(1, LANES)
    bt = bias.reshape(2 * H, 1)
    ng = norm_g.reshape(1, W)
    tri = jnp.stack([jnp.tril(jnp.ones((L, L), F32)), jnp.triu(jnp.ones((L, L), F32))])
    seq = lambda a: a.reshape(B, S, a.shape[1])
    rows = lambda w: pl.BlockSpec((nb, S, w), lambda b: (b, 0, 0))
    full = lambda a: pl.BlockSpec(a.shape, lambda b: (0,) * a.ndim)
    y = pl.pallas_call(
        _mlstm_kernel, grid=(B // nb,),
        in_specs=[rows(W), rows(W), rows(W), rows(W), rows(LANES),
                  pl.BlockSpec((nb, 2 * H, S // L, L), lambda b: (b, 0, 0, 0)),
                  full(cw), full(bn), full(bt), full(ng), full(tri)],
        out_specs=rows(W),
        out_shape=jax.ShapeDtypeStruct((B, S, W), BF16),
        scratch_shapes=[pltpu.VMEM((nb * H, dh, dh), F32), pltpu.VMEM((nb * H, 1, dh), F32)],
        compiler_params=_params(("parallel",)), name="mlstm",
    )(seq(mq), seq(mk), seq(mv), seq(mo), seq(mif), gt, cw, bn, bt, ng, tri)
    return y.reshape(T, W)


def _outproj_kernel(yn_ref, ym_ref, x_ref, w_ref, g_ref, b_ref, o_ref):
    mix = (jnp.dot(yn_ref[...], w_ref[0:512, :], preferred_element_type=F32) +
           jnp.dot(ym_ref[...], w_ref[512:1024, :], preferred_element_type=F32))
    o_ref[...] = _ln_rows(ALPHA * x_ref[...] + mix, g_ref[...], b_ref[...])


def _outproj(y_nsa, y_ml, x2d, w_out, g, b, tm):
    T = x2d.shape[0]
    wn = w_out[:512].reshape(NSA_GROUPS, NSA_HPG, NSA_DK, D_MODEL).transpose(1, 0, 2, 3).reshape(512, D_MODEL)
    w = jnp.concatenate([wn, w_out[512:]], axis=0).astype(BF16)
    row = lambda wd: pl.BlockSpec((tm, wd), lambda i: (i, 0))
    full = lambda a: pl.BlockSpec(a.shape, lambda i: (0,) * a.ndim)
    g2, b2 = g.reshape(1, -1), b.reshape(1, -1)
    return pl.pallas_call(
        _outproj_kernel, grid=(T // tm,),
        in_specs=[row(512), row(512), row(D_MODEL), full(w), full(g2), full(b2)],
        out_specs=row(D_MODEL), out_shape=jax.ShapeDtypeStruct((T, D_MODEL), F32),
        compiler_params=_params(("parallel",)), name="outproj")(y_nsa, y_ml, x2d, w, g2, b2)


def _memkv_kernel(m_ref, w_ref, o_ref):
    o_ref[...] = jnp.dot(m_ref[...].astype(BF16), w_ref[...], preferred_element_type=F32).astype(o_ref.dtype)


def _memkv(mem2d, wk, wv):
    w = jnp.concatenate([wk, wv], axis=1).astype(BF16)
    R = mem2d.shape[0]
    return pl.pallas_call(
        _memkv_kernel, grid=(R // MEM_LEN,),
        in_specs=[pl.BlockSpec((MEM_LEN, D_MODEL), lambda i: (i, 0)),
                  pl.BlockSpec(w.shape, lambda i: (0, 0))],
        out_specs=pl.BlockSpec((MEM_LEN, 2 * D_MODEL), lambda i: (i, 0)),
        out_shape=jax.ShapeDtypeStruct((R, 2 * D_MODEL), BF16),
        compiler_params=_params(("parallel",)), name="memkv")(mem2d, w)


def _xattn_kernel(x_ref, kv_ref, wq_ref, wo_ref, g_ref, b_ref, rw_ref, x2_ref, x2p_ref, sc_ref):
    x1 = x_ref[...]
    q = jnp.dot(x1.astype(BF16), wq_ref[...], preferred_element_type=F32).astype(BF16)
    outs = []
    for h in range(XA_HEADS):
        qh = q[:, h * XA_DH:(h + 1) * XA_DH]
        kh = kv_ref[:, h * XA_DH:(h + 1) * XA_DH]
        vh = kv_ref[:, D_MODEL + h * XA_DH:D_MODEL + (h + 1) * XA_DH]
        s = (lax.dot_general(qh, kh, _DN_T, preferred_element_type=F32) * (XA_DH ** -0.5)).astype(BF16)
        p = jnp.exp(s - jnp.max(s, axis=-1, keepdims=True))
        inv = 1.0 / jnp.sum(p.astype(F32), axis=-1, keepdims=True)
        outs.append((jnp.dot(p, vh, preferred_element_type=F32) * inv).astype(BF16))
    o = jnp.concatenate(outs, axis=1)
    xa = jnp.dot(o, wo_ref[...], preferred_element_type=F32)
    x2 = _ln_rows(ALPHA * x1 + xa, g_ref[...], b_ref[...])
    x2_ref[...] = x2
    x2p_ref[...] = _pack_bf16_pairs(x2)
    xh = x2.astype(BF16)
    xl = (x2 - xh.astype(F32)).astype(BF16)
    wh = rw_ref[0]
    wl = rw_ref[1]
    logit = (lax.dot_general(wh, xh, _DN_T, preferred_element_type=F32) +
             lax.dot_general(wh, xl, _DN_T, preferred_element_type=F32) +
             lax.dot_general(wl, xh, _DN_T, preferred_element_type=F32))
    sc_ref[...] = jax.nn.sigmoid(logit)


def _xattn(x1, kv, wq, wo, g, b, router_w, S, tq):
    T = x1.shape[0]
    wqb, wob = wq.astype(BF16), wo.astype(BF16)
    rwt = router_w.T
    rh = rwt.astype(BF16)
    rw = jnp.stack([rh, (rwt - rh.astype(F32)).astype(BF16)])
    g2, b2 = g.reshape(1, -1), b.reshape(1, -1)
    full = lambda a: pl.BlockSpec(a.shape, lambda i: (0,) * a.ndim)
    per = S // tq
    return pl.pallas_call(
        _xattn_kernel, grid=(T // tq,),
        in_specs=[pl.BlockSpec((tq, D_MODEL), lambda i: (i, 0)),
                  pl.BlockSpec((MEM_LEN, 2 * D_MODEL), lambda i: (i // per, 0)),
                  full(wqb), full(wob), full(g2), full(b2), full(rw)],
        out_specs=(pl.BlockSpec((tq, D_MODEL), lambda i: (i, 0)),
                   pl.BlockSpec((tq, D_MODEL // 2), lambda i: (i, 0)),
                   pl.BlockSpec((MOE_E, tq), lambda i: (0, i))),
        out_shape=(jax.ShapeDtypeStruct((T, D_MODEL), F32), jax.ShapeDtypeStruct((T, D_MODEL // 2), I32),
                   jax.ShapeDtypeStruct((MOE_E, T), F32)),
        compiler_params=_params(("parallel",)), name="xattn")(x1, kv, wqb, wob, g2, b2, rw)


def _route_kernel(sc_ref, rb_ref, idx_ref, w_ref):
    E, G = MOE_E, MOE_GROUPS
    per = E // G
    scores = sc_ref[...]
    tr = scores.shape[1]
    biased = scores + rb_ref[...]
    g3 = biased.reshape(G, per, tr)
    j3 = lax.broadcasted_iota(I32, (G, per, tr), 1)
    m1 = jnp.max(g3, axis=1, keepdims=True)
    first = jnp.min(jnp.where(g3 == m1, j3, per), axis=1, keepdims=True)
    m2 = jnp.max(jnp.where(j3 == first, -jnp.inf, g3), axis=1, keepdims=True)
    gs = (m1 + m2).reshape(G, tr)
    gi = lax.broadcasted_iota(I32, (G, tr), 0)
    cnt = jnp.zeros((G, tr), F32)
    for m in range(G):
        row = gs[m:m + 1, :]
        cnt = cnt + jnp.where((row > gs) | ((row == gs) & (gi > m)), 1.0, 0.0)
    gmask = cnt < float(MOE_TOPK_GROUPS)
    masked = jnp.where(gmask[:, None, :], g3, NEG).reshape(E, tr)
    ei = lax.broadcasted_iota(I32, (E, tr), 0)
    idxs, ws = [], []
    for _ in range(MOE_K):
        mx = jnp.max(masked, axis=0, keepdims=True)
        ix = jnp.min(jnp.where(masked == mx, ei, E), axis=0, keepdims=True)
        hit = ei == ix
        ws.append(jnp.sum(jnp.where(hit, scores, 0.0), axis=0, keepdims=True))
        idxs.append(ix)
        masked = jnp.where(hit, -jnp.inf, masked)
    w = jnp.concatenate(ws, axis=0)
    idx_ref[...] = jnp.concatenate(idxs, axis=0)
    w_ref[...] = w / jnp.sum(w, axis=0, keepdims=True) * MOE_ROUTE_SCALE


def _route(scores_t, router_bias, tr):
    E, T = scores_t.shape
    rb = router_bias.reshape(E, 1)
    return pl.pallas_call(
        _route_kernel, grid=(T // tr,),
        in_specs=[pl.BlockSpec((E, tr), lambda i: (0, i)), pl.BlockSpec((E, 1), lambda i: (0, 0))],
        out_specs=(pl.BlockSpec((MOE_K, tr), lambda i: (0, i)), pl.BlockSpec((MOE_K, tr), lambda i: (0, i))),
        out_shape=(jax.ShapeDtypeStruct((MOE_K, T), I32), jax.ShapeDtypeStruct((MOE_K, T), F32)),
        compiler_params=_params(("parallel",)), name="route")(scores_t, rb)


def _rank_kernel(idx_ref, u_ref, rank_ref, cnt_ref, carry):
    E = MOE_E

    @pl.when(pl.program_id(0) == 0)
    def _():
        carry[...] = jnp.zeros_like(carry)

    idx = idx_ref[...]
    tp = idx.shape[1]
    ei = lax.broadcasted_iota(I32, (E, tp), 0)
    hits = [ei == idx[k:k + 1, :] for k in range(MOE_K)]
    onehot = jnp.zeros((E, tp), F32)
    for hit in hits:
        onehot = onehot + jnp.where(hit, 1.0, 0.0)
    pos = jnp.dot(onehot.astype(BF16), u_ref[...], preferred_element_type=F32) + carry[...]
    ranks = [jnp.sum(jnp.where(hit, pos, 0.0), axis=0, keepdims=True) for hit in hits]
    rank_ref[...] = jnp.concatenate(ranks, axis=0).astype(I32)
    total = carry[...] + jnp.sum(onehot, axis=1, keepdims=True)
    carry[...] = total
    cnt_ref[...] = jnp.broadcast_to(total, cnt_ref.shape).astype(I32)


def _rank(idx_t, tp):
    K, T = idx_t.shape
    u = jnp.triu(jnp.ones((tp, tp), F32), k=1).astype(BF16)
    rank, cnt = pl.pallas_call(
        _rank_kernel, grid=(T // tp,),
        in_specs=[pl.BlockSpec((K, tp), lambda i: (0, i)), pl.BlockSpec((tp, tp), lambda i: (0, 0))],
        out_specs=(pl.BlockSpec((K, tp), lambda i: (0, i)), pl.BlockSpec((MOE_E, LANES), lambda i: (0, 0))),
        out_shape=(jax.ShapeDtypeStruct((K, T), I32), jax.ShapeDtypeStruct((MOE_E, LANES), I32)),
        scratch_shapes=[pltpu.VMEM((MOE_E, 1), F32)],
        compiler_params=_params(("arbitrary",)), name="rank")(idx_t, u)
    return rank, cnt[:, 0]


def _dest_kernel(idx_ref, rank_ref, po_ref, dest_ref):
    idx = idx_ref[...]
    tp = idx.shape[1]
    ei = lax.broadcasted_iota(I32, (MOE_E, tp), 0)
    po = po_ref[...]
    base = [jnp.sum(jnp.where(ei == idx[k:k + 1, :], po, 0.0), axis=0, keepdims=True) for k in range(MOE_K)]
    dest_ref[...] = jnp.concatenate(base, axis=0).astype(I32) + rank_ref[...]


def _dest(idx_t, rank_t, poffs, tp):
    K, T = idx_t.shape
    po = poffs.astype(F32).reshape(MOE_E, 1)
    spec = pl.BlockSpec((K, tp), lambda i: (0, i))
    return pl.pallas_call(
        _dest_kernel, grid=(T // tp,),
        in_specs=[spec, spec, pl.BlockSpec((MOE_E, 1), lambda i: (0, 0))],
        out_specs=spec, out_shape=jax.ShapeDtypeStruct((K, T), I32),
        compiler_params=_params(("parallel",)), name="dest")(idx_t, rank_t, po)


def _pack_bf16_pairs(v):
    m = v.shape[1] // 2
    bits = lax.bitcast_convert_type(v.astype(BF16).astype(F32), jnp.uint32)
    return lax.bitcast_convert_type((bits[:, :m] >> 16) | (bits[:, m:] & jnp.uint32(0xFFFF0000)), I32)


def _unpack_bf16_pairs(w):
    w = lax.bitcast_convert_type(w, jnp.uint32)
    lo = lax.bitcast_convert_type(w << 16, F32)
    hi = lax.bitcast_convert_type(w & jnp.uint32(0xFFFF0000), F32)
    return lo, hi


def _row_scatter(rows, dest_t, n_rows):
    T, d = rows.shape
    K = dest_t.shape[0]
    nw = SC_CORES * SC_SUBCORES
    per_w = T // nw
    ch = SC_GATHER_ROWS
    mesh = plsc.VectorSubcoreMesh(core_axis_name="c", subcore_axis_name="s")

    @functools.partial(
        pl.kernel, mesh=mesh, out_type=jax.ShapeDtypeStruct((n_rows, d), I32),
        scratch_types=[pltpu.VMEM((K, ch), I32), pltpu.VMEM((ch, d), I32), pltpu.SemaphoreType.DMA],
        name="row_scatter")
    def scatter(rows_hbm, dest_hbm, out_hbm, idx_v, rows_v, sem):
        wid = lax.axis_index("s") * SC_CORES + lax.axis_index("c")
        base = wid * per_w

        @pl.loop(0, per_w // ch)
        def _(i):
            off = pl.multiple_of(base + i * ch, ch)
            pltpu.sync_copy(rows_hbm.at[pl.ds(off, ch)], rows_v)
            pltpu.sync_copy(dest_hbm.at[:, pl.ds(off, ch)], idx_v)
            copies = [pltpu.async_copy(rows_v, out_hbm.at[idx_v.at[k]], sem) for k in range(K)]
            for cp in copies:
                cp.wait()

    return scatter(rows, dest_t)


RING_AHEAD = 3
RING_OUT = 3


def _expert_kernel(po_ref, cnt_ref, tot_ref, xs_hbm, w1_ref, w3_ref, w2_ref, ys_hbm,
                   xbuf, ybuf, w1b, w3b, w2b, insem, outsem):
    e = pl.program_id(0)
    n = cnt_ref[e]
    rb = xbuf.shape[1]
    ns = xbuf.shape[0]
    hw = D_MODEL // 2
    nblk = jnp.maximum((n + rb - 1) // rb, 1)
    g0 = po_ref[e] // rb
    total = tot_ref[0]
    w1b[...] = w1_ref[...].astype(BF16)
    w3b[...] = w3_ref[...].astype(BF16)
    w2b[...] = w2_ref[...].astype(BF16)

    def in_copy(g, slot):
        return pltpu.make_async_copy(xs_hbm.at[pl.ds(pl.multiple_of(g * rb, rb), rb)], xbuf.at[slot], insem.at[slot])

    def out_copy(g, slot):
        return pltpu.make_async_copy(ybuf.at[slot], ys_hbm.at[pl.ds(pl.multiple_of(g * rb, rb), rb)],
                                     outsem.at[slot])

    @pl.when(e == 0)
    def _():
        for d in range(RING_AHEAD):
            @pl.when(d < total)
            def _():
                in_copy(d, d).start()

    def body(j, c):
        g = g0 + j
        slot = g % ns
        oslot = g % RING_OUT

        @pl.when(g + RING_AHEAD < total)
        def _():
            in_copy(g + RING_AHEAD, (g + RING_AHEAD) % ns).start()

        in_copy(0, slot).wait()

        @pl.when(g >= RING_OUT)
        def _():
            out_copy(0, oslot).wait()

        words = xbuf[slot]
        row = j * rb + lax.broadcasted_iota(I32, words.shape, 0)
        lo, hi = _unpack_bf16_pairs(jnp.where(row < n, words, 0))
        lo, hi = lo.astype(BF16), hi.astype(BF16)
        a = (jnp.dot(lo, w1b[0:hw, :], preferred_element_type=F32) +
             jnp.dot(hi, w1b[hw:, :], preferred_element_type=F32))
        u = (jnp.dot(lo, w3b[0:hw, :], preferred_element_type=F32) +
             jnp.dot(hi, w3b[hw:, :], preferred_element_type=F32))
        h = (a * jax.nn.sigmoid(a) * u).astype(BF16)
        ybuf[oslot] = _pack_bf16_pairs(jnp.dot(h, w2b[...], preferred_element_type=F32))
        out_copy(g, oslot).start()
        return c

    lax.fori_loop(0, nblk, body, 0)

    @pl.when(e + 1 == pl.num_programs(0))
    def _():
        for i in range(RING_OUT):
            @pl.when(total > i)
            def _():
                out_copy(0, (total - 1 - i) % RING_OUT).wait()


def _experts(poffs, counts, total_blocks, xs, w1, w3, w2):
    n_rows, wp = xs.shape
    rb = ROW_BLOCK
    wspec = lambda shape: pl.BlockSpec((None,) + shape, lambda e, po, cn, tb: (e, 0, 0))
    gs = pltpu.PrefetchScalarGridSpec(
        num_scalar_prefetch=3, grid=(MOE_E,),
        in_specs=[pl.BlockSpec(memory_space=pl.ANY), wspec((D_MODEL, MOE_FF)), wspec((D_MODEL, MOE_FF)),
                  wspec((MOE_FF, D_MODEL))],
        out_specs=pl.BlockSpec(memory_space=pl.ANY),
        scratch_shapes=[pltpu.VMEM((RING_AHEAD + 1, rb, wp), I32), pltpu.VMEM((RING_OUT, rb, wp), I32),
                        pltpu.VMEM((D_MODEL, MOE_FF), BF16), pltpu.VMEM((D_MODEL, MOE_FF), BF16),
                        pltpu.VMEM((MOE_FF, D_MODEL), BF16),
                        pltpu.SemaphoreType.DMA((RING_AHEAD + 1,)), pltpu.SemaphoreType.DMA((RING_OUT,))])
    return pl.pallas_call(
        _expert_kernel, grid_spec=gs, out_shape=jax.ShapeDtypeStruct((n_rows, wp), I32),
        compiler_params=_params(("arbitrary",)), name="experts")(poffs, counts, total_blocks, xs, w1, w3, w2)


def _row_gather(table, idx):
    n, d = idx.shape[0], table.shape[1]
    nw = SC_CORES * SC_SUBCORES
    per_w = n // nw
    ch = SC_GATHER_ROWS // 2
    mesh = plsc.VectorSubcoreMesh(core_axis_name="c", subcore_axis_name="s")

    @functools.partial(
        pl.kernel, mesh=mesh, out_type=jax.ShapeDtypeStruct((n, d), I32),
        scratch_types=[pltpu.VMEM((ch,), I32), pltpu.VMEM((ch,), I32),
                       pltpu.VMEM((ch, d), I32), pltpu.VMEM((ch, d), I32),
                       pltpu.SemaphoreType.DMA, pltpu.SemaphoreType.DMA,
                       pltpu.SemaphoreType.DMA, pltpu.SemaphoreType.DMA],
        name="row_gather")
    def gather(table_hbm, idx_hbm, out_hbm, idx0, idx1, rows0, rows1, g0, g1, w0, w1):
        wid = lax.axis_index("s") * SC_CORES + lax.axis_index("c")
        base = wid * per_w

        @pl.loop(0, per_w // (2 * ch))
        def _(i):
            off0 = pl.multiple_of(base + 2 * i * ch, ch)
            off1 = pl.multiple_of(off0 + ch, ch)
            pltpu.sync_copy(idx_hbm.at[pl.ds(off0, ch)], idx0)
            c0 = pltpu.async_copy(table_hbm.at[idx0], rows0, g0)
            pltpu.sync_copy(idx_hbm.at[pl.ds(off1, ch)], idx1)
            c1 = pltpu.async_copy(table_hbm.at[idx1], rows1, g1)
            c0.wait()
            o0 = pltpu.async_copy(rows0, out_hbm.at[pl.ds(off0, ch)], w0)
            c1.wait()
            o1 = pltpu.async_copy(rows1, out_hbm.at[pl.ds(off1, ch)], w1)
            o0.wait()
            o1.wait()

    return gather(table, idx)


def _combine_kernel(w_ref, x_ref, ysg_ref, s1_ref, s3_ref, s2_ref, g_ref, b_ref, *rest):
    o_ref = rest[-1]
    x2 = x_ref[...]
    xb = x2.astype(BF16)
    a = jnp.dot(xb, s1_ref[...], preferred_element_type=F32)
    c = jnp.dot(xb, s3_ref[...], preferred_element_type=F32)
    shared = jnp.dot((a * jax.nn.sigmoid(a) * c).astype(BF16), s2_ref[...], preferred_element_type=F32)
    w = w_ref[...]
    hw = ysg_ref.shape[2]
    y_lo = jnp.zeros((x2.shape[0], hw), F32)
    y_hi = jnp.zeros((x2.shape[0], hw), F32)
    for k in range(MOE_K):
        lo, hi = _unpack_bf16_pairs(ysg_ref[k])
        y_lo = y_lo + w[:, k:k + 1] * lo
        y_hi = y_hi + w[:, k:k + 1] * hi
    y = shared + jnp.concatenate([y_lo, y_hi], axis=1)
    o_ref[...] = _ln_rows(ALPHA * x2 + y, g_ref[...], b_ref[...])


def _combine(w_nat, x2, ysg, prev, first_tile, sw1, sw3, sw2, g, b, tc):
    T = x2.shape[0]
    wp = ysg.shape[2]
    full = lambda a: pl.BlockSpec(a.shape, lambda i: (0,) * a.ndim)
    row = lambda wd: pl.BlockSpec((tc, wd), lambda i: (i + first_tile, 0))
    s1, s3, s2 = sw1.astype(BF16), sw3.astype(BF16), sw2.astype(BF16)
    g2, b2 = g.reshape(1, -1), b.reshape(1, -1)
    args = [w_nat, x2, ysg, s1, s3, s2, g2, b2]
    in_specs = [row(MOE_K), row(D_MODEL), pl.BlockSpec((MOE_K, tc, wp), lambda i: (0, i, 0)),
                full(s1), full(s3), full(s2), full(g2), full(b2)]
    aliases = {}
    if prev is not None:
        aliases = {len(args): 0}
        args.append(prev)
        in_specs.append(pl.BlockSpec(memory_space=pl.ANY))
    return pl.pallas_call(
        _combine_kernel, grid=(ysg.shape[1] // tc,), in_specs=in_specs,
        out_specs=row(D_MODEL), out_shape=jax.ShapeDtypeStruct((T, D_MODEL), F32),
        input_output_aliases=aliases,
        compiler_params=_params(("parallel",)), name="combine",
    )(*args)


def _moe(x2, x2p, scores_t, router_bias, w1, w3, w2, sw1, sw3, sw2, g, b, tiles):
    T = x2.shape[0]
    rb = ROW_BLOCK
    idx_t, w_t = _route(scores_t, router_bias, tiles['route'])
    rank_t, counts = _rank(idx_t, tiles['rank'])
    pcounts = jnp.maximum((counts + rb - 1) // rb, 1) * rb
    pends = jnp.cumsum(pcounts)
    poffs = (pends - pcounts).astype(I32)
    n_blocks = -(-T * MOE_K // rb) + MOE_E
    dest_t = _dest(idx_t, rank_t, poffs, tiles['rank'])
    xs = _row_scatter(x2p, dest_t, n_blocks * rb)
    total_blocks = (pends[-1:] // rb).astype(I32)
    ys = _experts(poffs, counts.astype(I32), total_blocks, xs, w1, w3, w2)
    tc = tiles['combine']
    gather_unit = SC_CORES * SC_SUBCORES * SC_GATHER_ROWS
    n_split = next((s for s in (4, 2) if T % (s * tc) == 0 and (T // s * MOE_K) % gather_unit == 0), 1)
    th = T // n_split
    w_nat = w_t.T
    out = None
    for p in range(n_split):
        ysg = _row_gather(ys, dest_t[:, p * th:(p + 1) * th].reshape(-1)).reshape(MOE_K, th, -1)
        out = _combine(w_nat, x2, ysg, out, p * th // tc, sw1, sw3, sw2, g, b, tc)
    return out


def _tiles(B, S):
    T = B * S
    pick = lambda want, n: want if n % want == 0 else n
    return dict(proj=pick(512, T), nsa_q=pick(128, S), nsa_ck=pick(512, S), outproj=pick(1024, T),
                mlstm_nb=2 if B % 2 == 0 else 1, xattn=pick(1024, S), route=pick(1024, T), rank=pick(512, T),
                combine=pick(512, T))


def kernel(x, mem, w_in, nsa_pos_k, nsa_cmp_k_w1, nsa_cmp_k_w2, nsa_pos_v, nsa_cmp_v_w1, nsa_cmp_v_w2,
           mlstm_conv_w, mlstm_i_bias, mlstm_f_bias, mlstm_norm_g, w_out, ln1_g, ln1_b,
           xa_wq, xa_wk, xa_wv, xa_wo, ln2_g, ln2_b, router_w, router_bias,
           moe_w1, moe_w3, moe_w2, shared_w1, shared_w3, shared_w2, ln3_g, ln3_b):
    B, S, D = x.shape
    T = B * S
    tl = _tiles(B, S)
    xc = x.reshape(T, D)
    memc = mem.reshape(B * MEM_LEN, D)
    for l in range(w_in.shape[0]):
        (q, cmp, ksel, vsel, kwin, vwin, gates, mq, mk, mv, mo, mif) = _project(xc, _prep_w_in(w_in[l]), tl['proj'])
        kcvc = _compress(cmp, B, S, _prep_cmp(nsa_pos_k[l], nsa_cmp_k_w1[l], nsa_cmp_k_w2[l],
                                              nsa_pos_v[l], nsa_cmp_v_w1[l], nsa_cmp_v_w2[l]))
        y_nsa = _nsa(q, kcvc, ksel, vsel, kwin, vwin, gates, B, S, tl['nsa_q'], tl['nsa_ck'])
        y_ml = _mlstm(mq, mk, mv, mo, mif, mlstm_conv_w[l], mlstm_i_bias[l], mlstm_f_bias[l],
                      mlstm_norm_g[l], B, S, tl['mlstm_nb'])
        x1 = _outproj(y_nsa, y_ml, xc, w_out[l], ln1_g[l], ln1_b[l], tl['outproj'])
        kv = _memkv(memc, xa_wk[l], xa_wv[l])
        x2, x2p, scores_t = _xattn(x1, kv, xa_wq[l], xa_wo[l], ln2_g[l], ln2_b[l], router_w[l], S, tl['xattn'])
        xc = _moe(x2, x2p, scores_t, router_bias[l], moe_w1[l], moe_w3[l], moe_w2[l],
                  shared_w1[l], shared_w3[l], shared_w2[l], ln3_g[l], ln3_b[l], tl)
    return xc.reshape(B, S, D)
```
